```python
import jax, jax.numpy as jnp
from jax import lax
import numpy as np

D_MODEL = 1024
BATCH = 8
SEQ = 2048
DEPTH = 2

RET_HEADS = 4
RET_DK = 128
RET_DV = 128
RET_CHUNK = 128
ROPE_BASE = 10000.0
GDN_HEADS = 4
GDN_DK = 128
GDN_DV = 128
GDN_CHUNK = 64
SHORT_CONV = 4
RET_WIDTH = RET_HEADS * RET_DV
GDN_WIDTH = GDN_HEADS * GDN_DV
MIX0_OUT = RET_WIDTH + GDN_WIDTH
MIX0_SPLITS = (RET_HEADS * RET_DK, RET_HEADS * RET_DK, RET_WIDTH, RET_WIDTH,
               GDN_HEADS * GDN_DK, GDN_HEADS * GDN_DK, GDN_WIDTH, GDN_WIDTH,
               GDN_HEADS, GDN_HEADS)
MIX0_IN = sum(MIX0_SPLITS)
D_RNN = D_MODEL
LRU_BLOCKS = 8
LRU_BLOCK = D_RNN // LRU_BLOCKS
LRU_CONV = 4
LRU_C = 8.0
D_FF = ((8 * D_MODEL // 3 + 127) // 128) * 128
FFN_CONV = 3
EPS = 1e-6
N_EVEN = (DEPTH + 1) // 2
N_ODD = DEPTH // 2

kernel_name = "hybrid_retention_gdn_rglru_convffn"


def _rms(x):
    xf = x.astype(jnp.float32)
    return xf * lax.rsqrt(jnp.mean(xf * xf, axis=-1, keepdims=True) + EPS)


def rms_norm(x, gain):
    return (_rms(x) * gain.astype(jnp.float32)).astype(x.dtype)


def l2norm(x):
    return x * lax.rsqrt(jnp.sum(x * x, axis=-1, keepdims=True) + EPS)


def causal_dwconv(x, w):
    width, ch = w.shape
    return lax.conv_general_dilated(
        x, w[:, None, :].astype(x.dtype), window_strides=(1,),
        padding=[(width - 1, 0)], dimension_numbers=('NWC', 'WIO', 'NWC'),
        feature_group_count=ch)


def rotary(x, pos):
    half = x.shape[-1] // 2
    inv_freq = ROPE_BASE ** (-jnp.arange(half, dtype=jnp.float32) / half)
    ang = pos.astype(jnp.float32)[:, None] * inv_freq[None, :]
    cos = jnp.cos(ang)[None, :, None, :]
    sin = jnp.sin(ang)[None, :, None, :]
    x1, x2 = x[..., :half], x[..., half:]
    return jnp.concatenate([x1 * cos - x2 * sin, x1 * sin + x2 * cos], axis=-1)


def to_chunks(x, c):
    b, t, h, d = x.shape
    return x.reshape(b, t // c, c, h, d).transpose(0, 3, 1, 2, 4)


def from_chunks(x):
    b, h, n, c, d = x.shape
    return x.transpose(0, 2, 3, 1, 4).reshape(b, n * c, h, d)


def retention_chunkwise(q, k, v):
    b, t, h, dk = q.shape
    dv = v.shape[-1]
    c = RET_CHUNK
    log_gamma = jnp.log1p(-jnp.exp2(-5.0 - jnp.arange(h, dtype=jnp.float32)))
    qc, kc, vc = to_chunks(q, c), to_chunks(k * dk ** -0.5, c), to_chunks(v, c)
    idx = jnp.arange(c, dtype=jnp.float32)
    rel = idx[:, None] - idx[None, :]
    causal = rel >= 0
    dmask = jnp.where(causal, jnp.exp(log_gamma[:, None, None] * jnp.where(causal, rel, 0.0)), 0.0)
    scores = jnp.einsum('bhnid,bhnjd->bhnij', qc, kc) * dmask[:, None]
    intra = jnp.einsum('bhnij,bhnjv->bhniv', scores, vc)
    k_tail = kc * jnp.exp(log_gamma[:, None, None] * (c - 1 - idx))[..., None]
    kv = jnp.einsum('bhncd,bhncv->nbhdv', k_tail, vc)
    chunk_decay = jnp.exp(log_gamma * c)[None, :, None, None]

    def step(s, kv_n):
        return s * chunk_decay + kv_n, s

    _, s_prev = lax.scan(step, jnp.zeros((b, h, dk, dv), jnp.float32), kv)
    q_decay = qc * jnp.exp(log_gamma[:, None, None] * (idx + 1.0))[..., None]
    inter = jnp.einsum('bhncd,nbhdv->bhncv', q_decay, s_prev)
    return from_chunks(intra + inter)


def gated_delta_chunkwise(q, k, v, g, beta):
    b, t, h, dk = q.shape
    dv = v.shape[-1]
    c = GDN_CHUNK
    qc = to_chunks(q * dk ** -0.5, c)
    kc = to_chunks(k, c)
    vc = to_chunks(v, c)
    gc = jnp.cumsum(to_chunks(g[..., None], c)[..., 0], axis=-1)
    bc = to_chunks(beta[..., None], c)
    tril = jnp.tril(jnp.ones((c, c), bool))
    strict = jnp.tril(jnp.ones((c, c), bool), -1)
    decay = jnp.exp(jnp.where(tril, gc[..., :, None] - gc[..., None, :], -jnp.inf))
    k_beta = kc * bc
    lmat = jnp.where(strict, jnp.einsum('bhnid,bhnjd->bhnij', k_beta, kc) * decay, 0.0)
    rhs = jnp.concatenate([vc * bc, k_beta * jnp.exp(gc)[..., None]], axis=-1)
    sol = lax.linalg.triangular_solve(lmat + jnp.eye(c, dtype=jnp.float32), rhs,
                                      left_side=True, lower=True)
    u, w = sol[..., :dv], sol[..., dv:]
    attn = jnp.where(tril, jnp.einsum('bhnid,bhnjd->bhnij', qc, kc) * decay, 0.0)
    q_decay = qc * jnp.exp(gc)[..., None]
    g_last = gc[..., -1:]
    k_tail = kc * jnp.exp(g_last - gc)[..., None]
    chunk_decay = jnp.exp(g_last)[..., None]
    xs = tuple(jnp.moveaxis(a, 2, 0) for a in (u, w, attn, q_decay, k_tail, chunk_decay))

    def step(s, inp):
        u_n, w_n, a_n, qd_n, kt_n, cd_n = inp
        v_new = u_n - jnp.einsum('bhcd,bhdv->bhcv', w_n, s)
        o = jnp.einsum('bhcd,bhdv->bhcv', qd_n, s) + jnp.einsum('bhij,bhjv->bhiv', a_n, v_new)
        s = s * cd_n + jnp.einsum('bhcd,bhcv->bhdv', kt_n, v_new)
        return s, o

    _, o = lax.scan(step, jnp.zeros((b, h, dk, dv), jnp.float32), xs)
    return from_chunks(jnp.moveaxis(o, 0, 2))


def retention_deltanet_mixer(hn, pos, w_in, conv_w, a_log, dt_bias, out_gain, w_out):
    b, t, _ = hn.shape
    f32 = jnp.float32
    split_at = np.cumsum(MIX0_SPLITS)[:-1].tolist()
    q_r, k_r, v_r, g_r, q_d, k_d, v_d, g_d, b_d, a_d = jnp.split(hn @ w_in, split_at, axis=-1)
    q_r = rotary(q_r.reshape(b, t, RET_HEADS, RET_DK).astype(f32), pos)
    k_r = rotary(k_r.reshape(b, t, RET_HEADS, RET_DK).astype(f32), pos)
    v_r = v_r.reshape(b, t, RET_HEADS, RET_DV).astype(f32)
    y_r = _rms(retention_chunkwise(q_r, k_r, v_r)).reshape(b, t, RET_WIDTH)
    y_r = y_r * jax.nn.silu(g_r.astype(f32))
    qkv = jax.nn.silu(causal_dwconv(jnp.concatenate([q_d, k_d, v_d], axis=-1), conv_w)).astype(f32)
    q_d, k_d, v_d = jnp.split(qkv, [GDN_HEADS * GDN_DK, 2 * GDN_HEADS * GDN_DK], axis=-1)
    q_d = l2norm(q_d.reshape(b, t, GDN_HEADS, GDN_DK))
    k_d = l2norm(k_d.reshape(b, t, GDN_HEADS, GDN_DK))
    v_d = v_d.reshape(b, t, GDN_HEADS, GDN_DV)
    beta = jax.nn.sigmoid(b_d.astype(f32))
    g = -jnp.exp(a_log.astype(f32)) * jax.nn.softplus(a_d.astype(f32) + dt_bias.astype(f32))
    y_d = gated_delta_chunkwise(q_d, k_d, v_d, g, beta)
    y_d = rms_norm(y_d, out_gain) * jax.nn.silu(g_d.astype(f32).reshape(b, t, GDN_HEADS, GDN_DV))
    y = jnp.concatenate([y_r, y_d.reshape(b, t, GDN_WIDTH)], axis=-1).astype(hn.dtype)
    return y @ w_out


def _linear_combine(e1, e2):
    a1, b1 = e1
    a2, b2 = e2
    return a1 * a2, a2 * b1 + b2


def rglru_mixer(hn, w_in, conv_w, conv_b, w_a, b_a, w_x, b_x, lam, w_out):
    b, t, _ = hn.shape
    f32 = jnp.float32
    gate, xr = jnp.split(hn @ w_in, 2, axis=-1)
    xr = (causal_dwconv(xr, conv_w) + conv_b).astype(f32)
    xb = xr.reshape(b, t, LRU_BLOCKS, LRU_BLOCK)
    r = jax.nn.sigmoid(jnp.einsum('btni,nij->btnj', xb, w_a.astype(f32)).reshape(b, t, D_RNN) + b_a)
    i = jax.nn.sigmoid(jnp.einsum('btni,nij->btnj', xb, w_x.astype(f32)).reshape(b, t, D_RNN) + b_x)
    log_a = -LRU_C * r * jax.nn.softplus(-lam.astype(f32))
    a = jnp.exp(log_a)
    u = jnp.sqrt(-jnp.expm1(2.0 * log_a)) * (i * xr)
    _, hs = lax.associative_scan(_linear_combine, (a, u), axis=1)
    y = jax.nn.gelu(gate.astype(f32)) * hs
    return y.astype(hn.dtype) @ w_out


def conv_ffn(hn, w_up, conv_w, conv_b, w_down):
    up = causal_dwconv(hn @ w_up, conv_w) + conv_b
    gate, val = jnp.split(up, 2, axis=-1)
    return (jax.nn.silu(gate) * val) @ w_down


def _fwd_setup_inputs(seed: int = 0) -> dict:
    key = jax.random.key(seed)
    ks = jax.random.split(key, 24)
    nrm = jax.random.normal
    f32 = jnp.float32
    x = nrm(ks[0], (BATCH, SEQ, D_MODEL), f32)
    norm_mix = 1.0 + 0.02 * nrm(ks[1], (DEPTH, D_MODEL), f32)
    norm_ffn = 1.0 + 0.02 * nrm(ks[2], (DEPTH, D_MODEL), f32)
    ret_gdn_w_in = nrm(ks[3], (N_EVEN, D_MODEL, MIX0_IN), f32) * D_MODEL ** -0.5
    gdn_conv_w = nrm(ks[4], (N_EVEN, SHORT_CONV, 2 * GDN_HEADS * GDN_DK + GDN_WIDTH), f32) * SHORT_CONV ** -0.5
    gdn_a_log = jnp.log(jax.random.uniform(ks[5], (N_EVEN, GDN_HEADS), f32, 1.0, 16.0))
    dt = jnp.exp(jax.random.uniform(ks[6], (N_EVEN, GDN_HEADS), f32, np.log(1e-3), np.log(1e-1)))
    gdn_dt_bias = dt + jnp.log(-jnp.expm1(-dt))
    gdn_out_gain = 1.0 + 0.02 * nrm(ks[7], (N_EVEN, GDN_DV), f32)
    ret_gdn_w_out = nrm(ks[8], (N_EVEN, MIX0_OUT, D_MODEL), f32) * MIX0_OUT ** -0.5
    lru_w_in = nrm(ks[9], (N_ODD, D_MODEL, 2 * D_RNN), f32) * D_MODEL ** -0.5
    lru_conv_w = nrm(ks[10], (N_ODD, LRU_CONV, D_RNN), f32) * LRU_CONV ** -0.5
    lru_conv_b = 0.01 * nrm(ks[11], (N_ODD, D_RNN), f32)
    lru_w_a = nrm(ks[12], (N_ODD, LRU_BLOCKS, LRU_BLOCK, LRU_BLOCK), f32) * LRU_BLOCK ** -0.5
    lru_b_a = 0.01 * nrm(ks[13], (N_ODD, D_RNN), f32)
    lru_w_x = nrm(ks[14], (N_ODD, LRU_BLOCKS, LRU_BLOCK, LRU_BLOCK), f32) * LRU_BLOCK ** -0.5
    lru_b_x = 0.01 * nrm(ks[15], (N_ODD, D_RNN), f32)
    a_c = jax.random.uniform(ks[16], (N_ODD, D_RNN), f32, 0.9, 0.999)
    a0 = a_c ** (1.0 / LRU_C)
    lru_lambda = jnp.log(a0) - jnp.log1p(-a0)
    lru_w_out = nrm(ks[17], (N_ODD, D_RNN, D_MODEL), f32) * D_RNN ** -0.5
    ffn_w_up = nrm(ks[18], (DEPTH, D_MODEL, 2 * D_FF), f32) * D_MODEL ** -0.5
    ffn_conv_w = nrm(ks[19], (DEPTH, FFN_CONV, 2 * D_FF), f32) * FFN_CONV ** -0.5
    ffn_conv_b = 0.01 * nrm(ks[20], (DEPTH, 2 * D_FF), f32)
    ffn_w_down = nrm(ks[21], (DEPTH, D_FF, D_MODEL), f32) * D_FF ** -0.5
    norm_final = 1.0 + 0.02 * nrm(ks[22], (D_MODEL,), f32)
    return {"x": x, "norm_mix": norm_mix, "norm_ffn": norm_ffn,
            "ret_gdn_w_in": ret_gdn_w_in, "gdn_conv_w": gdn_conv_w, "gdn_a_log": gdn_a_log,
            "gdn_dt_bias": gdn_dt_bias, "gdn_out_gain": gdn_out_gain, "ret_gdn_w_out": ret_gdn_w_out,
            "lru_w_in": lru_w_in, "lru_conv_w": lru_conv_w, "lru_conv_b": lru_conv_b,
            "lru_w_a": lru_w_a, "lru_b_a": lru_b_a, "lru_w_x": lru_w_x, "lru_b_x": lru_b_x,
            "lru_lambda": lru_lambda, "lru_w_out": lru_w_out,
            "ffn_w_up": ffn_w_up, "ffn_conv_w": ffn_conv_w, "ffn_conv_b": ffn_conv_b,
            "ffn_w_down": ffn_w_down, "norm_final": norm_final}


def _fwd_reference(x, norm_mix, norm_ffn, ret_gdn_w_in, gdn_conv_w, gdn_a_log, gdn_dt_bias,
              gdn_out_gain, ret_gdn_w_out, lru_w_in, lru_conv_w, lru_conv_b, lru_w_a, lru_b_a,
              lru_w_x, lru_b_x, lru_lambda, lru_w_out, ffn_w_up, ffn_conv_w, ffn_conv_b,
              ffn_w_down, norm_final):
    pos = jnp.arange(x.shape[1], dtype=jnp.int32)
    h = x
    for layer in range(DEPTH):
        hn = rms_norm(h, norm_mix[layer])
        if layer % 2 == 0:
            e = layer // 2
            h = h + retention_deltanet_mixer(hn, pos, ret_gdn_w_in[e], gdn_conv_w[e], gdn_a_log[e],
                                             gdn_dt_bias[e], gdn_out_gain[e], ret_gdn_w_out[e])
        else:
            o = layer // 2
            h = h + rglru_mixer(hn, lru_w_in[o], lru_conv_w[o], lru_conv_b[o], lru_w_a[o], lru_b_a[o],
                                lru_w_x[o], lru_b_x[o], lru_lambda[o], lru_w_out[o])
        h = h + conv_ffn(rms_norm(h, norm_ffn[layer]), ffn_w_up[layer], ffn_conv_w[layer],
                         ffn_conv_b[layer], ffn_w_down[layer])
    return rms_norm(h, norm_final)


import jax as _jax
import jax.numpy as _jnp

TWIN_FORMAT = 'train_step'
FWD_PARAMS = ['x', 'norm_mix', 'norm_ffn', 'ret_gdn_w_in', 'gdn_conv_w', 'gdn_a_log', 'gdn_dt_bias', 'gdn_out_gain', 'ret_gdn_w_out', 'lru_w_in', 'lru_conv_w', 'lru_conv_b', 'lru_w_a', 'lru_b_a', 'lru_w_x', 'lru_b_x', 'lru_lambda', 'lru_w_out', 'ffn_w_up', 'ffn_conv_w', 'ffn_conv_b', 'ffn_w_down', 'norm_final']
TWIN_WEIGHTS = ['norm_mix', 'norm_ffn', 'ret_gdn_w_in', 'gdn_conv_w', 'gdn_a_log', 'gdn_dt_bias', 'gdn_out_gain', 'ret_gdn_w_out', 'lru_w_in', 'lru_conv_w', 'lru_conv_b', 'lru_w_a', 'lru_b_a', 'lru_w_x', 'lru_b_x', 'lru_lambda', 'lru_w_out', 'ffn_w_up', 'ffn_conv_w', 'ffn_conv_b', 'ffn_w_down', 'norm_final']
TWIN_DIFF_INPUT = 'x'
TWIN_INPUTS = ['x', 'norm_mix', 'norm_ffn', 'ret_gdn_w_in', 'gdn_conv_w', 'gdn_a_log', 'gdn_dt_bias', 'gdn_out_gain', 'ret_gdn_w_out', 'lru_w_in', 'lru_conv_w', 'lru_conv_b', 'lru_w_a', 'lru_b_a', 'lru_w_x', 'lru_b_x', 'lru_lambda', 'lru_w_out', 'ffn_w_up', 'ffn_conv_w', 'ffn_conv_b', 'ffn_w_down', 'norm_final', 'loss_target', 'm_norm_mix', 'm_norm_ffn', 'm_ret_gdn_w_in', 'm_gdn_conv_w', 'm_gdn_a_log', 'm_gdn_dt_bias', 'm_gdn_out_gain', 'm_ret_gdn_w_out', 'm_lru_w_in', 'm_lru_conv_w', 'm_lru_conv_b', 'm_lru_w_a', 'm_lru_b_a', 'm_lru_w_x', 'm_lru_b_x', 'm_lru_lambda', 'm_lru_w_out', 'm_ffn_w_up', 'm_ffn_conv_w', 'm_ffn_conv_b', 'm_ffn_w_down', 'm_norm_final', 'v_norm_mix', 'v_norm_ffn', 'v_ret_gdn_w_in', 'v_gdn_conv_w', 'v_gdn_a_log', 'v_gdn_dt_bias', 'v_gdn_out_gain', 'v_ret_gdn_w_out', 'v_lru_w_in', 'v_lru_conv_w', 'v_lru_conv_b', 'v_lru_w_a', 'v_lru_b_a', 'v_lru_w_x', 'v_lru_b_x', 'v_lru_lambda', 'v_lru_w_out', 'v_ffn_w_up', 'v_ffn_conv_w', 'v_ffn_conv_b', 'v_ffn_w_down', 'v_norm_final']
TWIN_OUTPUTS = ['loss', 'grad_x', 'grad_norm_mix', 'grad_norm_ffn', 'grad_ret_gdn_w_in', 'grad_gdn_conv_w', 'grad_gdn_a_log', 'grad_gdn_dt_bias', 'grad_gdn_out_gain', 'grad_ret_gdn_w_out', 'grad_lru_w_in', 'grad_lru_conv_w', 'grad_lru_conv_b', 'grad_lru_w_a', 'grad_lru_b_a', 'grad_lru_w_x', 'grad_lru_b_x', 'grad_lru_lambda', 'grad_lru_w_out', 'grad_ffn_w_up', 'grad_ffn_conv_w', 'grad_ffn_conv_b', 'grad_ffn_w_down', 'grad_norm_final', 'delta_norm_mix', 'delta_norm_ffn', 'delta_ret_gdn_w_in', 'delta_gdn_conv_w', 'delta_gdn_a_log', 'delta_gdn_dt_bias', 'delta_gdn_out_gain', 'delta_ret_gdn_w_out', 'delta_lru_w_in', 'delta_lru_conv_w', 'delta_lru_conv_b', 'delta_lru_w_a', 'delta_lru_b_a', 'delta_lru_w_x', 'delta_lru_b_x', 'delta_lru_lambda', 'delta_lru_w_out', 'delta_ffn_w_up', 'delta_ffn_conv_w', 'delta_ffn_conv_b', 'delta_ffn_w_down', 'delta_norm_final', 'new_m_norm_mix', 'new_m_norm_ffn', 'new_m_ret_gdn_w_in', 'new_m_gdn_conv_w', 'new_m_gdn_a_log', 'new_m_gdn_dt_bias', 'new_m_gdn_out_gain', 'new_m_ret_gdn_w_out', 'new_m_lru_w_in', 'new_m_lru_conv_w', 'new_m_lru_conv_b', 'new_m_lru_w_a', 'new_m_lru_b_a', 'new_m_lru_w_x', 'new_m_lru_b_x', 'new_m_lru_lambda', 'new_m_lru_w_out', 'new_m_ffn_w_up', 'new_m_ffn_conv_w', 'new_m_ffn_conv_b', 'new_m_ffn_w_down', 'new_m_norm_final', 'new_v_norm_mix', 'new_v_norm_ffn', 'new_v_ret_gdn_w_in', 'new_v_gdn_conv_w', 'new_v_gdn_a_log', 'new_v_gdn_dt_bias', 'new_v_gdn_out_gain', 'new_v_ret_gdn_w_out', 'new_v_lru_w_in', 'new_v_lru_conv_w', 'new_v_lru_conv_b', 'new_v_lru_w_a', 'new_v_lru_b_a', 'new_v_lru_w_x', 'new_v_lru_b_x', 'new_v_lru_lambda', 'new_v_lru_w_out', 'new_v_ffn_w_up', 'new_v_ffn_conv_w', 'new_v_ffn_conv_b', 'new_v_ffn_w_down', 'new_v_norm_final']
TWIN_LEAF_KINDS = {'loss': 'loss', 'grad_x': 'grad_x', 'grad_norm_mix': 'grad_w', 'grad_norm_ffn': 'grad_w', 'grad_ret_gdn_w_in': 'grad_w', 'grad_gdn_conv_w': 'grad_w', 'grad_gdn_a_log': 'grad_w', 'grad_gdn_dt_bias': 'grad_w', 'grad_gdn_out_gain': 'grad_w', 'grad_ret_gdn_w_out': 'grad_w', 'grad_lru_w_in': 'grad_w', 'grad_lru_conv_w': 'grad_w', 'grad_lru_conv_b': 'grad_w', 'grad_lru_w_a': 'grad_w', 'grad_lru_b_a': 'grad_w', 'grad_lru_w_x': 'grad_w', 'grad_lru_b_x': 'grad_w', 'grad_lru_lambda': 'grad_w', 'grad_lru_w_out': 'grad_w', 'grad_ffn_w_up': 'grad_w', 'grad_ffn_conv_w': 'grad_w', 'grad_ffn_conv_b': 'grad_w', 'grad_ffn_w_down': 'grad_w', 'grad_norm_final': 'grad_w', 'delta_norm_mix': 'delta_w', 'delta_norm_ffn': 'delta_w', 'delta_ret_gdn_w_in': 'delta_w', 'delta_gdn_conv_w': 'delta_w', 'delta_gdn_a_log': 'delta_w', 'delta_gdn_dt_bias': 'delta_w', 'delta_gdn_out_gain': 'delta_w', 'delta_ret_gdn_w_out': 'delta_w', 'delta_lru_w_in': 'delta_w', 'delta_lru_conv_w': 'delta_w', 'delta_lru_conv_b': 'delta_w', 'delta_lru_w_a': 'delta_w', 'delta_lru_b_a': 'delta_w', 'delta_lru_w_x': 'delta_w', 'delta_lru_b_x': 'delta_w', 'delta_lru_lambda': 'delta_w', 'delta_lru_w_out': 'delta_w', 'delta_ffn_w_up': 'delta_w', 'delta_ffn_conv_w': 'delta_w', 'delta_ffn_conv_b': 'delta_w', 'delta_ffn_w_down': 'delta_w', 'delta_norm_final': 'delta_w', 'new_m_norm_mix': 'new_m', 'new_m_norm_ffn': 'new_m', 'new_m_ret_gdn_w_in': 'new_m', 'new_m_gdn_conv_w': 'new_m', 'new_m_gdn_a_log': 'new_m', 'new_m_gdn_dt_bias': 'new_m', 'new_m_gdn_out_gain': 'new_m', 'new_m_ret_gdn_w_out': 'new_m', 'new_m_lru_w_in': 'new_m', 'new_m_lru_conv_w': 'new_m', 'new_m_lru_conv_b': 'new_m', 'new_m_lru_w_a': 'new_m', 'new_m_lru_b_a': 'new_m', 'new_m_lru_w_x': 'new_m', 'new_m_lru_b_x': 'new_m', 'new_m_lru_lambda': 'new_m', 'new_m_lru_w_out': 'new_m', 'new_m_ffn_w_up': 'new_m', 'new_m_ffn_conv_w': 'new_m', 'new_m_ffn_conv_b': 'new_m', 'new_m_ffn_w_down': 'new_m', 'new_m_norm_final': 'new_m', 'new_v_norm_mix': 'new_v', 'new_v_norm_ffn': 'new_v', 'new_v_ret_gdn_w_in': 'new_v', 'new_v_gdn_conv_w': 'new_v', 'new_v_gdn_a_log': 'new_v', 'new_v_gdn_dt_bias': 'new_v', 'new_v_gdn_out_gain': 'new_v', 'new_v_ret_gdn_w_out': 'new_v', 'new_v_lru_w_in': 'new_v', 'new_v_lru_conv_w': 'new_v', 'new_v_lru_conv_b': 'new_v', 'new_v_lru_w_a': 'new_v', 'new_v_lru_b_a': 'new_v', 'new_v_lru_w_x': 'new_v', 'new_v_lru_b_x': 'new_v', 'new_v_lru_lambda': 'new_v', 'new_v_lru_w_out': 'new_v', 'new_v_ffn_w_up': 'new_v', 'new_v_ffn_conv_w': 'new_v', 'new_v_ffn_conv_b': 'new_v', 'new_v_ffn_w_down': 'new_v', 'new_v_norm_final': 'new_v'}


def _forward(args):
    return _fwd_reference(*[args[k] for k in FWD_PARAMS])


def _output_shape():
    out = _jax.eval_shape(lambda: _forward(_fwd_setup_inputs(0)))
    return out.shape, out.dtype

N_MICROBATCH = 1
ADAM_LR = 0.001
ADAM_B1 = 0.9
ADAM_B2 = 0.999
ADAM_EPS = 1e-08
ADAM_WD = 0.01
ADAM_STEP = 10
PER_EXAMPLE_BATCH_AXIS = {'x': 0, 'loss_target': 0}
SHARED_INPUTS = []
_WEIGHT_DTYPES = {'norm_mix': _jnp.float32, 'norm_ffn': _jnp.float32, 'ret_gdn_w_in': _jnp.float32, 'gdn_conv_w': _jnp.float32, 'gdn_a_log': _jnp.float32, 'gdn_dt_bias': _jnp.float32, 'gdn_out_gain': _jnp.float32, 'ret_gdn_w_out': _jnp.float32, 'lru_w_in': _jnp.float32, 'lru_conv_w': _jnp.float32, 'lru_conv_b': _jnp.float32, 'lru_w_a': _jnp.float32, 'lru_b_a': _jnp.float32, 'lru_w_x': _jnp.float32, 'lru_b_x': _jnp.float32, 'lru_lambda': _jnp.float32, 'lru_w_out': _jnp.float32, 'ffn_w_up': _jnp.float32, 'ffn_conv_w': _jnp.float32, 'ffn_conv_b': _jnp.float32, 'ffn_w_down': _jnp.float32, 'norm_final': _jnp.float32}
MOMENT_SCALE = {'norm_mix': 1.212566e-01, 'norm_ffn': 8.550332e-02, 'ret_gdn_w_in': 7.561200e-02, 'gdn_conv_w': 6.707512e-02, 'gdn_a_log': 6.980516e-01, 'gdn_dt_bias': 6.864469e-01, 'gdn_out_gain': 1.812571e-01, 'ret_gdn_w_out': 8.273513e-02, 'lru_w_in': 4.861609e-02, 'lru_conv_w': 5.226609e-02, 'lru_conv_b': 4.214724e-01, 'lru_w_a': 1.289068e-02, 'lru_b_a': 1.232490e-02, 'lru_w_x': 2.298036e-02, 'lru_b_x': 1.678592e-02, 'lru_lambda': 2.585897e-02, 'lru_w_out': 4.894684e-02, 'ffn_w_up': 3.599116e-02, 'ffn_conv_w': 3.584317e-02, 'ffn_conv_b': 4.111925e-02, 'ffn_w_down': 5.885113e-02, 'norm_final': 1.601363e+01}


def _to_microbatches(a, axis):
    t = _jnp.moveaxis(a, axis, 0)
    t = t.reshape((N_MICROBATCH, t.shape[0] // N_MICROBATCH) + t.shape[1:])
    return _jnp.moveaxis(t, 1, axis + 1)


def setup_inputs(seed: int = 0) -> dict:
    inp = _fwd_setup_inputs(seed)
    key = _jax.random.fold_in(_jax.random.key(seed), 7919)
    shape, _ = _output_shape()
    out = dict(inp)
    out["loss_target"] = _jax.random.normal(_jax.random.fold_in(key, 0), shape, _jnp.float32)
    for i, name in enumerate(TWIN_WEIGHTS):
        w = inp[name].astype(_jnp.float32)
        if MOMENT_SCALE is None:
            s = _jnp.sqrt(_jnp.mean(_jnp.square(w)) + 1e-30)
        else:
            s = MOMENT_SCALE[name]
        km, kv = _jax.random.split(_jax.random.fold_in(key, i + 1))
        out[name] = w
        out["m_" + name] = s * _jax.random.normal(km, w.shape, _jnp.float32)
        out["v_" + name] = (s * s) * _jax.random.uniform(kv, w.shape, _jnp.float32, 0.5, 1.5)
    if N_MICROBATCH > 1:
        for name, axis in PER_EXAMPLE_BATCH_AXIS.items():
            out[name] = _to_microbatches(out[name], axis)
    return {'x': out['x'], 'norm_mix': out['norm_mix'], 'norm_ffn': out['norm_ffn'], 'ret_gdn_w_in': out['ret_gdn_w_in'], 'gdn_conv_w': out['gdn_conv_w'], 'gdn_a_log': out['gdn_a_log'], 'gdn_dt_bias': out['gdn_dt_bias'], 'gdn_out_gain': out['gdn_out_gain'], 'ret_gdn_w_out': out['ret_gdn_w_out'], 'lru_w_in': out['lru_w_in'], 'lru_conv_w': out['lru_conv_w'], 'lru_conv_b': out['lru_conv_b'], 'lru_w_a': out['lru_w_a'], 'lru_b_a': out['lru_b_a'], 'lru_w_x': out['lru_w_x'], 'lru_b_x': out['lru_b_x'], 'lru_lambda': out['lru_lambda'], 'lru_w_out': out['lru_w_out'], 'ffn_w_up': out['ffn_w_up'], 'ffn_conv_w': out['ffn_conv_w'], 'ffn_conv_b': out['ffn_conv_b'], 'ffn_w_down': out['ffn_w_down'], 'norm_final': out['norm_final'], 'loss_target': out['loss_target'], 'm_norm_mix': out['m_norm_mix'], 'm_norm_ffn': out['m_norm_ffn'], 'm_ret_gdn_w_in': out['m_ret_gdn_w_in'], 'm_gdn_conv_w': out['m_gdn_conv_w'], 'm_gdn_a_log': out['m_gdn_a_log'], 'm_gdn_dt_bias': out['m_gdn_dt_bias'], 'm_gdn_out_gain': out['m_gdn_out_gain'], 'm_ret_gdn_w_out': out['m_ret_gdn_w_out'], 'm_lru_w_in': out['m_lru_w_in'], 'm_lru_conv_w': out['m_lru_conv_w'], 'm_lru_conv_b': out['m_lru_conv_b'], 'm_lru_w_a': out['m_lru_w_a'], 'm_lru_b_a': out['m_lru_b_a'], 'm_lru_w_x': out['m_lru_w_x'], 'm_lru_b_x': out['m_lru_b_x'], 'm_lru_lambda': out['m_lru_lambda'], 'm_lru_w_out': out['m_lru_w_out'], 'm_ffn_w_up': out['m_ffn_w_up'], 'm_ffn_conv_w': out['m_ffn_conv_w'], 'm_ffn_conv_b': out['m_ffn_conv_b'], 'm_ffn_w_down': out['m_ffn_w_down'], 'm_norm_final': out['m_norm_final'], 'v_norm_mix': out['v_norm_mix'], 'v_norm_ffn': out['v_norm_ffn'], 'v_ret_gdn_w_in': out['v_ret_gdn_w_in'], 'v_gdn_conv_w': out['v_gdn_conv_w'], 'v_gdn_a_log': out['v_gdn_a_log'], 'v_gdn_dt_bias': out['v_gdn_dt_bias'], 'v_gdn_out_gain': out['v_gdn_out_gain'], 'v_ret_gdn_w_out': out['v_ret_gdn_w_out'], 'v_lru_w_in': out['v_lru_w_in'], 'v_lru_conv_w': out['v_lru_conv_w'], 'v_lru_conv_b': out['v_lru_conv_b'], 'v_lru_w_a': out['v_lru_w_a'], 'v_lru_b_a': out['v_lru_b_a'], 'v_lru_w_x': out['v_lru_w_x'], 'v_lru_b_x': out['v_lru_b_x'], 'v_lru_lambda': out['v_lru_lambda'], 'v_lru_w_out': out['v_lru_w_out'], 'v_ffn_w_up': out['v_ffn_w_up'], 'v_ffn_conv_w': out['v_ffn_conv_w'], 'v_ffn_conv_b': out['v_ffn_conv_b'], 'v_ffn_w_down': out['v_ffn_w_down'], 'v_norm_final': out['v_norm_final']}


def _loss(weights, diff, rest, loss_target):
    with _jax.named_scope("forward"):
        args = {**rest, TWIN_DIFF_INPUT: diff, **{k: w.astype(_WEIGHT_DTYPES[k]) for k, w in weights.items()}}
        y = _forward(args)
    with _jax.named_scope("loss_head"):
        err = _jnp.square(y.astype(_jnp.float32) - loss_target)
        return 0.5 * _jnp.sum(_jnp.mean(err, axis=-1)) if err.ndim else 0.5 * err


def _adamw(w, g, m, v):
    m = ADAM_B1 * m + (1.0 - ADAM_B1) * g
    v = ADAM_B2 * v + (1.0 - ADAM_B2) * _jnp.square(g)
    m_hat = m / (1.0 - ADAM_B1 ** ADAM_STEP)
    v_hat = v / (1.0 - ADAM_B2 ** ADAM_STEP)
    delta = -ADAM_LR * (m_hat / (_jnp.sqrt(v_hat) + ADAM_EPS) + ADAM_WD * w)
    return delta, m, v


def reference(x, norm_mix, norm_ffn, ret_gdn_w_in, gdn_conv_w, gdn_a_log, gdn_dt_bias, gdn_out_gain, ret_gdn_w_out, lru_w_in, lru_conv_w, lru_conv_b, lru_w_a, lru_b_a, lru_w_x, lru_b_x, lru_lambda, lru_w_out, ffn_w_up, ffn_conv_w, ffn_conv_b, ffn_w_down, norm_final, loss_target, m_norm_mix, m_norm_ffn, m_ret_gdn_w_in, m_gdn_conv_w, m_gdn_a_log, m_gdn_dt_bias, m_gdn_out_gain, m_ret_gdn_w_out, m_lru_w_in, m_lru_conv_w, m_lru_conv_b, m_lru_w_a, m_lru_b_a, m_lru_w_x, m_lru_b_x, m_lru_lambda, m_lru_w_out, m_ffn_w_up, m_ffn_conv_w, m_ffn_conv_b, m_ffn_w_down, m_norm_final, v_norm_mix, v_norm_ffn, v_ret_gdn_w_in, v_gdn_conv_w, v_gdn_a_log, v_gdn_dt_bias, v_gdn_out_gain, v_ret_gdn_w_out, v_lru_w_in, v_lru_conv_w, v_lru_conv_b, v_lru_w_a, v_lru_b_a, v_lru_w_x, v_lru_b_x, v_lru_lambda, v_lru_w_out, v_ffn_w_up, v_ffn_conv_w, v_ffn_conv_b, v_ffn_w_down, v_norm_final):
    given = dict(x=x, norm_mix=norm_mix, norm_ffn=norm_ffn, ret_gdn_w_in=ret_gdn_w_in, gdn_conv_w=gdn_conv_w, gdn_a_log=gdn_a_log, gdn_dt_bias=gdn_dt_bias, gdn_out_gain=gdn_out_gain, ret_gdn_w_out=ret_gdn_w_out, lru_w_in=lru_w_in, lru_conv_w=lru_conv_w, lru_conv_b=lru_conv_b, lru_w_a=lru_w_a, lru_b_a=lru_b_a, lru_w_x=lru_w_x, lru_b_x=lru_b_x, lru_lambda=lru_lambda, lru_w_out=lru_w_out, ffn_w_up=ffn_w_up, ffn_conv_w=ffn_conv_w, ffn_conv_b=ffn_conv_b, ffn_w_down=ffn_w_down, norm_final=norm_final, loss_target=loss_target, m_norm_mix=m_norm_mix, m_norm_ffn=m_norm_ffn, m_ret_gdn_w_in=m_ret_gdn_w_in, m_gdn_conv_w=m_gdn_conv_w, m_gdn_a_log=m_gdn_a_log, m_gdn_dt_bias=m_gdn_dt_bias, m_gdn_out_gain=m_gdn_out_gain, m_ret_gdn_w_out=m_ret_gdn_w_out, m_lru_w_in=m_lru_w_in, m_lru_conv_w=m_lru_conv_w, m_lru_conv_b=m_lru_conv_b, m_lru_w_a=m_lru_w_a, m_lru_b_a=m_lru_b_a, m_lru_w_x=m_lru_w_x, m_lru_b_x=m_lru_b_x, m_lru_lambda=m_lru_lambda, m_lru_w_out=m_lru_w_out, m_ffn_w_up=m_ffn_w_up, m_ffn_conv_w=m_ffn_conv_w, m_ffn_conv_b=m_ffn_conv_b, m_ffn_w_down=m_ffn_w_down, m_norm_final=m_norm_final, v_norm_mix=v_norm_mix, v_norm_ffn=v_norm_ffn, v_ret_gdn_w_in=v_ret_gdn_w_in, v_gdn_conv_w=v_gdn_conv_w, v_gdn_a_log=v_gdn_a_log, v_gdn_dt_bias=v_gdn_dt_bias, v_gdn_out_gain=v_gdn_out_gain, v_ret_gdn_w_out=v_ret_gdn_w_out, v_lru_w_in=v_lru_w_in, v_lru_conv_w=v_lru_conv_w, v_lru_conv_b=v_lru_conv_b, v_lru_w_a=v_lru_w_a, v_lru_b_a=v_lru_b_a, v_lru_w_x=v_lru_w_x, v_lru_b_x=v_lru_b_x, v_lru_lambda=v_lru_lambda, v_lru_w_out=v_lru_w_out, v_ffn_w_up=v_ffn_w_up, v_ffn_conv_w=v_ffn_conv_w, v_ffn_conv_b=v_ffn_conv_b, v_ffn_w_down=v_ffn_w_down, v_norm_final=v_norm_final)
    weights = {n: given[n] for n in TWIN_WEIGHTS}
    shared = {n: given[n] for n in SHARED_INPUTS}
    per_example = {n: given[n] for n in ['x']}
    grad_fn = _jax.value_and_grad(_loss, argnums=(0, 1))

    def one_microbatch(ex, loss_target):
        ex = dict(ex)
        diff = ex.pop(TWIN_DIFF_INPUT)
        return grad_fn(weights, diff, {**shared, **ex}, loss_target)

    if N_MICROBATCH == 1:
        loss, (grad_w, grad_x) = one_microbatch(per_example, given["loss_target"])
    else:
        def body(carry, xs):
            loss_sum, grad_sum = carry
            l_k, (gw_k, gx_k) = one_microbatch(xs[0], xs[1])
            with _jax.named_scope("update"):
                return (loss_sum + l_k, _jax.tree.map(_jnp.add, grad_sum, gw_k)), gx_k

        init = (_jnp.zeros((), _jnp.float32), _jax.tree.map(_jnp.zeros_like, weights))
        (loss, grad_w), grad_x = _jax.lax.scan(body, init, (per_example, given["loss_target"]))
    with _jax.named_scope("update"):
        delta_w, new_m, new_v = {}, {}, {}
        for n in TWIN_WEIGHTS:
            delta_w[n], new_m[n], new_v[n] = _adamw(weights[n], grad_w[n], given["m_" + n], given["v_" + n])
    return (loss, grad_x, *[grad_w[n] for n in TWIN_WEIGHTS], *[delta_w[n] for n in TWIN_WEIGHTS],
            *[new_m[n] for n in TWIN_WEIGHTS], *[new_v[n] for n in TWIN_WEIGHTS])
```

```python
import functools
import math

import numpy as np
import jax
import jax.numpy as jnp
from jax import lax
from jax.experimental import pallas as pl
from jax.experimental.pallas import tpu as pltpu

F32 = jnp.float32
BF16 = jnp.bfloat16
HI = lax.Precision.HIGHEST
MESH = pl.DeviceIdType.MESH

N_DEV = 8
LANES = 128
SUBLANES = 8
EPS = 1e-6
D_MODEL = 1024
HEADS = 4
HEAD_DIM = 128
RET_CHUNK = 128
GDN_CHUNK = 64
ROPE_BASE = 10000.0
LRU_C = 8.0
D_FF = 2816
MAIN_IN = 4096
SMALL_IN = 8
QSCALE = HEAD_DIM ** -0.5

ADAM_LR, ADAM_B1, ADAM_B2, ADAM_EPS, ADAM_WD, ADAM_STEP = 0.001, 0.9, 0.999, 1e-08, 0.01, 10


def _cp(sem=None, vmem_mb=None):
    kw = {}
    if sem is not None:
        kw["dimension_semantics"] = sem
    if vmem_mb is not None:
        kw["vmem_limit_bytes"] = vmem_mb << 20
    return pltpu.CompilerParams(**kw)


def _rows(shape):
    return lax.broadcasted_iota(jnp.int32, shape, 0)


def _cols(shape):
    return lax.broadcasted_iota(jnp.int32, shape, 1)


def _shift_down(cur, prev8, s):
    if s == 0:
        return cur
    rc = pltpu.roll(cur, s, 0)
    rp = pltpu.roll(prev8, s, 0)
    top = jnp.where(_rows(prev8.shape) < s, rp, rc[:SUBLANES])
    return jnp.concatenate([top, rc[SUBLANES:]], axis=0)


def _shift_up(cur, next8, s):
    if s == 0:
        return cur
    tt = cur.shape[0]
    rc = pltpu.roll(cur, tt - s, 0)
    rn = pltpu.roll(next8, SUBLANES - s, 0)
    bot = jnp.where(_rows(next8.shape) >= SUBLANES - s, rn, rc[tt - SUBLANES:])
    return jnp.concatenate([rc[:tt - SUBLANES], bot], axis=0)


def _down_fill(x, d, fill):
    return jnp.where(_rows(x.shape) < d, fill, pltpu.roll(x, d, 0))


def _up_fill(x, d, fill):
    tt = x.shape[0]
    return jnp.where(_rows(x.shape) >= tt - d, fill, pltpu.roll(x, tt - d, 0))


def _sigmoid(x):
    return 1.0 / (1.0 + jnp.exp(-x))


def _softplus(x):
    return jnp.maximum(x, 0.0) + jnp.log(1.0 + jnp.exp(-jnp.abs(x)))


def _dot(a, b, dims=(((1,), (0,)), ((), ())), precision=None):
    return lax.dot_general(a, b, dims, preferred_element_type=F32, precision=precision)


NN = (((1,), (0,)), ((), ()))
NT = (((1,), (1,)), ((), ()))
TN = (((0,), (0,)), ((), ()))


def _bdot(a, b, dims=NN):
    return _dot(a.astype(BF16), b.astype(BF16), dims)


def _tile(dim, target):
    if dim <= target:
        return dim
    best = None
    for c in range(LANES, target + 1, LANES):
        if dim % c == 0:
            best = c
    assert best is not None, (dim, target)
    return best


def _mm(a, b, *, name, ta=False, tb=False, out_dtype=F32, res=None, tm=1024, tn=512, tk=512):
    m, k = (a.shape[1], a.shape[0]) if ta else a.shape
    n = b.shape[0] if tb else b.shape[1]
    tm, tn, tk = _tile(m, tm), _tile(n, tn), _tile(k, tk)
    nk = k // tk
    dims = (((0 if ta else 1,), (1 if tb else 0,)), ((), ()))

    def body(*refs):
        if res is None:
            a_ref, b_ref, o_ref, acc = refs
        else:
            a_ref, b_ref, r_ref, o_ref, acc = refs
        kk = pl.program_id(2)
        part = _bdot(a_ref[...], b_ref[...], dims)

        @pl.when(kk == 0)
        def _():
            acc[...] = part

        @pl.when(kk > 0)
        def _():
            acc[...] += part

        @pl.when(kk == nk - 1)
        def _():
            r = acc[...]
            if res is not None:
                r = r + r_ref[...]
            o_ref[...] = r.astype(out_dtype)

    a_spec = pl.BlockSpec((tk, tm), lambda i, j, kk: (kk, i)) if ta else pl.BlockSpec((tm, tk), lambda i, j, kk: (i, kk))
    b_spec = pl.BlockSpec((tn, tk), lambda i, j, kk: (j, kk)) if tb else pl.BlockSpec((tk, tn), lambda i, j, kk: (kk, j))
    o_spec = pl.BlockSpec((tm, tn), lambda i, j, kk: (i, j))
    in_specs = [a_spec, b_spec] + ([o_spec] if res is not None else [])
    args = (a, b) + ((res,) if res is not None else ())
    return pl.pallas_call(
        body, grid=(m // tm, n // tn, nk), in_specs=in_specs, out_specs=o_spec,
        out_shape=jax.ShapeDtypeStruct((m, n), out_dtype),
        scratch_shapes=[pltpu.VMEM((tm, tn), F32)], name=name,
        compiler_params=_cp(("parallel", "parallel", "arbitrary"), 48),
    )(*args)


def _norm_fwd(h, gain, *, name, tt=256):
    t, d = h.shape
    tt = min(tt, t)

    def body(h_ref, g_ref, o_ref):
        x = h_ref[...]
        r = lax.rsqrt(jnp.mean(x * x, axis=-1, keepdims=True) + EPS)
        o_ref[...] = (x * r * g_ref[...]).astype(BF16)

    row = pl.BlockSpec((tt, d), lambda i: (i, 0))
    return pl.pallas_call(
        body, grid=(t // tt,), in_specs=[row, pl.BlockSpec((1, d), lambda i: (0, 0))], out_specs=row,
        out_shape=jax.ShapeDtypeStruct((t, d), BF16), name=name, compiler_params=_cp(("parallel",)),
    )(h, gain)


def _norm_bwd(h, gain, dhn, dres, *, name, tt=256):
    t, d = h.shape
    tt = min(tt, t)

    def body(h_ref, g_ref, dy_ref, dr_ref, dx_ref, dg_ref):
        x, dy = h_ref[...], dy_ref[...]
        r = lax.rsqrt(jnp.mean(x * x, axis=-1, keepdims=True) + EPS)
        xh = x * r

        @pl.when(pl.program_id(0) == 0)
        def _():
            dg_ref[...] = jnp.zeros_like(dg_ref)

        dg_ref[...] += jnp.sum(dy * xh, axis=0, keepdims=True)
        dxh = dy * g_ref[...]
        dx_ref[...] = dr_ref[...] + r * (dxh - xh * jnp.mean(dxh * xh, axis=-1, keepdims=True))

    row = pl.BlockSpec((tt, d), lambda i: (i, 0))
    vec = pl.BlockSpec((1, d), lambda i: (0, 0))
    return pl.pallas_call(
        body, grid=(t // tt,), in_specs=[row, vec, row, row], out_specs=[row, vec],
        out_shape=[jax.ShapeDtypeStruct((t, d), F32), jax.ShapeDtypeStruct((1, d), F32)],
        name=name, compiler_params=_cp(("arbitrary",)),
    )(h, gain, dhn, dres)


def _final_loss(h, gain, target, *, name, tt=256):
    t, d = h.shape
    tt = min(tt, t)

    def body(h_ref, g_ref, tg_ref, dx_ref, dg_ref, loss_ref):
        x = h_ref[...]
        r = lax.rsqrt(jnp.mean(x * x, axis=-1, keepdims=True) + EPS)
        xh = x * r
        err = xh * g_ref[...] - tg_ref[...]

        @pl.when(pl.program_id(0) == 0)
        def _():
            dg_ref[...] = jnp.zeros_like(dg_ref)
            loss_ref[...] = jnp.zeros_like(loss_ref)

        loss_ref[...] += 0.5 * jnp.sum(jnp.mean(err * err, axis=-1, keepdims=True), axis=0, keepdims=True)
        dy = err * (1.0 / d)
        dg_ref[...] += jnp.sum(dy * xh, axis=0, keepdims=True)
        dxh = dy * g_ref[...]
        dx_ref[...] = r * (dxh - xh * jnp.mean(dxh * xh, axis=-1, keepdims=True))

    row = pl.BlockSpec((tt, d), lambda i: (i, 0))
    vec = pl.BlockSpec((1, d), lambda i: (0, 0))
    return pl.pallas_call(
        body, grid=(t // tt,), in_specs=[row, vec, row],
        out_specs=[row, vec, pl.BlockSpec((1, 1), lambda i: (0, 0))],
        out_shape=[jax.ShapeDtypeStruct((t, d), F32), jax.ShapeDtypeStruct((1, d), F32), jax.ShapeDtypeStruct((1, 1), F32)],
        name=name, compiler_params=_cp(("arbitrary",)),
    )(h, gain, target)


FFN_CT = 1408
FFN_TT = 256


def _prev8(n, tt):
    return jnp.maximum(n * (tt // SUBLANES) - 1, 0)


def _ffn_conv(cur, prev8, w, b):
    s1 = _shift_down(cur, prev8, 1)
    s2 = _shift_down(cur, prev8, 2)
    return w[0:1] * s2 + w[1:2] * s1 + w[2:3] * cur + b, s1, s2


def _ffn_act_fwd(ug, uv, wg, wv, bg, bv, *, name):
    t = ug.shape[0]
    tt, ct = min(FFN_TT, t), FFN_CT
    nj = D_FF // ct

    def body(ug_ref, uv_ref, pg, pv, wg_ref, wv_ref, bg_ref, bv_ref, o_ref):
        first = pl.program_id(1) == 0
        pgv = jnp.where(first, 0.0, pg[...])
        pvv = jnp.where(first, 0.0, pv[...])
        gate, _, _ = _ffn_conv(ug_ref[...], pgv, wg_ref[...], bg_ref[...])
        val, _, _ = _ffn_conv(uv_ref[...], pvv, wv_ref[...], bv_ref[...])
        o_ref[...] = (gate * _sigmoid(gate) * val).astype(BF16)

    cur = pl.BlockSpec((tt, ct), lambda j, n: (n, j))
    prev = pl.BlockSpec((SUBLANES, ct), lambda j, n: (_prev8(n, tt), j))
    wsp = lambda rows: pl.BlockSpec((rows, ct), lambda j, n: (0, j))
    return pl.pallas_call(
        body, grid=(nj, t // tt),
        in_specs=[cur, cur, prev, prev, wsp(3), wsp(3), wsp(1), wsp(1)], out_specs=cur,
        out_shape=jax.ShapeDtypeStruct((t, D_FF), BF16), name=name,
        compiler_params=_cp(("parallel", "arbitrary"), 48),
    )(ug, uv, ug, uv, wg, wv, bg, bv)


def _ffn_act_bwd(ug, uv, da, wg, wv, bg, bv, *, name):
    t = ug.shape[0]
    tt, ct = min(FFN_TT, t), FFN_CT
    nj = D_FF // ct
    nt = t // tt

    def body(ug_ref, uv_ref, pg, pv, da_ref, wg_ref, wv_ref, bg_ref, bv_ref,
             dug, duv, dwg, dwv, dbg, dbv, head_g, head_v):
        n = pl.program_id(1)
        tile0 = n == nt - 1

        @pl.when(n == 0)
        def _():
            for r in (head_g, head_v, dwg, dwv, dbg, dbv):
                r[...] = jnp.zeros_like(r)

        xg, xv = ug_ref[...], uv_ref[...]
        pgv = jnp.where(tile0, 0.0, pg[...])
        pvv = jnp.where(tile0, 0.0, pv[...])
        gate, g1, g2 = _ffn_conv(xg, pgv, wg_ref[...], bg_ref[...])
        val, v1, v2 = _ffn_conv(xv, pvv, wv_ref[...], bv_ref[...])
        d = da_ref[...]
        sg = _sigmoid(gate)
        dval = d * gate * sg
        dgate = d * val * sg * (1.0 + gate * (1.0 - sg))
        for dc, w_ref, x0, x1, x2, head, du, dw, db in ((dgate, wg_ref, xg, g1, g2, head_g, dug, dwg, dbg),
                                                        (dval, wv_ref, xv, v1, v2, head_v, duv, dwv, dbv)):
            w = w_ref[...]
            hd = head[...]
            du[...] = (w[2:3] * dc + w[1:2] * _shift_up(dc, hd, 1) + w[0:1] * _shift_up(dc, hd, 2)).astype(BF16)
            dw[0:1, :] += jnp.sum(dc * x2, axis=0, keepdims=True)
            dw[1:2, :] += jnp.sum(dc * x1, axis=0, keepdims=True)
            dw[2:3, :] += jnp.sum(dc * x0, axis=0, keepdims=True)
            db[...] += jnp.sum(dc, axis=0, keepdims=True)
            head[...] = dc[:SUBLANES]

    rev = lambda n: nt - 1 - n
    cur = pl.BlockSpec((tt, ct), lambda j, n: (rev(n), j))
    prev = pl.BlockSpec((SUBLANES, ct), lambda j, n: (_prev8(rev(n), tt), j))
    wsp = lambda rows: pl.BlockSpec((rows, ct), lambda j, n: (0, j))
    return pl.pallas_call(
        body, grid=(nj, nt),
        in_specs=[cur, cur, prev, prev, cur, wsp(3), wsp(3), wsp(1), wsp(1)],
        out_specs=[cur, cur, wsp(3), wsp(3), wsp(1), wsp(1)],
        out_shape=[jax.ShapeDtypeStruct((t, D_FF), BF16)] * 2 + [jax.ShapeDtypeStruct((3, D_FF), F32)] * 2
        + [jax.ShapeDtypeStruct((1, D_FF), F32)] * 2,
        scratch_shapes=[pltpu.VMEM((SUBLANES, ct), F32)] * 2, name=name,
        compiler_params=_cp(("parallel", "arbitrary"), 48),
    )(ug, uv, ug, uv, da, wg, wv, bg, bv)


LRU_TT = 256
LRU_CT = 512
GELU_C = math.sqrt(2.0 / math.pi)
GELU_A = 0.044715


def _gelu(x):
    return 0.5 * x * (1.0 + jnp.tanh(GELU_C * (x + GELU_A * x * x * x)))


def _gelu_grad(x):
    th = jnp.tanh(GELU_C * (x + GELU_A * x * x * x))
    return 0.5 * (1.0 + th) + 0.5 * x * (1.0 - th * th) * GELU_C * (1.0 + 3.0 * GELU_A * x * x)


def _neg_expm1(x):
    poly = -x * (1.0 + x * (0.5 + x * (1.0 / 6 + x * (1.0 / 24 + x * (1.0 / 120)))))
    return jnp.where(x > -0.1, poly, 1.0 - jnp.exp(x))


def _conv4(x, p8, w, b=None):
    s1, s2, s3 = _shift_down(x, p8, 1), _shift_down(x, p8, 2), _shift_down(x, p8, 3)
    y = w[0:1] * s3 + w[1:2] * s2 + w[2:3] * s1 + w[3:4] * x
    return (y if b is None else y + b), (s1, s2, s3)


def _conv4_bwd(dy, head, x, shifts, w):
    s1, s2, s3 = shifts
    dx = w[3:4] * dy + w[2:3] * _shift_up(dy, head, 1) + w[1:2] * _shift_up(dy, head, 2) + w[0:1] * _shift_up(dy, head, 3)
    dws = [jnp.sum(dy * s, axis=0, keepdims=True) for s in (s3, s2, s1, x)]
    return dx, dws


def _blockdiag(x, w_ref, dims=NN):
    nb = x.shape[1] // LANES
    return jnp.concatenate([_bdot(x[:, LANES * i:LANES * (i + 1)], w_ref[i], dims) for i in range(nb)], axis=1)


def _lru_gates(xr, wa_ref, wx_ref, ba, bx, lam):
    r = _sigmoid(_blockdiag(xr, wa_ref) + ba)
    i = _sigmoid(_blockdiag(xr, wx_ref) + bx)
    sp = _softplus(-lam)
    la = -LRU_C * r * sp
    a = jnp.exp(la)
    mult = jnp.sqrt(_neg_expm1(2.0 * la))
    return r, i, sp, a, mult


def _lru_specs(t, tt, ct, order):
    nb = ct // LANES
    cur = pl.BlockSpec((tt, ct), lambda j, n: (order(n), j))
    prev = pl.BlockSpec((SUBLANES, ct), lambda j, n: (_prev8(order(n), tt), j))
    vec = lambda rows: pl.BlockSpec((rows, ct), lambda j, n: (0, j))
    blk = pl.BlockSpec((nb, LANES, LANES), lambda j, n: (j, 0, 0))
    return cur, prev, vec, blk


def _lru_fwd(gate, xpre, cw, cb, wa, ba, wx, bx, lam, *, name):
    t, c = gate.shape
    tt, ct = min(LRU_TT, t), LRU_CT
    cur, prev, vec, blk = _lru_specs(t, tt, ct, lambda n: n)

    def body(gate_ref, x_ref, p_ref, cw_ref, cb_ref, wa_ref, ba_ref, wx_ref, bx_ref, lam_ref, y_ref, hs_ref, carry):
        n = pl.program_id(1)

        @pl.when(n == 0)
        def _():
            carry[...] = jnp.zeros_like(carry)

        p8 = jnp.where(n == 0, 0.0, p_ref[...])
        xr, _ = _conv4(x_ref[...], p8, cw_ref[...], cb_ref[...])
        r, i, sp, a, mult = _lru_gates(xr, wa_ref, wx_ref, ba_ref[...], bx_ref[...], lam_ref[...])
        acc_a, acc_b = a, mult * (i * xr)
        d = 1
        while d < tt:
            acc_b = acc_a * _down_fill(acc_b, d, 0.0) + acc_b
            acc_a = acc_a * _down_fill(acc_a, d, 1.0)
            d *= 2
        hs = acc_b + acc_a * carry[0:1]
        carry[...] = jnp.broadcast_to(hs[tt - 1:tt], carry.shape)
        hs_ref[...] = hs
        y_ref[...] = (_gelu(gate_ref[...]) * hs).astype(BF16)

    return pl.pallas_call(
        body, grid=(c // ct, t // tt),
        in_specs=[cur, cur, prev, vec(4), vec(1), blk, vec(1), blk, vec(1), vec(1)],
        out_specs=[cur, cur],
        out_shape=[jax.ShapeDtypeStruct((t, c), BF16), jax.ShapeDtypeStruct((t, c), F32)],
        scratch_shapes=[pltpu.VMEM((SUBLANES, ct), F32)], name=name,
        compiler_params=_cp(("parallel", "arbitrary"), 48),
    )(gate, xpre, xpre, cw, cb, wa, ba, wx, bx, lam)


def _lru_bwd(gate, xpre, hs, dy, cw, cb, wa, ba, wx, bx, lam, *, name):
    t, c = gate.shape
    tt, ct = min(LRU_TT, t), LRU_CT
    nt = t // tt
    cur, prev, vec, blk = _lru_specs(t, tt, ct, lambda n: nt - 1 - n)

    def body(gate_ref, x_ref, p_ref, hs_ref, phs_ref, dy_ref, cw_ref, cb_ref, wa_ref, ba_ref, wx_ref, bx_ref, lam_ref,
             dgate_ref, dx_ref, dcw_ref, dcb_ref, dwa_ref, dba_ref, dwx_ref, dbx_ref, dlam_ref, carry, head):
        n = pl.program_id(1)
        tile0 = n == nt - 1

        @pl.when(n == 0)
        def _():
            for ref in (carry, head, dcw_ref, dcb_ref, dwa_ref, dba_ref, dwx_ref, dbx_ref, dlam_ref):
                ref[...] = jnp.zeros_like(ref)

        xp, cwv, lam = x_ref[...], cw_ref[...], lam_ref[...]
        p8 = jnp.where(tile0, 0.0, p_ref[...])
        xr, shifts = _conv4(xp, p8, cwv, cb_ref[...])
        r, i, sp, a, mult = _lru_gates(xr, wa_ref, wx_ref, ba_ref[...], bx_ref[...], lam)
        gate, hsv, dyv = gate_ref[...], hs_ref[...], dy_ref[...]
        dgate_ref[...] = (dyv * hsv * _gelu_grad(gate)).astype(BF16)
        acc_b = dyv * _gelu(gate) + jnp.where(_rows(a.shape) == tt - 1, carry[0:1], 0.0)
        acc_a = _up_fill(a, 1, 0.0)
        d = 1
        while d < tt:
            acc_b = acc_b + acc_a * _up_fill(acc_b, d, 0.0)
            acc_a = acc_a * _up_fill(acc_a, d, 0.0)
            d *= 2
        gsum = acc_b
        carry[...] = jnp.broadcast_to(a[0:1] * gsum[0:1], carry.shape)
        hprev = _shift_down(hsv, jnp.where(tile0, 0.0, phs_ref[...]), 1)
        da = gsum * hprev
        dmult = gsum * i * xr
        di = gsum * mult * xr
        dxr = gsum * mult * i
        dla = da * a - dmult * (a * a) / mult
        dr = dla * (-LRU_C * sp)
        dlam_ref[...] += jnp.sum(dla * (-LRU_C * r), axis=0, keepdims=True) * (-_sigmoid(-lam))
        dpa = dr * r * (1.0 - r)
        dpx = di * i * (1.0 - i)
        dba_ref[...] += jnp.sum(dpa, axis=0, keepdims=True)
        dbx_ref[...] += jnp.sum(dpx, axis=0, keepdims=True)
        dxr = dxr + _blockdiag(dpa, wa_ref, NT) + _blockdiag(dpx, wx_ref, NT)
        for b in range(ct // LANES):
            sl = slice(LANES * b, LANES * (b + 1))
            dwa_ref[b] += _bdot(xr[:, sl], dpa[:, sl], TN)
            dwx_ref[b] += _bdot(xr[:, sl], dpx[:, sl], TN)
        dx, dws = _conv4_bwd(dxr, head[...], xp, shifts, cwv)
        dx_ref[...] = dx.astype(BF16)
        for k in range(4):
            dcw_ref[k:k + 1, :] += dws[k]
        dcb_ref[...] += jnp.sum(dxr, axis=0, keepdims=True)
        head[...] = dxr[:SUBLANES]

    return pl.pallas_call(
        body, grid=(c // ct, nt),
        in_specs=[cur, cur, prev, cur, prev, cur, vec(4), vec(1), blk, vec(1), blk, vec(1), vec(1)],
        out_specs=[cur, cur, vec(4), vec(1), blk, vec(1), blk, vec(1), vec(1)],
        out_shape=[jax.ShapeDtypeStruct((t, c), BF16)] * 2 + [jax.ShapeDtypeStruct((4, c), F32), jax.ShapeDtypeStruct((1, c), F32),
                   jax.ShapeDtypeStruct(wa.shape, F32), jax.ShapeDtypeStruct((1, c), F32),
                   jax.ShapeDtypeStruct(wx.shape, F32), jax.ShapeDtypeStruct((1, c), F32), jax.ShapeDtypeStruct((1, c), F32)],
        scratch_shapes=[pltpu.VMEM((SUBLANES, ct), F32)] * 2, name=name,
        compiler_params=_cp(("parallel", "arbitrary"), 48),
    )(gate, xpre, xpre, hs, hs, dy, cw, cb, wa, ba, wx, bx, lam)


RET_W = HEADS * HEAD_DIM
HALF = HEAD_DIM // 2


def _ret_tables(t):
    c = RET_CHUNK
    inv_freq = ROPE_BASE ** (-jnp.arange(HALF, dtype=F32) / HALF)
    ang = jnp.arange(t, dtype=jnp.int32).astype(F32)[:, None] * inv_freq[None, :]
    cos, sin = jnp.cos(ang), jnp.sin(ang)
    cosf = jnp.concatenate([cos, cos], axis=1)
    sinf = jnp.concatenate([-sin, sin], axis=1)
    log_gamma = jnp.log1p(-jnp.exp2(-5.0 - jnp.arange(HEADS, dtype=F32)))
    idx = jnp.arange(c, dtype=F32)
    rel = idx[:, None] - idx[None, :]
    causal = rel >= 0
    dmask = jnp.where(causal, jnp.exp(log_gamma[:, None, None] * jnp.where(causal, rel, 0.0)), 0.0)
    ktail = jnp.exp(log_gamma[:, None] * (c - 1 - idx))
    qdec = jnp.exp(log_gamma[:, None] * (idx + 1.0))
    rowtab = jnp.broadcast_to(jnp.stack([ktail, qdec], axis=1)[..., None], (HEADS, 2, c, HEAD_DIM))
    cdec = jnp.broadcast_to(jnp.exp(log_gamma * c)[:, None, None], (HEADS, SUBLANES, HEAD_DIM))
    return cosf, sinf, dmask, rowtab, cdec


def _rotary(x, cosf, sinf):
    return x * cosf + pltpu.roll(x, HALF, 1) * sinf


def _rotary_t(dx, cosf, sinf):
    return dx * cosf + pltpu.roll(dx * sinf, HALF, 1)


def _ret_specs(c, order):
    full = lambda shape: pl.BlockSpec(shape, lambda n: (0,) * len(shape))
    return dict(
        proj=pl.BlockSpec((c, 4 * RET_W), lambda n: (order(n), 0)),
        rot=pl.BlockSpec((c, HEAD_DIM), lambda n: (order(n), 0)),
        dmask=full((HEADS, c, c)), rowtab=full((HEADS, 2, c, HEAD_DIM)), cdec=full((HEADS, SUBLANES, HEAD_DIM)),
        state=pl.BlockSpec((1, HEADS, HEAD_DIM, HEAD_DIM), lambda n: (order(n), 0, 0, 0)),
        half=pl.BlockSpec((c, RET_W), lambda n: (order(n), 0)),
    )


def _ret_head(p_ref, h, cosf, sinf):
    sl = lambda j: slice(j * RET_W + h * HEAD_DIM, j * RET_W + (h + 1) * HEAD_DIM)
    q, k, v, g = p_ref[:, sl(0)], p_ref[:, sl(1)], p_ref[:, sl(2)], p_ref[:, sl(3)]
    return _rotary(q, cosf, sinf), _rotary(k, cosf, sinf) * QSCALE, v, g


def _ret_fwd(proj, tables, *, name):
    t = proj.shape[0]
    c = RET_CHUNK
    nc = t // c
    sp = _ret_specs(c, lambda n: n)

    def body(p_ref, cos_ref, sin_ref, dm_ref, rt_ref, cd_ref, y_ref, s_ref, state):
        @pl.when(pl.program_id(0) == 0)
        def _():
            state[...] = jnp.zeros_like(state)

        cosf, sinf = cos_ref[...], sin_ref[...]
        for h in range(HEADS):
            qr, kr, v, g = _ret_head(p_ref, h, cosf, sinf)
            s0 = state[h]
            s_ref[0, h] = s0
            scores = _bdot(qr, kr, NT) * dm_ref[h]
            o = _bdot(scores, v) + _bdot(qr * rt_ref[h, 1], s0)
            state[h] = s0 * cd_ref[h][0:1] + _bdot(kr * rt_ref[h, 0], v, TN)
            rinv = lax.rsqrt(jnp.mean(o * o, axis=-1, keepdims=True) + EPS)
            y_ref[:, h * HEAD_DIM:(h + 1) * HEAD_DIM] = (o * rinv * (g * _sigmoid(g))).astype(BF16)

    return pl.pallas_call(
        body, grid=(nc,),
        in_specs=[sp["proj"], sp["rot"], sp["rot"], sp["dmask"], sp["rowtab"], sp["cdec"]],
        out_specs=[sp["half"], sp["state"]],
        out_shape=[jax.ShapeDtypeStruct((t, 2 * RET_W), BF16), jax.ShapeDtypeStruct((nc, HEADS, HEAD_DIM, HEAD_DIM), F32)],
        scratch_shapes=[pltpu.VMEM((HEADS, HEAD_DIM, HEAD_DIM), F32)], name=name,
        compiler_params=_cp(("arbitrary",), 48),
    )(proj, *tables)


def _ret_bwd(proj, tables, states, dy, *, name):
    t = proj.shape[0]
    c = RET_CHUNK
    nc = t // c
    sp = _ret_specs(c, lambda n: nc - 1 - n)

    def body(p_ref, cos_ref, sin_ref, dm_ref, rt_ref, cd_ref, s_ref, dy_ref, dp_ref, dstate):
        @pl.when(pl.program_id(0) == 0)
        def _():
            dstate[...] = jnp.zeros_like(dstate)

        cosf, sinf = cos_ref[...], sin_ref[...]
        for h in range(HEADS):
            qr, kr, v, g = _ret_head(p_ref, h, cosf, sinf)
            s0, dm, ktl, qdc = s_ref[0, h], dm_ref[h], rt_ref[h, 0], rt_ref[h, 1]
            scores = _bdot(qr, kr, NT) * dm
            qd, kt = qr * qdc, kr * ktl
            o = _bdot(scores, v) + _bdot(qd, s0)
            rinv = lax.rsqrt(jnp.mean(o * o, axis=-1, keepdims=True) + EPS)
            oh = o * rinv
            sg = _sigmoid(g)
            dyh = dy_ref[:, h * HEAD_DIM:(h + 1) * HEAD_DIM]
            dg = dyh * oh * sg * (1.0 + g * (1.0 - sg))
            dyo = dyh * (g * sg)
            do = rinv * (dyo - oh * jnp.mean(dyo * oh, axis=-1, keepdims=True))
            ds1 = dstate[h]
            dsc = _bdot(do, v, NT) * dm
            dv = _bdot(scores, do, TN) + _bdot(kt, ds1)
            dqr = _bdot(dsc, kr) + _bdot(do, s0, NT) * qdc
            dkr = (_bdot(dsc, qr, TN) + _bdot(v, ds1, NT) * ktl) * QSCALE
            dstate[h] = ds1 * cd_ref[h][0:1] + _bdot(qd, do, TN)
            pieces = (_rotary_t(dqr, cosf, sinf), _rotary_t(dkr, cosf, sinf), dv, dg)
            for j, piece in enumerate(pieces):
                dp_ref[:, j * RET_W + h * HEAD_DIM:j * RET_W + (h + 1) * HEAD_DIM] = piece.astype(BF16)

    return pl.pallas_call(
        body, grid=(nc,),
        in_specs=[sp["proj"], sp["rot"], sp["rot"], sp["dmask"], sp["rowtab"], sp["cdec"], sp["state"], sp["half"]],
        out_specs=sp["proj"],
        out_shape=jax.ShapeDtypeStruct((t, 8 * RET_W), BF16),
        scratch_shapes=[pltpu.VMEM((HEADS, HEAD_DIM, HEAD_DIM), F32)], name=name,
        compiler_params=_cp(("arbitrary",), 48),
    )(proj, *tables, states, dy)


GDN_W = HEADS * HEAD_DIM
GDN_CONV = 3 * GDN_W
NEUMANN_STEPS = 5


def _gdn_gates(ps, al, dt):
    return _sigmoid(ps), -jnp.exp(al) * _softplus(ps + dt)


def _cumsum_rows(x):
    d = 1
    while d < x.shape[0]:
        x = x + _down_fill(x, d, 0.0)
        d *= 2
    return x


def _rev_cumsum_rows(x):
    d = 1
    while d < x.shape[0]:
        x = x + _up_fill(x, d, 0.0)
        d *= 2
    return x


class _Chunk:
    pass


def _gdn_chunk(qc, kc, v, beta, g, s0):
    c = GDN_CHUNK
    z = _Chunk()
    z.rq = lax.rsqrt(jnp.sum(qc * qc, axis=-1, keepdims=True) + EPS)
    z.rk = lax.rsqrt(jnp.sum(kc * kc, axis=-1, keepdims=True) + EPS)
    z.qn, z.k = qc * z.rq, kc * z.rk
    z.q = z.qn * QSCALE
    z.v, z.beta = v, beta
    gc = _cumsum_rows(jnp.broadcast_to(g, (c, LANES)))
    ri, ci = _rows((c, c)), _cols((c, c))
    z.tril, z.strict = ri >= ci, ri > ci
    diff = gc[:, :c] - gc.T[:c, :]
    z.decay = jnp.where(z.tril, jnp.exp(jnp.where(z.tril, diff, 0.0)), 0.0)
    z.eg = jnp.exp(gc)
    glast = gc[c - 1:c, :]
    z.egl = jnp.exp(glast - gc)
    z.cd = jnp.exp(glast)
    z.kb = z.k * beta
    z.m = _bdot(z.kb, z.k, NT)
    lmat = jnp.where(z.strict, z.m * z.decay, 0.0)
    neg = -lmat
    inv = (ri == ci).astype(F32) + neg
    pw = neg
    for _ in range(NEUMANN_STEPS):
        pw = _dot(pw, pw, precision=HI)
        inv = inv + _dot(inv, pw, precision=HI)
    z.inv = inv
    z.vb, z.kbg = v * beta, z.kb * z.eg
    z.u = _dot(inv, z.vb, precision=HI)
    z.w = _dot(inv, z.kbg, precision=HI)
    z.qk = _bdot(z.q, z.k, NT)
    z.attn = jnp.where(z.tril, z.qk * z.decay, 0.0)
    z.qd, z.kt = z.q * z.eg, z.k * z.egl
    z.vnew = z.u - _bdot(z.w, s0)
    z.o = _bdot(z.qd, s0) + _bdot(z.attn, z.vnew)
    z.s1 = s0 * z.cd + _bdot(z.kt, z.vnew, TN)
    return z


def _gdn_chunk_bwd(z, s0, do, ds1):
    c = GDN_CHUNK
    dvnew = _bdot(z.attn, do, TN) + _bdot(z.kt, ds1)
    dqd = _bdot(do, s0, NT)
    dattn = jnp.where(z.tril, _bdot(do, z.vnew, NT), 0.0)
    ds0 = _bdot(z.qd, do, TN) + ds1 * z.cd - _bdot(z.w, dvnew, TN)
    dcd = jnp.sum(jnp.sum(s0 * ds1, axis=1, keepdims=True), axis=0, keepdims=True)
    dkt = _bdot(z.vnew, ds1, NT)
    dw = -_bdot(dvnew, s0, NT)
    dvb = _dot(z.inv, dvnew, TN, precision=HI)
    dkbg = _dot(z.inv, dw, TN, precision=HI)
    dl = jnp.where(z.strict, -(_bdot(dvb, z.u, NT) + _bdot(dkbg, z.w, NT)), 0.0)
    dml = dl * z.decay
    dqk = dattn * z.decay
    ddecay = (dl * z.m + dattn * z.qk) * z.decay
    dq = _bdot(dqk, z.k) + dqd * z.eg
    dkb = _bdot(dml, z.k) + dkbg * z.eg
    dk = _bdot(dqk, z.q, TN) + _bdot(dml, z.kb, TN) + dkt * z.egl + dkb * z.beta
    dbeta = jnp.sum(dkb * z.k, axis=-1, keepdims=True) + jnp.sum(dvb * z.v, axis=-1, keepdims=True)
    dv = dvb * z.beta
    colsum = _dot(ddecay, jnp.ones((c, LANES), F32), TN, precision=HI)
    e = jnp.sum(dkt * z.kt, axis=-1, keepdims=True)
    dgc = (jnp.sum(ddecay, axis=-1, keepdims=True) - colsum
           + jnp.sum(dkbg * z.kbg, axis=-1, keepdims=True) + jnp.sum(dqd * z.qd, axis=-1, keepdims=True) - e)
    dglast = jnp.sum(e, axis=0, keepdims=True) + dcd * z.cd
    dgc = dgc + jnp.where(_rows((c, LANES)) == c - 1, dglast, 0.0)
    dg = _rev_cumsum_rows(dgc)[:, 0:1]
    dqn = dq * QSCALE
    dqc = z.rq * (dqn - z.qn * jnp.sum(dqn * z.qn, axis=-1, keepdims=True))
    dkc = z.rk * (dk - z.k * jnp.sum(dk * z.k, axis=-1, keepdims=True))
    return dqc, dkc, dv, dbeta, dg, ds0


def _gdn_specs(c, order):
    full = lambda shape: pl.BlockSpec(shape, lambda n: (0,) * len(shape))
    return dict(
        proj=pl.BlockSpec((c, 4 * GDN_W), lambda n: (order(n), 1)),
        prev=pl.BlockSpec((SUBLANES, 4 * GDN_W), lambda n: (_prev8(order(n), c), 1)),
        small=pl.BlockSpec((c, LANES), lambda n: (order(n), 0)),
        convw=full((4, GDN_CONV)), vec=full((1, LANES)),
        state=pl.BlockSpec((1, HEADS, HEAD_DIM, HEAD_DIM), lambda n: (order(n), 0, 0, 0)),
        half=pl.BlockSpec((c, GDN_W), lambda n: (order(n), 1)),
        any=pl.BlockSpec(memory_space=pl.ANY),
    )


def _gdn_fwd(proj, psmall, conv_w, al, dt, gain, y_in, *, name):
    t = proj.shape[0]
    c = GDN_CHUNK
    nc = t // c
    sp = _gdn_specs(c, lambda n: n)

    def body(p_ref, prev_ref, ps_ref, cw_ref, al_ref, dt_ref, gain_ref, yin_ref, y_ref, s_ref, state):
        n = pl.program_id(0)

        @pl.when(n == 0)
        def _():
            state[...] = jnp.zeros_like(state)

        p8 = jnp.where(n == 0, 0.0, prev_ref[:, :GDN_CONV])
        pre, _ = _conv4(p_ref[:, :GDN_CONV], p8, cw_ref[...])
        act = pre * _sigmoid(pre)
        beta_all, g_all = _gdn_gates(ps_ref[...], al_ref[...], dt_ref[...])
        for h in range(HEADS):
            sl = lambda j: slice(j * GDN_W + h * HEAD_DIM, j * GDN_W + (h + 1) * HEAD_DIM)
            s0 = state[h]
            s_ref[0, h] = s0
            z = _gdn_chunk(act[:, sl(0)], act[:, sl(1)], act[:, sl(2)], beta_all[:, h:h + 1], g_all[:, HEADS + h:HEADS + h + 1], s0)
            state[h] = z.s1
            rinv = lax.rsqrt(jnp.mean(z.o * z.o, axis=-1, keepdims=True) + EPS)
            gd = p_ref[:, sl(3)]
            y_ref[:, sl(0)] = (z.o * rinv * gain_ref[...] * (gd * _sigmoid(gd))).astype(BF16)

    return pl.pallas_call(
        body, grid=(nc,),
        in_specs=[sp["proj"], sp["prev"], sp["small"], sp["convw"], sp["vec"], sp["vec"], sp["vec"], sp["any"]],
        out_specs=[sp["half"], sp["state"]],
        out_shape=[jax.ShapeDtypeStruct((t, 2 * GDN_W), BF16), jax.ShapeDtypeStruct((nc, HEADS, HEAD_DIM, HEAD_DIM), F32)],
        scratch_shapes=[pltpu.VMEM((HEADS, HEAD_DIM, HEAD_DIM), F32)], name=name,
        input_output_aliases={7: 0}, compiler_params=_cp(("arbitrary",), 48),
    )(proj, proj, psmall, conv_w, al, dt, gain, y_in)


def _gdn_bwd(proj, psmall, conv_w, al, dt, gain, states, dy, dproj_in, *, name):
    t = proj.shape[0]
    c = GDN_CHUNK
    nc = t // c
    sp = _gdn_specs(c, lambda n: nc - 1 - n)

    def body(p_ref, prev_ref, ps_ref, cw_ref, al_ref, dt_ref, gain_ref, s_ref, dy_ref, dpin_ref,
             dp_ref, dps_ref, dcw_ref, dal_ref, ddt_ref, dgain_ref, dstate, head):
        n = pl.program_id(0)
        chunk0 = n == nc - 1

        @pl.when(n == 0)
        def _():
            for ref in (dstate, head, dcw_ref, dal_ref, ddt_ref, dgain_ref):
                ref[...] = jnp.zeros_like(ref)

        x, cwv = p_ref[:, :GDN_CONV], cw_ref[...]
        p8 = jnp.where(chunk0, 0.0, prev_ref[:, :GDN_CONV])
        pre, shifts = _conv4(x, p8, cwv)
        sg_pre = _sigmoid(pre)
        act = pre * sg_pre
        ps, alv, dtv, gain = ps_ref[...], al_ref[...], dt_ref[...], gain_ref[...]
        beta_all, g_all = _gdn_gates(ps, alv, dtv)
        lane = _cols((c, LANES))
        dbeta_all = jnp.zeros((c, LANES), F32)
        dg_all = jnp.zeros((c, LANES), F32)
        dact = [None] * (3 * HEADS)
        dgain = jnp.zeros((1, LANES), F32)
        for h in range(HEADS):
            sl = lambda j: slice(j * GDN_W + h * HEAD_DIM, j * GDN_W + (h + 1) * HEAD_DIM)
            s0 = s_ref[0, h]
            z = _gdn_chunk(act[:, sl(0)], act[:, sl(1)], act[:, sl(2)], beta_all[:, h:h + 1], g_all[:, HEADS + h:HEADS + h + 1], s0)
            rinv = lax.rsqrt(jnp.mean(z.o * z.o, axis=-1, keepdims=True) + EPS)
            oh = z.o * rinv
            gd = p_ref[:, sl(3)]
            sgd = _sigmoid(gd)
            dyh = dy_ref[:, sl(0)]
            dgain = dgain + jnp.sum(dyh * oh * (gd * sgd), axis=0, keepdims=True)
            dp_ref[:, sl(3)] = (dyh * oh * gain * sgd * (1.0 + gd * (1.0 - sgd))).astype(BF16)
            dyo = dyh * gain * (gd * sgd)
            do = rinv * (dyo - oh * jnp.mean(dyo * oh, axis=-1, keepdims=True))
            dqc, dkc, dv, dbeta, dg, ds0 = _gdn_chunk_bwd(z, s0, do, dstate[h])
            dstate[h] = ds0
            dact[h], dact[HEADS + h], dact[2 * HEADS + h] = dqc, dkc, dv
            dbeta_all = dbeta_all + jnp.where(lane == h, dbeta, 0.0)
            dg_all = dg_all + jnp.where(lane == HEADS + h, dg, 0.0)
        dpre = jnp.concatenate(dact, axis=1) * sg_pre * (1.0 + pre * (1.0 - sg_pre))
        dx, dws = _conv4_bwd(dpre, head[...], x, shifts, cwv)
        dp_ref[:, :GDN_CONV] = dx.astype(BF16)
        for k in range(4):
            dcw_ref[k:k + 1, :] += dws[k]
        head[...] = dpre[:SUBLANES]
        dsp = dg_all * (-jnp.exp(alv)) * _sigmoid(ps + dtv)
        dps_ref[...] = (dbeta_all * beta_all * (1.0 - beta_all) + dsp).astype(BF16)
        ddt_ref[...] += jnp.sum(dsp, axis=0, keepdims=True)
        dal_ref[...] += jnp.sum(dg_all * g_all, axis=0, keepdims=True)
        dgain_ref[...] += dgain

    vec_f32 = jax.ShapeDtypeStruct((1, LANES), F32)
    return pl.pallas_call(
        body, grid=(nc,),
        in_specs=[sp["proj"], sp["prev"], sp["small"], sp["convw"], sp["vec"], sp["vec"], sp["vec"], sp["state"], sp["half"], sp["any"]],
        out_specs=[sp["proj"], sp["small"], sp["convw"], sp["vec"], sp["vec"], sp["vec"]],
        out_shape=[jax.ShapeDtypeStruct((t, 8 * GDN_W), BF16), jax.ShapeDtypeStruct((t, LANES), BF16),
                   jax.ShapeDtypeStruct((4, GDN_CONV), F32), vec_f32, vec_f32, vec_f32],
        scratch_shapes=[pltpu.VMEM((HEADS, HEAD_DIM, HEAD_DIM), F32), pltpu.VMEM((SUBLANES, GDN_CONV), F32)], name=name,
        input_output_aliases={9: 0}, compiler_params=_cp(("arbitrary",), 48),
    )(proj, proj, psmall, conv_w, al, dt, gain, states, dy, dproj_in)


def _exchange(gathers, scatters, *, name):
    arrays = list(gathers) + list(scatters)
    ng, na = len(gathers), len(arrays)

    def body(*refs):
        ins, outs = refs[:na], refs[na:2 * na]
        send, recv, loc = refs[2 * na:]
        x, y, c = lax.axis_index("x"), lax.axis_index("y"), lax.axis_index("c")
        me = 4 * x + 2 * y + c
        copies = []
        for a in range(na):
            own = pltpu.make_async_copy(ins[a] if a < ng else ins[a].at[me], outs[a].at[me], loc.at[a])
            own.start()
            copies.append(own)
            for mask in range(1, N_DEV):
                px = 1 - x if mask & 4 else x
                py = 1 - y if mask & 2 else y
                pc = 1 - c if mask & 1 else c
                src = ins[a] if a < ng else ins[a].at[4 * px + 2 * py + pc]
                cp = pltpu.make_async_remote_copy(
                    src_ref=src, dst_ref=outs[a].at[me], send_sem=send.at[a, mask - 1], recv_sem=recv.at[a, mask - 1],
                    device_id=(px, py, pc), device_id_type=MESH)
                cp.start()
                copies.append(cp)
        for cp in copies:
            cp.wait()

    hbm = pl.BlockSpec(memory_space=pl.ANY)
    return pl.pallas_call(
        body, in_specs=[hbm] * na, out_specs=[hbm] * na,
        out_shape=[jax.ShapeDtypeStruct((N_DEV,) + a.shape[-2:], a.dtype) for a in arrays],
        scratch_shapes=[pltpu.SemaphoreType.DMA((na, N_DEV - 1)), pltpu.SemaphoreType.DMA((na, N_DEV - 1)),
                        pltpu.SemaphoreType.DMA((na,))],
        name=name,
    )(*arrays)


def _sum_slots(x, *, name, tr):
    _, r, l = x.shape

    def body(x_ref, o_ref):
        acc = x_ref[0].astype(F32)
        for s in range(1, N_DEV):
            acc = acc + x_ref[s].astype(F32)
        o_ref[...] = acc

    return pl.pallas_call(
        body, grid=(r // tr,), in_specs=[pl.BlockSpec((N_DEV, tr, l), lambda i: (0, i, 0))],
        out_specs=pl.BlockSpec((tr, l), lambda i: (i, 0)), out_shape=jax.ShapeDtypeStruct((r, l), F32),
        name=name, compiler_params=_cp(("parallel",), 48),
    )(x)


ADAM_TILE_ELEMS = 512 * 1024


def _adam(w, g, m, v, *, name):
    shape = w.shape
    cols = shape[-1]
    rows = math.prod(shape[:-1]) if len(shape) > 1 else 1
    tr = rows
    if rows * cols > ADAM_TILE_ELEMS:
        tr = max(d for d in range(SUBLANES, ADAM_TILE_ELEMS // cols + 1, SUBLANES) if rows % d == 0)
    c1, c2 = 1.0 - ADAM_B1 ** ADAM_STEP, 1.0 - ADAM_B2 ** ADAM_STEP

    def body(w_ref, g_ref, m_ref, v_ref, d_ref, m2_ref, v2_ref):
        gv = g_ref[...]
        m2 = ADAM_B1 * m_ref[...] + (1.0 - ADAM_B1) * gv
        v2 = ADAM_B2 * v_ref[...] + (1.0 - ADAM_B2) * (gv * gv)
        d_ref[...] = -ADAM_LR * ((m2 / c1) / (jnp.sqrt(v2 / c2) + ADAM_EPS) + ADAM_WD * w_ref[...])
        m2_ref[...] = m2
        v2_ref[...] = v2

    spec = pl.BlockSpec((tr, cols), lambda i: (i, 0))
    outs = pl.pallas_call(
        body, grid=(rows // tr,), in_specs=[spec] * 4, out_specs=[spec] * 3,
        out_shape=[jax.ShapeDtypeStruct((rows, cols), F32)] * 3, name=name, compiler_params=_cp(("parallel",), 48),
    )(*(a.reshape(rows, cols) for a in (w, g, m, v)))
    return tuple(o.reshape(shape) for o in outs)


WEIGHTS = ['norm_mix', 'norm_ffn', 'ret_gdn_w_in', 'gdn_conv_w', 'gdn_a_log', 'gdn_dt_bias', 'gdn_out_gain', 'ret_gdn_w_out',
           'lru_w_in', 'lru_conv_w', 'lru_conv_b', 'lru_w_a', 'lru_b_a', 'lru_w_x', 'lru_b_x', 'lru_lambda', 'lru_w_out',
           'ffn_w_up', 'ffn_conv_w', 'ffn_conv_b', 'ffn_w_down', 'norm_final']
BIG = {'ret_gdn_w_in': ((1, 1024, 513), 2), 'ret_gdn_w_out': ((1, 128, 1024), 1), 'lru_w_in': ((1, 1024, 256), 2),
       'lru_w_out': ((1, 128, 1024), 1), 'ffn_w_up': ((2, 1024, 704), 2), 'ffn_w_down': ((2, 352, 1024), 1)}
SMALL = {'gdn_conv_w': ((1, 4, 192), 2), 'lru_conv_w': ((1, 4, 128), 2), 'lru_conv_b': ((1, 128), 1), 'lru_b_a': ((1, 128), 1),
         'lru_b_x': ((1, 128), 1), 'lru_lambda': ((1, 128), 1), 'ffn_conv_w': ((2, 3, 704), 2)}
REPLICATED = {'norm_mix': (2, 1024), 'norm_ffn': (2, 1024), 'gdn_a_log': (1, 4), 'gdn_dt_bias': (1, 4), 'gdn_out_gain': (1, 128),
              'lru_w_a': (1, 8, 128, 128), 'lru_w_x': (1, 8, 128, 128), 'ffn_conv_b': (2, 5632), 'norm_final': (1024,)}
BIG_ROWS = 25600
BIG_SUM_TILE = 1024


def _full_shape(shard, axis):
    return tuple(d * N_DEV if i == axis else d for i, d in enumerate(shard))


def _rows_of(n_elems):
    return -(-n_elems // LANES)


def _to_rows(a, lead=()):
    flat = a.reshape(lead + (-1,))
    pad = _rows_of(flat.shape[-1]) * LANES - flat.shape[-1]
    if pad:
        flat = jnp.pad(flat, [(0, 0)] * len(lead) + [(0, pad)])
    return flat.reshape(lead + (-1, LANES))


def _pack(pieces, total_rows, lead=()):
    buf = jnp.concatenate(pieces, axis=len(lead))
    pad = total_rows - buf.shape[len(lead)]
    return jnp.pad(buf, [(0, 0)] * len(lead) + [(0, pad), (0, 0)]) if pad else buf


def _unpack(buf, shapes, lead=()):
    out, off = [], 0
    for shape in shapes:
        n = math.prod(shape)
        rows = _rows_of(n)
        piece = lax.slice_in_dim(buf, off, off + rows, axis=len(lead)).reshape(lead + (rows * LANES,))
        out.append(lax.slice_in_dim(piece, 0, n, axis=len(lead)).reshape(lead + shape))
        off += rows
    return out


def _join_blocks(g, axis):
    m = jnp.moveaxis(g, 0, axis)
    return m.reshape(m.shape[:axis] + (N_DEV * m.shape[axis + 1],) + m.shape[axis + 2:])


def _split_blocks(full, axis):
    s = full.shape
    return jnp.moveaxis(full.reshape(s[:axis] + (N_DEV, s[axis] // N_DEV) + s[axis + 1:]), axis, 0)


def _small_rows(shapes):
    total = sum(_rows_of(math.prod(s)) for s in shapes)
    return -(-total // SUBLANES) * SUBLANES


def _ffn_forward(h, gain, wug, wuv, cwg, cwv, cbg, cbv, wd, tag):
    hn = _norm_fwd(h, gain, name=f"ffn{tag}_norm")
    ug = _mm(hn, wug, name=f"ffn{tag}_up_gate", tn=1408)
    uv = _mm(hn, wuv, name=f"ffn{tag}_up_val", tn=1408)
    act = _ffn_act_fwd(ug, uv, cwg, cwv, cbg, cbv, name=f"ffn{tag}_act")
    out = _mm(act, wd, res=h, name=f"ffn{tag}_down", tk=1408)
    return out, (hn, ug, uv, act)


def _ffn_backward(dh, h, gain, saved, wug, wuv, cwg, cwv, cbg, cbv, wd, tag):
    hn, ug, uv, act = saved
    da = _mm(dh, wd, tb=True, name=f"ffn{tag}_d_act", tn=1408)
    dwd = _mm(act, dh, ta=True, out_dtype=BF16, name=f"ffn{tag}_d_wdown", tm=1408)
    dug, duv, dcwg, dcwv, dcbg, dcbv = _ffn_act_bwd(ug, uv, da, cwg, cwv, cbg, cbv, name=f"ffn{tag}_act_bwd")
    dhn = _mm(dug, wug, tb=True, name=f"ffn{tag}_d_hn_gate", tk=1408)
    dhn = _mm(duv, wuv, tb=True, res=dhn, name=f"ffn{tag}_d_hn_val", tk=1408)
    dwu = jnp.concatenate([_mm(hn, dug, ta=True, out_dtype=BF16, name=f"ffn{tag}_d_wup_gate", tn=1408),
                           _mm(hn, duv, ta=True, out_dtype=BF16, name=f"ffn{tag}_d_wup_val", tn=1408)], axis=1)
    dh_in, dgain = _norm_bwd(h, gain, dhn, dh, name=f"ffn{tag}_norm_bwd")
    grads = dict(w_up=dwu, w_down=dwd, conv_w=jnp.concatenate([dcwg, dcwv], axis=1), conv_b=jnp.concatenate([dcbg, dcbv], axis=1),
                 norm=dgain)
    return dh_in, grads


def kernel(x, norm_mix, norm_ffn, ret_gdn_w_in, gdn_conv_w, gdn_a_log, gdn_dt_bias, gdn_out_gain, ret_gdn_w_out, lru_w_in, lru_conv_w, lru_conv_b, lru_w_a, lru_b_a, lru_w_x, lru_b_x, lru_lambda, lru_w_out, ffn_w_up, ffn_conv_w, ffn_conv_b, ffn_w_down, norm_final, loss_target, m_norm_mix, m_norm_ffn, m_ret_gdn_w_in, m_gdn_conv_w, m_gdn_a_log, m_gdn_dt_bias, m_gdn_out_gain, m_ret_gdn_w_out, m_lru_w_in, m_lru_conv_w, m_lru_conv_b, m_lru_w_a, m_lru_b_a, m_lru_w_x, m_lru_b_x, m_lru_lambda, m_lru_w_out, m_ffn_w_up, m_ffn_conv_w, m_ffn_conv_b, m_ffn_w_down, m_norm_final, v_norm_mix, v_norm_ffn, v_ret_gdn_w_in, v_gdn_conv_w, v_gdn_a_log, v_gdn_dt_bias, v_gdn_out_gain, v_ret_gdn_w_out, v_lru_w_in, v_lru_conv_w, v_lru_conv_b, v_lru_w_a, v_lru_b_a, v_lru_w_x, v_lru_b_x, v_lru_lambda, v_lru_w_out, v_ffn_w_up, v_ffn_conv_w, v_ffn_conv_b, v_ffn_w_down, v_norm_final):
    given = dict(locals())
    w = {n: given[n] for n in WEIGHTS}
    me = 4 * lax.axis_index("x") + 2 * lax.axis_index("y") + lax.axis_index("c")
    t = x.shape[1]
    f = D_FF

    big_buf = _pack([_to_rows(w[n].astype(BF16)) for n in BIG], BIG_ROWS)
    small_shapes = [s for s, _ in SMALL.values()]
    small_buf = _pack([_to_rows(w[n]) for n in SMALL], _small_rows(small_shapes))
    g_big, g_small = _exchange([big_buf, small_buf], [], name="gather_params")
    full = {}
    for n, blocks in zip(BIG, _unpack(g_big, [s for s, _ in BIG.values()], lead=(N_DEV,))):
        full[n] = _join_blocks(blocks, BIG[n][1])
    for n, blocks in zip(SMALL, _unpack(g_small, small_shapes, lead=(N_DEV,))):
        full[n] = _join_blocks(blocks, SMALL[n][1])

    w_in0 = full['ret_gdn_w_in'][0]
    w_main = w_in0[:, :MAIN_IN]
    w_narrow = jnp.pad(w_in0[:, MAIN_IN:], ((0, 0), (0, LANES - SMALL_IN)))
    w_out0 = full['ret_gdn_w_out'][0]
    lru_in_g, lru_in_x = full['lru_w_in'][0][:, :D_MODEL], full['lru_w_in'][0][:, D_MODEL:]
    lru_out = full['lru_w_out'][0]
    up_g = [full['ffn_w_up'][l][:, :f] for l in range(2)]
    up_v = [full['ffn_w_up'][l][:, f:] for l in range(2)]
    down = [full['ffn_w_down'][l] for l in range(2)]
    fcw = full['ffn_conv_w']
    fcw_g, fcw_v = [fcw[l][:, :f] for l in range(2)], [fcw[l][:, f:] for l in range(2)]
    fcb_g, fcb_v = [ffn_conv_b[l:l + 1, :f] for l in range(2)], [ffn_conv_b[l:l + 1, f:] for l in range(2)]
    gdn_cw = full['gdn_conv_w'][0]
    al_pad = jnp.pad(gdn_a_log, ((0, 0), (HEADS, LANES - 2 * HEADS)))
    dt_pad = jnp.pad(gdn_dt_bias, ((0, 0), (HEADS, LANES - 2 * HEADS)))
    lru_cw, lru_cb = full['lru_conv_w'][0], full['lru_conv_b']
    lru_ba, lru_bx, lru_lam = full['lru_b_a'], full['lru_b_x'], full['lru_lambda']
    wa, wx = lru_w_a[0], lru_w_x[0]

    h0, target = x[0], loss_target[0]
    hn0 = _norm_fwd(h0, norm_mix[0:1], name="mix0_norm")
    proj = _mm(hn0, w_main, name="mix0_in")
    pnarrow = _mm(hn0, w_narrow, name="mix0_in_narrow")
    tables = _ret_tables(t)
    y0, ret_states = _ret_fwd(proj, tables, name="retention_fwd")
    y0, gdn_states = _gdn_fwd(proj, pnarrow, gdn_cw, al_pad, dt_pad, gdn_out_gain, y0, name="deltanet_fwd")
    h1 = _mm(y0, w_out0, res=h0, name="mix0_out")
    h2, ffn0_saved = _ffn_forward(h1, norm_ffn[0:1], up_g[0], up_v[0], fcw_g[0], fcw_v[0], fcb_g[0], fcb_v[0], down[0], 0)
    hn1 = _norm_fwd(h2, norm_mix[1:2], name="mix1_norm")
    gate = _mm(hn1, lru_in_g, name="mix1_in_gate")
    xpre = _mm(hn1, lru_in_x, name="mix1_in_x")
    y1, hs = _lru_fwd(gate, xpre, lru_cw, lru_cb, wa, lru_ba, wx, lru_bx, lru_lam, name="rglru_fwd")
    h3 = _mm(y1, lru_out, res=h2, name="mix1_out")
    h4, ffn1_saved = _ffn_forward(h3, norm_ffn[1:2], up_g[1], up_v[1], fcw_g[1], fcw_v[1], fcb_g[1], fcb_v[1], down[1], 1)
    dh4, d_norm_final, loss_part = _final_loss(h4, norm_final[None, :], target, name="final_norm_loss")
    loss = lax.psum(loss_part[0, 0], ("x", "y", "c"))

    dh3, gf1 = _ffn_backward(dh4, h3, norm_ffn[1:2], ffn1_saved, up_g[1], up_v[1], fcw_g[1], fcw_v[1], fcb_g[1], fcb_v[1], down[1], 1)
    dy1 = _mm(dh3, lru_out, tb=True, name="mix1_d_y")
    d_lru_out = _mm(y1, dh3, ta=True, out_dtype=BF16, name="mix1_d_wout")
    dgate, dxpre, d_lcw, d_lcb, d_wa, d_ba, d_wx, d_bx, d_lam = _lru_bwd(
        gate, xpre, hs, dy1, lru_cw, lru_cb, wa, lru_ba, wx, lru_bx, lru_lam, name="rglru_bwd")
    dhn1 = _mm(dgate, lru_in_g, tb=True, name="mix1_d_hn_gate")
    dhn1 = _mm(dxpre, lru_in_x, tb=True, res=dhn1, name="mix1_d_hn_x")
    d_lru_in = jnp.concatenate([_mm(hn1, dgate, ta=True, out_dtype=BF16, name="mix1_d_win_gate"),
                                _mm(hn1, dxpre, ta=True, out_dtype=BF16, name="mix1_d_win_x")], axis=1)
    dh2, d_mix1 = _norm_bwd(h2, norm_mix[1:2], dhn1, dh3, name="mix1_norm_bwd")
    dh1, gf0 = _ffn_backward(dh2, h1, norm_ffn[0:1], ffn0_saved, up_g[0], up_v[0], fcw_g[0], fcw_v[0], fcb_g[0], fcb_v[0], down[0], 0)
    dy0 = _mm(dh1, w_out0, tb=True, name="mix0_d_y")
    d_w_out0 = _mm(y0, dh1, ta=True, out_dtype=BF16, name="mix0_d_wout")
    dproj = _ret_bwd(proj, tables, ret_states, dy0, name="retention_bwd")
    dproj, dnarrow, d_gcw, d_alog, d_dtb, d_gain = _gdn_bwd(
        proj, pnarrow, gdn_cw, al_pad, dt_pad, gdn_out_gain, gdn_states, dy0, dproj, name="deltanet_bwd")
    dhn0 = _mm(dproj, w_main, tb=True, name="mix0_d_hn")
    dhn0 = _mm(dnarrow, w_narrow, tb=True, res=dhn0, name="mix0_d_hn_narrow")
    d_w_main = _mm(hn0, dproj, ta=True, out_dtype=BF16, name="mix0_d_win")
    d_w_narrow = _mm(hn0, dnarrow, ta=True, out_dtype=BF16, name="mix0_d_win_narrow")
    dx, d_mix0 = _norm_bwd(h0, norm_mix[0:1], dhn0, dh1, name="mix0_norm_bwd")

    big_grads = {
        'ret_gdn_w_in': jnp.concatenate([d_w_main, d_w_narrow[:, :SMALL_IN]], axis=1)[None],
        'ret_gdn_w_out': d_w_out0[None], 'lru_w_in': d_lru_in[None], 'lru_w_out': d_lru_out[None],
        'ffn_w_up': jnp.stack([gf0['w_up'], gf1['w_up']]), 'ffn_w_down': jnp.stack([gf0['w_down'], gf1['w_down']]),
    }
    small_grads = {
        'gdn_conv_w': d_gcw[None], 'lru_conv_w': d_lcw[None], 'lru_conv_b': d_lcb, 'lru_b_a': d_ba, 'lru_b_x': d_bx,
        'lru_lambda': d_lam, 'ffn_conv_w': jnp.stack([gf0['conv_w'], gf1['conv_w']]),
        'norm_mix': jnp.concatenate([d_mix0, d_mix1], axis=0), 'norm_ffn': jnp.concatenate([gf0['norm'], gf1['norm']], axis=0),
        'gdn_a_log': d_alog[:, HEADS:2 * HEADS], 'gdn_dt_bias': d_dtb[:, HEADS:2 * HEADS], 'gdn_out_gain': d_gain,
        'lru_w_a': d_wa[None], 'lru_w_x': d_wx[None], 'ffn_conv_b': jnp.concatenate([gf0['conv_b'], gf1['conv_b']], axis=0),
        'norm_final': d_norm_final[0],
    }
    scatter_buf = _pack([_to_rows(_split_blocks(big_grads[n], BIG[n][1]), lead=(N_DEV,)) for n in BIG], BIG_ROWS, lead=(N_DEV,))
    small_full_shapes = [_full_shape(*SMALL[n]) for n in SMALL] + list(REPLICATED.values())
    partial_buf = _pack([_to_rows(small_grads[n]) for n in list(SMALL) + list(REPLICATED)], _small_rows(small_full_shapes))
    got_partial, got_big = _exchange([partial_buf], [scatter_buf], name="exchange_grads")
    sum_big = _sum_slots(got_big, name="sum_grad_blocks", tr=BIG_SUM_TILE)
    sum_small = _sum_slots(got_partial, name="sum_grad_partials", tr=got_partial.shape[1])

    grads = dict(zip(BIG, _unpack(sum_big, [s for s, _ in BIG.values()])))
    for n, g_full in zip(list(SMALL) + list(REPLICATED), _unpack(sum_small, small_full_shapes)):
        if n in SMALL:
            shard, axis = SMALL[n]
            g_full = lax.dynamic_slice_in_dim(g_full, me * shard[axis], shard[axis], axis=axis)
        grads[n] = g_full

    delta, new_m, new_v = {}, {}, {}
    for n in WEIGHTS:
        delta[n], new_m[n], new_v[n] = _adam(w[n], grads[n], given["m_" + n], given["v_" + n], name=f"adamw_{n}")
    return (loss, dx[None], *[grads[n] for n in WEIGHTS], *[delta[n] for n in WEIGHTS],
            *[new_m[n] for n in WEIGHTS], *[new_v[n] for n in WEIGHTS])
```

```python
import functools
import math

import numpy as np
import jax
import jax.numpy as jnp
from jax import lax
from jax.experimental import pallas as pl
from jax.experimental.pallas import tpu as pltpu

F32 = jnp.float32
BF16 = jnp.bfloat16
HI = lax.Precision.HIGHEST
MESH = pl.DeviceIdType.MESH

N_DEV = 8
LANES = 128
SUBLANES = 8
EPS = 1e-6
D_MODEL = 1024
HEADS = 4
HEAD_DIM = 128
RET_CHUNK = 128
GDN_CHUNK = 64
ROPE_BASE = 10000.0
LRU_C = 8.0
D_FF = 2816
MAIN_IN = 4096
SMALL_IN = 8
QSCALE = HEAD_DIM ** -0.5

ADAM_LR, ADAM_B1, ADAM_B2, ADAM_EPS, ADAM_WD, ADAM_STEP = 0.001, 0.9, 0.999, 1e-08, 0.01, 10


def _cp(sem=None, vmem_mb=None):
    kw = {}
    if sem is not None:
        kw["dimension_semantics"] = sem
    if vmem_mb is not None:
        kw["vmem_limit_bytes"] = vmem_mb << 20
    return pltpu.CompilerParams(**kw)


def _rows(shape):
    return lax.broadcasted_iota(jnp.int32, shape, 0)


def _cols(shape):
    return lax.broadcasted_iota(jnp.int32, shape, 1)


def _shift_down(cur, prev8, s):
    if s == 0:
        return cur
    rc = pltpu.roll(cur, s, 0)
    rp = pltpu.roll(prev8, s, 0)
    top = jnp.where(_rows(prev8.shape) < s, rp, rc[:SUBLANES])
    return jnp.concatenate([top, rc[SUBLANES:]], axis=0)


def _shift_up(cur, next8, s):
    if s == 0:
        return cur
    tt = cur.shape[0]
    rc = pltpu.roll(cur, tt - s, 0)
    rn = pltpu.roll(next8, SUBLANES - s, 0)
    bot = jnp.where(_rows(next8.shape) >= SUBLANES - s, rn, rc[tt - SUBLANES:])
    return jnp.concatenate([rc[:tt - SUBLANES], bot], axis=0)


def _down_fill(x, d, fill):
    return jnp.where(_rows(x.shape) < d, fill, pltpu.roll(x, d, 0))


def _up_fill(x, d, fill):
    tt = x.shape[0]
    return jnp.where(_rows(x.shape) >= tt - d, fill, pltpu.roll(x, tt - d, 0))


def _sigmoid(x):
    return 1.0 / (1.0 + jnp.exp(-x))


def _softplus(x):
    return jnp.maximum(x, 0.0) + jnp.log(1.0 + jnp.exp(-jnp.abs(x)))


def _dot(a, b, dims=(((1,), (0,)), ((), ())), precision=None):
    return lax.dot_general(a, b, dims, preferred_element_type=F32, precision=precision)


NN = (((1,), (0,)), ((), ()))
NT = (((1,), (1,)), ((), ()))
TN = (((0,), (0,)), ((), ()))


def _bdot(a, b, dims=NN):
    return _dot(a.astype(BF16), b.astype(BF16), dims)


def _split(a):
    hi = a.astype(BF16)
    return hi, (a - hi.astype(F32)).astype(BF16)


def _dot3(a, b, dims=NN):
    ah, al = _split(a)
    bh, bl = _split(b)
    return _dot(ah, bh, dims) + (_dot(ah, bl, dims) + _dot(al, bh, dims))


def _tile(dim, target):
    if dim <= target:
        return dim
    best = None
    for c in range(LANES, target + 1, LANES):
        if dim % c == 0:
            best = c
    assert best is not None, (dim, target)
    return best


def _mm(a, b, *, name, ta=False, tb=False, out_dtype=F32, res=None, tm=2048, tn=512, tk=1024):
    m, k = (a.shape[1], a.shape[0]) if ta else a.shape
    n = b.shape[0] if tb else b.shape[1]
    tn, tk = _tile(n, tn), _tile(k, tk)
    tm = _tile(m, tm if max(tn, tk) <= 1024 else tm // 2)
    nk = k // tk
    dims = (((0 if ta else 1,), (1 if tb else 0,)), ((), ()))

    def body(*refs):
        a_ref, b_ref = refs[:2]
        r_ref = refs[2] if res is not None else None
        o_ref = refs[3] if res is not None else refs[2]
        acc = refs[-1]
        kk = pl.program_id(2)
        part = _bdot(a_ref[...], b_ref[...], dims)

        def finish(r):
            if res is not None:
                r = r + r_ref[...]
            o_ref[...] = r.astype(out_dtype)

        if nk == 1:
            finish(part)
            return

        @pl.when(kk == 0)
        def _():
            acc[...] = part

        @pl.when(jnp.logical_and(kk > 0, kk < nk - 1))
        def _():
            acc[...] += part

        @pl.when(kk == nk - 1)
        def _():
            finish(acc[...] + part)

    a_spec = pl.BlockSpec((tk, tm), lambda i, j, kk: (kk, i)) if ta else pl.BlockSpec((tm, tk), lambda i, j, kk: (i, kk))
    b_spec = pl.BlockSpec((tn, tk), lambda i, j, kk: (j, kk)) if tb else pl.BlockSpec((tk, tn), lambda i, j, kk: (kk, j))
    o_spec = pl.BlockSpec((tm, tn), lambda i, j, kk: (i, j))
    in_specs = [a_spec, b_spec] + ([o_spec] if res is not None else [])
    args = (a, b) + ((res,) if res is not None else ())
    return pl.pallas_call(
        body, grid=(m // tm, n // tn, nk), in_specs=in_specs, out_specs=o_spec,
        out_shape=jax.ShapeDtypeStruct((m, n), out_dtype),
        scratch_shapes=[pltpu.VMEM((tm, tn), F32)] if nk > 1 else [], name=name,
        compiler_params=_cp(("parallel", "parallel", "arbitrary"), 56),
    )(*args)


def _norm_fwd(h, gain, *, name, tt=256):
    t, d = h.shape
    tt = min(tt, t)

    def body(h_ref, g_ref, o_ref):
        x = h_ref[...]
        r = lax.rsqrt(jnp.mean(x * x, axis=-1, keepdims=True) + EPS)
        o_ref[...] = (x * r * g_ref[...]).astype(BF16)

    row = pl.BlockSpec((tt, d), lambda i: (i, 0))
    return pl.pallas_call(
        body, grid=(t // tt,), in_specs=[row, pl.BlockSpec((1, d), lambda i: (0, 0))], out_specs=row,
        out_shape=jax.ShapeDtypeStruct((t, d), BF16), name=name, compiler_params=_cp(("parallel",)),
    )(h, gain)


def _norm_bwd(h, gain, dhn, dres, *, name, tt=256):
    t, d = h.shape
    tt = min(tt, t)

    def body(h_ref, g_ref, dy_ref, dr_ref, dx_ref, dg_ref):
        x, dy = h_ref[...], dy_ref[...]
        r = lax.rsqrt(jnp.mean(x * x, axis=-1, keepdims=True) + EPS)
        xh = x * r

        @pl.when(pl.program_id(0) == 0)
        def _():
            dg_ref[...] = jnp.zeros_like(dg_ref)

        dg_ref[...] += jnp.sum(dy * xh, axis=0, keepdims=True)
        dxh = dy * g_ref[...]
        dx_ref[...] = dr_ref[...] + r * (dxh - xh * jnp.mean(dxh * xh, axis=-1, keepdims=True))

    row = pl.BlockSpec((tt, d), lambda i: (i, 0))
    vec = pl.BlockSpec((1, d), lambda i: (0, 0))
    return pl.pallas_call(
        body, grid=(t // tt,), in_specs=[row, vec, row, row], out_specs=[row, vec],
        out_shape=[jax.ShapeDtypeStruct((t, d), F32), jax.ShapeDtypeStruct((1, d), F32)],
        name=name, compiler_params=_cp(("arbitrary",)),
    )(h, gain, dhn, dres)


def _final_loss(h, gain, target, *, name, tt=256):
    t, d = h.shape
    tt = min(tt, t)

    def body(h_ref, g_ref, tg_ref, dx_ref, dg_ref, loss_ref):
        x = h_ref[...]
        r = lax.rsqrt(jnp.mean(x * x, axis=-1, keepdims=True) + EPS)
        xh = x * r
        err = xh * g_ref[...] - tg_ref[...]

        @pl.when(pl.program_id(0) == 0)
        def _():
            dg_ref[...] = jnp.zeros_like(dg_ref)
            loss_ref[...] = jnp.zeros_like(loss_ref)

        loss_ref[...] += 0.5 * jnp.sum(jnp.mean(err * err, axis=-1, keepdims=True), axis=0, keepdims=True)
        dy = err * (1.0 / d)
        dg_ref[...] += jnp.sum(dy * xh, axis=0, keepdims=True)
        dxh = dy * g_ref[...]
        dx_ref[...] = r * (dxh - xh * jnp.mean(dxh * xh, axis=-1, keepdims=True))

    row = pl.BlockSpec((tt, d), lambda i: (i, 0))
    vec = pl.BlockSpec((1, d), lambda i: (0, 0))
    return pl.pallas_call(
        body, grid=(t // tt,), in_specs=[row, vec, row],
        out_specs=[row, vec, pl.BlockSpec((1, 1), lambda i: (0, 0))],
        out_shape=[jax.ShapeDtypeStruct((t, d), F32), jax.ShapeDtypeStruct((1, d), F32), jax.ShapeDtypeStruct((1, 1), F32)],
        name=name, compiler_params=_cp(("arbitrary",)),
    )(h, gain, target)


FFN_CT = 1408
FFN_TT = 256


def _prev8(n, tt):
    return jnp.maximum(n * (tt // SUBLANES) - 1, 0)


def _ffn_conv(cur, prev8, w, b):
    s1 = _shift_down(cur, prev8, 1)
    s2 = _shift_down(cur, prev8, 2)
    return w[0:1] * s2 + w[1:2] * s1 + w[2:3] * cur + b, s1, s2


def _ffn_act_fwd(ug, uv, wg, wv, bg, bv, *, name):
    t = ug.shape[0]
    tt, ct = min(FFN_TT, t), FFN_CT
    nj = D_FF // ct

    def body(ug_ref, uv_ref, pg, pv, wg_ref, wv_ref, bg_ref, bv_ref, o_ref):
        first = pl.program_id(1) == 0
        pgv = jnp.where(first, 0.0, pg[...])
        pvv = jnp.where(first, 0.0, pv[...])
        gate, _, _ = _ffn_conv(ug_ref[...], pgv, wg_ref[...], bg_ref[...])
        val, _, _ = _ffn_conv(uv_ref[...], pvv, wv_ref[...], bv_ref[...])
        o_ref[...] = (gate * _sigmoid(gate) * val).astype(BF16)

    cur = pl.BlockSpec((tt, ct), lambda j, n: (n, j))
    prev = pl.BlockSpec((SUBLANES, ct), lambda j, n: (_prev8(n, tt), j))
    wsp = lambda rows: pl.BlockSpec((rows, ct), lambda j, n: (0, j))
    return pl.pallas_call(
        body, grid=(nj, t // tt),
        in_specs=[cur, cur, prev, prev, wsp(3), wsp(3), wsp(1), wsp(1)], out_specs=cur,
        out_shape=jax.ShapeDtypeStruct((t, D_FF), BF16), name=name,
        compiler_params=_cp(("parallel", "arbitrary"), 48),
    )(ug, uv, ug, uv, wg, wv, bg, bv)


def _ffn_act_bwd(ug, uv, da, wg, wv, bg, bv, *, name):
    t = ug.shape[0]
    tt, ct = min(FFN_TT, t), FFN_CT
    nj = D_FF // ct
    nt = t // tt

    def body(ug_ref, uv_ref, pg, pv, da_ref, wg_ref, wv_ref, bg_ref, bv_ref,
             dug, duv, dwg, dwv, dbg, dbv, head_g, head_v):
        n = pl.program_id(1)
        tile0 = n == nt - 1

        @pl.when(n == 0)
        def _():
            for r in (head_g, head_v, dwg, dwv, dbg, dbv):
                r[...] = jnp.zeros_like(r)

        xg, xv = ug_ref[...], uv_ref[...]
        pgv = jnp.where(tile0, 0.0, pg[...])
        pvv = jnp.where(tile0, 0.0, pv[...])
        gate, g1, g2 = _ffn_conv(xg, pgv, wg_ref[...], bg_ref[...])
        val, v1, v2 = _ffn_conv(xv, pvv, wv_ref[...], bv_ref[...])
        d = da_ref[...]
        sg = _sigmoid(gate)
        dval = d * gate * sg
        dgate = d * val * sg * (1.0 + gate * (1.0 - sg))
        for dc, w_ref, x0, x1, x2, head, du, dw, db in ((dgate, wg_ref, xg, g1, g2, head_g, dug, dwg, dbg),
                                                        (dval, wv_ref, xv, v1, v2, head_v, duv, dwv, dbv)):
            w = w_ref[...]
            hd = head[...]
            du[...] = (w[2:3] * dc + w[1:2] * _shift_up(dc, hd, 1) + w[0:1] * _shift_up(dc, hd, 2)).astype(BF16)
            dw[0:1, :] += jnp.sum(dc * x2, axis=0, keepdims=True)
            dw[1:2, :] += jnp.sum(dc * x1, axis=0, keepdims=True)
            dw[2:3, :] += jnp.sum(dc * x0, axis=0, keepdims=True)
            db[...] += jnp.sum(dc, axis=0, keepdims=True)
            head[...] = dc[:SUBLANES]

    rev = lambda n: nt - 1 - n
    cur = pl.BlockSpec((tt, ct), lambda j, n: (rev(n), j))
    prev = pl.BlockSpec((SUBLANES, ct), lambda j, n: (_prev8(rev(n), tt), j))
    wsp = lambda rows: pl.BlockSpec((rows, ct), lambda j, n: (0, j))
    return pl.pallas_call(
        body, grid=(nj, nt),
        in_specs=[cur, cur, prev, prev, cur, wsp(3), wsp(3), wsp(1), wsp(1)],
        out_specs=[cur, cur, wsp(3), wsp(3), wsp(1), wsp(1)],
        out_shape=[jax.ShapeDtypeStruct((t, D_FF), BF16)] * 2 + [jax.ShapeDtypeStruct((3, D_FF), F32)] * 2
        + [jax.ShapeDtypeStruct((1, D_FF), F32)] * 2,
        scratch_shapes=[pltpu.VMEM((SUBLANES, ct), F32)] * 2, name=name,
        compiler_params=_cp(("parallel", "arbitrary"), 48),
    )(ug, uv, ug, uv, da, wg, wv, bg, bv)


LRU_TT = 256
LRU_CT = 512
GELU_C = math.sqrt(2.0 / math.pi)
GELU_A = 0.044715


def _gelu(x):
    return 0.5 * x * (1.0 + jnp.tanh(GELU_C * (x + GELU_A * x * x * x)))


def _gelu_grad(x):
    th = jnp.tanh(GELU_C * (x + GELU_A * x * x * x))
    return 0.5 * (1.0 + th) + 0.5 * x * (1.0 - th * th) * GELU_C * (1.0 + 3.0 * GELU_A * x * x)


def _neg_expm1(x):
    poly = -x * (1.0 + x * (0.5 + x * (1.0 / 6 + x * (1.0 / 24 + x * (1.0 / 120)))))
    return jnp.where(x > -0.1, poly, 1.0 - jnp.exp(x))


def _conv4(x, p8, w, b=None):
    s1, s2, s3 = _shift_down(x, p8, 1), _shift_down(x, p8, 2), _shift_down(x, p8, 3)
    y = w[0:1] * s3 + w[1:2] * s2 + w[2:3] * s1 + w[3:4] * x
    return (y if b is None else y + b), (s1, s2, s3)


def _conv4_bwd(dy, head, x, shifts, w):
    s1, s2, s3 = shifts
    dx = w[3:4] * dy + w[2:3] * _shift_up(dy, head, 1) + w[1:2] * _shift_up(dy, head, 2) + w[0:1] * _shift_up(dy, head, 3)
    dws = [jnp.sum(dy * s, axis=0, keepdims=True) for s in (s3, s2, s1, x)]
    return dx, dws


def _blockdiag(x, w_ref, dims=NN):
    nb = x.shape[1] // LANES
    return jnp.concatenate([_bdot(x[:, LANES * i:LANES * (i + 1)], w_ref[i], dims) for i in range(nb)], axis=1)


def _lru_gates(xr, wa_ref, wx_ref, ba, bx, lam):
    r = _sigmoid(_blockdiag(xr, wa_ref) + ba)
    i = _sigmoid(_blockdiag(xr, wx_ref) + bx)
    sp = _softplus(-lam)
    la = -LRU_C * r * sp
    a = jnp.exp(la)
    mult = jnp.sqrt(_neg_expm1(2.0 * la))
    return r, i, sp, a, mult


def _lru_specs(t, tt, ct, order):
    nb = ct // LANES
    cur = pl.BlockSpec((tt, ct), lambda j, n: (order(n), j))
    prev = pl.BlockSpec((SUBLANES, ct), lambda j, n: (_prev8(order(n), tt), j))
    vec = lambda rows: pl.BlockSpec((rows, ct), lambda j, n: (0, j))
    blk = pl.BlockSpec((nb, LANES, LANES), lambda j, n: (j, 0, 0))
    return cur, prev, vec, blk


def _lru_fwd(gate, xpre, cw, cb, wa, ba, wx, bx, lam, *, name):
    t, c = gate.shape
    tt, ct = min(LRU_TT, t), LRU_CT
    cur, prev, vec, blk = _lru_specs(t, tt, ct, lambda n: n)

    def body(gate_ref, x_ref, p_ref, cw_ref, cb_ref, wa_ref, ba_ref, wx_ref, bx_ref, lam_ref, y_ref, hs_ref, carry):
        n = pl.program_id(1)

        @pl.when(n == 0)
        def _():
            carry[...] = jnp.zeros_like(carry)

        p8 = jnp.where(n == 0, 0.0, p_ref[...])
        xr, _ = _conv4(x_ref[...], p8, cw_ref[...], cb_ref[...])
        r, i, sp, a, mult = _lru_gates(xr, wa_ref, wx_ref, ba_ref[...], bx_ref[...], lam_ref[...])
        acc_a, acc_b = a, mult * (i * xr)
        d = 1
        while d < tt:
            acc_b = acc_a * _down_fill(acc_b, d, 0.0) + acc_b
            acc_a = acc_a * _down_fill(acc_a, d, 1.0)
            d *= 2
        hs = acc_b + acc_a * carry[0:1]
        carry[...] = jnp.broadcast_to(hs[tt - 1:tt], carry.shape)
        hs_ref[...] = hs
        y_ref[...] = (_gelu(gate_ref[...]) * hs).astype(BF16)

    return pl.pallas_call(
        body, grid=(c // ct, t // tt),
        in_specs=[cur, cur, prev, vec(4), vec(1), blk, vec(1), blk, vec(1), vec(1)],
        out_specs=[cur, cur],
        out_shape=[jax.ShapeDtypeStruct((t, c), BF16), jax.ShapeDtypeStruct((t, c), F32)],
        scratch_shapes=[pltpu.VMEM((SUBLANES, ct), F32)], name=name,
        compiler_params=_cp(("parallel", "arbitrary"), 48),
    )(gate, xpre, xpre, cw, cb, wa, ba, wx, bx, lam)


def _lru_bwd(gate, xpre, hs, dy, cw, cb, wa, ba, wx, bx, lam, *, name):
    t, c = gate.shape
    tt, ct = min(LRU_TT, t), LRU_CT
    nt = t // tt
    cur, prev, vec, blk = _lru_specs(t, tt, ct, lambda n: nt - 1 - n)

    def body(gate_ref, x_ref, p_ref, hs_ref, phs_ref, dy_ref, cw_ref, cb_ref, wa_ref, ba_ref, wx_ref, bx_ref, lam_ref,
             dgate_ref, dx_ref, dcw_ref, dcb_ref, dwa_ref, dba_ref, dwx_ref, dbx_ref, dlam_ref, carry, head):
        n = pl.program_id(1)
        tile0 = n == nt - 1

        @pl.when(n == 0)
        def _():
            for ref in (carry, head, dcw_ref, dcb_ref, dwa_ref, dba_ref, dwx_ref, dbx_ref, dlam_ref):
                ref[...] = jnp.zeros_like(ref)

        xp, cwv, lam = x_ref[...], cw_ref[...], lam_ref[...]
        p8 = jnp.where(tile0, 0.0, p_ref[...])
        xr, shifts = _conv4(xp, p8, cwv, cb_ref[...])
        r, i, sp, a, mult = _lru_gates(xr, wa_ref, wx_ref, ba_ref[...], bx_ref[...], lam)
        gate, hsv, dyv = gate_ref[...], hs_ref[...], dy_ref[...]
        dgate_ref[...] = (dyv * hsv * _gelu_grad(gate)).astype(BF16)
        acc_b = dyv * _gelu(gate) + jnp.where(_rows(a.shape) == tt - 1, carry[0:1], 0.0)
        acc_a = _up_fill(a, 1, 0.0)
        d = 1
        while d < tt:
            acc_b = acc_b + acc_a * _up_fill(acc_b, d, 0.0)
            acc_a = acc_a * _up_fill(acc_a, d, 0.0)
            d *= 2
        gsum = acc_b
        carry[...] = jnp.broadcast_to(a[0:1] * gsum[0:1], carry.shape)
        hprev = _shift_down(hsv, jnp.where(tile0, 0.0, phs_ref[...]), 1)
        da = gsum * hprev
        dmult = gsum * i * xr
        di = gsum * mult * xr
        dxr = gsum * mult * i
        dla = da * a - dmult * (a * a) / mult
        dr = dla * (-LRU_C * sp)
        dlam_ref[...] += jnp.sum(dla * (-LRU_C * r), axis=0, keepdims=True) * (-_sigmoid(-lam))
        dpa = dr * r * (1.0 - r)
        dpx = di * i * (1.0 - i)
        dba_ref[...] += jnp.sum(dpa, axis=0, keepdims=True)
        dbx_ref[...] += jnp.sum(dpx, axis=0, keepdims=True)
        dxr = dxr + _blockdiag(dpa, wa_ref, NT) + _blockdiag(dpx, wx_ref, NT)
        for b in range(ct // LANES):
            sl = slice(LANES * b, LANES * (b + 1))
            dwa_ref[b] += _bdot(xr[:, sl], dpa[:, sl], TN)
            dwx_ref[b] += _bdot(xr[:, sl], dpx[:, sl], TN)
        dx, dws = _conv4_bwd(dxr, head[...], xp, shifts, cwv)
        dx_ref[...] = dx.astype(BF16)
        for k in range(4):
            dcw_ref[k:k + 1, :] += dws[k]
        dcb_ref[...] += jnp.sum(dxr, axis=0, keepdims=True)
        head[...] = dxr[:SUBLANES]

    return pl.pallas_call(
        body, grid=(c // ct, nt),
        in_specs=[cur, cur, prev, cur, prev, cur, vec(4), vec(1), blk, vec(1), blk, vec(1), vec(1)],
        out_specs=[cur, cur, vec(4), vec(1), blk, vec(1), blk, vec(1), vec(1)],
        out_shape=[jax.ShapeDtypeStruct((t, c), BF16)] * 2 + [jax.ShapeDtypeStruct((4, c), F32), jax.ShapeDtypeStruct((1, c), F32),
                   jax.ShapeDtypeStruct(wa.shape, F32), jax.ShapeDtypeStruct((1, c), F32),
                   jax.ShapeDtypeStruct(wx.shape, F32), jax.ShapeDtypeStruct((1, c), F32), jax.ShapeDtypeStruct((1, c), F32)],
        scratch_shapes=[pltpu.VMEM((SUBLANES, ct), F32)] * 2, name=name,
        compiler_params=_cp(("parallel", "arbitrary"), 48),
    )(gate, xpre, xpre, hs, hs, dy, cw, cb, wa, ba, wx, bx, lam)


RET_W = HEADS * HEAD_DIM
HALF = HEAD_DIM // 2


def _ret_tables(t):
    c = RET_CHUNK
    inv_freq = ROPE_BASE ** (-jnp.arange(HALF, dtype=F32) / HALF)
    ang = jnp.arange(t, dtype=jnp.int32).astype(F32)[:, None] * inv_freq[None, :]
    cos, sin = jnp.cos(ang), jnp.sin(ang)
    cosf = jnp.concatenate([cos, cos], axis=1)
    sinf = jnp.concatenate([-sin, sin], axis=1)
    log_gamma = jnp.log1p(-jnp.exp2(-5.0 - jnp.arange(HEADS, dtype=F32)))
    idx = jnp.arange(c, dtype=F32)
    rel = idx[:, None] - idx[None, :]
    causal = rel >= 0
    dmask = jnp.where(causal, jnp.exp(log_gamma[:, None, None] * jnp.where(causal, rel, 0.0)), 0.0)
    ktail = jnp.exp(log_gamma[:, None] * (c - 1 - idx))
    qdec = jnp.exp(log_gamma[:, None] * (idx + 1.0))
    rowtab = jnp.broadcast_to(jnp.stack([ktail, qdec], axis=1)[..., None], (HEADS, 2, c, HEAD_DIM))
    cdec = jnp.broadcast_to(jnp.exp(log_gamma * c)[:, None, None], (HEADS, SUBLANES, HEAD_DIM))
    return cosf, sinf, dmask, rowtab, cdec


def _rotary(x, cosf, sinf):
    return x * cosf + pltpu.roll(x, HALF, 1) * sinf


def _rotary_t(dx, cosf, sinf):
    return dx * cosf + pltpu.roll(dx * sinf, HALF, 1)


def _ret_specs(c, order):
    full = lambda shape: pl.BlockSpec(shape, lambda n: (0,) * len(shape))
    return dict(
        proj=pl.BlockSpec((c, 4 * RET_W), lambda n: (order(n), 0)),
        rot=pl.BlockSpec((c, HEAD_DIM), lambda n: (order(n), 0)),
        dmask=full((HEADS, c, c)), rowtab=full((HEADS, 2, c, HEAD_DIM)), cdec=full((HEADS, SUBLANES, HEAD_DIM)),
        state=pl.BlockSpec((1, HEADS, HEAD_DIM, HEAD_DIM), lambda n: (order(n), 0, 0, 0)),
        half=pl.BlockSpec((c, RET_W), lambda n: (order(n), 0)),
    )


def _ret_head(p_ref, h, cosf, sinf):
    sl = lambda j: slice(j * RET_W + h * HEAD_DIM, j * RET_W + (h + 1) * HEAD_DIM)
    q, k, v, g = p_ref[:, sl(0)], p_ref[:, sl(1)], p_ref[:, sl(2)], p_ref[:, sl(3)]
    return _rotary(q, cosf, sinf), _rotary(k, cosf, sinf) * QSCALE, v, g


def _ret_fwd(proj, tables, *, name):
    t = proj.shape[0]
    c = RET_CHUNK
    nc = t // c
    sp = _ret_specs(c, lambda n: n)

    def body(p_ref, cos_ref, sin_ref, dm_ref, rt_ref, cd_ref, y_ref, s_ref, state):
        @pl.when(pl.program_id(0) == 0)
        def _():
            state[...] = jnp.zeros_like(state)

        cosf, sinf = cos_ref[...], sin_ref[...]
        for h in range(HEADS):
            qr, kr, v, g = _ret_head(p_ref, h, cosf, sinf)
            s0 = state[h]
            s_ref[0, h] = s0
            scores = _bdot(qr, kr, NT) * dm_ref[h]
            o = _bdot(scores, v) + _bdot(qr * rt_ref[h, 1], s0)
            state[h] = s0 * cd_ref[h][0:1] + _bdot(kr * rt_ref[h, 0], v, TN)
            rinv = lax.rsqrt(jnp.mean(o * o, axis=-1, keepdims=True) + EPS)
            y_ref[:, h * HEAD_DIM:(h + 1) * HEAD_DIM] = (o * rinv * (g * _sigmoid(g))).astype(BF16)

    return pl.pallas_call(
        body, grid=(nc,),
        in_specs=[sp["proj"], sp["rot"], sp["rot"], sp["dmask"], sp["rowtab"], sp["cdec"]],
        out_specs=[sp["half"], sp["state"]],
        out_shape=[jax.ShapeDtypeStruct((t, 2 * RET_W), BF16), jax.ShapeDtypeStruct((nc, HEADS, HEAD_DIM, HEAD_DIM), F32)],
        scratch_shapes=[pltpu.VMEM((HEADS, HEAD_DIM, HEAD_DIM), F32)], name=name,
        compiler_params=_cp(("arbitrary",), 48),
    )(proj, *tables)


def _ret_bwd(proj, tables, states, dy, *, name):
    t = proj.shape[0]
    c = RET_CHUNK
    nc = t // c
    sp = _ret_specs(c, lambda n: nc - 1 - n)

    def body(p_ref, cos_ref, sin_ref, dm_ref, rt_ref, cd_ref, s_ref, dy_ref, dp_ref, dstate):
        @pl.when(pl.program_id(0) == 0)
        def _():
            dstate[...] = jnp.zeros_like(dstate)

        cosf, sinf = cos_ref[...], sin_ref[...]
        for h in range(HEADS):
            qr, kr, v, g = _ret_head(p_ref, h, cosf, sinf)
            s0, dm, ktl, qdc = s_ref[0, h], dm_ref[h], rt_ref[h, 0], rt_ref[h, 1]
            scores = _bdot(qr, kr, NT) * dm
            qd, kt = qr * qdc, kr * ktl
            o = _bdot(scores, v) + _bdot(qd, s0)
            rinv = lax.rsqrt(jnp.mean(o * o, axis=-1, keepdims=True) + EPS)
            oh = o * rinv
            sg = _sigmoid(g)
            dyh = dy_ref[:, h * HEAD_DIM:(h + 1) * HEAD_DIM]
            dg = dyh * oh * sg * (1.0 + g * (1.0 - sg))
            dyo = dyh * (g * sg)
            do = rinv * (dyo - oh * jnp.mean(dyo * oh, axis=-1, keepdims=True))
            ds1 = dstate[h]
            dsc = _bdot(do, v, NT) * dm
            dv = _bdot(scores, do, TN) + _bdot(kt, ds1)
            dqr = _bdot(dsc, kr) + _bdot(do, s0, NT) * qdc
            dkr = (_bdot(dsc, qr, TN) + _bdot(v, ds1, NT) * ktl) * QSCALE
            dstate[h] = ds1 * cd_ref[h][0:1] + _bdot(qd, do, TN)
            pieces = (_rotary_t(dqr, cosf, sinf), _rotary_t(dkr, cosf, sinf), dv, dg)
            for j, piece in enumerate(pieces):
                dp_ref[:, j * RET_W + h * HEAD_DIM:j * RET_W + (h + 1) * HEAD_DIM] = piece.astype(BF16)

    return pl.pallas_call(
        body, grid=(nc,),
        in_specs=[sp["proj"], sp["rot"], sp["rot"], sp["dmask"], sp["rowtab"], sp["cdec"], sp["state"], sp["half"]],
        out_specs=sp["proj"],
        out_shape=jax.ShapeDtypeStruct((t, 8 * RET_W), BF16),
        scratch_shapes=[pltpu.VMEM((HEADS, HEAD_DIM, HEAD_DIM), F32)], name=name,
        compiler_params=_cp(("arbitrary",), 48),
    )(proj, *tables, states, dy)


GDN_W = HEADS * HEAD_DIM
GDN_CONV = 3 * GDN_W
NEUMANN_STEPS = 5


def _gdn_gates(ps, al, dt):
    return _sigmoid(ps), -jnp.exp(al) * _softplus(ps + dt)


def _cumsum_rows(x):
    d = 1
    while d < x.shape[0]:
        x = x + _down_fill(x, d, 0.0)
        d *= 2
    return x


def _rev_cumsum_rows(x):
    d = 1
    while d < x.shape[0]:
        x = x + _up_fill(x, d, 0.0)
        d *= 2
    return x


class _Chunk:
    pass


def _gdn_chunk(qc, kc, v, beta, g, s0):
    c = GDN_CHUNK
    z = _Chunk()
    z.rq = lax.rsqrt(jnp.sum(qc * qc, axis=-1, keepdims=True) + EPS)
    z.rk = lax.rsqrt(jnp.sum(kc * kc, axis=-1, keepdims=True) + EPS)
    z.qn, z.k = qc * z.rq, kc * z.rk
    z.q = z.qn * QSCALE
    z.v, z.beta = v, beta
    gc = _cumsum_rows(jnp.broadcast_to(g, (c, LANES)))
    ri, ci = _rows((c, c)), _cols((c, c))
    z.tril, z.strict = ri >= ci, ri > ci
    diff = gc[:, :c] - gc.T[:c, :]
    z.decay = jnp.where(z.tril, jnp.exp(jnp.where(z.tril, diff, 0.0)), 0.0)
    z.eg = jnp.exp(gc)
    glast = gc[c - 1:c, :]
    z.egl = jnp.exp(glast - gc)
    z.cd = jnp.exp(glast)
    z.kb = z.k * beta
    z.m = _bdot(z.kb, z.k, NT)
    lmat = jnp.where(z.strict, z.m * z.decay, 0.0)
    neg = -lmat
    inv = (ri == ci).astype(F32) + neg
    pw = neg
    for _ in range(NEUMANN_STEPS):
        pw = _dot3(pw, pw)
        inv = inv + _dot3(inv, pw)
    z.inv = inv
    z.vb, z.kbg = v * beta, z.kb * z.eg
    z.u = _dot3(inv, z.vb)
    z.w = _dot3(inv, z.kbg)
    z.qk = _bdot(z.q, z.k, NT)
    z.attn = jnp.where(z.tril, z.qk * z.decay, 0.0)
    z.qd, z.kt = z.q * z.eg, z.k * z.egl
    z.vnew = z.u - _bdot(z.w, s0)
    z.o = _bdot(z.qd, s0) + _bdot(z.attn, z.vnew)
    z.s1 = s0 * z.cd + _bdot(z.kt, z.vnew, TN)
    return z


def _gdn_chunk_bwd(z, s0, do, ds1):
    c = GDN_CHUNK
    dvnew = _bdot(z.attn, do, TN) + _bdot(z.kt, ds1)
    dqd = _bdot(do, s0, NT)
    dattn = jnp.where(z.tril, _bdot(do, z.vnew, NT), 0.0)
    ds0 = _bdot(z.qd, do, TN) + ds1 * z.cd - _bdot(z.w, dvnew, TN)
    dcd = jnp.sum(jnp.sum(s0 * ds1, axis=1, keepdims=True), axis=0, keepdims=True)
    dkt = _bdot(z.vnew, ds1, NT)
    dw = -_bdot(dvnew, s0, NT)
    dvb = _dot3(z.inv, dvnew, TN)
    dkbg = _dot3(z.inv, dw, TN)
    dl = jnp.where(z.strict, -(_bdot(dvb, z.u, NT) + _bdot(dkbg, z.w, NT)), 0.0)
    dml = dl * z.decay
    dqk = dattn * z.decay
    ddecay = (dl * z.m + dattn * z.qk) * z.decay
    dq = _bdot(dqk, z.k) + dqd * z.eg
    dkb = _bdot(dml, z.k) + dkbg * z.eg
    dk = _bdot(dqk, z.q, TN) + _bdot(dml, z.kb, TN) + dkt * z.egl + dkb * z.beta
    dbeta = jnp.sum(dkb * z.k, axis=-1, keepdims=True) + jnp.sum(dvb * z.v, axis=-1, keepdims=True)
    dv = dvb * z.beta
    colsum = _dot3(ddecay, jnp.ones((c, LANES), F32), TN)
    e = jnp.sum(dkt * z.kt, axis=-1, keepdims=True)
    dgc = (jnp.sum(ddecay, axis=-1, keepdims=True) - colsum
           + jnp.sum(dkbg * z.kbg, axis=-1, keepdims=True) + jnp.sum(dqd * z.qd, axis=-1, keepdims=True) - e)
    dglast = jnp.sum(e, axis=0, keepdims=True) + dcd * z.cd
    dgc = dgc + jnp.where(_rows((c, LANES)) == c - 1, dglast, 0.0)
    dg = _rev_cumsum_rows(dgc)[:, 0:1]
    dqn = dq * QSCALE
    dqc = z.rq * (dqn - z.qn * jnp.sum(dqn * z.qn, axis=-1, keepdims=True))
    dkc = z.rk * (dk - z.k * jnp.sum(dk * z.k, axis=-1, keepdims=True))
    return dqc, dkc, dv, dbeta, dg, ds0


def _gdn_specs(c, order):
    full = lambda shape: pl.BlockSpec(shape, lambda n: (0,) * len(shape))
    return dict(
        proj=pl.BlockSpec((c, 4 * GDN_W), lambda n: (order(n), 1)),
        prev=pl.BlockSpec((SUBLANES, 4 * GDN_W), lambda n: (_prev8(order(n), c), 1)),
        small=pl.BlockSpec((c, LANES), lambda n: (order(n), 0)),
        convw=full((4, GDN_CONV)), vec=full((1, LANES)),
        state=pl.BlockSpec((1, HEADS, HEAD_DIM, HEAD_DIM), lambda n: (order(n), 0, 0, 0)),
        half=pl.BlockSpec((c, GDN_W), lambda n: (order(n), 1)),
        any=pl.BlockSpec(memory_space=pl.ANY),
    )


def _gdn_fwd(proj, psmall, conv_w, al, dt, gain, y_in, comm, *, name):
    t = proj.shape[0]
    c = GDN_CHUNK
    nc = t // c
    sp = _gdn_specs(c, lambda n: n)

    def body(p_ref, prev_ref, ps_ref, cw_ref, al_ref, dt_ref, gain_ref, yin_ref, *rest):
        comm_in, (y_ref, s_ref), comm_out, (state,), comm_sems = comm.split(rest, n_out=2, n_scratch=1)
        n = pl.program_id(0)

        @pl.when(n == 0)
        def _():
            state[...] = jnp.zeros_like(state)
            comm.start(comm_in, comm_out, comm_sems)

        p8 = jnp.where(n == 0, 0.0, prev_ref[:, :GDN_CONV])
        pre, _ = _conv4(p_ref[:, :GDN_CONV], p8, cw_ref[...])
        act = pre * _sigmoid(pre)
        beta_all, g_all = _gdn_gates(ps_ref[...], al_ref[...], dt_ref[...])
        for h in range(HEADS):
            sl = lambda j: slice(j * GDN_W + h * HEAD_DIM, j * GDN_W + (h + 1) * HEAD_DIM)
            s0 = state[h]
            s_ref[0, h] = s0
            z = _gdn_chunk(act[:, sl(0)], act[:, sl(1)], act[:, sl(2)], beta_all[:, h:h + 1], g_all[:, HEADS + h:HEADS + h + 1], s0)
            state[h] = z.s1
            rinv = lax.rsqrt(jnp.mean(z.o * z.o, axis=-1, keepdims=True) + EPS)
            gd = p_ref[:, sl(3)]
            y_ref[:, sl(0)] = (z.o * rinv * gain_ref[...] * (gd * _sigmoid(gd))).astype(BF16)

        @pl.when(n == nc - 1)
        def _():
            comm.finish(comm_in, comm_out, comm_sems)

    outs = pl.pallas_call(
        body, grid=(nc,),
        in_specs=[sp["proj"], sp["prev"], sp["small"], sp["convw"], sp["vec"], sp["vec"], sp["vec"], sp["any"]]
        + [sp["any"]] * len(comm.arrays),
        out_specs=[sp["half"], sp["state"]] + [sp["any"]] * len(comm.out_shapes),
        out_shape=[jax.ShapeDtypeStruct((t, 2 * GDN_W), BF16), jax.ShapeDtypeStruct((nc, HEADS, HEAD_DIM, HEAD_DIM), F32)]
        + comm.out_shapes,
        scratch_shapes=[pltpu.VMEM((HEADS, HEAD_DIM, HEAD_DIM), F32)] + comm.scratch, name=name,
        input_output_aliases={7: 0}, compiler_params=_cp(("arbitrary",), 48),
    )(proj, proj, psmall, conv_w, al, dt, gain, y_in, *comm.arrays)
    return outs[0], outs[1], outs[2:]


def _gdn_bwd(proj, psmall, conv_w, al, dt, gain, states, dy, dproj_in, comm, *, name):
    t = proj.shape[0]
    c = GDN_CHUNK
    nc = t // c
    sp = _gdn_specs(c, lambda n: nc - 1 - n)

    def body(p_ref, prev_ref, ps_ref, cw_ref, al_ref, dt_ref, gain_ref, s_ref, dy_ref, dpin_ref, *rest):
        comm_in, outs, comm_out, (dstate, head), comm_sems = comm.split(rest, n_out=6, n_scratch=2)
        dp_ref, dps_ref, dcw_ref, dal_ref, ddt_ref, dgain_ref = outs
        n = pl.program_id(0)
        chunk0 = n == nc - 1

        @pl.when(n == 0)
        def _():
            for ref in (dstate, head, dcw_ref, dal_ref, ddt_ref, dgain_ref):
                ref[...] = jnp.zeros_like(ref)
            comm.start(comm_in, comm_out, comm_sems)

        x, cwv = p_ref[:, :GDN_CONV], cw_ref[...]
        p8 = jnp.where(chunk0, 0.0, prev_ref[:, :GDN_CONV])
        pre, shifts = _conv4(x, p8, cwv)
        sg_pre = _sigmoid(pre)
        act = pre * sg_pre
        ps, alv, dtv, gain = ps_ref[...], al_ref[...], dt_ref[...], gain_ref[...]
        beta_all, g_all = _gdn_gates(ps, alv, dtv)
        lane = _cols((c, LANES))
        dbeta_all = jnp.zeros((c, LANES), F32)
        dg_all = jnp.zeros((c, LANES), F32)
        dact = [None] * (3 * HEADS)
        dgain = jnp.zeros((1, LANES), F32)
        for h in range(HEADS):
            sl = lambda j: slice(j * GDN_W + h * HEAD_DIM, j * GDN_W + (h + 1) * HEAD_DIM)
            s0 = s_ref[0, h]
            z = _gdn_chunk(act[:, sl(0)], act[:, sl(1)], act[:, sl(2)], beta_all[:, h:h + 1], g_all[:, HEADS + h:HEADS + h + 1], s0)
            rinv = lax.rsqrt(jnp.mean(z.o * z.o, axis=-1, keepdims=True) + EPS)
            oh = z.o * rinv
            gd = p_ref[:, sl(3)]
            sgd = _sigmoid(gd)
            dyh = dy_ref[:, sl(0)]
            dgain = dgain + jnp.sum(dyh * oh * (gd * sgd), axis=0, keepdims=True)
            dp_ref[:, sl(3)] = (dyh * oh * gain * sgd * (1.0 + gd * (1.0 - sgd))).astype(BF16)
            dyo = dyh * gain * (gd * sgd)
            do = rinv * (dyo - oh * jnp.mean(dyo * oh, axis=-1, keepdims=True))
            dqc, dkc, dv, dbeta, dg, ds0 = _gdn_chunk_bwd(z, s0, do, dstate[h])
            dstate[h] = ds0
            dact[h], dact[HEADS + h], dact[2 * HEADS + h] = dqc, dkc, dv
            dbeta_all = dbeta_all + jnp.where(lane == h, dbeta, 0.0)
            dg_all = dg_all + jnp.where(lane == HEADS + h, dg, 0.0)
        dpre = jnp.concatenate(dact, axis=1) * sg_pre * (1.0 + pre * (1.0 - sg_pre))
        dx, dws = _conv4_bwd(dpre, head[...], x, shifts, cwv)
        dp_ref[:, :GDN_CONV] = dx.astype(BF16)
        for k in range(4):
            dcw_ref[k:k + 1, :] += dws[k]
        head[...] = dpre[:SUBLANES]
        dsp = dg_all * (-jnp.exp(alv)) * _sigmoid(ps + dtv)
        dps_ref[...] = (dbeta_all * beta_all * (1.0 - beta_all) + dsp).astype(BF16)
        ddt_ref[...] += jnp.sum(dsp, axis=0, keepdims=True)
        dal_ref[...] += jnp.sum(dg_all * g_all, axis=0, keepdims=True)
        dgain_ref[...] += dgain

        @pl.when(n == nc - 1)
        def _():
            comm.finish(comm_in, comm_out, comm_sems)

    vec_f32 = jax.ShapeDtypeStruct((1, LANES), F32)
    outs = pl.pallas_call(
        body, grid=(nc,),
        in_specs=[sp["proj"], sp["prev"], sp["small"], sp["convw"], sp["vec"], sp["vec"], sp["vec"], sp["state"], sp["half"], sp["any"]]
        + [sp["any"]] * len(comm.arrays),
        out_specs=[sp["proj"], sp["small"], sp["convw"], sp["vec"], sp["vec"], sp["vec"]] + [sp["any"]] * len(comm.out_shapes),
        out_shape=[jax.ShapeDtypeStruct((t, 8 * GDN_W), BF16), jax.ShapeDtypeStruct((t, LANES), BF16),
                   jax.ShapeDtypeStruct((4, GDN_CONV), F32), vec_f32, vec_f32, vec_f32] + comm.out_shapes,
        scratch_shapes=[pltpu.VMEM((HEADS, HEAD_DIM, HEAD_DIM), F32), pltpu.VMEM((SUBLANES, GDN_CONV), F32)] + comm.scratch,
        name=name, input_output_aliases={9: 0}, compiler_params=_cp(("arbitrary",), 48),
    )(proj, proj, psmall, conv_w, al, dt, gain, states, dy, dproj_in, *comm.arrays)
    return outs[:6], outs[6:]


def _here():
    x, y, c = lax.axis_index("x"), lax.axis_index("y"), lax.axis_index("c")
    return x, y, c, [(1 - x, y), (x, 1 - y), (1 - x, 1 - y)]


def _rdma(src, dst, send, recv, k, dev):
    return pltpu.make_async_remote_copy(src_ref=src, dst_ref=dst, send_sem=send.at[k], recv_sem=recv.at[k],
                                        device_id=dev, device_id_type=MESH)


def _dma_sems(n):
    return [pltpu.SemaphoreType.DMA((n,)), pltpu.SemaphoreType.DMA((n,)), pltpu.SemaphoreType.DMA((1,))]


class _AllGather:
    def __init__(self, array):
        self.arrays = [array]
        self.out_shapes = [jax.ShapeDtypeStruct((N_DEV,) + array.shape, array.dtype)]
        self.scratch = _dma_sems(7)

    def start(self, ins, outs, sems):
        (src,), (out,), (send, recv, loc) = ins, outs, sems
        x, y, c, chips = _here()
        mine = out.at[4 * x + 2 * y + c]
        pltpu.make_async_copy(src, mine, loc.at[0]).start()
        _rdma(src, mine, send, recv, 0, (x, y, 1 - c)).start()
        for j, (cx, cy) in enumerate(chips):
            _rdma(src, mine, send, recv, 1 + j, (cx, cy, c)).start()

    def finish(self, ins, outs, sems):
        (src,), (out,), (send, recv, loc) = ins, outs, sems
        x, y, c, chips = _here()
        sibling = (x, y, 1 - c)
        mine = out.at[4 * x + 2 * y + c]
        for j, (cx, cy) in enumerate(chips):
            got = out.at[4 * cx + 2 * cy + c]
            _rdma(got, got, send, recv, 1 + j, sibling).wait_recv()
            _rdma(got, got, send, recv, 4 + j, sibling).start()
        _rdma(src, mine, send, recv, 0, sibling).wait_recv()
        for j in range(3):
            _rdma(src, mine, send, recv, 4 + j, sibling).wait_recv()
        for k in range(7):
            _rdma(src, mine, send, recv, k, sibling).wait_send()
        pltpu.make_async_copy(src, mine, loc.at[0]).wait()


class _ChipExchange:
    def __init__(self, array):
        self.arrays = [array]
        self.out_shapes = [jax.ShapeDtypeStruct(array.shape, array.dtype)]
        self.scratch = _dma_sems(3)

    def _copies(self, ins, outs, sems):
        (src,), (out,), (send, recv, loc) = ins, outs, sems
        x, y, c, chips = _here()
        here = 2 * x + y
        local = pltpu.make_async_copy(src.at[here], out.at[here], loc.at[0])
        return local, [_rdma(src.at[2 * cx + cy], out.at[here], send, recv, j, (cx, cy, c)) for j, (cx, cy) in enumerate(chips)]

    def start(self, ins, outs, sems):
        local, remote = self._copies(ins, outs, sems)
        local.start()
        for cp in remote:
            cp.start()

    def finish(self, ins, outs, sems):
        local, remote = self._copies(ins, outs, sems)
        for cp in remote:
            cp.wait()
        local.wait()


class _PairSwap:
    def __init__(self, array):
        self.arrays = [array]
        self.out_shapes = [jax.ShapeDtypeStruct(array.shape[1:], array.dtype)] * 2
        self.scratch = _dma_sems(1)

    def _copies(self, ins, outs, sems):
        (src,), (mine, theirs), (send, recv, loc) = ins, outs, sems
        x, y, c, _ = _here()
        return pltpu.make_async_copy(src.at[c], mine, loc.at[0]), _rdma(src.at[1 - c], theirs, send, recv, 0, (x, y, 1 - c))

    def start(self, ins, outs, sems):
        for cp in self._copies(ins, outs, sems):
            cp.start()

    def finish(self, ins, outs, sems):
        for cp in self._copies(ins, outs, sems):
            cp.wait()


class _Comm:
    def __init__(self, ops):
        self.ops = ops
        self.arrays = [a for op in ops for a in op.arrays]
        self.out_shapes = [s for op in ops for s in op.out_shapes]
        self.scratch = [s for op in ops for s in op.scratch]

    def split(self, rest, n_out, n_scratch):
        cuts = np.cumsum([0, len(self.arrays), n_out, len(self.out_shapes), n_scratch, len(self.scratch)])
        assert cuts[-1] == len(rest)
        return tuple(rest[a:b] for a, b in zip(cuts[:-1], cuts[1:]))

    def _each(self, method, ins, outs, sems):
        i = o = s = 0
        for op in self.ops:
            ni, no, ns = len(op.arrays), len(op.out_shapes), len(op.scratch)
            getattr(op, method)(ins[i:i + ni], outs[o:o + no], sems[s:s + ns])
            i, o, s = i + ni, o + no, s + ns

    def start(self, ins, outs, sems):
        self._each("start", ins, outs, sems)

    def finish(self, ins, outs, sems):
        self._each("finish", ins, outs, sems)

    def run(self, name):
        def body(*refs):
            ins, _, outs, _, sems = self.split(refs, 0, 0)
            self.start(ins, outs, sems)
            self.finish(ins, outs, sems)

        hbm = pl.BlockSpec(memory_space=pl.ANY)
        return pl.pallas_call(body, in_specs=[hbm] * len(self.arrays), out_specs=[hbm] * len(self.out_shapes),
                              out_shape=self.out_shapes, scratch_shapes=self.scratch, name=name)(*self.arrays)


def _sum_slots(x, *, name, tr=None):
    n, r, l = x.shape
    tr = r if tr is None else tr

    def body(x_ref, o_ref):
        acc = x_ref[0].astype(F32)
        for s in range(1, n):
            acc = acc + x_ref[s].astype(F32)
        o_ref[...] = acc

    return pl.pallas_call(
        body, grid=(r // tr,), in_specs=[pl.BlockSpec((n, tr, l), lambda i: (0, i, 0))],
        out_specs=pl.BlockSpec((tr, l), lambda i: (i, 0)), out_shape=jax.ShapeDtypeStruct((r, l), F32),
        name=name, compiler_params=_cp(("parallel",), 48),
    )(x)


def _pair_add(a, b, *, name, tr=None):
    n, r, l = a.shape
    tr = r if tr is None else tr

    def body(a_ref, b_ref, o_ref):
        o_ref[...] = (a_ref[...].astype(F32) + b_ref[...].astype(F32)).astype(BF16)

    spec = pl.BlockSpec((n, tr, l), lambda i: (0, i, 0))
    return pl.pallas_call(body, grid=(r // tr,), in_specs=[spec, spec], out_specs=spec,
                          out_shape=jax.ShapeDtypeStruct(a.shape, BF16), name=name, compiler_params=_cp(("parallel",), 48))(a, b)


ADAM_TILE_ELEMS = 512 * 1024


def _adam(w, g, m, v, *, name):
    shape = w.shape
    cols = shape[-1]
    rows = math.prod(shape[:-1]) if len(shape) > 1 else 1
    tr = rows
    if rows * cols > ADAM_TILE_ELEMS:
        tr = max(d for d in range(SUBLANES, ADAM_TILE_ELEMS // cols + 1, SUBLANES) if rows % d == 0)
    c1, c2 = 1.0 - ADAM_B1 ** ADAM_STEP, 1.0 - ADAM_B2 ** ADAM_STEP

    def body(w_ref, g_ref, m_ref, v_ref, d_ref, m2_ref, v2_ref):
        gv = g_ref[...]
        m2 = ADAM_B1 * m_ref[...] + (1.0 - ADAM_B1) * gv
        v2 = ADAM_B2 * v_ref[...] + (1.0 - ADAM_B2) * (gv * gv)
        d_ref[...] = -ADAM_LR * ((m2 / c1) / (jnp.sqrt(v2 / c2) + ADAM_EPS) + ADAM_WD * w_ref[...])
        m2_ref[...] = m2
        v2_ref[...] = v2

    spec = pl.BlockSpec((tr, cols), lambda i: (i, 0))
    outs = pl.pallas_call(
        body, grid=(rows // tr,), in_specs=[spec] * 4, out_specs=[spec] * 3,
        out_shape=[jax.ShapeDtypeStruct((rows, cols), F32)] * 3, name=name, compiler_params=_cp(("parallel",), 48),
    )(*(a.reshape(rows, cols) for a in (w, g, m, v)))
    return tuple(o.reshape(shape) for o in outs)


WEIGHTS = ['norm_mix', 'norm_ffn', 'ret_gdn_w_in', 'gdn_conv_w', 'gdn_a_log', 'gdn_dt_bias', 'gdn_out_gain', 'ret_gdn_w_out',
           'lru_w_in', 'lru_conv_w', 'lru_conv_b', 'lru_w_a', 'lru_b_a', 'lru_w_x', 'lru_b_x', 'lru_lambda', 'lru_w_out',
           'ffn_w_up', 'ffn_conv_w', 'ffn_conv_b', 'ffn_w_down', 'norm_final']
BIG = {'ret_gdn_w_in': ((1, 1024, 513), 2), 'ret_gdn_w_out': ((1, 128, 1024), 1), 'lru_w_in': ((1, 1024, 256), 2),
       'lru_w_out': ((1, 128, 1024), 1), 'ffn_w_up': ((2, 1024, 704), 2), 'ffn_w_down': ((2, 352, 1024), 1)}
SMALL = {'gdn_conv_w': ((1, 4, 192), 2), 'lru_conv_w': ((1, 4, 128), 2), 'lru_conv_b': ((1, 128), 1), 'lru_b_a': ((1, 128), 1),
         'lru_b_x': ((1, 128), 1), 'lru_lambda': ((1, 128), 1), 'ffn_conv_w': ((2, 3, 704), 2)}
REPLICATED = {'norm_mix': (2, 1024), 'norm_ffn': (2, 1024), 'gdn_a_log': (1, 4), 'gdn_dt_bias': (1, 4), 'gdn_out_gain': (1, 128),
              'lru_w_a': (1, 8, 128, 128), 'lru_w_x': (1, 8, 128, 128), 'ffn_conv_b': (2, 5632), 'norm_final': (1024,)}
FIRST = ['ret_gdn_w_in', 'ret_gdn_w_out']
REST = ['lru_w_in', 'lru_w_out', 'ffn_w_up', 'ffn_w_down']
GROUP_ROWS = {FIRST[0]: 5632, REST[0]: 19968}
GROUP_TILE = {FIRST[0]: 512, REST[0]: 1536}
EARLY = {'lru_conv_w': (1, 4, 1024), 'lru_conv_b': (1, 1024), 'lru_b_a': (1, 1024), 'lru_b_x': (1, 1024), 'lru_lambda': (1, 1024),
         'ffn_conv_w': (2, 3, 5632), 'norm_ffn': (2, 1024), 'norm_mix1': (1, 1024), 'lru_w_a': (1, 8, 128, 128),
         'lru_w_x': (1, 8, 128, 128), 'ffn_conv_b': (2, 5632), 'norm_final': (1024,)}
LATE = {'gdn_conv_w': (1, 4, 1536), 'norm_mix0': (1, 1024), 'gdn_a_log': (1, 4), 'gdn_dt_bias': (1, 4), 'gdn_out_gain': (1, 128)}


def _full_shape(shard, axis):
    return tuple(d * N_DEV if i == axis else d for i, d in enumerate(shard))


def _rows_of(n_elems):
    return -(-n_elems // LANES)


def _to_rows(a, lead=()):
    flat = a.reshape(lead + (-1,))
    pad = _rows_of(flat.shape[-1]) * LANES - flat.shape[-1]
    if pad:
        flat = jnp.pad(flat, [(0, 0)] * len(lead) + [(0, pad)])
    return flat.reshape(lead + (-1, LANES))


def _pack(pieces, total_rows, lead=()):
    buf = jnp.concatenate(pieces, axis=len(lead))
    pad = total_rows - buf.shape[len(lead)]
    return jnp.pad(buf, [(0, 0)] * len(lead) + [(0, pad), (0, 0)]) if pad else buf


def _unpack(buf, shapes, lead=()):
    out, off = [], 0
    for shape in shapes:
        n = math.prod(shape)
        rows = _rows_of(n)
        piece = lax.slice_in_dim(buf, off, off + rows, axis=len(lead)).reshape(lead + (rows * LANES,))
        out.append(lax.slice_in_dim(piece, 0, n, axis=len(lead)).reshape(lead + shape))
        off += rows
    return out


def _join_blocks(g, axis):
    m = jnp.moveaxis(g, 0, axis)
    return m.reshape(m.shape[:axis] + (N_DEV * m.shape[axis + 1],) + m.shape[axis + 2:])


def _split_blocks(full, axis):
    s = full.shape
    return jnp.moveaxis(full.reshape(s[:axis] + (N_DEV, s[axis] // N_DEV) + s[axis + 1:]), axis, 0)


def _small_rows(shapes):
    total = sum(_rows_of(math.prod(s)) for s in shapes)
    return -(-total // SUBLANES) * SUBLANES


def _ffn_forward(h, gain, wug, wuv, cwg, cwv, cbg, cbv, wd, tag):
    hn = _norm_fwd(h, gain, name=f"ffn{tag}_norm")
    ug = _mm(hn, wug, name=f"ffn{tag}_up_gate", tn=1408)
    uv = _mm(hn, wuv, name=f"ffn{tag}_up_val", tn=1408)
    act = _ffn_act_fwd(ug, uv, cwg, cwv, cbg, cbv, name=f"ffn{tag}_act")
    out = _mm(act, wd, res=h, name=f"ffn{tag}_down", tk=1408)
    return out, (hn, ug, uv, act)


def _ffn_backward(dh, h, gain, saved, wug, wuv, cwg, cwv, cbg, cbv, wd, tag):
    hn, ug, uv, act = saved
    da = _mm(dh, wd, tb=True, name=f"ffn{tag}_d_act", tn=1408)
    dwd = _mm(act, dh, ta=True, out_dtype=BF16, name=f"ffn{tag}_d_wdown", tm=1408)
    dug, duv, dcwg, dcwv, dcbg, dcbv = _ffn_act_bwd(ug, uv, da, cwg, cwv, cbg, cbv, name=f"ffn{tag}_act_bwd")
    dhn = _mm(dug, wug, tb=True, name=f"ffn{tag}_d_hn_gate", tk=1408)
    dhn = _mm(duv, wuv, tb=True, res=dhn, name=f"ffn{tag}_d_hn_val", tk=1408)
    dwu = jnp.concatenate([_mm(hn, dug, ta=True, out_dtype=BF16, name=f"ffn{tag}_d_wup_gate", tn=1408),
                           _mm(hn, duv, ta=True, out_dtype=BF16, name=f"ffn{tag}_d_wup_val", tn=1408)], axis=1)
    dh_in, dgain = _norm_bwd(h, gain, dhn, dh, name=f"ffn{tag}_norm_bwd")
    grads = dict(w_up=dwu, w_down=dwd, conv_w=jnp.concatenate([dcwg, dcwv], axis=1), conv_b=jnp.concatenate([dcbg, dcbv], axis=1),
                 norm=dgain)
    return dh_in, grads


def kernel(x, norm_mix, norm_ffn, ret_gdn_w_in, gdn_conv_w, gdn_a_log, gdn_dt_bias, gdn_out_gain, ret_gdn_w_out, lru_w_in, lru_conv_w, lru_conv_b, lru_w_a, lru_b_a, lru_w_x, lru_b_x, lru_lambda, lru_w_out, ffn_w_up, ffn_conv_w, ffn_conv_b, ffn_w_down, norm_final, loss_target, m_norm_mix, m_norm_ffn, m_ret_gdn_w_in, m_gdn_conv_w, m_gdn_a_log, m_gdn_dt_bias, m_gdn_out_gain, m_ret_gdn_w_out, m_lru_w_in, m_lru_conv_w, m_lru_conv_b, m_lru_w_a, m_lru_b_a, m_lru_w_x, m_lru_b_x, m_lru_lambda, m_lru_w_out, m_ffn_w_up, m_ffn_conv_w, m_ffn_conv_b, m_ffn_w_down, m_norm_final, v_norm_mix, v_norm_ffn, v_ret_gdn_w_in, v_gdn_conv_w, v_gdn_a_log, v_gdn_dt_bias, v_gdn_out_gain, v_ret_gdn_w_out, v_lru_w_in, v_lru_conv_w, v_lru_conv_b, v_lru_w_a, v_lru_b_a, v_lru_w_x, v_lru_b_x, v_lru_lambda, v_lru_w_out, v_ffn_w_up, v_ffn_conv_w, v_ffn_conv_b, v_ffn_w_down, v_norm_final):
    given = dict(locals())
    w = {n: given[n] for n in WEIGHTS}
    me = 4 * lax.axis_index("x") + 2 * lax.axis_index("y") + lax.axis_index("c")
    t = x.shape[1]
    f = D_FF

    def rows_buf(names, cast):
        return _pack([_to_rows(w[n].astype(cast)) for n in names], GROUP_ROWS[names[0]])

    def scatter_buf(grads_by_name, names):
        blocks = _pack([_to_rows(_split_blocks(grads_by_name[n], BIG[n][1]), lead=(N_DEV,)) for n in names],
                       GROUP_ROWS[names[0]], lead=(N_DEV,))
        return blocks.reshape(4, 2, -1, LANES).transpose(1, 0, 2, 3)

    def joined(gathered, table, names):
        blocks = _unpack(gathered, [table[n][0] for n in names], lead=(N_DEV,))
        return {n: _join_blocks(b, table[n][1]) for n, b in zip(names, blocks)}

    small_shapes = [s for s, _ in SMALL.values()]
    small_buf = _pack([_to_rows(w[n]) for n in SMALL], _small_rows(small_shapes))
    g_first, g_small = _Comm([_AllGather(rows_buf(FIRST, BF16)), _AllGather(small_buf)]).run("gather_first")
    rest_gather = _Comm([_AllGather(rows_buf(REST, BF16))])
    full = {**joined(g_first, BIG, FIRST), **joined(g_small, SMALL, list(SMALL))}

    w_in0 = full['ret_gdn_w_in'][0]
    w_main = w_in0[:, :MAIN_IN]
    w_narrow = jnp.pad(w_in0[:, MAIN_IN:], ((0, 0), (0, LANES - SMALL_IN)))
    w_out0 = full['ret_gdn_w_out'][0]
    fcw = full['ffn_conv_w']
    fcw_g, fcw_v = [fcw[l][:, :f] for l in range(2)], [fcw[l][:, f:] for l in range(2)]
    fcb_g, fcb_v = [ffn_conv_b[l:l + 1, :f] for l in range(2)], [ffn_conv_b[l:l + 1, f:] for l in range(2)]
    gdn_cw = full['gdn_conv_w'][0]
    al_pad = jnp.pad(gdn_a_log, ((0, 0), (HEADS, LANES - 2 * HEADS)))
    dt_pad = jnp.pad(gdn_dt_bias, ((0, 0), (HEADS, LANES - 2 * HEADS)))
    lru_cw, lru_cb = full['lru_conv_w'][0], full['lru_conv_b']
    lru_ba, lru_bx, lru_lam = full['lru_b_a'], full['lru_b_x'], full['lru_lambda']
    wa, wx = lru_w_a[0], lru_w_x[0]

    h0, target = x[0], loss_target[0]
    hn0 = _norm_fwd(h0, norm_mix[0:1], name="mix0_norm")
    proj = _mm(hn0, w_main, name="mix0_in")
    pnarrow = _mm(hn0, w_narrow, name="mix0_in_narrow")
    tables = _ret_tables(t)
    y0, ret_states = _ret_fwd(proj, tables, name="retention_fwd")
    y0, gdn_states, (g_rest,) = _gdn_fwd(proj, pnarrow, gdn_cw, al_pad, dt_pad, gdn_out_gain, y0, rest_gather, name="deltanet_fwd")
    full.update(joined(g_rest, BIG, REST))
    lru_in_g, lru_in_x = full['lru_w_in'][0][:, :D_MODEL], full['lru_w_in'][0][:, D_MODEL:]
    lru_out = full['lru_w_out'][0]
    up_g = [full['ffn_w_up'][l][:, :f] for l in range(2)]
    up_v = [full['ffn_w_up'][l][:, f:] for l in range(2)]
    down = [full['ffn_w_down'][l] for l in range(2)]
    h1 = _mm(y0, w_out0, res=h0, name="mix0_out")
    h2, ffn0_saved = _ffn_forward(h1, norm_ffn[0:1], up_g[0], up_v[0], fcw_g[0], fcw_v[0], fcb_g[0], fcb_v[0], down[0], 0)
    hn1 = _norm_fwd(h2, norm_mix[1:2], name="mix1_norm")
    gate = _mm(hn1, lru_in_g, name="mix1_in_gate")
    xpre = _mm(hn1, lru_in_x, name="mix1_in_x")
    y1, hs = _lru_fwd(gate, xpre, lru_cw, lru_cb, wa, lru_ba, wx, lru_bx, lru_lam, name="rglru_fwd")
    h3 = _mm(y1, lru_out, res=h2, name="mix1_out")
    h4, ffn1_saved = _ffn_forward(h3, norm_ffn[1:2], up_g[1], up_v[1], fcw_g[1], fcw_v[1], fcb_g[1], fcb_v[1], down[1], 1)
    dh4, d_norm_final, loss_part = _final_loss(h4, norm_final[None, :], target, name="final_norm_loss")
    loss = lax.psum(loss_part[0, 0], ("x", "y", "c"))

    dh3, gf1 = _ffn_backward(dh4, h3, norm_ffn[1:2], ffn1_saved, up_g[1], up_v[1], fcw_g[1], fcw_v[1], fcb_g[1], fcb_v[1], down[1], 1)
    dy1 = _mm(dh3, lru_out, tb=True, name="mix1_d_y")
    d_lru_out = _mm(y1, dh3, ta=True, out_dtype=BF16, name="mix1_d_wout")
    dgate, dxpre, d_lcw, d_lcb, d_wa, d_ba, d_wx, d_bx, d_lam = _lru_bwd(
        gate, xpre, hs, dy1, lru_cw, lru_cb, wa, lru_ba, wx, lru_bx, lru_lam, name="rglru_bwd")
    dhn1 = _mm(dgate, lru_in_g, tb=True, name="mix1_d_hn_gate")
    dhn1 = _mm(dxpre, lru_in_x, tb=True, res=dhn1, name="mix1_d_hn_x")
    d_lru_in = jnp.concatenate([_mm(hn1, dgate, ta=True, out_dtype=BF16, name="mix1_d_win_gate"),
                                _mm(hn1, dxpre, ta=True, out_dtype=BF16, name="mix1_d_win_x")], axis=1)
    dh2, d_mix1 = _norm_bwd(h2, norm_mix[1:2], dhn1, dh3, name="mix1_norm_bwd")
    dh1, gf0 = _ffn_backward(dh2, h1, norm_ffn[0:1], ffn0_saved, up_g[0], up_v[0], fcw_g[0], fcw_v[0], fcb_g[0], fcb_v[0], down[0], 0)
    dy0 = _mm(dh1, w_out0, tb=True, name="mix0_d_y")
    d_w_out0 = _mm(y0, dh1, ta=True, out_dtype=BF16, name="mix0_d_wout")
    rest_grads = {'lru_w_in': d_lru_in[None], 'lru_w_out': d_lru_out[None],
                  'ffn_w_up': jnp.stack([gf0['w_up'], gf1['w_up']]), 'ffn_w_down': jnp.stack([gf0['w_down'], gf1['w_down']])}
    mine, theirs = _Comm([_PairSwap(scatter_buf(rest_grads, REST))]).run("pair_swap_rest")
    z_rest = _pair_add(mine, theirs, name="pair_add_rest", tr=GROUP_TILE[REST[0]])
    early = {'lru_conv_w': d_lcw[None], 'lru_conv_b': d_lcb, 'lru_b_a': d_ba, 'lru_b_x': d_bx, 'lru_lambda': d_lam,
             'ffn_conv_w': jnp.stack([gf0['conv_w'], gf1['conv_w']]), 'norm_ffn': jnp.concatenate([gf0['norm'], gf1['norm']], axis=0),
             'norm_mix1': d_mix1, 'lru_w_a': d_wa[None], 'lru_w_x': d_wx[None],
             'ffn_conv_b': jnp.concatenate([gf0['conv_b'], gf1['conv_b']], axis=0), 'norm_final': d_norm_final[0]}
    early_buf = _pack([_to_rows(early[n]) for n in EARLY], _small_rows(list(EARLY.values())))
    rest_exchange = _Comm([_ChipExchange(z_rest), _AllGather(early_buf)])

    dproj = _ret_bwd(proj, tables, ret_states, dy0, name="retention_bwd")
    (dproj, dnarrow, d_gcw, d_alog, d_dtb, d_gain), (w_rest, got_early) = _gdn_bwd(
        proj, pnarrow, gdn_cw, al_pad, dt_pad, gdn_out_gain, gdn_states, dy0, dproj, rest_exchange, name="deltanet_bwd")
    dhn0 = _mm(dproj, w_main, tb=True, name="mix0_d_hn")
    dhn0 = _mm(dnarrow, w_narrow, tb=True, res=dhn0, name="mix0_d_hn_narrow")
    d_w_main = _mm(hn0, dproj, ta=True, out_dtype=BF16, name="mix0_d_win")
    d_w_narrow = _mm(hn0, dnarrow, ta=True, out_dtype=BF16, name="mix0_d_win_narrow")
    dx, d_mix0 = _norm_bwd(h0, norm_mix[0:1], dhn0, dh1, name="mix0_norm_bwd")

    first_grads = {'ret_gdn_w_in': jnp.concatenate([d_w_main, d_w_narrow[:, :SMALL_IN]], axis=1)[None], 'ret_gdn_w_out': d_w_out0[None]}
    mine, theirs = _Comm([_PairSwap(scatter_buf(first_grads, FIRST))]).run("pair_swap_first")
    z_first = _pair_add(mine, theirs, name="pair_add_first", tr=GROUP_TILE[FIRST[0]])
    late = {'gdn_conv_w': d_gcw[None], 'norm_mix0': d_mix0, 'gdn_a_log': d_alog[:, HEADS:2 * HEADS],
            'gdn_dt_bias': d_dtb[:, HEADS:2 * HEADS], 'gdn_out_gain': d_gain}
    late_buf = _pack([_to_rows(late[n]) for n in LATE], _small_rows(list(LATE.values())))
    w_first, got_late = _Comm([_ChipExchange(z_first), _AllGather(late_buf)]).run("exchange_first")

    grads = {}
    for names, got in ((REST, w_rest), (FIRST, w_first)):
        total = _sum_slots(got, name=f"sum_blocks_{names[0]}", tr=GROUP_TILE[names[0]])
        grads.update(zip(names, _unpack(total, [BIG[n][0] for n in names])))
    partial = dict(zip(EARLY, _unpack(_sum_slots(got_early, name="sum_partials_early"), list(EARLY.values()))))
    partial.update(zip(LATE, _unpack(_sum_slots(got_late, name="sum_partials_late"), list(LATE.values()))))
    partial['norm_mix'] = jnp.concatenate([partial.pop('norm_mix0'), partial.pop('norm_mix1')], axis=0)
    for n, g_full in partial.items():
        if n in SMALL:
            shard, axis = SMALL[n]
            g_full = lax.dynamic_slice_in_dim(g_full, me * shard[axis], shard[axis], axis=axis)
        grads[n] = g_full

    delta, new_m, new_v = {}, {}, {}
    for n in WEIGHTS:
        delta[n], new_m[n], new_v[n] = _adam(w[n], grads[n], given["m_" + n], given["v_" + n], name=f"adamw_{n}")
    return (loss, dx[None], *[grads[n] for n in WEIGHTS], *[delta[n] for n in WEIGHTS],
            *[new_m[n] for n in WEIGHTS], *[new_v[n] for n in WEIGHTS])
```

```python
import functools
import math

import numpy as np
import jax
import jax.numpy as jnp
from jax import lax
from jax.experimental import pallas as pl
from jax.experimental.pallas import tpu as pltpu

F32 = jnp.float32
BF16 = jnp.bfloat16
HI = lax.Precision.HIGHEST
MESH = pl.DeviceIdType.MESH

N_DEV = 8
LANES = 128
SUBLANES = 8
EPS = 1e-6
D_MODEL = 1024
HEADS = 4
HEAD_DIM = 128
RET_CHUNK = 128
GDN_CHUNK = 64
ROPE_BASE = 10000.0
LRU_C = 8.0
D_FF = 2816
MAIN_IN = 4096
SMALL_IN = 8
QSCALE = HEAD_DIM ** -0.5

ADAM_LR, ADAM_B1, ADAM_B2, ADAM_EPS, ADAM_WD, ADAM_STEP = 0.001, 0.9, 0.999, 1e-08, 0.01, 10


def _cp(sem=None, vmem_mb=None):
    kw = {}
    if sem is not None:
        kw["dimension_semantics"] = sem
    if vmem_mb is not None:
        kw["vmem_limit_bytes"] = vmem_mb << 20
    return pltpu.CompilerParams(**kw)


def _rows(shape):
    return lax.broadcasted_iota(jnp.int32, shape, 0)


def _cols(shape):
    return lax.broadcasted_iota(jnp.int32, shape, 1)


def _shift_down(cur, prev8, s):
    if s == 0:
        return cur
    rc = pltpu.roll(cur, s, 0)
    rp = pltpu.roll(prev8, s, 0)
    top = jnp.where(_rows(prev8.shape) < s, rp, rc[:SUBLANES])
    return jnp.concatenate([top, rc[SUBLANES:]], axis=0)


def _shift_up(cur, next8, s):
    if s == 0:
        return cur
    tt = cur.shape[0]
    rc = pltpu.roll(cur, tt - s, 0)
    rn = pltpu.roll(next8, SUBLANES - s, 0)
    bot = jnp.where(_rows(next8.shape) >= SUBLANES - s, rn, rc[tt - SUBLANES:])
    return jnp.concatenate([rc[:tt - SUBLANES], bot], axis=0)


def _down_fill(x, d, fill):
    return jnp.where(_rows(x.shape) < d, fill, pltpu.roll(x, d, 0))


def _up_fill(x, d, fill):
    tt = x.shape[0]
    return jnp.where(_rows(x.shape) >= tt - d, fill, pltpu.roll(x, tt - d, 0))


def _sigmoid(x):
    return 1.0 / (1.0 + jnp.exp(-x))


def _softplus(x):
    return jnp.maximum(x, 0.0) + jnp.log(1.0 + jnp.exp(-jnp.abs(x)))


def _dot(a, b, dims=(((1,), (0,)), ((), ())), precision=None):
    return lax.dot_general(a, b, dims, preferred_element_type=F32, precision=precision)


NN = (((1,), (0,)), ((), ()))
NT = (((1,), (1,)), ((), ()))
TN = (((0,), (0,)), ((), ()))


def _bdot(a, b, dims=NN):
    return _dot(a.astype(BF16), b.astype(BF16), dims)


def _split(a):
    hi = a.astype(BF16)
    return hi, (a - hi.astype(F32)).astype(BF16)


def _dot3(a, b, dims=NN):
    ah, al = _split(a)
    bh, bl = _split(b)
    return _dot(ah, bh, dims) + (_dot(ah, bl, dims) + _dot(al, bh, dims))


def _tile(dim, target):
    if dim <= target:
        return dim
    best = None
    for c in range(LANES, target + 1, LANES):
        if dim % c == 0:
            best = c
    assert best is not None, (dim, target)
    return best


def _mm(a, b, *, name, ta=False, tb=False, out_dtype=F32, res=None, tm=2048, tn=512, tk=1024):
    m, k = (a.shape[1], a.shape[0]) if ta else a.shape
    n = b.shape[0] if tb else b.shape[1]
    tn, tk = _tile(n, tn), _tile(k, tk)
    tm = _tile(m, tm if max(tn, tk) <= 1024 else tm // 2)
    nk = k // tk
    dims = (((0 if ta else 1,), (1 if tb else 0,)), ((), ()))

    def body(*refs):
        a_ref, b_ref = refs[:2]
        r_ref = refs[2] if res is not None else None
        o_ref = refs[3] if res is not None else refs[2]
        acc = refs[-1]
        kk = pl.program_id(2)
        part = _bdot(a_ref[...], b_ref[...], dims)

        def finish(r):
            if res is not None:
                r = r + r_ref[...]
            o_ref[...] = r.astype(out_dtype)

        if nk == 1:
            finish(part)
            return

        @pl.when(kk == 0)
        def _():
            acc[...] = part

        @pl.when(jnp.logical_and(kk > 0, kk < nk - 1))
        def _():
            acc[...] += part

        @pl.when(kk == nk - 1)
        def _():
            finish(acc[...] + part)

    a_spec = pl.BlockSpec((tk, tm), lambda i, j, kk: (kk, i)) if ta else pl.BlockSpec((tm, tk), lambda i, j, kk: (i, kk))
    b_spec = pl.BlockSpec((tn, tk), lambda i, j, kk: (j, kk)) if tb else pl.BlockSpec((tk, tn), lambda i, j, kk: (kk, j))
    o_spec = pl.BlockSpec((tm, tn), lambda i, j, kk: (i, j))
    in_specs = [a_spec, b_spec] + ([o_spec] if res is not None else [])
    args = (a, b) + ((res,) if res is not None else ())
    return pl.pallas_call(
        body, grid=(m // tm, n // tn, nk), in_specs=in_specs, out_specs=o_spec,
        out_shape=jax.ShapeDtypeStruct((m, n), out_dtype),
        scratch_shapes=[pltpu.VMEM((tm, tn), F32)] if nk > 1 else [], name=name,
        compiler_params=_cp(("parallel", "parallel", "arbitrary"), 56),
    )(*args)


def _norm_fwd(h, gain, *, name, tt=256):
    t, d = h.shape
    tt = min(tt, t)

    def body(h_ref, g_ref, o_ref):
        x = h_ref[...]
        r = lax.rsqrt(jnp.mean(x * x, axis=-1, keepdims=True) + EPS)
        o_ref[...] = (x * r * g_ref[...]).astype(BF16)

    row = pl.BlockSpec((tt, d), lambda i: (i, 0))
    return pl.pallas_call(
        body, grid=(t // tt,), in_specs=[row, pl.BlockSpec((1, d), lambda i: (0, 0))], out_specs=row,
        out_shape=jax.ShapeDtypeStruct((t, d), BF16), name=name, compiler_params=_cp(("parallel",)),
    )(h, gain)


def _norm_bwd(h, gain, dhn, dres, *, name, tt=256):
    t, d = h.shape
    tt = min(tt, t)

    def body(h_ref, g_ref, dy_ref, dr_ref, dx_ref, dg_ref):
        x, dy = h_ref[...], dy_ref[...]
        r = lax.rsqrt(jnp.mean(x * x, axis=-1, keepdims=True) + EPS)
        xh = x * r

        @pl.when(pl.program_id(0) == 0)
        def _():
            dg_ref[...] = jnp.zeros_like(dg_ref)

        dg_ref[...] += jnp.sum(dy * xh, axis=0, keepdims=True)
        dxh = dy * g_ref[...]
        dx_ref[...] = dr_ref[...] + r * (dxh - xh * jnp.mean(dxh * xh, axis=-1, keepdims=True))

    row = pl.BlockSpec((tt, d), lambda i: (i, 0))
    vec = pl.BlockSpec((1, d), lambda i: (0, 0))
    return pl.pallas_call(
        body, grid=(t // tt,), in_specs=[row, vec, row, row], out_specs=[row, vec],
        out_shape=[jax.ShapeDtypeStruct((t, d), F32), jax.ShapeDtypeStruct((1, d), F32)],
        name=name, compiler_params=_cp(("arbitrary",)),
    )(h, gain, dhn, dres)


def _final_loss(h, gain, target, *, name, tt=256):
    t, d = h.shape
    tt = min(tt, t)

    def body(h_ref, g_ref, tg_ref, dx_ref, dg_ref, loss_ref):
        x = h_ref[...]
        r = lax.rsqrt(jnp.mean(x * x, axis=-1, keepdims=True) + EPS)
        xh = x * r
        err = xh * g_ref[...] - tg_ref[...]

        @pl.when(pl.program_id(0) == 0)
        def _():
            dg_ref[...] = jnp.zeros_like(dg_ref)
            loss_ref[...] = jnp.zeros_like(loss_ref)

        loss_ref[...] += 0.5 * jnp.sum(jnp.mean(err * err, axis=-1, keepdims=True), axis=0, keepdims=True)
        dy = err * (1.0 / d)
        dg_ref[...] += jnp.sum(dy * xh, axis=0, keepdims=True)
        dxh = dy * g_ref[...]
        dx_ref[...] = r * (dxh - xh * jnp.mean(dxh * xh, axis=-1, keepdims=True))

    row = pl.BlockSpec((tt, d), lambda i: (i, 0))
    vec = pl.BlockSpec((1, d), lambda i: (0, 0))
    return pl.pallas_call(
        body, grid=(t // tt,), in_specs=[row, vec, row],
        out_specs=[row, vec, pl.BlockSpec((1, 1), lambda i: (0, 0))],
        out_shape=[jax.ShapeDtypeStruct((t, d), F32), jax.ShapeDtypeStruct((1, d), F32), jax.ShapeDtypeStruct((1, 1), F32)],
        name=name, compiler_params=_cp(("arbitrary",)),
    )(h, gain, target)


FFN_CT = 1408
FFN_TT = 256


def _prev8(n, tt):
    return jnp.maximum(n * (tt // SUBLANES) - 1, 0)


def _ffn_conv(cur, prev8, w, b):
    s1 = _shift_down(cur, prev8, 1)
    s2 = _shift_down(cur, prev8, 2)
    return w[0:1] * s2 + w[1:2] * s1 + w[2:3] * cur + b, s1, s2


def _ffn_act_fwd(ug, uv, wg, wv, bg, bv, *, name):
    t = ug.shape[0]
    tt, ct = min(FFN_TT, t), FFN_CT
    nj = D_FF // ct

    def body(ug_ref, uv_ref, pg, pv, wg_ref, wv_ref, bg_ref, bv_ref, o_ref):
        first = pl.program_id(1) == 0
        pgv = jnp.where(first, 0.0, pg[...])
        pvv = jnp.where(first, 0.0, pv[...])
        gate, _, _ = _ffn_conv(ug_ref[...], pgv, wg_ref[...], bg_ref[...])
        val, _, _ = _ffn_conv(uv_ref[...], pvv, wv_ref[...], bv_ref[...])
        o_ref[...] = (gate * _sigmoid(gate) * val).astype(BF16)

    cur = pl.BlockSpec((tt, ct), lambda j, n: (n, j))
    prev = pl.BlockSpec((SUBLANES, ct), lambda j, n: (_prev8(n, tt), j))
    wsp = lambda rows: pl.BlockSpec((rows, ct), lambda j, n: (0, j))
    return pl.pallas_call(
        body, grid=(nj, t // tt),
        in_specs=[cur, cur, prev, prev, wsp(3), wsp(3), wsp(1), wsp(1)], out_specs=cur,
        out_shape=jax.ShapeDtypeStruct((t, D_FF), BF16), name=name,
        compiler_params=_cp(("parallel", "arbitrary"), 48),
    )(ug, uv, ug, uv, wg, wv, bg, bv)


def _ffn_act_bwd(ug, uv, da, wg, wv, bg, bv, *, name):
    t = ug.shape[0]
    tt, ct = min(FFN_TT, t), FFN_CT
    nj = D_FF // ct
    nt = t // tt

    def body(ug_ref, uv_ref, pg, pv, da_ref, wg_ref, wv_ref, bg_ref, bv_ref,
             dug, duv, dwg, dwv, dbg, dbv, head_g, head_v):
        n = pl.program_id(1)
        tile0 = n == nt - 1

        @pl.when(n == 0)
        def _():
            for r in (head_g, head_v, dwg, dwv, dbg, dbv):
                r[...] = jnp.zeros_like(r)

        xg, xv = ug_ref[...], uv_ref[...]
        pgv = jnp.where(tile0, 0.0, pg[...])
        pvv = jnp.where(tile0, 0.0, pv[...])
        gate, g1, g2 = _ffn_conv(xg, pgv, wg_ref[...], bg_ref[...])
        val, v1, v2 = _ffn_conv(xv, pvv, wv_ref[...], bv_ref[...])
        d = da_ref[...]
        sg = _sigmoid(gate)
        dval = d * gate * sg
        dgate = d * val * sg * (1.0 + gate * (1.0 - sg))
        for dc, w_ref, x0, x1, x2, head, du, dw, db in ((dgate, wg_ref, xg, g1, g2, head_g, dug, dwg, dbg),
                                                        (dval, wv_ref, xv, v1, v2, head_v, duv, dwv, dbv)):
            w = w_ref[...]
            hd = head[...]
            du[...] = (w[2:3] * dc + w[1:2] * _shift_up(dc, hd, 1) + w[0:1] * _shift_up(dc, hd, 2)).astype(BF16)
            dw[0:1, :] += jnp.sum(dc * x2, axis=0, keepdims=True)
            dw[1:2, :] += jnp.sum(dc * x1, axis=0, keepdims=True)
            dw[2:3, :] += jnp.sum(dc * x0, axis=0, keepdims=True)
            db[...] += jnp.sum(dc, axis=0, keepdims=True)
            head[...] = dc[:SUBLANES]

    rev = lambda n: nt - 1 - n
    cur = pl.BlockSpec((tt, ct), lambda j, n: (rev(n), j))
    prev = pl.BlockSpec((SUBLANES, ct), lambda j, n: (_prev8(rev(n), tt), j))
    wsp = lambda rows: pl.BlockSpec((rows, ct), lambda j, n: (0, j))
    return pl.pallas_call(
        body, grid=(nj, nt),
        in_specs=[cur, cur, prev, prev, cur, wsp(3), wsp(3), wsp(1), wsp(1)],
        out_specs=[cur, cur, wsp(3), wsp(3), wsp(1), wsp(1)],
        out_shape=[jax.ShapeDtypeStruct((t, D_FF), BF16)] * 2 + [jax.ShapeDtypeStruct((3, D_FF), F32)] * 2
        + [jax.ShapeDtypeStruct((1, D_FF), F32)] * 2,
        scratch_shapes=[pltpu.VMEM((SUBLANES, ct), F32)] * 2, name=name,
        compiler_params=_cp(("parallel", "arbitrary"), 48),
    )(ug, uv, ug, uv, da, wg, wv, bg, bv)


LRU_TT = 256
LRU_CT = 512
GELU_C = math.sqrt(2.0 / math.pi)
GELU_A = 0.044715


def _gelu(x):
    return 0.5 * x * (1.0 + jnp.tanh(GELU_C * (x + GELU_A * x * x * x)))


def _gelu_grad(x):
    th = jnp.tanh(GELU_C * (x + GELU_A * x * x * x))
    return 0.5 * (1.0 + th) + 0.5 * x * (1.0 - th * th) * GELU_C * (1.0 + 3.0 * GELU_A * x * x)


def _neg_expm1(x):
    poly = -x * (1.0 + x * (0.5 + x * (1.0 / 6 + x * (1.0 / 24 + x * (1.0 / 120)))))
    return jnp.where(x > -0.1, poly, 1.0 - jnp.exp(x))


def _conv4(x, p8, w, b=None):
    s1, s2, s3 = _shift_down(x, p8, 1), _shift_down(x, p8, 2), _shift_down(x, p8, 3)
    y = w[0:1] * s3 + w[1:2] * s2 + w[2:3] * s1 + w[3:4] * x
    return (y if b is None else y + b), (s1, s2, s3)


def _conv4_bwd(dy, head, x, shifts, w):
    s1, s2, s3 = shifts
    dx = w[3:4] * dy + w[2:3] * _shift_up(dy, head, 1) + w[1:2] * _shift_up(dy, head, 2) + w[0:1] * _shift_up(dy, head, 3)
    dws = [jnp.sum(dy * s, axis=0, keepdims=True) for s in (s3, s2, s1, x)]
    return dx, dws


def _blockdiag(x, w_ref, dims=NN):
    nb = x.shape[1] // LANES
    return jnp.concatenate([_bdot(x[:, LANES * i:LANES * (i + 1)], w_ref[i], dims) for i in range(nb)], axis=1)


def _lru_gates(xr, wa_ref, wx_ref, ba, bx, lam):
    r = _sigmoid(_blockdiag(xr, wa_ref) + ba)
    i = _sigmoid(_blockdiag(xr, wx_ref) + bx)
    sp = _softplus(-lam)
    la = -LRU_C * r * sp
    a = jnp.exp(la)
    mult = jnp.sqrt(_neg_expm1(2.0 * la))
    return r, i, sp, a, mult


def _lru_specs(t, tt, ct, order):
    nb = ct // LANES
    cur = pl.BlockSpec((tt, ct), lambda j, n: (order(n), j))
    prev = pl.BlockSpec((SUBLANES, ct), lambda j, n: (_prev8(order(n), tt), j))
    vec = lambda rows: pl.BlockSpec((rows, ct), lambda j, n: (0, j))
    blk = pl.BlockSpec((nb, LANES, LANES), lambda j, n: (j, 0, 0))
    return cur, prev, vec, blk


def _lru_fwd(gate, xpre, cw, cb, wa, ba, wx, bx, lam, *, name):
    t, c = gate.shape
    tt, ct = min(LRU_TT, t), LRU_CT
    cur, prev, vec, blk = _lru_specs(t, tt, ct, lambda n: n)

    def body(gate_ref, x_ref, p_ref, cw_ref, cb_ref, wa_ref, ba_ref, wx_ref, bx_ref, lam_ref, y_ref, hs_ref, carry):
        n = pl.program_id(1)

        @pl.when(n == 0)
        def _():
            carry[...] = jnp.zeros_like(carry)

        p8 = jnp.where(n == 0, 0.0, p_ref[...])
        xr, _ = _conv4(x_ref[...], p8, cw_ref[...], cb_ref[...])
        r, i, sp, a, mult = _lru_gates(xr, wa_ref, wx_ref, ba_ref[...], bx_ref[...], lam_ref[...])
        acc_a, acc_b = a, mult * (i * xr)
        d = 1
        while d < tt:
            acc_b = acc_a * _down_fill(acc_b, d, 0.0) + acc_b
            acc_a = acc_a * _down_fill(acc_a, d, 1.0)
            d *= 2
        hs = acc_b + acc_a * carry[0:1]
        carry[...] = jnp.broadcast_to(hs[tt - 1:tt], carry.shape)
        hs_ref[...] = hs
        y_ref[...] = (_gelu(gate_ref[...]) * hs).astype(BF16)

    return pl.pallas_call(
        body, grid=(c // ct, t // tt),
        in_specs=[cur, cur, prev, vec(4), vec(1), blk, vec(1), blk, vec(1), vec(1)],
        out_specs=[cur, cur],
        out_shape=[jax.ShapeDtypeStruct((t, c), BF16), jax.ShapeDtypeStruct((t, c), F32)],
        scratch_shapes=[pltpu.VMEM((SUBLANES, ct), F32)], name=name,
        compiler_params=_cp(("parallel", "arbitrary"), 48),
    )(gate, xpre, xpre, cw, cb, wa, ba, wx, bx, lam)


def _lru_bwd(gate, xpre, hs, dy, cw, cb, wa, ba, wx, bx, lam, *, name):
    t, c = gate.shape
    tt, ct = min(LRU_TT, t), LRU_CT
    nt = t // tt
    cur, prev, vec, blk = _lru_specs(t, tt, ct, lambda n: nt - 1 - n)

    def body(gate_ref, x_ref, p_ref, hs_ref, phs_ref, dy_ref, cw_ref, cb_ref, wa_ref, ba_ref, wx_ref, bx_ref, lam_ref,
             dgate_ref, dx_ref, dcw_ref, dcb_ref, dwa_ref, dba_ref, dwx_ref, dbx_ref, dlam_ref, carry, head):
        n = pl.program_id(1)
        tile0 = n == nt - 1

        @pl.when(n == 0)
        def _():
            for ref in (carry, head, dcw_ref, dcb_ref, dwa_ref, dba_ref, dwx_ref, dbx_ref, dlam_ref):
                ref[...] = jnp.zeros_like(ref)

        xp, cwv, lam = x_ref[...], cw_ref[...], lam_ref[...]
        p8 = jnp.where(tile0, 0.0, p_ref[...])
        xr, shifts = _conv4(xp, p8, cwv, cb_ref[...])
        r, i, sp, a, mult = _lru_gates(xr, wa_ref, wx_ref, ba_ref[...], bx_ref[...], lam)
        gate, hsv, dyv = gate_ref[...], hs_ref[...], dy_ref[...]
        dgate_ref[...] = (dyv * hsv * _gelu_grad(gate)).astype(BF16)
        acc_b = dyv * _gelu(gate) + jnp.where(_rows(a.shape) == tt - 1, carry[0:1], 0.0)
        acc_a = _up_fill(a, 1, 0.0)
        d = 1
        while d < tt:
            acc_b = acc_b + acc_a * _up_fill(acc_b, d, 0.0)
            acc_a = acc_a * _up_fill(acc_a, d, 0.0)
            d *= 2
        gsum = acc_b
        carry[...] = jnp.broadcast_to(a[0:1] * gsum[0:1], carry.shape)
        hprev = _shift_down(hsv, jnp.where(tile0, 0.0, phs_ref[...]), 1)
        da = gsum * hprev
        dmult = gsum * i * xr
        di = gsum * mult * xr
        dxr = gsum * mult * i
        dla = da * a - dmult * (a * a) / mult
        dr = dla * (-LRU_C * sp)
        dlam_ref[...] += jnp.sum(dla * (-LRU_C * r), axis=0, keepdims=True) * (-_sigmoid(-lam))
        dpa = dr * r * (1.0 - r)
        dpx = di * i * (1.0 - i)
        dba_ref[...] += jnp.sum(dpa, axis=0, keepdims=True)
        dbx_ref[...] += jnp.sum(dpx, axis=0, keepdims=True)
        dxr = dxr + _blockdiag(dpa, wa_ref, NT) + _blockdiag(dpx, wx_ref, NT)
        for b in range(ct // LANES):
            sl = slice(LANES * b, LANES * (b + 1))
            dwa_ref[b] += _bdot(xr[:, sl], dpa[:, sl], TN)
            dwx_ref[b] += _bdot(xr[:, sl], dpx[:, sl], TN)
        dx, dws = _conv4_bwd(dxr, head[...], xp, shifts, cwv)
        dx_ref[...] = dx.astype(BF16)
        for k in range(4):
            dcw_ref[k:k + 1, :] += dws[k]
        dcb_ref[...] += jnp.sum(dxr, axis=0, keepdims=True)
        head[...] = dxr[:SUBLANES]

    return pl.pallas_call(
        body, grid=(c // ct, nt),
        in_specs=[cur, cur, prev, cur, prev, cur, vec(4), vec(1), blk, vec(1), blk, vec(1), vec(1)],
        out_specs=[cur, cur, vec(4), vec(1), blk, vec(1), blk, vec(1), vec(1)],
        out_shape=[jax.ShapeDtypeStruct((t, c), BF16)] * 2 + [jax.ShapeDtypeStruct((4, c), F32), jax.ShapeDtypeStruct((1, c), F32),
                   jax.ShapeDtypeStruct(wa.shape, F32), jax.ShapeDtypeStruct((1, c), F32),
                   jax.ShapeDtypeStruct(wx.shape, F32), jax.ShapeDtypeStruct((1, c), F32), jax.ShapeDtypeStruct((1, c), F32)],
        scratch_shapes=[pltpu.VMEM((SUBLANES, ct), F32)] * 2, name=name,
        compiler_params=_cp(("parallel", "arbitrary"), 48),
    )(gate, xpre, xpre, hs, hs, dy, cw, cb, wa, ba, wx, bx, lam)


RET_W = HEADS * HEAD_DIM
HALF = HEAD_DIM // 2


def _ret_tables(t):
    c = RET_CHUNK
    inv_freq = ROPE_BASE ** (-jnp.arange(HALF, dtype=F32) / HALF)
    ang = jnp.arange(t, dtype=jnp.int32).astype(F32)[:, None] * inv_freq[None, :]
    cos, sin = jnp.cos(ang), jnp.sin(ang)
    cosf = jnp.concatenate([cos, cos], axis=1)
    sinf = jnp.concatenate([-sin, sin], axis=1)
    log_gamma = jnp.log1p(-jnp.exp2(-5.0 - jnp.arange(HEADS, dtype=F32)))
    idx = jnp.arange(c, dtype=F32)
    rel = idx[:, None] - idx[None, :]
    causal = rel >= 0
    dmask = jnp.where(causal, jnp.exp(log_gamma[:, None, None] * jnp.where(causal, rel, 0.0)), 0.0)
    ktail = jnp.exp(log_gamma[:, None] * (c - 1 - idx))
    qdec = jnp.exp(log_gamma[:, None] * (idx + 1.0))
    rowtab = jnp.broadcast_to(jnp.stack([ktail, qdec], axis=1)[..., None], (HEADS, 2, c, HEAD_DIM))
    cdec = jnp.broadcast_to(jnp.exp(log_gamma * c)[:, None, None], (HEADS, SUBLANES, HEAD_DIM))
    return cosf, sinf, dmask, rowtab, cdec


def _rotary(x, cosf, sinf):
    return x * cosf + pltpu.roll(x, HALF, 1) * sinf


def _rotary_t(dx, cosf, sinf):
    return dx * cosf + pltpu.roll(dx * sinf, HALF, 1)


def _ret_specs(c, order):
    full = lambda shape: pl.BlockSpec(shape, lambda n: (0,) * len(shape))
    return dict(
        proj=pl.BlockSpec((c, 4 * RET_W), lambda n: (order(n), 0)),
        rot=pl.BlockSpec((c, HEAD_DIM), lambda n: (order(n), 0)),
        dmask=full((HEADS, c, c)), rowtab=full((HEADS, 2, c, HEAD_DIM)), cdec=full((HEADS, SUBLANES, HEAD_DIM)),
        state=pl.BlockSpec((1, HEADS, HEAD_DIM, HEAD_DIM), lambda n: (order(n), 0, 0, 0)),
        half=pl.BlockSpec((c, RET_W), lambda n: (order(n), 0)),
    )


def _ret_head(p_ref, h, cosf, sinf):
    sl = lambda j: slice(j * RET_W + h * HEAD_DIM, j * RET_W + (h + 1) * HEAD_DIM)
    q, k, v, g = p_ref[:, sl(0)], p_ref[:, sl(1)], p_ref[:, sl(2)], p_ref[:, sl(3)]
    return _rotary(q, cosf, sinf), _rotary(k, cosf, sinf) * QSCALE, v, g


def _ret_fwd(proj, tables, *, name):
    t = proj.shape[0]
    c = RET_CHUNK
    nc = t // c
    sp = _ret_specs(c, lambda n: n)

    def body(p_ref, cos_ref, sin_ref, dm_ref, rt_ref, cd_ref, y_ref, s_ref, state):
        @pl.when(pl.program_id(0) == 0)
        def _():
            state[...] = jnp.zeros_like(state)

        cosf, sinf = cos_ref[...], sin_ref[...]
        for h in range(HEADS):
            qr, kr, v, g = _ret_head(p_ref, h, cosf, sinf)
            s0 = state[h]
            s_ref[0, h] = s0
            scores = _bdot(qr, kr, NT) * dm_ref[h]
            o = _bdot(scores, v) + _bdot(qr * rt_ref[h, 1], s0)
            state[h] = s0 * cd_ref[h][0:1] + _bdot(kr * rt_ref[h, 0], v, TN)
            rinv = lax.rsqrt(jnp.mean(o * o, axis=-1, keepdims=True) + EPS)
            y_ref[:, h * HEAD_DIM:(h + 1) * HEAD_DIM] = (o * rinv * (g * _sigmoid(g))).astype(BF16)

    return pl.pallas_call(
        body, grid=(nc,),
        in_specs=[sp["proj"], sp["rot"], sp["rot"], sp["dmask"], sp["rowtab"], sp["cdec"]],
        out_specs=[sp["half"], sp["state"]],
        out_shape=[jax.ShapeDtypeStruct((t, 2 * RET_W), BF16), jax.ShapeDtypeStruct((nc, HEADS, HEAD_DIM, HEAD_DIM), F32)],
        scratch_shapes=[pltpu.VMEM((HEADS, HEAD_DIM, HEAD_DIM), F32)], name=name,
        compiler_params=_cp(("arbitrary",), 48),
    )(proj, *tables)


def _ret_bwd(proj, tables, states, dy, *, name):
    t = proj.shape[0]
    c = RET_CHUNK
    nc = t // c
    sp = _ret_specs(c, lambda n: nc - 1 - n)

    def body(p_ref, cos_ref, sin_ref, dm_ref, rt_ref, cd_ref, s_ref, dy_ref, dp_ref, dstate):
        @pl.when(pl.program_id(0) == 0)
        def _():
            dstate[...] = jnp.zeros_like(dstate)

        cosf, sinf = cos_ref[...], sin_ref[...]
        for h in range(HEADS):
            qr, kr, v, g = _ret_head(p_ref, h, cosf, sinf)
            s0, dm, ktl, qdc = s_ref[0, h], dm_ref[h], rt_ref[h, 0], rt_ref[h, 1]
            scores = _bdot(qr, kr, NT) * dm
            qd, kt = qr * qdc, kr * ktl
            o = _bdot(scores, v) + _bdot(qd, s0)
            rinv = lax.rsqrt(jnp.mean(o * o, axis=-1, keepdims=True) + EPS)
            oh = o * rinv
            sg = _sigmoid(g)
            dyh = dy_ref[:, h * HEAD_DIM:(h + 1) * HEAD_DIM]
            dg = dyh * oh * sg * (1.0 + g * (1.0 - sg))
            dyo = dyh * (g * sg)
            do = rinv * (dyo - oh * jnp.mean(dyo * oh, axis=-1, keepdims=True))
            ds1 = dstate[h]
            dsc = _bdot(do, v, NT) * dm
            dv = _bdot(scores, do, TN) + _bdot(kt, ds1)
            dqr = _bdot(dsc, kr) + _bdot(do, s0, NT) * qdc
            dkr = (_bdot(dsc, qr, TN) + _bdot(v, ds1, NT) * ktl) * QSCALE
            dstate[h] = ds1 * cd_ref[h][0:1] + _bdot(qd, do, TN)
            pieces = (_rotary_t(dqr, cosf, sinf), _rotary_t(dkr, cosf, sinf), dv, dg)
            for j, piece in enumerate(pieces):
                dp_ref[:, j * RET_W + h * HEAD_DIM:j * RET_W + (h + 1) * HEAD_DIM] = piece.astype(BF16)

    return pl.pallas_call(
        body, grid=(nc,),
        in_specs=[sp["proj"], sp["rot"], sp["rot"], sp["dmask"], sp["rowtab"], sp["cdec"], sp["state"], sp["half"]],
        out_specs=sp["proj"],
        out_shape=jax.ShapeDtypeStruct((t, 8 * RET_W), BF16),
        scratch_shapes=[pltpu.VMEM((HEADS, HEAD_DIM, HEAD_DIM), F32)], name=name,
        compiler_params=_cp(("arbitrary",), 48),
    )(proj, *tables, states, dy)


GDN_W = HEADS * HEAD_DIM
GDN_CONV = 3 * GDN_W
NEUMANN_STEPS = 5


def _gdn_gates(ps, al, dt):
    return _sigmoid(ps), -jnp.exp(al) * _softplus(ps + dt)


def _cumsum_rows(x):
    d = 1
    while d < x.shape[0]:
        x = x + _down_fill(x, d, 0.0)
        d *= 2
    return x


def _rev_cumsum_rows(x):
    d = 1
    while d < x.shape[0]:
        x = x + _up_fill(x, d, 0.0)
        d *= 2
    return x


class _Chunk:
    pass


def _gdn_chunk(qc, kc, v, beta, g, s0):
    c = GDN_CHUNK
    z = _Chunk()
    z.rq = lax.rsqrt(jnp.sum(qc * qc, axis=-1, keepdims=True) + EPS)
    z.rk = lax.rsqrt(jnp.sum(kc * kc, axis=-1, keepdims=True) + EPS)
    z.qn, z.k = qc * z.rq, kc * z.rk
    z.q = z.qn * QSCALE
    z.v, z.beta = v, beta
    gc = _cumsum_rows(jnp.broadcast_to(g, (c, LANES)))
    ri, ci = _rows((c, c)), _cols((c, c))
    z.tril, z.strict = ri >= ci, ri > ci
    diff = gc[:, :c] - gc.T[:c, :]
    z.decay = jnp.where(z.tril, jnp.exp(jnp.where(z.tril, diff, 0.0)), 0.0)
    z.eg = jnp.exp(gc)
    glast = gc[c - 1:c, :]
    z.egl = jnp.exp(glast - gc)
    z.cd = jnp.exp(glast)
    z.kb = z.k * beta
    z.m = _bdot(z.kb, z.k, NT)
    lmat = jnp.where(z.strict, z.m * z.decay, 0.0)
    neg = -lmat
    inv = (ri == ci).astype(F32) + neg
    pw = neg
    for _ in range(NEUMANN_STEPS):
        pw = _dot3(pw, pw)
        inv = inv + _dot3(inv, pw)
    z.inv = inv
    z.vb, z.kbg = v * beta, z.kb * z.eg
    z.u = _dot3(inv, z.vb)
    z.w = _dot3(inv, z.kbg)
    z.qk = _bdot(z.q, z.k, NT)
    z.attn = jnp.where(z.tril, z.qk * z.decay, 0.0)
    z.qd, z.kt = z.q * z.eg, z.k * z.egl
    z.vnew = z.u - _bdot(z.w, s0)
    z.o = _bdot(z.qd, s0) + _bdot(z.attn, z.vnew)
    z.s1 = s0 * z.cd + _bdot(z.kt, z.vnew, TN)
    return z


def _gdn_chunk_bwd(z, s0, do, ds1):
    c = GDN_CHUNK
    dvnew = _bdot(z.attn, do, TN) + _bdot(z.kt, ds1)
    dqd = _bdot(do, s0, NT)
    dattn = jnp.where(z.tril, _bdot(do, z.vnew, NT), 0.0)
    ds0 = _bdot(z.qd, do, TN) + ds1 * z.cd - _bdot(z.w, dvnew, TN)
    dcd = jnp.sum(jnp.sum(s0 * ds1, axis=1, keepdims=True), axis=0, keepdims=True)
    dkt = _bdot(z.vnew, ds1, NT)
    dw = -_bdot(dvnew, s0, NT)
    dvb = _dot3(z.inv, dvnew, TN)
    dkbg = _dot3(z.inv, dw, TN)
    dl = jnp.where(z.strict, -(_bdot(dvb, z.u, NT) + _bdot(dkbg, z.w, NT)), 0.0)
    dml = dl * z.decay
    dqk = dattn * z.decay
    ddecay = (dl * z.m + dattn * z.qk) * z.decay
    dq = _bdot(dqk, z.k) + dqd * z.eg
    dkb = _bdot(dml, z.k) + dkbg * z.eg
    dk = _bdot(dqk, z.q, TN) + _bdot(dml, z.kb, TN) + dkt * z.egl + dkb * z.beta
    dbeta = jnp.sum(dkb * z.k, axis=-1, keepdims=True) + jnp.sum(dvb * z.v, axis=-1, keepdims=True)
    dv = dvb * z.beta
    colsum = _dot3(ddecay, jnp.ones((c, LANES), F32), TN)
    e = jnp.sum(dkt * z.kt, axis=-1, keepdims=True)
    dgc = (jnp.sum(ddecay, axis=-1, keepdims=True) - colsum
           + jnp.sum(dkbg * z.kbg, axis=-1, keepdims=True) + jnp.sum(dqd * z.qd, axis=-1, keepdims=True) - e)
    dglast = jnp.sum(e, axis=0, keepdims=True) + dcd * z.cd
    dgc = dgc + jnp.where(_rows((c, LANES)) == c - 1, dglast, 0.0)
    dg = _rev_cumsum_rows(dgc)[:, 0:1]
    dqn = dq * QSCALE
    dqc = z.rq * (dqn - z.qn * jnp.sum(dqn * z.qn, axis=-1, keepdims=True))
    dkc = z.rk * (dk - z.k * jnp.sum(dk * z.k, axis=-1, keepdims=True))
    return dqc, dkc, dv, dbeta, dg, ds0


def _gdn_specs(c, order):
    full = lambda shape: pl.BlockSpec(shape, lambda n: (0,) * len(shape))
    return dict(
        proj=pl.BlockSpec((c, 4 * GDN_W), lambda n: (order(n), 1)),
        prev=pl.BlockSpec((SUBLANES, 4 * GDN_W), lambda n: (_prev8(order(n), c), 1)),
        small=pl.BlockSpec((c, LANES), lambda n: (order(n), 0)),
        convw=full((4, GDN_CONV)), vec=full((1, LANES)),
        state=pl.BlockSpec((1, HEADS, HEAD_DIM, HEAD_DIM), lambda n: (order(n), 0, 0, 0)),
        half=pl.BlockSpec((c, GDN_W), lambda n: (order(n), 1)),
        any=pl.BlockSpec(memory_space=pl.ANY),
    )


def _gdn_fwd(proj, psmall, conv_w, al, dt, gain, y_in, comm, *, name):
    t = proj.shape[0]
    c = GDN_CHUNK
    nc = t // c
    sp = _gdn_specs(c, lambda n: n)

    def body(p_ref, prev_ref, ps_ref, cw_ref, al_ref, dt_ref, gain_ref, yin_ref, *rest):
        comm_in, (y_ref, s_ref), comm_out, (state,), comm_sems = comm.split(rest, n_out=2, n_scratch=1)
        n = pl.program_id(0)

        @pl.when(n == 0)
        def _():
            state[...] = jnp.zeros_like(state)
            comm.start(comm_in, comm_out, comm_sems)

        p8 = jnp.where(n == 0, 0.0, prev_ref[:, :GDN_CONV])
        pre, _ = _conv4(p_ref[:, :GDN_CONV], p8, cw_ref[...])
        act = pre * _sigmoid(pre)
        beta_all, g_all = _gdn_gates(ps_ref[...], al_ref[...], dt_ref[...])
        for h in range(HEADS):
            sl = lambda j: slice(j * GDN_W + h * HEAD_DIM, j * GDN_W + (h + 1) * HEAD_DIM)
            s0 = state[h]
            s_ref[0, h] = s0
            z = _gdn_chunk(act[:, sl(0)], act[:, sl(1)], act[:, sl(2)], beta_all[:, h:h + 1], g_all[:, HEADS + h:HEADS + h + 1], s0)
            state[h] = z.s1
            rinv = lax.rsqrt(jnp.mean(z.o * z.o, axis=-1, keepdims=True) + EPS)
            gd = p_ref[:, sl(3)]
            y_ref[:, sl(0)] = (z.o * rinv * gain_ref[...] * (gd * _sigmoid(gd))).astype(BF16)

        @pl.when(n == nc - 1)
        def _():
            comm.finish(comm_in, comm_out, comm_sems)

    outs = pl.pallas_call(
        body, grid=(nc,),
        in_specs=[sp["proj"], sp["prev"], sp["small"], sp["convw"], sp["vec"], sp["vec"], sp["vec"], sp["any"]]
        + [sp["any"]] * len(comm.arrays),
        out_specs=[sp["half"], sp["state"]] + [sp["any"]] * len(comm.out_shapes),
        out_shape=[jax.ShapeDtypeStruct((t, 2 * GDN_W), BF16), jax.ShapeDtypeStruct((nc, HEADS, HEAD_DIM, HEAD_DIM), F32)]
        + comm.out_shapes,
        scratch_shapes=[pltpu.VMEM((HEADS, HEAD_DIM, HEAD_DIM), F32)] + comm.scratch, name=name,
        input_output_aliases={7: 0}, compiler_params=_cp(("arbitrary",), 48),
    )(proj, proj, psmall, conv_w, al, dt, gain, y_in, *comm.arrays)
    return outs[0], outs[1], outs[2:]


def _gdn_bwd(proj, psmall, conv_w, al, dt, gain, states, dy, dproj_in, comm, *, name):
    t = proj.shape[0]
    c = GDN_CHUNK
    nc = t // c
    sp = _gdn_specs(c, lambda n: nc - 1 - n)

    def body(p_ref, prev_ref, ps_ref, cw_ref, al_ref, dt_ref, gain_ref, s_ref, dy_ref, dpin_ref, *rest):
        comm_in, outs, comm_out, (dstate, head), comm_sems = comm.split(rest, n_out=6, n_scratch=2)
        dp_ref, dps_ref, dcw_ref, dal_ref, ddt_ref, dgain_ref = outs
        n = pl.program_id(0)
        chunk0 = n == nc - 1

        @pl.when(n == 0)
        def _():
            for ref in (dstate, head, dcw_ref, dal_ref, ddt_ref, dgain_ref):
                ref[...] = jnp.zeros_like(ref)
            comm.start(comm_in, comm_out, comm_sems)

        x, cwv = p_ref[:, :GDN_CONV], cw_ref[...]
        p8 = jnp.where(chunk0, 0.0, prev_ref[:, :GDN_CONV])
        pre, shifts = _conv4(x, p8, cwv)
        sg_pre = _sigmoid(pre)
        act = pre * sg_pre
        ps, alv, dtv, gain = ps_ref[...], al_ref[...], dt_ref[...], gain_ref[...]
        beta_all, g_all = _gdn_gates(ps, alv, dtv)
        lane = _cols((c, LANES))
        dbeta_all = jnp.zeros((c, LANES), F32)
        dg_all = jnp.zeros((c, LANES), F32)
        dact = [None] * (3 * HEADS)
        dgain = jnp.zeros((1, LANES), F32)
        for h in range(HEADS):
            sl = lambda j: slice(j * GDN_W + h * HEAD_DIM, j * GDN_W + (h + 1) * HEAD_DIM)
            s0 = s_ref[0, h]
            z = _gdn_chunk(act[:, sl(0)], act[:, sl(1)], act[:, sl(2)], beta_all[:, h:h + 1], g_all[:, HEADS + h:HEADS + h + 1], s0)
            rinv = lax.rsqrt(jnp.mean(z.o * z.o, axis=-1, keepdims=True) + EPS)
            oh = z.o * rinv
            gd = p_ref[:, sl(3)]
            sgd = _sigmoid(gd)
            dyh = dy_ref[:, sl(0)]
            dgain = dgain + jnp.sum(dyh * oh * (gd * sgd), axis=0, keepdims=True)
            dp_ref[:, sl(3)] = (dyh * oh * gain * sgd * (1.0 + gd * (1.0 - sgd))).astype(BF16)
            dyo = dyh * gain * (gd * sgd)
            do = rinv * (dyo - oh * jnp.mean(dyo * oh, axis=-1, keepdims=True))
            dqc, dkc, dv, dbeta, dg, ds0 = _gdn_chunk_bwd(z, s0, do, dstate[h])
            dstate[h] = ds0
            dact[h], dact[HEADS + h], dact[2 * HEADS + h] = dqc, dkc, dv
            dbeta_all = dbeta_all + jnp.where(lane == h, dbeta, 0.0)
            dg_all = dg_all + jnp.where(lane == HEADS + h, dg, 0.0)
        dpre = jnp.concatenate(dact, axis=1) * sg_pre * (1.0 + pre * (1.0 - sg_pre))
        dx, dws = _conv4_bwd(dpre, head[...], x, shifts, cwv)
        dp_ref[:, :GDN_CONV] = dx.astype(BF16)
        for k in range(4):
            dcw_ref[k:k + 1, :] += dws[k]
        head[...] = dpre[:SUBLANES]
        dsp = dg_all * (-jnp.exp(alv)) * _sigmoid(ps + dtv)
        dps_ref[...] = (dbeta_all * beta_all * (1.0 - beta_all) + dsp).astype(BF16)
        ddt_ref[...] += jnp.sum(dsp, axis=0, keepdims=True)
        dal_ref[...] += jnp.sum(dg_all * g_all, axis=0, keepdims=True)
        dgain_ref[...] += dgain

        @pl.when(n == nc - 1)
        def _():
            comm.finish(comm_in, comm_out, comm_sems)

    vec_f32 = jax.ShapeDtypeStruct((1, LANES), F32)
    outs = pl.pallas_call(
        body, grid=(nc,),
        in_specs=[sp["proj"], sp["prev"], sp["small"], sp["convw"], sp["vec"], sp["vec"], sp["vec"], sp["state"], sp["half"], sp["any"]]
        + [sp["any"]] * len(comm.arrays),
        out_specs=[sp["proj"], sp["small"], sp["convw"], sp["vec"], sp["vec"], sp["vec"]] + [sp["any"]] * len(comm.out_shapes),
        out_shape=[jax.ShapeDtypeStruct((t, 8 * GDN_W), BF16), jax.ShapeDtypeStruct((t, LANES), BF16),
                   jax.ShapeDtypeStruct((4, GDN_CONV), F32), vec_f32, vec_f32, vec_f32] + comm.out_shapes,
        scratch_shapes=[pltpu.VMEM((HEADS, HEAD_DIM, HEAD_DIM), F32), pltpu.VMEM((SUBLANES, GDN_CONV), F32)] + comm.scratch,
        name=name, input_output_aliases={9: 0}, compiler_params=_cp(("arbitrary",), 48),
    )(proj, proj, psmall, conv_w, al, dt, gain, states, dy, dproj_in, *comm.arrays)
    return outs[:6], outs[6:]


def _here():
    x, y, c = lax.axis_index("x"), lax.axis_index("y"), lax.axis_index("c")
    return x, y, c, [(1 - x, y), (x, 1 - y), (1 - x, 1 - y)]


def _rdma(src, dst, send, recv, k, dev):
    return pltpu.make_async_remote_copy(src_ref=src, dst_ref=dst, send_sem=send.at[k], recv_sem=recv.at[k],
                                        device_id=dev, device_id_type=MESH)


def _dma_sems(n):
    return [pltpu.SemaphoreType.DMA((n,)), pltpu.SemaphoreType.DMA((n,)), pltpu.SemaphoreType.DMA((1,))]


COPY_PIECES = 4
COPY_PIECE_ALIGN = 16


def _row_parts(rows):
    n = COPY_PIECES if rows % (COPY_PIECES * COPY_PIECE_ALIGN) == 0 and rows >= 1024 else 1
    return [pl.ds(q * (rows // n), rows // n) for q in range(n)]


class _AllGather:
    def __init__(self, array):
        self.arrays = [array]
        self.out_shapes = [jax.ShapeDtypeStruct((N_DEV,) + array.shape, array.dtype)]
        self.parts = _row_parts(array.shape[0])
        self.scratch = _dma_sems(7 * len(self.parts))

    def start(self, ins, outs, sems):
        (src,), (out,), (send, recv, loc) = ins, outs, sems
        x, y, c, chips = _here()
        me, n = 4 * x + 2 * y + c, len(self.parts)
        pltpu.make_async_copy(src, out.at[me], loc.at[0]).start()
        for q, part in enumerate(self.parts):
            _rdma(src.at[part], out.at[me, part], send, recv, q, (x, y, 1 - c)).start()
            for j, (cx, cy) in enumerate(chips):
                _rdma(src.at[part], out.at[me, part], send, recv, (1 + j) * n + q, (cx, cy, c)).start()

    def finish(self, ins, outs, sems):
        (src,), (out,), (send, recv, loc) = ins, outs, sems
        x, y, c, chips = _here()
        sibling, me, n = (x, y, 1 - c), 4 * x + 2 * y + c, len(self.parts)
        piece = lambda k, q: _rdma(src.at[self.parts[q]], out.at[me, self.parts[q]], send, recv, k * n + q, sibling)
        for j, (cx, cy) in enumerate(chips):
            for q, part in enumerate(self.parts):
                got = out.at[4 * cx + 2 * cy + c, part]
                piece(1 + j, q).wait_recv()
                _rdma(got, got, send, recv, (4 + j) * n + q, sibling).start()
        for k in (0, 4, 5, 6):
            for q in range(n):
                piece(k, q).wait_recv()
        for k in range(7):
            for q in range(n):
                piece(k, q).wait_send()
        pltpu.make_async_copy(src, out.at[me], loc.at[0]).wait()


class _ChipExchange:
    def __init__(self, array):
        self.arrays = [array]
        self.out_shapes = [jax.ShapeDtypeStruct(array.shape, array.dtype)]
        self.parts = _row_parts(array.shape[1])
        self.scratch = _dma_sems(3 * len(self.parts))

    def _copies(self, ins, outs, sems):
        (src,), (out,), (send, recv, loc) = ins, outs, sems
        x, y, c, chips = _here()
        here, n = 2 * x + y, len(self.parts)
        local = pltpu.make_async_copy(src.at[here], out.at[here], loc.at[0])
        return local, [_rdma(src.at[2 * cx + cy, part], out.at[here, part], send, recv, j * n + q, (cx, cy, c))
                       for j, (cx, cy) in enumerate(chips) for q, part in enumerate(self.parts)]

    def start(self, ins, outs, sems):
        local, remote = self._copies(ins, outs, sems)
        local.start()
        for cp in remote:
            cp.start()

    def finish(self, ins, outs, sems):
        local, remote = self._copies(ins, outs, sems)
        for cp in remote:
            cp.wait()
        local.wait()


class _PairSwap:
    def __init__(self, array):
        self.arrays = [array]
        self.out_shapes = [jax.ShapeDtypeStruct(array.shape[1:], array.dtype)]
        self.parts = _row_parts(array.shape[2])
        self.scratch = _dma_sems(4 * len(self.parts))

    def _copies(self, ins, outs, sems):
        (src,), (theirs,), (send, recv, _) = ins, outs, sems
        x, y, c, _ = _here()
        return [_rdma(src.at[1 - c, p, part], theirs.at[p, part], send, recv, p * len(self.parts) + q, (x, y, 1 - c))
                for p in range(4) for q, part in enumerate(self.parts)]

    def start(self, ins, outs, sems):
        for cp in self._copies(ins, outs, sems):
            cp.start()

    def finish(self, ins, outs, sems):
        for cp in self._copies(ins, outs, sems):
            cp.wait()


class _Comm:
    def __init__(self, ops):
        self.ops = ops
        self.arrays = [a for op in ops for a in op.arrays]
        self.out_shapes = [s for op in ops for s in op.out_shapes]
        self.scratch = [s for op in ops for s in op.scratch]

    def split(self, rest, n_out, n_scratch):
        cuts = np.cumsum([0, len(self.arrays), n_out, len(self.out_shapes), n_scratch, len(self.scratch)])
        assert cuts[-1] == len(rest)
        return tuple(rest[a:b] for a, b in zip(cuts[:-1], cuts[1:]))

    def _each(self, method, ins, outs, sems):
        i = o = s = 0
        for op in self.ops:
            ni, no, ns = len(op.arrays), len(op.out_shapes), len(op.scratch)
            getattr(op, method)(ins[i:i + ni], outs[o:o + no], sems[s:s + ns])
            i, o, s = i + ni, o + no, s + ns

    def start(self, ins, outs, sems):
        self._each("start", ins, outs, sems)

    def finish(self, ins, outs, sems):
        self._each("finish", ins, outs, sems)

    def run(self, name):
        def body(*refs):
            ins, _, outs, _, sems = self.split(refs, 0, 0)
            self.start(ins, outs, sems)
            self.finish(ins, outs, sems)

        hbm = pl.BlockSpec(memory_space=pl.ANY)
        return pl.pallas_call(body, in_specs=[hbm] * len(self.arrays), out_specs=[hbm] * len(self.out_shapes),
                              out_shape=self.out_shapes, scratch_shapes=self.scratch, name=name)(*self.arrays)


def _sum_slots(x, *, name, tr=None):
    n, r, l = x.shape
    tr = r if tr is None else tr

    def body(x_ref, o_ref):
        acc = x_ref[0].astype(F32)
        for s in range(1, n):
            acc = acc + x_ref[s].astype(F32)
        o_ref[...] = acc

    return pl.pallas_call(
        body, grid=(r // tr,), in_specs=[pl.BlockSpec((n, tr, l), lambda i: (0, i, 0))],
        out_specs=pl.BlockSpec((tr, l), lambda i: (i, 0)), out_shape=jax.ShapeDtypeStruct((r, l), F32),
        name=name, compiler_params=_cp(("parallel",), 48),
    )(x)


def _pair_add(both, theirs, *, name, tr):
    _, n, r, l = both.shape

    def body(a_ref, b_ref, o_ref):
        mine = jnp.where(lax.axis_index("c") == 0, a_ref[0], a_ref[1])
        o_ref[...] = (mine.astype(F32) + b_ref[...].astype(F32)).astype(BF16)

    spec = pl.BlockSpec((n, tr, l), lambda i: (0, i, 0))
    return pl.pallas_call(body, grid=(r // tr,), in_specs=[pl.BlockSpec((2, n, tr, l), lambda i: (0, 0, i, 0)), spec], out_specs=spec,
                          out_shape=jax.ShapeDtypeStruct(theirs.shape, BF16), name=name,
                          compiler_params=_cp(("parallel",), 48))(both, theirs)


ADAM_TILE_ELEMS = 512 * 1024


def _adam(w, g, m, v, *, name):
    shape = w.shape
    cols = shape[-1]
    rows = math.prod(shape[:-1]) if len(shape) > 1 else 1
    tr = rows
    if rows * cols > ADAM_TILE_ELEMS:
        tr = max(d for d in range(SUBLANES, ADAM_TILE_ELEMS // cols + 1, SUBLANES) if rows % d == 0)
    c1, c2 = 1.0 - ADAM_B1 ** ADAM_STEP, 1.0 - ADAM_B2 ** ADAM_STEP

    def body(w_ref, g_ref, m_ref, v_ref, d_ref, m2_ref, v2_ref):
        gv = g_ref[...]
        m2 = ADAM_B1 * m_ref[...] + (1.0 - ADAM_B1) * gv
        v2 = ADAM_B2 * v_ref[...] + (1.0 - ADAM_B2) * (gv * gv)
        d_ref[...] = -ADAM_LR * ((m2 / c1) / (jnp.sqrt(v2 / c2) + ADAM_EPS) + ADAM_WD * w_ref[...])
        m2_ref[...] = m2
        v2_ref[...] = v2

    spec = pl.BlockSpec((tr, cols), lambda i: (i, 0))
    outs = pl.pallas_call(
        body, grid=(rows // tr,), in_specs=[spec] * 4, out_specs=[spec] * 3,
        out_shape=[jax.ShapeDtypeStruct((rows, cols), F32)] * 3, name=name, compiler_params=_cp(("parallel",), 48),
    )(*(a.reshape(rows, cols) for a in (w, g, m, v)))
    return tuple(o.reshape(shape) for o in outs)


WEIGHTS = ['norm_mix', 'norm_ffn', 'ret_gdn_w_in', 'gdn_conv_w', 'gdn_a_log', 'gdn_dt_bias', 'gdn_out_gain', 'ret_gdn_w_out',
           'lru_w_in', 'lru_conv_w', 'lru_conv_b', 'lru_w_a', 'lru_b_a', 'lru_w_x', 'lru_b_x', 'lru_lambda', 'lru_w_out',
           'ffn_w_up', 'ffn_conv_w', 'ffn_conv_b', 'ffn_w_down', 'norm_final']
BIG = {'ret_gdn_w_in': ((1, 1024, 513), 2), 'ret_gdn_w_out': ((1, 128, 1024), 1), 'lru_w_in': ((1, 1024, 256), 2),
       'lru_w_out': ((1, 128, 1024), 1), 'ffn_w_up': ((2, 1024, 704), 2), 'ffn_w_down': ((2, 352, 1024), 1)}
SMALL = {'gdn_conv_w': ((1, 4, 192), 2), 'lru_conv_w': ((1, 4, 128), 2), 'lru_conv_b': ((1, 128), 1), 'lru_b_a': ((1, 128), 1),
         'lru_b_x': ((1, 128), 1), 'lru_lambda': ((1, 128), 1), 'ffn_conv_w': ((2, 3, 704), 2)}
REPLICATED = {'norm_mix': (2, 1024), 'norm_ffn': (2, 1024), 'gdn_a_log': (1, 4), 'gdn_dt_bias': (1, 4), 'gdn_out_gain': (1, 128),
              'lru_w_a': (1, 8, 128, 128), 'lru_w_x': (1, 8, 128, 128), 'ffn_conv_b': (2, 5632), 'norm_final': (1024,)}
FIRST = ['ret_gdn_w_in', 'ret_gdn_w_out']
REST = ['lru_w_in', 'lru_w_out', 'ffn_w_up', 'ffn_w_down']
GROUP_ROWS = {FIRST[0]: 5632, REST[0]: 19968}
GROUP_TILE = {FIRST[0]: 512, REST[0]: 1536}
EARLY = {'lru_conv_w': (1, 4, 1024), 'lru_conv_b': (1, 1024), 'lru_b_a': (1, 1024), 'lru_b_x': (1, 1024), 'lru_lambda': (1, 1024),
         'ffn_conv_w': (2, 3, 5632), 'norm_ffn': (2, 1024), 'norm_mix1': (1, 1024), 'lru_w_a': (1, 8, 128, 128),
         'lru_w_x': (1, 8, 128, 128), 'ffn_conv_b': (2, 5632), 'norm_final': (1024,)}
LATE = {'gdn_conv_w': (1, 4, 1536), 'norm_mix0': (1, 1024), 'gdn_a_log': (1, 4), 'gdn_dt_bias': (1, 4), 'gdn_out_gain': (1, 128)}


def _full_shape(shard, axis):
    return tuple(d * N_DEV if i == axis else d for i, d in enumerate(shard))


def _rows_of(n_elems):
    return -(-n_elems // LANES)


def _to_rows(a, lead=()):
    flat = a.reshape(lead + (-1,))
    pad = _rows_of(flat.shape[-1]) * LANES - flat.shape[-1]
    if pad:
        flat = jnp.pad(flat, [(0, 0)] * len(lead) + [(0, pad)])
    return flat.reshape(lead + (-1, LANES))


def _pack(pieces, total_rows, lead=()):
    buf = jnp.concatenate(pieces, axis=len(lead))
    pad = total_rows - buf.shape[len(lead)]
    return jnp.pad(buf, [(0, 0)] * len(lead) + [(0, pad), (0, 0)]) if pad else buf


def _unpack(buf, shapes, lead=()):
    out, off = [], 0
    for shape in shapes:
        n = math.prod(shape)
        rows = _rows_of(n)
        piece = lax.slice_in_dim(buf, off, off + rows, axis=len(lead)).reshape(lead + (rows * LANES,))
        out.append(lax.slice_in_dim(piece, 0, n, axis=len(lead)).reshape(lead + shape))
        off += rows
    return out


def _join_blocks(g, axis):
    m = jnp.moveaxis(g, 0, axis)
    return m.reshape(m.shape[:axis] + (N_DEV * m.shape[axis + 1],) + m.shape[axis + 2:])


def _split_blocks(full, axis):
    s = full.shape
    return jnp.moveaxis(full.reshape(s[:axis] + (N_DEV, s[axis] // N_DEV) + s[axis + 1:]), axis, 0)


def _small_rows(shapes):
    total = sum(_rows_of(math.prod(s)) for s in shapes)
    return -(-total // SUBLANES) * SUBLANES


def _ffn_forward(h, gain, wug, wuv, cwg, cwv, cbg, cbv, wd, tag):
    hn = _norm_fwd(h, gain, name=f"ffn{tag}_norm")
    ug = _mm(hn, wug, name=f"ffn{tag}_up_gate", tn=1408)
    uv = _mm(hn, wuv, name=f"ffn{tag}_up_val", tn=1408)
    act = _ffn_act_fwd(ug, uv, cwg, cwv, cbg, cbv, name=f"ffn{tag}_act")
    out = _mm(act, wd, res=h, name=f"ffn{tag}_down", tk=1408)
    return out, (hn, ug, uv, act)


def _ffn_backward(dh, h, gain, saved, wug, wuv, cwg, cwv, cbg, cbv, wd, tag):
    hn, ug, uv, act = saved
    da = _mm(dh, wd, tb=True, name=f"ffn{tag}_d_act", tn=1408)
    dwd = _mm(act, dh, ta=True, out_dtype=BF16, name=f"ffn{tag}_d_wdown", tm=1408)
    dug, duv, dcwg, dcwv, dcbg, dcbv = _ffn_act_bwd(ug, uv, da, cwg, cwv, cbg, cbv, name=f"ffn{tag}_act_bwd")
    dhn = _mm(dug, wug, tb=True, name=f"ffn{tag}_d_hn_gate", tk=1408)
    dhn = _mm(duv, wuv, tb=True, res=dhn, name=f"ffn{tag}_d_hn_val", tk=1408)
    dwu = jnp.concatenate([_mm(hn, dug, ta=True, out_dtype=BF16, name=f"ffn{tag}_d_wup_gate", tn=1408),
                           _mm(hn, duv, ta=True, out_dtype=BF16, name=f"ffn{tag}_d_wup_val", tn=1408)], axis=1)
    dh_in, dgain = _norm_bwd(h, gain, dhn, dh, name=f"ffn{tag}_norm_bwd")
    grads = dict(w_up=dwu, w_down=dwd, conv_w=jnp.concatenate([dcwg, dcwv], axis=1), conv_b=jnp.concatenate([dcbg, dcbv], axis=1),
                 norm=dgain)
    return dh_in, grads


def kernel(x, norm_mix, norm_ffn, ret_gdn_w_in, gdn_conv_w, gdn_a_log, gdn_dt_bias, gdn_out_gain, ret_gdn_w_out, lru_w_in, lru_conv_w, lru_conv_b, lru_w_a, lru_b_a, lru_w_x, lru_b_x, lru_lambda, lru_w_out, ffn_w_up, ffn_conv_w, ffn_conv_b, ffn_w_down, norm_final, loss_target, m_norm_mix, m_norm_ffn, m_ret_gdn_w_in, m_gdn_conv_w, m_gdn_a_log, m_gdn_dt_bias, m_gdn_out_gain, m_ret_gdn_w_out, m_lru_w_in, m_lru_conv_w, m_lru_conv_b, m_lru_w_a, m_lru_b_a, m_lru_w_x, m_lru_b_x, m_lru_lambda, m_lru_w_out, m_ffn_w_up, m_ffn_conv_w, m_ffn_conv_b, m_ffn_w_down, m_norm_final, v_norm_mix, v_norm_ffn, v_ret_gdn_w_in, v_gdn_conv_w, v_gdn_a_log, v_gdn_dt_bias, v_gdn_out_gain, v_ret_gdn_w_out, v_lru_w_in, v_lru_conv_w, v_lru_conv_b, v_lru_w_a, v_lru_b_a, v_lru_w_x, v_lru_b_x, v_lru_lambda, v_lru_w_out, v_ffn_w_up, v_ffn_conv_w, v_ffn_conv_b, v_ffn_w_down, v_norm_final):
    given = dict(locals())
    w = {n: given[n] for n in WEIGHTS}
    me = 4 * lax.axis_index("x") + 2 * lax.axis_index("y") + lax.axis_index("c")
    t = x.shape[1]
    f = D_FF

    def rows_buf(names, cast):
        return _pack([_to_rows(w[n].astype(cast)) for n in names], GROUP_ROWS[names[0]])

    def scatter_buf(grads_by_name, names):
        blocks = _pack([_to_rows(_split_blocks(grads_by_name[n], BIG[n][1]), lead=(N_DEV,)) for n in names],
                       GROUP_ROWS[names[0]], lead=(N_DEV,))
        return blocks.reshape(4, 2, -1, LANES).transpose(1, 0, 2, 3)

    def joined(gathered, table, names):
        blocks = _unpack(gathered, [table[n][0] for n in names], lead=(N_DEV,))
        return {n: _join_blocks(b, table[n][1]) for n, b in zip(names, blocks)}

    small_shapes = [s for s, _ in SMALL.values()]
    small_buf = _pack([_to_rows(w[n]) for n in SMALL], _small_rows(small_shapes))
    g_first, g_small = _Comm([_AllGather(rows_buf(FIRST, BF16)), _AllGather(small_buf)]).run("gather_first")
    rest_gather = _Comm([_AllGather(rows_buf(REST, BF16))])
    full = {**joined(g_first, BIG, FIRST), **joined(g_small, SMALL, list(SMALL))}

    w_in0 = full['ret_gdn_w_in'][0]
    w_main = w_in0[:, :MAIN_IN]
    w_narrow = jnp.pad(w_in0[:, MAIN_IN:], ((0, 0), (0, LANES - SMALL_IN)))
    w_out0 = full['ret_gdn_w_out'][0]
    fcw = full['ffn_conv_w']
    fcw_g, fcw_v = [fcw[l][:, :f] for l in range(2)], [fcw[l][:, f:] for l in range(2)]
    fcb_g, fcb_v = [ffn_conv_b[l:l + 1, :f] for l in range(2)], [ffn_conv_b[l:l + 1, f:] for l in range(2)]
    gdn_cw = full['gdn_conv_w'][0]
    al_pad = jnp.pad(gdn_a_log, ((0, 0), (HEADS, LANES - 2 * HEADS)))
    dt_pad = jnp.pad(gdn_dt_bias, ((0, 0), (HEADS, LANES - 2 * HEADS)))
    lru_cw, lru_cb = full['lru_conv_w'][0], full['lru_conv_b']
    lru_ba, lru_bx, lru_lam = full['lru_b_a'], full['lru_b_x'], full['lru_lambda']
    wa, wx = lru_w_a[0], lru_w_x[0]

    h0, target = x[0], loss_target[0]
    hn0 = _norm_fwd(h0, norm_mix[0:1], name="mix0_norm")
    proj = _mm(hn0, w_main, name="mix0_in")
    pnarrow = _mm(hn0, w_narrow, name="mix0_in_narrow")
    tables = _ret_tables(t)
    y0, ret_states = _ret_fwd(proj, tables, name="retention_fwd")
    y0, gdn_states, (g_rest,) = _gdn_fwd(proj, pnarrow, gdn_cw, al_pad, dt_pad, gdn_out_gain, y0, rest_gather, name="deltanet_fwd")
    full.update(joined(g_rest, BIG, REST))
    lru_in_g, lru_in_x = full['lru_w_in'][0][:, :D_MODEL], full['lru_w_in'][0][:, D_MODEL:]
    lru_out = full['lru_w_out'][0]
    up_g = [full['ffn_w_up'][l][:, :f] for l in range(2)]
    up_v = [full['ffn_w_up'][l][:, f:] for l in range(2)]
    down = [full['ffn_w_down'][l] for l in range(2)]
    h1 = _mm(y0, w_out0, res=h0, name="mix0_out")
    h2, ffn0_saved = _ffn_forward(h1, norm_ffn[0:1], up_g[0], up_v[0], fcw_g[0], fcw_v[0], fcb_g[0], fcb_v[0], down[0], 0)
    hn1 = _norm_fwd(h2, norm_mix[1:2], name="mix1_norm")
    gate = _mm(hn1, lru_in_g, name="mix1_in_gate")
    xpre = _mm(hn1, lru_in_x, name="mix1_in_x")
    y1, hs = _lru_fwd(gate, xpre, lru_cw, lru_cb, wa, lru_ba, wx, lru_bx, lru_lam, name="rglru_fwd")
    h3 = _mm(y1, lru_out, res=h2, name="mix1_out")
    h4, ffn1_saved = _ffn_forward(h3, norm_ffn[1:2], up_g[1], up_v[1], fcw_g[1], fcw_v[1], fcb_g[1], fcb_v[1], down[1], 1)
    dh4, d_norm_final, loss_part = _final_loss(h4, norm_final[None, :], target, name="final_norm_loss")
    loss = lax.psum(loss_part[0, 0], ("x", "y", "c"))

    dh3, gf1 = _ffn_backward(dh4, h3, norm_ffn[1:2], ffn1_saved, up_g[1], up_v[1], fcw_g[1], fcw_v[1], fcb_g[1], fcb_v[1], down[1], 1)
    dy1 = _mm(dh3, lru_out, tb=True, name="mix1_d_y")
    d_lru_out = _mm(y1, dh3, ta=True, out_dtype=BF16, name="mix1_d_wout")
    dgate, dxpre, d_lcw, d_lcb, d_wa, d_ba, d_wx, d_bx, d_lam = _lru_bwd(
        gate, xpre, hs, dy1, lru_cw, lru_cb, wa, lru_ba, wx, lru_bx, lru_lam, name="rglru_bwd")
    dhn1 = _mm(dgate, lru_in_g, tb=True, name="mix1_d_hn_gate")
    dhn1 = _mm(dxpre, lru_in_x, tb=True, res=dhn1, name="mix1_d_hn_x")
    d_lru_in = jnp.concatenate([_mm(hn1, dgate, ta=True, out_dtype=BF16, name="mix1_d_win_gate"),
                                _mm(hn1, dxpre, ta=True, out_dtype=BF16, name="mix1_d_win_x")], axis=1)
    dh2, d_mix1 = _norm_bwd(h2, norm_mix[1:2], dhn1, dh3, name="mix1_norm_bwd")
    dh1, gf0 = _ffn_backward(dh2, h1, norm_ffn[0:1], ffn0_saved, up_g[0], up_v[0], fcw_g[0], fcw_v[0], fcb_g[0], fcb_v[0], down[0], 0)
    dy0 = _mm(dh1, w_out0, tb=True, name="mix0_d_y")
    d_w_out0 = _mm(y0, dh1, ta=True, out_dtype=BF16, name="mix0_d_wout")
    rest_grads = {'lru_w_in': d_lru_in[None], 'lru_w_out': d_lru_out[None],
                  'ffn_w_up': jnp.stack([gf0['w_up'], gf1['w_up']]), 'ffn_w_down': jnp.stack([gf0['w_down'], gf1['w_down']])}
    rest_blocks = scatter_buf(rest_grads, REST)
    (theirs,) = _Comm([_PairSwap(rest_blocks)]).run("pair_swap_rest")
    z_rest = _pair_add(rest_blocks, theirs, name="pair_add_rest", tr=GROUP_TILE[REST[0]])
    early = {'lru_conv_w': d_lcw[None], 'lru_conv_b': d_lcb, 'lru_b_a': d_ba, 'lru_b_x': d_bx, 'lru_lambda': d_lam,
             'ffn_conv_w': jnp.stack([gf0['conv_w'], gf1['conv_w']]), 'norm_ffn': jnp.concatenate([gf0['norm'], gf1['norm']], axis=0),
             'norm_mix1': d_mix1, 'lru_w_a': d_wa[None], 'lru_w_x': d_wx[None],
             'ffn_conv_b': jnp.concatenate([gf0['conv_b'], gf1['conv_b']], axis=0), 'norm_final': d_norm_final[0]}
    early_buf = _pack([_to_rows(early[n]) for n in EARLY], _small_rows(list(EARLY.values())))
    rest_exchange = _Comm([_ChipExchange(z_rest), _AllGather(early_buf)])

    dproj = _ret_bwd(proj, tables, ret_states, dy0, name="retention_bwd")
    (dproj, dnarrow, d_gcw, d_alog, d_dtb, d_gain), (w_rest, got_early) = _gdn_bwd(
        proj, pnarrow, gdn_cw, al_pad, dt_pad, gdn_out_gain, gdn_states, dy0, dproj, rest_exchange, name="deltanet_bwd")
    dhn0 = _mm(dproj, w_main, tb=True, name="mix0_d_hn")
    dhn0 = _mm(dnarrow, w_narrow, tb=True, res=dhn0, name="mix0_d_hn_narrow")
    d_w_main = _mm(hn0, dproj, ta=True, out_dtype=BF16, name="mix0_d_win")
    d_w_narrow = _mm(hn0, dnarrow, ta=True, out_dtype=BF16, name="mix0_d_win_narrow")
    dx, d_mix0 = _norm_bwd(h0, norm_mix[0:1], dhn0, dh1, name="mix0_norm_bwd")

    first_grads = {'ret_gdn_w_in': jnp.concatenate([d_w_main, d_w_narrow[:, :SMALL_IN]], axis=1)[None], 'ret_gdn_w_out': d_w_out0[None]}
    first_blocks = scatter_buf(first_grads, FIRST)
    (theirs,) = _Comm([_PairSwap(first_blocks)]).run("pair_swap_first")
    z_first = _pair_add(first_blocks, theirs, name="pair_add_first", tr=GROUP_TILE[FIRST[0]])
    late = {'gdn_conv_w': d_gcw[None], 'norm_mix0': d_mix0, 'gdn_a_log': d_alog[:, HEADS:2 * HEADS],
            'gdn_dt_bias': d_dtb[:, HEADS:2 * HEADS], 'gdn_out_gain': d_gain}
    late_buf = _pack([_to_rows(late[n]) for n in LATE], _small_rows(list(LATE.values())))
    w_first, got_late = _Comm([_ChipExchange(z_first), _AllGather(late_buf)]).run("exchange_first")

    grads = {}
    for names, got in ((REST, w_rest), (FIRST, w_first)):
        total = _sum_slots(got, name=f"sum_blocks_{names[0]}", tr=GROUP_TILE[names[0]])
        grads.update(zip(names, _unpack(total, [BIG[n][0] for n in names])))
    partial = dict(zip(EARLY, _unpack(_sum_slots(got_early, name="sum_partials_early"), list(EARLY.values()))))
    partial.update(zip(LATE, _unpack(_sum_slots(got_late, name="sum_partials_late"), list(LATE.values()))))
    partial['norm_mix'] = jnp.concatenate([partial.pop('norm_mix0'), partial.pop('norm_mix1')], axis=0)
    for n, g_full in partial.items():
        if n in SMALL:
            shard, axis = SMALL[n]
            g_full = lax.dynamic_slice_in_dim(g_full, me * shard[axis], shard[axis], axis=axis)
        grads[n] = g_full

    delta, new_m, new_v = {}, {}, {}
    for n in WEIGHTS:
        delta[n], new_m[n], new_v[n] = _adam(w[n], grads[n], given["m_" + n], given["v_" + n], name=f"adamw_{n}")
    return (loss, dx[None], *[grads[n] for n in WEIGHTS], *[delta[n] for n in WEIGHTS],
            *[new_m[n] for n in WEIGHTS], *[new_v[n] for n in WEIGHTS])
```

```python
import functools
import math

import numpy as np
import jax
import jax.numpy as jnp
from jax import lax
from jax.experimental import pallas as pl
from jax.experimental.pallas import tpu as pltpu

F32 = jnp.float32
BF16 = jnp.bfloat16
HI = lax.Precision.HIGHEST
MESH = pl.DeviceIdType.MESH

N_DEV = 8
LANES = 128
SUBLANES = 8
EPS = 1e-6
D_MODEL = 1024
HEADS = 4
HEAD_DIM = 128
RET_CHUNK = 128
GDN_CHUNK = 64
ROPE_BASE = 10000.0
LRU_C = 8.0
D_FF = 2816
MAIN_IN = 4096
SMALL_IN = 8
QSCALE = HEAD_DIM ** -0.5

ADAM_LR, ADAM_B1, ADAM_B2, ADAM_EPS, ADAM_WD, ADAM_STEP = 0.001, 0.9, 0.999, 1e-08, 0.01, 10


def _cp(sem=None, vmem_mb=None):
    kw = {}
    if sem is not None:
        kw["dimension_semantics"] = sem
    if vmem_mb is not None:
        kw["vmem_limit_bytes"] = vmem_mb << 20
    return pltpu.CompilerParams(**kw)


def _rows(shape):
    return lax.broadcasted_iota(jnp.int32, shape, 0)


def _cols(shape):
    return lax.broadcasted_iota(jnp.int32, shape, 1)


def _shift_down(cur, prev8, s):
    if s == 0:
        return cur
    rc = pltpu.roll(cur, s, 0)
    rp = pltpu.roll(prev8, s, 0)
    top = jnp.where(_rows(prev8.shape) < s, rp, rc[:SUBLANES])
    return jnp.concatenate([top, rc[SUBLANES:]], axis=0)


def _shift_up(cur, next8, s):
    if s == 0:
        return cur
    tt = cur.shape[0]
    rc = pltpu.roll(cur, tt - s, 0)
    rn = pltpu.roll(next8, SUBLANES - s, 0)
    bot = jnp.where(_rows(next8.shape) >= SUBLANES - s, rn, rc[tt - SUBLANES:])
    return jnp.concatenate([rc[:tt - SUBLANES], bot], axis=0)


def _down_fill(x, d, fill):
    return jnp.where(_rows(x.shape) < d, fill, pltpu.roll(x, d, 0))


def _up_fill(x, d, fill):
    tt = x.shape[0]
    return jnp.where(_rows(x.shape) >= tt - d, fill, pltpu.roll(x, tt - d, 0))


def _sigmoid(x):
    return 1.0 / (1.0 + jnp.exp(-x))


def _softplus(x):
    return jnp.maximum(x, 0.0) + jnp.log(1.0 + jnp.exp(-jnp.abs(x)))


def _dot(a, b, dims=(((1,), (0,)), ((), ())), precision=None):
    return lax.dot_general(a, b, dims, preferred_element_type=F32, precision=precision)


NN = (((1,), (0,)), ((), ()))
NT = (((1,), (1,)), ((), ()))
TN = (((0,), (0,)), ((), ()))


def _bdot(a, b, dims=NN):
    return _dot(a.astype(BF16), b.astype(BF16), dims)


def _split(a):
    hi = a.astype(BF16)
    return hi, (a - hi.astype(F32)).astype(BF16)


def _dot3(a, b, dims=NN):
    ah, al = _split(a)
    bh, bl = _split(b)
    return _dot(ah, bh, dims) + (_dot(ah, bl, dims) + _dot(al, bh, dims))


def _tile(dim, target):
    if dim <= target:
        return dim
    best = None
    for c in range(LANES, target + 1, LANES):
        if dim % c == 0:
            best = c
    assert best is not None, (dim, target)
    return best


def _mm(a, b, *, name, ta=False, tb=False, out_dtype=F32, res=None, tm=2048, tn=512, tk=1024):
    m, k = (a.shape[1], a.shape[0]) if ta else a.shape
    n = b.shape[0] if tb else b.shape[1]
    tn, tk = _tile(n, tn), _tile(k, tk)
    tm = _tile(m, tm if max(tn, tk) <= 1024 else tm // 2)
    nk = k // tk
    dims = (((0 if ta else 1,), (1 if tb else 0,)), ((), ()))

    def body(*refs):
        a_ref, b_ref = refs[:2]
        r_ref = refs[2] if res is not None else None
        o_ref = refs[3] if res is not None else refs[2]
        acc = refs[-1]
        kk = pl.program_id(2)
        part = _bdot(a_ref[...], b_ref[...], dims)

        def finish(r):
            if res is not None:
                r = r + r_ref[...]
            o_ref[...] = r.astype(out_dtype)

        if nk == 1:
            finish(part)
            return

        @pl.when(kk == 0)
        def _():
            acc[...] = part

        @pl.when(jnp.logical_and(kk > 0, kk < nk - 1))
        def _():
            acc[...] += part

        @pl.when(kk == nk - 1)
        def _():
            finish(acc[...] + part)

    a_spec = pl.BlockSpec((tk, tm), lambda i, j, kk: (kk, i)) if ta else pl.BlockSpec((tm, tk), lambda i, j, kk: (i, kk))
    b_spec = pl.BlockSpec((tn, tk), lambda i, j, kk: (j, kk)) if tb else pl.BlockSpec((tk, tn), lambda i, j, kk: (kk, j))
    o_spec = pl.BlockSpec((tm, tn), lambda i, j, kk: (i, j))
    in_specs = [a_spec, b_spec] + ([o_spec] if res is not None else [])
    args = (a, b) + ((res,) if res is not None else ())
    return pl.pallas_call(
        body, grid=(m // tm, n // tn, nk), in_specs=in_specs, out_specs=o_spec,
        out_shape=jax.ShapeDtypeStruct((m, n), out_dtype),
        scratch_shapes=[pltpu.VMEM((tm, tn), F32)] if nk > 1 else [], name=name,
        compiler_params=_cp(("parallel", "parallel", "arbitrary"), 56),
    )(*args)


def _mmx(a, b, *, dims, grid, a_spec, b_spec, o_spec, out_shape, tile, name, res=None, split_rows=None):
    nk = grid[-1]

    def body(*refs):
        a_ref, b_ref = refs[:2]
        r_ref = refs[2] if res is not None else None
        o_ref = refs[3] if res is not None else refs[2]
        acc = refs[-1]
        part = _bdot(a_ref[...], b_ref[...], dims)

        def finish(r):
            if res is not None:
                r = r + r_ref[...]
            if split_rows is None:
                o_ref[...] = r.astype(o_ref.dtype)
            else:
                o_ref[0] = r[:split_rows].astype(o_ref.dtype)
                o_ref[1] = r[split_rows:].astype(o_ref.dtype)

        if nk == 1:
            finish(part)
            return
        kk = pl.program_id(len(grid) - 1)

        @pl.when(kk == 0)
        def _():
            acc[...] = part

        @pl.when(jnp.logical_and(kk > 0, kk < nk - 1))
        def _():
            acc[...] += part

        @pl.when(kk == nk - 1)
        def _():
            finish(acc[...] + part)

    args = (a, b) + ((res,) if res is not None else ())
    return pl.pallas_call(
        body, grid=grid, in_specs=[a_spec, b_spec] + ([o_spec] if res is not None else []), out_specs=o_spec,
        out_shape=out_shape, scratch_shapes=[pltpu.VMEM(tile, F32)] if nk > 1 else [], name=name,
        compiler_params=_cp(("parallel",) * (len(grid) - 1) + ("arbitrary",), 56),
    )(*args)


def _norm_fwd(h, gain, *, name, tt=256):
    t, d = h.shape
    tt = min(tt, t)

    def body(h_ref, g_ref, o_ref):
        x = h_ref[...]
        r = lax.rsqrt(jnp.mean(x * x, axis=-1, keepdims=True) + EPS)
        o_ref[...] = (x * r * g_ref[...]).astype(BF16)

    row = pl.BlockSpec((tt, d), lambda i: (i, 0))
    return pl.pallas_call(
        body, grid=(t // tt,), in_specs=[row, pl.BlockSpec((1, d), lambda i: (0, 0))], out_specs=row,
        out_shape=jax.ShapeDtypeStruct((t, d), BF16), name=name, compiler_params=_cp(("parallel",)),
    )(h, gain)


def _norm_bwd(h, gain, dhn, dres, *, name, tt=256):
    t, d = h.shape
    tt = min(tt, t)

    def body(h_ref, g_ref, dy_ref, dr_ref, dx_ref, dg_ref):
        x, dy = h_ref[...], dy_ref[...]
        r = lax.rsqrt(jnp.mean(x * x, axis=-1, keepdims=True) + EPS)
        xh = x * r

        @pl.when(pl.program_id(0) == 0)
        def _():
            dg_ref[...] = jnp.zeros_like(dg_ref)

        dg_ref[...] += jnp.sum(dy * xh, axis=0, keepdims=True)
        dxh = dy * g_ref[...]
        dx_ref[...] = dr_ref[...] + r * (dxh - xh * jnp.mean(dxh * xh, axis=-1, keepdims=True))

    row = pl.BlockSpec((tt, d), lambda i: (i, 0))
    vec = pl.BlockSpec((1, d), lambda i: (0, 0))
    return pl.pallas_call(
        body, grid=(t // tt,), in_specs=[row, vec, row, row], out_specs=[row, vec],
        out_shape=[jax.ShapeDtypeStruct((t, d), F32), jax.ShapeDtypeStruct((1, d), F32)],
        name=name, compiler_params=_cp(("arbitrary",)),
    )(h, gain, dhn, dres)


def _final_loss(h, gain, target, *, name, tt=256):
    t, d = h.shape
    tt = min(tt, t)

    def body(h_ref, g_ref, tg_ref, dx_ref, dg_ref, loss_ref):
        x = h_ref[...]
        r = lax.rsqrt(jnp.mean(x * x, axis=-1, keepdims=True) + EPS)
        xh = x * r
        err = xh * g_ref[...] - tg_ref[...]

        @pl.when(pl.program_id(0) == 0)
        def _():
            dg_ref[...] = jnp.zeros_like(dg_ref)
            loss_ref[...] = jnp.zeros_like(loss_ref)

        loss_ref[...] += 0.5 * jnp.sum(jnp.mean(err * err, axis=-1, keepdims=True), axis=0, keepdims=True)
        dy = err * (1.0 / d)
        dg_ref[...] += jnp.sum(dy * xh, axis=0, keepdims=True)
        dxh = dy * g_ref[...]
        dx_ref[...] = r * (dxh - xh * jnp.mean(dxh * xh, axis=-1, keepdims=True))

    row = pl.BlockSpec((tt, d), lambda i: (i, 0))
    vec = pl.BlockSpec((1, d), lambda i: (0, 0))
    return pl.pallas_call(
        body, grid=(t // tt,), in_specs=[row, vec, row],
        out_specs=[row, vec, pl.BlockSpec((1, 1), lambda i: (0, 0))],
        out_shape=[jax.ShapeDtypeStruct((t, d), F32), jax.ShapeDtypeStruct((1, d), F32), jax.ShapeDtypeStruct((1, 1), F32)],
        name=name, compiler_params=_cp(("arbitrary",)),
    )(h, gain, target)


FFN_BLK = 704
FFN_NB = 4
FFN_TT = 256


def _prev8(n, tt):
    return jnp.maximum(n * (tt // SUBLANES) - 1, 0)


def _ffn_conv(cur, prev8, w, b):
    s1 = _shift_down(cur, prev8, 1)
    s2 = _shift_down(cur, prev8, 2)
    return w[0:1] * s2 + w[1:2] * s1 + w[2:3] * cur + b, s1, s2


def _ffn_specs(t, tt, order):
    pair = lambda rows, row_index: pl.BlockSpec((2, None, rows, FFN_BLK), lambda j, n: (0, j, row_index(n), 0))
    return dict(cur=pair(tt, order), prev=pair(SUBLANES, lambda n: _prev8(order(n), tt)), w=pair(3, lambda n: 0), b=pair(1, lambda n: 0),
                one=pl.BlockSpec((None, tt, FFN_BLK), lambda j, n: (j, order(n), 0)))


def _ffn_act_fwd(up, cw, cb, *, name):
    t = up.shape[2]
    tt = min(FFN_TT, t)
    sp = _ffn_specs(t, tt, lambda n: n)

    def body(u_ref, p_ref, w_ref, b_ref, o_ref):
        first = pl.program_id(1) == 0
        gate, _, _ = _ffn_conv(u_ref[0], jnp.where(first, 0.0, p_ref[0]), w_ref[0], b_ref[0])
        val, _, _ = _ffn_conv(u_ref[1], jnp.where(first, 0.0, p_ref[1]), w_ref[1], b_ref[1])
        o_ref[...] = (gate * _sigmoid(gate) * val).astype(BF16)

    return pl.pallas_call(
        body, grid=(FFN_NB, t // tt), in_specs=[sp["cur"], sp["prev"], sp["w"], sp["b"]], out_specs=sp["one"],
        out_shape=jax.ShapeDtypeStruct((FFN_NB, t, FFN_BLK), BF16), name=name,
        compiler_params=_cp(("parallel", "arbitrary"), 48),
    )(up, up, cw, cb)


def _ffn_act_bwd(up, da, cw, cb, *, name):
    t = up.shape[2]
    tt = min(FFN_TT, t)
    nt = t // tt
    sp = _ffn_specs(t, tt, lambda n: nt - 1 - n)

    def body(u_ref, p_ref, da_ref, w_ref, b_ref, du_ref, dw_ref, db_ref, head):
        n = pl.program_id(1)
        tile0 = n == nt - 1

        @pl.when(n == 0)
        def _():
            for r in (head, dw_ref, db_ref):
                r[...] = jnp.zeros_like(r)

        convs = [_ffn_conv(u_ref[s], jnp.where(tile0, 0.0, p_ref[s]), w_ref[s], b_ref[s]) for s in range(2)]
        gate, val = convs[0][0], convs[1][0]
        d = da_ref[...]
        sg = _sigmoid(gate)
        dcs = (d * val * sg * (1.0 + gate * (1.0 - sg)), d * gate * sg)
        for s in range(2):
            dc, w, hd = dcs[s], w_ref[s], head[s]
            _, x1, x2 = convs[s]
            du_ref[s] = (w[2:3] * dc + w[1:2] * _shift_up(dc, hd, 1) + w[0:1] * _shift_up(dc, hd, 2)).astype(BF16)
            dw_ref[s, 0:1, :] += jnp.sum(dc * x2, axis=0, keepdims=True)
            dw_ref[s, 1:2, :] += jnp.sum(dc * x1, axis=0, keepdims=True)
            dw_ref[s, 2:3, :] += jnp.sum(dc * u_ref[s], axis=0, keepdims=True)
            db_ref[s] += jnp.sum(dc, axis=0, keepdims=True)
            head[s] = dc[:SUBLANES]

    return pl.pallas_call(
        body, grid=(FFN_NB, nt), in_specs=[sp["cur"], sp["prev"], sp["one"], sp["w"], sp["b"]],
        out_specs=[sp["cur"], sp["w"], sp["b"]],
        out_shape=[jax.ShapeDtypeStruct(up.shape, BF16), jax.ShapeDtypeStruct(cw.shape, F32), jax.ShapeDtypeStruct(cb.shape, F32)],
        scratch_shapes=[pltpu.VMEM((2, SUBLANES, FFN_BLK), F32)], name=name,
        compiler_params=_cp(("parallel", "arbitrary"), 48),
    )(up, up, da, cw, cb)


LRU_TT = 256
LRU_CT = 512
GELU_C = math.sqrt(2.0 / math.pi)
GELU_A = 0.044715


def _gelu(x):
    return 0.5 * x * (1.0 + jnp.tanh(GELU_C * (x + GELU_A * x * x * x)))


def _gelu_grad(x):
    th = jnp.tanh(GELU_C * (x + GELU_A * x * x * x))
    return 0.5 * (1.0 + th) + 0.5 * x * (1.0 - th * th) * GELU_C * (1.0 + 3.0 * GELU_A * x * x)


def _neg_expm1(x):
    poly = -x * (1.0 + x * (0.5 + x * (1.0 / 6 + x * (1.0 / 24 + x * (1.0 / 120)))))
    return jnp.where(x > -0.1, poly, 1.0 - jnp.exp(x))


def _conv4(x, p8, w, b=None):
    s1, s2, s3 = _shift_down(x, p8, 1), _shift_down(x, p8, 2), _shift_down(x, p8, 3)
    y = w[0:1] * s3 + w[1:2] * s2 + w[2:3] * s1 + w[3:4] * x
    return (y if b is None else y + b), (s1, s2, s3)


def _conv4_bwd(dy, head, x, shifts, w):
    s1, s2, s3 = shifts
    dx = w[3:4] * dy + w[2:3] * _shift_up(dy, head, 1) + w[1:2] * _shift_up(dy, head, 2) + w[0:1] * _shift_up(dy, head, 3)
    dws = [jnp.sum(dy * s, axis=0, keepdims=True) for s in (s3, s2, s1, x)]
    return dx, dws


def _blockdiag(x, w_ref, dims=NN):
    nb = x.shape[1] // LANES
    return jnp.concatenate([_bdot(x[:, LANES * i:LANES * (i + 1)], w_ref[i], dims) for i in range(nb)], axis=1)


def _lru_gates(xr, wa_ref, wx_ref, ba, bx, lam):
    r = _sigmoid(_blockdiag(xr, wa_ref) + ba)
    i = _sigmoid(_blockdiag(xr, wx_ref) + bx)
    sp = _softplus(-lam)
    la = -LRU_C * r * sp
    a = jnp.exp(la)
    mult = jnp.sqrt(_neg_expm1(2.0 * la))
    return r, i, sp, a, mult


def _lru_specs(t, tt, ct, order):
    nb = ct // LANES
    cur = pl.BlockSpec((tt, ct), lambda j, n: (order(n), j))
    prev = pl.BlockSpec((SUBLANES, ct), lambda j, n: (_prev8(order(n), tt), j))
    vec = lambda rows: pl.BlockSpec((rows, ct), lambda j, n: (0, j))
    blk = pl.BlockSpec((nb, LANES, LANES), lambda j, n: (j, 0, 0))
    return cur, prev, vec, blk


def _lru_fwd(gate, xpre, cw, cb, wa, ba, wx, bx, lam, *, name):
    t, c = gate.shape
    tt, ct = min(LRU_TT, t), LRU_CT
    cur, prev, vec, blk = _lru_specs(t, tt, ct, lambda n: n)

    def body(gate_ref, x_ref, p_ref, cw_ref, cb_ref, wa_ref, ba_ref, wx_ref, bx_ref, lam_ref, y_ref, hs_ref, carry):
        n = pl.program_id(1)

        @pl.when(n == 0)
        def _():
            carry[...] = jnp.zeros_like(carry)

        p8 = jnp.where(n == 0, 0.0, p_ref[...])
        xr, _ = _conv4(x_ref[...], p8, cw_ref[...], cb_ref[...])
        r, i, sp, a, mult = _lru_gates(xr, wa_ref, wx_ref, ba_ref[...], bx_ref[...], lam_ref[...])
        acc_a, acc_b = a, mult * (i * xr)
        d = 1
        while d < tt:
            acc_b = acc_a * _down_fill(acc_b, d, 0.0) + acc_b
            acc_a = acc_a * _down_fill(acc_a, d, 1.0)
            d *= 2
        hs = acc_b + acc_a * carry[0:1]
        carry[...] = jnp.broadcast_to(hs[tt - 1:tt], carry.shape)
        hs_ref[...] = hs
        y_ref[...] = (_gelu(gate_ref[...]) * hs).astype(BF16)

    return pl.pallas_call(
        body, grid=(c // ct, t // tt),
        in_specs=[cur, cur, prev, vec(4), vec(1), blk, vec(1), blk, vec(1), vec(1)],
        out_specs=[cur, cur],
        out_shape=[jax.ShapeDtypeStruct((t, c), BF16), jax.ShapeDtypeStruct((t, c), F32)],
        scratch_shapes=[pltpu.VMEM((SUBLANES, ct), F32)], name=name,
        compiler_params=_cp(("parallel", "arbitrary"), 48),
    )(gate, xpre, xpre, cw, cb, wa, ba, wx, bx, lam)


def _lru_bwd(gate, xpre, hs, dy, cw, cb, wa, ba, wx, bx, lam, *, name):
    t, c = gate.shape
    tt, ct = min(LRU_TT, t), LRU_CT
    nt = t // tt
    cur, prev, vec, blk = _lru_specs(t, tt, ct, lambda n: nt - 1 - n)

    def body(gate_ref, x_ref, p_ref, hs_ref, phs_ref, dy_ref, cw_ref, cb_ref, wa_ref, ba_ref, wx_ref, bx_ref, lam_ref,
             dgate_ref, dx_ref, dcw_ref, dcb_ref, dwa_ref, dba_ref, dwx_ref, dbx_ref, dlam_ref, carry, head):
        n = pl.program_id(1)
        tile0 = n == nt - 1

        @pl.when(n == 0)
        def _():
            for ref in (carry, head, dcw_ref, dcb_ref, dwa_ref, dba_ref, dwx_ref, dbx_ref, dlam_ref):
                ref[...] = jnp.zeros_like(ref)

        xp, cwv, lam = x_ref[...], cw_ref[...], lam_ref[...]
        p8 = jnp.where(tile0, 0.0, p_ref[...])
        xr, shifts = _conv4(xp, p8, cwv, cb_ref[...])
        r, i, sp, a, mult = _lru_gates(xr, wa_ref, wx_ref, ba_ref[...], bx_ref[...], lam)
        gate, hsv, dyv = gate_ref[...], hs_ref[...], dy_ref[...]
        dgate_ref[...] = (dyv * hsv * _gelu_grad(gate)).astype(BF16)
        acc_b = dyv * _gelu(gate) + jnp.where(_rows(a.shape) == tt - 1, carry[0:1], 0.0)
        acc_a = _up_fill(a, 1, 0.0)
        d = 1
        while d < tt:
            acc_b = acc_b + acc_a * _up_fill(acc_b, d, 0.0)
            acc_a = acc_a * _up_fill(acc_a, d, 0.0)
            d *= 2
        gsum = acc_b
        carry[...] = jnp.broadcast_to(a[0:1] * gsum[0:1], carry.shape)
        hprev = _shift_down(hsv, jnp.where(tile0, 0.0, phs_ref[...]), 1)
        da = gsum * hprev
        dmult = gsum * i * xr
        di = gsum * mult * xr
        dxr = gsum * mult * i
        dla = da * a - dmult * (a * a) / mult
        dr = dla * (-LRU_C * sp)
        dlam_ref[...] += jnp.sum(dla * (-LRU_C * r), axis=0, keepdims=True) * (-_sigmoid(-lam))
        dpa = dr * r * (1.0 - r)
        dpx = di * i * (1.0 - i)
        dba_ref[...] += jnp.sum(dpa, axis=0, keepdims=True)
        dbx_ref[...] += jnp.sum(dpx, axis=0, keepdims=True)
        dxr = dxr + _blockdiag(dpa, wa_ref, NT) + _blockdiag(dpx, wx_ref, NT)
        for b in range(ct // LANES):
            sl = slice(LANES * b, LANES * (b + 1))
            dwa_ref[b] += _bdot(xr[:, sl], dpa[:, sl], TN)
            dwx_ref[b] += _bdot(xr[:, sl], dpx[:, sl], TN)
        dx, dws = _conv4_bwd(dxr, head[...], xp, shifts, cwv)
        dx_ref[...] = dx.astype(BF16)
        for k in range(4):
            dcw_ref[k:k + 1, :] += dws[k]
        dcb_ref[...] += jnp.sum(dxr, axis=0, keepdims=True)
        head[...] = dxr[:SUBLANES]

    return pl.pallas_call(
        body, grid=(c // ct, nt),
        in_specs=[cur, cur, prev, cur, prev, cur, vec(4), vec(1), blk, vec(1), blk, vec(1), vec(1)],
        out_specs=[cur, cur, vec(4), vec(1), blk, vec(1), blk, vec(1), vec(1)],
        out_shape=[jax.ShapeDtypeStruct((t, c), BF16)] * 2 + [jax.ShapeDtypeStruct((4, c), F32), jax.ShapeDtypeStruct((1, c), F32),
                   jax.ShapeDtypeStruct(wa.shape, F32), jax.ShapeDtypeStruct((1, c), F32),
                   jax.ShapeDtypeStruct(wx.shape, F32), jax.ShapeDtypeStruct((1, c), F32), jax.ShapeDtypeStruct((1, c), F32)],
        scratch_shapes=[pltpu.VMEM((SUBLANES, ct), F32)] * 2, name=name,
        compiler_params=_cp(("parallel", "arbitrary"), 48),
    )(gate, xpre, xpre, hs, hs, dy, cw, cb, wa, ba, wx, bx, lam)


RET_W = HEADS * HEAD_DIM
HALF = HEAD_DIM // 2


def _ret_tables(t):
    c = RET_CHUNK
    inv_freq = ROPE_BASE ** (-jnp.arange(HALF, dtype=F32) / HALF)
    ang = jnp.arange(t, dtype=jnp.int32).astype(F32)[:, None] * inv_freq[None, :]
    cos, sin = jnp.cos(ang), jnp.sin(ang)
    cosf = jnp.concatenate([cos, cos], axis=1)
    sinf = jnp.concatenate([-sin, sin], axis=1)
    log_gamma = jnp.log1p(-jnp.exp2(-5.0 - jnp.arange(HEADS, dtype=F32)))
    idx = jnp.arange(c, dtype=F32)
    rel = idx[:, None] - idx[None, :]
    causal = rel >= 0
    dmask = jnp.where(causal, jnp.exp(log_gamma[:, None, None] * jnp.where(causal, rel, 0.0)), 0.0)
    ktail = jnp.exp(log_gamma[:, None] * (c - 1 - idx))
    qdec = jnp.exp(log_gamma[:, None] * (idx + 1.0))
    rowtab = jnp.broadcast_to(jnp.stack([ktail, qdec], axis=1)[..., None], (HEADS, 2, c, HEAD_DIM))
    cdec = jnp.broadcast_to(jnp.exp(log_gamma * c)[:, None, None], (HEADS, SUBLANES, HEAD_DIM))
    return cosf, sinf, dmask, rowtab, cdec


def _rotary(x, cosf, sinf):
    return x * cosf + pltpu.roll(x, HALF, 1) * sinf


def _rotary_t(dx, cosf, sinf):
    return dx * cosf + pltpu.roll(dx * sinf, HALF, 1)


def _ret_specs(c, order):
    full = lambda shape: pl.BlockSpec(shape, lambda n: (0,) * len(shape))
    return dict(
        proj=pl.BlockSpec((c, 4 * RET_W), lambda n: (order(n), 0)),
        rot=pl.BlockSpec((c, HEAD_DIM), lambda n: (order(n), 0)),
        dmask=full((HEADS, c, c)), rowtab=full((HEADS, 2, c, HEAD_DIM)), cdec=full((HEADS, SUBLANES, HEAD_DIM)),
        state=pl.BlockSpec((1, HEADS, HEAD_DIM, HEAD_DIM), lambda n: (order(n), 0, 0, 0)),
        half=pl.BlockSpec((c, RET_W), lambda n: (order(n), 0)),
    )


def _ret_head(p_ref, h, cosf, sinf):
    sl = lambda j: slice(j * RET_W + h * HEAD_DIM, j * RET_W + (h + 1) * HEAD_DIM)
    q, k, v, g = p_ref[:, sl(0)], p_ref[:, sl(1)], p_ref[:, sl(2)], p_ref[:, sl(3)]
    return _rotary(q, cosf, sinf), _rotary(k, cosf, sinf) * QSCALE, v, g


def _ret_fwd(proj, tables, *, name):
    t = proj.shape[0]
    c = RET_CHUNK
    nc = t // c
    sp = _ret_specs(c, lambda n: n)

    def body(p_ref, cos_ref, sin_ref, dm_ref, rt_ref, cd_ref, y_ref, s_ref, state):
        @pl.when(pl.program_id(0) == 0)
        def _():
            state[...] = jnp.zeros_like(state)

        cosf, sinf = cos_ref[...], sin_ref[...]
        for h in range(HEADS):
            qr, kr, v, g = _ret_head(p_ref, h, cosf, sinf)
            s0 = state[h]
            s_ref[0, h] = s0
            scores = _bdot(qr, kr, NT) * dm_ref[h]
            o = _bdot(scores, v) + _bdot(qr * rt_ref[h, 1], s0)
            state[h] = s0 * cd_ref[h][0:1] + _bdot(kr * rt_ref[h, 0], v, TN)
            rinv = lax.rsqrt(jnp.mean(o * o, axis=-1, keepdims=True) + EPS)
            y_ref[:, h * HEAD_DIM:(h + 1) * HEAD_DIM] = (o * rinv * (g * _sigmoid(g))).astype(BF16)

    return pl.pallas_call(
        body, grid=(nc,),
        in_specs=[sp["proj"], sp["rot"], sp["rot"], sp["dmask"], sp["rowtab"], sp["cdec"]],
        out_specs=[sp["half"], sp["state"]],
        out_shape=[jax.ShapeDtypeStruct((t, 2 * RET_W), BF16), jax.ShapeDtypeStruct((nc, HEADS, HEAD_DIM, HEAD_DIM), F32)],
        scratch_shapes=[pltpu.VMEM((HEADS, HEAD_DIM, HEAD_DIM), F32)], name=name,
        compiler_params=_cp(("arbitrary",), 48),
    )(proj, *tables)


def _ret_bwd(proj, tables, states, dy, *, name):
    t = proj.shape[0]
    c = RET_CHUNK
    nc = t // c
    sp = _ret_specs(c, lambda n: nc - 1 - n)

    def body(p_ref, cos_ref, sin_ref, dm_ref, rt_ref, cd_ref, s_ref, dy_ref, dp_ref, dstate):
        @pl.when(pl.program_id(0) == 0)
        def _():
            dstate[...] = jnp.zeros_like(dstate)

        cosf, sinf = cos_ref[...], sin_ref[...]
        for h in range(HEADS):
            qr, kr, v, g = _ret_head(p_ref, h, cosf, sinf)
            s0, dm, ktl, qdc = s_ref[0, h], dm_ref[h], rt_ref[h, 0], rt_ref[h, 1]
            scores = _bdot(qr, kr, NT) * dm
            qd, kt = qr * qdc, kr * ktl
            o = _bdot(scores, v) + _bdot(qd, s0)
            rinv = lax.rsqrt(jnp.mean(o * o, axis=-1, keepdims=True) + EPS)
            oh = o * rinv
            sg = _sigmoid(g)
            dyh = dy_ref[:, h * HEAD_DIM:(h + 1) * HEAD_DIM]
            dg = dyh * oh * sg * (1.0 + g * (1.0 - sg))
            dyo = dyh * (g * sg)
            do = rinv * (dyo - oh * jnp.mean(dyo * oh, axis=-1, keepdims=True))
            ds1 = dstate[h]
            dsc = _bdot(do, v, NT) * dm
            dv = _bdot(scores, do, TN) + _bdot(kt, ds1)
            dqr = _bdot(dsc, kr) + _bdot(do, s0, NT) * qdc
            dkr = (_bdot(dsc, qr, TN) + _bdot(v, ds1, NT) * ktl) * QSCALE
            dstate[h] = ds1 * cd_ref[h][0:1] + _bdot(qd, do, TN)
            pieces = (_rotary_t(dqr, cosf, sinf), _rotary_t(dkr, cosf, sinf), dv, dg)
            for j, piece in enumerate(pieces):
                dp_ref[:, j * RET_W + h * HEAD_DIM:j * RET_W + (h + 1) * HEAD_DIM] = piece.astype(BF16)

    return pl.pallas_call(
        body, grid=(nc,),
        in_specs=[sp["proj"], sp["rot"], sp["rot"], sp["dmask"], sp["rowtab"], sp["cdec"], sp["state"], sp["half"]],
        out_specs=sp["proj"],
        out_shape=jax.ShapeDtypeStruct((t, 8 * RET_W), BF16),
        scratch_shapes=[pltpu.VMEM((HEADS, HEAD_DIM, HEAD_DIM), F32)], name=name,
        compiler_params=_cp(("arbitrary",), 48),
    )(proj, *tables, states, dy)


GDN_W = HEADS * HEAD_DIM
GDN_CONV = 3 * GDN_W
NEUMANN_STEPS = 5


def _gdn_gates(ps, al, dt):
    return _sigmoid(ps), -jnp.exp(al) * _softplus(ps + dt)


def _cumsum_rows(x):
    d = 1
    while d < x.shape[0]:
        x = x + _down_fill(x, d, 0.0)
        d *= 2
    return x


def _rev_cumsum_rows(x):
    d = 1
    while d < x.shape[0]:
        x = x + _up_fill(x, d, 0.0)
        d *= 2
    return x


class _Chunk:
    pass


def _gdn_chunk(qc, kc, v, beta, g, s0):
    c = GDN_CHUNK
    z = _Chunk()
    z.rq = lax.rsqrt(jnp.sum(qc * qc, axis=-1, keepdims=True) + EPS)
    z.rk = lax.rsqrt(jnp.sum(kc * kc, axis=-1, keepdims=True) + EPS)
    z.qn, z.k = qc * z.rq, kc * z.rk
    z.q = z.qn * QSCALE
    z.v, z.beta = v, beta
    gc = _cumsum_rows(jnp.broadcast_to(g, (c, LANES)))
    ri, ci = _rows((c, c)), _cols((c, c))
    z.tril, z.strict = ri >= ci, ri > ci
    diff = gc[:, :c] - gc.T[:c, :]
    z.decay = jnp.where(z.tril, jnp.exp(jnp.where(z.tril, diff, 0.0)), 0.0)
    z.eg = jnp.exp(gc)
    glast = gc[c - 1:c, :]
    z.egl = jnp.exp(glast - gc)
    z.cd = jnp.exp(glast)
    z.kb = z.k * beta
    z.m = _bdot(z.kb, z.k, NT)
    lmat = jnp.where(z.strict, z.m * z.decay, 0.0)
    neg = -lmat
    inv = (ri == ci).astype(F32) + neg
    pw = neg
    for _ in range(NEUMANN_STEPS):
        pw = _dot3(pw, pw)
        inv = inv + _dot3(inv, pw)
    z.inv = inv
    z.vb, z.kbg = v * beta, z.kb * z.eg
    z.u = _dot3(inv, z.vb)
    z.w = _dot3(inv, z.kbg)
    z.qk = _bdot(z.q, z.k, NT)
    z.attn = jnp.where(z.tril, z.qk * z.decay, 0.0)
    z.qd, z.kt = z.q * z.eg, z.k * z.egl
    z.vnew = z.u - _bdot(z.w, s0)
    z.o = _bdot(z.qd, s0) + _bdot(z.attn, z.vnew)
    z.s1 = s0 * z.cd + _bdot(z.kt, z.vnew, TN)
    return z


def _gdn_chunk_bwd(z, s0, do, ds1):
    c = GDN_CHUNK
    dvnew = _bdot(z.attn, do, TN) + _bdot(z.kt, ds1)
    dqd = _bdot(do, s0, NT)
    dattn = jnp.where(z.tril, _bdot(do, z.vnew, NT), 0.0)
    ds0 = _bdot(z.qd, do, TN) + ds1 * z.cd - _bdot(z.w, dvnew, TN)
    dcd = jnp.sum(jnp.sum(s0 * ds1, axis=1, keepdims=True), axis=0, keepdims=True)
    dkt = _bdot(z.vnew, ds1, NT)
    dw = -_bdot(dvnew, s0, NT)
    dvb = _dot3(z.inv, dvnew, TN)
    dkbg = _dot3(z.inv, dw, TN)
    dl = jnp.where(z.strict, -(_bdot(dvb, z.u, NT) + _bdot(dkbg, z.w, NT)), 0.0)
    dml = dl * z.decay
    dqk = dattn * z.decay
    ddecay = (dl * z.m + dattn * z.qk) * z.decay
    dq = _bdot(dqk, z.k) + dqd * z.eg
    dkb = _bdot(dml, z.k) + dkbg * z.eg
    dk = _bdot(dqk, z.q, TN) + _bdot(dml, z.kb, TN) + dkt * z.egl + dkb * z.beta
    dbeta = jnp.sum(dkb * z.k, axis=-1, keepdims=True) + jnp.sum(dvb * z.v, axis=-1, keepdims=True)
    dv = dvb * z.beta
    colsum = _dot3(ddecay, jnp.ones((c, LANES), F32), TN)
    e = jnp.sum(dkt * z.kt, axis=-1, keepdims=True)
    dgc = (jnp.sum(ddecay, axis=-1, keepdims=True) - colsum
           + jnp.sum(dkbg * z.kbg, axis=-1, keepdims=True) + jnp.sum(dqd * z.qd, axis=-1, keepdims=True) - e)
    dglast = jnp.sum(e, axis=0, keepdims=True) + dcd * z.cd
    dgc = dgc + jnp.where(_rows((c, LANES)) == c - 1, dglast, 0.0)
    dg = _rev_cumsum_rows(dgc)[:, 0:1]
    dqn = dq * QSCALE
    dqc = z.rq * (dqn - z.qn * jnp.sum(dqn * z.qn, axis=-1, keepdims=True))
    dkc = z.rk * (dk - z.k * jnp.sum(dk * z.k, axis=-1, keepdims=True))
    return dqc, dkc, dv, dbeta, dg, ds0


def _gdn_specs(c, order):
    full = lambda shape: pl.BlockSpec(shape, lambda n: (0,) * len(shape))
    return dict(
        proj=pl.BlockSpec((c, 4 * GDN_W), lambda n: (order(n), 1)),
        prev=pl.BlockSpec((SUBLANES, 4 * GDN_W), lambda n: (_prev8(order(n), c), 1)),
        small=pl.BlockSpec((c, LANES), lambda n: (order(n), 0)),
        convw=full((4, GDN_CONV)), vec=full((1, LANES)),
        state=pl.BlockSpec((1, HEADS, HEAD_DIM, HEAD_DIM), lambda n: (order(n), 0, 0, 0)),
        half=pl.BlockSpec((c, GDN_W), lambda n: (order(n), 1)),
        any=pl.BlockSpec(memory_space=pl.ANY),
    )


def _gdn_fwd(proj, psmall, conv_w, al, dt, gain, y_in, comm, *, name):
    t = proj.shape[0]
    c = GDN_CHUNK
    nc = t // c
    sp = _gdn_specs(c, lambda n: n)

    def body(p_ref, prev_ref, ps_ref, cw_ref, al_ref, dt_ref, gain_ref, yin_ref, *rest):
        comm_in, (y_ref, s_ref), comm_out, (state,), comm_sems = comm.split(rest, n_out=2, n_scratch=1)
        n = pl.program_id(0)

        @pl.when(n == 0)
        def _():
            state[...] = jnp.zeros_like(state)
            comm.start(comm_in, comm_out, comm_sems)

        p8 = jnp.where(n == 0, 0.0, prev_ref[:, :GDN_CONV])
        pre, _ = _conv4(p_ref[:, :GDN_CONV], p8, cw_ref[...])
        act = pre * _sigmoid(pre)
        beta_all, g_all = _gdn_gates(ps_ref[...], al_ref[...], dt_ref[...])
        for h in range(HEADS):
            sl = lambda j: slice(j * GDN_W + h * HEAD_DIM, j * GDN_W + (h + 1) * HEAD_DIM)
            s0 = state[h]
            s_ref[0, h] = s0
            z = _gdn_chunk(act[:, sl(0)], act[:, sl(1)], act[:, sl(2)], beta_all[:, h:h + 1], g_all[:, HEADS + h:HEADS + h + 1], s0)
            state[h] = z.s1
            rinv = lax.rsqrt(jnp.mean(z.o * z.o, axis=-1, keepdims=True) + EPS)
            gd = p_ref[:, sl(3)]
            y_ref[:, sl(0)] = (z.o * rinv * gain_ref[...] * (gd * _sigmoid(gd))).astype(BF16)

        @pl.when(n == nc - 1)
        def _():
            comm.finish(comm_in, comm_out, comm_sems)

    outs = pl.pallas_call(
        body, grid=(nc,),
        in_specs=[sp["proj"], sp["prev"], sp["small"], sp["convw"], sp["vec"], sp["vec"], sp["vec"], sp["any"]]
        + [sp["any"]] * len(comm.arrays),
        out_specs=[sp["half"], sp["state"]] + [sp["any"]] * len(comm.out_shapes),
        out_shape=[jax.ShapeDtypeStruct((t, 2 * GDN_W), BF16), jax.ShapeDtypeStruct((nc, HEADS, HEAD_DIM, HEAD_DIM), F32)]
        + comm.out_shapes,
        scratch_shapes=[pltpu.VMEM((HEADS, HEAD_DIM, HEAD_DIM), F32)] + comm.scratch, name=name,
        input_output_aliases={7: 0}, compiler_params=_cp(("arbitrary",), 48),
    )(proj, proj, psmall, conv_w, al, dt, gain, y_in, *comm.arrays)
    return outs[0], outs[1], outs[2:]


def _gdn_bwd(proj, psmall, conv_w, al, dt, gain, states, dy, dproj_in, comm, *, name):
    t = proj.shape[0]
    c = GDN_CHUNK
    nc = t // c
    sp = _gdn_specs(c, lambda n: nc - 1 - n)

    def body(p_ref, prev_ref, ps_ref, cw_ref, al_ref, dt_ref, gain_ref, s_ref, dy_ref, dpin_ref, *rest):
        comm_in, outs, comm_out, (dstate, head), comm_sems = comm.split(rest, n_out=6, n_scratch=2)
        dp_ref, dps_ref, dcw_ref, dal_ref, ddt_ref, dgain_ref = outs
        n = pl.program_id(0)
        chunk0 = n == nc - 1

        @pl.when(n == 0)
        def _():
            for ref in (dstate, head, dcw_ref, dal_ref, ddt_ref, dgain_ref):
                ref[...] = jnp.zeros_like(ref)
            comm.start(comm_in, comm_out, comm_sems)

        x, cwv = p_ref[:, :GDN_CONV], cw_ref[...]
        p8 = jnp.where(chunk0, 0.0, prev_ref[:, :GDN_CONV])
        pre, shifts = _conv4(x, p8, cwv)
        sg_pre = _sigmoid(pre)
        act = pre * sg_pre
        ps, alv, dtv, gain = ps_ref[...], al_ref[...], dt_ref[...], gain_ref[...]
        beta_all, g_all = _gdn_gates(ps, alv, dtv)
        lane = _cols((c, LANES))
        dbeta_all = jnp.zeros((c, LANES), F32)
        dg_all = jnp.zeros((c, LANES), F32)
        dact = [None] * (3 * HEADS)
        dgain = jnp.zeros((1, LANES), F32)
        for h in range(HEADS):
            sl = lambda j: slice(j * GDN_W + h * HEAD_DIM, j * GDN_W + (h + 1) * HEAD_DIM)
            s0 = s_ref[0, h]
            z = _gdn_chunk(act[:, sl(0)], act[:, sl(1)], act[:, sl(2)], beta_all[:, h:h + 1], g_all[:, HEADS + h:HEADS + h + 1], s0)
            rinv = lax.rsqrt(jnp.mean(z.o * z.o, axis=-1, keepdims=True) + EPS)
            oh = z.o * rinv
            gd = p_ref[:, sl(3)]
            sgd = _sigmoid(gd)
            dyh = dy_ref[:, sl(0)]
            dgain = dgain + jnp.sum(dyh * oh * (gd * sgd), axis=0, keepdims=True)
            dp_ref[:, sl(3)] = (dyh * oh * gain * sgd * (1.0 + gd * (1.0 - sgd))).astype(BF16)
            dyo = dyh * gain * (gd * sgd)
            do = rinv * (dyo - oh * jnp.mean(dyo * oh, axis=-1, keepdims=True))
            dqc, dkc, dv, dbeta, dg, ds0 = _gdn_chunk_bwd(z, s0, do, dstate[h])
            dstate[h] = ds0
            dact[h], dact[HEADS + h], dact[2 * HEADS + h] = dqc, dkc, dv
            dbeta_all = dbeta_all + jnp.where(lane == h, dbeta, 0.0)
            dg_all = dg_all + jnp.where(lane == HEADS + h, dg, 0.0)
        dpre = jnp.concatenate(dact, axis=1) * sg_pre * (1.0 + pre * (1.0 - sg_pre))
        dx, dws = _conv4_bwd(dpre, head[...], x, shifts, cwv)
        dp_ref[:, :GDN_CONV] = dx.astype(BF16)
        for k in range(4):
            dcw_ref[k:k + 1, :] += dws[k]
        head[...] = dpre[:SUBLANES]
        dsp = dg_all * (-jnp.exp(alv)) * _sigmoid(ps + dtv)
        dps_ref[...] = (dbeta_all * beta_all * (1.0 - beta_all) + dsp).astype(BF16)
        ddt_ref[...] += jnp.sum(dsp, axis=0, keepdims=True)
        dal_ref[...] += jnp.sum(dg_all * g_all, axis=0, keepdims=True)
        dgain_ref[...] += dgain

        @pl.when(n == nc - 1)
        def _():
            comm.finish(comm_in, comm_out, comm_sems)

    vec_f32 = jax.ShapeDtypeStruct((1, LANES), F32)
    outs = pl.pallas_call(
        body, grid=(nc,),
        in_specs=[sp["proj"], sp["prev"], sp["small"], sp["convw"], sp["vec"], sp["vec"], sp["vec"], sp["state"], sp["half"], sp["any"]]
        + [sp["any"]] * len(comm.arrays),
        out_specs=[sp["proj"], sp["small"], sp["convw"], sp["vec"], sp["vec"], sp["vec"]] + [sp["any"]] * len(comm.out_shapes),
        out_shape=[jax.ShapeDtypeStruct((t, 8 * GDN_W), BF16), jax.ShapeDtypeStruct((t, LANES), BF16),
                   jax.ShapeDtypeStruct((4, GDN_CONV), F32), vec_f32, vec_f32, vec_f32] + comm.out_shapes,
        scratch_shapes=[pltpu.VMEM((HEADS, HEAD_DIM, HEAD_DIM), F32), pltpu.VMEM((SUBLANES, GDN_CONV), F32)] + comm.scratch,
        name=name, input_output_aliases={9: 0}, compiler_params=_cp(("arbitrary",), 48),
    )(proj, proj, psmall, conv_w, al, dt, gain, states, dy, dproj_in, *comm.arrays)
    return outs[:6], outs[6:]


def _here():
    x, y, c = lax.axis_index("x"), lax.axis_index("y"), lax.axis_index("c")
    return x, y, c, [(1 - x, y), (x, 1 - y), (1 - x, 1 - y)]


def _rdma(src, dst, send, recv, k, dev):
    return pltpu.make_async_remote_copy(src_ref=src, dst_ref=dst, send_sem=send.at[k], recv_sem=recv.at[k],
                                        device_id=dev, device_id_type=MESH)


def _dma_sems(n):
    return [pltpu.SemaphoreType.DMA((n,)), pltpu.SemaphoreType.DMA((n,)), pltpu.SemaphoreType.DMA((1,))]


COPY_PIECES = 4
COPY_PIECE_ALIGN = 16


def _row_parts(rows):
    n = COPY_PIECES if rows % (COPY_PIECES * COPY_PIECE_ALIGN) == 0 and rows >= 1024 else 1
    return [pl.ds(q * (rows // n), rows // n) for q in range(n)]


class _AllGather:
    def __init__(self, array):
        self.arrays = [array]
        self.out_shapes = [jax.ShapeDtypeStruct((N_DEV,) + array.shape, array.dtype)]
        self.parts = _row_parts(array.shape[0])
        self.scratch = _dma_sems(7 * len(self.parts))

    def start(self, ins, outs, sems):
        (src,), (out,), (send, recv, loc) = ins, outs, sems
        x, y, c, chips = _here()
        me, n = 4 * x + 2 * y + c, len(self.parts)
        pltpu.make_async_copy(src, out.at[me], loc.at[0]).start()
        for q, part in enumerate(self.parts):
            _rdma(src.at[part], out.at[me, part], send, recv, q, (x, y, 1 - c)).start()
            for j, (cx, cy) in enumerate(chips):
                _rdma(src.at[part], out.at[me, part], send, recv, (1 + j) * n + q, (cx, cy, c)).start()

    def finish(self, ins, outs, sems):
        (src,), (out,), (send, recv, loc) = ins, outs, sems
        x, y, c, chips = _here()
        sibling, me, n = (x, y, 1 - c), 4 * x + 2 * y + c, len(self.parts)
        piece = lambda k, q: _rdma(src.at[self.parts[q]], out.at[me, self.parts[q]], send, recv, k * n + q, sibling)
        for j, (cx, cy) in enumerate(chips):
            for q, part in enumerate(self.parts):
                got = out.at[4 * cx + 2 * cy + c, part]
                piece(1 + j, q).wait_recv()
                _rdma(got, got, send, recv, (4 + j) * n + q, sibling).start()
        for k in (0, 4, 5, 6):
            for q in range(n):
                piece(k, q).wait_recv()
        for k in range(7):
            for q in range(n):
                piece(k, q).wait_send()
        pltpu.make_async_copy(src, out.at[me], loc.at[0]).wait()


class _ChipExchange:
    def __init__(self, array):
        self.arrays = [array]
        self.out_shapes = [jax.ShapeDtypeStruct(array.shape, array.dtype)]
        self.parts = _row_parts(array.shape[1])
        self.scratch = _dma_sems(3 * len(self.parts))

    def _copies(self, ins, outs, sems):
        (src,), (out,), (send, recv, loc) = ins, outs, sems
        x, y, c, chips = _here()
        here, n = 2 * x + y, len(self.parts)
        local = pltpu.make_async_copy(src.at[here], out.at[here], loc.at[0])
        return local, [_rdma(src.at[2 * cx + cy, part], out.at[here, part], send, recv, j * n + q, (cx, cy, c))
                       for j, (cx, cy) in enumerate(chips) for q, part in enumerate(self.parts)]

    def start(self, ins, outs, sems):
        local, remote = self._copies(ins, outs, sems)
        local.start()
        for cp in remote:
            cp.start()

    def finish(self, ins, outs, sems):
        local, remote = self._copies(ins, outs, sems)
        for cp in remote:
            cp.wait()
        local.wait()


class _PairSwap:
    def __init__(self, array):
        self.arrays = [array]
        self.out_shapes = [jax.ShapeDtypeStruct(array.shape[1:], array.dtype)]
        self.parts = _row_parts(array.shape[2])
        self.scratch = _dma_sems(4 * len(self.parts))

    def _copies(self, ins, outs, sems):
        (src,), (theirs,), (send, recv, _) = ins, outs, sems
        x, y, c, _ = _here()
        return [_rdma(src.at[1 - c, p, part], theirs.at[p, part], send, recv, p * len(self.parts) + q, (x, y, 1 - c))
                for p in range(4) for q, part in enumerate(self.parts)]

    def start(self, ins, outs, sems):
        for cp in self._copies(ins, outs, sems):
            cp.start()

    def finish(self, ins, outs, sems):
        for cp in self._copies(ins, outs, sems):
            cp.wait()


class _Comm:
    def __init__(self, ops):
        self.ops = ops
        self.arrays = [a for op in ops for a in op.arrays]
        self.out_shapes = [s for op in ops for s in op.out_shapes]
        self.scratch = [s for op in ops for s in op.scratch]

    def split(self, rest, n_out, n_scratch):
        cuts = np.cumsum([0, len(self.arrays), n_out, len(self.out_shapes), n_scratch, len(self.scratch)])
        assert cuts[-1] == len(rest)
        return tuple(rest[a:b] for a, b in zip(cuts[:-1], cuts[1:]))

    def _each(self, method, ins, outs, sems):
        i = o = s = 0
        for op in self.ops:
            ni, no, ns = len(op.arrays), len(op.out_shapes), len(op.scratch)
            getattr(op, method)(ins[i:i + ni], outs[o:o + no], sems[s:s + ns])
            i, o, s = i + ni, o + no, s + ns

    def start(self, ins, outs, sems):
        self._each("start", ins, outs, sems)

    def finish(self, ins, outs, sems):
        self._each("finish", ins, outs, sems)

    def run(self, name):
        def body(*refs):
            ins, _, outs, _, sems = self.split(refs, 0, 0)
            self.start(ins, outs, sems)
            self.finish(ins, outs, sems)

        hbm = pl.BlockSpec(memory_space=pl.ANY)
        return pl.pallas_call(body, in_specs=[hbm] * len(self.arrays), out_specs=[hbm] * len(self.out_shapes),
                              out_shape=self.out_shapes, scratch_shapes=self.scratch, name=name)(*self.arrays)


def _sum_slots(x, *, name, tr=None):
    n, r, l = x.shape
    tr = r if tr is None else tr

    def body(x_ref, o_ref):
        acc = x_ref[0].astype(F32)
        for s in range(1, n):
            acc = acc + x_ref[s].astype(F32)
        o_ref[...] = acc

    return pl.pallas_call(
        body, grid=(r // tr,), in_specs=[pl.BlockSpec((n, tr, l), lambda i: (0, i, 0))],
        out_specs=pl.BlockSpec((tr, l), lambda i: (i, 0)), out_shape=jax.ShapeDtypeStruct((r, l), F32),
        name=name, compiler_params=_cp(("parallel",), 48),
    )(x)


def _pair_add(both, theirs, *, name, tr):
    _, n, r, l = both.shape

    def body(a_ref, b_ref, o_ref):
        mine = jnp.where(lax.axis_index("c") == 0, a_ref[0], a_ref[1])
        o_ref[...] = (mine.astype(F32) + b_ref[...].astype(F32)).astype(BF16)

    spec = pl.BlockSpec((n, tr, l), lambda i: (0, i, 0))
    return pl.pallas_call(body, grid=(r // tr,), in_specs=[pl.BlockSpec((2, n, tr, l), lambda i: (0, 0, i, 0)), spec], out_specs=spec,
                          out_shape=jax.ShapeDtypeStruct(theirs.shape, BF16), name=name,
                          compiler_params=_cp(("parallel",), 48))(both, theirs)


ADAM_TILE_ELEMS = 512 * 1024


def _adam(w, g, m, v, *, name):
    shape = w.shape
    cols = shape[-1]
    rows = math.prod(shape[:-1]) if len(shape) > 1 else 1
    tr = rows
    if rows * cols > ADAM_TILE_ELEMS:
        tr = max(d for d in range(SUBLANES, ADAM_TILE_ELEMS // cols + 1, SUBLANES) if rows % d == 0)
    c1, c2 = 1.0 - ADAM_B1 ** ADAM_STEP, 1.0 - ADAM_B2 ** ADAM_STEP

    def body(w_ref, g_ref, m_ref, v_ref, d_ref, m2_ref, v2_ref):
        gv = g_ref[...]
        m2 = ADAM_B1 * m_ref[...] + (1.0 - ADAM_B1) * gv
        v2 = ADAM_B2 * v_ref[...] + (1.0 - ADAM_B2) * (gv * gv)
        d_ref[...] = -ADAM_LR * ((m2 / c1) / (jnp.sqrt(v2 / c2) + ADAM_EPS) + ADAM_WD * w_ref[...])
        m2_ref[...] = m2
        v2_ref[...] = v2

    spec = pl.BlockSpec((tr, cols), lambda i: (i, 0))
    outs = pl.pallas_call(
        body, grid=(rows // tr,), in_specs=[spec] * 4, out_specs=[spec] * 3,
        out_shape=[jax.ShapeDtypeStruct((rows, cols), F32)] * 3, name=name, compiler_params=_cp(("parallel",), 48),
    )(*(a.reshape(rows, cols) for a in (w, g, m, v)))
    return tuple(o.reshape(shape) for o in outs)


WEIGHTS = ['norm_mix', 'norm_ffn', 'ret_gdn_w_in', 'gdn_conv_w', 'gdn_a_log', 'gdn_dt_bias', 'gdn_out_gain', 'ret_gdn_w_out',
           'lru_w_in', 'lru_conv_w', 'lru_conv_b', 'lru_w_a', 'lru_b_a', 'lru_w_x', 'lru_b_x', 'lru_lambda', 'lru_w_out',
           'ffn_w_up', 'ffn_conv_w', 'ffn_conv_b', 'ffn_w_down', 'norm_final']
BIG = {'ret_gdn_w_in': ((1, 1024, 513), 2), 'ret_gdn_w_out': ((1, 128, 1024), 1), 'lru_w_in': ((1, 1024, 256), 2),
       'lru_w_out': ((1, 128, 1024), 1), 'ffn_w_up': ((2, 1024, 704), 2), 'ffn_w_down': ((2, 352, 1024), 1)}
SMALL = {'gdn_conv_w': ((1, 4, 192), 2), 'lru_conv_w': ((1, 4, 128), 2), 'lru_conv_b': ((1, 128), 1), 'lru_b_a': ((1, 128), 1),
         'lru_b_x': ((1, 128), 1), 'lru_lambda': ((1, 128), 1), 'ffn_conv_w': ((2, 3, 704), 2)}
REPLICATED = {'norm_mix': (2, 1024), 'norm_ffn': (2, 1024), 'gdn_a_log': (1, 4), 'gdn_dt_bias': (1, 4), 'gdn_out_gain': (1, 128),
              'lru_w_a': (1, 8, 128, 128), 'lru_w_x': (1, 8, 128, 128), 'ffn_conv_b': (2, 5632), 'norm_final': (1024,)}
FIRST = ['ret_gdn_w_in', 'ret_gdn_w_out']
REST = ['lru_w_in', 'lru_w_out', 'ffn_w_up', 'ffn_w_down']
GROUP_ROWS = {FIRST[0]: 5632, REST[0]: 19968}
GROUP_TILE = {FIRST[0]: 512, REST[0]: 1536}
EARLY = {'lru_conv_w': (1, 4, 1024), 'lru_conv_b': (1, 1024), 'lru_b_a': (1, 1024), 'lru_b_x': (1, 1024), 'lru_lambda': (1, 1024),
         'ffn_conv_w': (2, 3, 5632), 'norm_ffn': (2, 1024), 'norm_mix1': (1, 1024), 'lru_w_a': (1, 8, 128, 128),
         'lru_w_x': (1, 8, 128, 128), 'ffn_conv_b': (2, 5632), 'norm_final': (1024,)}
LATE = {'gdn_conv_w': (1, 4, 1536), 'norm_mix0': (1, 1024), 'gdn_a_log': (1, 4), 'gdn_dt_bias': (1, 4), 'gdn_out_gain': (1, 128)}


def _full_shape(shard, axis):
    return tuple(d * N_DEV if i == axis else d for i, d in enumerate(shard))


def _rows_of(n_elems):
    return -(-n_elems // LANES)


def _to_rows(a, lead=()):
    flat = a.reshape(lead + (-1,))
    pad = _rows_of(flat.shape[-1]) * LANES - flat.shape[-1]
    if pad:
        flat = jnp.pad(flat, [(0, 0)] * len(lead) + [(0, pad)])
    return flat.reshape(lead + (-1, LANES))


def _pack(pieces, total_rows, lead=()):
    buf = jnp.concatenate(pieces, axis=len(lead))
    pad = total_rows - buf.shape[len(lead)]
    return jnp.pad(buf, [(0, 0)] * len(lead) + [(0, pad), (0, 0)]) if pad else buf


def _unpack(buf, shapes, lead=()):
    out, off = [], 0
    for shape in shapes:
        n = math.prod(shape)
        rows = _rows_of(n)
        piece = lax.slice_in_dim(buf, off, off + rows, axis=len(lead)).reshape(lead + (rows * LANES,))
        out.append(lax.slice_in_dim(piece, 0, n, axis=len(lead)).reshape(lead + shape))
        off += rows
    return out


def _join_blocks(g, axis):
    m = jnp.moveaxis(g, 0, axis)
    return m.reshape(m.shape[:axis] + (N_DEV * m.shape[axis + 1],) + m.shape[axis + 2:])


def _split_blocks(full, axis):
    s = full.shape
    return jnp.moveaxis(full.reshape(s[:axis] + (N_DEV, s[axis] // N_DEV) + s[axis + 1:]), axis, 0)


def _small_rows(shapes):
    total = sum(_rows_of(math.prod(s)) for s in shapes)
    return -(-total // SUBLANES) * SUBLANES


FFN_TM = 1024
FFN_TN = 512


def _ffn_forward(h, gain, w_up, cw, cb, w_down, tag):
    t, d = h.shape
    tm, tn, blk, nb = min(FFN_TM, t), FFN_TN, FFN_BLK, FFN_NB
    hn = _norm_fwd(h, gain, name=f"ffn{tag}_norm")
    up = _mmx(hn, w_up, dims=NN, grid=(t // tm, 2 * nb, 1), name=f"ffn{tag}_up", tile=(tm, blk),
              a_spec=pl.BlockSpec((tm, d), lambda i, j, k: (i, 0)),
              b_spec=pl.BlockSpec((None, d, blk), lambda i, j, k: (j, 0, 0)),
              o_spec=pl.BlockSpec((None, None, tm, blk), lambda i, j, k: (j // nb, j % nb, i, 0)),
              out_shape=jax.ShapeDtypeStruct((2, nb, t, blk), F32))
    act = _ffn_act_fwd(up, cw, cb, name=f"ffn{tag}_act")
    out = _mmx(act, w_down, dims=NN, grid=(t // tm, d // tn, nb), name=f"ffn{tag}_down", tile=(tm, tn), res=h,
               a_spec=pl.BlockSpec((None, tm, blk), lambda i, j, k: (k, i, 0)),
               b_spec=pl.BlockSpec((blk, tn), lambda i, j, k: (k, j)),
               o_spec=pl.BlockSpec((tm, tn), lambda i, j, k: (i, j)),
               out_shape=jax.ShapeDtypeStruct((t, d), F32))
    return out, (hn, up, act)


def _ffn_backward(dh, h, gain, saved, w_up, cw, cb, w_down, tag):
    hn, up, act = saved
    t, d = h.shape
    tm, tn, blk, nb = min(FFN_TM, t), FFN_TN, FFN_BLK, FFN_NB
    tk = min(FFN_TM, t)
    da = _mmx(dh, w_down, dims=NT, grid=(t // tm, nb, 1), name=f"ffn{tag}_d_act", tile=(tm, blk),
              a_spec=pl.BlockSpec((tm, d), lambda i, j, k: (i, 0)),
              b_spec=pl.BlockSpec((blk, d), lambda i, j, k: (j, 0)),
              o_spec=pl.BlockSpec((None, tm, blk), lambda i, j, k: (j, i, 0)),
              out_shape=jax.ShapeDtypeStruct((nb, t, blk), F32))
    dwd = _mmx(act, dh, dims=TN, grid=(nb, d // tn, t // tk), name=f"ffn{tag}_d_wdown", tile=(blk, tn), split_rows=blk // 2,
               a_spec=pl.BlockSpec((None, tk, blk), lambda i, j, k: (i, k, 0)),
               b_spec=pl.BlockSpec((tk, tn), lambda i, j, k: (k, j)),
               o_spec=pl.BlockSpec((2, None, blk // 2, tn), lambda i, j, k: (0, i, 0, j)),
               out_shape=jax.ShapeDtypeStruct((2, nb, blk // 2, d), BF16))
    dup, dcw, dcb = _ffn_act_bwd(up, da, cw, cb, name=f"ffn{tag}_act_bwd")
    dhn = _mmx(dup, w_up, dims=NT, grid=(t // tm, d // tn, 2 * nb), name=f"ffn{tag}_d_hn", tile=(tm, tn),
               a_spec=pl.BlockSpec((None, None, tm, blk), lambda i, j, k: (k // nb, k % nb, i, 0)),
               b_spec=pl.BlockSpec((None, tn, blk), lambda i, j, k: (k, j, 0)),
               o_spec=pl.BlockSpec((tm, tn), lambda i, j, k: (i, j)),
               out_shape=jax.ShapeDtypeStruct((t, d), F32))
    dwu = _mmx(hn, dup, dims=TN, grid=(1, 2 * nb, t // tk), name=f"ffn{tag}_d_wup", tile=(d, blk),
               a_spec=pl.BlockSpec((tk, d), lambda i, j, k: (k, 0)),
               b_spec=pl.BlockSpec((None, None, tk, blk), lambda i, j, k: (j // nb, j % nb, k, 0)),
               o_spec=pl.BlockSpec((None, None, d, blk), lambda i, j, k: (j % 2, j // 2, 0, 0)),
               out_shape=jax.ShapeDtypeStruct((2, N_DEV // 2, d, blk), BF16))
    dh_in, dgain = _norm_bwd(h, gain, dhn, dh, name=f"ffn{tag}_norm_bwd")
    conv_w = dcw.transpose(2, 0, 1, 3).reshape(3, 2 * nb * blk)
    return dh_in, dict(w_up=dwu, w_down=dwd, conv_w=conv_w, conv_b=dcb.reshape(1, 2 * nb * blk), norm=dgain)


def kernel(x, norm_mix, norm_ffn, ret_gdn_w_in, gdn_conv_w, gdn_a_log, gdn_dt_bias, gdn_out_gain, ret_gdn_w_out, lru_w_in, lru_conv_w, lru_conv_b, lru_w_a, lru_b_a, lru_w_x, lru_b_x, lru_lambda, lru_w_out, ffn_w_up, ffn_conv_w, ffn_conv_b, ffn_w_down, norm_final, loss_target, m_norm_mix, m_norm_ffn, m_ret_gdn_w_in, m_gdn_conv_w, m_gdn_a_log, m_gdn_dt_bias, m_gdn_out_gain, m_ret_gdn_w_out, m_lru_w_in, m_lru_conv_w, m_lru_conv_b, m_lru_w_a, m_lru_b_a, m_lru_w_x, m_lru_b_x, m_lru_lambda, m_lru_w_out, m_ffn_w_up, m_ffn_conv_w, m_ffn_conv_b, m_ffn_w_down, m_norm_final, v_norm_mix, v_norm_ffn, v_ret_gdn_w_in, v_gdn_conv_w, v_gdn_a_log, v_gdn_dt_bias, v_gdn_out_gain, v_ret_gdn_w_out, v_lru_w_in, v_lru_conv_w, v_lru_conv_b, v_lru_w_a, v_lru_b_a, v_lru_w_x, v_lru_b_x, v_lru_lambda, v_lru_w_out, v_ffn_w_up, v_ffn_conv_w, v_ffn_conv_b, v_ffn_w_down, v_norm_final):
    given = dict(locals())
    w = {n: given[n] for n in WEIGHTS}
    me = 4 * lax.axis_index("x") + 2 * lax.axis_index("y") + lax.axis_index("c")
    t = x.shape[1]
    f = D_FF

    first_shards = {'w_in0': ret_gdn_w_in[0], 'w_out0': ret_gdn_w_out[0]}
    rest_shards = {'lru_in': lru_w_in[0], 'lru_out': lru_w_out[0], 'up0': ffn_w_up[0], 'up1': ffn_w_up[1],
                   'down0': ffn_w_down[0], 'down1': ffn_w_down[1]}
    small_shapes = [s for s, _ in SMALL.values()]
    small_buf = _pack([_to_rows(w[n]) for n in SMALL], _small_rows(small_shapes))
    *g_first, g_small = _Comm([_AllGather(a.astype(BF16)) for a in first_shards.values()] + [_AllGather(small_buf)]).run("gather_first")
    rest_gather = _Comm([_AllGather(a.astype(BF16)) for a in rest_shards.values()])
    got = dict(zip(first_shards, g_first))
    small_blocks = dict(zip(SMALL, _unpack(g_small, small_shapes, lead=(N_DEV,))))
    full = {n: _join_blocks(small_blocks[n], SMALL[n][1]) for n in SMALL if n != 'ffn_conv_w'}

    w_in0 = _join_blocks(got['w_in0'], 1)
    w_main = w_in0[:, :MAIN_IN]
    w_narrow = jnp.pad(w_in0[:, MAIN_IN:], ((0, 0), (0, LANES - SMALL_IN)))
    w_out0 = got['w_out0'].reshape(D_MODEL, D_MODEL)
    fcw = [small_blocks['ffn_conv_w'][:, l].reshape(2, FFN_NB, 3, FFN_BLK) for l in range(2)]
    fcb = [ffn_conv_b[l].reshape(2, FFN_NB, 1, FFN_BLK) for l in range(2)]
    gdn_cw = full['gdn_conv_w'][0]
    al_pad = jnp.pad(gdn_a_log, ((0, 0), (HEADS, LANES - 2 * HEADS)))
    dt_pad = jnp.pad(gdn_dt_bias, ((0, 0), (HEADS, LANES - 2 * HEADS)))
    lru_cw, lru_cb = full['lru_conv_w'][0], full['lru_conv_b']
    lru_ba, lru_bx, lru_lam = full['lru_b_a'], full['lru_b_x'], full['lru_lambda']
    wa, wx = lru_w_a[0], lru_w_x[0]

    h0, target = x[0], loss_target[0]
    hn0 = _norm_fwd(h0, norm_mix[0:1], name="mix0_norm")
    proj = _mm(hn0, w_main, name="mix0_in")
    pnarrow = _mm(hn0, w_narrow, name="mix0_in_narrow")
    tables = _ret_tables(t)
    y0, ret_states = _ret_fwd(proj, tables, name="retention_fwd")
    y0, gdn_states, g_rest = _gdn_fwd(proj, pnarrow, gdn_cw, al_pad, dt_pad, gdn_out_gain, y0, rest_gather, name="deltanet_fwd")
    got.update(zip(rest_shards, g_rest))
    lru_in = _join_blocks(got['lru_in'], 1)
    lru_in_g, lru_in_x = lru_in[:, :D_MODEL], lru_in[:, D_MODEL:]
    lru_out = got['lru_out'].reshape(D_MODEL, D_MODEL)
    w_up = [got['up0'], got['up1']]
    down = [got['down0'].reshape(D_FF, D_MODEL), got['down1'].reshape(D_FF, D_MODEL)]
    h1 = _mm(y0, w_out0, res=h0, name="mix0_out")
    h2, ffn0_saved = _ffn_forward(h1, norm_ffn[0:1], w_up[0], fcw[0], fcb[0], down[0], 0)
    hn1 = _norm_fwd(h2, norm_mix[1:2], name="mix1_norm")
    gate = _mm(hn1, lru_in_g, name="mix1_in_gate")
    xpre = _mm(hn1, lru_in_x, name="mix1_in_x")
    y1, hs = _lru_fwd(gate, xpre, lru_cw, lru_cb, wa, lru_ba, wx, lru_bx, lru_lam, name="rglru_fwd")
    h3 = _mm(y1, lru_out, res=h2, name="mix1_out")
    h4, ffn1_saved = _ffn_forward(h3, norm_ffn[1:2], w_up[1], fcw[1], fcb[1], down[1], 1)
    dh4, d_norm_final, loss_part = _final_loss(h4, norm_final[None, :], target, name="final_norm_loss")
    loss = lax.psum(loss_part[0, 0], ("x", "y", "c"))

    dh3, gf1 = _ffn_backward(dh4, h3, norm_ffn[1:2], ffn1_saved, w_up[1], fcw[1], fcb[1], down[1], 1)
    dy1 = _mm(dh3, lru_out, tb=True, name="mix1_d_y")
    d_lru_out = _mm(y1, dh3, ta=True, out_dtype=BF16, name="mix1_d_wout")
    dgate, dxpre, d_lcw, d_lcb, d_wa, d_ba, d_wx, d_bx, d_lam = _lru_bwd(
        gate, xpre, hs, dy1, lru_cw, lru_cb, wa, lru_ba, wx, lru_bx, lru_lam, name="rglru_bwd")
    dhn1 = _mm(dgate, lru_in_g, tb=True, name="mix1_d_hn_gate")
    dhn1 = _mm(dxpre, lru_in_x, tb=True, res=dhn1, name="mix1_d_hn_x")
    d_lru_in = jnp.concatenate([_mm(hn1, dgate, ta=True, out_dtype=BF16, name="mix1_d_win_gate"),
                                _mm(hn1, dxpre, ta=True, out_dtype=BF16, name="mix1_d_win_x")], axis=1)
    dh2, d_mix1 = _norm_bwd(h2, norm_mix[1:2], dhn1, dh3, name="mix1_norm_bwd")
    dh1, gf0 = _ffn_backward(dh2, h1, norm_ffn[0:1], ffn0_saved, w_up[0], fcw[0], fcb[0], down[0], 0)
    dy0 = _mm(dh1, w_out0, tb=True, name="mix0_d_y")
    d_w_out0 = _mm(y0, dh1, ta=True, out_dtype=BF16, name="mix0_d_wout")

    def by_core_chip(full_grad, axis):
        blocks = _split_blocks(full_grad, axis)
        return blocks.reshape((4, 2) + blocks.shape[1:]).transpose(1, 0, 2, 3)

    def pair_sums(blocks, tag):
        theirs = _Comm([_PairSwap(b) for b in blocks.values()]).run(f"pair_swap_{tag}")
        return {k: _pair_add(b, o, name=f"pair_add_{k}", tr=row_tile(b.shape[2])) for (k, b), o in zip(blocks.items(), theirs)}

    row_tile = lambda rows: rows if rows <= 512 else 256

    z_rest = pair_sums({'lru_in': by_core_chip(d_lru_in, 1), 'lru_out': by_core_chip(d_lru_out, 0), 'up0': gf0['w_up'],
                        'up1': gf1['w_up'], 'down0': gf0['w_down'], 'down1': gf1['w_down']}, "rest")
    early = {'lru_conv_w': d_lcw[None], 'lru_conv_b': d_lcb, 'lru_b_a': d_ba, 'lru_b_x': d_bx, 'lru_lambda': d_lam,
             'ffn_conv_w': jnp.stack([gf0['conv_w'], gf1['conv_w']]), 'norm_ffn': jnp.concatenate([gf0['norm'], gf1['norm']], axis=0),
             'norm_mix1': d_mix1, 'lru_w_a': d_wa[None], 'lru_w_x': d_wx[None],
             'ffn_conv_b': jnp.concatenate([gf0['conv_b'], gf1['conv_b']], axis=0), 'norm_final': d_norm_final[0]}
    early_buf = _pack([_to_rows(early[n]) for n in EARLY], _small_rows(list(EARLY.values())))
    rest_exchange = _Comm([_ChipExchange(z) for z in z_rest.values()] + [_AllGather(early_buf)])

    dproj = _ret_bwd(proj, tables, ret_states, dy0, name="retention_bwd")
    (dproj, dnarrow, d_gcw, d_alog, d_dtb, d_gain), (*w_rest, got_early) = _gdn_bwd(
        proj, pnarrow, gdn_cw, al_pad, dt_pad, gdn_out_gain, gdn_states, dy0, dproj, rest_exchange, name="deltanet_bwd")
    dhn0 = _mm(dproj, w_main, tb=True, name="mix0_d_hn")
    dhn0 = _mm(dnarrow, w_narrow, tb=True, res=dhn0, name="mix0_d_hn_narrow")
    d_w_main = _mm(hn0, dproj, ta=True, out_dtype=BF16, name="mix0_d_win")
    d_w_narrow = _mm(hn0, dnarrow, ta=True, out_dtype=BF16, name="mix0_d_win_narrow")
    dx, d_mix0 = _norm_bwd(h0, norm_mix[0:1], dhn0, dh1, name="mix0_norm_bwd")

    d_w_in0 = jnp.concatenate([d_w_main, d_w_narrow[:, :SMALL_IN]], axis=1)
    z_first = pair_sums({'w_in0': by_core_chip(d_w_in0, 1), 'w_out0': by_core_chip(d_w_out0, 0)}, "first")
    late = {'gdn_conv_w': d_gcw[None], 'norm_mix0': d_mix0, 'gdn_a_log': d_alog[:, HEADS:2 * HEADS],
            'gdn_dt_bias': d_dtb[:, HEADS:2 * HEADS], 'gdn_out_gain': d_gain}
    late_buf = _pack([_to_rows(late[n]) for n in LATE], _small_rows(list(LATE.values())))
    *w_first, got_late = _Comm([_ChipExchange(z) for z in z_first.values()] + [_AllGather(late_buf)]).run("exchange_first")

    summed = {k: _sum_slots(blocks, name=f"sum_blocks_{k}", tr=row_tile(blocks.shape[1]))
              for k, blocks in list(zip(z_rest, w_rest)) + list(zip(z_first, w_first))}
    grads = {'ret_gdn_w_in': summed['w_in0'][None], 'ret_gdn_w_out': summed['w_out0'][None], 'lru_w_in': summed['lru_in'][None],
             'lru_w_out': summed['lru_out'][None], 'ffn_w_up': jnp.stack([summed['up0'], summed['up1']]),
             'ffn_w_down': jnp.stack([summed['down0'], summed['down1']])}
    partial = dict(zip(EARLY, _unpack(_sum_slots(got_early, name="sum_partials_early"), list(EARLY.values()))))
    partial.update(zip(LATE, _unpack(_sum_slots(got_late, name="sum_partials_late"), list(LATE.values()))))
    partial['norm_mix'] = jnp.concatenate([partial.pop('norm_mix0'), partial.pop('norm_mix1')], axis=0)
    for n, g_full in partial.items():
        if n in SMALL:
            shard, axis = SMALL[n]
            g_full = lax.dynamic_slice_in_dim(g_full, me * shard[axis], shard[axis], axis=axis)
        grads[n] = g_full

    delta, new_m, new_v = {}, {}, {}
    for n in WEIGHTS:
        delta[n], new_m[n], new_v[n] = _adam(w[n], grads[n], given["m_" + n], given["v_" + n], name=f"adamw_{n}")
    return (loss, dx[None], *[grads[n] for n in WEIGHTS], *[delta[n] for n in WEIGHTS],
            *[new_m[n] for n in WEIGHTS], *[new_v[n] for n in WEIGHTS])
```

```python
import functools
import math

import numpy as np
import jax
import jax.numpy as jnp
from jax import lax
from jax.experimental import pallas as pl
from jax.experimental.pallas import tpu as pltpu

F32 = jnp.float32
BF16 = jnp.bfloat16
HI = lax.Precision.HIGHEST
MESH = pl.DeviceIdType.MESH

N_DEV = 8
LANES = 128
SUBLANES = 8
EPS = 1e-6
D_MODEL = 1024
HEADS = 4
HEAD_DIM = 128
RET_CHUNK = 128
GDN_CHUNK = 64
ROPE_BASE = 10000.0
LRU_C = 8.0
D_FF = 2816
MAIN_IN = 4096
SMALL_IN = 8
QSCALE = HEAD_DIM ** -0.5

ADAM_LR, ADAM_B1, ADAM_B2, ADAM_EPS, ADAM_WD, ADAM_STEP = 0.001, 0.9, 0.999, 1e-08, 0.01, 10


def _cp(sem=None, vmem_mb=None):
    kw = {}
    if sem is not None:
        kw["dimension_semantics"] = sem
    if vmem_mb is not None:
        kw["vmem_limit_bytes"] = vmem_mb << 20
    return pltpu.CompilerParams(**kw)


def _rows(shape):
    return lax.broadcasted_iota(jnp.int32, shape, 0)


def _cols(shape):
    return lax.broadcasted_iota(jnp.int32, shape, 1)


def _shift_down(cur, prev8, s):
    if s == 0:
        return cur
    rc = pltpu.roll(cur, s, 0)
    rp = pltpu.roll(prev8, s, 0)
    top = jnp.where(_rows(prev8.shape) < s, rp, rc[:SUBLANES])
    return jnp.concatenate([top, rc[SUBLANES:]], axis=0)


def _shift_up(cur, next8, s):
    if s == 0:
        return cur
    tt = cur.shape[0]
    rc = pltpu.roll(cur, tt - s, 0)
    rn = pltpu.roll(next8, SUBLANES - s, 0)
    bot = jnp.where(_rows(next8.shape) >= SUBLANES - s, rn, rc[tt - SUBLANES:])
    return jnp.concatenate([rc[:tt - SUBLANES], bot], axis=0)


def _down_fill(x, d, fill):
    return jnp.where(_rows(x.shape) < d, fill, pltpu.roll(x, d, 0))


def _up_fill(x, d, fill):
    tt = x.shape[0]
    return jnp.where(_rows(x.shape) >= tt - d, fill, pltpu.roll(x, tt - d, 0))


def _sigmoid(x):
    return 1.0 / (1.0 + jnp.exp(-x))


def _softplus(x):
    return jnp.maximum(x, 0.0) + jnp.log(1.0 + jnp.exp(-jnp.abs(x)))


def _dot(a, b, dims=(((1,), (0,)), ((), ())), precision=None):
    return lax.dot_general(a, b, dims, preferred_element_type=F32, precision=precision)


NN = (((1,), (0,)), ((), ()))
NT = (((1,), (1,)), ((), ()))
TN = (((0,), (0,)), ((), ()))


def _bdot(a, b, dims=NN):
    return _dot(a.astype(BF16), b.astype(BF16), dims)


def _split(a):
    hi = a.astype(BF16)
    return hi, (a - hi.astype(F32)).astype(BF16)


def _dot3(a, b, dims=NN):
    ah, al = _split(a)
    bh, bl = _split(b)
    return _dot(ah, bh, dims) + (_dot(ah, bl, dims) + _dot(al, bh, dims))


def _tile(dim, target):
    if dim <= target:
        return dim
    best = None
    for c in range(LANES, target + 1, LANES):
        if dim % c == 0:
            best = c
    assert best is not None, (dim, target)
    return best


def _mm(a, b, *, name, ta=False, tb=False, out_dtype=F32, res=None, tm=2048, tn=512, tk=1024):
    m, k = (a.shape[1], a.shape[0]) if ta else a.shape
    n = b.shape[0] if tb else b.shape[1]
    tn, tk = _tile(n, tn), _tile(k, tk)
    tm = _tile(m, tm if max(tn, tk) <= 1024 else tm // 2)
    nk = k // tk
    dims = (((0 if ta else 1,), (1 if tb else 0,)), ((), ()))

    def body(*refs):
        a_ref, b_ref = refs[:2]
        r_ref = refs[2] if res is not None else None
        o_ref = refs[3] if res is not None else refs[2]
        acc = refs[-1]
        kk = pl.program_id(2)
        part = _bdot(a_ref[...], b_ref[...], dims)

        def finish(r):
            if res is not None:
                r = r + r_ref[...]
            o_ref[...] = r.astype(out_dtype)

        if nk == 1:
            finish(part)
            return

        @pl.when(kk == 0)
        def _():
            acc[...] = part

        @pl.when(jnp.logical_and(kk > 0, kk < nk - 1))
        def _():
            acc[...] += part

        @pl.when(kk == nk - 1)
        def _():
            finish(acc[...] + part)

    a_spec = pl.BlockSpec((tk, tm), lambda i, j, kk: (kk, i)) if ta else pl.BlockSpec((tm, tk), lambda i, j, kk: (i, kk))
    b_spec = pl.BlockSpec((tn, tk), lambda i, j, kk: (j, kk)) if tb else pl.BlockSpec((tk, tn), lambda i, j, kk: (kk, j))
    o_spec = pl.BlockSpec((tm, tn), lambda i, j, kk: (i, j))
    in_specs = [a_spec, b_spec] + ([o_spec] if res is not None else [])
    args = (a, b) + ((res,) if res is not None else ())
    return pl.pallas_call(
        body, grid=(m // tm, n // tn, nk), in_specs=in_specs, out_specs=o_spec,
        out_shape=jax.ShapeDtypeStruct((m, n), out_dtype),
        scratch_shapes=[pltpu.VMEM((tm, tn), F32)] if nk > 1 else [], name=name,
        compiler_params=_cp(("parallel", "parallel", "arbitrary"), 56),
    )(*args)


def _mmx(a, b, *, dims, grid, a_spec, b_spec, o_spec, out_shape, tile, name, res=None, split_rows=None):
    nk = grid[-1]

    def body(*refs):
        a_ref, b_ref = refs[:2]
        r_ref = refs[2] if res is not None else None
        o_ref = refs[3] if res is not None else refs[2]
        acc = refs[-1]
        part = _bdot(a_ref[...], b_ref[...], dims)

        def finish(r):
            if res is not None:
                r = r + r_ref[...]
            if split_rows is None:
                o_ref[...] = r.astype(o_ref.dtype)
            else:
                o_ref[0] = r[:split_rows].astype(o_ref.dtype)
                o_ref[1] = r[split_rows:].astype(o_ref.dtype)

        if nk == 1:
            finish(part)
            return
        kk = pl.program_id(len(grid) - 1)

        @pl.when(kk == 0)
        def _():
            acc[...] = part

        @pl.when(jnp.logical_and(kk > 0, kk < nk - 1))
        def _():
            acc[...] += part

        @pl.when(kk == nk - 1)
        def _():
            finish(acc[...] + part)

    args = (a, b) + ((res,) if res is not None else ())
    return pl.pallas_call(
        body, grid=grid, in_specs=[a_spec, b_spec] + ([o_spec] if res is not None else []), out_specs=o_spec,
        out_shape=out_shape, scratch_shapes=[pltpu.VMEM(tile, F32)] if nk > 1 else [], name=name,
        compiler_params=_cp(("parallel",) * (len(grid) - 1) + ("arbitrary",), 56),
    )(*args)


W_IN_BLK = 513
W_IN_TR = 256


def _join_w_in(blocks, *, name):
    _, d, _ = blocks.shape
    tr = W_IN_TR

    def body(x_ref, main_ref, narrow_ref):
        for m in range(MAIN_IN // LANES):
            lo = LANES * m
            dev, off = divmod(lo, W_IN_BLK)
            if off + LANES <= W_IN_BLK:
                main_ref[:, lo:lo + LANES] = x_ref[dev, :, off:off + LANES]
            else:
                main_ref[:, lo:lo + LANES] = jnp.concatenate(
                    [x_ref[dev, :, off:W_IN_BLK], x_ref[dev + 1, :, 0:LANES - (W_IN_BLK - off)]], axis=1)
        tail = x_ref[N_DEV - 1, :, W_IN_BLK - SMALL_IN:W_IN_BLK]
        narrow_ref[...] = jnp.concatenate([tail, jnp.zeros((tr, LANES - SMALL_IN), tail.dtype)], axis=1)

    return pl.pallas_call(
        body, grid=(d // tr,), in_specs=[pl.BlockSpec((N_DEV, tr, W_IN_BLK), lambda i: (0, i, 0))],
        out_specs=[pl.BlockSpec((tr, MAIN_IN), lambda i: (i, 0)), pl.BlockSpec((tr, LANES), lambda i: (i, 0))],
        out_shape=[jax.ShapeDtypeStruct((d, MAIN_IN), blocks.dtype), jax.ShapeDtypeStruct((d, LANES), blocks.dtype)],
        name=name, compiler_params=_cp(("parallel",), 48),
    )(blocks)


def _split_w_in(main, narrow, *, name):
    d = main.shape[0]
    tr = W_IN_TR

    def body(m_ref, n_ref, o_ref):
        for dev in range(N_DEV):
            lo = W_IN_BLK * dev
            if dev < N_DEV - 1:
                piece = m_ref[:, lo:lo + W_IN_BLK]
            else:
                piece = jnp.concatenate([m_ref[:, lo:MAIN_IN], n_ref[:, 0:SMALL_IN]], axis=1)
            o_ref[dev % 2, dev // 2] = piece

    return pl.pallas_call(
        body, grid=(d // tr,),
        in_specs=[pl.BlockSpec((tr, MAIN_IN), lambda i: (i, 0)), pl.BlockSpec((tr, LANES), lambda i: (i, 0))],
        out_specs=pl.BlockSpec((2, N_DEV // 2, tr, W_IN_BLK), lambda i: (0, 0, i, 0)),
        out_shape=jax.ShapeDtypeStruct((2, N_DEV // 2, d, W_IN_BLK), main.dtype),
        name=name, compiler_params=_cp(("parallel",), 48),
    )(main, narrow)


def _norm_fwd(h, gain, *, name, tt=256):
    t, d = h.shape
    tt = min(tt, t)

    def body(h_ref, g_ref, o_ref):
        x = h_ref[...]
        r = lax.rsqrt(jnp.mean(x * x, axis=-1, keepdims=True) + EPS)
        o_ref[...] = (x * r * g_ref[...]).astype(BF16)

    row = pl.BlockSpec((tt, d), lambda i: (i, 0))
    return pl.pallas_call(
        body, grid=(t // tt,), in_specs=[row, pl.BlockSpec((1, d), lambda i: (0, 0))], out_specs=row,
        out_shape=jax.ShapeDtypeStruct((t, d), BF16), name=name, compiler_params=_cp(("parallel",)),
    )(h, gain)


def _norm_bwd(h, gain, dhn, dres, *, name, tt=256):
    t, d = h.shape
    tt = min(tt, t)

    def body(h_ref, g_ref, dy_ref, dr_ref, dx_ref, dg_ref):
        x, dy = h_ref[...], dy_ref[...]
        r = lax.rsqrt(jnp.mean(x * x, axis=-1, keepdims=True) + EPS)
        xh = x * r

        @pl.when(pl.program_id(0) == 0)
        def _():
            dg_ref[...] = jnp.zeros_like(dg_ref)

        dg_ref[...] += jnp.sum(dy * xh, axis=0, keepdims=True)
        dxh = dy * g_ref[...]
        dx_ref[...] = dr_ref[...] + r * (dxh - xh * jnp.mean(dxh * xh, axis=-1, keepdims=True))

    row = pl.BlockSpec((tt, d), lambda i: (i, 0))
    vec = pl.BlockSpec((1, d), lambda i: (0, 0))
    return pl.pallas_call(
        body, grid=(t // tt,), in_specs=[row, vec, row, row], out_specs=[row, vec],
        out_shape=[jax.ShapeDtypeStruct((t, d), F32), jax.ShapeDtypeStruct((1, d), F32)],
        name=name, compiler_params=_cp(("arbitrary",)),
    )(h, gain, dhn, dres)


def _final_loss(h, gain, target, *, name, tt=256):
    t, d = h.shape
    tt = min(tt, t)

    def body(h_ref, g_ref, tg_ref, dx_ref, dg_ref, loss_ref):
        x = h_ref[...]
        r = lax.rsqrt(jnp.mean(x * x, axis=-1, keepdims=True) + EPS)
        xh = x * r
        err = xh * g_ref[...] - tg_ref[...]

        @pl.when(pl.program_id(0) == 0)
        def _():
            dg_ref[...] = jnp.zeros_like(dg_ref)
            loss_ref[...] = jnp.zeros_like(loss_ref)

        loss_ref[...] += 0.5 * jnp.sum(jnp.mean(err * err, axis=-1, keepdims=True), axis=0, keepdims=True)
        dy = err * (1.0 / d)
        dg_ref[...] += jnp.sum(dy * xh, axis=0, keepdims=True)
        dxh = dy * g_ref[...]
        dx_ref[...] = r * (dxh - xh * jnp.mean(dxh * xh, axis=-1, keepdims=True))

    row = pl.BlockSpec((tt, d), lambda i: (i, 0))
    vec = pl.BlockSpec((1, d), lambda i: (0, 0))
    return pl.pallas_call(
        body, grid=(t // tt,), in_specs=[row, vec, row],
        out_specs=[row, vec, pl.BlockSpec((1, 1), lambda i: (0, 0))],
        out_shape=[jax.ShapeDtypeStruct((t, d), F32), jax.ShapeDtypeStruct((1, d), F32), jax.ShapeDtypeStruct((1, 1), F32)],
        name=name, compiler_params=_cp(("arbitrary",)),
    )(h, gain, target)


FFN_BLK = 704
FFN_NB = 4
FFN_TT = 256


def _prev8(n, tt):
    return jnp.maximum(n * (tt // SUBLANES) - 1, 0)


def _ffn_conv(cur, prev8, w, b):
    s1 = _shift_down(cur, prev8, 1)
    s2 = _shift_down(cur, prev8, 2)
    return w[0:1] * s2 + w[1:2] * s1 + w[2:3] * cur + b, s1, s2


def _ffn_specs(t, tt, order):
    pair = lambda rows, row_index: pl.BlockSpec((2, None, rows, FFN_BLK), lambda j, n: (0, j, row_index(n), 0))
    return dict(cur=pair(tt, order), prev=pair(SUBLANES, lambda n: _prev8(order(n), tt)), w=pair(3, lambda n: 0), b=pair(1, lambda n: 0),
                one=pl.BlockSpec((None, tt, FFN_BLK), lambda j, n: (j, order(n), 0)))


def _ffn_act_fwd(up, cw, cb, *, name):
    t = up.shape[2]
    tt = min(FFN_TT, t)
    sp = _ffn_specs(t, tt, lambda n: n)

    def body(u_ref, p_ref, w_ref, b_ref, o_ref):
        first = pl.program_id(1) == 0
        gate, _, _ = _ffn_conv(u_ref[0], jnp.where(first, 0.0, p_ref[0]), w_ref[0], b_ref[0])
        val, _, _ = _ffn_conv(u_ref[1], jnp.where(first, 0.0, p_ref[1]), w_ref[1], b_ref[1])
        o_ref[...] = (gate * _sigmoid(gate) * val).astype(BF16)

    return pl.pallas_call(
        body, grid=(FFN_NB, t // tt), in_specs=[sp["cur"], sp["prev"], sp["w"], sp["b"]], out_specs=sp["one"],
        out_shape=jax.ShapeDtypeStruct((FFN_NB, t, FFN_BLK), BF16), name=name,
        compiler_params=_cp(("parallel", "arbitrary"), 48),
    )(up, up, cw, cb)


def _ffn_act_bwd(up, da, cw, cb, *, name):
    t = up.shape[2]
    tt = min(FFN_TT, t)
    nt = t // tt
    sp = _ffn_specs(t, tt, lambda n: nt - 1 - n)

    def body(u_ref, p_ref, da_ref, w_ref, b_ref, du_ref, dw_ref, db_ref, head):
        n = pl.program_id(1)
        tile0 = n == nt - 1

        @pl.when(n == 0)
        def _():
            for r in (head, dw_ref, db_ref):
                r[...] = jnp.zeros_like(r)

        convs = [_ffn_conv(u_ref[s], jnp.where(tile0, 0.0, p_ref[s]), w_ref[s], b_ref[s]) for s in range(2)]
        gate, val = convs[0][0], convs[1][0]
        d = da_ref[...]
        sg = _sigmoid(gate)
        dcs = (d * val * sg * (1.0 + gate * (1.0 - sg)), d * gate * sg)
        for s in range(2):
            dc, w, hd = dcs[s], w_ref[s], head[s]
            _, x1, x2 = convs[s]
            du_ref[s] = (w[2:3] * dc + w[1:2] * _shift_up(dc, hd, 1) + w[0:1] * _shift_up(dc, hd, 2)).astype(BF16)
            dw_ref[s, 0:1, :] += jnp.sum(dc * x2, axis=0, keepdims=True)
            dw_ref[s, 1:2, :] += jnp.sum(dc * x1, axis=0, keepdims=True)
            dw_ref[s, 2:3, :] += jnp.sum(dc * u_ref[s], axis=0, keepdims=True)
            db_ref[s] += jnp.sum(dc, axis=0, keepdims=True)
            head[s] = dc[:SUBLANES]

    return pl.pallas_call(
        body, grid=(FFN_NB, nt), in_specs=[sp["cur"], sp["prev"], sp["one"], sp["w"], sp["b"]],
        out_specs=[sp["cur"], sp["w"], sp["b"]],
        out_shape=[jax.ShapeDtypeStruct(up.shape, BF16), jax.ShapeDtypeStruct(cw.shape, F32), jax.ShapeDtypeStruct(cb.shape, F32)],
        scratch_shapes=[pltpu.VMEM((2, SUBLANES, FFN_BLK), F32)], name=name,
        compiler_params=_cp(("parallel", "arbitrary"), 48),
    )(up, up, da, cw, cb)


LRU_TT = 256
LRU_CT = 512
GELU_C = math.sqrt(2.0 / math.pi)
GELU_A = 0.044715


def _gelu(x):
    return 0.5 * x * (1.0 + jnp.tanh(GELU_C * (x + GELU_A * x * x * x)))


def _gelu_grad(x):
    th = jnp.tanh(GELU_C * (x + GELU_A * x * x * x))
    return 0.5 * (1.0 + th) + 0.5 * x * (1.0 - th * th) * GELU_C * (1.0 + 3.0 * GELU_A * x * x)


def _neg_expm1(x):
    poly = -x * (1.0 + x * (0.5 + x * (1.0 / 6 + x * (1.0 / 24 + x * (1.0 / 120)))))
    return jnp.where(x > -0.1, poly, 1.0 - jnp.exp(x))


def _conv4(x, p8, w, b=None):
    s1, s2, s3 = _shift_down(x, p8, 1), _shift_down(x, p8, 2), _shift_down(x, p8, 3)
    y = w[0:1] * s3 + w[1:2] * s2 + w[2:3] * s1 + w[3:4] * x
    return (y if b is None else y + b), (s1, s2, s3)


def _conv4_bwd(dy, head, x, shifts, w):
    s1, s2, s3 = shifts
    dx = w[3:4] * dy + w[2:3] * _shift_up(dy, head, 1) + w[1:2] * _shift_up(dy, head, 2) + w[0:1] * _shift_up(dy, head, 3)
    dws = [jnp.sum(dy * s, axis=0, keepdims=True) for s in (s3, s2, s1, x)]
    return dx, dws


def _blockdiag(x, w_ref, dims=NN):
    nb = x.shape[1] // LANES
    return jnp.concatenate([_bdot(x[:, LANES * i:LANES * (i + 1)], w_ref[i], dims) for i in range(nb)], axis=1)


def _lru_gates(xr, wa_ref, wx_ref, ba, bx, lam):
    r = _sigmoid(_blockdiag(xr, wa_ref) + ba)
    i = _sigmoid(_blockdiag(xr, wx_ref) + bx)
    sp = _softplus(-lam)
    la = -LRU_C * r * sp
    a = jnp.exp(la)
    mult = jnp.sqrt(_neg_expm1(2.0 * la))
    return r, i, sp, a, mult


def _lru_specs(t, tt, ct, order):
    nb = ct // LANES
    cur = pl.BlockSpec((tt, ct), lambda j, n: (order(n), j))
    prev = pl.BlockSpec((SUBLANES, ct), lambda j, n: (_prev8(order(n), tt), j))
    vec = lambda rows: pl.BlockSpec((rows, ct), lambda j, n: (0, j))
    blk = pl.BlockSpec((nb, LANES, LANES), lambda j, n: (j, 0, 0))
    return cur, prev, vec, blk


def _lru_fwd(gate, xpre, cw, cb, wa, ba, wx, bx, lam, *, name):
    t, c = gate.shape
    tt, ct = min(LRU_TT, t), LRU_CT
    cur, prev, vec, blk = _lru_specs(t, tt, ct, lambda n: n)

    def body(gate_ref, x_ref, p_ref, cw_ref, cb_ref, wa_ref, ba_ref, wx_ref, bx_ref, lam_ref, y_ref, hs_ref, carry):
        n = pl.program_id(1)

        @pl.when(n == 0)
        def _():
            carry[...] = jnp.zeros_like(carry)

        p8 = jnp.where(n == 0, 0.0, p_ref[...])
        xr, _ = _conv4(x_ref[...], p8, cw_ref[...], cb_ref[...])
        r, i, sp, a, mult = _lru_gates(xr, wa_ref, wx_ref, ba_ref[...], bx_ref[...], lam_ref[...])
        acc_a, acc_b = a, mult * (i * xr)
        d = 1
        while d < tt:
            acc_b = acc_a * _down_fill(acc_b, d, 0.0) + acc_b
            acc_a = acc_a * _down_fill(acc_a, d, 1.0)
            d *= 2
        hs = acc_b + acc_a * carry[0:1]
        carry[...] = jnp.broadcast_to(hs[tt - 1:tt], carry.shape)
        hs_ref[...] = hs
        y_ref[...] = (_gelu(gate_ref[...]) * hs).astype(BF16)

    return pl.pallas_call(
        body, grid=(c // ct, t // tt),
        in_specs=[cur, cur, prev, vec(4), vec(1), blk, vec(1), blk, vec(1), vec(1)],
        out_specs=[cur, cur],
        out_shape=[jax.ShapeDtypeStruct((t, c), BF16), jax.ShapeDtypeStruct((t, c), F32)],
        scratch_shapes=[pltpu.VMEM((SUBLANES, ct), F32)], name=name,
        compiler_params=_cp(("parallel", "arbitrary"), 48),
    )(gate, xpre, xpre, cw, cb, wa, ba, wx, bx, lam)


def _lru_bwd(gate, xpre, hs, dy, cw, cb, wa, ba, wx, bx, lam, *, name):
    t, c = gate.shape
    tt, ct = min(LRU_TT, t), LRU_CT
    nt = t // tt
    cur, prev, vec, blk = _lru_specs(t, tt, ct, lambda n: nt - 1 - n)

    def body(gate_ref, x_ref, p_ref, hs_ref, phs_ref, dy_ref, cw_ref, cb_ref, wa_ref, ba_ref, wx_ref, bx_ref, lam_ref,
             dgate_ref, dx_ref, dcw_ref, dcb_ref, dwa_ref, dba_ref, dwx_ref, dbx_ref, dlam_ref, carry, head):
        n = pl.program_id(1)
        tile0 = n == nt - 1

        @pl.when(n == 0)
        def _():
            for ref in (carry, head, dcw_ref, dcb_ref, dwa_ref, dba_ref, dwx_ref, dbx_ref, dlam_ref):
                ref[...] = jnp.zeros_like(ref)

        xp, cwv, lam = x_ref[...], cw_ref[...], lam_ref[...]
        p8 = jnp.where(tile0, 0.0, p_ref[...])
        xr, shifts = _conv4(xp, p8, cwv, cb_ref[...])
        r, i, sp, a, mult = _lru_gates(xr, wa_ref, wx_ref, ba_ref[...], bx_ref[...], lam)
        gate, hsv, dyv = gate_ref[...], hs_ref[...], dy_ref[...]
        dgate_ref[...] = (dyv * hsv * _gelu_grad(gate)).astype(BF16)
        acc_b = dyv * _gelu(gate) + jnp.where(_rows(a.shape) == tt - 1, carry[0:1], 0.0)
        acc_a = _up_fill(a, 1, 0.0)
        d = 1
        while d < tt:
            acc_b = acc_b + acc_a * _up_fill(acc_b, d, 0.0)
            acc_a = acc_a * _up_fill(acc_a, d, 0.0)
            d *= 2
        gsum = acc_b
        carry[...] = jnp.broadcast_to(a[0:1] * gsum[0:1], carry.shape)
        hprev = _shift_down(hsv, jnp.where(tile0, 0.0, phs_ref[...]), 1)
        da = gsum * hprev
        dmult = gsum * i * xr
        di = gsum * mult * xr
        dxr = gsum * mult * i
        dla = da * a - dmult * (a * a) / mult
        dr = dla * (-LRU_C * sp)
        dlam_ref[...] += jnp.sum(dla * (-LRU_C * r), axis=0, keepdims=True) * (-_sigmoid(-lam))
        dpa = dr * r * (1.0 - r)
        dpx = di * i * (1.0 - i)
        dba_ref[...] += jnp.sum(dpa, axis=0, keepdims=True)
        dbx_ref[...] += jnp.sum(dpx, axis=0, keepdims=True)
        dxr = dxr + _blockdiag(dpa, wa_ref, NT) + _blockdiag(dpx, wx_ref, NT)
        for b in range(ct // LANES):
            sl = slice(LANES * b, LANES * (b + 1))
            dwa_ref[b] += _bdot(xr[:, sl], dpa[:, sl], TN)
            dwx_ref[b] += _bdot(xr[:, sl], dpx[:, sl], TN)
        dx, dws = _conv4_bwd(dxr, head[...], xp, shifts, cwv)
        dx_ref[...] = dx.astype(BF16)
        for k in range(4):
            dcw_ref[k:k + 1, :] += dws[k]
        dcb_ref[...] += jnp.sum(dxr, axis=0, keepdims=True)
        head[...] = dxr[:SUBLANES]

    return pl.pallas_call(
        body, grid=(c // ct, nt),
        in_specs=[cur, cur, prev, cur, prev, cur, vec(4), vec(1), blk, vec(1), blk, vec(1), vec(1)],
        out_specs=[cur, cur, vec(4), vec(1), blk, vec(1), blk, vec(1), vec(1)],
        out_shape=[jax.ShapeDtypeStruct((t, c), BF16)] * 2 + [jax.ShapeDtypeStruct((4, c), F32), jax.ShapeDtypeStruct((1, c), F32),
                   jax.ShapeDtypeStruct(wa.shape, F32), jax.ShapeDtypeStruct((1, c), F32),
                   jax.ShapeDtypeStruct(wx.shape, F32), jax.ShapeDtypeStruct((1, c), F32), jax.ShapeDtypeStruct((1, c), F32)],
        scratch_shapes=[pltpu.VMEM((SUBLANES, ct), F32)] * 2, name=name,
        compiler_params=_cp(("parallel", "arbitrary"), 48),
    )(gate, xpre, xpre, hs, hs, dy, cw, cb, wa, ba, wx, bx, lam)


RET_W = HEADS * HEAD_DIM
HALF = HEAD_DIM // 2


def _ret_tables(t):
    c = RET_CHUNK
    inv_freq = ROPE_BASE ** (-jnp.arange(HALF, dtype=F32) / HALF)
    ang = jnp.arange(t, dtype=jnp.int32).astype(F32)[:, None] * inv_freq[None, :]
    cos, sin = jnp.cos(ang), jnp.sin(ang)
    cosf = jnp.concatenate([cos, cos], axis=1)
    sinf = jnp.concatenate([-sin, sin], axis=1)
    log_gamma = jnp.log1p(-jnp.exp2(-5.0 - jnp.arange(HEADS, dtype=F32)))
    idx = jnp.arange(c, dtype=F32)
    rel = idx[:, None] - idx[None, :]
    causal = rel >= 0
    dmask = jnp.where(causal, jnp.exp(log_gamma[:, None, None] * jnp.where(causal, rel, 0.0)), 0.0)
    ktail = jnp.exp(log_gamma[:, None] * (c - 1 - idx))
    qdec = jnp.exp(log_gamma[:, None] * (idx + 1.0))
    rowtab = jnp.broadcast_to(jnp.stack([ktail, qdec], axis=1)[..., None], (HEADS, 2, c, HEAD_DIM))
    cdec = jnp.broadcast_to(jnp.exp(log_gamma * c)[:, None, None], (HEADS, SUBLANES, HEAD_DIM))
    return cosf, sinf, dmask, rowtab, cdec


def _rotary(x, cosf, sinf):
    return x * cosf + pltpu.roll(x, HALF, 1) * sinf


def _rotary_t(dx, cosf, sinf):
    return dx * cosf + pltpu.roll(dx * sinf, HALF, 1)


def _ret_specs(c, order):
    full = lambda shape: pl.BlockSpec(shape, lambda n: (0,) * len(shape))
    return dict(
        proj=pl.BlockSpec((c, 4 * RET_W), lambda n: (order(n), 0)),
        rot=pl.BlockSpec((c, HEAD_DIM), lambda n: (order(n), 0)),
        dmask=full((HEADS, c, c)), rowtab=full((HEADS, 2, c, HEAD_DIM)), cdec=full((HEADS, SUBLANES, HEAD_DIM)),
        state=pl.BlockSpec((1, HEADS, HEAD_DIM, HEAD_DIM), lambda n: (order(n), 0, 0, 0)),
        half=pl.BlockSpec((c, RET_W), lambda n: (order(n), 0)),
    )


def _ret_head(p_ref, h, cosf, sinf):
    sl = lambda j: slice(j * RET_W + h * HEAD_DIM, j * RET_W + (h + 1) * HEAD_DIM)
    q, k, v, g = p_ref[:, sl(0)], p_ref[:, sl(1)], p_ref[:, sl(2)], p_ref[:, sl(3)]
    return _rotary(q, cosf, sinf), _rotary(k, cosf, sinf) * QSCALE, v, g


def _ret_fwd(proj, tables, *, name):
    t = proj.shape[0]
    c = RET_CHUNK
    nc = t // c
    sp = _ret_specs(c, lambda n: n)

    def body(p_ref, cos_ref, sin_ref, dm_ref, rt_ref, cd_ref, y_ref, s_ref, state):
        @pl.when(pl.program_id(0) == 0)
        def _():
            state[...] = jnp.zeros_like(state)

        cosf, sinf = cos_ref[...], sin_ref[...]
        for h in range(HEADS):
            qr, kr, v, g = _ret_head(p_ref, h, cosf, sinf)
            s0 = state[h]
            s_ref[0, h] = s0
            scores = _bdot(qr, kr, NT) * dm_ref[h]
            o = _bdot(scores, v) + _bdot(qr * rt_ref[h, 1], s0)
            state[h] = s0 * cd_ref[h][0:1] + _bdot(kr * rt_ref[h, 0], v, TN)
            rinv = lax.rsqrt(jnp.mean(o * o, axis=-1, keepdims=True) + EPS)
            y_ref[:, h * HEAD_DIM:(h + 1) * HEAD_DIM] = (o * rinv * (g * _sigmoid(g))).astype(BF16)

    return pl.pallas_call(
        body, grid=(nc,),
        in_specs=[sp["proj"], sp["rot"], sp["rot"], sp["dmask"], sp["rowtab"], sp["cdec"]],
        out_specs=[sp["half"], sp["state"]],
        out_shape=[jax.ShapeDtypeStruct((t, 2 * RET_W), BF16), jax.ShapeDtypeStruct((nc, HEADS, HEAD_DIM, HEAD_DIM), F32)],
        scratch_shapes=[pltpu.VMEM((HEADS, HEAD_DIM, HEAD_DIM), F32)], name=name,
        compiler_params=_cp(("arbitrary",), 48),
    )(proj, *tables)


def _ret_bwd(proj, tables, states, dy, comm, *, name):
    t = proj.shape[0]
    c = RET_CHUNK
    nc = t // c
    sp = _ret_specs(c, lambda n: nc - 1 - n)
    hbm = pl.BlockSpec(memory_space=pl.ANY)

    def body(p_ref, cos_ref, sin_ref, dm_ref, rt_ref, cd_ref, s_ref, dy_ref, *rest):
        comm_in, (dp_ref,), comm_out, (dstate,), comm_sems = comm.split(rest, n_out=1, n_scratch=1)

        @pl.when(pl.program_id(0) == 0)
        def _():
            dstate[...] = jnp.zeros_like(dstate)
            comm.start(comm_in, comm_out, comm_sems)

        cosf, sinf = cos_ref[...], sin_ref[...]
        for h in range(HEADS):
            qr, kr, v, g = _ret_head(p_ref, h, cosf, sinf)
            s0, dm, ktl, qdc = s_ref[0, h], dm_ref[h], rt_ref[h, 0], rt_ref[h, 1]
            scores = _bdot(qr, kr, NT) * dm
            qd, kt = qr * qdc, kr * ktl
            o = _bdot(scores, v) + _bdot(qd, s0)
            rinv = lax.rsqrt(jnp.mean(o * o, axis=-1, keepdims=True) + EPS)
            oh = o * rinv
            sg = _sigmoid(g)
            dyh = dy_ref[:, h * HEAD_DIM:(h + 1) * HEAD_DIM]
            dg = dyh * oh * sg * (1.0 + g * (1.0 - sg))
            dyo = dyh * (g * sg)
            do = rinv * (dyo - oh * jnp.mean(dyo * oh, axis=-1, keepdims=True))
            ds1 = dstate[h]
            dsc = _bdot(do, v, NT) * dm
            dv = _bdot(scores, do, TN) + _bdot(kt, ds1)
            dqr = _bdot(dsc, kr) + _bdot(do, s0, NT) * qdc
            dkr = (_bdot(dsc, qr, TN) + _bdot(v, ds1, NT) * ktl) * QSCALE
            dstate[h] = ds1 * cd_ref[h][0:1] + _bdot(qd, do, TN)
            pieces = (_rotary_t(dqr, cosf, sinf), _rotary_t(dkr, cosf, sinf), dv, dg)
            for j, piece in enumerate(pieces):
                dp_ref[:, j * RET_W + h * HEAD_DIM:j * RET_W + (h + 1) * HEAD_DIM] = piece.astype(BF16)

        @pl.when(pl.program_id(0) == nc - 1)
        def _():
            comm.finish(comm_in, comm_out, comm_sems)

    outs = pl.pallas_call(
        body, grid=(nc,),
        in_specs=[sp["proj"], sp["rot"], sp["rot"], sp["dmask"], sp["rowtab"], sp["cdec"], sp["state"], sp["half"]]
        + [hbm] * len(comm.arrays),
        out_specs=[sp["proj"]] + [hbm] * len(comm.out_shapes),
        out_shape=[jax.ShapeDtypeStruct((t, 8 * RET_W), BF16)] + comm.out_shapes,
        scratch_shapes=[pltpu.VMEM((HEADS, HEAD_DIM, HEAD_DIM), F32)] + comm.scratch, name=name,
        compiler_params=_cp(("arbitrary",), 48),
    )(proj, *tables, states, dy, *comm.arrays)
    return outs[0], outs[1:]


GDN_W = HEADS * HEAD_DIM
GDN_CONV = 3 * GDN_W
NEUMANN_STEPS = 5


def _gdn_gates(ps, al, dt):
    return _sigmoid(ps), -jnp.exp(al) * _softplus(ps + dt)


def _cumsum_rows(x):
    d = 1
    while d < x.shape[0]:
        x = x + _down_fill(x, d, 0.0)
        d *= 2
    return x


def _rev_cumsum_rows(x):
    d = 1
    while d < x.shape[0]:
        x = x + _up_fill(x, d, 0.0)
        d *= 2
    return x


class _Chunk:
    pass


def _gdn_chunk(qc, kc, v, beta, g, s0):
    c = GDN_CHUNK
    z = _Chunk()
    z.rq = lax.rsqrt(jnp.sum(qc * qc, axis=-1, keepdims=True) + EPS)
    z.rk = lax.rsqrt(jnp.sum(kc * kc, axis=-1, keepdims=True) + EPS)
    z.qn, z.k = qc * z.rq, kc * z.rk
    z.q = z.qn * QSCALE
    z.v, z.beta = v, beta
    gc = _cumsum_rows(jnp.broadcast_to(g, (c, LANES)))
    ri, ci = _rows((c, c)), _cols((c, c))
    z.tril, z.strict = ri >= ci, ri > ci
    diff = gc[:, :c] - gc.T[:c, :]
    z.decay = jnp.where(z.tril, jnp.exp(jnp.where(z.tril, diff, 0.0)), 0.0)
    z.eg = jnp.exp(gc)
    glast = gc[c - 1:c, :]
    z.egl = jnp.exp(glast - gc)
    z.cd = jnp.exp(glast)
    z.kb = z.k * beta
    z.m = _bdot(z.kb, z.k, NT)
    lmat = jnp.where(z.strict, z.m * z.decay, 0.0)
    neg = -lmat
    inv = (ri == ci).astype(F32) + neg
    pw = neg
    for _ in range(NEUMANN_STEPS):
        pw = _dot3(pw, pw)
        inv = inv + _dot3(inv, pw)
    z.inv = inv
    z.vb, z.kbg = v * beta, z.kb * z.eg
    z.u = _dot3(inv, z.vb)
    z.w = _dot3(inv, z.kbg)
    z.qk = _bdot(z.q, z.k, NT)
    z.attn = jnp.where(z.tril, z.qk * z.decay, 0.0)
    z.qd, z.kt = z.q * z.eg, z.k * z.egl
    z.vnew = z.u - _bdot(z.w, s0)
    z.o = _bdot(z.qd, s0) + _bdot(z.attn, z.vnew)
    z.s1 = s0 * z.cd + _bdot(z.kt, z.vnew, TN)
    return z


def _gdn_chunk_bwd(z, s0, do, ds1):
    c = GDN_CHUNK
    dvnew = _bdot(z.attn, do, TN) + _bdot(z.kt, ds1)
    dqd = _bdot(do, s0, NT)
    dattn = jnp.where(z.tril, _bdot(do, z.vnew, NT), 0.0)
    ds0 = _bdot(z.qd, do, TN) + ds1 * z.cd - _bdot(z.w, dvnew, TN)
    dcd = jnp.sum(jnp.sum(s0 * ds1, axis=1, keepdims=True), axis=0, keepdims=True)
    dkt = _bdot(z.vnew, ds1, NT)
    dw = -_bdot(dvnew, s0, NT)
    dvb = _dot3(z.inv, dvnew, TN)
    dkbg = _dot3(z.inv, dw, TN)
    dl = jnp.where(z.strict, -(_bdot(dvb, z.u, NT) + _bdot(dkbg, z.w, NT)), 0.0)
    dml = dl * z.decay
    dqk = dattn * z.decay
    ddecay = (dl * z.m + dattn * z.qk) * z.decay
    dq = _bdot(dqk, z.k) + dqd * z.eg
    dkb = _bdot(dml, z.k) + dkbg * z.eg
    dk = _bdot(dqk, z.q, TN) + _bdot(dml, z.kb, TN) + dkt * z.egl + dkb * z.beta
    dbeta = jnp.sum(dkb * z.k, axis=-1, keepdims=True) + jnp.sum(dvb * z.v, axis=-1, keepdims=True)
    dv = dvb * z.beta
    colsum = _dot3(ddecay, jnp.ones((c, LANES), F32), TN)
    e = jnp.sum(dkt * z.kt, axis=-1, keepdims=True)
    dgc = (jnp.sum(ddecay, axis=-1, keepdims=True) - colsum
           + jnp.sum(dkbg * z.kbg, axis=-1, keepdims=True) + jnp.sum(dqd * z.qd, axis=-1, keepdims=True) - e)
    dglast = jnp.sum(e, axis=0, keepdims=True) + dcd * z.cd
    dgc = dgc + jnp.where(_rows((c, LANES)) == c - 1, dglast, 0.0)
    dg = _rev_cumsum_rows(dgc)[:, 0:1]
    dqn = dq * QSCALE
    dqc = z.rq * (dqn - z.qn * jnp.sum(dqn * z.qn, axis=-1, keepdims=True))
    dkc = z.rk * (dk - z.k * jnp.sum(dk * z.k, axis=-1, keepdims=True))
    return dqc, dkc, dv, dbeta, dg, ds0


GDN_SUB = 1


def _gdn_specs(c, order):
    full = lambda shape: pl.BlockSpec(shape, lambda n: (0,) * len(shape))
    return dict(
        proj=pl.BlockSpec((c, 4 * GDN_W), lambda n: (order(n), 1)),
        prev=pl.BlockSpec((SUBLANES, 4 * GDN_W), lambda n: (_prev8(order(n), c), 1)),
        small=pl.BlockSpec((c, LANES), lambda n: (order(n), 0)),
        convw=full((4, GDN_CONV)), vec=full((1, LANES)),
        state=pl.BlockSpec((GDN_SUB, HEADS, HEAD_DIM, HEAD_DIM), lambda n: (order(n), 0, 0, 0)),
        half=pl.BlockSpec((c, GDN_W), lambda n: (order(n), 1)),
        any=pl.BlockSpec(memory_space=pl.ANY),
    )


def _gdn_fwd(proj, psmall, conv_w, al, dt, gain, y_in, comm, *, name):
    t = proj.shape[0]
    c = GDN_CHUNK
    nc, ns = t // c, t // (c * GDN_SUB)
    sp = _gdn_specs(c * GDN_SUB, lambda n: n)

    def body(p_ref, prev_ref, ps_ref, cw_ref, al_ref, dt_ref, gain_ref, yin_ref, *rest):
        comm_in, (y_ref, s_ref), comm_out, (state,), comm_sems = comm.split(rest, n_out=2, n_scratch=1)
        n = pl.program_id(0)

        @pl.when(n == 0)
        def _():
            state[...] = jnp.zeros_like(state)
            comm.start(comm_in, comm_out, comm_sems)

        p8 = jnp.where(n == 0, 0.0, prev_ref[:, :GDN_CONV])
        pre, _ = _conv4(p_ref[:, :GDN_CONV], p8, cw_ref[...])
        act = pre * _sigmoid(pre)
        beta_all, g_all = _gdn_gates(ps_ref[...], al_ref[...], dt_ref[...])
        gd_all, gain = p_ref[:, GDN_CONV:], gain_ref[...]
        swish = gd_all * _sigmoid(gd_all)
        cur = [state[h] for h in range(HEADS)]
        starts, ys = [], []
        for sub in range(GDN_SUB):
            rows = slice(sub * c, (sub + 1) * c)
            starts.append(list(cur))
            pieces = []
            for h in range(HEADS):
                sl = lambda j: slice(j * GDN_W + h * HEAD_DIM, j * GDN_W + (h + 1) * HEAD_DIM)
                z = _gdn_chunk(act[rows, sl(0)], act[rows, sl(1)], act[rows, sl(2)], beta_all[rows, h:h + 1],
                               g_all[rows, HEADS + h:HEADS + h + 1], cur[h])
                cur[h] = z.s1
                rinv = lax.rsqrt(jnp.mean(z.o * z.o, axis=-1, keepdims=True) + EPS)
                pieces.append(z.o * rinv * gain * swish[rows, sl(0)])
            ys.append(jnp.concatenate(pieces, axis=1))
        y_ref[...] = jnp.concatenate(ys, axis=0).astype(BF16)
        for sub in range(GDN_SUB):
            for h in range(HEADS):
                s_ref[sub, h] = starts[sub][h]
        for h in range(HEADS):
            state[h] = cur[h]

        @pl.when(n == ns - 1)
        def _():
            comm.finish(comm_in, comm_out, comm_sems)

    outs = pl.pallas_call(
        body, grid=(ns,),
        in_specs=[sp["proj"], sp["prev"], sp["small"], sp["convw"], sp["vec"], sp["vec"], sp["vec"], sp["any"]]
        + [sp["any"]] * len(comm.arrays),
        out_specs=[sp["half"], sp["state"]] + [sp["any"]] * len(comm.out_shapes),
        out_shape=[jax.ShapeDtypeStruct((t, 2 * GDN_W), BF16), jax.ShapeDtypeStruct((nc, HEADS, HEAD_DIM, HEAD_DIM), F32)]
        + comm.out_shapes,
        scratch_shapes=[pltpu.VMEM((HEADS, HEAD_DIM, HEAD_DIM), F32)] + comm.scratch, name=name,
        input_output_aliases={7: 0}, compiler_params=_cp(("arbitrary",), 48),
    )(proj, proj, psmall, conv_w, al, dt, gain, y_in, *comm.arrays)
    return outs[0], outs[1], outs[2:]


def _gdn_bwd(proj, psmall, conv_w, al, dt, gain, states, dy, dproj_in, comm, *, name):
    t = proj.shape[0]
    c = GDN_CHUNK
    nc, ns = t // c, t // (c * GDN_SUB)
    sp = _gdn_specs(c * GDN_SUB, lambda n: ns - 1 - n)

    def body(p_ref, prev_ref, ps_ref, cw_ref, al_ref, dt_ref, gain_ref, s_ref, dy_ref, dpin_ref, *rest):
        comm_in, outs, comm_out, (dstate, head), comm_sems = comm.split(rest, n_out=6, n_scratch=2)
        dp_ref, dps_ref, dcw_ref, dal_ref, ddt_ref, dgain_ref = outs
        n = pl.program_id(0)
        first_rows = n == ns - 1

        @pl.when(n == 0)
        def _():
            for ref in (dstate, head, dcw_ref, dal_ref, ddt_ref, dgain_ref):
                ref[...] = jnp.zeros_like(ref)
            comm.start(comm_in, comm_out, comm_sems)

        x, cwv = p_ref[:, :GDN_CONV], cw_ref[...]
        p8 = jnp.where(first_rows, 0.0, prev_ref[:, :GDN_CONV])
        pre, shifts = _conv4(x, p8, cwv)
        sg_pre = _sigmoid(pre)
        act = pre * sg_pre
        ps, alv, dtv, gain = ps_ref[...], al_ref[...], dt_ref[...], gain_ref[...]
        beta_all, g_all = _gdn_gates(ps, alv, dtv)
        lane = _cols((c, LANES))
        dgain = jnp.zeros((1, LANES), F32)
        dcur = [dstate[h] for h in range(HEADS)]
        dact_rows, dbeta_rows, dg_rows = [None] * GDN_SUB, [None] * GDN_SUB, [None] * GDN_SUB
        for sub in reversed(range(GDN_SUB)):
            rows = slice(sub * c, (sub + 1) * c)
            dbeta_all = jnp.zeros((c, LANES), F32)
            dg_all = jnp.zeros((c, LANES), F32)
            dact = [None] * (3 * HEADS)
            for h in range(HEADS):
                sl = lambda j: slice(j * GDN_W + h * HEAD_DIM, j * GDN_W + (h + 1) * HEAD_DIM)
                s0 = s_ref[sub, h]
                z = _gdn_chunk(act[rows, sl(0)], act[rows, sl(1)], act[rows, sl(2)], beta_all[rows, h:h + 1],
                               g_all[rows, HEADS + h:HEADS + h + 1], s0)
                rinv = lax.rsqrt(jnp.mean(z.o * z.o, axis=-1, keepdims=True) + EPS)
                oh = z.o * rinv
                gd = p_ref[rows, sl(3)]
                sgd = _sigmoid(gd)
                dyh = dy_ref[rows, sl(0)]
                dgain = dgain + jnp.sum(dyh * oh * (gd * sgd), axis=0, keepdims=True)
                dp_ref[rows, sl(3)] = (dyh * oh * gain * sgd * (1.0 + gd * (1.0 - sgd))).astype(BF16)
                dyo = dyh * gain * (gd * sgd)
                do = rinv * (dyo - oh * jnp.mean(dyo * oh, axis=-1, keepdims=True))
                dqc, dkc, dv, dbeta, dg, dcur[h] = _gdn_chunk_bwd(z, s0, do, dcur[h])
                dact[h], dact[HEADS + h], dact[2 * HEADS + h] = dqc, dkc, dv
                dbeta_all = dbeta_all + jnp.where(lane == h, dbeta, 0.0)
                dg_all = dg_all + jnp.where(lane == HEADS + h, dg, 0.0)
            dact_rows[sub], dbeta_rows[sub], dg_rows[sub] = jnp.concatenate(dact, axis=1), dbeta_all, dg_all
        for h in range(HEADS):
            dstate[h] = dcur[h]
        dbeta_all, dg_all = jnp.concatenate(dbeta_rows, axis=0), jnp.concatenate(dg_rows, axis=0)
        dpre = jnp.concatenate(dact_rows, axis=0) * sg_pre * (1.0 + pre * (1.0 - sg_pre))
        dx, dws = _conv4_bwd(dpre, head[...], x, shifts, cwv)
        dp_ref[:, :GDN_CONV] = dx.astype(BF16)
        for k in range(4):
            dcw_ref[k:k + 1, :] += dws[k]
        head[...] = dpre[:SUBLANES]
        dsp = dg_all * (-jnp.exp(alv)) * _sigmoid(ps + dtv)
        dps_ref[...] = (dbeta_all * beta_all * (1.0 - beta_all) + dsp).astype(BF16)
        ddt_ref[...] += jnp.sum(dsp, axis=0, keepdims=True)
        dal_ref[...] += jnp.sum(dg_all * g_all, axis=0, keepdims=True)
        dgain_ref[...] += dgain

        @pl.when(n == ns - 1)
        def _():
            comm.finish(comm_in, comm_out, comm_sems)

    vec_f32 = jax.ShapeDtypeStruct((1, LANES), F32)
    outs = pl.pallas_call(
        body, grid=(ns,),
        in_specs=[sp["proj"], sp["prev"], sp["small"], sp["convw"], sp["vec"], sp["vec"], sp["vec"], sp["state"], sp["half"], sp["any"]]
        + [sp["any"]] * len(comm.arrays),
        out_specs=[sp["proj"], sp["small"], sp["convw"], sp["vec"], sp["vec"], sp["vec"]] + [sp["any"]] * len(comm.out_shapes),
        out_shape=[jax.ShapeDtypeStruct((t, 8 * GDN_W), BF16), jax.ShapeDtypeStruct((t, LANES), BF16),
                   jax.ShapeDtypeStruct((4, GDN_CONV), F32), vec_f32, vec_f32, vec_f32] + comm.out_shapes,
        scratch_shapes=[pltpu.VMEM((HEADS, HEAD_DIM, HEAD_DIM), F32), pltpu.VMEM((SUBLANES, GDN_CONV), F32)] + comm.scratch,
        name=name, input_output_aliases={9: 0}, compiler_params=_cp(("arbitrary",), 48),
    )(proj, proj, psmall, conv_w, al, dt, gain, states, dy, dproj_in, *comm.arrays)
    return outs[:6], outs[6:]


def _here():
    x, y, c = lax.axis_index("x"), lax.axis_index("y"), lax.axis_index("c")
    return x, y, c, [(1 - x, y), (x, 1 - y), (1 - x, 1 - y)]


def _rdma(src, dst, send, recv, k, dev):
    return pltpu.make_async_remote_copy(src_ref=src, dst_ref=dst, send_sem=send.at[k], recv_sem=recv.at[k],
                                        device_id=dev, device_id_type=MESH)


def _dma_sems(n):
    return [pltpu.SemaphoreType.DMA((n,)), pltpu.SemaphoreType.DMA((n,)), pltpu.SemaphoreType.DMA((1,))]


COPY_PIECES = 4
COPY_PIECE_ALIGN = 16


def _row_parts(rows):
    n = COPY_PIECES if rows % (COPY_PIECES * COPY_PIECE_ALIGN) == 0 and rows >= 1024 else 1
    return [pl.ds(q * (rows // n), rows // n) for q in range(n)]


class _AllGather:
    def __init__(self, array):
        self.arrays = [array]
        self.out_shapes = [jax.ShapeDtypeStruct((N_DEV,) + array.shape, array.dtype)]
        self.parts = _row_parts(array.shape[0])
        self.scratch = _dma_sems(7 * len(self.parts))

    def start(self, ins, outs, sems):
        (src,), (out,), (send, recv, loc) = ins, outs, sems
        x, y, c, chips = _here()
        me, n = 4 * x + 2 * y + c, len(self.parts)
        pltpu.make_async_copy(src, out.at[me], loc.at[0]).start()
        for q, part in enumerate(self.parts):
            _rdma(src.at[part], out.at[me, part], send, recv, q, (x, y, 1 - c)).start()
            for j, (cx, cy) in enumerate(chips):
                _rdma(src.at[part], out.at[me, part], send, recv, (1 + j) * n + q, (cx, cy, c)).start()

    def finish(self, ins, outs, sems):
        (src,), (out,), (send, recv, loc) = ins, outs, sems
        x, y, c, chips = _here()
        sibling, me, n = (x, y, 1 - c), 4 * x + 2 * y + c, len(self.parts)
        piece = lambda k, q: _rdma(src.at[self.parts[q]], out.at[me, self.parts[q]], send, recv, k * n + q, sibling)
        for j, (cx, cy) in enumerate(chips):
            for q, part in enumerate(self.parts):
                got = out.at[4 * cx + 2 * cy + c, part]
                piece(1 + j, q).wait_recv()
                _rdma(got, got, send, recv, (4 + j) * n + q, sibling).start()
        for k in (0, 4, 5, 6):
            for q in range(n):
                piece(k, q).wait_recv()
        for k in range(7):
            for q in range(n):
                piece(k, q).wait_send()
        pltpu.make_async_copy(src, out.at[me], loc.at[0]).wait()


class _ChipExchange:
    def __init__(self, array):
        self.arrays = [array]
        self.out_shapes = [jax.ShapeDtypeStruct(array.shape, array.dtype)]
        self.parts = _row_parts(array.shape[1])
        self.scratch = _dma_sems(3 * len(self.parts))

    def _copies(self, ins, outs, sems):
        (src,), (out,), (send, recv, loc) = ins, outs, sems
        x, y, c, chips = _here()
        here, n = 2 * x + y, len(self.parts)
        local = pltpu.make_async_copy(src.at[here], out.at[here], loc.at[0])
        return local, [_rdma(src.at[2 * cx + cy, part], out.at[here, part], send, recv, j * n + q, (cx, cy, c))
                       for j, (cx, cy) in enumerate(chips) for q, part in enumerate(self.parts)]

    def start(self, ins, outs, sems):
        local, remote = self._copies(ins, outs, sems)
        local.start()
        for cp in remote:
            cp.start()

    def finish(self, ins, outs, sems):
        local, remote = self._copies(ins, outs, sems)
        for cp in remote:
            cp.wait()
        local.wait()


class _PairSwap:
    def __init__(self, array):
        self.arrays = [array]
        self.out_shapes = [jax.ShapeDtypeStruct(array.shape[1:], array.dtype)]
        self.parts = _row_parts(array.shape[2])
        self.scratch = _dma_sems(4 * len(self.parts))

    def _copies(self, ins, outs, sems):
        (src,), (theirs,), (send, recv, _) = ins, outs, sems
        x, y, c, _ = _here()
        return [_rdma(src.at[1 - c, p, part], theirs.at[p, part], send, recv, p * len(self.parts) + q, (x, y, 1 - c))
                for p in range(4) for q, part in enumerate(self.parts)]

    def start(self, ins, outs, sems):
        for cp in self._copies(ins, outs, sems):
            cp.start()

    def finish(self, ins, outs, sems):
        for cp in self._copies(ins, outs, sems):
            cp.wait()


class _Comm:
    def __init__(self, ops):
        self.ops = ops
        self.arrays = [a for op in ops for a in op.arrays]
        self.out_shapes = [s for op in ops for s in op.out_shapes]
        self.scratch = [s for op in ops for s in op.scratch]

    def split(self, rest, n_out, n_scratch):
        cuts = np.cumsum([0, len(self.arrays), n_out, len(self.out_shapes), n_scratch, len(self.scratch)])
        assert cuts[-1] == len(rest)
        return tuple(rest[a:b] for a, b in zip(cuts[:-1], cuts[1:]))

    def _each(self, method, ins, outs, sems):
        i = o = s = 0
        for op in self.ops:
            ni, no, ns = len(op.arrays), len(op.out_shapes), len(op.scratch)
            getattr(op, method)(ins[i:i + ni], outs[o:o + no], sems[s:s + ns])
            i, o, s = i + ni, o + no, s + ns

    def start(self, ins, outs, sems):
        self._each("start", ins, outs, sems)

    def finish(self, ins, outs, sems):
        self._each("finish", ins, outs, sems)

    def run(self, name):
        def body(*refs):
            ins, _, outs, _, sems = self.split(refs, 0, 0)
            self.start(ins, outs, sems)
            self.finish(ins, outs, sems)

        hbm = pl.BlockSpec(memory_space=pl.ANY)
        return pl.pallas_call(body, in_specs=[hbm] * len(self.arrays), out_specs=[hbm] * len(self.out_shapes),
                              out_shape=self.out_shapes, scratch_shapes=self.scratch, name=name)(*self.arrays)


def _sum_slots(x, *, name, tr=None):
    n, r, l = x.shape
    tr = r if tr is None else tr

    def body(x_ref, o_ref):
        acc = x_ref[0].astype(F32)
        for s in range(1, n):
            acc = acc + x_ref[s].astype(F32)
        o_ref[...] = acc

    return pl.pallas_call(
        body, grid=(r // tr,), in_specs=[pl.BlockSpec((n, tr, l), lambda i: (0, i, 0))],
        out_specs=pl.BlockSpec((tr, l), lambda i: (i, 0)), out_shape=jax.ShapeDtypeStruct((r, l), F32),
        name=name, compiler_params=_cp(("parallel",), 48),
    )(x)


def _pair_add(both, theirs, *, name, tr):
    _, n, r, l = both.shape

    def body(a_ref, b_ref, o_ref):
        mine = jnp.where(lax.axis_index("c") == 0, a_ref[0], a_ref[1])
        o_ref[...] = (mine.astype(F32) + b_ref[...].astype(F32)).astype(BF16)

    spec = pl.BlockSpec((n, tr, l), lambda i: (0, i, 0))
    return pl.pallas_call(body, grid=(r // tr,), in_specs=[pl.BlockSpec((2, n, tr, l), lambda i: (0, 0, i, 0)), spec], out_specs=spec,
                          out_shape=jax.ShapeDtypeStruct(theirs.shape, BF16), name=name,
                          compiler_params=_cp(("parallel",), 48))(both, theirs)


ADAM_TILE_ELEMS = 512 * 1024


def _adam(w, g, m, v, *, name):
    shape = w.shape
    cols = shape[-1]
    rows = math.prod(shape[:-1]) if len(shape) > 1 else 1
    tr = rows
    if rows * cols > ADAM_TILE_ELEMS:
        tr = max(d for d in range(SUBLANES, ADAM_TILE_ELEMS // cols + 1, SUBLANES) if rows % d == 0)
    c1, c2 = 1.0 - ADAM_B1 ** ADAM_STEP, 1.0 - ADAM_B2 ** ADAM_STEP

    def body(w_ref, g_ref, m_ref, v_ref, d_ref, m2_ref, v2_ref):
        gv = g_ref[...]
        m2 = ADAM_B1 * m_ref[...] + (1.0 - ADAM_B1) * gv
        v2 = ADAM_B2 * v_ref[...] + (1.0 - ADAM_B2) * (gv * gv)
        d_ref[...] = -ADAM_LR * ((m2 / c1) / (jnp.sqrt(v2 / c2) + ADAM_EPS) + ADAM_WD * w_ref[...])
        m2_ref[...] = m2
        v2_ref[...] = v2

    spec = pl.BlockSpec((tr, cols), lambda i: (i, 0))
    outs = pl.pallas_call(
        body, grid=(rows // tr,), in_specs=[spec] * 4, out_specs=[spec] * 3,
        out_shape=[jax.ShapeDtypeStruct((rows, cols), F32)] * 3, name=name, compiler_params=_cp(("parallel",), 48),
    )(*(a.reshape(rows, cols) for a in (w, g, m, v)))
    return tuple(o.reshape(shape) for o in outs)


WEIGHTS = ['norm_mix', 'norm_ffn', 'ret_gdn_w_in', 'gdn_conv_w', 'gdn_a_log', 'gdn_dt_bias', 'gdn_out_gain', 'ret_gdn_w_out',
           'lru_w_in', 'lru_conv_w', 'lru_conv_b', 'lru_w_a', 'lru_b_a', 'lru_w_x', 'lru_b_x', 'lru_lambda', 'lru_w_out',
           'ffn_w_up', 'ffn_conv_w', 'ffn_conv_b', 'ffn_w_down', 'norm_final']
BIG = {'ret_gdn_w_in': ((1, 1024, 513), 2), 'ret_gdn_w_out': ((1, 128, 1024), 1), 'lru_w_in': ((1, 1024, 256), 2),
       'lru_w_out': ((1, 128, 1024), 1), 'ffn_w_up': ((2, 1024, 704), 2), 'ffn_w_down': ((2, 352, 1024), 1)}
SMALL = {'gdn_conv_w': ((1, 4, 192), 2), 'lru_conv_w': ((1, 4, 128), 2), 'lru_conv_b': ((1, 128), 1), 'lru_b_a': ((1, 128), 1),
         'lru_b_x': ((1, 128), 1), 'lru_lambda': ((1, 128), 1), 'ffn_conv_w': ((2, 3, 704), 2)}
REPLICATED = {'norm_mix': (2, 1024), 'norm_ffn': (2, 1024), 'gdn_a_log': (1, 4), 'gdn_dt_bias': (1, 4), 'gdn_out_gain': (1, 128),
              'lru_w_a': (1, 8, 128, 128), 'lru_w_x': (1, 8, 128, 128), 'ffn_conv_b': (2, 5632), 'norm_final': (1024,)}
FIRST = ['ret_gdn_w_in', 'ret_gdn_w_out']
REST = ['lru_w_in', 'lru_w_out', 'ffn_w_up', 'ffn_w_down']
GROUP_ROWS = {FIRST[0]: 5632, REST[0]: 19968}
GROUP_TILE = {FIRST[0]: 512, REST[0]: 1536}
EARLY = {'lru_conv_w': (1, 4, 1024), 'lru_conv_b': (1, 1024), 'lru_b_a': (1, 1024), 'lru_b_x': (1, 1024), 'lru_lambda': (1, 1024),
         'ffn_conv_w': (2, 3, 5632), 'norm_ffn': (2, 1024), 'norm_mix1': (1, 1024), 'lru_w_a': (1, 8, 128, 128),
         'lru_w_x': (1, 8, 128, 128), 'ffn_conv_b': (2, 5632), 'norm_final': (1024,)}
LATE = {'gdn_conv_w': (1, 4, 1536), 'norm_mix0': (1, 1024), 'gdn_a_log': (1, 4), 'gdn_dt_bias': (1, 4), 'gdn_out_gain': (1, 128)}


def _full_shape(shard, axis):
    return tuple(d * N_DEV if i == axis else d for i, d in enumerate(shard))


def _rows_of(n_elems):
    return -(-n_elems // LANES)


def _to_rows(a, lead=()):
    flat = a.reshape(lead + (-1,))
    pad = _rows_of(flat.shape[-1]) * LANES - flat.shape[-1]
    if pad:
        flat = jnp.pad(flat, [(0, 0)] * len(lead) + [(0, pad)])
    return flat.reshape(lead + (-1, LANES))


def _pack(pieces, total_rows, lead=()):
    buf = jnp.concatenate(pieces, axis=len(lead))
    pad = total_rows - buf.shape[len(lead)]
    return jnp.pad(buf, [(0, 0)] * len(lead) + [(0, pad), (0, 0)]) if pad else buf


def _unpack(buf, shapes, lead=()):
    out, off = [], 0
    for shape in shapes:
        n = math.prod(shape)
        rows = _rows_of(n)
        piece = lax.slice_in_dim(buf, off, off + rows, axis=len(lead)).reshape(lead + (rows * LANES,))
        out.append(lax.slice_in_dim(piece, 0, n, axis=len(lead)).reshape(lead + shape))
        off += rows
    return out


def _join_blocks(g, axis):
    m = jnp.moveaxis(g, 0, axis)
    return m.reshape(m.shape[:axis] + (N_DEV * m.shape[axis + 1],) + m.shape[axis + 2:])


def _split_blocks(full, axis):
    s = full.shape
    return jnp.moveaxis(full.reshape(s[:axis] + (N_DEV, s[axis] // N_DEV) + s[axis + 1:]), axis, 0)


def _small_rows(shapes):
    total = sum(_rows_of(math.prod(s)) for s in shapes)
    return -(-total // SUBLANES) * SUBLANES


FFN_TM = 1024
FFN_TN = 512


def _ffn_forward(h, gain, w_up, cw, cb, w_down, tag):
    t, d = h.shape
    tm, tn, blk, nb = min(FFN_TM, t), FFN_TN, FFN_BLK, FFN_NB
    hn = _norm_fwd(h, gain, name=f"ffn{tag}_norm")
    up = _mmx(hn, w_up, dims=NN, grid=(t // tm, 2 * nb, 1), name=f"ffn{tag}_up", tile=(tm, blk),
              a_spec=pl.BlockSpec((tm, d), lambda i, j, k: (i, 0)),
              b_spec=pl.BlockSpec((None, d, blk), lambda i, j, k: (j, 0, 0)),
              o_spec=pl.BlockSpec((None, None, tm, blk), lambda i, j, k: (j // nb, j % nb, i, 0)),
              out_shape=jax.ShapeDtypeStruct((2, nb, t, blk), F32))
    act = _ffn_act_fwd(up, cw, cb, name=f"ffn{tag}_act")
    out = _mmx(act, w_down, dims=NN, grid=(t // tm, d // tn, nb), name=f"ffn{tag}_down", tile=(tm, tn), res=h,
               a_spec=pl.BlockSpec((None, tm, blk), lambda i, j, k: (k, i, 0)),
               b_spec=pl.BlockSpec((blk, tn), lambda i, j, k: (k, j)),
               o_spec=pl.BlockSpec((tm, tn), lambda i, j, k: (i, j)),
               out_shape=jax.ShapeDtypeStruct((t, d), F32))
    return out, (hn, up, act)


def _ffn_backward(dh, h, gain, saved, w_up, cw, cb, w_down, tag):
    hn, up, act = saved
    t, d = h.shape
    tm, tn, blk, nb = min(FFN_TM, t), FFN_TN, FFN_BLK, FFN_NB
    tk = min(FFN_TM, t)
    da = _mmx(dh, w_down, dims=NT, grid=(t // tm, nb, 1), name=f"ffn{tag}_d_act", tile=(tm, blk),
              a_spec=pl.BlockSpec((tm, d), lambda i, j, k: (i, 0)),
              b_spec=pl.BlockSpec((blk, d), lambda i, j, k: (j, 0)),
              o_spec=pl.BlockSpec((None, tm, blk), lambda i, j, k: (j, i, 0)),
              out_shape=jax.ShapeDtypeStruct((nb, t, blk), F32))
    dwd = _mmx(act, dh, dims=TN, grid=(nb, d // tn, t // tk), name=f"ffn{tag}_d_wdown", tile=(blk, tn), split_rows=blk // 2,
               a_spec=pl.BlockSpec((None, tk, blk), lambda i, j, k: (i, k, 0)),
               b_spec=pl.BlockSpec((tk, tn), lambda i, j, k: (k, j)),
               o_spec=pl.BlockSpec((2, None, blk // 2, tn), lambda i, j, k: (0, i, 0, j)),
               out_shape=jax.ShapeDtypeStruct((2, nb, blk // 2, d), BF16))
    dup, dcw, dcb = _ffn_act_bwd(up, da, cw, cb, name=f"ffn{tag}_act_bwd")
    dhn = _mmx(dup, w_up, dims=NT, grid=(t // tm, d // tn, 2 * nb), name=f"ffn{tag}_d_hn", tile=(tm, tn),
               a_spec=pl.BlockSpec((None, None, tm, blk), lambda i, j, k: (k // nb, k % nb, i, 0)),
               b_spec=pl.BlockSpec((None, tn, blk), lambda i, j, k: (k, j, 0)),
               o_spec=pl.BlockSpec((tm, tn), lambda i, j, k: (i, j)),
               out_shape=jax.ShapeDtypeStruct((t, d), F32))
    dwu = _mmx(hn, dup, dims=TN, grid=(1, 2 * nb, t // tk), name=f"ffn{tag}_d_wup", tile=(d, blk),
               a_spec=pl.BlockSpec((tk, d), lambda i, j, k: (k, 0)),
               b_spec=pl.BlockSpec((None, None, tk, blk), lambda i, j, k: (j // nb, j % nb, k, 0)),
               o_spec=pl.BlockSpec((None, None, d, blk), lambda i, j, k: (j % 2, j // 2, 0, 0)),
               out_shape=jax.ShapeDtypeStruct((2, N_DEV // 2, d, blk), BF16))
    dh_in, dgain = _norm_bwd(h, gain, dhn, dh, name=f"ffn{tag}_norm_bwd")
    conv_w = dcw.transpose(2, 0, 1, 3).reshape(3, 2 * nb * blk)
    return dh_in, dict(w_up=dwu, w_down=dwd, conv_w=conv_w, conv_b=dcb.reshape(1, 2 * nb * blk), norm=dgain)


def kernel(x, norm_mix, norm_ffn, ret_gdn_w_in, gdn_conv_w, gdn_a_log, gdn_dt_bias, gdn_out_gain, ret_gdn_w_out, lru_w_in, lru_conv_w, lru_conv_b, lru_w_a, lru_b_a, lru_w_x, lru_b_x, lru_lambda, lru_w_out, ffn_w_up, ffn_conv_w, ffn_conv_b, ffn_w_down, norm_final, loss_target, m_norm_mix, m_norm_ffn, m_ret_gdn_w_in, m_gdn_conv_w, m_gdn_a_log, m_gdn_dt_bias, m_gdn_out_gain, m_ret_gdn_w_out, m_lru_w_in, m_lru_conv_w, m_lru_conv_b, m_lru_w_a, m_lru_b_a, m_lru_w_x, m_lru_b_x, m_lru_lambda, m_lru_w_out, m_ffn_w_up, m_ffn_conv_w, m_ffn_conv_b, m_ffn_w_down, m_norm_final, v_norm_mix, v_norm_ffn, v_ret_gdn_w_in, v_gdn_conv_w, v_gdn_a_log, v_gdn_dt_bias, v_gdn_out_gain, v_ret_gdn_w_out, v_lru_w_in, v_lru_conv_w, v_lru_conv_b, v_lru_w_a, v_lru_b_a, v_lru_w_x, v_lru_b_x, v_lru_lambda, v_lru_w_out, v_ffn_w_up, v_ffn_conv_w, v_ffn_conv_b, v_ffn_w_down, v_norm_final):
    given = dict(locals())
    w = {n: given[n] for n in WEIGHTS}
    me = 4 * lax.axis_index("x") + 2 * lax.axis_index("y") + lax.axis_index("c")
    t = x.shape[1]
    f = D_FF

    first_shards = {'w_in0': ret_gdn_w_in[0]}
    rest_shards = {'w_out0': ret_gdn_w_out[0], 'lru_in': lru_w_in[0], 'lru_out': lru_w_out[0], 'up0': ffn_w_up[0], 'up1': ffn_w_up[1],
                   'down0': ffn_w_down[0], 'down1': ffn_w_down[1]}
    small_shapes = [s for s, _ in SMALL.values()]
    small_buf = _pack([_to_rows(w[n]) for n in SMALL], _small_rows(small_shapes))
    *g_first, g_small = _Comm([_AllGather(a.astype(BF16)) for a in first_shards.values()] + [_AllGather(small_buf)]).run("gather_first")
    rest_gather = _Comm([_AllGather(a.astype(BF16)) for a in rest_shards.values()])
    got = dict(zip(first_shards, g_first))
    small_blocks = dict(zip(SMALL, _unpack(g_small, small_shapes, lead=(N_DEV,))))
    full = {n: _join_blocks(small_blocks[n], SMALL[n][1]) for n in SMALL if n != 'ffn_conv_w'}

    w_main, w_narrow = _join_w_in(got['w_in0'], name="join_w_in")
    fcw = [small_blocks['ffn_conv_w'][:, l].reshape(2, FFN_NB, 3, FFN_BLK) for l in range(2)]
    fcb = [ffn_conv_b[l].reshape(2, FFN_NB, 1, FFN_BLK) for l in range(2)]
    gdn_cw = full['gdn_conv_w'][0]
    al_pad = jnp.pad(gdn_a_log, ((0, 0), (HEADS, LANES - 2 * HEADS)))
    dt_pad = jnp.pad(gdn_dt_bias, ((0, 0), (HEADS, LANES - 2 * HEADS)))
    lru_cw, lru_cb = full['lru_conv_w'][0], full['lru_conv_b']
    lru_ba, lru_bx, lru_lam = full['lru_b_a'], full['lru_b_x'], full['lru_lambda']
    wa, wx = lru_w_a[0], lru_w_x[0]

    h0, target = x[0], loss_target[0]
    hn0 = _norm_fwd(h0, norm_mix[0:1], name="mix0_norm")
    proj = _mm(hn0, w_main, name="mix0_in")
    pnarrow = _mm(hn0, w_narrow, name="mix0_in_narrow")
    tables = _ret_tables(t)
    y0, ret_states = _ret_fwd(proj, tables, name="retention_fwd")
    y0, gdn_states, g_rest = _gdn_fwd(proj, pnarrow, gdn_cw, al_pad, dt_pad, gdn_out_gain, y0, rest_gather, name="deltanet_fwd")
    got.update(zip(rest_shards, g_rest))
    lru_in = _join_blocks(got['lru_in'], 1)
    lru_in_g, lru_in_x = lru_in[:, :D_MODEL], lru_in[:, D_MODEL:]
    lru_out = got['lru_out'].reshape(D_MODEL, D_MODEL)
    w_out0 = got['w_out0'].reshape(D_MODEL, D_MODEL)
    w_up = [got['up0'], got['up1']]
    down = [got['down0'].reshape(D_FF, D_MODEL), got['down1'].reshape(D_FF, D_MODEL)]
    h1 = _mm(y0, w_out0, res=h0, name="mix0_out")
    h2, ffn0_saved = _ffn_forward(h1, norm_ffn[0:1], w_up[0], fcw[0], fcb[0], down[0], 0)
    hn1 = _norm_fwd(h2, norm_mix[1:2], name="mix1_norm")
    gate = _mm(hn1, lru_in_g, name="mix1_in_gate")
    xpre = _mm(hn1, lru_in_x, name="mix1_in_x")
    y1, hs = _lru_fwd(gate, xpre, lru_cw, lru_cb, wa, lru_ba, wx, lru_bx, lru_lam, name="rglru_fwd")
    h3 = _mm(y1, lru_out, res=h2, name="mix1_out")
    h4, ffn1_saved = _ffn_forward(h3, norm_ffn[1:2], w_up[1], fcw[1], fcb[1], down[1], 1)
    dh4, d_norm_final, loss_part = _final_loss(h4, norm_final[None, :], target, name="final_norm_loss")
    loss = lax.psum(loss_part[0, 0], ("x", "y", "c"))

    dh3, gf1 = _ffn_backward(dh4, h3, norm_ffn[1:2], ffn1_saved, w_up[1], fcw[1], fcb[1], down[1], 1)
    dy1 = _mm(dh3, lru_out, tb=True, name="mix1_d_y")
    d_lru_out = _mm(y1, dh3, ta=True, out_dtype=BF16, name="mix1_d_wout")
    dgate, dxpre, d_lcw, d_lcb, d_wa, d_ba, d_wx, d_bx, d_lam = _lru_bwd(
        gate, xpre, hs, dy1, lru_cw, lru_cb, wa, lru_ba, wx, lru_bx, lru_lam, name="rglru_bwd")
    dhn1 = _mm(dgate, lru_in_g, tb=True, name="mix1_d_hn_gate")
    dhn1 = _mm(dxpre, lru_in_x, tb=True, res=dhn1, name="mix1_d_hn_x")
    d_lru_in = jnp.concatenate([_mm(hn1, dgate, ta=True, out_dtype=BF16, name="mix1_d_win_gate"),
                                _mm(hn1, dxpre, ta=True, out_dtype=BF16, name="mix1_d_win_x")], axis=1)
    dh2, d_mix1 = _norm_bwd(h2, norm_mix[1:2], dhn1, dh3, name="mix1_norm_bwd")
    dh1, gf0 = _ffn_backward(dh2, h1, norm_ffn[0:1], ffn0_saved, w_up[0], fcw[0], fcb[0], down[0], 0)
    dy0 = _mm(dh1, w_out0, tb=True, name="mix0_d_y")
    d_w_out0 = _mm(y0, dh1, ta=True, out_dtype=BF16, name="mix0_d_wout")

    def by_core_chip(full_grad, axis):
        blocks = _split_blocks(full_grad, axis)
        return blocks.reshape((4, 2) + blocks.shape[1:]).transpose(1, 0, 2, 3)

    def pair_add(blocks, theirs):
        return {k: _pair_add(b, o, name=f"pair_add_{k}", tr=row_tile(b.shape[2])) for (k, b), o in zip(blocks.items(), theirs)}

    row_tile = lambda rows: rows if rows <= 512 else 256

    rest_blocks = {'w_out0': by_core_chip(d_w_out0, 0), 'lru_in': by_core_chip(d_lru_in, 1), 'lru_out': by_core_chip(d_lru_out, 0),
                   'up0': gf0['w_up'], 'up1': gf1['w_up'], 'down0': gf0['w_down'], 'down1': gf1['w_down']}
    dproj, theirs = _ret_bwd(proj, tables, ret_states, dy0, _Comm([_PairSwap(b) for b in rest_blocks.values()]), name="retention_bwd")
    z_rest = pair_add(rest_blocks, theirs)
    early = {'lru_conv_w': d_lcw[None], 'lru_conv_b': d_lcb, 'lru_b_a': d_ba, 'lru_b_x': d_bx, 'lru_lambda': d_lam,
             'ffn_conv_w': jnp.stack([gf0['conv_w'], gf1['conv_w']]), 'norm_ffn': jnp.concatenate([gf0['norm'], gf1['norm']], axis=0),
             'norm_mix1': d_mix1, 'lru_w_a': d_wa[None], 'lru_w_x': d_wx[None],
             'ffn_conv_b': jnp.concatenate([gf0['conv_b'], gf1['conv_b']], axis=0), 'norm_final': d_norm_final[0]}
    early_buf = _pack([_to_rows(early[n]) for n in EARLY], _small_rows(list(EARLY.values())))
    rest_exchange = _Comm([_ChipExchange(z) for z in z_rest.values()] + [_AllGather(early_buf)])

    (dproj, dnarrow, d_gcw, d_alog, d_dtb, d_gain), (*w_rest, got_early) = _gdn_bwd(
        proj, pnarrow, gdn_cw, al_pad, dt_pad, gdn_out_gain, gdn_states, dy0, dproj, rest_exchange, name="deltanet_bwd")
    dhn0 = _mm(dproj, w_main, tb=True, name="mix0_d_hn")
    dhn0 = _mm(dnarrow, w_narrow, tb=True, res=dhn0, name="mix0_d_hn_narrow")
    d_w_main = _mm(hn0, dproj, ta=True, out_dtype=BF16, name="mix0_d_win")
    d_w_narrow = _mm(hn0, dnarrow, ta=True, out_dtype=BF16, name="mix0_d_win_narrow")
    dx, d_mix0 = _norm_bwd(h0, norm_mix[0:1], dhn0, dh1, name="mix0_norm_bwd")

    first_blocks = {'w_in0': _split_w_in(d_w_main, d_w_narrow, name="split_d_w_in")}
    z_first = pair_add(first_blocks, _Comm([_PairSwap(b) for b in first_blocks.values()]).run("pair_swap_first"))
    late = {'gdn_conv_w': d_gcw[None], 'norm_mix0': d_mix0, 'gdn_a_log': d_alog[:, HEADS:2 * HEADS],
            'gdn_dt_bias': d_dtb[:, HEADS:2 * HEADS], 'gdn_out_gain': d_gain}
    late_buf = _pack([_to_rows(late[n]) for n in LATE], _small_rows(list(LATE.values())))
    *w_first, got_late = _Comm([_ChipExchange(z) for z in z_first.values()] + [_AllGather(late_buf)]).run("exchange_first")

    summed = {k: _sum_slots(blocks, name=f"sum_blocks_{k}", tr=row_tile(blocks.shape[1]))
              for k, blocks in list(zip(z_rest, w_rest)) + list(zip(z_first, w_first))}
    grads = {'ret_gdn_w_in': summed['w_in0'][None], 'ret_gdn_w_out': summed['w_out0'][None], 'lru_w_in': summed['lru_in'][None],
             'lru_w_out': summed['lru_out'][None], 'ffn_w_up': jnp.stack([summed['up0'], summed['up1']]),
             'ffn_w_down': jnp.stack([summed['down0'], summed['down1']])}
    partial = dict(zip(EARLY, _unpack(_sum_slots(got_early, name="sum_partials_early"), list(EARLY.values()))))
    partial.update(zip(LATE, _unpack(_sum_slots(got_late, name="sum_partials_late"), list(LATE.values()))))
    partial['norm_mix'] = jnp.concatenate([partial.pop('norm_mix0'), partial.pop('norm_mix1')], axis=0)
    for n, g_full in partial.items():
        if n in SMALL:
            shard, axis = SMALL[n]
            g_full = lax.dynamic_slice_in_dim(g_full, me * shard[axis], shard[axis], axis=axis)
        grads[n] = g_full

    delta, new_m, new_v = {}, {}, {}
    for n in WEIGHTS:
        delta[n], new_m[n], new_v[n] = _adam(w[n], grads[n], given["m_" + n], given["v_" + n], name=f"adamw_{n}")
    return (loss, dx[None], *[grads[n] for n in WEIGHTS], *[delta[n] for n in WEIGHTS],
            *[new_m[n] for n in WEIGHTS], *[new_v[n] for n in WEIGHTS])
```

```python
import functools
import math

import numpy as np
import jax
import jax.numpy as jnp
from jax import lax
from jax.experimental import pallas as pl
from jax.experimental.pallas import tpu as pltpu

F32 = jnp.float32
BF16 = jnp.bfloat16
HI = lax.Precision.HIGHEST
MESH = pl.DeviceIdType.MESH

N_DEV = 8
LANES = 128
SUBLANES = 8
EPS = 1e-6
D_MODEL = 1024
HEADS = 4
HEAD_DIM = 128
RET_CHUNK = 128
GDN_CHUNK = 64
ROPE_BASE = 10000.0
LRU_C = 8.0
D_FF = 2816
MAIN_IN = 4096
SMALL_IN = 8
QSCALE = HEAD_DIM ** -0.5

ADAM_LR, ADAM_B1, ADAM_B2, ADAM_EPS, ADAM_WD, ADAM_STEP = 0.001, 0.9, 0.999, 1e-08, 0.01, 10


def _cp(sem=None, vmem_mb=None):
    kw = {}
    if sem is not None:
        kw["dimension_semantics"] = sem
    if vmem_mb is not None:
        kw["vmem_limit_bytes"] = vmem_mb << 20
    return pltpu.CompilerParams(**kw)


def _rows(shape):
    return lax.broadcasted_iota(jnp.int32, shape, 0)


def _cols(shape):
    return lax.broadcasted_iota(jnp.int32, shape, 1)


def _shift_down(cur, prev8, s):
    if s == 0:
        return cur
    rc = pltpu.roll(cur, s, 0)
    rp = pltpu.roll(prev8, s, 0)
    top = jnp.where(_rows(prev8.shape) < s, rp, rc[:SUBLANES])
    return jnp.concatenate([top, rc[SUBLANES:]], axis=0)


def _shift_up(cur, next8, s):
    if s == 0:
        return cur
    tt = cur.shape[0]
    rc = pltpu.roll(cur, tt - s, 0)
    rn = pltpu.roll(next8, SUBLANES - s, 0)
    bot = jnp.where(_rows(next8.shape) >= SUBLANES - s, rn, rc[tt - SUBLANES:])
    return jnp.concatenate([rc[:tt - SUBLANES], bot], axis=0)


def _down_fill(x, d, fill):
    return jnp.where(_rows(x.shape) < d, fill, pltpu.roll(x, d, 0))


def _up_fill(x, d, fill):
    tt = x.shape[0]
    return jnp.where(_rows(x.shape) >= tt - d, fill, pltpu.roll(x, tt - d, 0))


def _sigmoid(x):
    return 1.0 / (1.0 + jnp.exp(-x))


def _softplus(x):
    return jnp.maximum(x, 0.0) + jnp.log(1.0 + jnp.exp(-jnp.abs(x)))


def _dot(a, b, dims=(((1,), (0,)), ((), ())), precision=None):
    return lax.dot_general(a, b, dims, preferred_element_type=F32, precision=precision)


NN = (((1,), (0,)), ((), ()))
NT = (((1,), (1,)), ((), ()))
TN = (((0,), (0,)), ((), ()))


def _bdot(a, b, dims=NN):
    return _dot(a.astype(BF16), b.astype(BF16), dims)


def _split(a):
    hi = a.astype(BF16)
    return hi, (a - hi.astype(F32)).astype(BF16)


def _dot3(a, b, dims=NN):
    ah, al = _split(a)
    bh, bl = _split(b)
    return _dot(ah, bh, dims) + (_dot(ah, bl, dims) + _dot(al, bh, dims))


def _tile(dim, target):
    if dim <= target:
        return dim
    best = None
    for c in range(LANES, target + 1, LANES):
        if dim % c == 0:
            best = c
    assert best is not None, (dim, target)
    return best


def _mm(a, b, *, name, ta=False, tb=False, out_dtype=F32, res=None, tm=2048, tn=512, tk=1024):
    m, k = (a.shape[1], a.shape[0]) if ta else a.shape
    n = b.shape[0] if tb else b.shape[1]
    tn, tk = _tile(n, tn), _tile(k, tk)
    tm = _tile(m, tm if max(tn, tk) <= 1024 else tm // 2)
    nk = k // tk
    dims = (((0 if ta else 1,), (1 if tb else 0,)), ((), ()))

    def body(*refs):
        a_ref, b_ref = refs[:2]
        r_ref = refs[2] if res is not None else None
        o_ref = refs[3] if res is not None else refs[2]
        acc = refs[-1]
        kk = pl.program_id(2)
        part = _bdot(a_ref[...], b_ref[...], dims)

        def finish(r):
            if res is not None:
                r = r + r_ref[...]
            o_ref[...] = r.astype(out_dtype)

        if nk == 1:
            finish(part)
            return

        @pl.when(kk == 0)
        def _():
            acc[...] = part

        @pl.when(jnp.logical_and(kk > 0, kk < nk - 1))
        def _():
            acc[...] += part

        @pl.when(kk == nk - 1)
        def _():
            finish(acc[...] + part)

    a_spec = pl.BlockSpec((tk, tm), lambda i, j, kk: (kk, i)) if ta else pl.BlockSpec((tm, tk), lambda i, j, kk: (i, kk))
    b_spec = pl.BlockSpec((tn, tk), lambda i, j, kk: (j, kk)) if tb else pl.BlockSpec((tk, tn), lambda i, j, kk: (kk, j))
    o_spec = pl.BlockSpec((tm, tn), lambda i, j, kk: (i, j))
    in_specs = [a_spec, b_spec] + ([o_spec] if res is not None else [])
    args = (a, b) + ((res,) if res is not None else ())
    return pl.pallas_call(
        body, grid=(m // tm, n // tn, nk), in_specs=in_specs, out_specs=o_spec,
        out_shape=jax.ShapeDtypeStruct((m, n), out_dtype),
        scratch_shapes=[pltpu.VMEM((tm, tn), F32)] if nk > 1 else [], name=name,
        compiler_params=_cp(("parallel", "parallel", "arbitrary"), 56),
    )(*args)


def _mmx(a, b, *, dims, grid, a_spec, b_spec, o_spec, out_shape, tile, name, res=None, split_rows=None):
    nk = grid[-1]

    def body(*refs):
        a_ref, b_ref = refs[:2]
        r_ref = refs[2] if res is not None else None
        o_ref = refs[3] if res is not None else refs[2]
        acc = refs[-1]
        part = _bdot(a_ref[...], b_ref[...], dims)

        def finish(r):
            if res is not None:
                r = r + r_ref[...]
            if split_rows is None:
                o_ref[...] = r.astype(o_ref.dtype)
            else:
                o_ref[0] = r[:split_rows].astype(o_ref.dtype)
                o_ref[1] = r[split_rows:].astype(o_ref.dtype)

        if nk == 1:
            finish(part)
            return
        kk = pl.program_id(len(grid) - 1)

        @pl.when(kk == 0)
        def _():
            acc[...] = part

        @pl.when(jnp.logical_and(kk > 0, kk < nk - 1))
        def _():
            acc[...] += part

        @pl.when(kk == nk - 1)
        def _():
            finish(acc[...] + part)

    args = (a, b) + ((res,) if res is not None else ())
    return pl.pallas_call(
        body, grid=grid, in_specs=[a_spec, b_spec] + ([o_spec] if res is not None else []), out_specs=o_spec,
        out_shape=out_shape, scratch_shapes=[pltpu.VMEM(tile, F32)] if nk > 1 else [], name=name,
        compiler_params=_cp(("parallel",) * (len(grid) - 1) + ("arbitrary",), 56),
    )(*args)


W_IN_BLK = 513
W_IN_TR = 256


def _join_w_in(blocks, *, name):
    _, d, _ = blocks.shape
    tr = W_IN_TR

    def body(x_ref, main_ref, narrow_ref):
        for m in range(MAIN_IN // LANES):
            lo = LANES * m
            dev, off = divmod(lo, W_IN_BLK)
            if off + LANES <= W_IN_BLK:
                main_ref[:, lo:lo + LANES] = x_ref[dev, :, off:off + LANES]
            else:
                main_ref[:, lo:lo + LANES] = jnp.concatenate(
                    [x_ref[dev, :, off:W_IN_BLK], x_ref[dev + 1, :, 0:LANES - (W_IN_BLK - off)]], axis=1)
        tail = x_ref[N_DEV - 1, :, W_IN_BLK - SMALL_IN:W_IN_BLK]
        narrow_ref[...] = jnp.concatenate([tail, jnp.zeros((tr, LANES - SMALL_IN), tail.dtype)], axis=1)

    return pl.pallas_call(
        body, grid=(d // tr,), in_specs=[pl.BlockSpec((N_DEV, tr, W_IN_BLK), lambda i: (0, i, 0))],
        out_specs=[pl.BlockSpec((tr, MAIN_IN), lambda i: (i, 0)), pl.BlockSpec((tr, LANES), lambda i: (i, 0))],
        out_shape=[jax.ShapeDtypeStruct((d, MAIN_IN), blocks.dtype), jax.ShapeDtypeStruct((d, LANES), blocks.dtype)],
        name=name, compiler_params=_cp(("parallel",), 48),
    )(blocks)


def _split_w_in(main, narrow, *, name):
    d = main.shape[0]
    tr = W_IN_TR

    def body(m_ref, n_ref, o_ref):
        for dev in range(N_DEV):
            lo = W_IN_BLK * dev
            if dev < N_DEV - 1:
                piece = m_ref[:, lo:lo + W_IN_BLK]
            else:
                piece = jnp.concatenate([m_ref[:, lo:MAIN_IN], n_ref[:, 0:SMALL_IN]], axis=1)
            o_ref[dev % 2, dev // 2] = piece

    return pl.pallas_call(
        body, grid=(d // tr,),
        in_specs=[pl.BlockSpec((tr, MAIN_IN), lambda i: (i, 0)), pl.BlockSpec((tr, LANES), lambda i: (i, 0))],
        out_specs=pl.BlockSpec((2, N_DEV // 2, tr, W_IN_BLK), lambda i: (0, 0, i, 0)),
        out_shape=jax.ShapeDtypeStruct((2, N_DEV // 2, d, W_IN_BLK), main.dtype),
        name=name, compiler_params=_cp(("parallel",), 48),
    )(main, narrow)


def _norm_fwd(h, gain, *, name, tt=256):
    t, d = h.shape
    tt = min(tt, t)

    def body(h_ref, g_ref, o_ref):
        x = h_ref[...]
        r = lax.rsqrt(jnp.mean(x * x, axis=-1, keepdims=True) + EPS)
        o_ref[...] = (x * r * g_ref[...]).astype(BF16)

    row = pl.BlockSpec((tt, d), lambda i: (i, 0))
    return pl.pallas_call(
        body, grid=(t // tt,), in_specs=[row, pl.BlockSpec((1, d), lambda i: (0, 0))], out_specs=row,
        out_shape=jax.ShapeDtypeStruct((t, d), BF16), name=name, compiler_params=_cp(("parallel",)),
    )(h, gain)


def _norm_bwd(h, gain, dhn, dres, *, name, tt=256):
    t, d = h.shape
    tt = min(tt, t)

    def body(h_ref, g_ref, dy_ref, dr_ref, dx_ref, dg_ref):
        x, dy = h_ref[...], dy_ref[...]
        r = lax.rsqrt(jnp.mean(x * x, axis=-1, keepdims=True) + EPS)
        xh = x * r

        @pl.when(pl.program_id(0) == 0)
        def _():
            dg_ref[...] = jnp.zeros_like(dg_ref)

        dg_ref[...] += jnp.sum(dy * xh, axis=0, keepdims=True)
        dxh = dy * g_ref[...]
        dx_ref[...] = dr_ref[...] + r * (dxh - xh * jnp.mean(dxh * xh, axis=-1, keepdims=True))

    row = pl.BlockSpec((tt, d), lambda i: (i, 0))
    vec = pl.BlockSpec((1, d), lambda i: (0, 0))
    return pl.pallas_call(
        body, grid=(t // tt,), in_specs=[row, vec, row, row], out_specs=[row, vec],
        out_shape=[jax.ShapeDtypeStruct((t, d), F32), jax.ShapeDtypeStruct((1, d), F32)],
        name=name, compiler_params=_cp(("arbitrary",)),
    )(h, gain, dhn, dres)


def _final_loss(h, gain, target, *, name, tt=256):
    t, d = h.shape
    tt = min(tt, t)

    def body(h_ref, g_ref, tg_ref, dx_ref, dg_ref, loss_ref):
        x = h_ref[...]
        r = lax.rsqrt(jnp.mean(x * x, axis=-1, keepdims=True) + EPS)
        xh = x * r
        err = xh * g_ref[...] - tg_ref[...]

        @pl.when(pl.program_id(0) == 0)
        def _():
            dg_ref[...] = jnp.zeros_like(dg_ref)
            loss_ref[...] = jnp.zeros_like(loss_ref)

        loss_ref[...] += 0.5 * jnp.sum(jnp.mean(err * err, axis=-1, keepdims=True), axis=0, keepdims=True)
        dy = err * (1.0 / d)
        dg_ref[...] += jnp.sum(dy * xh, axis=0, keepdims=True)
        dxh = dy * g_ref[...]
        dx_ref[...] = r * (dxh - xh * jnp.mean(dxh * xh, axis=-1, keepdims=True))

    row = pl.BlockSpec((tt, d), lambda i: (i, 0))
    vec = pl.BlockSpec((1, d), lambda i: (0, 0))
    return pl.pallas_call(
        body, grid=(t // tt,), in_specs=[row, vec, row],
        out_specs=[row, vec, pl.BlockSpec((1, 1), lambda i: (0, 0))],
        out_shape=[jax.ShapeDtypeStruct((t, d), F32), jax.ShapeDtypeStruct((1, d), F32), jax.ShapeDtypeStruct((1, 1), F32)],
        name=name, compiler_params=_cp(("arbitrary",)),
    )(h, gain, target)


FFN_BLK = 704
FFN_NB = 4
FFN_TT = 256


def _prev8(n, tt):
    return jnp.maximum(n * (tt // SUBLANES) - 1, 0)


def _ffn_conv(cur, prev8, w, b):
    s1 = _shift_down(cur, prev8, 1)
    s2 = _shift_down(cur, prev8, 2)
    return w[0:1] * s2 + w[1:2] * s1 + w[2:3] * cur + b, s1, s2


def _ffn_specs(t, tt, order):
    pair = lambda rows, row_index: pl.BlockSpec((2, None, rows, FFN_BLK), lambda j, n: (0, j, row_index(n), 0))
    return dict(cur=pair(tt, order), prev=pair(SUBLANES, lambda n: _prev8(order(n), tt)), w=pair(3, lambda n: 0), b=pair(1, lambda n: 0),
                one=pl.BlockSpec((None, tt, FFN_BLK), lambda j, n: (j, order(n), 0)))


def _ffn_act_fwd(up, cw, cb, *, name):
    t = up.shape[2]
    tt = min(FFN_TT, t)
    sp = _ffn_specs(t, tt, lambda n: n)

    def body(u_ref, p_ref, w_ref, b_ref, o_ref):
        first = pl.program_id(1) == 0
        gate, _, _ = _ffn_conv(u_ref[0], jnp.where(first, 0.0, p_ref[0]), w_ref[0], b_ref[0])
        val, _, _ = _ffn_conv(u_ref[1], jnp.where(first, 0.0, p_ref[1]), w_ref[1], b_ref[1])
        o_ref[...] = (gate * _sigmoid(gate) * val).astype(BF16)

    return pl.pallas_call(
        body, grid=(FFN_NB, t // tt), in_specs=[sp["cur"], sp["prev"], sp["w"], sp["b"]], out_specs=sp["one"],
        out_shape=jax.ShapeDtypeStruct((FFN_NB, t, FFN_BLK), BF16), name=name,
        compiler_params=_cp(("parallel", "arbitrary"), 48),
    )(up, up, cw, cb)


def _ffn_act_bwd(up, da, cw, cb, *, name):
    t = up.shape[2]
    tt = min(FFN_TT, t)
    nt = t // tt
    sp = _ffn_specs(t, tt, lambda n: nt - 1 - n)

    def body(u_ref, p_ref, da_ref, w_ref, b_ref, du_ref, dw_ref, db_ref, head):
        n = pl.program_id(1)
        tile0 = n == nt - 1

        @pl.when(n == 0)
        def _():
            for r in (head, dw_ref, db_ref):
                r[...] = jnp.zeros_like(r)

        convs = [_ffn_conv(u_ref[s], jnp.where(tile0, 0.0, p_ref[s]), w_ref[s], b_ref[s]) for s in range(2)]
        gate, val = convs[0][0], convs[1][0]
        d = da_ref[...]
        sg = _sigmoid(gate)
        dcs = (d * val * sg * (1.0 + gate * (1.0 - sg)), d * gate * sg)
        for s in range(2):
            dc, w, hd = dcs[s], w_ref[s], head[s]
            _, x1, x2 = convs[s]
            du_ref[s] = (w[2:3] * dc + w[1:2] * _shift_up(dc, hd, 1) + w[0:1] * _shift_up(dc, hd, 2)).astype(BF16)
            dw_ref[s, 0:1, :] += jnp.sum(dc * x2, axis=0, keepdims=True)
            dw_ref[s, 1:2, :] += jnp.sum(dc * x1, axis=0, keepdims=True)
            dw_ref[s, 2:3, :] += jnp.sum(dc * u_ref[s], axis=0, keepdims=True)
            db_ref[s] += jnp.sum(dc, axis=0, keepdims=True)
            head[s] = dc[:SUBLANES]

    return pl.pallas_call(
        body, grid=(FFN_NB, nt), in_specs=[sp["cur"], sp["prev"], sp["one"], sp["w"], sp["b"]],
        out_specs=[sp["cur"], sp["w"], sp["b"]],
        out_shape=[jax.ShapeDtypeStruct(up.shape, BF16), jax.ShapeDtypeStruct(cw.shape, F32), jax.ShapeDtypeStruct(cb.shape, F32)],
        scratch_shapes=[pltpu.VMEM((2, SUBLANES, FFN_BLK), F32)], name=name,
        compiler_params=_cp(("parallel", "arbitrary"), 48),
    )(up, up, da, cw, cb)


LRU_TT = 256
LRU_CT = 512
GELU_C = math.sqrt(2.0 / math.pi)
GELU_A = 0.044715


def _gelu(x):
    return 0.5 * x * (1.0 + jnp.tanh(GELU_C * (x + GELU_A * x * x * x)))


def _gelu_grad(x):
    th = jnp.tanh(GELU_C * (x + GELU_A * x * x * x))
    return 0.5 * (1.0 + th) + 0.5 * x * (1.0 - th * th) * GELU_C * (1.0 + 3.0 * GELU_A * x * x)


def _neg_expm1(x):
    poly = -x * (1.0 + x * (0.5 + x * (1.0 / 6 + x * (1.0 / 24 + x * (1.0 / 120)))))
    return jnp.where(x > -0.1, poly, 1.0 - jnp.exp(x))


def _conv4(x, p8, w, b=None):
    s1, s2, s3 = _shift_down(x, p8, 1), _shift_down(x, p8, 2), _shift_down(x, p8, 3)
    y = w[0:1] * s3 + w[1:2] * s2 + w[2:3] * s1 + w[3:4] * x
    return (y if b is None else y + b), (s1, s2, s3)


def _conv4_bwd(dy, head, x, shifts, w):
    s1, s2, s3 = shifts
    dx = w[3:4] * dy + w[2:3] * _shift_up(dy, head, 1) + w[1:2] * _shift_up(dy, head, 2) + w[0:1] * _shift_up(dy, head, 3)
    dws = [jnp.sum(dy * s, axis=0, keepdims=True) for s in (s3, s2, s1, x)]
    return dx, dws


def _blockdiag(x, w_ref, dims=NN):
    nb = x.shape[1] // LANES
    return jnp.concatenate([_bdot(x[:, LANES * i:LANES * (i + 1)], w_ref[i], dims) for i in range(nb)], axis=1)


def _lru_gates(xr, wa_ref, wx_ref, ba, bx, lam):
    r = _sigmoid(_blockdiag(xr, wa_ref) + ba)
    i = _sigmoid(_blockdiag(xr, wx_ref) + bx)
    sp = _softplus(-lam)
    la = -LRU_C * r * sp
    a = jnp.exp(la)
    mult = jnp.sqrt(_neg_expm1(2.0 * la))
    return r, i, sp, a, mult


def _lru_specs(t, tt, ct, order):
    nb = ct // LANES
    cur = pl.BlockSpec((tt, ct), lambda j, n: (order(n), j))
    prev = pl.BlockSpec((SUBLANES, ct), lambda j, n: (_prev8(order(n), tt), j))
    vec = lambda rows: pl.BlockSpec((rows, ct), lambda j, n: (0, j))
    blk = pl.BlockSpec((nb, LANES, LANES), lambda j, n: (j, 0, 0))
    return cur, prev, vec, blk


def _lru_fwd(gate, xpre, cw, cb, wa, ba, wx, bx, lam, *, name):
    t, c = gate.shape
    tt, ct = min(LRU_TT, t), LRU_CT
    cur, prev, vec, blk = _lru_specs(t, tt, ct, lambda n: n)

    def body(gate_ref, x_ref, p_ref, cw_ref, cb_ref, wa_ref, ba_ref, wx_ref, bx_ref, lam_ref, y_ref, hs_ref, carry):
        n = pl.program_id(1)

        @pl.when(n == 0)
        def _():
            carry[...] = jnp.zeros_like(carry)

        p8 = jnp.where(n == 0, 0.0, p_ref[...])
        xr, _ = _conv4(x_ref[...], p8, cw_ref[...], cb_ref[...])
        r, i, sp, a, mult = _lru_gates(xr, wa_ref, wx_ref, ba_ref[...], bx_ref[...], lam_ref[...])
        acc_a, acc_b = a, mult * (i * xr)
        d = 1
        while d < tt:
            acc_b = acc_a * _down_fill(acc_b, d, 0.0) + acc_b
            acc_a = acc_a * _down_fill(acc_a, d, 1.0)
            d *= 2
        hs = acc_b + acc_a * carry[0:1]
        carry[...] = jnp.broadcast_to(hs[tt - 1:tt], carry.shape)
        hs_ref[...] = hs
        y_ref[...] = (_gelu(gate_ref[...]) * hs).astype(BF16)

    return pl.pallas_call(
        body, grid=(c // ct, t // tt),
        in_specs=[cur, cur, prev, vec(4), vec(1), blk, vec(1), blk, vec(1), vec(1)],
        out_specs=[cur, cur],
        out_shape=[jax.ShapeDtypeStruct((t, c), BF16), jax.ShapeDtypeStruct((t, c), F32)],
        scratch_shapes=[pltpu.VMEM((SUBLANES, ct), F32)], name=name,
        compiler_params=_cp(("parallel", "arbitrary"), 48),
    )(gate, xpre, xpre, cw, cb, wa, ba, wx, bx, lam)


def _lru_bwd(gate, xpre, hs, dy, cw, cb, wa, ba, wx, bx, lam, *, name):
    t, c = gate.shape
    tt, ct = min(LRU_TT, t), LRU_CT
    nt = t // tt
    cur, prev, vec, blk = _lru_specs(t, tt, ct, lambda n: nt - 1 - n)

    def body(gate_ref, x_ref, p_ref, hs_ref, phs_ref, dy_ref, cw_ref, cb_ref, wa_ref, ba_ref, wx_ref, bx_ref, lam_ref,
             dgate_ref, dx_ref, dcw_ref, dcb_ref, dwa_ref, dba_ref, dwx_ref, dbx_ref, dlam_ref, carry, head):
        n = pl.program_id(1)
        tile0 = n == nt - 1

        @pl.when(n == 0)
        def _():
            for ref in (carry, head, dcw_ref, dcb_ref, dwa_ref, dba_ref, dwx_ref, dbx_ref, dlam_ref):
                ref[...] = jnp.zeros_like(ref)

        xp, cwv, lam = x_ref[...], cw_ref[...], lam_ref[...]
        p8 = jnp.where(tile0, 0.0, p_ref[...])
        xr, shifts = _conv4(xp, p8, cwv, cb_ref[...])
        r, i, sp, a, mult = _lru_gates(xr, wa_ref, wx_ref, ba_ref[...], bx_ref[...], lam)
        gate, hsv, dyv = gate_ref[...], hs_ref[...], dy_ref[...]
        dgate_ref[...] = (dyv * hsv * _gelu_grad(gate)).astype(BF16)
        acc_b = dyv * _gelu(gate) + jnp.where(_rows(a.shape) == tt - 1, carry[0:1], 0.0)
        acc_a = _up_fill(a, 1, 0.0)
        d = 1
        while d < tt:
            acc_b = acc_b + acc_a * _up_fill(acc_b, d, 0.0)
            acc_a = acc_a * _up_fill(acc_a, d, 0.0)
            d *= 2
        gsum = acc_b
        carry[...] = jnp.broadcast_to(a[0:1] * gsum[0:1], carry.shape)
        hprev = _shift_down(hsv, jnp.where(tile0, 0.0, phs_ref[...]), 1)
        da = gsum * hprev
        dmult = gsum * i * xr
        di = gsum * mult * xr
        dxr = gsum * mult * i
        dla = da * a - dmult * (a * a) / mult
        dr = dla * (-LRU_C * sp)
        dlam_ref[...] += jnp.sum(dla * (-LRU_C * r), axis=0, keepdims=True) * (-_sigmoid(-lam))
        dpa = dr * r * (1.0 - r)
        dpx = di * i * (1.0 - i)
        dba_ref[...] += jnp.sum(dpa, axis=0, keepdims=True)
        dbx_ref[...] += jnp.sum(dpx, axis=0, keepdims=True)
        dxr = dxr + _blockdiag(dpa, wa_ref, NT) + _blockdiag(dpx, wx_ref, NT)
        for b in range(ct // LANES):
            sl = slice(LANES * b, LANES * (b + 1))
            dwa_ref[b] += _bdot(xr[:, sl], dpa[:, sl], TN)
            dwx_ref[b] += _bdot(xr[:, sl], dpx[:, sl], TN)
        dx, dws = _conv4_bwd(dxr, head[...], xp, shifts, cwv)
        dx_ref[...] = dx.astype(BF16)
        for k in range(4):
            dcw_ref[k:k + 1, :] += dws[k]
        dcb_ref[...] += jnp.sum(dxr, axis=0, keepdims=True)
        head[...] = dxr[:SUBLANES]

    return pl.pallas_call(
        body, grid=(c // ct, nt),
        in_specs=[cur, cur, prev, cur, prev, cur, vec(4), vec(1), blk, vec(1), blk, vec(1), vec(1)],
        out_specs=[cur, cur, vec(4), vec(1), blk, vec(1), blk, vec(1), vec(1)],
        out_shape=[jax.ShapeDtypeStruct((t, c), BF16)] * 2 + [jax.ShapeDtypeStruct((4, c), F32), jax.ShapeDtypeStruct((1, c), F32),
                   jax.ShapeDtypeStruct(wa.shape, F32), jax.ShapeDtypeStruct((1, c), F32),
                   jax.ShapeDtypeStruct(wx.shape, F32), jax.ShapeDtypeStruct((1, c), F32), jax.ShapeDtypeStruct((1, c), F32)],
        scratch_shapes=[pltpu.VMEM((SUBLANES, ct), F32)] * 2, name=name,
        compiler_params=_cp(("parallel", "arbitrary"), 48),
    )(gate, xpre, xpre, hs, hs, dy, cw, cb, wa, ba, wx, bx, lam)


RET_W = HEADS * HEAD_DIM
HALF = HEAD_DIM // 2


def _ret_tables(t):
    c = RET_CHUNK
    inv_freq = ROPE_BASE ** (-jnp.arange(HALF, dtype=F32) / HALF)
    ang = jnp.arange(t, dtype=jnp.int32).astype(F32)[:, None] * inv_freq[None, :]
    cos, sin = jnp.cos(ang), jnp.sin(ang)
    cosf = jnp.concatenate([cos, cos], axis=1)
    sinf = jnp.concatenate([-sin, sin], axis=1)
    log_gamma = jnp.log1p(-jnp.exp2(-5.0 - jnp.arange(HEADS, dtype=F32)))
    idx = jnp.arange(c, dtype=F32)
    rel = idx[:, None] - idx[None, :]
    causal = rel >= 0
    dmask = jnp.where(causal, jnp.exp(log_gamma[:, None, None] * jnp.where(causal, rel, 0.0)), 0.0)
    ktail = jnp.exp(log_gamma[:, None] * (c - 1 - idx))
    qdec = jnp.exp(log_gamma[:, None] * (idx + 1.0))
    rowtab = jnp.broadcast_to(jnp.stack([ktail, qdec], axis=1)[..., None], (HEADS, 2, c, HEAD_DIM))
    cdec = jnp.broadcast_to(jnp.exp(log_gamma * c)[:, None, None], (HEADS, SUBLANES, HEAD_DIM))
    return cosf, sinf, dmask, rowtab, cdec


def _rotary(x, cosf, sinf):
    return x * cosf + pltpu.roll(x, HALF, 1) * sinf


def _rotary_t(dx, cosf, sinf):
    return dx * cosf + pltpu.roll(dx * sinf, HALF, 1)


def _ret_specs(c, order):
    full = lambda shape: pl.BlockSpec(shape, lambda n: (0,) * len(shape))
    return dict(
        proj=pl.BlockSpec((c, 4 * RET_W), lambda n: (order(n), 0)),
        rot=pl.BlockSpec((c, HEAD_DIM), lambda n: (order(n), 0)),
        dmask=full((HEADS, c, c)), rowtab=full((HEADS, 2, c, HEAD_DIM)), cdec=full((HEADS, SUBLANES, HEAD_DIM)),
        state=pl.BlockSpec((1, HEADS, HEAD_DIM, HEAD_DIM), lambda n: (order(n), 0, 0, 0)),
        half=pl.BlockSpec((c, RET_W), lambda n: (order(n), 0)),
    )


def _ret_head(p_ref, h, cosf, sinf):
    sl = lambda j: slice(j * RET_W + h * HEAD_DIM, j * RET_W + (h + 1) * HEAD_DIM)
    q, k, v, g = p_ref[:, sl(0)], p_ref[:, sl(1)], p_ref[:, sl(2)], p_ref[:, sl(3)]
    return _rotary(q, cosf, sinf), _rotary(k, cosf, sinf) * QSCALE, v, g


def _ret_fwd(proj, tables, *, name):
    t = proj.shape[0]
    c = RET_CHUNK
    nc = t // c
    sp = _ret_specs(c, lambda n: n)

    def body(p_ref, cos_ref, sin_ref, dm_ref, rt_ref, cd_ref, y_ref, s_ref, state):
        @pl.when(pl.program_id(0) == 0)
        def _():
            state[...] = jnp.zeros_like(state)

        cosf, sinf = cos_ref[...], sin_ref[...]
        for h in range(HEADS):
            qr, kr, v, g = _ret_head(p_ref, h, cosf, sinf)
            s0 = state[h]
            s_ref[0, h] = s0
            scores = _bdot(qr, kr, NT) * dm_ref[h]
            o = _bdot(scores, v) + _bdot(qr * rt_ref[h, 1], s0)
            state[h] = s0 * cd_ref[h][0:1] + _bdot(kr * rt_ref[h, 0], v, TN)
            rinv = lax.rsqrt(jnp.mean(o * o, axis=-1, keepdims=True) + EPS)
            y_ref[:, h * HEAD_DIM:(h + 1) * HEAD_DIM] = (o * rinv * (g * _sigmoid(g))).astype(BF16)

    return pl.pallas_call(
        body, grid=(nc,),
        in_specs=[sp["proj"], sp["rot"], sp["rot"], sp["dmask"], sp["rowtab"], sp["cdec"]],
        out_specs=[sp["half"], sp["state"]],
        out_shape=[jax.ShapeDtypeStruct((t, 2 * RET_W), BF16), jax.ShapeDtypeStruct((nc, HEADS, HEAD_DIM, HEAD_DIM), F32)],
        scratch_shapes=[pltpu.VMEM((HEADS, HEAD_DIM, HEAD_DIM), F32)], name=name,
        compiler_params=_cp(("arbitrary",), 48),
    )(proj, *tables)


def _ret_bwd(proj, tables, states, dy, comm, *, name):
    t = proj.shape[0]
    c = RET_CHUNK
    nc = t // c
    sp = _ret_specs(c, lambda n: nc - 1 - n)
    hbm = pl.BlockSpec(memory_space=pl.ANY)

    def body(p_ref, cos_ref, sin_ref, dm_ref, rt_ref, cd_ref, s_ref, dy_ref, *rest):
        comm_in, (dp_ref,), comm_out, (dstate,), comm_sems = comm.split(rest, n_out=1, n_scratch=1)

        @pl.when(pl.program_id(0) == 0)
        def _():
            dstate[...] = jnp.zeros_like(dstate)
            comm.start(comm_in, comm_out, comm_sems)

        cosf, sinf = cos_ref[...], sin_ref[...]
        for h in range(HEADS):
            qr, kr, v, g = _ret_head(p_ref, h, cosf, sinf)
            s0, dm, ktl, qdc = s_ref[0, h], dm_ref[h], rt_ref[h, 0], rt_ref[h, 1]
            scores = _bdot(qr, kr, NT) * dm
            qd, kt = qr * qdc, kr * ktl
            o = _bdot(scores, v) + _bdot(qd, s0)
            rinv = lax.rsqrt(jnp.mean(o * o, axis=-1, keepdims=True) + EPS)
            oh = o * rinv
            sg = _sigmoid(g)
            dyh = dy_ref[:, h * HEAD_DIM:(h + 1) * HEAD_DIM]
            dg = dyh * oh * sg * (1.0 + g * (1.0 - sg))
            dyo = dyh * (g * sg)
            do = rinv * (dyo - oh * jnp.mean(dyo * oh, axis=-1, keepdims=True))
            ds1 = dstate[h]
            dsc = _bdot(do, v, NT) * dm
            dv = _bdot(scores, do, TN) + _bdot(kt, ds1)
            dqr = _bdot(dsc, kr) + _bdot(do, s0, NT) * qdc
            dkr = (_bdot(dsc, qr, TN) + _bdot(v, ds1, NT) * ktl) * QSCALE
            dstate[h] = ds1 * cd_ref[h][0:1] + _bdot(qd, do, TN)
            pieces = (_rotary_t(dqr, cosf, sinf), _rotary_t(dkr, cosf, sinf), dv, dg)
            for j, piece in enumerate(pieces):
                dp_ref[:, j * RET_W + h * HEAD_DIM:j * RET_W + (h + 1) * HEAD_DIM] = piece.astype(BF16)

        @pl.when(pl.program_id(0) == nc - 1)
        def _():
            comm.finish(comm_in, comm_out, comm_sems)

    outs = pl.pallas_call(
        body, grid=(nc,),
        in_specs=[sp["proj"], sp["rot"], sp["rot"], sp["dmask"], sp["rowtab"], sp["cdec"], sp["state"], sp["half"]]
        + [hbm] * len(comm.arrays),
        out_specs=[sp["proj"]] + [hbm] * len(comm.out_shapes),
        out_shape=[jax.ShapeDtypeStruct((t, 8 * RET_W), BF16)] + comm.out_shapes,
        scratch_shapes=[pltpu.VMEM((HEADS, HEAD_DIM, HEAD_DIM), F32)] + comm.scratch, name=name,
        compiler_params=_cp(("arbitrary",), 48),
    )(proj, *tables, states, dy, *comm.arrays)
    return outs[0], outs[1:]


GDN_W = HEADS * HEAD_DIM
GDN_CONV = 3 * GDN_W
NEUMANN_STEPS = 5


def _gdn_gates(ps, al, dt):
    return _sigmoid(ps), -jnp.exp(al) * _softplus(ps + dt)


def _cumsum_rows(x):
    d = 1
    while d < x.shape[0]:
        x = x + _down_fill(x, d, 0.0)
        d *= 2
    return x


def _rev_cumsum_rows(x):
    d = 1
    while d < x.shape[0]:
        x = x + _up_fill(x, d, 0.0)
        d *= 2
    return x


class _Chunk:
    pass


def _gdn_chunk(qc, kc, v, beta, g, s0, inv=None):
    c = GDN_CHUNK
    z = _Chunk()
    z.rq = lax.rsqrt(jnp.sum(qc * qc, axis=-1, keepdims=True) + EPS)
    z.rk = lax.rsqrt(jnp.sum(kc * kc, axis=-1, keepdims=True) + EPS)
    z.qn, z.k = qc * z.rq, kc * z.rk
    z.q = z.qn * QSCALE
    z.v, z.beta = v, beta
    gc = _cumsum_rows(jnp.broadcast_to(g, (c, LANES)))
    ri, ci = _rows((c, c)), _cols((c, c))
    z.tril, z.strict = ri >= ci, ri > ci
    diff = gc[:, :c] - gc.T[:c, :]
    z.decay = jnp.where(z.tril, jnp.exp(jnp.where(z.tril, diff, 0.0)), 0.0)
    z.eg = jnp.exp(gc)
    glast = gc[c - 1:c, :]
    z.egl = jnp.exp(glast - gc)
    z.cd = jnp.exp(glast)
    z.kb = z.k * beta
    both = _bdot(jnp.concatenate([z.kb, z.q], axis=0), z.k, NT)
    z.m, z.qk = both[:c], both[c:]
    if inv is None:
        neg = -jnp.where(z.strict, z.m * z.decay, 0.0)
        inv = (ri == ci).astype(F32) + neg
        pw = neg
        for _ in range(NEUMANN_STEPS):
            pw = _dot3(pw, pw)
            inv = inv + _dot3(inv, pw)
    z.inv = inv
    z.vb, z.kbg = v * beta, z.kb * z.eg
    solved = _dot3(inv, jnp.concatenate([z.vb, z.kbg], axis=1))
    z.u, z.w = solved[:, :HEAD_DIM], solved[:, HEAD_DIM:]
    z.attn = jnp.where(z.tril, z.qk * z.decay, 0.0)
    z.qd, z.kt = z.q * z.eg, z.k * z.egl
    through = _bdot(jnp.concatenate([z.w, z.qd], axis=0), s0)
    z.vnew = z.u - through[:c]
    z.o = through[c:] + _bdot(z.attn, z.vnew)
    z.s1 = s0 * z.cd + _bdot(z.kt, z.vnew, TN)
    return z


def _gdn_chunk_bwd(z, s0, do, ds1):
    c = GDN_CHUNK
    dvnew = _bdot(z.attn, do, TN) + _bdot(z.kt, ds1)
    against = _bdot(do, jnp.concatenate([s0, z.vnew], axis=0), NT)
    dqd = against[:, :HEAD_DIM]
    dattn = jnp.where(z.tril, against[:, HEAD_DIM:], 0.0)
    ds0 = ds1 * z.cd + _bdot(jnp.concatenate([z.qd, -z.w], axis=0), jnp.concatenate([do, dvnew], axis=0), TN)
    dcd = jnp.sum(jnp.sum(s0 * ds1, axis=1, keepdims=True), axis=0, keepdims=True)
    dkt = _bdot(z.vnew, ds1, NT)
    dw = -_bdot(dvnew, s0, NT)
    dsolved = _dot3(z.inv, jnp.concatenate([dvnew, dw], axis=1), TN)
    dvb, dkbg = dsolved[:, :HEAD_DIM], dsolved[:, HEAD_DIM:]
    dl = jnp.where(z.strict, -_bdot(dsolved, jnp.concatenate([z.u, z.w], axis=1), NT), 0.0)
    dml = dl * z.decay
    dqk = dattn * z.decay
    ddecay = (dl * z.m + dattn * z.qk) * z.decay
    stacked = jnp.concatenate([dqk, dml], axis=0)
    onto_k = _bdot(stacked, z.k)
    dq = onto_k[:c] + dqd * z.eg
    dkb = onto_k[c:] + dkbg * z.eg
    dk = _bdot(stacked, jnp.concatenate([z.q, z.kb], axis=0), TN) + dkt * z.egl + dkb * z.beta
    dbeta = jnp.sum(dkb * z.k, axis=-1, keepdims=True) + jnp.sum(dvb * z.v, axis=-1, keepdims=True)
    dv = dvb * z.beta
    colsum = _dot3(ddecay, jnp.ones((c, LANES), F32), TN)
    e = jnp.sum(dkt * z.kt, axis=-1, keepdims=True)
    dgc = (jnp.sum(ddecay, axis=-1, keepdims=True) - colsum
           + jnp.sum(dkbg * z.kbg, axis=-1, keepdims=True) + jnp.sum(dqd * z.qd, axis=-1, keepdims=True) - e)
    dglast = jnp.sum(e, axis=0, keepdims=True) + dcd * z.cd
    dgc = dgc + jnp.where(_rows((c, LANES)) == c - 1, dglast, 0.0)
    dg = _rev_cumsum_rows(dgc)[:, 0:1]
    dqn = dq * QSCALE
    dqc = z.rq * (dqn - z.qn * jnp.sum(dqn * z.qn, axis=-1, keepdims=True))
    dkc = z.rk * (dk - z.k * jnp.sum(dk * z.k, axis=-1, keepdims=True))
    return dqc, dkc, dv, dbeta, dg, ds0


GDN_SUB = 1


def _gdn_specs(c, order):
    full = lambda shape: pl.BlockSpec(shape, lambda n: (0,) * len(shape))
    return dict(
        proj=pl.BlockSpec((c, 4 * GDN_W), lambda n: (order(n), 1)),
        prev=pl.BlockSpec((SUBLANES, 4 * GDN_W), lambda n: (_prev8(order(n), c), 1)),
        small=pl.BlockSpec((c, LANES), lambda n: (order(n), 0)),
        convw=full((4, GDN_CONV)), vec=full((1, LANES)),
        state=pl.BlockSpec((GDN_SUB, HEADS, HEAD_DIM, HEAD_DIM), lambda n: (order(n), 0, 0, 0)),
        inv=pl.BlockSpec((GDN_SUB, HEADS, GDN_CHUNK, GDN_CHUNK), lambda n: (order(n), 0, 0, 0)),
        half=pl.BlockSpec((c, GDN_W), lambda n: (order(n), 1)),
        any=pl.BlockSpec(memory_space=pl.ANY),
    )


def _gdn_fwd(proj, psmall, conv_w, al, dt, gain, y_in, comm, *, name):
    t = proj.shape[0]
    c = GDN_CHUNK
    nc, ns = t // c, t // (c * GDN_SUB)
    sp = _gdn_specs(c * GDN_SUB, lambda n: n)

    def body(p_ref, prev_ref, ps_ref, cw_ref, al_ref, dt_ref, gain_ref, yin_ref, *rest):
        comm_in, (y_ref, s_ref, inv_ref), comm_out, (state,), comm_sems = comm.split(rest, n_out=3, n_scratch=1)
        n = pl.program_id(0)

        @pl.when(n == 0)
        def _():
            state[...] = jnp.zeros_like(state)
            comm.start(comm_in, comm_out, comm_sems)

        p8 = jnp.where(n == 0, 0.0, prev_ref[:, :GDN_CONV])
        pre, _ = _conv4(p_ref[:, :GDN_CONV], p8, cw_ref[...])
        act = pre * _sigmoid(pre)
        beta_all, g_all = _gdn_gates(ps_ref[...], al_ref[...], dt_ref[...])
        gd_all, gain = p_ref[:, GDN_CONV:], gain_ref[...]
        swish = gd_all * _sigmoid(gd_all)
        cur = [state[h] for h in range(HEADS)]
        starts, ys = [], []
        for sub in range(GDN_SUB):
            rows = slice(sub * c, (sub + 1) * c)
            starts.append(list(cur))
            pieces = []
            for h in range(HEADS):
                sl = lambda j: slice(j * GDN_W + h * HEAD_DIM, j * GDN_W + (h + 1) * HEAD_DIM)
                z = _gdn_chunk(act[rows, sl(0)], act[rows, sl(1)], act[rows, sl(2)], beta_all[rows, h:h + 1],
                               g_all[rows, HEADS + h:HEADS + h + 1], cur[h])
                cur[h] = z.s1
                inv_ref[sub, h] = z.inv
                rinv = lax.rsqrt(jnp.mean(z.o * z.o, axis=-1, keepdims=True) + EPS)
                pieces.append(z.o * rinv * gain * swish[rows, sl(0)])
            ys.append(jnp.concatenate(pieces, axis=1))
        y_ref[...] = jnp.concatenate(ys, axis=0).astype(BF16)
        for sub in range(GDN_SUB):
            for h in range(HEADS):
                s_ref[sub, h] = starts[sub][h]
        for h in range(HEADS):
            state[h] = cur[h]

        @pl.when(n == ns - 1)
        def _():
            comm.finish(comm_in, comm_out, comm_sems)

    outs = pl.pallas_call(
        body, grid=(ns,),
        in_specs=[sp["proj"], sp["prev"], sp["small"], sp["convw"], sp["vec"], sp["vec"], sp["vec"], sp["any"]]
        + [sp["any"]] * len(comm.arrays),
        out_specs=[sp["half"], sp["state"], sp["inv"]] + [sp["any"]] * len(comm.out_shapes),
        out_shape=[jax.ShapeDtypeStruct((t, 2 * GDN_W), BF16), jax.ShapeDtypeStruct((nc, HEADS, HEAD_DIM, HEAD_DIM), F32),
                   jax.ShapeDtypeStruct((nc, HEADS, c, c), F32)] + comm.out_shapes,
        scratch_shapes=[pltpu.VMEM((HEADS, HEAD_DIM, HEAD_DIM), F32)] + comm.scratch, name=name,
        input_output_aliases={7: 0}, compiler_params=_cp(("arbitrary",), 48),
    )(proj, proj, psmall, conv_w, al, dt, gain, y_in, *comm.arrays)
    return outs[0], (outs[1], outs[2]), outs[3:]


def _gdn_bwd(proj, psmall, conv_w, al, dt, gain, states, dy, dproj_in, comm, *, name):
    t = proj.shape[0]
    c = GDN_CHUNK
    nc, ns = t // c, t // (c * GDN_SUB)
    sp = _gdn_specs(c * GDN_SUB, lambda n: ns - 1 - n)

    def body(p_ref, prev_ref, ps_ref, cw_ref, al_ref, dt_ref, gain_ref, s_ref, inv_ref, dy_ref, dpin_ref, *rest):
        comm_in, outs, comm_out, (dstate, head), comm_sems = comm.split(rest, n_out=6, n_scratch=2)
        dp_ref, dps_ref, dcw_ref, dal_ref, ddt_ref, dgain_ref = outs
        n = pl.program_id(0)
        first_rows = n == ns - 1

        @pl.when(n == 0)
        def _():
            for ref in (dstate, head, dcw_ref, dal_ref, ddt_ref, dgain_ref):
                ref[...] = jnp.zeros_like(ref)
            comm.start(comm_in, comm_out, comm_sems)

        x, cwv = p_ref[:, :GDN_CONV], cw_ref[...]
        p8 = jnp.where(first_rows, 0.0, prev_ref[:, :GDN_CONV])
        pre, shifts = _conv4(x, p8, cwv)
        sg_pre = _sigmoid(pre)
        act = pre * sg_pre
        ps, alv, dtv, gain = ps_ref[...], al_ref[...], dt_ref[...], gain_ref[...]
        beta_all, g_all = _gdn_gates(ps, alv, dtv)
        lane = _cols((c, LANES))
        dgain = jnp.zeros((1, LANES), F32)
        dcur = [dstate[h] for h in range(HEADS)]
        dact_rows, dbeta_rows, dg_rows = [None] * GDN_SUB, [None] * GDN_SUB, [None] * GDN_SUB
        for sub in reversed(range(GDN_SUB)):
            rows = slice(sub * c, (sub + 1) * c)
            dbeta_all = jnp.zeros((c, LANES), F32)
            dg_all = jnp.zeros((c, LANES), F32)
            dact = [None] * (3 * HEADS)
            for h in range(HEADS):
                sl = lambda j: slice(j * GDN_W + h * HEAD_DIM, j * GDN_W + (h + 1) * HEAD_DIM)
                s0 = s_ref[sub, h]
                z = _gdn_chunk(act[rows, sl(0)], act[rows, sl(1)], act[rows, sl(2)], beta_all[rows, h:h + 1],
                               g_all[rows, HEADS + h:HEADS + h + 1], s0, inv=inv_ref[sub, h])
                rinv = lax.rsqrt(jnp.mean(z.o * z.o, axis=-1, keepdims=True) + EPS)
                oh = z.o * rinv
                gd = p_ref[rows, sl(3)]
                sgd = _sigmoid(gd)
                dyh = dy_ref[rows, sl(0)]
                dgain = dgain + jnp.sum(dyh * oh * (gd * sgd), axis=0, keepdims=True)
                dp_ref[rows, sl(3)] = (dyh * oh * gain * sgd * (1.0 + gd * (1.0 - sgd))).astype(BF16)
                dyo = dyh * gain * (gd * sgd)
                do = rinv * (dyo - oh * jnp.mean(dyo * oh, axis=-1, keepdims=True))
                dqc, dkc, dv, dbeta, dg, dcur[h] = _gdn_chunk_bwd(z, s0, do, dcur[h])
                dact[h], dact[HEADS + h], dact[2 * HEADS + h] = dqc, dkc, dv
                dbeta_all = dbeta_all + jnp.where(lane == h, dbeta, 0.0)
                dg_all = dg_all + jnp.where(lane == HEADS + h, dg, 0.0)
            dact_rows[sub], dbeta_rows[sub], dg_rows[sub] = jnp.concatenate(dact, axis=1), dbeta_all, dg_all
        for h in range(HEADS):
            dstate[h] = dcur[h]
        dbeta_all, dg_all = jnp.concatenate(dbeta_rows, axis=0), jnp.concatenate(dg_rows, axis=0)
        dpre = jnp.concatenate(dact_rows, axis=0) * sg_pre * (1.0 + pre * (1.0 - sg_pre))
        dx, dws = _conv4_bwd(dpre, head[...], x, shifts, cwv)
        dp_ref[:, :GDN_CONV] = dx.astype(BF16)
        for k in range(4):
            dcw_ref[k:k + 1, :] += dws[k]
        head[...] = dpre[:SUBLANES]
        dsp = dg_all * (-jnp.exp(alv)) * _sigmoid(ps + dtv)
        dps_ref[...] = (dbeta_all * beta_all * (1.0 - beta_all) + dsp).astype(BF16)
        ddt_ref[...] += jnp.sum(dsp, axis=0, keepdims=True)
        dal_ref[...] += jnp.sum(dg_all * g_all, axis=0, keepdims=True)
        dgain_ref[...] += dgain

        @pl.when(n == ns - 1)
        def _():
            comm.finish(comm_in, comm_out, comm_sems)

    vec_f32 = jax.ShapeDtypeStruct((1, LANES), F32)
    outs = pl.pallas_call(
        body, grid=(ns,),
        in_specs=[sp["proj"], sp["prev"], sp["small"], sp["convw"], sp["vec"], sp["vec"], sp["vec"], sp["state"], sp["inv"],
                  sp["half"], sp["any"]]
        + [sp["any"]] * len(comm.arrays),
        out_specs=[sp["proj"], sp["small"], sp["convw"], sp["vec"], sp["vec"], sp["vec"]] + [sp["any"]] * len(comm.out_shapes),
        out_shape=[jax.ShapeDtypeStruct((t, 8 * GDN_W), BF16), jax.ShapeDtypeStruct((t, LANES), BF16),
                   jax.ShapeDtypeStruct((4, GDN_CONV), F32), vec_f32, vec_f32, vec_f32] + comm.out_shapes,
        scratch_shapes=[pltpu.VMEM((HEADS, HEAD_DIM, HEAD_DIM), F32), pltpu.VMEM((SUBLANES, GDN_CONV), F32)] + comm.scratch,
        name=name, input_output_aliases={10: 0}, compiler_params=_cp(("arbitrary",), 48),
    )(proj, proj, psmall, conv_w, al, dt, gain, *states, dy, dproj_in, *comm.arrays)
    return outs[:6], outs[6:]


def _here():
    x, y, c = lax.axis_index("x"), lax.axis_index("y"), lax.axis_index("c")
    return x, y, c, [(1 - x, y), (x, 1 - y), (1 - x, 1 - y)]


def _rdma(src, dst, send, recv, k, dev):
    return pltpu.make_async_remote_copy(src_ref=src, dst_ref=dst, send_sem=send.at[k], recv_sem=recv.at[k],
                                        device_id=dev, device_id_type=MESH)


def _dma_sems(n):
    return [pltpu.SemaphoreType.DMA((n,)), pltpu.SemaphoreType.DMA((n,)), pltpu.SemaphoreType.DMA((1,))]


COPY_PIECES = 4
COPY_PIECE_ALIGN = 16


def _row_parts(rows):
    n = COPY_PIECES if rows % (COPY_PIECES * COPY_PIECE_ALIGN) == 0 and rows >= 1024 else 1
    return [pl.ds(q * (rows // n), rows // n) for q in range(n)]


class _AllGather:
    def __init__(self, array):
        self.arrays = [array]
        self.out_shapes = [jax.ShapeDtypeStruct((N_DEV,) + array.shape, array.dtype)]
        self.parts = _row_parts(array.shape[0])
        self.scratch = _dma_sems(7 * len(self.parts))

    def start(self, ins, outs, sems):
        (src,), (out,), (send, recv, loc) = ins, outs, sems
        x, y, c, chips = _here()
        me, n = 4 * x + 2 * y + c, len(self.parts)
        pltpu.make_async_copy(src, out.at[me], loc.at[0]).start()
        for q, part in enumerate(self.parts):
            _rdma(src.at[part], out.at[me, part], send, recv, q, (x, y, 1 - c)).start()
            for j, (cx, cy) in enumerate(chips):
                _rdma(src.at[part], out.at[me, part], send, recv, (1 + j) * n + q, (cx, cy, c)).start()

    def finish(self, ins, outs, sems):
        (src,), (out,), (send, recv, loc) = ins, outs, sems
        x, y, c, chips = _here()
        sibling, me, n = (x, y, 1 - c), 4 * x + 2 * y + c, len(self.parts)
        piece = lambda k, q: _rdma(src.at[self.parts[q]], out.at[me, self.parts[q]], send, recv, k * n + q, sibling)
        for j, (cx, cy) in enumerate(chips):
            for q, part in enumerate(self.parts):
                got = out.at[4 * cx + 2 * cy + c, part]
                piece(1 + j, q).wait_recv()
                _rdma(got, got, send, recv, (4 + j) * n + q, sibling).start()
        for k in (0, 4, 5, 6):
            for q in range(n):
                piece(k, q).wait_recv()
        for k in range(7):
            for q in range(n):
                piece(k, q).wait_send()
        pltpu.make_async_copy(src, out.at[me], loc.at[0]).wait()


class _ChipExchange:
    def __init__(self, array):
        self.arrays = [array]
        self.out_shapes = [jax.ShapeDtypeStruct(array.shape, array.dtype)]
        self.parts = _row_parts(array.shape[1])
        self.scratch = _dma_sems(3 * len(self.parts))

    def _copies(self, ins, outs, sems):
        (src,), (out,), (send, recv, loc) = ins, outs, sems
        x, y, c, chips = _here()
        here, n = 2 * x + y, len(self.parts)
        local = pltpu.make_async_copy(src.at[here], out.at[here], loc.at[0])
        return local, [_rdma(src.at[2 * cx + cy, part], out.at[here, part], send, recv, j * n + q, (cx, cy, c))
                       for j, (cx, cy) in enumerate(chips) for q, part in enumerate(self.parts)]

    def start(self, ins, outs, sems):
        local, remote = self._copies(ins, outs, sems)
        local.start()
        for cp in remote:
            cp.start()

    def finish(self, ins, outs, sems):
        local, remote = self._copies(ins, outs, sems)
        for cp in remote:
            cp.wait()
        local.wait()


class _PairSwap:
    def __init__(self, array):
        self.arrays = [array]
        self.out_shapes = [jax.ShapeDtypeStruct(array.shape[1:], array.dtype)]
        self.parts = _row_parts(array.shape[2])
        self.scratch = _dma_sems(4 * len(self.parts))

    def _copies(self, ins, outs, sems):
        (src,), (theirs,), (send, recv, _) = ins, outs, sems
        x, y, c, _ = _here()
        return [_rdma(src.at[1 - c, p, part], theirs.at[p, part], send, recv, p * len(self.parts) + q, (x, y, 1 - c))
                for p in range(4) for q, part in enumerate(self.parts)]

    def start(self, ins, outs, sems):
        for cp in self._copies(ins, outs, sems):
            cp.start()

    def finish(self, ins, outs, sems):
        for cp in self._copies(ins, outs, sems):
            cp.wait()


class _Comm:
    def __init__(self, ops):
        self.ops = ops
        self.arrays = [a for op in ops for a in op.arrays]
        self.out_shapes = [s for op in ops for s in op.out_shapes]
        self.scratch = [s for op in ops for s in op.scratch]

    def split(self, rest, n_out, n_scratch):
        cuts = np.cumsum([0, len(self.arrays), n_out, len(self.out_shapes), n_scratch, len(self.scratch)])
        assert cuts[-1] == len(rest)
        return tuple(rest[a:b] for a, b in zip(cuts[:-1], cuts[1:]))

    def _each(self, method, ins, outs, sems):
        i = o = s = 0
        for op in self.ops:
            ni, no, ns = len(op.arrays), len(op.out_shapes), len(op.scratch)
            getattr(op, method)(ins[i:i + ni], outs[o:o + no], sems[s:s + ns])
            i, o, s = i + ni, o + no, s + ns

    def start(self, ins, outs, sems):
        self._each("start", ins, outs, sems)

    def finish(self, ins, outs, sems):
        self._each("finish", ins, outs, sems)

    def run(self, name):
        def body(*refs):
            ins, _, outs, _, sems = self.split(refs, 0, 0)
            self.start(ins, outs, sems)
            self.finish(ins, outs, sems)

        hbm = pl.BlockSpec(memory_space=pl.ANY)
        return pl.pallas_call(body, in_specs=[hbm] * len(self.arrays), out_specs=[hbm] * len(self.out_shapes),
                              out_shape=self.out_shapes, scratch_shapes=self.scratch, name=name)(*self.arrays)


def _sum_slots(x, *, name, tr=None):
    n, r, l = x.shape
    tr = r if tr is None else tr

    def body(x_ref, o_ref):
        acc = x_ref[0].astype(F32)
        for s in range(1, n):
            acc = acc + x_ref[s].astype(F32)
        o_ref[...] = acc

    return pl.pallas_call(
        body, grid=(r // tr,), in_specs=[pl.BlockSpec((n, tr, l), lambda i: (0, i, 0))],
        out_specs=pl.BlockSpec((tr, l), lambda i: (i, 0)), out_shape=jax.ShapeDtypeStruct((r, l), F32),
        name=name, compiler_params=_cp(("parallel",), 48),
    )(x)


def _pair_add(both, theirs, *, name, tr):
    _, n, r, l = both.shape

    def body(a_ref, b_ref, o_ref):
        mine = jnp.where(lax.axis_index("c") == 0, a_ref[0], a_ref[1])
        o_ref[...] = (mine.astype(F32) + b_ref[...].astype(F32)).astype(BF16)

    spec = pl.BlockSpec((n, tr, l), lambda i: (0, i, 0))
    return pl.pallas_call(body, grid=(r // tr,), in_specs=[pl.BlockSpec((2, n, tr, l), lambda i: (0, 0, i, 0)), spec], out_specs=spec,
                          out_shape=jax.ShapeDtypeStruct(theirs.shape, BF16), name=name,
                          compiler_params=_cp(("parallel",), 48))(both, theirs)


ADAM_TILE_ELEMS = 512 * 1024


def _adam(w, g, m, v, *, name):
    shape = w.shape
    cols = shape[-1]
    rows = math.prod(shape[:-1]) if len(shape) > 1 else 1
    tr = rows
    if rows * cols > ADAM_TILE_ELEMS:
        tr = max(d for d in range(SUBLANES, ADAM_TILE_ELEMS // cols + 1, SUBLANES) if rows % d == 0)
    c1, c2 = 1.0 - ADAM_B1 ** ADAM_STEP, 1.0 - ADAM_B2 ** ADAM_STEP

    def body(w_ref, g_ref, m_ref, v_ref, d_ref, m2_ref, v2_ref):
        gv = g_ref[...]
        m2 = ADAM_B1 * m_ref[...] + (1.0 - ADAM_B1) * gv
        v2 = ADAM_B2 * v_ref[...] + (1.0 - ADAM_B2) * (gv * gv)
        d_ref[...] = -ADAM_LR * ((m2 / c1) / (jnp.sqrt(v2 / c2) + ADAM_EPS) + ADAM_WD * w_ref[...])
        m2_ref[...] = m2
        v2_ref[...] = v2

    spec = pl.BlockSpec((tr, cols), lambda i: (i, 0))
    outs = pl.pallas_call(
        body, grid=(rows // tr,), in_specs=[spec] * 4, out_specs=[spec] * 3,
        out_shape=[jax.ShapeDtypeStruct((rows, cols), F32)] * 3, name=name, compiler_params=_cp(("parallel",), 48),
    )(*(a.reshape(rows, cols) for a in (w, g, m, v)))
    return tuple(o.reshape(shape) for o in outs)


WEIGHTS = ['norm_mix', 'norm_ffn', 'ret_gdn_w_in', 'gdn_conv_w', 'gdn_a_log', 'gdn_dt_bias', 'gdn_out_gain', 'ret_gdn_w_out',
           'lru_w_in', 'lru_conv_w', 'lru_conv_b', 'lru_w_a', 'lru_b_a', 'lru_w_x', 'lru_b_x', 'lru_lambda', 'lru_w_out',
           'ffn_w_up', 'ffn_conv_w', 'ffn_conv_b', 'ffn_w_down', 'norm_final']
BIG = {'ret_gdn_w_in': ((1, 1024, 513), 2), 'ret_gdn_w_out': ((1, 128, 1024), 1), 'lru_w_in': ((1, 1024, 256), 2),
       'lru_w_out': ((1, 128, 1024), 1), 'ffn_w_up': ((2, 1024, 704), 2), 'ffn_w_down': ((2, 352, 1024), 1)}
SMALL = {'gdn_conv_w': ((1, 4, 192), 2), 'lru_conv_w': ((1, 4, 128), 2), 'lru_conv_b': ((1, 128), 1), 'lru_b_a': ((1, 128), 1),
         'lru_b_x': ((1, 128), 1), 'lru_lambda': ((1, 128), 1), 'ffn_conv_w': ((2, 3, 704), 2)}
REPLICATED = {'norm_mix': (2, 1024), 'norm_ffn': (2, 1024), 'gdn_a_log': (1, 4), 'gdn_dt_bias': (1, 4), 'gdn_out_gain': (1, 128),
              'lru_w_a': (1, 8, 128, 128), 'lru_w_x': (1, 8, 128, 128), 'ffn_conv_b': (2, 5632), 'norm_final': (1024,)}
FIRST = ['ret_gdn_w_in', 'ret_gdn_w_out']
REST = ['lru_w_in', 'lru_w_out', 'ffn_w_up', 'ffn_w_down']
GROUP_ROWS = {FIRST[0]: 5632, REST[0]: 19968}
GROUP_TILE = {FIRST[0]: 512, REST[0]: 1536}
EARLY = {'lru_conv_w': (1, 4, 1024), 'lru_conv_b': (1, 1024), 'lru_b_a': (1, 1024), 'lru_b_x': (1, 1024), 'lru_lambda': (1, 1024),
         'ffn_conv_w': (2, 3, 5632), 'norm_ffn': (2, 1024), 'norm_mix1': (1, 1024), 'lru_w_a': (1, 8, 128, 128),
         'lru_w_x': (1, 8, 128, 128), 'ffn_conv_b': (2, 5632), 'norm_final': (1024,)}
LATE = {'gdn_conv_w': (1, 4, 1536), 'norm_mix0': (1, 1024), 'gdn_a_log': (1, 4), 'gdn_dt_bias': (1, 4), 'gdn_out_gain': (1, 128)}


def _full_shape(shard, axis):
    return tuple(d * N_DEV if i == axis else d for i, d in enumerate(shard))


def _rows_of(n_elems):
    return -(-n_elems // LANES)


def _to_rows(a, lead=()):
    flat = a.reshape(lead + (-1,))
    pad = _rows_of(flat.shape[-1]) * LANES - flat.shape[-1]
    if pad:
        flat = jnp.pad(flat, [(0, 0)] * len(lead) + [(0, pad)])
    return flat.reshape(lead + (-1, LANES))


def _pack(pieces, total_rows, lead=()):
    buf = jnp.concatenate(pieces, axis=len(lead))
    pad = total_rows - buf.shape[len(lead)]
    return jnp.pad(buf, [(0, 0)] * len(lead) + [(0, pad), (0, 0)]) if pad else buf


def _unpack(buf, shapes, lead=()):
    out, off = [], 0
    for shape in shapes:
        n = math.prod(shape)
        rows = _rows_of(n)
        piece = lax.slice_in_dim(buf, off, off + rows, axis=len(lead)).reshape(lead + (rows * LANES,))
        out.append(lax.slice_in_dim(piece, 0, n, axis=len(lead)).reshape(lead + shape))
        off += rows
    return out


def _join_blocks(g, axis):
    m = jnp.moveaxis(g, 0, axis)
    return m.reshape(m.shape[:axis] + (N_DEV * m.shape[axis + 1],) + m.shape[axis + 2:])


def _split_blocks(full, axis):
    s = full.shape
    return jnp.moveaxis(full.reshape(s[:axis] + (N_DEV, s[axis] // N_DEV) + s[axis + 1:]), axis, 0)


def _small_rows(shapes):
    total = sum(_rows_of(math.prod(s)) for s in shapes)
    return -(-total // SUBLANES) * SUBLANES


FFN_TM = 1024
FFN_TN = 512


def _ffn_forward(h, gain, w_up, cw, cb, w_down, tag):
    t, d = h.shape
    tm, tn, blk, nb = min(FFN_TM, t), FFN_TN, FFN_BLK, FFN_NB
    hn = _norm_fwd(h, gain, name=f"ffn{tag}_norm")
    up = _mmx(hn, w_up, dims=NN, grid=(t // tm, 2 * nb, 1), name=f"ffn{tag}_up", tile=(tm, blk),
              a_spec=pl.BlockSpec((tm, d), lambda i, j, k: (i, 0)),
              b_spec=pl.BlockSpec((None, d, blk), lambda i, j, k: (j, 0, 0)),
              o_spec=pl.BlockSpec((None, None, tm, blk), lambda i, j, k: (j // nb, j % nb, i, 0)),
              out_shape=jax.ShapeDtypeStruct((2, nb, t, blk), F32))
    act = _ffn_act_fwd(up, cw, cb, name=f"ffn{tag}_act")
    out = _mmx(act, w_down, dims=NN, grid=(t // tm, d // tn, nb), name=f"ffn{tag}_down", tile=(tm, tn), res=h,
               a_spec=pl.BlockSpec((None, tm, blk), lambda i, j, k: (k, i, 0)),
               b_spec=pl.BlockSpec((blk, tn), lambda i, j, k: (k, j)),
               o_spec=pl.BlockSpec((tm, tn), lambda i, j, k: (i, j)),
               out_shape=jax.ShapeDtypeStruct((t, d), F32))
    return out, (hn, up, act)


def _ffn_backward(dh, h, gain, saved, w_up, cw, cb, w_down, tag):
    hn, up, act = saved
    t, d = h.shape
    tm, tn, blk, nb = min(FFN_TM, t), FFN_TN, FFN_BLK, FFN_NB
    tk = min(FFN_TM, t)
    da = _mmx(dh, w_down, dims=NT, grid=(t // tm, nb, 1), name=f"ffn{tag}_d_act", tile=(tm, blk),
              a_spec=pl.BlockSpec((tm, d), lambda i, j, k: (i, 0)),
              b_spec=pl.BlockSpec((blk, d), lambda i, j, k: (j, 0)),
              o_spec=pl.BlockSpec((None, tm, blk), lambda i, j, k: (j, i, 0)),
              out_shape=jax.ShapeDtypeStruct((nb, t, blk), F32))
    dwd = _mmx(act, dh, dims=TN, grid=(nb, d // tn, t // tk), name=f"ffn{tag}_d_wdown", tile=(blk, tn), split_rows=blk // 2,
               a_spec=pl.BlockSpec((None, tk, blk), lambda i, j, k: (i, k, 0)),
               b_spec=pl.BlockSpec((tk, tn), lambda i, j, k: (k, j)),
               o_spec=pl.BlockSpec((2, None, blk // 2, tn), lambda i, j, k: (0, i, 0, j)),
               out_shape=jax.ShapeDtypeStruct((2, nb, blk // 2, d), BF16))
    dup, dcw, dcb = _ffn_act_bwd(up, da, cw, cb, name=f"ffn{tag}_act_bwd")
    dhn = _mmx(dup, w_up, dims=NT, grid=(t // tm, d // tn, 2 * nb), name=f"ffn{tag}_d_hn", tile=(tm, tn),
               a_spec=pl.BlockSpec((None, None, tm, blk), lambda i, j, k: (k // nb, k % nb, i, 0)),
               b_spec=pl.BlockSpec((None, tn, blk), lambda i, j, k: (k, j, 0)),
               o_spec=pl.BlockSpec((tm, tn), lambda i, j, k: (i, j)),
               out_shape=jax.ShapeDtypeStruct((t, d), F32))
    dwu = _mmx(hn, dup, dims=TN, grid=(1, 2 * nb, t // tk), name=f"ffn{tag}_d_wup", tile=(d, blk),
               a_spec=pl.BlockSpec((tk, d), lambda i, j, k: (k, 0)),
               b_spec=pl.BlockSpec((None, None, tk, blk), lambda i, j, k: (j // nb, j % nb, k, 0)),
               o_spec=pl.BlockSpec((None, None, d, blk), lambda i, j, k: (j % 2, j // 2, 0, 0)),
               out_shape=jax.ShapeDtypeStruct((2, N_DEV // 2, d, blk), BF16))
    dh_in, dgain = _norm_bwd(h, gain, dhn, dh, name=f"ffn{tag}_norm_bwd")
    conv_w = dcw.transpose(2, 0, 1, 3).reshape(3, 2 * nb * blk)
    return dh_in, dict(w_up=dwu, w_down=dwd, conv_w=conv_w, conv_b=dcb.reshape(1, 2 * nb * blk), norm=dgain)


def kernel(x, norm_mix, norm_ffn, ret_gdn_w_in, gdn_conv_w, gdn_a_log, gdn_dt_bias, gdn_out_gain, ret_gdn_w_out, lru_w_in, lru_conv_w, lru_conv_b, lru_w_a, lru_b_a, lru_w_x, lru_b_x, lru_lambda, lru_w_out, ffn_w_up, ffn_conv_w, ffn_conv_b, ffn_w_down, norm_final, loss_target, m_norm_mix, m_norm_ffn, m_ret_gdn_w_in, m_gdn_conv_w, m_gdn_a_log, m_gdn_dt_bias, m_gdn_out_gain, m_ret_gdn_w_out, m_lru_w_in, m_lru_conv_w, m_lru_conv_b, m_lru_w_a, m_lru_b_a, m_lru_w_x, m_lru_b_x, m_lru_lambda, m_lru_w_out, m_ffn_w_up, m_ffn_conv_w, m_ffn_conv_b, m_ffn_w_down, m_norm_final, v_norm_mix, v_norm_ffn, v_ret_gdn_w_in, v_gdn_conv_w, v_gdn_a_log, v_gdn_dt_bias, v_gdn_out_gain, v_ret_gdn_w_out, v_lru_w_in, v_lru_conv_w, v_lru_conv_b, v_lru_w_a, v_lru_b_a, v_lru_w_x, v_lru_b_x, v_lru_lambda, v_lru_w_out, v_ffn_w_up, v_ffn_conv_w, v_ffn_conv_b, v_ffn_w_down, v_norm_final):
    given = dict(locals())
    w = {n: given[n] for n in WEIGHTS}
    me = 4 * lax.axis_index("x") + 2 * lax.axis_index("y") + lax.axis_index("c")
    t = x.shape[1]
    f = D_FF

    first_shards = {'w_in0': ret_gdn_w_in[0]}
    rest_shards = {'w_out0': ret_gdn_w_out[0], 'lru_in': lru_w_in[0], 'lru_out': lru_w_out[0], 'up0': ffn_w_up[0], 'up1': ffn_w_up[1],
                   'down0': ffn_w_down[0], 'down1': ffn_w_down[1]}
    small_shapes = [s for s, _ in SMALL.values()]
    small_buf = _pack([_to_rows(w[n]) for n in SMALL], _small_rows(small_shapes))
    *g_first, g_small = _Comm([_AllGather(a.astype(BF16)) for a in first_shards.values()] + [_AllGather(small_buf)]).run("gather_first")
    rest_gather = _Comm([_AllGather(a.astype(BF16)) for a in rest_shards.values()])
    got = dict(zip(first_shards, g_first))
    small_blocks = dict(zip(SMALL, _unpack(g_small, small_shapes, lead=(N_DEV,))))
    full = {n: _join_blocks(small_blocks[n], SMALL[n][1]) for n in SMALL if n != 'ffn_conv_w'}

    w_main, w_narrow = _join_w_in(got['w_in0'], name="join_w_in")
    fcw = [small_blocks['ffn_conv_w'][:, l].reshape(2, FFN_NB, 3, FFN_BLK) for l in range(2)]
    fcb = [ffn_conv_b[l].reshape(2, FFN_NB, 1, FFN_BLK) for l in range(2)]
    gdn_cw = full['gdn_conv_w'][0]
    al_pad = jnp.pad(gdn_a_log, ((0, 0), (HEADS, LANES - 2 * HEADS)))
    dt_pad = jnp.pad(gdn_dt_bias, ((0, 0), (HEADS, LANES - 2 * HEADS)))
    lru_cw, lru_cb = full['lru_conv_w'][0], full['lru_conv_b']
    lru_ba, lru_bx, lru_lam = full['lru_b_a'], full['lru_b_x'], full['lru_lambda']
    wa, wx = lru_w_a[0], lru_w_x[0]

    h0, target = x[0], loss_target[0]
    hn0 = _norm_fwd(h0, norm_mix[0:1], name="mix0_norm")
    proj = _mm(hn0, w_main, name="mix0_in")
    pnarrow = _mm(hn0, w_narrow, name="mix0_in_narrow")
    tables = _ret_tables(t)
    y0, ret_states = _ret_fwd(proj, tables, name="retention_fwd")
    y0, gdn_states, g_rest = _gdn_fwd(proj, pnarrow, gdn_cw, al_pad, dt_pad, gdn_out_gain, y0, rest_gather, name="deltanet_fwd")
    got.update(zip(rest_shards, g_rest))
    lru_in = _join_blocks(got['lru_in'], 1)
    lru_in_g, lru_in_x = lru_in[:, :D_MODEL], lru_in[:, D_MODEL:]
    lru_out = got['lru_out'].reshape(D_MODEL, D_MODEL)
    w_out0 = got['w_out0'].reshape(D_MODEL, D_MODEL)
    w_up = [got['up0'], got['up1']]
    down = [got['down0'].reshape(D_FF, D_MODEL), got['down1'].reshape(D_FF, D_MODEL)]
    h1 = _mm(y0, w_out0, res=h0, name="mix0_out")
    h2, ffn0_saved = _ffn_forward(h1, norm_ffn[0:1], w_up[0], fcw[0], fcb[0], down[0], 0)
    hn1 = _norm_fwd(h2, norm_mix[1:2], name="mix1_norm")
    gate = _mm(hn1, lru_in_g, name="mix1_in_gate")
    xpre = _mm(hn1, lru_in_x, name="mix1_in_x")
    y1, hs = _lru_fwd(gate, xpre, lru_cw, lru_cb, wa, lru_ba, wx, lru_bx, lru_lam, name="rglru_fwd")
    h3 = _mm(y1, lru_out, res=h2, name="mix1_out")
    h4, ffn1_saved = _ffn_forward(h3, norm_ffn[1:2], w_up[1], fcw[1], fcb[1], down[1], 1)
    dh4, d_norm_final, loss_part = _final_loss(h4, norm_final[None, :], target, name="final_norm_loss")
    loss = lax.psum(loss_part[0, 0], ("x", "y", "c"))

    dh3, gf1 = _ffn_backward(dh4, h3, norm_ffn[1:2], ffn1_saved, w_up[1], fcw[1], fcb[1], down[1], 1)
    dy1 = _mm(dh3, lru_out, tb=True, name="mix1_d_y")
    d_lru_out = _mm(y1, dh3, ta=True, out_dtype=BF16, name="mix1_d_wout")
    dgate, dxpre, d_lcw, d_lcb, d_wa, d_ba, d_wx, d_bx, d_lam = _lru_bwd(
        gate, xpre, hs, dy1, lru_cw, lru_cb, wa, lru_ba, wx, lru_bx, lru_lam, name="rglru_bwd")
    dhn1 = _mm(dgate, lru_in_g, tb=True, name="mix1_d_hn_gate")
    dhn1 = _mm(dxpre, lru_in_x, tb=True, res=dhn1, name="mix1_d_hn_x")
    d_lru_in = jnp.concatenate([_mm(hn1, dgate, ta=True, out_dtype=BF16, name="mix1_d_win_gate"),
                                _mm(hn1, dxpre, ta=True, out_dtype=BF16, name="mix1_d_win_x")], axis=1)
    dh2, d_mix1 = _norm_bwd(h2, norm_mix[1:2], dhn1, dh3, name="mix1_norm_bwd")
    dh1, gf0 = _ffn_backward(dh2, h1, norm_ffn[0:1], ffn0_saved, w_up[0], fcw[0], fcb[0], down[0], 0)
    dy0 = _mm(dh1, w_out0, tb=True, name="mix0_d_y")
    d_w_out0 = _mm(y0, dh1, ta=True, out_dtype=BF16, name="mix0_d_wout")

    def by_core_chip(full_grad, axis):
        blocks = _split_blocks(full_grad, axis)
        return blocks.reshape((4, 2) + blocks.shape[1:]).transpose(1, 0, 2, 3)

    def pair_add(blocks, theirs):
        return {k: _pair_add(b, o, name=f"pair_add_{k}", tr=row_tile(b.shape[2])) for (k, b), o in zip(blocks.items(), theirs)}

    row_tile = lambda rows: rows if rows <= 512 else 256

    rest_blocks = {'w_out0': by_core_chip(d_w_out0, 0), 'lru_in': by_core_chip(d_lru_in, 1), 'lru_out': by_core_chip(d_lru_out, 0),
                   'up0': gf0['w_up'], 'up1': gf1['w_up'], 'down0': gf0['w_down'], 'down1': gf1['w_down']}
    dproj, theirs = _ret_bwd(proj, tables, ret_states, dy0, _Comm([_PairSwap(b) for b in rest_blocks.values()]), name="retention_bwd")
    z_rest = pair_add(rest_blocks, theirs)
    early = {'lru_conv_w': d_lcw[None], 'lru_conv_b': d_lcb, 'lru_b_a': d_ba, 'lru_b_x': d_bx, 'lru_lambda': d_lam,
             'ffn_conv_w': jnp.stack([gf0['conv_w'], gf1['conv_w']]), 'norm_ffn': jnp.concatenate([gf0['norm'], gf1['norm']], axis=0),
             'norm_mix1': d_mix1, 'lru_w_a': d_wa[None], 'lru_w_x': d_wx[None],
             'ffn_conv_b': jnp.concatenate([gf0['conv_b'], gf1['conv_b']], axis=0), 'norm_final': d_norm_final[0]}
    early_buf = _pack([_to_rows(early[n]) for n in EARLY], _small_rows(list(EARLY.values())))
    rest_exchange = _Comm([_ChipExchange(z) for z in z_rest.values()] + [_AllGather(early_buf)])

    (dproj, dnarrow, d_gcw, d_alog, d_dtb, d_gain), (*w_rest, got_early) = _gdn_bwd(
        proj, pnarrow, gdn_cw, al_pad, dt_pad, gdn_out_gain, gdn_states, dy0, dproj, rest_exchange, name="deltanet_bwd")
    dhn0 = _mm(dproj, w_main, tb=True, name="mix0_d_hn")
    dhn0 = _mm(dnarrow, w_narrow, tb=True, res=dhn0, name="mix0_d_hn_narrow")
    d_w_main = _mm(hn0, dproj, ta=True, out_dtype=BF16, name="mix0_d_win")
    d_w_narrow = _mm(hn0, dnarrow, ta=True, out_dtype=BF16, name="mix0_d_win_narrow")
    dx, d_mix0 = _norm_bwd(h0, norm_mix[0:1], dhn0, dh1, name="mix0_norm_bwd")

    first_blocks = {'w_in0': _split_w_in(d_w_main, d_w_narrow, name="split_d_w_in")}
    z_first = pair_add(first_blocks, _Comm([_PairSwap(b) for b in first_blocks.values()]).run("pair_swap_first"))
    late = {'gdn_conv_w': d_gcw[None], 'norm_mix0': d_mix0, 'gdn_a_log': d_alog[:, HEADS:2 * HEADS],
            'gdn_dt_bias': d_dtb[:, HEADS:2 * HEADS], 'gdn_out_gain': d_gain}
    late_buf = _pack([_to_rows(late[n]) for n in LATE], _small_rows(list(LATE.values())))
    *w_first, got_late = _Comm([_ChipExchange(z) for z in z_first.values()] + [_AllGather(late_buf)]).run("exchange_first")

    summed = {k: _sum_slots(blocks, name=f"sum_blocks_{k}", tr=row_tile(blocks.shape[1]))
              for k, blocks in list(zip(z_rest, w_rest)) + list(zip(z_first, w_first))}
    grads = {'ret_gdn_w_in': summed['w_in0'][None], 'ret_gdn_w_out': summed['w_out0'][None], 'lru_w_in': summed['lru_in'][None],
             'lru_w_out': summed['lru_out'][None], 'ffn_w_up': jnp.stack([summed['up0'], summed['up1']]),
             'ffn_w_down': jnp.stack([summed['down0'], summed['down1']])}
    partial = dict(zip(EARLY, _unpack(_sum_slots(got_early, name="sum_partials_early"), list(EARLY.values()))))
    partial.update(zip(LATE, _unpack(_sum_slots(got_late, name="sum_partials_late"), list(LATE.values()))))
    partial['norm_mix'] = jnp.concatenate([partial.pop('norm_mix0'), partial.pop('norm_mix1')], axis=0)
    for n, g_full in partial.items():
        if n in SMALL:
            shard, axis = SMALL[n]
            g_full = lax.dynamic_slice_in_dim(g_full, me * shard[axis], shard[axis], axis=axis)
        grads[n] = g_full

    delta, new_m, new_v = {}, {}, {}
    for n in WEIGHTS:
        delta[n], new_m[n], new_v[n] = _adam(w[n], grads[n], given["m_" + n], given["v_" + n], name=f"adamw_{n}")
    return (loss, dx[None], *[grads[n] for n in WEIGHTS], *[delta[n] for n in WEIGHTS],
            *[new_m[n] for n in WEIGHTS], *[new_v[n] for n in WEIGHTS])
```

```python
import functools
import math

import numpy as np
import jax
import jax.numpy as jnp
from jax import lax
from jax.experimental import pallas as pl
from jax.experimental.pallas import tpu as pltpu

F32 = jnp.float32
BF16 = jnp.bfloat16
HI = lax.Precision.HIGHEST
MESH = pl.DeviceIdType.MESH

N_DEV = 8
LANES = 128
SUBLANES = 8
EPS = 1e-6
D_MODEL = 1024
HEADS = 4
HEAD_DIM = 128
RET_CHUNK = 128
GDN_CHUNK = 64
ROPE_BASE = 10000.0
LRU_C = 8.0
D_FF = 2816
MAIN_IN = 4096
SMALL_IN = 8
QSCALE = HEAD_DIM ** -0.5

ADAM_LR, ADAM_B1, ADAM_B2, ADAM_EPS, ADAM_WD, ADAM_STEP = 0.001, 0.9, 0.999, 1e-08, 0.01, 10


def _cp(sem=None, vmem_mb=None):
    kw = {}
    if sem is not None:
        kw["dimension_semantics"] = sem
    if vmem_mb is not None:
        kw["vmem_limit_bytes"] = vmem_mb << 20
    return pltpu.CompilerParams(**kw)


def _rows(shape):
    return lax.broadcasted_iota(jnp.int32, shape, 0)


def _cols(shape):
    return lax.broadcasted_iota(jnp.int32, shape, 1)


def _shift_down(cur, prev8, s):
    if s == 0:
        return cur
    rc = pltpu.roll(cur, s, 0)
    rp = pltpu.roll(prev8, s, 0)
    top = jnp.where(_rows(prev8.shape) < s, rp, rc[:SUBLANES])
    return jnp.concatenate([top, rc[SUBLANES:]], axis=0)


def _shift_up(cur, next8, s):
    if s == 0:
        return cur
    tt = cur.shape[0]
    rc = pltpu.roll(cur, tt - s, 0)
    rn = pltpu.roll(next8, SUBLANES - s, 0)
    bot = jnp.where(_rows(next8.shape) >= SUBLANES - s, rn, rc[tt - SUBLANES:])
    return jnp.concatenate([rc[:tt - SUBLANES], bot], axis=0)


def _down_fill(x, d, fill):
    return jnp.where(_rows(x.shape) < d, fill, pltpu.roll(x, d, 0))


def _up_fill(x, d, fill):
    tt = x.shape[0]
    return jnp.where(_rows(x.shape) >= tt - d, fill, pltpu.roll(x, tt - d, 0))


def _sigmoid(x):
    return 1.0 / (1.0 + jnp.exp(-x))


def _softplus(x):
    return jnp.maximum(x, 0.0) + jnp.log(1.0 + jnp.exp(-jnp.abs(x)))


def _dot(a, b, dims=(((1,), (0,)), ((), ())), precision=None):
    return lax.dot_general(a, b, dims, preferred_element_type=F32, precision=precision)


NN = (((1,), (0,)), ((), ()))
NT = (((1,), (1,)), ((), ()))
TN = (((0,), (0,)), ((), ()))


def _bdot(a, b, dims=NN):
    return _dot(a.astype(BF16), b.astype(BF16), dims)


def _split(a):
    hi = a.astype(BF16)
    return hi, (a - hi.astype(F32)).astype(BF16)


def _dot3(a, b, dims=NN):
    ah, al = _split(a)
    bh, bl = _split(b)
    return _dot(ah, bh, dims) + (_dot(ah, bl, dims) + _dot(al, bh, dims))


def _tile(dim, target):
    if dim <= target:
        return dim
    best = None
    for c in range(LANES, target + 1, LANES):
        if dim % c == 0:
            best = c
    assert best is not None, (dim, target)
    return best


def _mm(a, b, *, name, ta=False, tb=False, out_dtype=F32, res=None, tm=2048, tn=512, tk=1024):
    m, k = (a.shape[1], a.shape[0]) if ta else a.shape
    n = b.shape[0] if tb else b.shape[1]
    tn, tk = _tile(n, tn), _tile(k, tk)
    tm = _tile(m, tm if max(tn, tk) <= 1024 else tm // 2)
    nk = k // tk
    dims = (((0 if ta else 1,), (1 if tb else 0,)), ((), ()))

    def body(*refs):
        a_ref, b_ref = refs[:2]
        r_ref = refs[2] if res is not None else None
        o_ref = refs[3] if res is not None else refs[2]
        acc = refs[-1]
        kk = pl.program_id(2)
        part = _bdot(a_ref[...], b_ref[...], dims)

        def finish(r):
            if res is not None:
                r = r + r_ref[...]
            o_ref[...] = r.astype(out_dtype)

        if nk == 1:
            finish(part)
            return

        @pl.when(kk == 0)
        def _():
            acc[...] = part

        @pl.when(jnp.logical_and(kk > 0, kk < nk - 1))
        def _():
            acc[...] += part

        @pl.when(kk == nk - 1)
        def _():
            finish(acc[...] + part)

    a_spec = pl.BlockSpec((tk, tm), lambda i, j, kk: (kk, i)) if ta else pl.BlockSpec((tm, tk), lambda i, j, kk: (i, kk))
    b_spec = pl.BlockSpec((tn, tk), lambda i, j, kk: (j, kk)) if tb else pl.BlockSpec((tk, tn), lambda i, j, kk: (kk, j))
    o_spec = pl.BlockSpec((tm, tn), lambda i, j, kk: (i, j))
    in_specs = [a_spec, b_spec] + ([o_spec] if res is not None else [])
    args = (a, b) + ((res,) if res is not None else ())
    return pl.pallas_call(
        body, grid=(m // tm, n // tn, nk), in_specs=in_specs, out_specs=o_spec,
        out_shape=jax.ShapeDtypeStruct((m, n), out_dtype),
        scratch_shapes=[pltpu.VMEM((tm, tn), F32)] if nk > 1 else [], name=name,
        compiler_params=_cp(("parallel", "parallel", "arbitrary"), 56),
    )(*args)


def _mmx(a, b, *, dims, grid, a_spec, b_spec, o_spec, out_shape, tile, name, res=None, split_rows=None):
    nk = grid[-1]

    def body(*refs):
        a_ref, b_ref = refs[:2]
        r_ref = refs[2] if res is not None else None
        o_ref = refs[3] if res is not None else refs[2]
        acc = refs[-1]
        part = _bdot(a_ref[...], b_ref[...], dims)

        def finish(r):
            if res is not None:
                r = r + r_ref[...]
            if split_rows is None:
                o_ref[...] = r.astype(o_ref.dtype)
            else:
                o_ref[0] = r[:split_rows].astype(o_ref.dtype)
                o_ref[1] = r[split_rows:].astype(o_ref.dtype)

        if nk == 1:
            finish(part)
            return
        kk = pl.program_id(len(grid) - 1)

        @pl.when(kk == 0)
        def _():
            acc[...] = part

        @pl.when(jnp.logical_and(kk > 0, kk < nk - 1))
        def _():
            acc[...] += part

        @pl.when(kk == nk - 1)
        def _():
            finish(acc[...] + part)

    args = (a, b) + ((res,) if res is not None else ())
    return pl.pallas_call(
        body, grid=grid, in_specs=[a_spec, b_spec] + ([o_spec] if res is not None else []), out_specs=o_spec,
        out_shape=out_shape, scratch_shapes=[pltpu.VMEM(tile, F32)] if nk > 1 else [], name=name,
        compiler_params=_cp(("parallel",) * (len(grid) - 1) + ("arbitrary",), 56),
    )(*args)


W_IN_BLK = 513
W_IN_TR = 256


def _join_w_in(blocks, *, name):
    _, d, _ = blocks.shape
    tr = W_IN_TR

    def body(x_ref, main_ref, narrow_ref):
        for m in range(MAIN_IN // LANES):
            lo = LANES * m
            dev, off = divmod(lo, W_IN_BLK)
            if off + LANES <= W_IN_BLK:
                main_ref[:, lo:lo + LANES] = x_ref[dev, :, off:off + LANES]
            else:
                main_ref[:, lo:lo + LANES] = jnp.concatenate(
                    [x_ref[dev, :, off:W_IN_BLK], x_ref[dev + 1, :, 0:LANES - (W_IN_BLK - off)]], axis=1)
        tail = x_ref[N_DEV - 1, :, W_IN_BLK - SMALL_IN:W_IN_BLK]
        narrow_ref[...] = jnp.concatenate([tail, jnp.zeros((tr, LANES - SMALL_IN), tail.dtype)], axis=1)

    return pl.pallas_call(
        body, grid=(d // tr,), in_specs=[pl.BlockSpec((N_DEV, tr, W_IN_BLK), lambda i: (0, i, 0))],
        out_specs=[pl.BlockSpec((tr, MAIN_IN), lambda i: (i, 0)), pl.BlockSpec((tr, LANES), lambda i: (i, 0))],
        out_shape=[jax.ShapeDtypeStruct((d, MAIN_IN), blocks.dtype), jax.ShapeDtypeStruct((d, LANES), blocks.dtype)],
        name=name, compiler_params=_cp(("parallel",), 48),
    )(blocks)


def _split_w_in(main, narrow, *, name):
    d = main.shape[0]
    tr = W_IN_TR

    def body(m_ref, n_ref, o_ref):
        for dev in range(N_DEV):
            lo = W_IN_BLK * dev
            if dev < N_DEV - 1:
                piece = m_ref[:, lo:lo + W_IN_BLK]
            else:
                piece = jnp.concatenate([m_ref[:, lo:MAIN_IN], n_ref[:, 0:SMALL_IN]], axis=1)
            o_ref[dev % 2, dev // 2] = piece

    return pl.pallas_call(
        body, grid=(d // tr,),
        in_specs=[pl.BlockSpec((tr, MAIN_IN), lambda i: (i, 0)), pl.BlockSpec((tr, LANES), lambda i: (i, 0))],
        out_specs=pl.BlockSpec((2, N_DEV // 2, tr, W_IN_BLK), lambda i: (0, 0, i, 0)),
        out_shape=jax.ShapeDtypeStruct((2, N_DEV // 2, d, W_IN_BLK), main.dtype),
        name=name, compiler_params=_cp(("parallel",), 48),
    )(main, narrow)


def _norm_fwd(h, gain, *, name, tt=256):
    t, d = h.shape
    tt = min(tt, t)

    def body(h_ref, g_ref, o_ref):
        x = h_ref[...]
        r = lax.rsqrt(jnp.mean(x * x, axis=-1, keepdims=True) + EPS)
        o_ref[...] = (x * r * g_ref[...]).astype(BF16)

    row = pl.BlockSpec((tt, d), lambda i: (i, 0))
    return pl.pallas_call(
        body, grid=(t // tt,), in_specs=[row, pl.BlockSpec((1, d), lambda i: (0, 0))], out_specs=row,
        out_shape=jax.ShapeDtypeStruct((t, d), BF16), name=name, compiler_params=_cp(("parallel",)),
    )(h, gain)


def _norm_bwd(h, gain, dhn, dres, *, name, tt=256):
    t, d = h.shape
    tt = min(tt, t)

    def body(h_ref, g_ref, dy_ref, dr_ref, dx_ref, dg_ref):
        x, dy = h_ref[...], dy_ref[...]
        r = lax.rsqrt(jnp.mean(x * x, axis=-1, keepdims=True) + EPS)
        xh = x * r

        @pl.when(pl.program_id(0) == 0)
        def _():
            dg_ref[...] = jnp.zeros_like(dg_ref)

        dg_ref[...] += jnp.sum(dy * xh, axis=0, keepdims=True)
        dxh = dy * g_ref[...]
        dx_ref[...] = dr_ref[...] + r * (dxh - xh * jnp.mean(dxh * xh, axis=-1, keepdims=True))

    row = pl.BlockSpec((tt, d), lambda i: (i, 0))
    vec = pl.BlockSpec((1, d), lambda i: (0, 0))
    return pl.pallas_call(
        body, grid=(t // tt,), in_specs=[row, vec, row, row], out_specs=[row, vec],
        out_shape=[jax.ShapeDtypeStruct((t, d), F32), jax.ShapeDtypeStruct((1, d), F32)],
        name=name, compiler_params=_cp(("arbitrary",)),
    )(h, gain, dhn, dres)


def _final_loss(h, gain, target, *, name, tt=256):
    t, d = h.shape
    tt = min(tt, t)

    def body(h_ref, g_ref, tg_ref, dx_ref, dg_ref, loss_ref):
        x = h_ref[...]
        r = lax.rsqrt(jnp.mean(x * x, axis=-1, keepdims=True) + EPS)
        xh = x * r
        err = xh * g_ref[...] - tg_ref[...]

        @pl.when(pl.program_id(0) == 0)
        def _():
            dg_ref[...] = jnp.zeros_like(dg_ref)
            loss_ref[...] = jnp.zeros_like(loss_ref)

        loss_ref[...] += 0.5 * jnp.sum(jnp.mean(err * err, axis=-1, keepdims=True), axis=0, keepdims=True)
        dy = err * (1.0 / d)
        dg_ref[...] += jnp.sum(dy * xh, axis=0, keepdims=True)
        dxh = dy * g_ref[...]
        dx_ref[...] = r * (dxh - xh * jnp.mean(dxh * xh, axis=-1, keepdims=True))

    row = pl.BlockSpec((tt, d), lambda i: (i, 0))
    vec = pl.BlockSpec((1, d), lambda i: (0, 0))
    return pl.pallas_call(
        body, grid=(t // tt,), in_specs=[row, vec, row],
        out_specs=[row, vec, pl.BlockSpec((1, 1), lambda i: (0, 0))],
        out_shape=[jax.ShapeDtypeStruct((t, d), F32), jax.ShapeDtypeStruct((1, d), F32), jax.ShapeDtypeStruct((1, 1), F32)],
        name=name, compiler_params=_cp(("arbitrary",)),
    )(h, gain, target)


FFN_BLK = 704
FFN_NB = 4
FFN_TT = 256


def _prev8(n, tt):
    return jnp.maximum(n * (tt // SUBLANES) - 1, 0)


def _ffn_conv(cur, prev8, w, b):
    s1 = _shift_down(cur, prev8, 1)
    s2 = _shift_down(cur, prev8, 2)
    return w[0:1] * s2 + w[1:2] * s1 + w[2:3] * cur + b, s1, s2


def _ffn_specs(t, tt, order):
    pair = lambda rows, row_index: pl.BlockSpec((2, None, rows, FFN_BLK), lambda j, n: (0, j, row_index(n), 0))
    return dict(cur=pair(tt, order), prev=pair(SUBLANES, lambda n: _prev8(order(n), tt)), w=pair(3, lambda n: 0), b=pair(1, lambda n: 0),
                one=pl.BlockSpec((None, tt, FFN_BLK), lambda j, n: (j, order(n), 0)))


def _ffn_act_fwd(up, cw, cb, comm, *, name):
    t = up.shape[2]
    tt = min(FFN_TT, t)
    nt = t // tt
    sp = _ffn_specs(t, tt, lambda n: n)
    hbm = pl.BlockSpec(memory_space=pl.ANY)

    def body(u_ref, p_ref, w_ref, b_ref, *rest):
        comm_in, (o_ref,), comm_out, _, comm_sems = comm.split(rest, n_out=1, n_scratch=0)
        j, n = pl.program_id(0), pl.program_id(1)

        @pl.when(jnp.logical_and(j == 0, n == 0))
        def _():
            comm.start(comm_in, comm_out, comm_sems)

        first = n == 0
        gate, _, _ = _ffn_conv(u_ref[0], jnp.where(first, 0.0, p_ref[0]), w_ref[0], b_ref[0])
        val, _, _ = _ffn_conv(u_ref[1], jnp.where(first, 0.0, p_ref[1]), w_ref[1], b_ref[1])
        o_ref[...] = (gate * _sigmoid(gate) * val).astype(BF16)

        @pl.when(jnp.logical_and(j == FFN_NB - 1, n == nt - 1))
        def _():
            comm.finish(comm_in, comm_out, comm_sems)

    outs = pl.pallas_call(
        body, grid=(FFN_NB, nt), in_specs=[sp["cur"], sp["prev"], sp["w"], sp["b"]] + [hbm] * len(comm.arrays),
        out_specs=[sp["one"]] + [hbm] * len(comm.out_shapes),
        out_shape=[jax.ShapeDtypeStruct((FFN_NB, t, FFN_BLK), BF16)] + comm.out_shapes, scratch_shapes=comm.scratch, name=name,
        compiler_params=_cp(("arbitrary", "arbitrary"), 48),
    )(up, up, cw, cb, *comm.arrays)
    return outs[0], outs[1:]


def _ffn_act_bwd(up, da, cw, cb, *, name):
    t = up.shape[2]
    tt = min(FFN_TT, t)
    nt = t // tt
    sp = _ffn_specs(t, tt, lambda n: nt - 1 - n)

    def body(u_ref, p_ref, da_ref, w_ref, b_ref, du_ref, dw_ref, db_ref, head):
        n = pl.program_id(1)
        tile0 = n == nt - 1

        @pl.when(n == 0)
        def _():
            for r in (head, dw_ref, db_ref):
                r[...] = jnp.zeros_like(r)

        convs = [_ffn_conv(u_ref[s], jnp.where(tile0, 0.0, p_ref[s]), w_ref[s], b_ref[s]) for s in range(2)]
        gate, val = convs[0][0], convs[1][0]
        d = da_ref[...]
        sg = _sigmoid(gate)
        dcs = (d * val * sg * (1.0 + gate * (1.0 - sg)), d * gate * sg)
        for s in range(2):
            dc, w, hd = dcs[s], w_ref[s], head[s]
            _, x1, x2 = convs[s]
            du_ref[s] = (w[2:3] * dc + w[1:2] * _shift_up(dc, hd, 1) + w[0:1] * _shift_up(dc, hd, 2)).astype(BF16)
            dw_ref[s, 0:1, :] += jnp.sum(dc * x2, axis=0, keepdims=True)
            dw_ref[s, 1:2, :] += jnp.sum(dc * x1, axis=0, keepdims=True)
            dw_ref[s, 2:3, :] += jnp.sum(dc * u_ref[s], axis=0, keepdims=True)
            db_ref[s] += jnp.sum(dc, axis=0, keepdims=True)
            head[s] = dc[:SUBLANES]

    return pl.pallas_call(
        body, grid=(FFN_NB, nt), in_specs=[sp["cur"], sp["prev"], sp["one"], sp["w"], sp["b"]],
        out_specs=[sp["cur"], sp["w"], sp["b"]],
        out_shape=[jax.ShapeDtypeStruct(up.shape, BF16), jax.ShapeDtypeStruct(cw.shape, F32), jax.ShapeDtypeStruct(cb.shape, F32)],
        scratch_shapes=[pltpu.VMEM((2, SUBLANES, FFN_BLK), F32)], name=name,
        compiler_params=_cp(("parallel", "arbitrary"), 48),
    )(up, up, da, cw, cb)


LRU_TT = 256
LRU_CT = 512
GELU_C = math.sqrt(2.0 / math.pi)
GELU_A = 0.044715


def _gelu(x):
    return 0.5 * x * (1.0 + jnp.tanh(GELU_C * (x + GELU_A * x * x * x)))


def _gelu_grad(x):
    th = jnp.tanh(GELU_C * (x + GELU_A * x * x * x))
    return 0.5 * (1.0 + th) + 0.5 * x * (1.0 - th * th) * GELU_C * (1.0 + 3.0 * GELU_A * x * x)


def _neg_expm1(x):
    poly = -x * (1.0 + x * (0.5 + x * (1.0 / 6 + x * (1.0 / 24 + x * (1.0 / 120)))))
    return jnp.where(x > -0.1, poly, 1.0 - jnp.exp(x))


def _conv4(x, p8, w, b=None):
    s1, s2, s3 = _shift_down(x, p8, 1), _shift_down(x, p8, 2), _shift_down(x, p8, 3)
    y = w[0:1] * s3 + w[1:2] * s2 + w[2:3] * s1 + w[3:4] * x
    return (y if b is None else y + b), (s1, s2, s3)


def _conv4_bwd(dy, head, x, shifts, w):
    s1, s2, s3 = shifts
    dx = w[3:4] * dy + w[2:3] * _shift_up(dy, head, 1) + w[1:2] * _shift_up(dy, head, 2) + w[0:1] * _shift_up(dy, head, 3)
    dws = [jnp.sum(dy * s, axis=0, keepdims=True) for s in (s3, s2, s1, x)]
    return dx, dws


def _blockdiag(x, w_ref, dims=NN):
    nb = x.shape[1] // LANES
    return jnp.concatenate([_bdot(x[:, LANES * i:LANES * (i + 1)], w_ref[i], dims) for i in range(nb)], axis=1)


def _lru_gates(xr, wa_ref, wx_ref, ba, bx, lam):
    r = _sigmoid(_blockdiag(xr, wa_ref) + ba)
    i = _sigmoid(_blockdiag(xr, wx_ref) + bx)
    sp = _softplus(-lam)
    la = -LRU_C * r * sp
    a = jnp.exp(la)
    mult = jnp.sqrt(_neg_expm1(2.0 * la))
    return r, i, sp, a, mult


def _lru_specs(t, tt, ct, order):
    nb = ct // LANES
    cur = pl.BlockSpec((tt, ct), lambda j, n: (order(n), j))
    prev = pl.BlockSpec((SUBLANES, ct), lambda j, n: (_prev8(order(n), tt), j))
    vec = lambda rows: pl.BlockSpec((rows, ct), lambda j, n: (0, j))
    blk = pl.BlockSpec((nb, LANES, LANES), lambda j, n: (j, 0, 0))
    return cur, prev, vec, blk


def _lru_fwd(gate, xpre, cw, cb, wa, ba, wx, bx, lam, comm, *, name):
    t, c = gate.shape
    tt, ct = min(LRU_TT, t), LRU_CT
    nj, nt = c // ct, t // tt
    cur, prev, vec, blk = _lru_specs(t, tt, ct, lambda n: n)
    hbm = pl.BlockSpec(memory_space=pl.ANY)

    def body(gate_ref, x_ref, p_ref, cw_ref, cb_ref, wa_ref, ba_ref, wx_ref, bx_ref, lam_ref, *rest):
        comm_in, (y_ref, hs_ref), comm_out, (carry,), comm_sems = comm.split(rest, n_out=2, n_scratch=1)
        j, n = pl.program_id(0), pl.program_id(1)

        @pl.when(jnp.logical_and(j == 0, n == 0))
        def _():
            comm.start(comm_in, comm_out, comm_sems)

        @pl.when(n == 0)
        def _():
            carry[...] = jnp.zeros_like(carry)

        p8 = jnp.where(n == 0, 0.0, p_ref[...])
        xr, _ = _conv4(x_ref[...], p8, cw_ref[...], cb_ref[...])
        r, i, sp, a, mult = _lru_gates(xr, wa_ref, wx_ref, ba_ref[...], bx_ref[...], lam_ref[...])
        acc_a, acc_b = a, mult * (i * xr)
        d = 1
        while d < tt:
            acc_b = acc_a * _down_fill(acc_b, d, 0.0) + acc_b
            acc_a = acc_a * _down_fill(acc_a, d, 1.0)
            d *= 2
        hs = acc_b + acc_a * carry[0:1]
        carry[...] = jnp.broadcast_to(hs[tt - 1:tt], carry.shape)
        hs_ref[...] = hs
        y_ref[...] = (_gelu(gate_ref[...]) * hs).astype(BF16)

        @pl.when(jnp.logical_and(j == nj - 1, n == nt - 1))
        def _():
            comm.finish(comm_in, comm_out, comm_sems)

    outs = pl.pallas_call(
        body, grid=(nj, nt),
        in_specs=[cur, cur, prev, vec(4), vec(1), blk, vec(1), blk, vec(1), vec(1)] + [hbm] * len(comm.arrays),
        out_specs=[cur, cur] + [hbm] * len(comm.out_shapes),
        out_shape=[jax.ShapeDtypeStruct((t, c), BF16), jax.ShapeDtypeStruct((t, c), F32)] + comm.out_shapes,
        scratch_shapes=[pltpu.VMEM((SUBLANES, ct), F32)] + comm.scratch, name=name,
        compiler_params=_cp(("arbitrary", "arbitrary"), 48),
    )(gate, xpre, xpre, cw, cb, wa, ba, wx, bx, lam, *comm.arrays)
    return outs[0], outs[1], outs[2:]


def _lru_bwd(gate, xpre, hs, dy, cw, cb, wa, ba, wx, bx, lam, *, name):
    t, c = gate.shape
    tt, ct = min(LRU_TT, t), LRU_CT
    nt = t // tt
    cur, prev, vec, blk = _lru_specs(t, tt, ct, lambda n: nt - 1 - n)

    def body(gate_ref, x_ref, p_ref, hs_ref, phs_ref, dy_ref, cw_ref, cb_ref, wa_ref, ba_ref, wx_ref, bx_ref, lam_ref,
             dgate_ref, dx_ref, dcw_ref, dcb_ref, dwa_ref, dba_ref, dwx_ref, dbx_ref, dlam_ref, carry, head):
        n = pl.program_id(1)
        tile0 = n == nt - 1

        @pl.when(n == 0)
        def _():
            for ref in (carry, head, dcw_ref, dcb_ref, dwa_ref, dba_ref, dwx_ref, dbx_ref, dlam_ref):
                ref[...] = jnp.zeros_like(ref)

        xp, cwv, lam = x_ref[...], cw_ref[...], lam_ref[...]
        p8 = jnp.where(tile0, 0.0, p_ref[...])
        xr, shifts = _conv4(xp, p8, cwv, cb_ref[...])
        r, i, sp, a, mult = _lru_gates(xr, wa_ref, wx_ref, ba_ref[...], bx_ref[...], lam)
        gate, hsv, dyv = gate_ref[...], hs_ref[...], dy_ref[...]
        dgate_ref[...] = (dyv * hsv * _gelu_grad(gate)).astype(BF16)
        acc_b = dyv * _gelu(gate) + jnp.where(_rows(a.shape) == tt - 1, carry[0:1], 0.0)
        acc_a = _up_fill(a, 1, 0.0)
        d = 1
        while d < tt:
            acc_b = acc_b + acc_a * _up_fill(acc_b, d, 0.0)
            acc_a = acc_a * _up_fill(acc_a, d, 0.0)
            d *= 2
        gsum = acc_b
        carry[...] = jnp.broadcast_to(a[0:1] * gsum[0:1], carry.shape)
        hprev = _shift_down(hsv, jnp.where(tile0, 0.0, phs_ref[...]), 1)
        da = gsum * hprev
        dmult = gsum * i * xr
        di = gsum * mult * xr
        dxr = gsum * mult * i
        dla = da * a - dmult * (a * a) / mult
        dr = dla * (-LRU_C * sp)
        dlam_ref[...] += jnp.sum(dla * (-LRU_C * r), axis=0, keepdims=True) * (-_sigmoid(-lam))
        dpa = dr * r * (1.0 - r)
        dpx = di * i * (1.0 - i)
        dba_ref[...] += jnp.sum(dpa, axis=0, keepdims=True)
        dbx_ref[...] += jnp.sum(dpx, axis=0, keepdims=True)
        dxr = dxr + _blockdiag(dpa, wa_ref, NT) + _blockdiag(dpx, wx_ref, NT)
        for b in range(ct // LANES):
            sl = slice(LANES * b, LANES * (b + 1))
            dwa_ref[b] += _bdot(xr[:, sl], dpa[:, sl], TN)
            dwx_ref[b] += _bdot(xr[:, sl], dpx[:, sl], TN)
        dx, dws = _conv4_bwd(dxr, head[...], xp, shifts, cwv)
        dx_ref[...] = dx.astype(BF16)
        for k in range(4):
            dcw_ref[k:k + 1, :] += dws[k]
        dcb_ref[...] += jnp.sum(dxr, axis=0, keepdims=True)
        head[...] = dxr[:SUBLANES]

    return pl.pallas_call(
        body, grid=(c // ct, nt),
        in_specs=[cur, cur, prev, cur, prev, cur, vec(4), vec(1), blk, vec(1), blk, vec(1), vec(1)],
        out_specs=[cur, cur, vec(4), vec(1), blk, vec(1), blk, vec(1), vec(1)],
        out_shape=[jax.ShapeDtypeStruct((t, c), BF16)] * 2 + [jax.ShapeDtypeStruct((4, c), F32), jax.ShapeDtypeStruct((1, c), F32),
                   jax.ShapeDtypeStruct(wa.shape, F32), jax.ShapeDtypeStruct((1, c), F32),
                   jax.ShapeDtypeStruct(wx.shape, F32), jax.ShapeDtypeStruct((1, c), F32), jax.ShapeDtypeStruct((1, c), F32)],
        scratch_shapes=[pltpu.VMEM((SUBLANES, ct), F32)] * 2, name=name,
        compiler_params=_cp(("parallel", "arbitrary"), 48),
    )(gate, xpre, xpre, hs, hs, dy, cw, cb, wa, ba, wx, bx, lam)


RET_W = HEADS * HEAD_DIM
HALF = HEAD_DIM // 2


def _ret_tables(t):
    c = RET_CHUNK
    inv_freq = ROPE_BASE ** (-jnp.arange(HALF, dtype=F32) / HALF)
    ang = jnp.arange(t, dtype=jnp.int32).astype(F32)[:, None] * inv_freq[None, :]
    cos, sin = jnp.cos(ang), jnp.sin(ang)
    cosf = jnp.concatenate([cos, cos], axis=1)
    sinf = jnp.concatenate([-sin, sin], axis=1)
    log_gamma = jnp.log1p(-jnp.exp2(-5.0 - jnp.arange(HEADS, dtype=F32)))
    idx = jnp.arange(c, dtype=F32)
    rel = idx[:, None] - idx[None, :]
    causal = rel >= 0
    dmask = jnp.where(causal, jnp.exp(log_gamma[:, None, None] * jnp.where(causal, rel, 0.0)), 0.0)
    ktail = jnp.exp(log_gamma[:, None] * (c - 1 - idx))
    qdec = jnp.exp(log_gamma[:, None] * (idx + 1.0))
    rowtab = jnp.broadcast_to(jnp.stack([ktail, qdec], axis=1)[..., None], (HEADS, 2, c, HEAD_DIM))
    cdec = jnp.broadcast_to(jnp.exp(log_gamma * c)[:, None, None], (HEADS, SUBLANES, HEAD_DIM))
    return cosf, sinf, dmask, rowtab, cdec


def _rotary(x, cosf, sinf):
    return x * cosf + pltpu.roll(x, HALF, 1) * sinf


def _rotary_t(dx, cosf, sinf):
    return dx * cosf + pltpu.roll(dx * sinf, HALF, 1)


def _ret_specs(c, order):
    full = lambda shape: pl.BlockSpec(shape, lambda n: (0,) * len(shape))
    return dict(
        proj=pl.BlockSpec((c, 4 * RET_W), lambda n: (order(n), 0)),
        rot=pl.BlockSpec((c, HEAD_DIM), lambda n: (order(n), 0)),
        dmask=full((HEADS, c, c)), rowtab=full((HEADS, 2, c, HEAD_DIM)), cdec=full((HEADS, SUBLANES, HEAD_DIM)),
        state=pl.BlockSpec((1, HEADS, HEAD_DIM, HEAD_DIM), lambda n: (order(n), 0, 0, 0)),
        half=pl.BlockSpec((c, RET_W), lambda n: (order(n), 0)),
    )


def _ret_head(p_ref, h, cosf, sinf):
    sl = lambda j: slice(j * RET_W + h * HEAD_DIM, j * RET_W + (h + 1) * HEAD_DIM)
    q, k, v, g = p_ref[:, sl(0)], p_ref[:, sl(1)], p_ref[:, sl(2)], p_ref[:, sl(3)]
    return _rotary(q, cosf, sinf), _rotary(k, cosf, sinf) * QSCALE, v, g


def _ret_fwd(proj, tables, comm, *, name):
    t = proj.shape[0]
    c = RET_CHUNK
    nc = t // c
    sp = _ret_specs(c, lambda n: n)
    hbm = pl.BlockSpec(memory_space=pl.ANY)

    def body(p_ref, cos_ref, sin_ref, dm_ref, rt_ref, cd_ref, *rest):
        comm_in, (y_ref, s_ref), comm_out, (state,), comm_sems = comm.split(rest, n_out=2, n_scratch=1)

        @pl.when(pl.program_id(0) == 0)
        def _():
            state[...] = jnp.zeros_like(state)
            comm.start(comm_in, comm_out, comm_sems)

        cosf, sinf = cos_ref[...], sin_ref[...]
        for h in range(HEADS):
            qr, kr, v, g = _ret_head(p_ref, h, cosf, sinf)
            s0 = state[h]
            s_ref[0, h] = s0
            scores = _bdot(qr, kr, NT) * dm_ref[h]
            o = _bdot(scores, v) + _bdot(qr * rt_ref[h, 1], s0)
            state[h] = s0 * cd_ref[h][0:1] + _bdot(kr * rt_ref[h, 0], v, TN)
            rinv = lax.rsqrt(jnp.mean(o * o, axis=-1, keepdims=True) + EPS)
            y_ref[:, h * HEAD_DIM:(h + 1) * HEAD_DIM] = (o * rinv * (g * _sigmoid(g))).astype(BF16)

        @pl.when(pl.program_id(0) == nc - 1)
        def _():
            comm.finish(comm_in, comm_out, comm_sems)

    outs = pl.pallas_call(
        body, grid=(nc,),
        in_specs=[sp["proj"], sp["rot"], sp["rot"], sp["dmask"], sp["rowtab"], sp["cdec"]] + [hbm] * len(comm.arrays),
        out_specs=[sp["half"], sp["state"]] + [hbm] * len(comm.out_shapes),
        out_shape=[jax.ShapeDtypeStruct((t, 2 * RET_W), BF16), jax.ShapeDtypeStruct((nc, HEADS, HEAD_DIM, HEAD_DIM), F32)]
        + comm.out_shapes,
        scratch_shapes=[pltpu.VMEM((HEADS, HEAD_DIM, HEAD_DIM), F32)] + comm.scratch, name=name,
        compiler_params=_cp(("arbitrary",), 48),
    )(proj, *tables, *comm.arrays)
    return outs[0], outs[1], outs[2:]


def _ret_bwd(proj, tables, states, dy, comm, *, name):
    t = proj.shape[0]
    c = RET_CHUNK
    nc = t // c
    sp = _ret_specs(c, lambda n: nc - 1 - n)
    hbm = pl.BlockSpec(memory_space=pl.ANY)

    def body(p_ref, cos_ref, sin_ref, dm_ref, rt_ref, cd_ref, s_ref, dy_ref, *rest):
        comm_in, (dp_ref,), comm_out, (dstate,), comm_sems = comm.split(rest, n_out=1, n_scratch=1)

        @pl.when(pl.program_id(0) == 0)
        def _():
            dstate[...] = jnp.zeros_like(dstate)
            comm.start(comm_in, comm_out, comm_sems)

        cosf, sinf = cos_ref[...], sin_ref[...]
        for h in range(HEADS):
            qr, kr, v, g = _ret_head(p_ref, h, cosf, sinf)
            s0, dm, ktl, qdc = s_ref[0, h], dm_ref[h], rt_ref[h, 0], rt_ref[h, 1]
            scores = _bdot(qr, kr, NT) * dm
            qd, kt = qr * qdc, kr * ktl
            o = _bdot(scores, v) + _bdot(qd, s0)
            rinv = lax.rsqrt(jnp.mean(o * o, axis=-1, keepdims=True) + EPS)
            oh = o * rinv
            sg = _sigmoid(g)
            dyh = dy_ref[:, h * HEAD_DIM:(h + 1) * HEAD_DIM]
            dg = dyh * oh * sg * (1.0 + g * (1.0 - sg))
            dyo = dyh * (g * sg)
            do = rinv * (dyo - oh * jnp.mean(dyo * oh, axis=-1, keepdims=True))
            ds1 = dstate[h]
            dsc = _bdot(do, v, NT) * dm
            dv = _bdot(scores, do, TN) + _bdot(kt, ds1)
            dqr = _bdot(dsc, kr) + _bdot(do, s0, NT) * qdc
            dkr = (_bdot(dsc, qr, TN) + _bdot(v, ds1, NT) * ktl) * QSCALE
            dstate[h] = ds1 * cd_ref[h][0:1] + _bdot(qd, do, TN)
            pieces = (_rotary_t(dqr, cosf, sinf), _rotary_t(dkr, cosf, sinf), dv, dg)
            for j, piece in enumerate(pieces):
                dp_ref[:, j * RET_W + h * HEAD_DIM:j * RET_W + (h + 1) * HEAD_DIM] = piece.astype(BF16)

        @pl.when(pl.program_id(0) == nc - 1)
        def _():
            comm.finish(comm_in, comm_out, comm_sems)

    outs = pl.pallas_call(
        body, grid=(nc,),
        in_specs=[sp["proj"], sp["rot"], sp["rot"], sp["dmask"], sp["rowtab"], sp["cdec"], sp["state"], sp["half"]]
        + [hbm] * len(comm.arrays),
        out_specs=[sp["proj"]] + [hbm] * len(comm.out_shapes),
        out_shape=[jax.ShapeDtypeStruct((t, 8 * RET_W), BF16)] + comm.out_shapes,
        scratch_shapes=[pltpu.VMEM((HEADS, HEAD_DIM, HEAD_DIM), F32)] + comm.scratch, name=name,
        compiler_params=_cp(("arbitrary",), 48),
    )(proj, *tables, states, dy, *comm.arrays)
    return outs[0], outs[1:]


GDN_W = HEADS * HEAD_DIM
GDN_CONV = 3 * GDN_W
NEUMANN_STEPS = 5


def _gdn_gates(ps, al, dt):
    return _sigmoid(ps), -jnp.exp(al) * _softplus(ps + dt)


def _cumsum_rows(x):
    d = 1
    while d < x.shape[0]:
        x = x + _down_fill(x, d, 0.0)
        d *= 2
    return x


def _rev_cumsum_rows(x):
    d = 1
    while d < x.shape[0]:
        x = x + _up_fill(x, d, 0.0)
        d *= 2
    return x


class _Chunk:
    pass


def _gdn_chunk(qc, kc, v, beta, g, s0, inv=None):
    c = GDN_CHUNK
    z = _Chunk()
    z.rq = lax.rsqrt(jnp.sum(qc * qc, axis=-1, keepdims=True) + EPS)
    z.rk = lax.rsqrt(jnp.sum(kc * kc, axis=-1, keepdims=True) + EPS)
    z.qn, z.k = qc * z.rq, kc * z.rk
    z.q = z.qn * QSCALE
    z.v, z.beta = v, beta
    gc = _cumsum_rows(jnp.broadcast_to(g, (c, LANES)))
    ri, ci = _rows((c, c)), _cols((c, c))
    z.tril, z.strict = ri >= ci, ri > ci
    diff = gc[:, :c] - gc.T[:c, :]
    z.decay = jnp.where(z.tril, jnp.exp(jnp.where(z.tril, diff, 0.0)), 0.0)
    z.eg = jnp.exp(gc)
    glast = gc[c - 1:c, :]
    z.egl = jnp.exp(glast - gc)
    z.cd = jnp.exp(glast)
    z.kb = z.k * beta
    both = _bdot(jnp.concatenate([z.kb, z.q], axis=0), z.k, NT)
    z.m, z.qk = both[:c], both[c:]
    if inv is None:
        neg = -jnp.where(z.strict, z.m * z.decay, 0.0)
        inv = (ri == ci).astype(F32) + neg
        pw = neg
        for _ in range(NEUMANN_STEPS):
            pw = _dot3(pw, pw)
            inv = inv + _dot3(inv, pw)
    z.inv = inv
    z.vb, z.kbg = v * beta, z.kb * z.eg
    solved = _dot3(inv, jnp.concatenate([z.vb, z.kbg], axis=1))
    z.u, z.w = solved[:, :HEAD_DIM], solved[:, HEAD_DIM:]
    z.attn = jnp.where(z.tril, z.qk * z.decay, 0.0)
    z.qd, z.kt = z.q * z.eg, z.k * z.egl
    through = _bdot(jnp.concatenate([z.w, z.qd], axis=0), s0)
    z.vnew = z.u - through[:c]
    z.o = through[c:] + _bdot(z.attn, z.vnew)
    z.s1 = s0 * z.cd + _bdot(z.kt, z.vnew, TN)
    return z


def _gdn_chunk_bwd(z, s0, do, ds1):
    c = GDN_CHUNK
    dvnew = _bdot(z.attn, do, TN) + _bdot(z.kt, ds1)
    against = _bdot(do, jnp.concatenate([s0, z.vnew], axis=0), NT)
    dqd = against[:, :HEAD_DIM]
    dattn = jnp.where(z.tril, against[:, HEAD_DIM:], 0.0)
    ds0 = ds1 * z.cd + _bdot(jnp.concatenate([z.qd, -z.w], axis=0), jnp.concatenate([do, dvnew], axis=0), TN)
    dcd = jnp.sum(jnp.sum(s0 * ds1, axis=1, keepdims=True), axis=0, keepdims=True)
    dkt = _bdot(z.vnew, ds1, NT)
    dw = -_bdot(dvnew, s0, NT)
    dsolved = _dot3(z.inv, jnp.concatenate([dvnew, dw], axis=1), TN)
    dvb, dkbg = dsolved[:, :HEAD_DIM], dsolved[:, HEAD_DIM:]
    dl = jnp.where(z.strict, -_bdot(dsolved, jnp.concatenate([z.u, z.w], axis=1), NT), 0.0)
    dml = dl * z.decay
    dqk = dattn * z.decay
    ddecay = (dl * z.m + dattn * z.qk) * z.decay
    stacked = jnp.concatenate([dqk, dml], axis=0)
    onto_k = _bdot(stacked, z.k)
    dq = onto_k[:c] + dqd * z.eg
    dkb = onto_k[c:] + dkbg * z.eg
    dk = _bdot(stacked, jnp.concatenate([z.q, z.kb], axis=0), TN) + dkt * z.egl + dkb * z.beta
    dbeta = jnp.sum(dkb * z.k, axis=-1, keepdims=True) + jnp.sum(dvb * z.v, axis=-1, keepdims=True)
    dv = dvb * z.beta
    colsum = _dot3(ddecay, jnp.ones((c, LANES), F32), TN)
    e = jnp.sum(dkt * z.kt, axis=-1, keepdims=True)
    dgc = (jnp.sum(ddecay, axis=-1, keepdims=True) - colsum
           + jnp.sum(dkbg * z.kbg, axis=-1, keepdims=True) + jnp.sum(dqd * z.qd, axis=-1, keepdims=True) - e)
    dglast = jnp.sum(e, axis=0, keepdims=True) + dcd * z.cd
    dgc = dgc + jnp.where(_rows((c, LANES)) == c - 1, dglast, 0.0)
    dg = _rev_cumsum_rows(dgc)[:, 0:1]
    dqn = dq * QSCALE
    dqc = z.rq * (dqn - z.qn * jnp.sum(dqn * z.qn, axis=-1, keepdims=True))
    dkc = z.rk * (dk - z.k * jnp.sum(dk * z.k, axis=-1, keepdims=True))
    return dqc, dkc, dv, dbeta, dg, ds0


GDN_SUB = 1


def _gdn_specs(c, order):
    full = lambda shape: pl.BlockSpec(shape, lambda n: (0,) * len(shape))
    return dict(
        proj=pl.BlockSpec((c, 4 * GDN_W), lambda n: (order(n), 1)),
        prev=pl.BlockSpec((SUBLANES, 4 * GDN_W), lambda n: (_prev8(order(n), c), 1)),
        small=pl.BlockSpec((c, LANES), lambda n: (order(n), 0)),
        convw=full((4, GDN_CONV)), vec=full((1, LANES)),
        state=pl.BlockSpec((GDN_SUB, HEADS, HEAD_DIM, HEAD_DIM), lambda n: (order(n), 0, 0, 0)),
        inv=pl.BlockSpec((GDN_SUB, HEADS, GDN_CHUNK, GDN_CHUNK), lambda n: (order(n), 0, 0, 0)),
        half=pl.BlockSpec((c, GDN_W), lambda n: (order(n), 1)),
        any=pl.BlockSpec(memory_space=pl.ANY),
    )


def _gdn_fwd(proj, psmall, conv_w, al, dt, gain, y_in, comm, *, name):
    t = proj.shape[0]
    c = GDN_CHUNK
    nc, ns = t // c, t // (c * GDN_SUB)
    sp = _gdn_specs(c * GDN_SUB, lambda n: n)

    def body(p_ref, prev_ref, ps_ref, cw_ref, al_ref, dt_ref, gain_ref, yin_ref, *rest):
        comm_in, (y_ref, s_ref, inv_ref), comm_out, (state,), comm_sems = comm.split(rest, n_out=3, n_scratch=1)
        n = pl.program_id(0)

        @pl.when(n == 0)
        def _():
            state[...] = jnp.zeros_like(state)
            comm.start(comm_in, comm_out, comm_sems)

        p8 = jnp.where(n == 0, 0.0, prev_ref[:, :GDN_CONV])
        pre, _ = _conv4(p_ref[:, :GDN_CONV], p8, cw_ref[...])
        act = pre * _sigmoid(pre)
        beta_all, g_all = _gdn_gates(ps_ref[...], al_ref[...], dt_ref[...])
        gd_all, gain = p_ref[:, GDN_CONV:], gain_ref[...]
        swish = gd_all * _sigmoid(gd_all)
        cur = [state[h] for h in range(HEADS)]
        starts, ys = [], []
        for sub in range(GDN_SUB):
            rows = slice(sub * c, (sub + 1) * c)
            starts.append(list(cur))
            pieces = []
            for h in range(HEADS):
                sl = lambda j: slice(j * GDN_W + h * HEAD_DIM, j * GDN_W + (h + 1) * HEAD_DIM)
                z = _gdn_chunk(act[rows, sl(0)], act[rows, sl(1)], act[rows, sl(2)], beta_all[rows, h:h + 1],
                               g_all[rows, HEADS + h:HEADS + h + 1], cur[h])
                cur[h] = z.s1
                inv_ref[sub, h] = z.inv
                rinv = lax.rsqrt(jnp.mean(z.o * z.o, axis=-1, keepdims=True) + EPS)
                pieces.append(z.o * rinv * gain * swish[rows, sl(0)])
            ys.append(jnp.concatenate(pieces, axis=1))
        y_ref[...] = jnp.concatenate(ys, axis=0).astype(BF16)
        for sub in range(GDN_SUB):
            for h in range(HEADS):
                s_ref[sub, h] = starts[sub][h]
        for h in range(HEADS):
            state[h] = cur[h]

        @pl.when(n == ns - 1)
        def _():
            comm.finish(comm_in, comm_out, comm_sems)

    outs = pl.pallas_call(
        body, grid=(ns,),
        in_specs=[sp["proj"], sp["prev"], sp["small"], sp["convw"], sp["vec"], sp["vec"], sp["vec"], sp["any"]]
        + [sp["any"]] * len(comm.arrays),
        out_specs=[sp["half"], sp["state"], sp["inv"]] + [sp["any"]] * len(comm.out_shapes),
        out_shape=[jax.ShapeDtypeStruct((t, 2 * GDN_W), BF16), jax.ShapeDtypeStruct((nc, HEADS, HEAD_DIM, HEAD_DIM), F32),
                   jax.ShapeDtypeStruct((nc, HEADS, c, c), F32)] + comm.out_shapes,
        scratch_shapes=[pltpu.VMEM((HEADS, HEAD_DIM, HEAD_DIM), F32)] + comm.scratch, name=name,
        input_output_aliases={7: 0}, compiler_params=_cp(("arbitrary",), 48),
    )(proj, proj, psmall, conv_w, al, dt, gain, y_in, *comm.arrays)
    return outs[0], (outs[1], outs[2]), outs[3:]


def _gdn_bwd(proj, psmall, conv_w, al, dt, gain, states, dy, dproj_in, comm, *, name):
    t = proj.shape[0]
    c = GDN_CHUNK
    nc, ns = t // c, t // (c * GDN_SUB)
    sp = _gdn_specs(c * GDN_SUB, lambda n: ns - 1 - n)

    def body(p_ref, prev_ref, ps_ref, cw_ref, al_ref, dt_ref, gain_ref, s_ref, inv_ref, dy_ref, dpin_ref, *rest):
        comm_in, outs, comm_out, (dstate, head), comm_sems = comm.split(rest, n_out=6, n_scratch=2)
        dp_ref, dps_ref, dcw_ref, dal_ref, ddt_ref, dgain_ref = outs
        n = pl.program_id(0)
        first_rows = n == ns - 1

        @pl.when(n == 0)
        def _():
            for ref in (dstate, head, dcw_ref, dal_ref, ddt_ref, dgain_ref):
                ref[...] = jnp.zeros_like(ref)
            comm.start(comm_in, comm_out, comm_sems)

        x, cwv = p_ref[:, :GDN_CONV], cw_ref[...]
        p8 = jnp.where(first_rows, 0.0, prev_ref[:, :GDN_CONV])
        pre, shifts = _conv4(x, p8, cwv)
        sg_pre = _sigmoid(pre)
        act = pre * sg_pre
        ps, alv, dtv, gain = ps_ref[...], al_ref[...], dt_ref[...], gain_ref[...]
        beta_all, g_all = _gdn_gates(ps, alv, dtv)
        lane = _cols((c, LANES))
        dgain = jnp.zeros((1, LANES), F32)
        dcur = [dstate[h] for h in range(HEADS)]
        dact_rows, dbeta_rows, dg_rows = [None] * GDN_SUB, [None] * GDN_SUB, [None] * GDN_SUB
        for sub in reversed(range(GDN_SUB)):
            rows = slice(sub * c, (sub + 1) * c)
            dbeta_all = jnp.zeros((c, LANES), F32)
            dg_all = jnp.zeros((c, LANES), F32)
            dact = [None] * (3 * HEADS)
            for h in range(HEADS):
                sl = lambda j: slice(j * GDN_W + h * HEAD_DIM, j * GDN_W + (h + 1) * HEAD_DIM)
                s0 = s_ref[sub, h]
                z = _gdn_chunk(act[rows, sl(0)], act[rows, sl(1)], act[rows, sl(2)], beta_all[rows, h:h + 1],
                               g_all[rows, HEADS + h:HEADS + h + 1], s0, inv=inv_ref[sub, h])
                rinv = lax.rsqrt(jnp.mean(z.o * z.o, axis=-1, keepdims=True) + EPS)
                oh = z.o * rinv
                gd = p_ref[rows, sl(3)]
                sgd = _sigmoid(gd)
                dyh = dy_ref[rows, sl(0)]
                dgain = dgain + jnp.sum(dyh * oh * (gd * sgd), axis=0, keepdims=True)
                dp_ref[rows, sl(3)] = (dyh * oh * gain * sgd * (1.0 + gd * (1.0 - sgd))).astype(BF16)
                dyo = dyh * gain * (gd * sgd)
                do = rinv * (dyo - oh * jnp.mean(dyo * oh, axis=-1, keepdims=True))
                dqc, dkc, dv, dbeta, dg, dcur[h] = _gdn_chunk_bwd(z, s0, do, dcur[h])
                dact[h], dact[HEADS + h], dact[2 * HEADS + h] = dqc, dkc, dv
                dbeta_all = dbeta_all + jnp.where(lane == h, dbeta, 0.0)
                dg_all = dg_all + jnp.where(lane == HEADS + h, dg, 0.0)
            dact_rows[sub], dbeta_rows[sub], dg_rows[sub] = jnp.concatenate(dact, axis=1), dbeta_all, dg_all
        for h in range(HEADS):
            dstate[h] = dcur[h]
        dbeta_all, dg_all = jnp.concatenate(dbeta_rows, axis=0), jnp.concatenate(dg_rows, axis=0)
        dpre = jnp.concatenate(dact_rows, axis=0) * sg_pre * (1.0 + pre * (1.0 - sg_pre))
        dx, dws = _conv4_bwd(dpre, head[...], x, shifts, cwv)
        dp_ref[:, :GDN_CONV] = dx.astype(BF16)
        for k in range(4):
            dcw_ref[k:k + 1, :] += dws[k]
        head[...] = dpre[:SUBLANES]
        dsp = dg_all * (-jnp.exp(alv)) * _sigmoid(ps + dtv)
        dps_ref[...] = (dbeta_all * beta_all * (1.0 - beta_all) + dsp).astype(BF16)
        ddt_ref[...] += jnp.sum(dsp, axis=0, keepdims=True)
        dal_ref[...] += jnp.sum(dg_all * g_all, axis=0, keepdims=True)
        dgain_ref[...] += dgain

        @pl.when(n == ns - 1)
        def _():
            comm.finish(comm_in, comm_out, comm_sems)

    vec_f32 = jax.ShapeDtypeStruct((1, LANES), F32)
    outs = pl.pallas_call(
        body, grid=(ns,),
        in_specs=[sp["proj"], sp["prev"], sp["small"], sp["convw"], sp["vec"], sp["vec"], sp["vec"], sp["state"], sp["inv"],
                  sp["half"], sp["any"]]
        + [sp["any"]] * len(comm.arrays),
        out_specs=[sp["proj"], sp["small"], sp["convw"], sp["vec"], sp["vec"], sp["vec"]] + [sp["any"]] * len(comm.out_shapes),
        out_shape=[jax.ShapeDtypeStruct((t, 8 * GDN_W), BF16), jax.ShapeDtypeStruct((t, LANES), BF16),
                   jax.ShapeDtypeStruct((4, GDN_CONV), F32), vec_f32, vec_f32, vec_f32] + comm.out_shapes,
        scratch_shapes=[pltpu.VMEM((HEADS, HEAD_DIM, HEAD_DIM), F32), pltpu.VMEM((SUBLANES, GDN_CONV), F32)] + comm.scratch,
        name=name, input_output_aliases={10: 0}, compiler_params=_cp(("arbitrary",), 48),
    )(proj, proj, psmall, conv_w, al, dt, gain, *states, dy, dproj_in, *comm.arrays)
    return outs[:6], outs[6:]


def _here():
    x, y, c = lax.axis_index("x"), lax.axis_index("y"), lax.axis_index("c")
    return x, y, c, [(1 - x, y), (x, 1 - y), (1 - x, 1 - y)]


def _rdma(src, dst, send, recv, k, dev):
    return pltpu.make_async_remote_copy(src_ref=src, dst_ref=dst, send_sem=send.at[k], recv_sem=recv.at[k],
                                        device_id=dev, device_id_type=MESH)


def _dma_sems(n):
    return [pltpu.SemaphoreType.DMA((n,)), pltpu.SemaphoreType.DMA((n,)), pltpu.SemaphoreType.DMA((1,))]


COPY_PIECES = 4
COPY_PIECE_ALIGN = 16


def _row_parts(rows):
    n = COPY_PIECES if rows % (COPY_PIECES * COPY_PIECE_ALIGN) == 0 and rows >= 1024 else 1
    return [pl.ds(q * (rows // n), rows // n) for q in range(n)]


class _AllGather:
    def __init__(self, array):
        self.arrays = [array]
        self.out_shapes = [jax.ShapeDtypeStruct((N_DEV,) + array.shape, array.dtype)]
        self.parts = _row_parts(array.shape[0])
        self.scratch = _dma_sems(7 * len(self.parts))

    def start(self, ins, outs, sems):
        (src,), (out,), (send, recv, loc) = ins, outs, sems
        x, y, c, chips = _here()
        me, n = 4 * x + 2 * y + c, len(self.parts)
        pltpu.make_async_copy(src, out.at[me], loc.at[0]).start()
        for q, part in enumerate(self.parts):
            _rdma(src.at[part], out.at[me, part], send, recv, q, (x, y, 1 - c)).start()
            for j, (cx, cy) in enumerate(chips):
                _rdma(src.at[part], out.at[me, part], send, recv, (1 + j) * n + q, (cx, cy, c)).start()

    def finish(self, ins, outs, sems):
        (src,), (out,), (send, recv, loc) = ins, outs, sems
        x, y, c, chips = _here()
        sibling, me, n = (x, y, 1 - c), 4 * x + 2 * y + c, len(self.parts)
        piece = lambda k, q: _rdma(src.at[self.parts[q]], out.at[me, self.parts[q]], send, recv, k * n + q, sibling)
        for j, (cx, cy) in enumerate(chips):
            for q, part in enumerate(self.parts):
                got = out.at[4 * cx + 2 * cy + c, part]
                piece(1 + j, q).wait_recv()
                _rdma(got, got, send, recv, (4 + j) * n + q, sibling).start()
        for k in (0, 4, 5, 6):
            for q in range(n):
                piece(k, q).wait_recv()
        for k in range(7):
            for q in range(n):
                piece(k, q).wait_send()
        pltpu.make_async_copy(src, out.at[me], loc.at[0]).wait()


class _ChipExchange:
    def __init__(self, array):
        self.arrays = [array]
        self.out_shapes = [jax.ShapeDtypeStruct(array.shape, array.dtype)]
        self.parts = _row_parts(array.shape[1])
        self.scratch = _dma_sems(3 * len(self.parts))

    def _copies(self, ins, outs, sems):
        (src,), (out,), (send, recv, loc) = ins, outs, sems
        x, y, c, chips = _here()
        here, n = 2 * x + y, len(self.parts)
        local = pltpu.make_async_copy(src.at[here], out.at[here], loc.at[0])
        return local, [_rdma(src.at[2 * cx + cy, part], out.at[here, part], send, recv, j * n + q, (cx, cy, c))
                       for j, (cx, cy) in enumerate(chips) for q, part in enumerate(self.parts)]

    def start(self, ins, outs, sems):
        local, remote = self._copies(ins, outs, sems)
        local.start()
        for cp in remote:
            cp.start()

    def finish(self, ins, outs, sems):
        local, remote = self._copies(ins, outs, sems)
        for cp in remote:
            cp.wait()
        local.wait()


class _PairSwap:
    def __init__(self, array):
        self.arrays = [array]
        self.out_shapes = [jax.ShapeDtypeStruct(array.shape[1:], array.dtype)]
        self.parts = _row_parts(array.shape[2])
        self.scratch = _dma_sems(4 * len(self.parts))

    def _copies(self, ins, outs, sems):
        (src,), (theirs,), (send, recv, _) = ins, outs, sems
        x, y, c, _ = _here()
        return [_rdma(src.at[1 - c, p, part], theirs.at[p, part], send, recv, p * len(self.parts) + q, (x, y, 1 - c))
                for p in range(4) for q, part in enumerate(self.parts)]

    def start(self, ins, outs, sems):
        for cp in self._copies(ins, outs, sems):
            cp.start()

    def finish(self, ins, outs, sems):
        for cp in self._copies(ins, outs, sems):
            cp.wait()


class _Comm:
    def __init__(self, ops):
        self.ops = ops
        self.arrays = [a for op in ops for a in op.arrays]
        self.out_shapes = [s for op in ops for s in op.out_shapes]
        self.scratch = [s for op in ops for s in op.scratch]

    def split(self, rest, n_out, n_scratch):
        cuts = np.cumsum([0, len(self.arrays), n_out, len(self.out_shapes), n_scratch, len(self.scratch)])
        assert cuts[-1] == len(rest)
        return tuple(rest[a:b] for a, b in zip(cuts[:-1], cuts[1:]))

    def _each(self, method, ins, outs, sems):
        i = o = s = 0
        for op in self.ops:
            ni, no, ns = len(op.arrays), len(op.out_shapes), len(op.scratch)
            getattr(op, method)(ins[i:i + ni], outs[o:o + no], sems[s:s + ns])
            i, o, s = i + ni, o + no, s + ns

    def start(self, ins, outs, sems):
        self._each("start", ins, outs, sems)

    def finish(self, ins, outs, sems):
        self._each("finish", ins, outs, sems)

    def run(self, name):
        def body(*refs):
            ins, _, outs, _, sems = self.split(refs, 0, 0)
            self.start(ins, outs, sems)
            self.finish(ins, outs, sems)

        hbm = pl.BlockSpec(memory_space=pl.ANY)
        return pl.pallas_call(body, in_specs=[hbm] * len(self.arrays), out_specs=[hbm] * len(self.out_shapes),
                              out_shape=self.out_shapes, scratch_shapes=self.scratch, name=name)(*self.arrays)


def _sum_slots(x, *, name, tr=None):
    n, r, l = x.shape
    tr = r if tr is None else tr

    def body(x_ref, o_ref):
        acc = x_ref[0].astype(F32)
        for s in range(1, n):
            acc = acc + x_ref[s].astype(F32)
        o_ref[...] = acc

    return pl.pallas_call(
        body, grid=(r // tr,), in_specs=[pl.BlockSpec((n, tr, l), lambda i: (0, i, 0))],
        out_specs=pl.BlockSpec((tr, l), lambda i: (i, 0)), out_shape=jax.ShapeDtypeStruct((r, l), F32),
        name=name, compiler_params=_cp(("parallel",), 48),
    )(x)


def _pair_add(both, theirs, *, name, tr):
    _, n, r, l = both.shape

    def body(a_ref, b_ref, o_ref):
        mine = jnp.where(lax.axis_index("c") == 0, a_ref[0], a_ref[1])
        o_ref[...] = (mine.astype(F32) + b_ref[...].astype(F32)).astype(BF16)

    spec = pl.BlockSpec((n, tr, l), lambda i: (0, i, 0))
    return pl.pallas_call(body, grid=(r // tr,), in_specs=[pl.BlockSpec((2, n, tr, l), lambda i: (0, 0, i, 0)), spec], out_specs=spec,
                          out_shape=jax.ShapeDtypeStruct(theirs.shape, BF16), name=name,
                          compiler_params=_cp(("parallel",), 48))(both, theirs)


ADAM_TILE_ELEMS = 512 * 1024


def _adam(w, g, m, v, *, name):
    shape = w.shape
    cols = shape[-1]
    rows = math.prod(shape[:-1]) if len(shape) > 1 else 1
    tr = rows
    if rows * cols > ADAM_TILE_ELEMS:
        tr = max(d for d in range(SUBLANES, ADAM_TILE_ELEMS // cols + 1, SUBLANES) if rows % d == 0)
    c1, c2 = 1.0 - ADAM_B1 ** ADAM_STEP, 1.0 - ADAM_B2 ** ADAM_STEP

    def body(w_ref, g_ref, m_ref, v_ref, d_ref, m2_ref, v2_ref):
        gv = g_ref[...]
        m2 = ADAM_B1 * m_ref[...] + (1.0 - ADAM_B1) * gv
        v2 = ADAM_B2 * v_ref[...] + (1.0 - ADAM_B2) * (gv * gv)
        d_ref[...] = -ADAM_LR * ((m2 / c1) / (jnp.sqrt(v2 / c2) + ADAM_EPS) + ADAM_WD * w_ref[...])
        m2_ref[...] = m2
        v2_ref[...] = v2

    spec = pl.BlockSpec((tr, cols), lambda i: (i, 0))
    outs = pl.pallas_call(
        body, grid=(rows // tr,), in_specs=[spec] * 4, out_specs=[spec] * 3,
        out_shape=[jax.ShapeDtypeStruct((rows, cols), F32)] * 3, name=name, compiler_params=_cp(("parallel",), 48),
    )(*(a.reshape(rows, cols) for a in (w, g, m, v)))
    return tuple(o.reshape(shape) for o in outs)


WEIGHTS = ['norm_mix', 'norm_ffn', 'ret_gdn_w_in', 'gdn_conv_w', 'gdn_a_log', 'gdn_dt_bias', 'gdn_out_gain', 'ret_gdn_w_out',
           'lru_w_in', 'lru_conv_w', 'lru_conv_b', 'lru_w_a', 'lru_b_a', 'lru_w_x', 'lru_b_x', 'lru_lambda', 'lru_w_out',
           'ffn_w_up', 'ffn_conv_w', 'ffn_conv_b', 'ffn_w_down', 'norm_final']
BIG = {'ret_gdn_w_in': ((1, 1024, 513), 2), 'ret_gdn_w_out': ((1, 128, 1024), 1), 'lru_w_in': ((1, 1024, 256), 2),
       'lru_w_out': ((1, 128, 1024), 1), 'ffn_w_up': ((2, 1024, 704), 2), 'ffn_w_down': ((2, 352, 1024), 1)}
SMALL = {'gdn_conv_w': ((1, 4, 192), 2), 'lru_conv_w': ((1, 4, 128), 2), 'lru_conv_b': ((1, 128), 1), 'lru_b_a': ((1, 128), 1),
         'lru_b_x': ((1, 128), 1), 'lru_lambda': ((1, 128), 1), 'ffn_conv_w': ((2, 3, 704), 2)}
REPLICATED = {'norm_mix': (2, 1024), 'norm_ffn': (2, 1024), 'gdn_a_log': (1, 4), 'gdn_dt_bias': (1, 4), 'gdn_out_gain': (1, 128),
              'lru_w_a': (1, 8, 128, 128), 'lru_w_x': (1, 8, 128, 128), 'ffn_conv_b': (2, 5632), 'norm_final': (1024,)}
FIRST = ['ret_gdn_w_in', 'ret_gdn_w_out']
REST = ['lru_w_in', 'lru_w_out', 'ffn_w_up', 'ffn_w_down']
GROUP_ROWS = {FIRST[0]: 5632, REST[0]: 19968}
GROUP_TILE = {FIRST[0]: 512, REST[0]: 1536}
EARLY = {'lru_conv_w': (1, 4, 1024), 'lru_conv_b': (1, 1024), 'lru_b_a': (1, 1024), 'lru_b_x': (1, 1024), 'lru_lambda': (1, 1024),
         'ffn_conv_w': (2, 3, 5632), 'norm_ffn': (2, 1024), 'norm_mix1': (1, 1024), 'lru_w_a': (1, 8, 128, 128),
         'lru_w_x': (1, 8, 128, 128), 'ffn_conv_b': (2, 5632), 'norm_final': (1024,)}
LATE = {'gdn_conv_w': (1, 4, 1536), 'norm_mix0': (1, 1024), 'gdn_a_log': (1, 4), 'gdn_dt_bias': (1, 4), 'gdn_out_gain': (1, 128)}


def _full_shape(shard, axis):
    return tuple(d * N_DEV if i == axis else d for i, d in enumerate(shard))


def _rows_of(n_elems):
    return -(-n_elems // LANES)


def _to_rows(a, lead=()):
    flat = a.reshape(lead + (-1,))
    pad = _rows_of(flat.shape[-1]) * LANES - flat.shape[-1]
    if pad:
        flat = jnp.pad(flat, [(0, 0)] * len(lead) + [(0, pad)])
    return flat.reshape(lead + (-1, LANES))


def _pack(pieces, total_rows, lead=()):
    buf = jnp.concatenate(pieces, axis=len(lead))
    pad = total_rows - buf.shape[len(lead)]
    return jnp.pad(buf, [(0, 0)] * len(lead) + [(0, pad), (0, 0)]) if pad else buf


def _unpack(buf, shapes, lead=()):
    out, off = [], 0
    for shape in shapes:
        n = math.prod(shape)
        rows = _rows_of(n)
        piece = lax.slice_in_dim(buf, off, off + rows, axis=len(lead)).reshape(lead + (rows * LANES,))
        out.append(lax.slice_in_dim(piece, 0, n, axis=len(lead)).reshape(lead + shape))
        off += rows
    return out


def _join_blocks(g, axis):
    m = jnp.moveaxis(g, 0, axis)
    return m.reshape(m.shape[:axis] + (N_DEV * m.shape[axis + 1],) + m.shape[axis + 2:])


def _split_blocks(full, axis):
    s = full.shape
    return jnp.moveaxis(full.reshape(s[:axis] + (N_DEV, s[axis] // N_DEV) + s[axis + 1:]), axis, 0)


def _small_rows(shapes):
    total = sum(_rows_of(math.prod(s)) for s in shapes)
    return -(-total // SUBLANES) * SUBLANES


FFN_TM = 1024
FFN_TN = 512


def _ffn_forward(h, gain, w_up, cw, cb, w_down, tag, comm):
    t, d = h.shape
    tm, tn, blk, nb = min(FFN_TM, t), FFN_TN, FFN_BLK, FFN_NB
    hn = _norm_fwd(h, gain, name=f"ffn{tag}_norm")
    up = _mmx(hn, w_up, dims=NN, grid=(t // tm, 2 * nb, 1), name=f"ffn{tag}_up", tile=(tm, blk),
              a_spec=pl.BlockSpec((tm, d), lambda i, j, k: (i, 0)),
              b_spec=pl.BlockSpec((None, d, blk), lambda i, j, k: (j, 0, 0)),
              o_spec=pl.BlockSpec((None, None, tm, blk), lambda i, j, k: (j // nb, j % nb, i, 0)),
              out_shape=jax.ShapeDtypeStruct((2, nb, t, blk), F32))
    act, comm_out = _ffn_act_fwd(up, cw, cb, comm, name=f"ffn{tag}_act")
    out = _mmx(act, w_down, dims=NN, grid=(t // tm, d // tn, nb), name=f"ffn{tag}_down", tile=(tm, tn), res=h,
               a_spec=pl.BlockSpec((None, tm, blk), lambda i, j, k: (k, i, 0)),
               b_spec=pl.BlockSpec((blk, tn), lambda i, j, k: (k, j)),
               o_spec=pl.BlockSpec((tm, tn), lambda i, j, k: (i, j)),
               out_shape=jax.ShapeDtypeStruct((t, d), F32))
    return out, (hn, up, act), comm_out


def _ffn_backward(dh, h, gain, saved, w_up, cw, cb, w_down, tag):
    hn, up, act = saved
    t, d = h.shape
    tm, tn, blk, nb = min(FFN_TM, t), FFN_TN, FFN_BLK, FFN_NB
    tk = min(FFN_TM, t)
    da = _mmx(dh, w_down, dims=NT, grid=(t // tm, nb, 1), name=f"ffn{tag}_d_act", tile=(tm, blk),
              a_spec=pl.BlockSpec((tm, d), lambda i, j, k: (i, 0)),
              b_spec=pl.BlockSpec((blk, d), lambda i, j, k: (j, 0)),
              o_spec=pl.BlockSpec((None, tm, blk), lambda i, j, k: (j, i, 0)),
              out_shape=jax.ShapeDtypeStruct((nb, t, blk), F32))
    dwd = _mmx(act, dh, dims=TN, grid=(nb, d // tn, t // tk), name=f"ffn{tag}_d_wdown", tile=(blk, tn), split_rows=blk // 2,
               a_spec=pl.BlockSpec((None, tk, blk), lambda i, j, k: (i, k, 0)),
               b_spec=pl.BlockSpec((tk, tn), lambda i, j, k: (k, j)),
               o_spec=pl.BlockSpec((2, None, blk // 2, tn), lambda i, j, k: (0, i, 0, j)),
               out_shape=jax.ShapeDtypeStruct((2, nb, blk // 2, d), BF16))
    dup, dcw, dcb = _ffn_act_bwd(up, da, cw, cb, name=f"ffn{tag}_act_bwd")
    dhn = _mmx(dup, w_up, dims=NT, grid=(t // tm, d // tn, 2 * nb), name=f"ffn{tag}_d_hn", tile=(tm, tn),
               a_spec=pl.BlockSpec((None, None, tm, blk), lambda i, j, k: (k // nb, k % nb, i, 0)),
               b_spec=pl.BlockSpec((None, tn, blk), lambda i, j, k: (k, j, 0)),
               o_spec=pl.BlockSpec((tm, tn), lambda i, j, k: (i, j)),
               out_shape=jax.ShapeDtypeStruct((t, d), F32))
    dwu = _mmx(hn, dup, dims=TN, grid=(1, 2 * nb, t // tk), name=f"ffn{tag}_d_wup", tile=(d, blk),
               a_spec=pl.BlockSpec((tk, d), lambda i, j, k: (k, 0)),
               b_spec=pl.BlockSpec((None, None, tk, blk), lambda i, j, k: (j // nb, j % nb, k, 0)),
               o_spec=pl.BlockSpec((None, None, d, blk), lambda i, j, k: (j % 2, j // 2, 0, 0)),
               out_shape=jax.ShapeDtypeStruct((2, N_DEV // 2, d, blk), BF16))
    dh_in, dgain = _norm_bwd(h, gain, dhn, dh, name=f"ffn{tag}_norm_bwd")
    conv_w = dcw.transpose(2, 0, 1, 3).reshape(3, 2 * nb * blk)
    return dh_in, dict(w_up=dwu, w_down=dwd, conv_w=conv_w, conv_b=dcb.reshape(1, 2 * nb * blk), norm=dgain)


def kernel(x, norm_mix, norm_ffn, ret_gdn_w_in, gdn_conv_w, gdn_a_log, gdn_dt_bias, gdn_out_gain, ret_gdn_w_out, lru_w_in, lru_conv_w, lru_conv_b, lru_w_a, lru_b_a, lru_w_x, lru_b_x, lru_lambda, lru_w_out, ffn_w_up, ffn_conv_w, ffn_conv_b, ffn_w_down, norm_final, loss_target, m_norm_mix, m_norm_ffn, m_ret_gdn_w_in, m_gdn_conv_w, m_gdn_a_log, m_gdn_dt_bias, m_gdn_out_gain, m_ret_gdn_w_out, m_lru_w_in, m_lru_conv_w, m_lru_conv_b, m_lru_w_a, m_lru_b_a, m_lru_w_x, m_lru_b_x, m_lru_lambda, m_lru_w_out, m_ffn_w_up, m_ffn_conv_w, m_ffn_conv_b, m_ffn_w_down, m_norm_final, v_norm_mix, v_norm_ffn, v_ret_gdn_w_in, v_gdn_conv_w, v_gdn_a_log, v_gdn_dt_bias, v_gdn_out_gain, v_ret_gdn_w_out, v_lru_w_in, v_lru_conv_w, v_lru_conv_b, v_lru_w_a, v_lru_b_a, v_lru_w_x, v_lru_b_x, v_lru_lambda, v_lru_w_out, v_ffn_w_up, v_ffn_conv_w, v_ffn_conv_b, v_ffn_w_down, v_norm_final):
    given = dict(locals())
    w = {n: given[n] for n in WEIGHTS}
    me = 4 * lax.axis_index("x") + 2 * lax.axis_index("y") + lax.axis_index("c")
    t = x.shape[1]
    f = D_FF

    first_shards = {'w_in0': ret_gdn_w_in[0]}
    rest_shards = {'w_out0': ret_gdn_w_out[0], 'lru_in': lru_w_in[0], 'lru_out': lru_w_out[0], 'up0': ffn_w_up[0], 'up1': ffn_w_up[1],
                   'down0': ffn_w_down[0], 'down1': ffn_w_down[1]}
    small_shapes = [s for s, _ in SMALL.values()]
    small_buf = _pack([_to_rows(w[n]) for n in SMALL], _small_rows(small_shapes))
    *g_first, g_small = _Comm([_AllGather(a.astype(BF16)) for a in first_shards.values()] + [_AllGather(small_buf)]).run("gather_first")
    hosted = {'retention_fwd': ['w_out0', 'lru_out'], 'deltanet_fwd': ['lru_in', 'up0', 'down0'], 'ffn0_act': ['up1'], 'rglru_fwd': ['down1']}
    gather_in = {host: _Comm([_AllGather(rest_shards[n].astype(BF16)) for n in names]) for host, names in hosted.items()}
    got = dict(zip(first_shards, g_first))
    small_blocks = dict(zip(SMALL, _unpack(g_small, small_shapes, lead=(N_DEV,))))
    full = {n: _join_blocks(small_blocks[n], SMALL[n][1]) for n in SMALL if n != 'ffn_conv_w'}

    w_main, w_narrow = _join_w_in(got['w_in0'], name="join_w_in")
    fcw = [small_blocks['ffn_conv_w'][:, l].reshape(2, FFN_NB, 3, FFN_BLK) for l in range(2)]
    fcb = [ffn_conv_b[l].reshape(2, FFN_NB, 1, FFN_BLK) for l in range(2)]
    gdn_cw = full['gdn_conv_w'][0]
    al_pad = jnp.pad(gdn_a_log, ((0, 0), (HEADS, LANES - 2 * HEADS)))
    dt_pad = jnp.pad(gdn_dt_bias, ((0, 0), (HEADS, LANES - 2 * HEADS)))
    lru_cw, lru_cb = full['lru_conv_w'][0], full['lru_conv_b']
    lru_ba, lru_bx, lru_lam = full['lru_b_a'], full['lru_b_x'], full['lru_lambda']
    wa, wx = lru_w_a[0], lru_w_x[0]

    h0, target = x[0], loss_target[0]
    hn0 = _norm_fwd(h0, norm_mix[0:1], name="mix0_norm")
    proj = _mm(hn0, w_main, name="mix0_in")
    pnarrow = _mm(hn0, w_narrow, name="mix0_in_narrow")
    tables = _ret_tables(t)
    y0, ret_states, g = _ret_fwd(proj, tables, gather_in['retention_fwd'], name="retention_fwd")
    got.update(zip(hosted['retention_fwd'], g))
    y0, gdn_states, g = _gdn_fwd(proj, pnarrow, gdn_cw, al_pad, dt_pad, gdn_out_gain, y0, gather_in['deltanet_fwd'], name="deltanet_fwd")
    got.update(zip(hosted['deltanet_fwd'], g))
    lru_in = _join_blocks(got['lru_in'], 1)
    lru_in_g, lru_in_x = lru_in[:, :D_MODEL], lru_in[:, D_MODEL:]
    lru_out = got['lru_out'].reshape(D_MODEL, D_MODEL)
    w_out0 = got['w_out0'].reshape(D_MODEL, D_MODEL)
    h1 = _mm(y0, w_out0, res=h0, name="mix0_out")
    h2, ffn0_saved, g = _ffn_forward(h1, norm_ffn[0:1], got['up0'], fcw[0], fcb[0], got['down0'].reshape(D_FF, D_MODEL), 0,
                                     gather_in['ffn0_act'])
    got.update(zip(hosted['ffn0_act'], g))
    hn1 = _norm_fwd(h2, norm_mix[1:2], name="mix1_norm")
    gate = _mm(hn1, lru_in_g, name="mix1_in_gate")
    xpre = _mm(hn1, lru_in_x, name="mix1_in_x")
    y1, hs, g = _lru_fwd(gate, xpre, lru_cw, lru_cb, wa, lru_ba, wx, lru_bx, lru_lam, gather_in['rglru_fwd'], name="rglru_fwd")
    got.update(zip(hosted['rglru_fwd'], g))
    h3 = _mm(y1, lru_out, res=h2, name="mix1_out")
    w_up = [got['up0'], got['up1']]
    down = [got['down0'].reshape(D_FF, D_MODEL), got['down1'].reshape(D_FF, D_MODEL)]
    h4, ffn1_saved, _ = _ffn_forward(h3, norm_ffn[1:2], w_up[1], fcw[1], fcb[1], down[1], 1, _Comm([]))
    dh4, d_norm_final, loss_part = _final_loss(h4, norm_final[None, :], target, name="final_norm_loss")
    loss = lax.psum(loss_part[0, 0], ("x", "y", "c"))

    dh3, gf1 = _ffn_backward(dh4, h3, norm_ffn[1:2], ffn1_saved, w_up[1], fcw[1], fcb[1], down[1], 1)
    dy1 = _mm(dh3, lru_out, tb=True, name="mix1_d_y")
    d_lru_out = _mm(y1, dh3, ta=True, out_dtype=BF16, name="mix1_d_wout")
    dgate, dxpre, d_lcw, d_lcb, d_wa, d_ba, d_wx, d_bx, d_lam = _lru_bwd(
        gate, xpre, hs, dy1, lru_cw, lru_cb, wa, lru_ba, wx, lru_bx, lru_lam, name="rglru_bwd")
    dhn1 = _mm(dgate, lru_in_g, tb=True, name="mix1_d_hn_gate")
    dhn1 = _mm(dxpre, lru_in_x, tb=True, res=dhn1, name="mix1_d_hn_x")
    d_lru_in = jnp.concatenate([_mm(hn1, dgate, ta=True, out_dtype=BF16, name="mix1_d_win_gate"),
                                _mm(hn1, dxpre, ta=True, out_dtype=BF16, name="mix1_d_win_x")], axis=1)
    dh2, d_mix1 = _norm_bwd(h2, norm_mix[1:2], dhn1, dh3, name="mix1_norm_bwd")
    dh1, gf0 = _ffn_backward(dh2, h1, norm_ffn[0:1], ffn0_saved, w_up[0], fcw[0], fcb[0], down[0], 0)
    dy0 = _mm(dh1, w_out0, tb=True, name="mix0_d_y")
    d_w_out0 = _mm(y0, dh1, ta=True, out_dtype=BF16, name="mix0_d_wout")

    def by_core_chip(full_grad, axis):
        blocks = _split_blocks(full_grad, axis)
        return blocks.reshape((4, 2) + blocks.shape[1:]).transpose(1, 0, 2, 3)

    def pair_add(blocks, theirs):
        return {k: _pair_add(b, o, name=f"pair_add_{k}", tr=row_tile(b.shape[2])) for (k, b), o in zip(blocks.items(), theirs)}

    row_tile = lambda rows: rows if rows <= 512 else 256

    rest_blocks = {'w_out0': by_core_chip(d_w_out0, 0), 'lru_in': by_core_chip(d_lru_in, 1), 'lru_out': by_core_chip(d_lru_out, 0),
                   'up0': gf0['w_up'], 'up1': gf1['w_up'], 'down0': gf0['w_down'], 'down1': gf1['w_down']}
    early = {'lru_conv_w': d_lcw[None], 'lru_conv_b': d_lcb, 'lru_b_a': d_ba, 'lru_b_x': d_bx, 'lru_lambda': d_lam,
             'ffn_conv_w': jnp.stack([gf0['conv_w'], gf1['conv_w']]), 'norm_ffn': jnp.concatenate([gf0['norm'], gf1['norm']], axis=0),
             'norm_mix1': d_mix1, 'lru_w_a': d_wa[None], 'lru_w_x': d_wx[None],
             'ffn_conv_b': jnp.concatenate([gf0['conv_b'], gf1['conv_b']], axis=0), 'norm_final': d_norm_final[0]}
    early_buf = _pack([_to_rows(early[n]) for n in EARLY], _small_rows(list(EARLY.values())))
    dproj, (*theirs, got_early) = _ret_bwd(proj, tables, ret_states, dy0,
                                           _Comm([_PairSwap(b) for b in rest_blocks.values()] + [_AllGather(early_buf)]), name="retention_bwd")
    z_rest = pair_add(rest_blocks, theirs)
    (dproj, dnarrow, d_gcw, d_alog, d_dtb, d_gain), w_rest = _gdn_bwd(
        proj, pnarrow, gdn_cw, al_pad, dt_pad, gdn_out_gain, gdn_states, dy0, dproj,
        _Comm([_ChipExchange(z) for z in z_rest.values()]), name="deltanet_bwd")
    dhn0 = _mm(dproj, w_main, tb=True, name="mix0_d_hn")
    dhn0 = _mm(dnarrow, w_narrow, tb=True, res=dhn0, name="mix0_d_hn_narrow")
    d_w_main = _mm(hn0, dproj, ta=True, out_dtype=BF16, name="mix0_d_win")
    d_w_narrow = _mm(hn0, dnarrow, ta=True, out_dtype=BF16, name="mix0_d_win_narrow")
    dx, d_mix0 = _norm_bwd(h0, norm_mix[0:1], dhn0, dh1, name="mix0_norm_bwd")

    first_blocks = {'w_in0': _split_w_in(d_w_main, d_w_narrow, name="split_d_w_in")}
    z_first = pair_add(first_blocks, _Comm([_PairSwap(b) for b in first_blocks.values()]).run("pair_swap_first"))
    late = {'gdn_conv_w': d_gcw[None], 'norm_mix0': d_mix0, 'gdn_a_log': d_alog[:, HEADS:2 * HEADS],
            'gdn_dt_bias': d_dtb[:, HEADS:2 * HEADS], 'gdn_out_gain': d_gain}
    late_buf = _pack([_to_rows(late[n]) for n in LATE], _small_rows(list(LATE.values())))
    *w_first, got_late = _Comm([_ChipExchange(z) for z in z_first.values()] + [_AllGather(late_buf)]).run("exchange_first")

    summed = {k: _sum_slots(blocks, name=f"sum_blocks_{k}", tr=row_tile(blocks.shape[1]))
              for k, blocks in list(zip(z_rest, w_rest)) + list(zip(z_first, w_first))}
    grads = {'ret_gdn_w_in': summed['w_in0'][None], 'ret_gdn_w_out': summed['w_out0'][None], 'lru_w_in': summed['lru_in'][None],
             'lru_w_out': summed['lru_out'][None], 'ffn_w_up': jnp.stack([summed['up0'], summed['up1']]),
             'ffn_w_down': jnp.stack([summed['down0'], summed['down1']])}
    partial = dict(zip(EARLY, _unpack(_sum_slots(got_early, name="sum_partials_early"), list(EARLY.values()))))
    partial.update(zip(LATE, _unpack(_sum_slots(got_late, name="sum_partials_late"), list(LATE.values()))))
    partial['norm_mix'] = jnp.concatenate([partial.pop('norm_mix0'), partial.pop('norm_mix1')], axis=0)
    for n, g_full in partial.items():
        if n in SMALL:
            shard, axis = SMALL[n]
            g_full = lax.dynamic_slice_in_dim(g_full, me * shard[axis], shard[axis], axis=axis)
        grads[n] = g_full

    delta, new_m, new_v = {}, {}, {}
    for n in WEIGHTS:
        delta[n], new_m[n], new_v[n] = _adam(w[n], grads[n], given["m_" + n], given["v_" + n], name=f"adamw_{n}")
    return (loss, dx[None], *[grads[n] for n in WEIGHTS], *[delta[n] for n in WEIGHTS],
            *[new_m[n] for n in WEIGHTS], *[new_v[n] for n in WEIGHTS])
```

```python
import functools
import math

import numpy as np
import jax
import jax.numpy as jnp
from jax import lax
from jax.experimental import pallas as pl
from jax.experimental.pallas import tpu as pltpu

F32 = jnp.float32
BF16 = jnp.bfloat16
HI = lax.Precision.HIGHEST
MESH = pl.DeviceIdType.MESH

N_DEV = 8
LANES = 128
SUBLANES = 8
EPS = 1e-6
D_MODEL = 1024
HEADS = 4
HEAD_DIM = 128
RET_CHUNK = 128
GDN_CHUNK = 64
ROPE_BASE = 10000.0
LRU_C = 8.0
D_FF = 2816
MAIN_IN = 4096
SMALL_IN = 8
QSCALE = HEAD_DIM ** -0.5

ADAM_LR, ADAM_B1, ADAM_B2, ADAM_EPS, ADAM_WD, ADAM_STEP = 0.001, 0.9, 0.999, 1e-08, 0.01, 10


def _cp(sem=None, vmem_mb=None):
    kw = {}
    if sem is not None:
        kw["dimension_semantics"] = sem
    if vmem_mb is not None:
        kw["vmem_limit_bytes"] = vmem_mb << 20
    return pltpu.CompilerParams(**kw)


def _rows(shape):
    return lax.broadcasted_iota(jnp.int32, shape, 0)


def _cols(shape):
    return lax.broadcasted_iota(jnp.int32, shape, 1)


def _shift_down(cur, prev8, s):
    if s == 0:
        return cur
    rc = pltpu.roll(cur, s, 0)
    rp = pltpu.roll(prev8, s, 0)
    top = jnp.where(_rows(prev8.shape) < s, rp, rc[:SUBLANES])
    return jnp.concatenate([top, rc[SUBLANES:]], axis=0)


def _shift_up(cur, next8, s):
    if s == 0:
        return cur
    tt = cur.shape[0]
    rc = pltpu.roll(cur, tt - s, 0)
    rn = pltpu.roll(next8, SUBLANES - s, 0)
    bot = jnp.where(_rows(next8.shape) >= SUBLANES - s, rn, rc[tt - SUBLANES:])
    return jnp.concatenate([rc[:tt - SUBLANES], bot], axis=0)


def _down_fill(x, d, fill):
    return jnp.where(_rows(x.shape) < d, fill, pltpu.roll(x, d, 0))


def _up_fill(x, d, fill):
    tt = x.shape[0]
    return jnp.where(_rows(x.shape) >= tt - d, fill, pltpu.roll(x, tt - d, 0))


def _sigmoid(x):
    return 1.0 / (1.0 + jnp.exp(-x))


def _softplus(x):
    return jnp.maximum(x, 0.0) + jnp.log(1.0 + jnp.exp(-jnp.abs(x)))


def _dot(a, b, dims=(((1,), (0,)), ((), ())), precision=None):
    return lax.dot_general(a, b, dims, preferred_element_type=F32, precision=precision)


NN = (((1,), (0,)), ((), ()))
NT = (((1,), (1,)), ((), ()))
TN = (((0,), (0,)), ((), ()))


def _bdot(a, b, dims=NN):
    return _dot(a.astype(BF16), b.astype(BF16), dims)


def _split(a):
    hi = a.astype(BF16)
    return hi, (a - hi.astype(F32)).astype(BF16)


def _dot3(a, b, dims=NN):
    ah, al = _split(a)
    bh, bl = _split(b)
    return _dot(ah, bh, dims) + (_dot(ah, bl, dims) + _dot(al, bh, dims))


def _tile(dim, target):
    if dim <= target:
        return dim
    best = None
    for c in range(LANES, target + 1, LANES):
        if dim % c == 0:
            best = c
    assert best is not None, (dim, target)
    return best


def _mm(a, b, *, name, ta=False, tb=False, out_dtype=F32, res=None, tm=2048, tn=512, tk=1024):
    m, k = (a.shape[1], a.shape[0]) if ta else a.shape
    n = b.shape[0] if tb else b.shape[1]
    tn, tk = _tile(n, tn), _tile(k, tk)
    tm = _tile(m, tm if max(tn, tk) <= 1024 else tm // 2)
    nk = k // tk
    dims = (((0 if ta else 1,), (1 if tb else 0,)), ((), ()))

    def body(*refs):
        a_ref, b_ref = refs[:2]
        r_ref = refs[2] if res is not None else None
        o_ref = refs[3] if res is not None else refs[2]
        acc = refs[-1]
        kk = pl.program_id(2)
        part = _bdot(a_ref[...], b_ref[...], dims)

        def finish(r):
            if res is not None:
                r = r + r_ref[...]
            o_ref[...] = r.astype(out_dtype)

        if nk == 1:
            finish(part)
            return

        @pl.when(kk == 0)
        def _():
            acc[...] = part

        @pl.when(jnp.logical_and(kk > 0, kk < nk - 1))
        def _():
            acc[...] += part

        @pl.when(kk == nk - 1)
        def _():
            finish(acc[...] + part)

    a_spec = pl.BlockSpec((tk, tm), lambda i, j, kk: (kk, i)) if ta else pl.BlockSpec((tm, tk), lambda i, j, kk: (i, kk))
    b_spec = pl.BlockSpec((tn, tk), lambda i, j, kk: (j, kk)) if tb else pl.BlockSpec((tk, tn), lambda i, j, kk: (kk, j))
    o_spec = pl.BlockSpec((tm, tn), lambda i, j, kk: (i, j))
    in_specs = [a_spec, b_spec] + ([o_spec] if res is not None else [])
    args = (a, b) + ((res,) if res is not None else ())
    return pl.pallas_call(
        body, grid=(m // tm, n // tn, nk), in_specs=in_specs, out_specs=o_spec,
        out_shape=jax.ShapeDtypeStruct((m, n), out_dtype),
        scratch_shapes=[pltpu.VMEM((tm, tn), F32)] if nk > 1 else [], name=name,
        compiler_params=_cp(("parallel", "parallel", "arbitrary"), 56),
    )(*args)


def _mmx(a, b, *, dims, grid, a_spec, b_spec, o_spec, out_shape, tile, name, res=None, split_rows=None):
    nk = grid[-1]

    def body(*refs):
        a_ref, b_ref = refs[:2]
        r_ref = refs[2] if res is not None else None
        o_ref = refs[3] if res is not None else refs[2]
        acc = refs[-1]
        part = _bdot(a_ref[...], b_ref[...], dims)

        def finish(r):
            if res is not None:
                r = r + r_ref[...]
            if split_rows is None:
                o_ref[...] = r.astype(o_ref.dtype)
            else:
                o_ref[0] = r[:split_rows].astype(o_ref.dtype)
                o_ref[1] = r[split_rows:].astype(o_ref.dtype)

        if nk == 1:
            finish(part)
            return
        kk = pl.program_id(len(grid) - 1)

        @pl.when(kk == 0)
        def _():
            acc[...] = part

        @pl.when(jnp.logical_and(kk > 0, kk < nk - 1))
        def _():
            acc[...] += part

        @pl.when(kk == nk - 1)
        def _():
            finish(acc[...] + part)

    args = (a, b) + ((res,) if res is not None else ())
    return pl.pallas_call(
        body, grid=grid, in_specs=[a_spec, b_spec] + ([o_spec] if res is not None else []), out_specs=o_spec,
        out_shape=out_shape, scratch_shapes=[pltpu.VMEM(tile, F32)] if nk > 1 else [], name=name,
        compiler_params=_cp(("parallel",) * (len(grid) - 1) + ("arbitrary",), 56),
    )(*args)


W_IN_BLK = 513
W_IN_TR = 256


def _join_w_in(blocks, *, name):
    _, d, _ = blocks.shape
    tr = W_IN_TR

    def body(x_ref, main_ref, narrow_ref):
        for m in range(MAIN_IN // LANES):
            lo = LANES * m
            dev, off = divmod(lo, W_IN_BLK)
            if off + LANES <= W_IN_BLK:
                main_ref[:, lo:lo + LANES] = x_ref[dev, :, off:off + LANES]
            else:
                main_ref[:, lo:lo + LANES] = jnp.concatenate(
                    [x_ref[dev, :, off:W_IN_BLK], x_ref[dev + 1, :, 0:LANES - (W_IN_BLK - off)]], axis=1)
        tail = x_ref[N_DEV - 1, :, W_IN_BLK - SMALL_IN:W_IN_BLK]
        narrow_ref[...] = jnp.concatenate([tail, jnp.zeros((tr, LANES - SMALL_IN), tail.dtype)], axis=1)

    return pl.pallas_call(
        body, grid=(d // tr,), in_specs=[pl.BlockSpec((N_DEV, tr, W_IN_BLK), lambda i: (0, i, 0))],
        out_specs=[pl.BlockSpec((tr, MAIN_IN), lambda i: (i, 0)), pl.BlockSpec((tr, LANES), lambda i: (i, 0))],
        out_shape=[jax.ShapeDtypeStruct((d, MAIN_IN), blocks.dtype), jax.ShapeDtypeStruct((d, LANES), blocks.dtype)],
        name=name, compiler_params=_cp(("parallel",), 48),
    )(blocks)


def _split_w_in(main, narrow, *, name):
    d = main.shape[0]
    tr = W_IN_TR

    def body(m_ref, n_ref, o_ref):
        for dev in range(N_DEV):
            lo = W_IN_BLK * dev
            if dev < N_DEV - 1:
                piece = m_ref[:, lo:lo + W_IN_BLK]
            else:
                piece = jnp.concatenate([m_ref[:, lo:MAIN_IN], n_ref[:, 0:SMALL_IN]], axis=1)
            o_ref[dev % 2, dev // 2] = piece

    return pl.pallas_call(
        body, grid=(d // tr,),
        in_specs=[pl.BlockSpec((tr, MAIN_IN), lambda i: (i, 0)), pl.BlockSpec((tr, LANES), lambda i: (i, 0))],
        out_specs=pl.BlockSpec((2, N_DEV // 2, tr, W_IN_BLK), lambda i: (0, 0, i, 0)),
        out_shape=jax.ShapeDtypeStruct((2, N_DEV // 2, d, W_IN_BLK), main.dtype),
        name=name, compiler_params=_cp(("parallel",), 48),
    )(main, narrow)


def _norm_fwd(h, gain, *, name, tt=256):
    t, d = h.shape
    tt = min(tt, t)

    def body(h_ref, g_ref, o_ref):
        x = h_ref[...]
        r = lax.rsqrt(jnp.mean(x * x, axis=-1, keepdims=True) + EPS)
        o_ref[...] = (x * r * g_ref[...]).astype(BF16)

    row = pl.BlockSpec((tt, d), lambda i: (i, 0))
    return pl.pallas_call(
        body, grid=(t // tt,), in_specs=[row, pl.BlockSpec((1, d), lambda i: (0, 0))], out_specs=row,
        out_shape=jax.ShapeDtypeStruct((t, d), BF16), name=name, compiler_params=_cp(("parallel",)),
    )(h, gain)


def _norm_bwd(h, gain, dhn, dres, *, name, tt=256):
    t, d = h.shape
    tt = min(tt, t)

    def body(h_ref, g_ref, dy_ref, dr_ref, dx_ref, dg_ref):
        x, dy = h_ref[...], dy_ref[...]
        r = lax.rsqrt(jnp.mean(x * x, axis=-1, keepdims=True) + EPS)
        xh = x * r

        @pl.when(pl.program_id(0) == 0)
        def _():
            dg_ref[...] = jnp.zeros_like(dg_ref)

        dg_ref[...] += jnp.sum(dy * xh, axis=0, keepdims=True)
        dxh = dy * g_ref[...]
        dx_ref[...] = dr_ref[...] + r * (dxh - xh * jnp.mean(dxh * xh, axis=-1, keepdims=True))

    row = pl.BlockSpec((tt, d), lambda i: (i, 0))
    vec = pl.BlockSpec((1, d), lambda i: (0, 0))
    return pl.pallas_call(
        body, grid=(t // tt,), in_specs=[row, vec, row, row], out_specs=[row, vec],
        out_shape=[jax.ShapeDtypeStruct((t, d), F32), jax.ShapeDtypeStruct((1, d), F32)],
        name=name, compiler_params=_cp(("arbitrary",)),
    )(h, gain, dhn, dres)


def _final_loss(h, gain, target, *, name, tt=256):
    t, d = h.shape
    tt = min(tt, t)

    def body(h_ref, g_ref, tg_ref, dx_ref, dg_ref, loss_ref):
        x = h_ref[...]
        r = lax.rsqrt(jnp.mean(x * x, axis=-1, keepdims=True) + EPS)
        xh = x * r
        err = xh * g_ref[...] - tg_ref[...]

        @pl.when(pl.program_id(0) == 0)
        def _():
            dg_ref[...] = jnp.zeros_like(dg_ref)
            loss_ref[...] = jnp.zeros_like(loss_ref)

        loss_ref[...] += 0.5 * jnp.sum(jnp.mean(err * err, axis=-1, keepdims=True), axis=0, keepdims=True)
        dy = err * (1.0 / d)
        dg_ref[...] += jnp.sum(dy * xh, axis=0, keepdims=True)
        dxh = dy * g_ref[...]
        dx_ref[...] = r * (dxh - xh * jnp.mean(dxh * xh, axis=-1, keepdims=True))

    row = pl.BlockSpec((tt, d), lambda i: (i, 0))
    vec = pl.BlockSpec((1, d), lambda i: (0, 0))
    return pl.pallas_call(
        body, grid=(t // tt,), in_specs=[row, vec, row],
        out_specs=[row, vec, pl.BlockSpec((1, 1), lambda i: (0, 0))],
        out_shape=[jax.ShapeDtypeStruct((t, d), F32), jax.ShapeDtypeStruct((1, d), F32), jax.ShapeDtypeStruct((1, 1), F32)],
        name=name, compiler_params=_cp(("arbitrary",)),
    )(h, gain, target)


FFN_BLK = 704
FFN_NB = 4
FFN_TT = 256


def _prev8(n, tt):
    return jnp.maximum(n * (tt // SUBLANES) - 1, 0)


def _ffn_conv(cur, prev8, w, b):
    s1 = _shift_down(cur, prev8, 1)
    s2 = _shift_down(cur, prev8, 2)
    return w[0:1] * s2 + w[1:2] * s1 + w[2:3] * cur + b, s1, s2


def _ffn_specs(t, tt, order):
    pair = lambda rows, row_index: pl.BlockSpec((2, None, rows, FFN_BLK), lambda j, n: (0, j, row_index(n), 0))
    return dict(cur=pair(tt, order), prev=pair(SUBLANES, lambda n: _prev8(order(n), tt)), w=pair(3, lambda n: 0), b=pair(1, lambda n: 0),
                one=pl.BlockSpec((None, tt, FFN_BLK), lambda j, n: (j, order(n), 0)))


def _ffn_act_fwd(up, cw, cb, comm, *, name):
    t = up.shape[2]
    tt = min(FFN_TT, t)
    nt = t // tt
    sp = _ffn_specs(t, tt, lambda n: n)
    hbm = pl.BlockSpec(memory_space=pl.ANY)

    def body(u_ref, p_ref, w_ref, b_ref, *rest):
        comm_in, (o_ref,), comm_out, _, comm_sems = comm.split(rest, n_out=1, n_scratch=0)
        j, n = pl.program_id(0), pl.program_id(1)

        @pl.when(jnp.logical_and(j == 0, n == 0))
        def _():
            comm.start(comm_in, comm_out, comm_sems)

        first = n == 0
        gate, _, _ = _ffn_conv(u_ref[0], jnp.where(first, 0.0, p_ref[0]), w_ref[0], b_ref[0])
        val, _, _ = _ffn_conv(u_ref[1], jnp.where(first, 0.0, p_ref[1]), w_ref[1], b_ref[1])
        o_ref[...] = (gate * _sigmoid(gate) * val).astype(BF16)

        @pl.when(jnp.logical_and(j == FFN_NB - 1, n == nt - 1))
        def _():
            comm.finish(comm_in, comm_out, comm_sems)

    outs = pl.pallas_call(
        body, grid=(FFN_NB, nt), in_specs=[sp["cur"], sp["prev"], sp["w"], sp["b"]] + [hbm] * len(comm.arrays),
        out_specs=[sp["one"]] + [hbm] * len(comm.out_shapes),
        out_shape=[jax.ShapeDtypeStruct((FFN_NB, t, FFN_BLK), BF16)] + comm.out_shapes, scratch_shapes=comm.scratch, name=name,
        compiler_params=_cp(("arbitrary", "arbitrary"), 48),
    )(up, up, cw, cb, *comm.arrays)
    return outs[0], outs[1:]


def _ffn_act_bwd(up, da, cw, cb, *, name):
    t = up.shape[2]
    tt = min(FFN_TT, t)
    nt = t // tt
    sp = _ffn_specs(t, tt, lambda n: nt - 1 - n)

    def body(u_ref, p_ref, da_ref, w_ref, b_ref, du_ref, dw_ref, db_ref, head):
        n = pl.program_id(1)
        tile0 = n == nt - 1

        @pl.when(n == 0)
        def _():
            for r in (head, dw_ref, db_ref):
                r[...] = jnp.zeros_like(r)

        convs = [_ffn_conv(u_ref[s], jnp.where(tile0, 0.0, p_ref[s]), w_ref[s], b_ref[s]) for s in range(2)]
        gate, val = convs[0][0], convs[1][0]
        d = da_ref[...]
        sg = _sigmoid(gate)
        dcs = (d * val * sg * (1.0 + gate * (1.0 - sg)), d * gate * sg)
        for s in range(2):
            dc, w, hd = dcs[s], w_ref[s], head[s]
            _, x1, x2 = convs[s]
            du_ref[s] = (w[2:3] * dc + w[1:2] * _shift_up(dc, hd, 1) + w[0:1] * _shift_up(dc, hd, 2)).astype(BF16)
            dw_ref[s, 0:1, :] += jnp.sum(dc * x2, axis=0, keepdims=True)
            dw_ref[s, 1:2, :] += jnp.sum(dc * x1, axis=0, keepdims=True)
            dw_ref[s, 2:3, :] += jnp.sum(dc * u_ref[s], axis=0, keepdims=True)
            db_ref[s] += jnp.sum(dc, axis=0, keepdims=True)
            head[s] = dc[:SUBLANES]

    return pl.pallas_call(
        body, grid=(FFN_NB, nt), in_specs=[sp["cur"], sp["prev"], sp["one"], sp["w"], sp["b"]],
        out_specs=[sp["cur"], sp["w"], sp["b"]],
        out_shape=[jax.ShapeDtypeStruct(up.shape, BF16), jax.ShapeDtypeStruct(cw.shape, F32), jax.ShapeDtypeStruct(cb.shape, F32)],
        scratch_shapes=[pltpu.VMEM((2, SUBLANES, FFN_BLK), F32)], name=name,
        compiler_params=_cp(("parallel", "arbitrary"), 48),
    )(up, up, da, cw, cb)


LRU_TT = 256
LRU_CT = 512
GELU_C = math.sqrt(2.0 / math.pi)
GELU_A = 0.044715


def _gelu(x):
    return 0.5 * x * (1.0 + jnp.tanh(GELU_C * (x + GELU_A * x * x * x)))


def _gelu_grad(x):
    th = jnp.tanh(GELU_C * (x + GELU_A * x * x * x))
    return 0.5 * (1.0 + th) + 0.5 * x * (1.0 - th * th) * GELU_C * (1.0 + 3.0 * GELU_A * x * x)


def _neg_expm1(x):
    poly = -x * (1.0 + x * (0.5 + x * (1.0 / 6 + x * (1.0 / 24 + x * (1.0 / 120)))))
    return jnp.where(x > -0.1, poly, 1.0 - jnp.exp(x))


def _conv4(x, p8, w, b=None):
    s1, s2, s3 = _shift_down(x, p8, 1), _shift_down(x, p8, 2), _shift_down(x, p8, 3)
    y = w[0:1] * s3 + w[1:2] * s2 + w[2:3] * s1 + w[3:4] * x
    return (y if b is None else y + b), (s1, s2, s3)


def _conv4_bwd(dy, head, x, shifts, w):
    s1, s2, s3 = shifts
    dx = w[3:4] * dy + w[2:3] * _shift_up(dy, head, 1) + w[1:2] * _shift_up(dy, head, 2) + w[0:1] * _shift_up(dy, head, 3)
    dws = [jnp.sum(dy * s, axis=0, keepdims=True) for s in (s3, s2, s1, x)]
    return dx, dws


def _blockdiag(x, w_ref, dims=NN):
    nb = x.shape[1] // LANES
    return jnp.concatenate([_bdot(x[:, LANES * i:LANES * (i + 1)], w_ref[i], dims) for i in range(nb)], axis=1)


def _lru_gates(xr, wa_ref, wx_ref, ba, bx, lam):
    r = _sigmoid(_blockdiag(xr, wa_ref) + ba)
    i = _sigmoid(_blockdiag(xr, wx_ref) + bx)
    sp = _softplus(-lam)
    la = -LRU_C * r * sp
    a = jnp.exp(la)
    mult = jnp.sqrt(_neg_expm1(2.0 * la))
    return r, i, sp, a, mult


def _lru_specs(t, tt, ct, order):
    nb = ct // LANES
    cur = pl.BlockSpec((tt, ct), lambda j, n: (order(n), j))
    prev = pl.BlockSpec((SUBLANES, ct), lambda j, n: (_prev8(order(n), tt), j))
    vec = lambda rows: pl.BlockSpec((rows, ct), lambda j, n: (0, j))
    blk = pl.BlockSpec((nb, LANES, LANES), lambda j, n: (j, 0, 0))
    return cur, prev, vec, blk


def _lru_fwd(gate, xpre, cw, cb, wa, ba, wx, bx, lam, comm, *, name):
    t, c = gate.shape
    tt, ct = min(LRU_TT, t), LRU_CT
    nj, nt = c // ct, t // tt
    cur, prev, vec, blk = _lru_specs(t, tt, ct, lambda n: n)
    hbm = pl.BlockSpec(memory_space=pl.ANY)

    def body(gate_ref, x_ref, p_ref, cw_ref, cb_ref, wa_ref, ba_ref, wx_ref, bx_ref, lam_ref, *rest):
        comm_in, (y_ref, hs_ref), comm_out, (carry,), comm_sems = comm.split(rest, n_out=2, n_scratch=1)
        j, n = pl.program_id(0), pl.program_id(1)

        @pl.when(jnp.logical_and(j == 0, n == 0))
        def _():
            comm.start(comm_in, comm_out, comm_sems)

        @pl.when(n == 0)
        def _():
            carry[...] = jnp.zeros_like(carry)

        p8 = jnp.where(n == 0, 0.0, p_ref[...])
        xr, _ = _conv4(x_ref[...], p8, cw_ref[...], cb_ref[...])
        r, i, sp, a, mult = _lru_gates(xr, wa_ref, wx_ref, ba_ref[...], bx_ref[...], lam_ref[...])
        acc_a, acc_b = a, mult * (i * xr)
        d = 1
        while d < tt:
            acc_b = acc_a * _down_fill(acc_b, d, 0.0) + acc_b
            acc_a = acc_a * _down_fill(acc_a, d, 1.0)
            d *= 2
        hs = acc_b + acc_a * carry[0:1]
        carry[...] = jnp.broadcast_to(hs[tt - 1:tt], carry.shape)
        hs_ref[...] = hs
        y_ref[...] = (_gelu(gate_ref[...]) * hs).astype(BF16)

        @pl.when(jnp.logical_and(j == nj - 1, n == nt - 1))
        def _():
            comm.finish(comm_in, comm_out, comm_sems)

    outs = pl.pallas_call(
        body, grid=(nj, nt),
        in_specs=[cur, cur, prev, vec(4), vec(1), blk, vec(1), blk, vec(1), vec(1)] + [hbm] * len(comm.arrays),
        out_specs=[cur, cur] + [hbm] * len(comm.out_shapes),
        out_shape=[jax.ShapeDtypeStruct((t, c), BF16), jax.ShapeDtypeStruct((t, c), F32)] + comm.out_shapes,
        scratch_shapes=[pltpu.VMEM((SUBLANES, ct), F32)] + comm.scratch, name=name,
        compiler_params=_cp(("arbitrary", "arbitrary"), 48),
    )(gate, xpre, xpre, cw, cb, wa, ba, wx, bx, lam, *comm.arrays)
    return outs[0], outs[1], outs[2:]


def _lru_bwd(gate, xpre, hs, dy, cw, cb, wa, ba, wx, bx, lam, *, name):
    t, c = gate.shape
    tt, ct = min(LRU_TT, t), LRU_CT
    nt = t // tt
    cur, prev, vec, blk = _lru_specs(t, tt, ct, lambda n: nt - 1 - n)

    def body(gate_ref, x_ref, p_ref, hs_ref, phs_ref, dy_ref, cw_ref, cb_ref, wa_ref, ba_ref, wx_ref, bx_ref, lam_ref,
             dgate_ref, dx_ref, dcw_ref, dcb_ref, dwa_ref, dba_ref, dwx_ref, dbx_ref, dlam_ref, carry, head):
        n = pl.program_id(1)
        tile0 = n == nt - 1

        @pl.when(n == 0)
        def _():
            for ref in (carry, head, dcw_ref, dcb_ref, dwa_ref, dba_ref, dwx_ref, dbx_ref, dlam_ref):
                ref[...] = jnp.zeros_like(ref)

        xp, cwv, lam = x_ref[...], cw_ref[...], lam_ref[...]
        p8 = jnp.where(tile0, 0.0, p_ref[...])
        xr, shifts = _conv4(xp, p8, cwv, cb_ref[...])
        r, i, sp, a, mult = _lru_gates(xr, wa_ref, wx_ref, ba_ref[...], bx_ref[...], lam)
        gate, hsv, dyv = gate_ref[...], hs_ref[...], dy_ref[...]
        dgate_ref[...] = (dyv * hsv * _gelu_grad(gate)).astype(BF16)
        acc_b = dyv * _gelu(gate) + jnp.where(_rows(a.shape) == tt - 1, carry[0:1], 0.0)
        acc_a = _up_fill(a, 1, 0.0)
        d = 1
        while d < tt:
            acc_b = acc_b + acc_a * _up_fill(acc_b, d, 0.0)
            acc_a = acc_a * _up_fill(acc_a, d, 0.0)
            d *= 2
        gsum = acc_b
        carry[...] = jnp.broadcast_to(a[0:1] * gsum[0:1], carry.shape)
        hprev = _shift_down(hsv, jnp.where(tile0, 0.0, phs_ref[...]), 1)
        da = gsum * hprev
        dmult = gsum * i * xr
        di = gsum * mult * xr
        dxr = gsum * mult * i
        dla = da * a - dmult * (a * a) / mult
        dr = dla * (-LRU_C * sp)
        dlam_ref[...] += jnp.sum(dla * (-LRU_C * r), axis=0, keepdims=True) * (-_sigmoid(-lam))
        dpa = dr * r * (1.0 - r)
        dpx = di * i * (1.0 - i)
        dba_ref[...] += jnp.sum(dpa, axis=0, keepdims=True)
        dbx_ref[...] += jnp.sum(dpx, axis=0, keepdims=True)
        dxr = dxr + _blockdiag(dpa, wa_ref, NT) + _blockdiag(dpx, wx_ref, NT)
        for b in range(ct // LANES):
            sl = slice(LANES * b, LANES * (b + 1))
            dwa_ref[b] += _bdot(xr[:, sl], dpa[:, sl], TN)
            dwx_ref[b] += _bdot(xr[:, sl], dpx[:, sl], TN)
        dx, dws = _conv4_bwd(dxr, head[...], xp, shifts, cwv)
        dx_ref[...] = dx.astype(BF16)
        for k in range(4):
            dcw_ref[k:k + 1, :] += dws[k]
        dcb_ref[...] += jnp.sum(dxr, axis=0, keepdims=True)
        head[...] = dxr[:SUBLANES]

    return pl.pallas_call(
        body, grid=(c // ct, nt),
        in_specs=[cur, cur, prev, cur, prev, cur, vec(4), vec(1), blk, vec(1), blk, vec(1), vec(1)],
        out_specs=[cur, cur, vec(4), vec(1), blk, vec(1), blk, vec(1), vec(1)],
        out_shape=[jax.ShapeDtypeStruct((t, c), BF16)] * 2 + [jax.ShapeDtypeStruct((4, c), F32), jax.ShapeDtypeStruct((1, c), F32),
                   jax.ShapeDtypeStruct(wa.shape, F32), jax.ShapeDtypeStruct((1, c), F32),
                   jax.ShapeDtypeStruct(wx.shape, F32), jax.ShapeDtypeStruct((1, c), F32), jax.ShapeDtypeStruct((1, c), F32)],
        scratch_shapes=[pltpu.VMEM((SUBLANES, ct), F32)] * 2, name=name,
        compiler_params=_cp(("parallel", "arbitrary"), 48),
    )(gate, xpre, xpre, hs, hs, dy, cw, cb, wa, ba, wx, bx, lam)


RET_W = HEADS * HEAD_DIM
HALF = HEAD_DIM // 2


def _ret_tables(t):
    c = RET_CHUNK
    inv_freq = ROPE_BASE ** (-jnp.arange(HALF, dtype=F32) / HALF)
    ang = jnp.arange(t, dtype=jnp.int32).astype(F32)[:, None] * inv_freq[None, :]
    cos, sin = jnp.cos(ang), jnp.sin(ang)
    cosf = jnp.concatenate([cos, cos], axis=1)
    sinf = jnp.concatenate([-sin, sin], axis=1)
    log_gamma = jnp.log1p(-jnp.exp2(-5.0 - jnp.arange(HEADS, dtype=F32)))
    idx = jnp.arange(c, dtype=F32)
    rel = idx[:, None] - idx[None, :]
    causal = rel >= 0
    dmask = jnp.where(causal, jnp.exp(log_gamma[:, None, None] * jnp.where(causal, rel, 0.0)), 0.0)
    ktail = jnp.exp(log_gamma[:, None] * (c - 1 - idx))
    qdec = jnp.exp(log_gamma[:, None] * (idx + 1.0))
    rowtab = jnp.broadcast_to(jnp.stack([ktail, qdec], axis=1)[..., None], (HEADS, 2, c, HEAD_DIM))
    cdec = jnp.broadcast_to(jnp.exp(log_gamma * c)[:, None, None], (HEADS, SUBLANES, HEAD_DIM))
    return cosf, sinf, dmask, rowtab, cdec


def _rotary(x, cosf, sinf):
    return x * cosf + pltpu.roll(x, HALF, 1) * sinf


def _rotary_t(dx, cosf, sinf):
    return dx * cosf + pltpu.roll(dx * sinf, HALF, 1)


def _ret_specs(c, order):
    full = lambda shape: pl.BlockSpec(shape, lambda n: (0,) * len(shape))
    return dict(
        proj=pl.BlockSpec((c, 4 * RET_W), lambda n: (order(n), 0)),
        rot=pl.BlockSpec((c, HEAD_DIM), lambda n: (order(n), 0)),
        dmask=full((HEADS, c, c)), rowtab=full((HEADS, 2, c, HEAD_DIM)), cdec=full((HEADS, SUBLANES, HEAD_DIM)),
        state=pl.BlockSpec((1, HEADS, HEAD_DIM, HEAD_DIM), lambda n: (order(n), 0, 0, 0)),
        half=pl.BlockSpec((c, RET_W), lambda n: (order(n), 0)),
    )


def _ret_head(p_ref, h, cosf, sinf):
    sl = lambda j: slice(j * RET_W + h * HEAD_DIM, j * RET_W + (h + 1) * HEAD_DIM)
    q, k, v, g = p_ref[:, sl(0)], p_ref[:, sl(1)], p_ref[:, sl(2)], p_ref[:, sl(3)]
    return _rotary(q, cosf, sinf), _rotary(k, cosf, sinf) * QSCALE, v, g


def _ret_fwd(proj, tables, comm, *, name):
    t = proj.shape[0]
    c = RET_CHUNK
    nc = t // c
    sp = _ret_specs(c, lambda n: n)
    hbm = pl.BlockSpec(memory_space=pl.ANY)

    def body(p_ref, cos_ref, sin_ref, dm_ref, rt_ref, cd_ref, *rest):
        comm_in, (y_ref, s_ref), comm_out, (state,), comm_sems = comm.split(rest, n_out=2, n_scratch=1)

        @pl.when(pl.program_id(0) == 0)
        def _():
            state[...] = jnp.zeros_like(state)
            comm.start(comm_in, comm_out, comm_sems)

        cosf, sinf = cos_ref[...], sin_ref[...]
        for h in range(HEADS):
            qr, kr, v, g = _ret_head(p_ref, h, cosf, sinf)
            s0 = state[h]
            s_ref[0, h] = s0
            scores = _bdot(qr, kr, NT) * dm_ref[h]
            o = _bdot(scores, v) + _bdot(qr * rt_ref[h, 1], s0)
            state[h] = s0 * cd_ref[h][0:1] + _bdot(kr * rt_ref[h, 0], v, TN)
            rinv = lax.rsqrt(jnp.mean(o * o, axis=-1, keepdims=True) + EPS)
            y_ref[:, h * HEAD_DIM:(h + 1) * HEAD_DIM] = (o * rinv * (g * _sigmoid(g))).astype(BF16)

        @pl.when(pl.program_id(0) == nc - 1)
        def _():
            comm.finish(comm_in, comm_out, comm_sems)

    outs = pl.pallas_call(
        body, grid=(nc,),
        in_specs=[sp["proj"], sp["rot"], sp["rot"], sp["dmask"], sp["rowtab"], sp["cdec"]] + [hbm] * len(comm.arrays),
        out_specs=[sp["half"], sp["state"]] + [hbm] * len(comm.out_shapes),
        out_shape=[jax.ShapeDtypeStruct((t, 2 * RET_W), BF16), jax.ShapeDtypeStruct((nc, HEADS, HEAD_DIM, HEAD_DIM), F32)]
        + comm.out_shapes,
        scratch_shapes=[pltpu.VMEM((HEADS, HEAD_DIM, HEAD_DIM), F32)] + comm.scratch, name=name,
        compiler_params=_cp(("arbitrary",), 48),
    )(proj, *tables, *comm.arrays)
    return outs[0], outs[1], outs[2:]


def _ret_bwd(proj, tables, states, dy, comm, *, name):
    t = proj.shape[0]
    c = RET_CHUNK
    nc = t // c
    sp = _ret_specs(c, lambda n: nc - 1 - n)
    hbm = pl.BlockSpec(memory_space=pl.ANY)

    def body(p_ref, cos_ref, sin_ref, dm_ref, rt_ref, cd_ref, s_ref, dy_ref, *rest):
        comm_in, (dp_ref,), comm_out, (dstate,), comm_sems = comm.split(rest, n_out=1, n_scratch=1)

        @pl.when(pl.program_id(0) == 0)
        def _():
            dstate[...] = jnp.zeros_like(dstate)
            comm.start(comm_in, comm_out, comm_sems)

        cosf, sinf = cos_ref[...], sin_ref[...]
        for h in range(HEADS):
            qr, kr, v, g = _ret_head(p_ref, h, cosf, sinf)
            s0, dm, ktl, qdc = s_ref[0, h], dm_ref[h], rt_ref[h, 0], rt_ref[h, 1]
            scores = _bdot(qr, kr, NT) * dm
            qd, kt = qr * qdc, kr * ktl
            o = _bdot(scores, v) + _bdot(qd, s0)
            rinv = lax.rsqrt(jnp.mean(o * o, axis=-1, keepdims=True) + EPS)
            oh = o * rinv
            sg = _sigmoid(g)
            dyh = dy_ref[:, h * HEAD_DIM:(h + 1) * HEAD_DIM]
            dg = dyh * oh * sg * (1.0 + g * (1.0 - sg))
            dyo = dyh * (g * sg)
            do = rinv * (dyo - oh * jnp.mean(dyo * oh, axis=-1, keepdims=True))
            ds1 = dstate[h]
            dsc = _bdot(do, v, NT) * dm
            dv = _bdot(scores, do, TN) + _bdot(kt, ds1)
            dqr = _bdot(dsc, kr) + _bdot(do, s0, NT) * qdc
            dkr = (_bdot(dsc, qr, TN) + _bdot(v, ds1, NT) * ktl) * QSCALE
            dstate[h] = ds1 * cd_ref[h][0:1] + _bdot(qd, do, TN)
            pieces = (_rotary_t(dqr, cosf, sinf), _rotary_t(dkr, cosf, sinf), dv, dg)
            for j, piece in enumerate(pieces):
                dp_ref[:, j * RET_W + h * HEAD_DIM:j * RET_W + (h + 1) * HEAD_DIM] = piece.astype(BF16)

        @pl.when(pl.program_id(0) == nc - 1)
        def _():
            comm.finish(comm_in, comm_out, comm_sems)

    outs = pl.pallas_call(
        body, grid=(nc,),
        in_specs=[sp["proj"], sp["rot"], sp["rot"], sp["dmask"], sp["rowtab"], sp["cdec"], sp["state"], sp["half"]]
        + [hbm] * len(comm.arrays),
        out_specs=[sp["proj"]] + [hbm] * len(comm.out_shapes),
        out_shape=[jax.ShapeDtypeStruct((t, 8 * RET_W), BF16)] + comm.out_shapes,
        scratch_shapes=[pltpu.VMEM((HEADS, HEAD_DIM, HEAD_DIM), F32)] + comm.scratch, name=name,
        compiler_params=_cp(("arbitrary",), 48),
    )(proj, *tables, states, dy, *comm.arrays)
    return outs[0], outs[1:]


GDN_W = HEADS * HEAD_DIM
GDN_CONV = 3 * GDN_W
NEUMANN_STEPS = 5


def _gdn_gates(ps, al, dt):
    return _sigmoid(ps), -jnp.exp(al) * _softplus(ps + dt)


def _cumsum_rows(x):
    d = 1
    while d < x.shape[0]:
        x = x + _down_fill(x, d, 0.0)
        d *= 2
    return x


def _rev_cumsum_rows(x):
    d = 1
    while d < x.shape[0]:
        x = x + _up_fill(x, d, 0.0)
        d *= 2
    return x


class _Chunk:
    pass


def _gdn_chunk(qc, kc, v, beta, g, s0, inv=None):
    c = GDN_CHUNK
    z = _Chunk()
    z.rq = lax.rsqrt(jnp.sum(qc * qc, axis=-1, keepdims=True) + EPS)
    z.rk = lax.rsqrt(jnp.sum(kc * kc, axis=-1, keepdims=True) + EPS)
    z.qn, z.k = qc * z.rq, kc * z.rk
    z.q = z.qn * QSCALE
    z.v, z.beta = v, beta
    gc = _cumsum_rows(jnp.broadcast_to(g, (c, LANES)))
    ri, ci = _rows((c, c)), _cols((c, c))
    z.tril, z.strict = ri >= ci, ri > ci
    diff = gc[:, :c] - gc.T[:c, :]
    z.decay = jnp.where(z.tril, jnp.exp(jnp.where(z.tril, diff, 0.0)), 0.0)
    z.eg = jnp.exp(gc)
    glast = gc[c - 1:c, :]
    z.egl = jnp.exp(glast - gc)
    z.cd = jnp.exp(glast)
    z.kb = z.k * beta
    both = _bdot(jnp.concatenate([z.kb, z.q], axis=0), z.k, NT)
    z.m, z.qk = both[:c], both[c:]
    if inv is None:
        neg = -jnp.where(z.strict, z.m * z.decay, 0.0)
        inv = (ri == ci).astype(F32) + neg
        pw = neg
        for _ in range(NEUMANN_STEPS):
            pw = _dot3(pw, pw)
            inv = inv + _dot3(inv, pw)
    z.inv = inv
    z.vb, z.kbg = v * beta, z.kb * z.eg
    solved = _dot3(inv, jnp.concatenate([z.vb, z.kbg], axis=1))
    z.u, z.w = solved[:, :HEAD_DIM], solved[:, HEAD_DIM:]
    z.attn = jnp.where(z.tril, z.qk * z.decay, 0.0)
    z.qd, z.kt = z.q * z.eg, z.k * z.egl
    through = _bdot(jnp.concatenate([z.w, z.qd], axis=0), s0)
    z.vnew = z.u - through[:c]
    z.o = through[c:] + _bdot(z.attn, z.vnew)
    z.s1 = s0 * z.cd + _bdot(z.kt, z.vnew, TN)
    return z


def _gdn_chunk_bwd(z, s0, do, ds1):
    c = GDN_CHUNK
    dvnew = _bdot(z.attn, do, TN) + _bdot(z.kt, ds1)
    against = _bdot(do, jnp.concatenate([s0, z.vnew], axis=0), NT)
    dqd = against[:, :HEAD_DIM]
    dattn = jnp.where(z.tril, against[:, HEAD_DIM:], 0.0)
    ds0 = ds1 * z.cd + _bdot(jnp.concatenate([z.qd, -z.w], axis=0), jnp.concatenate([do, dvnew], axis=0), TN)
    dcd = jnp.sum(jnp.sum(s0 * ds1, axis=1, keepdims=True), axis=0, keepdims=True)
    dkt = _bdot(z.vnew, ds1, NT)
    dw = -_bdot(dvnew, s0, NT)
    dsolved = _dot3(z.inv, jnp.concatenate([dvnew, dw], axis=1), TN)
    dvb, dkbg = dsolved[:, :HEAD_DIM], dsolved[:, HEAD_DIM:]
    dl = jnp.where(z.strict, -_bdot(dsolved, jnp.concatenate([z.u, z.w], axis=1), NT), 0.0)
    dml = dl * z.decay
    dqk = dattn * z.decay
    ddecay = (dl * z.m + dattn * z.qk) * z.decay
    stacked = jnp.concatenate([dqk, dml], axis=0)
    onto_k = _bdot(stacked, z.k)
    dq = onto_k[:c] + dqd * z.eg
    dkb = onto_k[c:] + dkbg * z.eg
    dk = _bdot(stacked, jnp.concatenate([z.q, z.kb], axis=0), TN) + dkt * z.egl + dkb * z.beta
    dbeta = jnp.sum(dkb * z.k, axis=-1, keepdims=True) + jnp.sum(dvb * z.v, axis=-1, keepdims=True)
    dv = dvb * z.beta
    colsum = _dot3(ddecay, jnp.ones((c, LANES), F32), TN)
    e = jnp.sum(dkt * z.kt, axis=-1, keepdims=True)
    dgc = (jnp.sum(ddecay, axis=-1, keepdims=True) - colsum
           + jnp.sum(dkbg * z.kbg, axis=-1, keepdims=True) + jnp.sum(dqd * z.qd, axis=-1, keepdims=True) - e)
    dglast = jnp.sum(e, axis=0, keepdims=True) + dcd * z.cd
    dgc = dgc + jnp.where(_rows((c, LANES)) == c - 1, dglast, 0.0)
    dg = _rev_cumsum_rows(dgc)[:, 0:1]
    dqn = dq * QSCALE
    dqc = z.rq * (dqn - z.qn * jnp.sum(dqn * z.qn, axis=-1, keepdims=True))
    dkc = z.rk * (dk - z.k * jnp.sum(dk * z.k, axis=-1, keepdims=True))
    return dqc, dkc, dv, dbeta, dg, ds0


GDN_SUB = 1


def _gdn_specs(c, order):
    full = lambda shape: pl.BlockSpec(shape, lambda n: (0,) * len(shape))
    return dict(
        proj=pl.BlockSpec((c, 4 * GDN_W), lambda n: (order(n), 1)),
        prev=pl.BlockSpec((SUBLANES, 4 * GDN_W), lambda n: (_prev8(order(n), c), 1)),
        small=pl.BlockSpec((c, LANES), lambda n: (order(n), 0)),
        convw=full((4, GDN_CONV)), vec=full((1, LANES)),
        state=pl.BlockSpec((GDN_SUB, HEADS, HEAD_DIM, HEAD_DIM), lambda n: (order(n), 0, 0, 0)),
        inv=pl.BlockSpec((GDN_SUB, HEADS, GDN_CHUNK, GDN_CHUNK), lambda n: (order(n), 0, 0, 0)),
        half=pl.BlockSpec((c, GDN_W), lambda n: (order(n), 1)),
        any=pl.BlockSpec(memory_space=pl.ANY),
    )


def _gdn_fwd(proj, psmall, conv_w, al, dt, gain, y_in, comm, *, name):
    t = proj.shape[0]
    c = GDN_CHUNK
    nc, ns = t // c, t // (c * GDN_SUB)
    sp = _gdn_specs(c * GDN_SUB, lambda n: n)

    def body(p_ref, prev_ref, ps_ref, cw_ref, al_ref, dt_ref, gain_ref, yin_ref, *rest):
        comm_in, (y_ref, s_ref, inv_ref), comm_out, (state,), comm_sems = comm.split(rest, n_out=3, n_scratch=1)
        n = pl.program_id(0)

        @pl.when(n == 0)
        def _():
            state[...] = jnp.zeros_like(state)
            comm.start(comm_in, comm_out, comm_sems)

        p8 = jnp.where(n == 0, 0.0, prev_ref[:, :GDN_CONV])
        pre, _ = _conv4(p_ref[:, :GDN_CONV], p8, cw_ref[...])
        act = pre * _sigmoid(pre)
        beta_all, g_all = _gdn_gates(ps_ref[...], al_ref[...], dt_ref[...])
        gd_all, gain = p_ref[:, GDN_CONV:], gain_ref[...]
        swish = gd_all * _sigmoid(gd_all)
        cur = [state[h] for h in range(HEADS)]
        starts, ys = [], []
        for sub in range(GDN_SUB):
            rows = slice(sub * c, (sub + 1) * c)
            starts.append(list(cur))
            pieces = []
            for h in range(HEADS):
                sl = lambda j: slice(j * GDN_W + h * HEAD_DIM, j * GDN_W + (h + 1) * HEAD_DIM)
                z = _gdn_chunk(act[rows, sl(0)], act[rows, sl(1)], act[rows, sl(2)], beta_all[rows, h:h + 1],
                               g_all[rows, HEADS + h:HEADS + h + 1], cur[h])
                cur[h] = z.s1
                inv_ref[sub, h] = z.inv
                rinv = lax.rsqrt(jnp.mean(z.o * z.o, axis=-1, keepdims=True) + EPS)
                pieces.append(z.o * rinv * gain * swish[rows, sl(0)])
            ys.append(jnp.concatenate(pieces, axis=1))
        y_ref[...] = jnp.concatenate(ys, axis=0).astype(BF16)
        for sub in range(GDN_SUB):
            for h in range(HEADS):
                s_ref[sub, h] = starts[sub][h]
        for h in range(HEADS):
            state[h] = cur[h]

        @pl.when(n == ns - 1)
        def _():
            comm.finish(comm_in, comm_out, comm_sems)

    outs = pl.pallas_call(
        body, grid=(ns,),
        in_specs=[sp["proj"], sp["prev"], sp["small"], sp["convw"], sp["vec"], sp["vec"], sp["vec"], sp["any"]]
        + [sp["any"]] * len(comm.arrays),
        out_specs=[sp["half"], sp["state"], sp["inv"]] + [sp["any"]] * len(comm.out_shapes),
        out_shape=[jax.ShapeDtypeStruct((t, 2 * GDN_W), BF16), jax.ShapeDtypeStruct((nc, HEADS, HEAD_DIM, HEAD_DIM), F32),
                   jax.ShapeDtypeStruct((nc, HEADS, c, c), F32)] + comm.out_shapes,
        scratch_shapes=[pltpu.VMEM((HEADS, HEAD_DIM, HEAD_DIM), F32)] + comm.scratch, name=name,
        input_output_aliases={7: 0}, compiler_params=_cp(("arbitrary",), 48),
    )(proj, proj, psmall, conv_w, al, dt, gain, y_in, *comm.arrays)
    return outs[0], (outs[1], outs[2]), outs[3:]


def _gdn_bwd(proj, psmall, conv_w, al, dt, gain, states, dy, dproj_in, comm, *, name):
    t = proj.shape[0]
    c = GDN_CHUNK
    nc, ns = t // c, t // (c * GDN_SUB)
    sp = _gdn_specs(c * GDN_SUB, lambda n: ns - 1 - n)

    def body(p_ref, prev_ref, ps_ref, cw_ref, al_ref, dt_ref, gain_ref, s_ref, inv_ref, dy_ref, dpin_ref, *rest):
        comm_in, outs, comm_out, (dstate, head), comm_sems = comm.split(rest, n_out=6, n_scratch=2)
        dp_ref, dps_ref, dcw_ref, dal_ref, ddt_ref, dgain_ref = outs
        n = pl.program_id(0)
        first_rows = n == ns - 1

        @pl.when(n == 0)
        def _():
            for ref in (dstate, head, dcw_ref, dal_ref, ddt_ref, dgain_ref):
                ref[...] = jnp.zeros_like(ref)
            comm.start(comm_in, comm_out, comm_sems)

        x, cwv = p_ref[:, :GDN_CONV], cw_ref[...]
        p8 = jnp.where(first_rows, 0.0, prev_ref[:, :GDN_CONV])
        pre, shifts = _conv4(x, p8, cwv)
        sg_pre = _sigmoid(pre)
        act = pre * sg_pre
        ps, alv, dtv, gain = ps_ref[...], al_ref[...], dt_ref[...], gain_ref[...]
        beta_all, g_all = _gdn_gates(ps, alv, dtv)
        lane = _cols((c, LANES))
        dgain = jnp.zeros((1, LANES), F32)
        dcur = [dstate[h] for h in range(HEADS)]
        dact_rows, dbeta_rows, dg_rows = [None] * GDN_SUB, [None] * GDN_SUB, [None] * GDN_SUB
        for sub in reversed(range(GDN_SUB)):
            rows = slice(sub * c, (sub + 1) * c)
            dbeta_all = jnp.zeros((c, LANES), F32)
            dg_all = jnp.zeros((c, LANES), F32)
            dact = [None] * (3 * HEADS)
            for h in range(HEADS):
                sl = lambda j: slice(j * GDN_W + h * HEAD_DIM, j * GDN_W + (h + 1) * HEAD_DIM)
                s0 = s_ref[sub, h]
                z = _gdn_chunk(act[rows, sl(0)], act[rows, sl(1)], act[rows, sl(2)], beta_all[rows, h:h + 1],
                               g_all[rows, HEADS + h:HEADS + h + 1], s0, inv=inv_ref[sub, h])
                rinv = lax.rsqrt(jnp.mean(z.o * z.o, axis=-1, keepdims=True) + EPS)
                oh = z.o * rinv
                gd = p_ref[rows, sl(3)]
                sgd = _sigmoid(gd)
                dyh = dy_ref[rows, sl(0)]
                dgain = dgain + jnp.sum(dyh * oh * (gd * sgd), axis=0, keepdims=True)
                dp_ref[rows, sl(3)] = (dyh * oh * gain * sgd * (1.0 + gd * (1.0 - sgd))).astype(BF16)
                dyo = dyh * gain * (gd * sgd)
                do = rinv * (dyo - oh * jnp.mean(dyo * oh, axis=-1, keepdims=True))
                dqc, dkc, dv, dbeta, dg, dcur[h] = _gdn_chunk_bwd(z, s0, do, dcur[h])
                dact[h], dact[HEADS + h], dact[2 * HEADS + h] = dqc, dkc, dv
                dbeta_all = dbeta_all + jnp.where(lane == h, dbeta, 0.0)
                dg_all = dg_all + jnp.where(lane == HEADS + h, dg, 0.0)
            dact_rows[sub], dbeta_rows[sub], dg_rows[sub] = jnp.concatenate(dact, axis=1), dbeta_all, dg_all
        for h in range(HEADS):
            dstate[h] = dcur[h]
        dbeta_all, dg_all = jnp.concatenate(dbeta_rows, axis=0), jnp.concatenate(dg_rows, axis=0)
        dpre = jnp.concatenate(dact_rows, axis=0) * sg_pre * (1.0 + pre * (1.0 - sg_pre))
        dx, dws = _conv4_bwd(dpre, head[...], x, shifts, cwv)
        dp_ref[:, :GDN_CONV] = dx.astype(BF16)
        for k in range(4):
            dcw_ref[k:k + 1, :] += dws[k]
        head[...] = dpre[:SUBLANES]
        dsp = dg_all * (-jnp.exp(alv)) * _sigmoid(ps + dtv)
        dps_ref[...] = (dbeta_all * beta_all * (1.0 - beta_all) + dsp).astype(BF16)
        ddt_ref[...] += jnp.sum(dsp, axis=0, keepdims=True)
        dal_ref[...] += jnp.sum(dg_all * g_all, axis=0, keepdims=True)
        dgain_ref[...] += dgain

        @pl.when(n == ns - 1)
        def _():
            comm.finish(comm_in, comm_out, comm_sems)

    vec_f32 = jax.ShapeDtypeStruct((1, LANES), F32)
    outs = pl.pallas_call(
        body, grid=(ns,),
        in_specs=[sp["proj"], sp["prev"], sp["small"], sp["convw"], sp["vec"], sp["vec"], sp["vec"], sp["state"], sp["inv"],
                  sp["half"], sp["any"]]
        + [sp["any"]] * len(comm.arrays),
        out_specs=[sp["proj"], sp["small"], sp["convw"], sp["vec"], sp["vec"], sp["vec"]] + [sp["any"]] * len(comm.out_shapes),
        out_shape=[jax.ShapeDtypeStruct((t, 8 * GDN_W), BF16), jax.ShapeDtypeStruct((t, LANES), BF16),
                   jax.ShapeDtypeStruct((4, GDN_CONV), F32), vec_f32, vec_f32, vec_f32] + comm.out_shapes,
        scratch_shapes=[pltpu.VMEM((HEADS, HEAD_DIM, HEAD_DIM), F32), pltpu.VMEM((SUBLANES, GDN_CONV), F32)] + comm.scratch,
        name=name, input_output_aliases={10: 0}, compiler_params=_cp(("arbitrary",), 48),
    )(proj, proj, psmall, conv_w, al, dt, gain, *states, dy, dproj_in, *comm.arrays)
    return outs[:6], outs[6:]


def _here():
    x, y, c = lax.axis_index("x"), lax.axis_index("y"), lax.axis_index("c")
    return x, y, c, [(1 - x, y), (x, 1 - y), (1 - x, 1 - y)]


def _rdma(src, dst, send, recv, k, dev):
    return pltpu.make_async_remote_copy(src_ref=src, dst_ref=dst, send_sem=send.at[k], recv_sem=recv.at[k],
                                        device_id=dev, device_id_type=MESH)


def _dma_sems(n):
    return [pltpu.SemaphoreType.DMA((n,)), pltpu.SemaphoreType.DMA((n,)), pltpu.SemaphoreType.DMA((1,))]


COPY_PIECES = 4
COPY_PIECE_ALIGN = 16


def _row_parts(rows):
    n = COPY_PIECES if rows % (COPY_PIECES * COPY_PIECE_ALIGN) == 0 and rows >= 1024 else 1
    return [pl.ds(q * (rows // n), rows // n) for q in range(n)]


class _AllGather:
    def __init__(self, array):
        self.arrays = [array]
        self.out_shapes = [jax.ShapeDtypeStruct((N_DEV,) + array.shape, array.dtype)]
        self.parts = _row_parts(array.shape[0])
        self.scratch = _dma_sems(7 * len(self.parts))

    def start(self, ins, outs, sems):
        (src,), (out,), (send, recv, loc) = ins, outs, sems
        x, y, c, chips = _here()
        me, n = 4 * x + 2 * y + c, len(self.parts)
        pltpu.make_async_copy(src, out.at[me], loc.at[0]).start()
        for q, part in enumerate(self.parts):
            _rdma(src.at[part], out.at[me, part], send, recv, q, (x, y, 1 - c)).start()
            for j, (cx, cy) in enumerate(chips):
                _rdma(src.at[part], out.at[me, part], send, recv, (1 + j) * n + q, (cx, cy, c)).start()

    def finish(self, ins, outs, sems):
        (src,), (out,), (send, recv, loc) = ins, outs, sems
        x, y, c, chips = _here()
        sibling, me, n = (x, y, 1 - c), 4 * x + 2 * y + c, len(self.parts)
        piece = lambda k, q: _rdma(src.at[self.parts[q]], out.at[me, self.parts[q]], send, recv, k * n + q, sibling)
        for j, (cx, cy) in enumerate(chips):
            for q, part in enumerate(self.parts):
                got = out.at[4 * cx + 2 * cy + c, part]
                piece(1 + j, q).wait_recv()
                _rdma(got, got, send, recv, (4 + j) * n + q, sibling).start()
        for k in (0, 4, 5, 6):
            for q in range(n):
                piece(k, q).wait_recv()
        for k in range(7):
            for q in range(n):
                piece(k, q).wait_send()
        pltpu.make_async_copy(src, out.at[me], loc.at[0]).wait()


class _ChipExchange:
    def __init__(self, array):
        self.arrays = [array]
        self.out_shapes = [jax.ShapeDtypeStruct(array.shape, array.dtype)]
        self.parts = _row_parts(array.shape[1])
        self.scratch = _dma_sems(3 * len(self.parts))

    def _copies(self, ins, outs, sems):
        (src,), (out,), (send, recv, loc) = ins, outs, sems
        x, y, c, chips = _here()
        here, n = 2 * x + y, len(self.parts)
        local = pltpu.make_async_copy(src.at[here], out.at[here], loc.at[0])
        return local, [_rdma(src.at[2 * cx + cy, part], out.at[here, part], send, recv, j * n + q, (cx, cy, c))
                       for j, (cx, cy) in enumerate(chips) for q, part in enumerate(self.parts)]

    def start(self, ins, outs, sems):
        local, remote = self._copies(ins, outs, sems)
        local.start()
        for cp in remote:
            cp.start()

    def finish(self, ins, outs, sems):
        local, remote = self._copies(ins, outs, sems)
        for cp in remote:
            cp.wait()
        local.wait()


class _PairSwap:
    def __init__(self, array):
        self.arrays = [array]
        self.out_shapes = [jax.ShapeDtypeStruct(array.shape[1:], array.dtype)]
        self.parts = _row_parts(array.shape[2])
        self.scratch = _dma_sems(4 * len(self.parts))

    def _copies(self, ins, outs, sems):
        (src,), (theirs,), (send, recv, _) = ins, outs, sems
        x, y, c, _ = _here()
        return [_rdma(src.at[1 - c, p, part], theirs.at[p, part], send, recv, p * len(self.parts) + q, (x, y, 1 - c))
                for p in range(4) for q, part in enumerate(self.parts)]

    def start(self, ins, outs, sems):
        for cp in self._copies(ins, outs, sems):
            cp.start()

    def finish(self, ins, outs, sems):
        for cp in self._copies(ins, outs, sems):
            cp.wait()


class _Comm:
    def __init__(self, ops):
        self.ops = ops
        self.arrays = [a for op in ops for a in op.arrays]
        self.out_shapes = [s for op in ops for s in op.out_shapes]
        self.scratch = [s for op in ops for s in op.scratch]

    def split(self, rest, n_out, n_scratch):
        cuts = np.cumsum([0, len(self.arrays), n_out, len(self.out_shapes), n_scratch, len(self.scratch)])
        assert cuts[-1] == len(rest)
        return tuple(rest[a:b] for a, b in zip(cuts[:-1], cuts[1:]))

    def _each(self, method, ins, outs, sems):
        i = o = s = 0
        for op in self.ops:
            ni, no, ns = len(op.arrays), len(op.out_shapes), len(op.scratch)
            getattr(op, method)(ins[i:i + ni], outs[o:o + no], sems[s:s + ns])
            i, o, s = i + ni, o + no, s + ns

    def start(self, ins, outs, sems):
        self._each("start", ins, outs, sems)

    def finish(self, ins, outs, sems):
        self._each("finish", ins, outs, sems)

    def run(self, name):
        def body(*refs):
            ins, _, outs, _, sems = self.split(refs, 0, 0)
            self.start(ins, outs, sems)
            self.finish(ins, outs, sems)

        hbm = pl.BlockSpec(memory_space=pl.ANY)
        return pl.pallas_call(body, in_specs=[hbm] * len(self.arrays), out_specs=[hbm] * len(self.out_shapes),
                              out_shape=self.out_shapes, scratch_shapes=self.scratch, name=name)(*self.arrays)


def _sum_slots(x, *, name, tr=None):
    n, r, l = x.shape
    tr = r if tr is None else tr

    def body(x_ref, o_ref):
        acc = x_ref[0].astype(F32)
        for s in range(1, n):
            acc = acc + x_ref[s].astype(F32)
        o_ref[...] = acc

    return pl.pallas_call(
        body, grid=(r // tr,), in_specs=[pl.BlockSpec((n, tr, l), lambda i: (0, i, 0))],
        out_specs=pl.BlockSpec((tr, l), lambda i: (i, 0)), out_shape=jax.ShapeDtypeStruct((r, l), F32),
        name=name, compiler_params=_cp(("parallel",), 48),
    )(x)


def _pair_add(both, theirs, *, name, tr):
    _, n, r, l = both.shape

    def body(a_ref, b_ref, o_ref):
        mine = jnp.where(lax.axis_index("c") == 0, a_ref[0], a_ref[1])
        o_ref[...] = (mine.astype(F32) + b_ref[...].astype(F32)).astype(BF16)

    spec = pl.BlockSpec((n, tr, l), lambda i: (0, i, 0))
    return pl.pallas_call(body, grid=(r // tr,), in_specs=[pl.BlockSpec((2, n, tr, l), lambda i: (0, 0, i, 0)), spec], out_specs=spec,
                          out_shape=jax.ShapeDtypeStruct(theirs.shape, BF16), name=name,
                          compiler_params=_cp(("parallel",), 48))(both, theirs)


ADAM_TILE_ELEMS = 512 * 1024


def _adam(w, g, m, v, *, name):
    shape = w.shape
    cols = shape[-1]
    rows = math.prod(shape[:-1]) if len(shape) > 1 else 1
    tr = rows
    if rows * cols > ADAM_TILE_ELEMS:
        tr = max(d for d in range(SUBLANES, ADAM_TILE_ELEMS // cols + 1, SUBLANES) if rows % d == 0)
    c1, c2 = 1.0 - ADAM_B1 ** ADAM_STEP, 1.0 - ADAM_B2 ** ADAM_STEP

    def body(w_ref, g_ref, m_ref, v_ref, d_ref, m2_ref, v2_ref):
        gv = g_ref[...]
        m2 = ADAM_B1 * m_ref[...] + (1.0 - ADAM_B1) * gv
        v2 = ADAM_B2 * v_ref[...] + (1.0 - ADAM_B2) * (gv * gv)
        d_ref[...] = -ADAM_LR * ((m2 / c1) / (jnp.sqrt(v2 / c2) + ADAM_EPS) + ADAM_WD * w_ref[...])
        m2_ref[...] = m2
        v2_ref[...] = v2

    spec = pl.BlockSpec((tr, cols), lambda i: (i, 0))
    outs = pl.pallas_call(
        body, grid=(rows // tr,), in_specs=[spec] * 4, out_specs=[spec] * 3,
        out_shape=[jax.ShapeDtypeStruct((rows, cols), F32)] * 3, name=name, compiler_params=_cp(("parallel",), 48),
    )(*(a.reshape(rows, cols) for a in (w, g, m, v)))
    return tuple(o.reshape(shape) for o in outs)


WEIGHTS = ['norm_mix', 'norm_ffn', 'ret_gdn_w_in', 'gdn_conv_w', 'gdn_a_log', 'gdn_dt_bias', 'gdn_out_gain', 'ret_gdn_w_out',
           'lru_w_in', 'lru_conv_w', 'lru_conv_b', 'lru_w_a', 'lru_b_a', 'lru_w_x', 'lru_b_x', 'lru_lambda', 'lru_w_out',
           'ffn_w_up', 'ffn_conv_w', 'ffn_conv_b', 'ffn_w_down', 'norm_final']
BIG = {'ret_gdn_w_in': ((1, 1024, 513), 2), 'ret_gdn_w_out': ((1, 128, 1024), 1), 'lru_w_in': ((1, 1024, 256), 2),
       'lru_w_out': ((1, 128, 1024), 1), 'ffn_w_up': ((2, 1024, 704), 2), 'ffn_w_down': ((2, 352, 1024), 1)}
SMALL = {'gdn_conv_w': ((1, 4, 192), 2), 'lru_conv_w': ((1, 4, 128), 2), 'lru_conv_b': ((1, 128), 1), 'lru_b_a': ((1, 128), 1),
         'lru_b_x': ((1, 128), 1), 'lru_lambda': ((1, 128), 1), 'ffn_conv_w': ((2, 3, 704), 2)}
REPLICATED = {'norm_mix': (2, 1024), 'norm_ffn': (2, 1024), 'gdn_a_log': (1, 4), 'gdn_dt_bias': (1, 4), 'gdn_out_gain': (1, 128),
              'lru_w_a': (1, 8, 128, 128), 'lru_w_x': (1, 8, 128, 128), 'ffn_conv_b': (2, 5632), 'norm_final': (1024,)}
FIRST = ['ret_gdn_w_in', 'ret_gdn_w_out']
REST = ['lru_w_in', 'lru_w_out', 'ffn_w_up', 'ffn_w_down']
GROUP_ROWS = {FIRST[0]: 5632, REST[0]: 19968}
GROUP_TILE = {FIRST[0]: 512, REST[0]: 1536}
EARLY = {'lru_conv_w': (1, 4, 1024), 'lru_conv_b': (1, 1024), 'lru_b_a': (1, 1024), 'lru_b_x': (1, 1024), 'lru_lambda': (1, 1024),
         'ffn_conv_w': (2, 3, 5632), 'norm_ffn': (2, 1024), 'norm_mix1': (1, 1024), 'lru_w_a': (1, 8, 128, 128),
         'lru_w_x': (1, 8, 128, 128), 'ffn_conv_b': (2, 5632), 'norm_final': (1024,)}
LATE = {'gdn_conv_w': (1, 4, 1536), 'norm_mix0': (1, 1024), 'gdn_a_log': (1, 4), 'gdn_dt_bias': (1, 4), 'gdn_out_gain': (1, 128),
        'loss': (1, 1)}


def _full_shape(shard, axis):
    return tuple(d * N_DEV if i == axis else d for i, d in enumerate(shard))


def _rows_of(n_elems):
    return -(-n_elems // LANES)


def _to_rows(a, lead=()):
    flat = a.reshape(lead + (-1,))
    pad = _rows_of(flat.shape[-1]) * LANES - flat.shape[-1]
    if pad:
        flat = jnp.pad(flat, [(0, 0)] * len(lead) + [(0, pad)])
    return flat.reshape(lead + (-1, LANES))


def _pack(pieces, total_rows, lead=()):
    buf = jnp.concatenate(pieces, axis=len(lead))
    pad = total_rows - buf.shape[len(lead)]
    return jnp.pad(buf, [(0, 0)] * len(lead) + [(0, pad), (0, 0)]) if pad else buf


def _unpack(buf, shapes, lead=()):
    out, off = [], 0
    for shape in shapes:
        n = math.prod(shape)
        rows = _rows_of(n)
        piece = lax.slice_in_dim(buf, off, off + rows, axis=len(lead)).reshape(lead + (rows * LANES,))
        out.append(lax.slice_in_dim(piece, 0, n, axis=len(lead)).reshape(lead + shape))
        off += rows
    return out


def _join_blocks(g, axis):
    m = jnp.moveaxis(g, 0, axis)
    return m.reshape(m.shape[:axis] + (N_DEV * m.shape[axis + 1],) + m.shape[axis + 2:])


def _split_blocks(full, axis):
    s = full.shape
    return jnp.moveaxis(full.reshape(s[:axis] + (N_DEV, s[axis] // N_DEV) + s[axis + 1:]), axis, 0)


def _small_rows(shapes):
    total = sum(_rows_of(math.prod(s)) for s in shapes)
    return -(-total // SUBLANES) * SUBLANES


FFN_TM = 1024
FFN_TN = 512


def _ffn_forward(h, gain, w_up, cw, cb, w_down, tag, comm):
    t, d = h.shape
    tm, tn, blk, nb = min(FFN_TM, t), FFN_TN, FFN_BLK, FFN_NB
    hn = _norm_fwd(h, gain, name=f"ffn{tag}_norm")
    up = _mmx(hn, w_up, dims=NN, grid=(t // tm, 2 * nb, 1), name=f"ffn{tag}_up", tile=(tm, blk),
              a_spec=pl.BlockSpec((tm, d), lambda i, j, k: (i, 0)),
              b_spec=pl.BlockSpec((None, d, blk), lambda i, j, k: (j, 0, 0)),
              o_spec=pl.BlockSpec((None, None, tm, blk), lambda i, j, k: (j // nb, j % nb, i, 0)),
              out_shape=jax.ShapeDtypeStruct((2, nb, t, blk), F32))
    act, comm_out = _ffn_act_fwd(up, cw, cb, comm, name=f"ffn{tag}_act")
    out = _mmx(act, w_down, dims=NN, grid=(t // tm, d // tn, nb), name=f"ffn{tag}_down", tile=(tm, tn), res=h,
               a_spec=pl.BlockSpec((None, tm, blk), lambda i, j, k: (k, i, 0)),
               b_spec=pl.BlockSpec((blk, tn), lambda i, j, k: (k, j)),
               o_spec=pl.BlockSpec((tm, tn), lambda i, j, k: (i, j)),
               out_shape=jax.ShapeDtypeStruct((t, d), F32))
    return out, (hn, up, act), comm_out


def _ffn_backward(dh, h, gain, saved, w_up, cw, cb, w_down, tag):
    hn, up, act = saved
    t, d = h.shape
    tm, tn, blk, nb = min(FFN_TM, t), FFN_TN, FFN_BLK, FFN_NB
    tk = min(FFN_TM, t)
    da = _mmx(dh, w_down, dims=NT, grid=(t // tm, nb, 1), name=f"ffn{tag}_d_act", tile=(tm, blk),
              a_spec=pl.BlockSpec((tm, d), lambda i, j, k: (i, 0)),
              b_spec=pl.BlockSpec((blk, d), lambda i, j, k: (j, 0)),
              o_spec=pl.BlockSpec((None, tm, blk), lambda i, j, k: (j, i, 0)),
              out_shape=jax.ShapeDtypeStruct((nb, t, blk), F32))
    dwd = _mmx(act, dh, dims=TN, grid=(nb, d // tn, t // tk), name=f"ffn{tag}_d_wdown", tile=(blk, tn), split_rows=blk // 2,
               a_spec=pl.BlockSpec((None, tk, blk), lambda i, j, k: (i, k, 0)),
               b_spec=pl.BlockSpec((tk, tn), lambda i, j, k: (k, j)),
               o_spec=pl.BlockSpec((2, None, blk // 2, tn), lambda i, j, k: (0, i, 0, j)),
               out_shape=jax.ShapeDtypeStruct((2, nb, blk // 2, d), BF16))
    dup, dcw, dcb = _ffn_act_bwd(up, da, cw, cb, name=f"ffn{tag}_act_bwd")
    dhn = _mmx(dup, w_up, dims=NT, grid=(t // tm, d // tn, 2 * nb), name=f"ffn{tag}_d_hn", tile=(tm, tn),
               a_spec=pl.BlockSpec((None, None, tm, blk), lambda i, j, k: (k // nb, k % nb, i, 0)),
               b_spec=pl.BlockSpec((None, tn, blk), lambda i, j, k: (k, j, 0)),
               o_spec=pl.BlockSpec((tm, tn), lambda i, j, k: (i, j)),
               out_shape=jax.ShapeDtypeStruct((t, d), F32))
    dwu = _mmx(hn, dup, dims=TN, grid=(1, 2 * nb, t // tk), name=f"ffn{tag}_d_wup", tile=(d, blk),
               a_spec=pl.BlockSpec((tk, d), lambda i, j, k: (k, 0)),
               b_spec=pl.BlockSpec((None, None, tk, blk), lambda i, j, k: (j // nb, j % nb, k, 0)),
               o_spec=pl.BlockSpec((None, None, d, blk), lambda i, j, k: (j % 2, j // 2, 0, 0)),
               out_shape=jax.ShapeDtypeStruct((2, N_DEV // 2, d, blk), BF16))
    dh_in, dgain = _norm_bwd(h, gain, dhn, dh, name=f"ffn{tag}_norm_bwd")
    conv_w = dcw.transpose(2, 0, 1, 3).reshape(3, 2 * nb * blk)
    return dh_in, dict(w_up=dwu, w_down=dwd, conv_w=conv_w, conv_b=dcb.reshape(1, 2 * nb * blk), norm=dgain)


def kernel(x, norm_mix, norm_ffn, ret_gdn_w_in, gdn_conv_w, gdn_a_log, gdn_dt_bias, gdn_out_gain, ret_gdn_w_out, lru_w_in, lru_conv_w, lru_conv_b, lru_w_a, lru_b_a, lru_w_x, lru_b_x, lru_lambda, lru_w_out, ffn_w_up, ffn_conv_w, ffn_conv_b, ffn_w_down, norm_final, loss_target, m_norm_mix, m_norm_ffn, m_ret_gdn_w_in, m_gdn_conv_w, m_gdn_a_log, m_gdn_dt_bias, m_gdn_out_gain, m_ret_gdn_w_out, m_lru_w_in, m_lru_conv_w, m_lru_conv_b, m_lru_w_a, m_lru_b_a, m_lru_w_x, m_lru_b_x, m_lru_lambda, m_lru_w_out, m_ffn_w_up, m_ffn_conv_w, m_ffn_conv_b, m_ffn_w_down, m_norm_final, v_norm_mix, v_norm_ffn, v_ret_gdn_w_in, v_gdn_conv_w, v_gdn_a_log, v_gdn_dt_bias, v_gdn_out_gain, v_ret_gdn_w_out, v_lru_w_in, v_lru_conv_w, v_lru_conv_b, v_lru_w_a, v_lru_b_a, v_lru_w_x, v_lru_b_x, v_lru_lambda, v_lru_w_out, v_ffn_w_up, v_ffn_conv_w, v_ffn_conv_b, v_ffn_w_down, v_norm_final):
    given = dict(locals())
    w = {n: given[n] for n in WEIGHTS}
    me = 4 * lax.axis_index("x") + 2 * lax.axis_index("y") + lax.axis_index("c")
    t = x.shape[1]
    f = D_FF

    first_shards = {'w_in0': ret_gdn_w_in[0]}
    rest_shards = {'w_out0': ret_gdn_w_out[0], 'lru_in': lru_w_in[0], 'lru_out': lru_w_out[0], 'up0': ffn_w_up[0], 'up1': ffn_w_up[1],
                   'down0': ffn_w_down[0], 'down1': ffn_w_down[1]}
    small_shapes = [s for s, _ in SMALL.values()]
    small_buf = _pack([_to_rows(w[n]) for n in SMALL], _small_rows(small_shapes))
    *g_first, g_small = _Comm([_AllGather(a.astype(BF16)) for a in first_shards.values()] + [_AllGather(small_buf)]).run("gather_first")
    hosted = {'retention_fwd': ['w_out0', 'lru_out'], 'deltanet_fwd': ['lru_in', 'up0', 'down0', 'up1'], 'ffn0_act': ['down1'], 'rglru_fwd': []}
    gather_in = {host: _Comm([_AllGather(rest_shards[n].astype(BF16)) for n in names]) for host, names in hosted.items()}
    got = dict(zip(first_shards, g_first))
    small_blocks = dict(zip(SMALL, _unpack(g_small, small_shapes, lead=(N_DEV,))))
    full = {n: _join_blocks(small_blocks[n], SMALL[n][1]) for n in SMALL if n != 'ffn_conv_w'}

    w_main, w_narrow = _join_w_in(got['w_in0'], name="join_w_in")
    fcw = [small_blocks['ffn_conv_w'][:, l].reshape(2, FFN_NB, 3, FFN_BLK) for l in range(2)]
    fcb = [ffn_conv_b[l].reshape(2, FFN_NB, 1, FFN_BLK) for l in range(2)]
    gdn_cw = full['gdn_conv_w'][0]
    al_pad = jnp.pad(gdn_a_log, ((0, 0), (HEADS, LANES - 2 * HEADS)))
    dt_pad = jnp.pad(gdn_dt_bias, ((0, 0), (HEADS, LANES - 2 * HEADS)))
    lru_cw, lru_cb = full['lru_conv_w'][0], full['lru_conv_b']
    lru_ba, lru_bx, lru_lam = full['lru_b_a'], full['lru_b_x'], full['lru_lambda']
    wa, wx = lru_w_a[0], lru_w_x[0]

    h0, target = x[0], loss_target[0]
    hn0 = _norm_fwd(h0, norm_mix[0:1], name="mix0_norm")
    proj = _mm(hn0, w_main, name="mix0_in")
    pnarrow = _mm(hn0, w_narrow, name="mix0_in_narrow")
    tables = _ret_tables(t)
    y0, ret_states, g = _ret_fwd(proj, tables, gather_in['retention_fwd'], name="retention_fwd")
    got.update(zip(hosted['retention_fwd'], g))
    y0, gdn_states, g = _gdn_fwd(proj, pnarrow, gdn_cw, al_pad, dt_pad, gdn_out_gain, y0, gather_in['deltanet_fwd'], name="deltanet_fwd")
    got.update(zip(hosted['deltanet_fwd'], g))
    lru_in = _join_blocks(got['lru_in'], 1)
    lru_in_g, lru_in_x = lru_in[:, :D_MODEL], lru_in[:, D_MODEL:]
    lru_out = got['lru_out'].reshape(D_MODEL, D_MODEL)
    w_out0 = got['w_out0'].reshape(D_MODEL, D_MODEL)
    h1 = _mm(y0, w_out0, res=h0, name="mix0_out")
    h2, ffn0_saved, g = _ffn_forward(h1, norm_ffn[0:1], got['up0'], fcw[0], fcb[0], got['down0'].reshape(D_FF, D_MODEL), 0,
                                     gather_in['ffn0_act'])
    got.update(zip(hosted['ffn0_act'], g))
    hn1 = _norm_fwd(h2, norm_mix[1:2], name="mix1_norm")
    gate = _mm(hn1, lru_in_g, name="mix1_in_gate")
    xpre = _mm(hn1, lru_in_x, name="mix1_in_x")
    y1, hs, g = _lru_fwd(gate, xpre, lru_cw, lru_cb, wa, lru_ba, wx, lru_bx, lru_lam, gather_in['rglru_fwd'], name="rglru_fwd")
    got.update(zip(hosted['rglru_fwd'], g))
    h3 = _mm(y1, lru_out, res=h2, name="mix1_out")
    w_up = [got['up0'], got['up1']]
    down = [got['down0'].reshape(D_FF, D_MODEL), got['down1'].reshape(D_FF, D_MODEL)]
    h4, ffn1_saved, _ = _ffn_forward(h3, norm_ffn[1:2], w_up[1], fcw[1], fcb[1], down[1], 1, _Comm([]))
    dh4, d_norm_final, loss_part = _final_loss(h4, norm_final[None, :], target, name="final_norm_loss")

    dh3, gf1 = _ffn_backward(dh4, h3, norm_ffn[1:2], ffn1_saved, w_up[1], fcw[1], fcb[1], down[1], 1)
    dy1 = _mm(dh3, lru_out, tb=True, name="mix1_d_y")
    d_lru_out = _mm(y1, dh3, ta=True, out_dtype=BF16, name="mix1_d_wout")
    dgate, dxpre, d_lcw, d_lcb, d_wa, d_ba, d_wx, d_bx, d_lam = _lru_bwd(
        gate, xpre, hs, dy1, lru_cw, lru_cb, wa, lru_ba, wx, lru_bx, lru_lam, name="rglru_bwd")
    dhn1 = _mm(dgate, lru_in_g, tb=True, name="mix1_d_hn_gate")
    dhn1 = _mm(dxpre, lru_in_x, tb=True, res=dhn1, name="mix1_d_hn_x")
    d_lru_in = jnp.concatenate([_mm(hn1, dgate, ta=True, out_dtype=BF16, name="mix1_d_win_gate"),
                                _mm(hn1, dxpre, ta=True, out_dtype=BF16, name="mix1_d_win_x")], axis=1)
    dh2, d_mix1 = _norm_bwd(h2, norm_mix[1:2], dhn1, dh3, name="mix1_norm_bwd")
    dh1, gf0 = _ffn_backward(dh2, h1, norm_ffn[0:1], ffn0_saved, w_up[0], fcw[0], fcb[0], down[0], 0)
    dy0 = _mm(dh1, w_out0, tb=True, name="mix0_d_y")
    d_w_out0 = _mm(y0, dh1, ta=True, out_dtype=BF16, name="mix0_d_wout")

    def by_core_chip(full_grad, axis):
        blocks = _split_blocks(full_grad, axis)
        return blocks.reshape((4, 2) + blocks.shape[1:]).transpose(1, 0, 2, 3)

    def pair_add(blocks, theirs):
        return {k: _pair_add(b, o, name=f"pair_add_{k}", tr=row_tile(b.shape[2])) for (k, b), o in zip(blocks.items(), theirs)}

    row_tile = lambda rows: rows if rows <= 512 else 256

    rest_blocks = {'w_out0': by_core_chip(d_w_out0, 0), 'lru_in': by_core_chip(d_lru_in, 1), 'lru_out': by_core_chip(d_lru_out, 0),
                   'up0': gf0['w_up'], 'up1': gf1['w_up'], 'down0': gf0['w_down'], 'down1': gf1['w_down']}
    early = {'lru_conv_w': d_lcw[None], 'lru_conv_b': d_lcb, 'lru_b_a': d_ba, 'lru_b_x': d_bx, 'lru_lambda': d_lam,
             'ffn_conv_w': jnp.stack([gf0['conv_w'], gf1['conv_w']]), 'norm_ffn': jnp.concatenate([gf0['norm'], gf1['norm']], axis=0),
             'norm_mix1': d_mix1, 'lru_w_a': d_wa[None], 'lru_w_x': d_wx[None],
             'ffn_conv_b': jnp.concatenate([gf0['conv_b'], gf1['conv_b']], axis=0), 'norm_final': d_norm_final[0]}
    early_buf = _pack([_to_rows(early[n]) for n in EARLY], _small_rows(list(EARLY.values())))
    dproj, (*theirs, got_early) = _ret_bwd(proj, tables, ret_states, dy0,
                                           _Comm([_PairSwap(b) for b in rest_blocks.values()] + [_AllGather(early_buf)]), name="retention_bwd")
    z_rest = pair_add(rest_blocks, theirs)
    (dproj, dnarrow, d_gcw, d_alog, d_dtb, d_gain), w_rest = _gdn_bwd(
        proj, pnarrow, gdn_cw, al_pad, dt_pad, gdn_out_gain, gdn_states, dy0, dproj,
        _Comm([_ChipExchange(z) for z in z_rest.values()]), name="deltanet_bwd")
    dhn0 = _mm(dproj, w_main, tb=True, name="mix0_d_hn")
    dhn0 = _mm(dnarrow, w_narrow, tb=True, res=dhn0, name="mix0_d_hn_narrow")
    d_w_main = _mm(hn0, dproj, ta=True, out_dtype=BF16, name="mix0_d_win")
    d_w_narrow = _mm(hn0, dnarrow, ta=True, out_dtype=BF16, name="mix0_d_win_narrow")
    dx, d_mix0 = _norm_bwd(h0, norm_mix[0:1], dhn0, dh1, name="mix0_norm_bwd")

    first_blocks = {'w_in0': _split_w_in(d_w_main, d_w_narrow, name="split_d_w_in")}
    z_first = pair_add(first_blocks, _Comm([_PairSwap(b) for b in first_blocks.values()]).run("pair_swap_first"))
    late = {'gdn_conv_w': d_gcw[None], 'norm_mix0': d_mix0, 'gdn_a_log': d_alog[:, HEADS:2 * HEADS],
            'gdn_dt_bias': d_dtb[:, HEADS:2 * HEADS], 'gdn_out_gain': d_gain, 'loss': loss_part}
    late_buf = _pack([_to_rows(late[n]) for n in LATE], _small_rows(list(LATE.values())))
    *w_first, got_late = _Comm([_ChipExchange(z) for z in z_first.values()] + [_AllGather(late_buf)]).run("exchange_first")

    summed = {k: _sum_slots(blocks, name=f"sum_blocks_{k}", tr=row_tile(blocks.shape[1]))
              for k, blocks in list(zip(z_rest, w_rest)) + list(zip(z_first, w_first))}
    grads = {'ret_gdn_w_in': summed['w_in0'][None], 'ret_gdn_w_out': summed['w_out0'][None], 'lru_w_in': summed['lru_in'][None],
             'lru_w_out': summed['lru_out'][None], 'ffn_w_up': jnp.stack([summed['up0'], summed['up1']]),
             'ffn_w_down': jnp.stack([summed['down0'], summed['down1']])}
    partial = dict(zip(EARLY, _unpack(_sum_slots(got_early, name="sum_partials_early"), list(EARLY.values()))))
    partial.update(zip(LATE, _unpack(_sum_slots(got_late, name="sum_partials_late"), list(LATE.values()))))
    partial['norm_mix'] = jnp.concatenate([partial.pop('norm_mix0'), partial.pop('norm_mix1')], axis=0)
    loss = partial.pop('loss')[0, 0]
    for n, g_full in partial.items():
        if n in SMALL:
            shard, axis = SMALL[n]
            g_full = lax.dynamic_slice_in_dim(g_full, me * shard[axis], shard[axis], axis=axis)
        grads[n] = g_full

    delta, new_m, new_v = {}, {}, {}
    for n in WEIGHTS:
        delta[n], new_m[n], new_v[n] = _adam(w[n], grads[n], given["m_" + n], given["v_" + n], name=f"adamw_{n}")
    return (loss, dx[None], *[grads[n] for n in WEIGHTS], *[delta[n] for n in WEIGHTS],
            *[new_m[n] for n in WEIGHTS], *[new_v[n] for n in WEIGHTS])
```

```python
import math

import numpy as np
import jax
import jax.numpy as jnp
from jax import lax
from jax.experimental import pallas as pl
from jax.experimental.pallas import tpu as pltpu

F32 = jnp.float32
BF16 = jnp.bfloat16
MESH = pl.DeviceIdType.MESH

N_DEV = 8
LANES = 128
SUBLANES = 8
EPS = 1e-6
D_MODEL = 1024
HEADS = 4
HEAD_DIM = 128
RET_CHUNK = 128
GDN_CHUNK = 64
ROPE_BASE = 10000.0
LRU_C = 8.0
D_FF = 2816
MAIN_IN = 4096
SMALL_IN = 8
QSCALE = HEAD_DIM ** -0.5

ADAM_LR, ADAM_B1, ADAM_B2, ADAM_EPS, ADAM_WD, ADAM_STEP = 0.001, 0.9, 0.999, 1e-08, 0.01, 10


def _cp(sem=None, vmem_mb=None):
    kw = {}
    if sem is not None:
        kw["dimension_semantics"] = sem
    if vmem_mb is not None:
        kw["vmem_limit_bytes"] = vmem_mb << 20
    return pltpu.CompilerParams(**kw)


def _rows(shape):
    return lax.broadcasted_iota(jnp.int32, shape, 0)


def _cols(shape):
    return lax.broadcasted_iota(jnp.int32, shape, 1)


def _shift_down(cur, prev8, s):
    if s == 0:
        return cur
    rc = pltpu.roll(cur, s, 0)
    rp = pltpu.roll(prev8, s, 0)
    top = jnp.where(_rows(prev8.shape) < s, rp, rc[:SUBLANES])
    return jnp.concatenate([top, rc[SUBLANES:]], axis=0)


def _shift_up(cur, next8, s):
    if s == 0:
        return cur
    tt = cur.shape[0]
    rc = pltpu.roll(cur, tt - s, 0)
    rn = pltpu.roll(next8, SUBLANES - s, 0)
    bot = jnp.where(_rows(next8.shape) >= SUBLANES - s, rn, rc[tt - SUBLANES:])
    return jnp.concatenate([rc[:tt - SUBLANES], bot], axis=0)


def _down_fill(x, d, fill):
    return jnp.where(_rows(x.shape) < d, fill, pltpu.roll(x, d, 0))


def _up_fill(x, d, fill):
    tt = x.shape[0]
    return jnp.where(_rows(x.shape) >= tt - d, fill, pltpu.roll(x, tt - d, 0))


def _sigmoid(x):
    return 1.0 / (1.0 + jnp.exp(-x))


def _softplus(x):
    return jnp.maximum(x, 0.0) + jnp.log(1.0 + jnp.exp(-jnp.abs(x)))


def _dot(a, b, dims=(((1,), (0,)), ((), ())), precision=None):
    return lax.dot_general(a, b, dims, preferred_element_type=F32, precision=precision)


NN = (((1,), (0,)), ((), ()))
NT = (((1,), (1,)), ((), ()))
TN = (((0,), (0,)), ((), ()))


def _bdot(a, b, dims=NN):
    return _dot(a.astype(BF16), b.astype(BF16), dims)


def _split(a):
    hi = a.astype(BF16)
    return hi, (a - hi.astype(F32)).astype(BF16)


def _dot3(a, b, dims=NN):
    ah, al = _split(a)
    bh, bl = _split(b)
    return _dot(ah, bh, dims) + (_dot(ah, bl, dims) + _dot(al, bh, dims))


def _tile(dim, target):
    if dim <= target:
        return dim
    best = None
    for c in range(LANES, target + 1, LANES):
        if dim % c == 0:
            best = c
    assert best is not None, (dim, target)
    return best


def _mm(a, b, *, name, ta=False, tb=False, out_dtype=F32, res=None, tm=2048, tn=512, tk=1024):
    m, k = (a.shape[1], a.shape[0]) if ta else a.shape
    n = b.shape[0] if tb else b.shape[1]
    tn, tk = _tile(n, tn), _tile(k, tk)
    tm = _tile(m, tm if max(tn, tk) <= 1024 else tm // 2)
    nk = k // tk
    dims = (((0 if ta else 1,), (1 if tb else 0,)), ((), ()))

    def body(*refs):
        a_ref, b_ref = refs[:2]
        r_ref = refs[2] if res is not None else None
        o_ref = refs[3] if res is not None else refs[2]
        acc = refs[-1]
        kk = pl.program_id(2)
        part = _bdot(a_ref[...], b_ref[...], dims)

        def finish(r):
            if res is not None:
                r = r + r_ref[...]
            o_ref[...] = r.astype(out_dtype)

        if nk == 1:
            finish(part)
            return

        @pl.when(kk == 0)
        def _():
            acc[...] = part

        @pl.when(jnp.logical_and(kk > 0, kk < nk - 1))
        def _():
            acc[...] += part

        @pl.when(kk == nk - 1)
        def _():
            finish(acc[...] + part)

    a_spec = pl.BlockSpec((tk, tm), lambda i, j, kk: (kk, i)) if ta else pl.BlockSpec((tm, tk), lambda i, j, kk: (i, kk))
    b_spec = pl.BlockSpec((tn, tk), lambda i, j, kk: (j, kk)) if tb else pl.BlockSpec((tk, tn), lambda i, j, kk: (kk, j))
    o_spec = pl.BlockSpec((tm, tn), lambda i, j, kk: (i, j))
    in_specs = [a_spec, b_spec] + ([o_spec] if res is not None else [])
    args = (a, b) + ((res,) if res is not None else ())
    return pl.pallas_call(
        body, grid=(m // tm, n // tn, nk), in_specs=in_specs, out_specs=o_spec,
        out_shape=jax.ShapeDtypeStruct((m, n), out_dtype),
        scratch_shapes=[pltpu.VMEM((tm, tn), F32)] if nk > 1 else [], name=name,
        compiler_params=_cp(("parallel", "parallel", "arbitrary"), 56),
    )(*args)


def _mmx(a, b, *, dims, grid, a_spec, b_spec, o_spec, out_shape, tile, name, res=None, split_rows=None, pairs=False):
    nk = grid[-1]

    def body(*refs):
        a_ref, b_ref = refs[:2]
        r_ref = refs[2] if res is not None else None
        o_ref = refs[3] if res is not None else refs[2]
        acc = refs[-1]
        if pairs:
            part = _bdot(a_ref[0], b_ref[0], dims) + _bdot(a_ref[1], b_ref[1], dims)
        else:
            part = _bdot(a_ref[...], b_ref[...], dims)

        def finish(r):
            if res is not None:
                r = r + r_ref[...]
            if split_rows is None:
                o_ref[...] = r.astype(o_ref.dtype)
            else:
                o_ref[0] = r[:split_rows].astype(o_ref.dtype)
                o_ref[1] = r[split_rows:].astype(o_ref.dtype)

        if nk == 1:
            finish(part)
            return
        kk = pl.program_id(len(grid) - 1)

        @pl.when(kk == 0)
        def _():
            acc[...] = part

        @pl.when(jnp.logical_and(kk > 0, kk < nk - 1))
        def _():
            acc[...] += part

        @pl.when(kk == nk - 1)
        def _():
            finish(acc[...] + part)

    args = (a, b) + ((res,) if res is not None else ())
    return pl.pallas_call(
        body, grid=grid, in_specs=[a_spec, b_spec] + ([o_spec] if res is not None else []), out_specs=o_spec,
        out_shape=out_shape, scratch_shapes=[pltpu.VMEM(tile, F32)] if nk > 1 else [], name=name,
        compiler_params=_cp(("parallel",) * (len(grid) - 1) + ("arbitrary",), 56),
    )(*args)


W_IN_BLK = 513
W_IN_TR = 256


def _join_w_in(blocks, *, name):
    _, d, _ = blocks.shape
    tr = W_IN_TR

    def body(x_ref, main_ref, narrow_ref):
        for m in range(MAIN_IN // LANES):
            lo = LANES * m
            dev, off = divmod(lo, W_IN_BLK)
            if off + LANES <= W_IN_BLK:
                main_ref[:, lo:lo + LANES] = x_ref[dev, :, off:off + LANES]
            else:
                main_ref[:, lo:lo + LANES] = jnp.concatenate(
                    [x_ref[dev, :, off:W_IN_BLK], x_ref[dev + 1, :, 0:LANES - (W_IN_BLK - off)]], axis=1)
        tail = x_ref[N_DEV - 1, :, W_IN_BLK - SMALL_IN:W_IN_BLK]
        narrow_ref[...] = jnp.concatenate([tail, jnp.zeros((tr, LANES - SMALL_IN), tail.dtype)], axis=1)

    return pl.pallas_call(
        body, grid=(d // tr,), in_specs=[pl.BlockSpec((N_DEV, tr, W_IN_BLK), lambda i: (0, i, 0))],
        out_specs=[pl.BlockSpec((tr, MAIN_IN), lambda i: (i, 0)), pl.BlockSpec((tr, LANES), lambda i: (i, 0))],
        out_shape=[jax.ShapeDtypeStruct((d, MAIN_IN), blocks.dtype), jax.ShapeDtypeStruct((d, LANES), blocks.dtype)],
        name=name, compiler_params=_cp(("parallel",), 48),
    )(blocks)


def _split_w_in(main, narrow, *, name):
    d = main.shape[0]
    tr = W_IN_TR

    def body(m_ref, n_ref, o_ref):
        for dev in range(N_DEV):
            lo = W_IN_BLK * dev
            if dev < N_DEV - 1:
                piece = m_ref[:, lo:lo + W_IN_BLK]
            else:
                piece = jnp.concatenate([m_ref[:, lo:MAIN_IN], n_ref[:, 0:SMALL_IN]], axis=1)
            o_ref[dev % 2, dev // 2] = piece

    return pl.pallas_call(
        body, grid=(d // tr,),
        in_specs=[pl.BlockSpec((tr, MAIN_IN), lambda i: (i, 0)), pl.BlockSpec((tr, LANES), lambda i: (i, 0))],
        out_specs=pl.BlockSpec((2, N_DEV // 2, tr, W_IN_BLK), lambda i: (0, 0, i, 0)),
        out_shape=jax.ShapeDtypeStruct((2, N_DEV // 2, d, W_IN_BLK), main.dtype),
        name=name, compiler_params=_cp(("parallel",), 48),
    )(main, narrow)


def _norm_fwd(h, gain, *, name, tt=512):
    t, d = h.shape
    tt = min(tt, t)

    def body(h_ref, g_ref, o_ref):
        x = h_ref[...]
        r = lax.rsqrt(jnp.mean(x * x, axis=-1, keepdims=True) + EPS)
        o_ref[...] = (x * r * g_ref[...]).astype(BF16)

    row = pl.BlockSpec((tt, d), lambda i: (i, 0))
    return pl.pallas_call(
        body, grid=(t // tt,), in_specs=[row, pl.BlockSpec((1, d), lambda i: (0, 0))], out_specs=row,
        out_shape=jax.ShapeDtypeStruct((t, d), BF16), name=name, compiler_params=_cp(("parallel",), 48),
    )(h, gain)


def _norm_bwd(h, gain, dhn, dres, *, name, tt=512):
    t, d = h.shape
    tt = min(tt, t)

    def body(h_ref, g_ref, dy_ref, dr_ref, dx_ref, dg_ref):
        x, dy = h_ref[...], dy_ref[...]
        r = lax.rsqrt(jnp.mean(x * x, axis=-1, keepdims=True) + EPS)
        xh = x * r

        @pl.when(pl.program_id(0) == 0)
        def _():
            dg_ref[...] = jnp.zeros_like(dg_ref)

        dg_ref[...] += jnp.sum(dy * xh, axis=0, keepdims=True)
        dxh = dy * g_ref[...]
        dx_ref[...] = dr_ref[...] + r * (dxh - xh * jnp.mean(dxh * xh, axis=-1, keepdims=True))

    row = pl.BlockSpec((tt, d), lambda i: (i, 0))
    vec = pl.BlockSpec((1, d), lambda i: (0, 0))
    return pl.pallas_call(
        body, grid=(t // tt,), in_specs=[row, vec, row, row], out_specs=[row, vec],
        out_shape=[jax.ShapeDtypeStruct((t, d), F32), jax.ShapeDtypeStruct((1, d), F32)],
        name=name, compiler_params=_cp(("arbitrary",), 48),
    )(h, gain, dhn, dres)


def _final_loss(h, gain, target, *, name, tt=512):
    t, d = h.shape
    tt = min(tt, t)

    def body(h_ref, g_ref, tg_ref, dx_ref, dg_ref, loss_ref):
        x = h_ref[...]
        r = lax.rsqrt(jnp.mean(x * x, axis=-1, keepdims=True) + EPS)
        xh = x * r
        err = xh * g_ref[...] - tg_ref[...]

        @pl.when(pl.program_id(0) == 0)
        def _():
            dg_ref[...] = jnp.zeros_like(dg_ref)
            loss_ref[...] = jnp.zeros_like(loss_ref)

        loss_ref[...] += 0.5 * jnp.sum(jnp.mean(err * err, axis=-1, keepdims=True), axis=0, keepdims=True)
        dy = err * (1.0 / d)
        dg_ref[...] += jnp.sum(dy * xh, axis=0, keepdims=True)
        dxh = dy * g_ref[...]
        dx_ref[...] = r * (dxh - xh * jnp.mean(dxh * xh, axis=-1, keepdims=True))

    row = pl.BlockSpec((tt, d), lambda i: (i, 0))
    vec = pl.BlockSpec((1, d), lambda i: (0, 0))
    return pl.pallas_call(
        body, grid=(t // tt,), in_specs=[row, vec, row],
        out_specs=[row, vec, pl.BlockSpec((1, 1), lambda i: (0, 0))],
        out_shape=[jax.ShapeDtypeStruct((t, d), F32), jax.ShapeDtypeStruct((1, d), F32), jax.ShapeDtypeStruct((1, 1), F32)],
        name=name, compiler_params=_cp(("arbitrary",), 48),
    )(h, gain, target)


FFN_BLK = 704
FFN_NB = 4
FFN_TT = 512


def _prev8(n, tt):
    return jnp.maximum(n * (tt // SUBLANES) - 1, 0)


def _ffn_conv(cur, prev8, w, b):
    s1 = _shift_down(cur, prev8, 1)
    s2 = _shift_down(cur, prev8, 2)
    return w[0:1] * s2 + w[1:2] * s1 + w[2:3] * cur + b, s1, s2


def _ffn_specs(t, tt, order):
    pair = lambda rows, row_index: pl.BlockSpec((2, None, rows, FFN_BLK), lambda j, n: (0, j, row_index(n), 0))
    return dict(cur=pair(tt, order), prev=pair(SUBLANES, lambda n: _prev8(order(n), tt)), w=pair(3, lambda n: 0), b=pair(1, lambda n: 0),
                one=pl.BlockSpec((None, tt, FFN_BLK), lambda j, n: (j, order(n), 0)))


def _ffn_act_fwd(up, cw, cb, comm, *, name):
    t = up.shape[2]
    tt = min(FFN_TT, t)
    nt = t // tt
    sp = _ffn_specs(t, tt, lambda n: n)
    hbm = pl.BlockSpec(memory_space=pl.ANY)

    def body(u_ref, p_ref, w_ref, b_ref, *rest):
        comm_in, (o_ref,), comm_out, _, comm_sems = comm.split(rest, n_out=1, n_scratch=0)
        j, n = pl.program_id(0), pl.program_id(1)

        @pl.when(jnp.logical_and(j == 0, n == 0))
        def _():
            comm.start(comm_in, comm_out, comm_sems)

        first = n == 0
        gate, _, _ = _ffn_conv(u_ref[0], jnp.where(first, 0.0, p_ref[0]), w_ref[0], b_ref[0])
        val, _, _ = _ffn_conv(u_ref[1], jnp.where(first, 0.0, p_ref[1]), w_ref[1], b_ref[1])
        o_ref[...] = (gate * _sigmoid(gate) * val).astype(BF16)

        @pl.when(jnp.logical_and(j == FFN_NB - 1, n == nt - 1))
        def _():
            comm.finish(comm_in, comm_out, comm_sems)

    outs = pl.pallas_call(
        body, grid=(FFN_NB, nt), in_specs=[sp["cur"], sp["prev"], sp["w"], sp["b"]] + [hbm] * len(comm.arrays),
        out_specs=[sp["one"]] + [hbm] * len(comm.out_shapes),
        out_shape=[jax.ShapeDtypeStruct((FFN_NB, t, FFN_BLK), BF16)] + comm.out_shapes, scratch_shapes=comm.scratch, name=name,
        compiler_params=_cp(("arbitrary", "arbitrary"), 48),
    )(up, up, cw, cb, *comm.arrays)
    return outs[0], outs[1:]


def _ffn_act_bwd(up, da, cw, cb, *, name):
    t = up.shape[2]
    tt = min(FFN_TT, t)
    nt = t // tt
    sp = _ffn_specs(t, tt, lambda n: nt - 1 - n)

    def body(u_ref, p_ref, da_ref, w_ref, b_ref, du_ref, dw_ref, db_ref, head):
        n = pl.program_id(1)
        tile0 = n == nt - 1

        @pl.when(n == 0)
        def _():
            for r in (head, dw_ref, db_ref):
                r[...] = jnp.zeros_like(r)

        convs = [_ffn_conv(u_ref[s], jnp.where(tile0, 0.0, p_ref[s]), w_ref[s], b_ref[s]) for s in range(2)]
        gate, val = convs[0][0], convs[1][0]
        d = da_ref[...]
        sg = _sigmoid(gate)
        dcs = (d * val * sg * (1.0 + gate * (1.0 - sg)), d * gate * sg)
        for s in range(2):
            dc, w, hd = dcs[s], w_ref[s], head[s]
            _, x1, x2 = convs[s]
            du_ref[s] = (w[2:3] * dc + w[1:2] * _shift_up(dc, hd, 1) + w[0:1] * _shift_up(dc, hd, 2)).astype(BF16)
            dw_ref[s, 0:1, :] += jnp.sum(dc * x2, axis=0, keepdims=True)
            dw_ref[s, 1:2, :] += jnp.sum(dc * x1, axis=0, keepdims=True)
            dw_ref[s, 2:3, :] += jnp.sum(dc * u_ref[s], axis=0, keepdims=True)
            db_ref[s] += jnp.sum(dc, axis=0, keepdims=True)
            head[s] = dc[:SUBLANES]

    return pl.pallas_call(
        body, grid=(FFN_NB, nt), in_specs=[sp["cur"], sp["prev"], sp["one"], sp["w"], sp["b"]],
        out_specs=[sp["cur"], sp["w"], sp["b"]],
        out_shape=[jax.ShapeDtypeStruct(up.shape, BF16), jax.ShapeDtypeStruct(cw.shape, F32), jax.ShapeDtypeStruct(cb.shape, F32)],
        scratch_shapes=[pltpu.VMEM((2, SUBLANES, FFN_BLK), F32)], name=name,
        compiler_params=_cp(("parallel", "arbitrary"), 48),
    )(up, up, da, cw, cb)


LRU_TT = 256
LRU_CT = 512
GELU_C = math.sqrt(2.0 / math.pi)
GELU_A = 0.044715


def _gelu(x):
    return 0.5 * x * (1.0 + jnp.tanh(GELU_C * (x + GELU_A * x * x * x)))


def _gelu_grad(x):
    th = jnp.tanh(GELU_C * (x + GELU_A * x * x * x))
    return 0.5 * (1.0 + th) + 0.5 * x * (1.0 - th * th) * GELU_C * (1.0 + 3.0 * GELU_A * x * x)


def _neg_expm1(x):
    poly = -x * (1.0 + x * (0.5 + x * (1.0 / 6 + x * (1.0 / 24 + x * (1.0 / 120)))))
    return jnp.where(x > -0.1, poly, 1.0 - jnp.exp(x))


def _conv4(x, p8, w, b=None):
    s1, s2, s3 = _shift_down(x, p8, 1), _shift_down(x, p8, 2), _shift_down(x, p8, 3)
    y = w[0:1] * s3 + w[1:2] * s2 + w[2:3] * s1 + w[3:4] * x
    return (y if b is None else y + b), (s1, s2, s3)


def _conv4_bwd(dy, head, x, shifts, w):
    s1, s2, s3 = shifts
    dx = w[3:4] * dy + w[2:3] * _shift_up(dy, head, 1) + w[1:2] * _shift_up(dy, head, 2) + w[0:1] * _shift_up(dy, head, 3)
    dws = [jnp.sum(dy * s, axis=0, keepdims=True) for s in (s3, s2, s1, x)]
    return dx, dws


def _blockdiag(x, w_ref, dims=NN):
    nb = x.shape[1] // LANES
    return jnp.concatenate([_bdot(x[:, LANES * i:LANES * (i + 1)], w_ref[i], dims) for i in range(nb)], axis=1)


def _lru_gates(xr, wa_ref, wx_ref, ba, bx, lam):
    r = _sigmoid(_blockdiag(xr, wa_ref) + ba)
    i = _sigmoid(_blockdiag(xr, wx_ref) + bx)
    sp = _softplus(-lam)
    la = -LRU_C * r * sp
    a = jnp.exp(la)
    mult = jnp.sqrt(_neg_expm1(2.0 * la))
    return r, i, sp, a, mult


def _lru_specs(t, tt, ct, order):
    nb = ct // LANES
    cur = pl.BlockSpec((tt, ct), lambda j, n: (order(n), j))
    prev = pl.BlockSpec((SUBLANES, ct), lambda j, n: (_prev8(order(n), tt), j))
    vec = lambda rows: pl.BlockSpec((rows, ct), lambda j, n: (0, j))
    blk = pl.BlockSpec((nb, LANES, LANES), lambda j, n: (j, 0, 0))
    return cur, prev, vec, blk


def _lru_fwd(gate, xpre, cw, cb, wa, ba, wx, bx, lam, comm, *, name):
    t, c = gate.shape
    tt, ct = min(LRU_TT, t), LRU_CT
    nj, nt = c // ct, t // tt
    cur, prev, vec, blk = _lru_specs(t, tt, ct, lambda n: n)
    hbm = pl.BlockSpec(memory_space=pl.ANY)

    def body(gate_ref, x_ref, p_ref, cw_ref, cb_ref, wa_ref, ba_ref, wx_ref, bx_ref, lam_ref, *rest):
        comm_in, (y_ref, hs_ref), comm_out, (carry,), comm_sems = comm.split(rest, n_out=2, n_scratch=1)
        j, n = pl.program_id(0), pl.program_id(1)

        @pl.when(jnp.logical_and(j == 0, n == 0))
        def _():
            comm.start(comm_in, comm_out, comm_sems)

        @pl.when(n == 0)
        def _():
            carry[...] = jnp.zeros_like(carry)

        p8 = jnp.where(n == 0, 0.0, p_ref[...])
        xr, _ = _conv4(x_ref[...], p8, cw_ref[...], cb_ref[...])
        r, i, sp, a, mult = _lru_gates(xr, wa_ref, wx_ref, ba_ref[...], bx_ref[...], lam_ref[...])
        acc_a, acc_b = a, mult * (i * xr)
        d = 1
        while d < tt:
            acc_b = acc_a * _down_fill(acc_b, d, 0.0) + acc_b
            acc_a = acc_a * _down_fill(acc_a, d, 1.0)
            d *= 2
        hs = acc_b + acc_a * carry[0:1]
        carry[...] = jnp.broadcast_to(hs[tt - 1:tt], carry.shape)
        hs_ref[...] = hs
        y_ref[...] = (_gelu(gate_ref[...]) * hs).astype(BF16)

        @pl.when(jnp.logical_and(j == nj - 1, n == nt - 1))
        def _():
            comm.finish(comm_in, comm_out, comm_sems)

    outs = pl.pallas_call(
        body, grid=(nj, nt),
        in_specs=[cur, cur, prev, vec(4), vec(1), blk, vec(1), blk, vec(1), vec(1)] + [hbm] * len(comm.arrays),
        out_specs=[cur, cur] + [hbm] * len(comm.out_shapes),
        out_shape=[jax.ShapeDtypeStruct((t, c), BF16), jax.ShapeDtypeStruct((t, c), F32)] + comm.out_shapes,
        scratch_shapes=[pltpu.VMEM((SUBLANES, ct), F32)] + comm.scratch, name=name,
        compiler_params=_cp(("arbitrary", "arbitrary"), 48),
    )(gate, xpre, xpre, cw, cb, wa, ba, wx, bx, lam, *comm.arrays)
    return outs[0], outs[1], outs[2:]


def _lru_bwd(gate, xpre, hs, dy, cw, cb, wa, ba, wx, bx, lam, *, name):
    t, c = gate.shape
    tt, ct = min(LRU_TT, t), LRU_CT
    nt = t // tt
    cur, prev, vec, blk = _lru_specs(t, tt, ct, lambda n: nt - 1 - n)

    def body(gate_ref, x_ref, p_ref, hs_ref, phs_ref, dy_ref, cw_ref, cb_ref, wa_ref, ba_ref, wx_ref, bx_ref, lam_ref,
             dgate_ref, dx_ref, dcw_ref, dcb_ref, dwa_ref, dba_ref, dwx_ref, dbx_ref, dlam_ref, carry, head):
        n = pl.program_id(1)
        tile0 = n == nt - 1

        @pl.when(n == 0)
        def _():
            for ref in (carry, head, dcw_ref, dcb_ref, dwa_ref, dba_ref, dwx_ref, dbx_ref, dlam_ref):
                ref[...] = jnp.zeros_like(ref)

        xp, cwv, lam = x_ref[...], cw_ref[...], lam_ref[...]
        p8 = jnp.where(tile0, 0.0, p_ref[...])
        xr, shifts = _conv4(xp, p8, cwv, cb_ref[...])
        r, i, sp, a, mult = _lru_gates(xr, wa_ref, wx_ref, ba_ref[...], bx_ref[...], lam)
        gate, hsv, dyv = gate_ref[...], hs_ref[...], dy_ref[...]
        dgate_ref[...] = (dyv * hsv * _gelu_grad(gate)).astype(BF16)
        acc_b = dyv * _gelu(gate) + jnp.where(_rows(a.shape) == tt - 1, carry[0:1], 0.0)
        acc_a = _up_fill(a, 1, 0.0)
        d = 1
        while d < tt:
            acc_b = acc_b + acc_a * _up_fill(acc_b, d, 0.0)
            acc_a = acc_a * _up_fill(acc_a, d, 0.0)
            d *= 2
        gsum = acc_b
        carry[...] = jnp.broadcast_to(a[0:1] * gsum[0:1], carry.shape)
        hprev = _shift_down(hsv, jnp.where(tile0, 0.0, phs_ref[...]), 1)
        da = gsum * hprev
        dmult = gsum * i * xr
        di = gsum * mult * xr
        dxr = gsum * mult * i
        dla = da * a - dmult * (a * a) / mult
        dr = dla * (-LRU_C * sp)
        dlam_ref[...] += jnp.sum(dla * (-LRU_C * r), axis=0, keepdims=True) * (-_sigmoid(-lam))
        dpa = dr * r * (1.0 - r)
        dpx = di * i * (1.0 - i)
        dba_ref[...] += jnp.sum(dpa, axis=0, keepdims=True)
        dbx_ref[...] += jnp.sum(dpx, axis=0, keepdims=True)
        dxr = dxr + _blockdiag(dpa, wa_ref, NT) + _blockdiag(dpx, wx_ref, NT)
        for b in range(ct // LANES):
            sl = slice(LANES * b, LANES * (b + 1))
            dwa_ref[b] += _bdot(xr[:, sl], dpa[:, sl], TN)
            dwx_ref[b] += _bdot(xr[:, sl], dpx[:, sl], TN)
        dx, dws = _conv4_bwd(dxr, head[...], xp, shifts, cwv)
        dx_ref[...] = dx.astype(BF16)
        for k in range(4):
            dcw_ref[k:k + 1, :] += dws[k]
        dcb_ref[...] += jnp.sum(dxr, axis=0, keepdims=True)
        head[...] = dxr[:SUBLANES]

    return pl.pallas_call(
        body, grid=(c // ct, nt),
        in_specs=[cur, cur, prev, cur, prev, cur, vec(4), vec(1), blk, vec(1), blk, vec(1), vec(1)],
        out_specs=[cur, cur, vec(4), vec(1), blk, vec(1), blk, vec(1), vec(1)],
        out_shape=[jax.ShapeDtypeStruct((t, c), BF16)] * 2 + [jax.ShapeDtypeStruct((4, c), F32), jax.ShapeDtypeStruct((1, c), F32),
                   jax.ShapeDtypeStruct(wa.shape, F32), jax.ShapeDtypeStruct((1, c), F32),
                   jax.ShapeDtypeStruct(wx.shape, F32), jax.ShapeDtypeStruct((1, c), F32), jax.ShapeDtypeStruct((1, c), F32)],
        scratch_shapes=[pltpu.VMEM((SUBLANES, ct), F32)] * 2, name=name,
        compiler_params=_cp(("parallel", "arbitrary"), 48),
    )(gate, xpre, xpre, hs, hs, dy, cw, cb, wa, ba, wx, bx, lam)


RET_W = HEADS * HEAD_DIM
HALF = HEAD_DIM // 2


def _ret_tables(t):
    c = RET_CHUNK
    inv_freq = ROPE_BASE ** (-jnp.arange(HALF, dtype=F32) / HALF)
    ang = jnp.arange(t, dtype=jnp.int32).astype(F32)[:, None] * inv_freq[None, :]
    cos, sin = jnp.cos(ang), jnp.sin(ang)
    cosf = jnp.concatenate([cos, cos], axis=1)
    sinf = jnp.concatenate([-sin, sin], axis=1)
    log_gamma = jnp.log1p(-jnp.exp2(-5.0 - jnp.arange(HEADS, dtype=F32)))
    idx = jnp.arange(c, dtype=F32)
    rel = idx[:, None] - idx[None, :]
    causal = rel >= 0
    dmask = jnp.where(causal, jnp.exp(log_gamma[:, None, None] * jnp.where(causal, rel, 0.0)), 0.0)
    ktail = jnp.exp(log_gamma[:, None] * (c - 1 - idx))
    qdec = jnp.exp(log_gamma[:, None] * (idx + 1.0))
    rowtab = jnp.broadcast_to(jnp.stack([ktail, qdec], axis=1)[..., None], (HEADS, 2, c, HEAD_DIM))
    cdec = jnp.broadcast_to(jnp.exp(log_gamma * c)[:, None, None], (HEADS, SUBLANES, HEAD_DIM))
    return cosf, sinf, dmask, rowtab, cdec


def _rotary(x, cosf, sinf):
    return x * cosf + pltpu.roll(x, HALF, 1) * sinf


def _rotary_t(dx, cosf, sinf):
    return dx * cosf + pltpu.roll(dx * sinf, HALF, 1)


def _ret_specs(c, order):
    full = lambda shape: pl.BlockSpec(shape, lambda n: (0,) * len(shape))
    return dict(
        proj=pl.BlockSpec((c, 4 * RET_W), lambda n: (order(n), 0)),
        rot=pl.BlockSpec((c, HEAD_DIM), lambda n: (order(n), 0)),
        dmask=full((HEADS, c, c)), rowtab=full((HEADS, 2, c, HEAD_DIM)), cdec=full((HEADS, SUBLANES, HEAD_DIM)),
        state=pl.BlockSpec((1, HEADS, HEAD_DIM, HEAD_DIM), lambda n: (order(n), 0, 0, 0)),
        half=pl.BlockSpec((c, RET_W), lambda n: (order(n), 0)),
    )


def _ret_head(p_ref, h, cosf, sinf):
    sl = lambda j: slice(j * RET_W + h * HEAD_DIM, j * RET_W + (h + 1) * HEAD_DIM)
    q, k, v, g = p_ref[:, sl(0)], p_ref[:, sl(1)], p_ref[:, sl(2)], p_ref[:, sl(3)]
    return _rotary(q, cosf, sinf), _rotary(k, cosf, sinf) * QSCALE, v, g


def _ret_fwd(proj, tables, comm, *, name):
    t = proj.shape[0]
    c = RET_CHUNK
    nc = t // c
    sp = _ret_specs(c, lambda n: n)
    hbm = pl.BlockSpec(memory_space=pl.ANY)

    def body(p_ref, cos_ref, sin_ref, dm_ref, rt_ref, cd_ref, *rest):
        comm_in, (y_ref, s_ref), comm_out, (state,), comm_sems = comm.split(rest, n_out=2, n_scratch=1)

        @pl.when(pl.program_id(0) == 0)
        def _():
            state[...] = jnp.zeros_like(state)
            comm.start(comm_in, comm_out, comm_sems)

        cosf, sinf = cos_ref[...], sin_ref[...]
        for h in range(HEADS):
            qr, kr, v, g = _ret_head(p_ref, h, cosf, sinf)
            s0 = state[h]
            s_ref[0, h] = s0
            scores = _bdot(qr, kr, NT) * dm_ref[h]
            o = _bdot(scores, v) + _bdot(qr * rt_ref[h, 1], s0)
            state[h] = s0 * cd_ref[h][0:1] + _bdot(kr * rt_ref[h, 0], v, TN)
            rinv = lax.rsqrt(jnp.mean(o * o, axis=-1, keepdims=True) + EPS)
            y_ref[:, h * HEAD_DIM:(h + 1) * HEAD_DIM] = (o * rinv * (g * _sigmoid(g))).astype(BF16)

        @pl.when(pl.program_id(0) == nc - 1)
        def _():
            comm.finish(comm_in, comm_out, comm_sems)

    outs = pl.pallas_call(
        body, grid=(nc,),
        in_specs=[sp["proj"], sp["rot"], sp["rot"], sp["dmask"], sp["rowtab"], sp["cdec"]] + [hbm] * len(comm.arrays),
        out_specs=[sp["half"], sp["state"]] + [hbm] * len(comm.out_shapes),
        out_shape=[jax.ShapeDtypeStruct((t, 2 * RET_W), BF16), jax.ShapeDtypeStruct((nc, HEADS, HEAD_DIM, HEAD_DIM), F32)]
        + comm.out_shapes,
        scratch_shapes=[pltpu.VMEM((HEADS, HEAD_DIM, HEAD_DIM), F32)] + comm.scratch, name=name,
        compiler_params=_cp(("arbitrary",), 48),
    )(proj, *tables, *comm.arrays)
    return outs[0], outs[1], outs[2:]


def _ret_bwd(proj, tables, states, dy, comm, *, name):
    t = proj.shape[0]
    c = RET_CHUNK
    nc = t // c
    sp = _ret_specs(c, lambda n: nc - 1 - n)
    hbm = pl.BlockSpec(memory_space=pl.ANY)

    def body(p_ref, cos_ref, sin_ref, dm_ref, rt_ref, cd_ref, s_ref, dy_ref, *rest):
        comm_in, (dp_ref,), comm_out, (dstate,), comm_sems = comm.split(rest, n_out=1, n_scratch=1)

        @pl.when(pl.program_id(0) == 0)
        def _():
            dstate[...] = jnp.zeros_like(dstate)
            comm.start(comm_in, comm_out, comm_sems)

        cosf, sinf = cos_ref[...], sin_ref[...]
        for h in range(HEADS):
            qr, kr, v, g = _ret_head(p_ref, h, cosf, sinf)
            s0, dm, ktl, qdc = s_ref[0, h], dm_ref[h], rt_ref[h, 0], rt_ref[h, 1]
            scores = _bdot(qr, kr, NT) * dm
            qd, kt = qr * qdc, kr * ktl
            o = _bdot(scores, v) + _bdot(qd, s0)
            rinv = lax.rsqrt(jnp.mean(o * o, axis=-1, keepdims=True) + EPS)
            oh = o * rinv
            sg = _sigmoid(g)
            dyh = dy_ref[:, h * HEAD_DIM:(h + 1) * HEAD_DIM]
            dg = dyh * oh * sg * (1.0 + g * (1.0 - sg))
            dyo = dyh * (g * sg)
            do = rinv * (dyo - oh * jnp.mean(dyo * oh, axis=-1, keepdims=True))
            ds1 = dstate[h]
            dsc = _bdot(do, v, NT) * dm
            dv = _bdot(scores, do, TN) + _bdot(kt, ds1)
            dqr = _bdot(dsc, kr) + _bdot(do, s0, NT) * qdc
            dkr = (_bdot(dsc, qr, TN) + _bdot(v, ds1, NT) * ktl) * QSCALE
            dstate[h] = ds1 * cd_ref[h][0:1] + _bdot(qd, do, TN)
            pieces = (_rotary_t(dqr, cosf, sinf), _rotary_t(dkr, cosf, sinf), dv, dg)
            for j, piece in enumerate(pieces):
                dp_ref[:, j * RET_W + h * HEAD_DIM:j * RET_W + (h + 1) * HEAD_DIM] = piece.astype(BF16)

        @pl.when(pl.program_id(0) == nc - 1)
        def _():
            comm.finish(comm_in, comm_out, comm_sems)

    outs = pl.pallas_call(
        body, grid=(nc,),
        in_specs=[sp["proj"], sp["rot"], sp["rot"], sp["dmask"], sp["rowtab"], sp["cdec"], sp["state"], sp["half"]]
        + [hbm] * len(comm.arrays),
        out_specs=[sp["proj"]] + [hbm] * len(comm.out_shapes),
        out_shape=[jax.ShapeDtypeStruct((t, 8 * RET_W), BF16)] + comm.out_shapes,
        scratch_shapes=[pltpu.VMEM((HEADS, HEAD_DIM, HEAD_DIM), F32)] + comm.scratch, name=name,
        compiler_params=_cp(("arbitrary",), 48),
    )(proj, *tables, states, dy, *comm.arrays)
    return outs[0], outs[1:]


GDN_W = HEADS * HEAD_DIM
GDN_CONV = 3 * GDN_W
NEUMANN_STEPS = 5


def _gdn_gates(ps, al, dt):
    return _sigmoid(ps), -jnp.exp(al) * _softplus(ps + dt)


def _cumsum_rows(x):
    d = 1
    while d < x.shape[0]:
        x = x + _down_fill(x, d, 0.0)
        d *= 2
    return x


def _rev_cumsum_rows(x):
    d = 1
    while d < x.shape[0]:
        x = x + _up_fill(x, d, 0.0)
        d *= 2
    return x


class _Chunk:
    pass


def _gdn_chunk(qc, kc, v, beta, g, s0, inv=None):
    c = GDN_CHUNK
    z = _Chunk()
    z.rq = lax.rsqrt(jnp.sum(qc * qc, axis=-1, keepdims=True) + EPS)
    z.rk = lax.rsqrt(jnp.sum(kc * kc, axis=-1, keepdims=True) + EPS)
    z.qn, z.k = qc * z.rq, kc * z.rk
    z.q = z.qn * QSCALE
    z.v, z.beta = v, beta
    gc = _cumsum_rows(jnp.broadcast_to(g, (c, LANES)))
    ri, ci = _rows((c, c)), _cols((c, c))
    z.tril, z.strict = ri >= ci, ri > ci
    diff = gc[:, :c] - gc.T[:c, :]
    z.decay = jnp.where(z.tril, jnp.exp(jnp.where(z.tril, diff, 0.0)), 0.0)
    z.eg = jnp.exp(gc)
    glast = gc[c - 1:c, :]
    z.egl = jnp.exp(glast - gc)
    z.cd = jnp.exp(glast)
    z.kb = z.k * beta
    both = _bdot(jnp.concatenate([z.kb, z.q], axis=0), z.k, NT)
    z.m, z.qk = both[:c], both[c:]
    if inv is None:
        neg = -jnp.where(z.strict, z.m * z.decay, 0.0)
        inv = (ri == ci).astype(F32) + neg
        pw = neg
        for _ in range(NEUMANN_STEPS):
            pw = _dot3(pw, pw)
            inv = inv + _dot3(inv, pw)
    z.inv = inv
    z.vb, z.kbg = v * beta, z.kb * z.eg
    solved = _dot3(inv, jnp.concatenate([z.vb, z.kbg], axis=1))
    z.u, z.w = solved[:, :HEAD_DIM], solved[:, HEAD_DIM:]
    z.attn = jnp.where(z.tril, z.qk * z.decay, 0.0)
    z.qd, z.kt = z.q * z.eg, z.k * z.egl
    through = _bdot(jnp.concatenate([z.w, z.qd], axis=0), s0)
    z.vnew = z.u - through[:c]
    z.o = through[c:] + _bdot(z.attn, z.vnew)
    z.s1 = s0 * z.cd + _bdot(z.kt, z.vnew, TN)
    return z


def _gdn_chunk_bwd(z, s0, do, ds1):
    c = GDN_CHUNK
    dvnew = _bdot(z.attn, do, TN) + _bdot(z.kt, ds1)
    against = _bdot(do, jnp.concatenate([s0, z.vnew], axis=0), NT)
    dqd = against[:, :HEAD_DIM]
    dattn = jnp.where(z.tril, against[:, HEAD_DIM:], 0.0)
    ds0 = ds1 * z.cd + _bdot(jnp.concatenate([z.qd, -z.w], axis=0), jnp.concatenate([do, dvnew], axis=0), TN)
    dcd = jnp.sum(jnp.sum(s0 * ds1, axis=1, keepdims=True), axis=0, keepdims=True)
    dkt = _bdot(z.vnew, ds1, NT)
    dw = -_bdot(dvnew, s0, NT)
    dsolved = _dot3(z.inv, jnp.concatenate([dvnew, dw], axis=1), TN)
    dvb, dkbg = dsolved[:, :HEAD_DIM], dsolved[:, HEAD_DIM:]
    dl = jnp.where(z.strict, -_bdot(dsolved, jnp.concatenate([z.u, z.w], axis=1), NT), 0.0)
    dml = dl * z.decay
    dqk = dattn * z.decay
    ddecay = (dl * z.m + dattn * z.qk) * z.decay
    stacked = jnp.concatenate([dqk, dml], axis=0)
    onto_k = _bdot(stacked, z.k)
    dq = onto_k[:c] + dqd * z.eg
    dkb = onto_k[c:] + dkbg * z.eg
    dk = _bdot(stacked, jnp.concatenate([z.q, z.kb], axis=0), TN) + dkt * z.egl + dkb * z.beta
    dbeta = jnp.sum(dkb * z.k, axis=-1, keepdims=True) + jnp.sum(dvb * z.v, axis=-1, keepdims=True)
    dv = dvb * z.beta
    colsum = _dot3(ddecay, jnp.ones((c, LANES), F32), TN)
    e = jnp.sum(dkt * z.kt, axis=-1, keepdims=True)
    dgc = (jnp.sum(ddecay, axis=-1, keepdims=True) - colsum
           + jnp.sum(dkbg * z.kbg, axis=-1, keepdims=True) + jnp.sum(dqd * z.qd, axis=-1, keepdims=True) - e)
    dglast = jnp.sum(e, axis=0, keepdims=True) + dcd * z.cd
    dgc = dgc + jnp.where(_rows((c, LANES)) == c - 1, dglast, 0.0)
    dg = _rev_cumsum_rows(dgc)[:, 0:1]
    dqn = dq * QSCALE
    dqc = z.rq * (dqn - z.qn * jnp.sum(dqn * z.qn, axis=-1, keepdims=True))
    dkc = z.rk * (dk - z.k * jnp.sum(dk * z.k, axis=-1, keepdims=True))
    return dqc, dkc, dv, dbeta, dg, ds0


GDN_SUB = 1


def _gdn_specs(c, order):
    full = lambda shape: pl.BlockSpec(shape, lambda n: (0,) * len(shape))
    return dict(
        proj=pl.BlockSpec((c, 4 * GDN_W), lambda n: (order(n), 1)),
        prev=pl.BlockSpec((SUBLANES, 4 * GDN_W), lambda n: (_prev8(order(n), c), 1)),
        small=pl.BlockSpec((c, LANES), lambda n: (order(n), 0)),
        convw=full((4, GDN_CONV)), vec=full((1, LANES)),
        state=pl.BlockSpec((GDN_SUB, HEADS, HEAD_DIM, HEAD_DIM), lambda n: (order(n), 0, 0, 0)),
        inv=pl.BlockSpec((GDN_SUB, HEADS, GDN_CHUNK, GDN_CHUNK), lambda n: (order(n), 0, 0, 0)),
        half=pl.BlockSpec((c, GDN_W), lambda n: (order(n), 1)),
        any=pl.BlockSpec(memory_space=pl.ANY),
    )


def _gdn_fwd(proj, psmall, conv_w, al, dt, gain, y_in, comm, *, name):
    t = proj.shape[0]
    c = GDN_CHUNK
    nc, ns = t // c, t // (c * GDN_SUB)
    sp = _gdn_specs(c * GDN_SUB, lambda n: n)

    def body(p_ref, prev_ref, ps_ref, cw_ref, al_ref, dt_ref, gain_ref, yin_ref, *rest):
        comm_in, (y_ref, s_ref, inv_ref), comm_out, (state,), comm_sems = comm.split(rest, n_out=3, n_scratch=1)
        n = pl.program_id(0)

        @pl.when(n == 0)
        def _():
            state[...] = jnp.zeros_like(state)
            comm.start(comm_in, comm_out, comm_sems)

        p8 = jnp.where(n == 0, 0.0, prev_ref[:, :GDN_CONV])
        pre, _ = _conv4(p_ref[:, :GDN_CONV], p8, cw_ref[...])
        act = pre * _sigmoid(pre)
        beta_all, g_all = _gdn_gates(ps_ref[...], al_ref[...], dt_ref[...])
        gd_all, gain = p_ref[:, GDN_CONV:], gain_ref[...]
        swish = gd_all * _sigmoid(gd_all)
        cur = [state[h] for h in range(HEADS)]
        starts, ys = [], []
        for sub in range(GDN_SUB):
            rows = slice(sub * c, (sub + 1) * c)
            starts.append(list(cur))
            pieces = []
            for h in range(HEADS):
                sl = lambda j: slice(j * GDN_W + h * HEAD_DIM, j * GDN_W + (h + 1) * HEAD_DIM)
                z = _gdn_chunk(act[rows, sl(0)], act[rows, sl(1)], act[rows, sl(2)], beta_all[rows, h:h + 1],
                               g_all[rows, HEADS + h:HEADS + h + 1], cur[h])
                cur[h] = z.s1
                inv_ref[sub, h] = z.inv
                rinv = lax.rsqrt(jnp.mean(z.o * z.o, axis=-1, keepdims=True) + EPS)
                pieces.append(z.o * rinv * gain * swish[rows, sl(0)])
            ys.append(jnp.concatenate(pieces, axis=1))
        y_ref[...] = jnp.concatenate(ys, axis=0).astype(BF16)
        for sub in range(GDN_SUB):
            for h in range(HEADS):
                s_ref[sub, h] = starts[sub][h]
        for h in range(HEADS):
            state[h] = cur[h]

        @pl.when(n == ns - 1)
        def _():
            comm.finish(comm_in, comm_out, comm_sems)

    outs = pl.pallas_call(
        body, grid=(ns,),
        in_specs=[sp["proj"], sp["prev"], sp["small"], sp["convw"], sp["vec"], sp["vec"], sp["vec"], sp["any"]]
        + [sp["any"]] * len(comm.arrays),
        out_specs=[sp["half"], sp["state"], sp["inv"]] + [sp["any"]] * len(comm.out_shapes),
        out_shape=[jax.ShapeDtypeStruct((t, 2 * GDN_W), BF16), jax.ShapeDtypeStruct((nc, HEADS, HEAD_DIM, HEAD_DIM), F32),
                   jax.ShapeDtypeStruct((nc, HEADS, c, c), F32)] + comm.out_shapes,
        scratch_shapes=[pltpu.VMEM((HEADS, HEAD_DIM, HEAD_DIM), F32)] + comm.scratch, name=name,
        input_output_aliases={7: 0}, compiler_params=_cp(("arbitrary",), 48),
    )(proj, proj, psmall, conv_w, al, dt, gain, y_in, *comm.arrays)
    return outs[0], (outs[1], outs[2]), outs[3:]


def _gdn_bwd(proj, psmall, conv_w, al, dt, gain, states, dy, dproj_in, comm, *, name):
    t = proj.shape[0]
    c = GDN_CHUNK
    nc, ns = t // c, t // (c * GDN_SUB)
    sp = _gdn_specs(c * GDN_SUB, lambda n: ns - 1 - n)

    def body(p_ref, prev_ref, ps_ref, cw_ref, al_ref, dt_ref, gain_ref, s_ref, inv_ref, dy_ref, dpin_ref, *rest):
        comm_in, outs, comm_out, (dstate, head), comm_sems = comm.split(rest, n_out=6, n_scratch=2)
        dp_ref, dps_ref, dcw_ref, dal_ref, ddt_ref, dgain_ref = outs
        n = pl.program_id(0)
        first_rows = n == ns - 1

        @pl.when(n == 0)
        def _():
            for ref in (dstate, head, dcw_ref, dal_ref, ddt_ref, dgain_ref):
                ref[...] = jnp.zeros_like(ref)
            comm.start(comm_in, comm_out, comm_sems)

        x, cwv = p_ref[:, :GDN_CONV], cw_ref[...]
        p8 = jnp.where(first_rows, 0.0, prev_ref[:, :GDN_CONV])
        pre, shifts = _conv4(x, p8, cwv)
        sg_pre = _sigmoid(pre)
        act = pre * sg_pre
        ps, alv, dtv, gain = ps_ref[...], al_ref[...], dt_ref[...], gain_ref[...]
        beta_all, g_all = _gdn_gates(ps, alv, dtv)
        lane = _cols((c, LANES))
        dgain = jnp.zeros((1, LANES), F32)
        dcur = [dstate[h] for h in range(HEADS)]
        dact_rows, dbeta_rows, dg_rows = [None] * GDN_SUB, [None] * GDN_SUB, [None] * GDN_SUB
        for sub in reversed(range(GDN_SUB)):
            rows = slice(sub * c, (sub + 1) * c)
            dbeta_all = jnp.zeros((c, LANES), F32)
            dg_all = jnp.zeros((c, LANES), F32)
            dact = [None] * (3 * HEADS)
            for h in range(HEADS):
                sl = lambda j: slice(j * GDN_W + h * HEAD_DIM, j * GDN_W + (h + 1) * HEAD_DIM)
                s0 = s_ref[sub, h]
                z = _gdn_chunk(act[rows, sl(0)], act[rows, sl(1)], act[rows, sl(2)], beta_all[rows, h:h + 1],
                               g_all[rows, HEADS + h:HEADS + h + 1], s0, inv=inv_ref[sub, h])
                rinv = lax.rsqrt(jnp.mean(z.o * z.o, axis=-1, keepdims=True) + EPS)
                oh = z.o * rinv
                gd = p_ref[rows, sl(3)]
                sgd = _sigmoid(gd)
                dyh = dy_ref[rows, sl(0)]
                dgain = dgain + jnp.sum(dyh * oh * (gd * sgd), axis=0, keepdims=True)
                dp_ref[rows, sl(3)] = (dyh * oh * gain * sgd * (1.0 + gd * (1.0 - sgd))).astype(BF16)
                dyo = dyh * gain * (gd * sgd)
                do = rinv * (dyo - oh * jnp.mean(dyo * oh, axis=-1, keepdims=True))
                dqc, dkc, dv, dbeta, dg, dcur[h] = _gdn_chunk_bwd(z, s0, do, dcur[h])
                dact[h], dact[HEADS + h], dact[2 * HEADS + h] = dqc, dkc, dv
                dbeta_all = dbeta_all + jnp.where(lane == h, dbeta, 0.0)
                dg_all = dg_all + jnp.where(lane == HEADS + h, dg, 0.0)
            dact_rows[sub], dbeta_rows[sub], dg_rows[sub] = jnp.concatenate(dact, axis=1), dbeta_all, dg_all
        for h in range(HEADS):
            dstate[h] = dcur[h]
        dbeta_all, dg_all = jnp.concatenate(dbeta_rows, axis=0), jnp.concatenate(dg_rows, axis=0)
        dpre = jnp.concatenate(dact_rows, axis=0) * sg_pre * (1.0 + pre * (1.0 - sg_pre))
        dx, dws = _conv4_bwd(dpre, head[...], x, shifts, cwv)
        dp_ref[:, :GDN_CONV] = dx.astype(BF16)
        for k in range(4):
            dcw_ref[k:k + 1, :] += dws[k]
        head[...] = dpre[:SUBLANES]
        dsp = dg_all * (-jnp.exp(alv)) * _sigmoid(ps + dtv)
        dps_ref[...] = (dbeta_all * beta_all * (1.0 - beta_all) + dsp).astype(BF16)
        ddt_ref[...] += jnp.sum(dsp, axis=0, keepdims=True)
        dal_ref[...] += jnp.sum(dg_all * g_all, axis=0, keepdims=True)
        dgain_ref[...] += dgain

        @pl.when(n == ns - 1)
        def _():
            comm.finish(comm_in, comm_out, comm_sems)

    vec_f32 = jax.ShapeDtypeStruct((1, LANES), F32)
    outs = pl.pallas_call(
        body, grid=(ns,),
        in_specs=[sp["proj"], sp["prev"], sp["small"], sp["convw"], sp["vec"], sp["vec"], sp["vec"], sp["state"], sp["inv"],
                  sp["half"], sp["any"]]
        + [sp["any"]] * len(comm.arrays),
        out_specs=[sp["proj"], sp["small"], sp["convw"], sp["vec"], sp["vec"], sp["vec"]] + [sp["any"]] * len(comm.out_shapes),
        out_shape=[jax.ShapeDtypeStruct((t, 8 * GDN_W), BF16), jax.ShapeDtypeStruct((t, LANES), BF16),
                   jax.ShapeDtypeStruct((4, GDN_CONV), F32), vec_f32, vec_f32, vec_f32] + comm.out_shapes,
        scratch_shapes=[pltpu.VMEM((HEADS, HEAD_DIM, HEAD_DIM), F32), pltpu.VMEM((SUBLANES, GDN_CONV), F32)] + comm.scratch,
        name=name, input_output_aliases={10: 0}, compiler_params=_cp(("arbitrary",), 48),
    )(proj, proj, psmall, conv_w, al, dt, gain, *states, dy, dproj_in, *comm.arrays)
    return outs[:6], outs[6:]


def _here():
    x, y, c = lax.axis_index("x"), lax.axis_index("y"), lax.axis_index("c")
    return x, y, c, [(1 - x, y), (x, 1 - y), (1 - x, 1 - y)]


def _rdma(src, dst, send, recv, k, dev):
    return pltpu.make_async_remote_copy(src_ref=src, dst_ref=dst, send_sem=send.at[k], recv_sem=recv.at[k],
                                        device_id=dev, device_id_type=MESH)


def _dma_sems(n):
    return [pltpu.SemaphoreType.DMA((n,)), pltpu.SemaphoreType.DMA((n,)), pltpu.SemaphoreType.DMA((1,))]


COPY_PIECES = 4
COPY_PIECE_ALIGN = 16


def _row_parts(rows):
    n = COPY_PIECES if rows % (COPY_PIECES * COPY_PIECE_ALIGN) == 0 and rows >= 1024 else 1
    return [pl.ds(q * (rows // n), rows // n) for q in range(n)]


class _AllGather:
    def __init__(self, array):
        self.arrays = [array]
        self.out_shapes = [jax.ShapeDtypeStruct((N_DEV,) + array.shape, array.dtype)]
        self.parts = _row_parts(array.shape[0])
        self.scratch = _dma_sems(7 * len(self.parts))

    def start(self, ins, outs, sems):
        (src,), (out,), (send, recv, loc) = ins, outs, sems
        x, y, c, chips = _here()
        me, n = 4 * x + 2 * y + c, len(self.parts)
        pltpu.make_async_copy(src, out.at[me], loc.at[0]).start()
        for q, part in enumerate(self.parts):
            _rdma(src.at[part], out.at[me, part], send, recv, q, (x, y, 1 - c)).start()
            for j, (cx, cy) in enumerate(chips):
                _rdma(src.at[part], out.at[me, part], send, recv, (1 + j) * n + q, (cx, cy, c)).start()

    def finish(self, ins, outs, sems):
        (src,), (out,), (send, recv, loc) = ins, outs, sems
        x, y, c, chips = _here()
        sibling, me, n = (x, y, 1 - c), 4 * x + 2 * y + c, len(self.parts)
        piece = lambda k, q: _rdma(src.at[self.parts[q]], out.at[me, self.parts[q]], send, recv, k * n + q, sibling)
        for j, (cx, cy) in enumerate(chips):
            for q, part in enumerate(self.parts):
                got = out.at[4 * cx + 2 * cy + c, part]
                piece(1 + j, q).wait_recv()
                _rdma(got, got, send, recv, (4 + j) * n + q, sibling).start()
        for k in (0, 4, 5, 6):
            for q in range(n):
                piece(k, q).wait_recv()
        for k in range(7):
            for q in range(n):
                piece(k, q).wait_send()
        pltpu.make_async_copy(src, out.at[me], loc.at[0]).wait()


class _ChipExchange:
    def __init__(self, array):
        self.arrays = [array]
        self.out_shapes = [jax.ShapeDtypeStruct(array.shape, array.dtype)]
        self.parts = _row_parts(array.shape[1])
        self.scratch = _dma_sems(3 * len(self.parts))

    def _copies(self, ins, outs, sems):
        (src,), (out,), (send, recv, loc) = ins, outs, sems
        x, y, c, chips = _here()
        here, n = 2 * x + y, len(self.parts)
        local = pltpu.make_async_copy(src.at[here], out.at[here], loc.at[0])
        return local, [_rdma(src.at[2 * cx + cy, part], out.at[here, part], send, recv, j * n + q, (cx, cy, c))
                       for j, (cx, cy) in enumerate(chips) for q, part in enumerate(self.parts)]

    def start(self, ins, outs, sems):
        local, remote = self._copies(ins, outs, sems)
        local.start()
        for cp in remote:
            cp.start()

    def finish(self, ins, outs, sems):
        local, remote = self._copies(ins, outs, sems)
        for cp in remote:
            cp.wait()
        local.wait()


class _PairSwap:
    def __init__(self, array):
        self.arrays = [array]
        self.out_shapes = [jax.ShapeDtypeStruct(array.shape[1:], array.dtype)]
        self.parts = _row_parts(array.shape[2])
        self.scratch = _dma_sems(4 * len(self.parts))

    def _copies(self, ins, outs, sems):
        (src,), (theirs,), (send, recv, _) = ins, outs, sems
        x, y, c, _ = _here()
        return [_rdma(src.at[1 - c, p, part], theirs.at[p, part], send, recv, p * len(self.parts) + q, (x, y, 1 - c))
                for p in range(4) for q, part in enumerate(self.parts)]

    def start(self, ins, outs, sems):
        for cp in self._copies(ins, outs, sems):
            cp.start()

    def finish(self, ins, outs, sems):
        for cp in self._copies(ins, outs, sems):
            cp.wait()


class _Comm:
    def __init__(self, ops):
        self.ops = ops
        self.arrays = [a for op in ops for a in op.arrays]
        self.out_shapes = [s for op in ops for s in op.out_shapes]
        self.scratch = [s for op in ops for s in op.scratch]

    def split(self, rest, n_out, n_scratch):
        cuts = np.cumsum([0, len(self.arrays), n_out, len(self.out_shapes), n_scratch, len(self.scratch)])
        assert cuts[-1] == len(rest)
        return tuple(rest[a:b] for a, b in zip(cuts[:-1], cuts[1:]))

    def _each(self, method, ins, outs, sems):
        i = o = s = 0
        for op in self.ops:
            ni, no, ns = len(op.arrays), len(op.out_shapes), len(op.scratch)
            getattr(op, method)(ins[i:i + ni], outs[o:o + no], sems[s:s + ns])
            i, o, s = i + ni, o + no, s + ns

    def start(self, ins, outs, sems):
        self._each("start", ins, outs, sems)

    def finish(self, ins, outs, sems):
        self._each("finish", ins, outs, sems)

    def run(self, name):
        def body(*refs):
            ins, _, outs, _, sems = self.split(refs, 0, 0)
            self.start(ins, outs, sems)
            self.finish(ins, outs, sems)

        hbm = pl.BlockSpec(memory_space=pl.ANY)
        return pl.pallas_call(body, in_specs=[hbm] * len(self.arrays), out_specs=[hbm] * len(self.out_shapes),
                              out_shape=self.out_shapes, scratch_shapes=self.scratch, name=name)(*self.arrays)


def _sum_slots(x, *, name, tr=None):
    n, r, l = x.shape
    tr = r if tr is None else tr

    def body(x_ref, o_ref):
        acc = x_ref[0].astype(F32)
        for s in range(1, n):
            acc = acc + x_ref[s].astype(F32)
        o_ref[...] = acc

    return pl.pallas_call(
        body, grid=(r // tr,), in_specs=[pl.BlockSpec((n, tr, l), lambda i: (0, i, 0))],
        out_specs=pl.BlockSpec((tr, l), lambda i: (i, 0)), out_shape=jax.ShapeDtypeStruct((r, l), F32),
        name=name, compiler_params=_cp(("parallel",), 48),
    )(x)


def _pair_add(both, theirs, *, name, tr):
    _, n, r, l = both.shape

    def body(a_ref, b_ref, o_ref):
        mine = jnp.where(lax.axis_index("c") == 0, a_ref[0], a_ref[1])
        o_ref[...] = (mine.astype(F32) + b_ref[...].astype(F32)).astype(BF16)

    spec = pl.BlockSpec((n, tr, l), lambda i: (0, i, 0))
    return pl.pallas_call(body, grid=(r // tr,), in_specs=[pl.BlockSpec((2, n, tr, l), lambda i: (0, 0, i, 0)), spec], out_specs=spec,
                          out_shape=jax.ShapeDtypeStruct(theirs.shape, BF16), name=name,
                          compiler_params=_cp(("parallel",), 48))(both, theirs)


ADAM_TILE_ELEMS = 512 * 1024


def _adam(w, g, m, v, *, name):
    shape = w.shape
    cols = shape[-1]
    rows = math.prod(shape[:-1]) if len(shape) > 1 else 1
    tr = rows
    if rows * cols > ADAM_TILE_ELEMS:
        tr = max(d for d in range(SUBLANES, ADAM_TILE_ELEMS // cols + 1, SUBLANES) if rows % d == 0)
    c1, c2 = 1.0 - ADAM_B1 ** ADAM_STEP, 1.0 - ADAM_B2 ** ADAM_STEP

    def body(w_ref, g_ref, m_ref, v_ref, d_ref, m2_ref, v2_ref):
        gv = g_ref[...]
        m2 = ADAM_B1 * m_ref[...] + (1.0 - ADAM_B1) * gv
        v2 = ADAM_B2 * v_ref[...] + (1.0 - ADAM_B2) * (gv * gv)
        d_ref[...] = -ADAM_LR * ((m2 / c1) / (jnp.sqrt(v2 / c2) + ADAM_EPS) + ADAM_WD * w_ref[...])
        m2_ref[...] = m2
        v2_ref[...] = v2

    spec = pl.BlockSpec((tr, cols), lambda i: (i, 0))
    outs = pl.pallas_call(
        body, grid=(rows // tr,), in_specs=[spec] * 4, out_specs=[spec] * 3,
        out_shape=[jax.ShapeDtypeStruct((rows, cols), F32)] * 3, name=name, compiler_params=_cp(("parallel",), 48),
    )(*(a.reshape(rows, cols) for a in (w, g, m, v)))
    return tuple(o.reshape(shape) for o in outs)


WEIGHTS = ['norm_mix', 'norm_ffn', 'ret_gdn_w_in', 'gdn_conv_w', 'gdn_a_log', 'gdn_dt_bias', 'gdn_out_gain', 'ret_gdn_w_out',
           'lru_w_in', 'lru_conv_w', 'lru_conv_b', 'lru_w_a', 'lru_b_a', 'lru_w_x', 'lru_b_x', 'lru_lambda', 'lru_w_out',
           'ffn_w_up', 'ffn_conv_w', 'ffn_conv_b', 'ffn_w_down', 'norm_final']
SMALL = {'gdn_conv_w': ((1, 4, 192), 2), 'lru_conv_w': ((1, 4, 128), 2), 'lru_conv_b': ((1, 128), 1), 'lru_b_a': ((1, 128), 1),
         'lru_b_x': ((1, 128), 1), 'lru_lambda': ((1, 128), 1), 'ffn_conv_w': ((2, 3, 704), 2)}
EARLY = {'lru_conv_w': (1, 4, 1024), 'lru_conv_b': (1, 1024), 'lru_b_a': (1, 1024), 'lru_b_x': (1, 1024), 'lru_lambda': (1, 1024),
         'ffn_conv_w': (2, 3, 5632), 'norm_ffn': (2, 1024), 'norm_mix1': (1, 1024), 'lru_w_a': (1, 8, 128, 128),
         'lru_w_x': (1, 8, 128, 128), 'ffn_conv_b': (2, 5632), 'norm_final': (1024,)}
LATE = {'gdn_conv_w': (1, 4, 1536), 'norm_mix0': (1, 1024), 'gdn_a_log': (1, 4), 'gdn_dt_bias': (1, 4), 'gdn_out_gain': (1, 128),
        'loss': (1, 1)}


def _rows_of(n_elems):
    return -(-n_elems // LANES)


def _to_rows(a, lead=()):
    flat = a.reshape(lead + (-1,))
    pad = _rows_of(flat.shape[-1]) * LANES - flat.shape[-1]
    if pad:
        flat = jnp.pad(flat, [(0, 0)] * len(lead) + [(0, pad)])
    return flat.reshape(lead + (-1, LANES))


def _pack(pieces, total_rows, lead=()):
    buf = jnp.concatenate(pieces, axis=len(lead))
    pad = total_rows - buf.shape[len(lead)]
    return jnp.pad(buf, [(0, 0)] * len(lead) + [(0, pad), (0, 0)]) if pad else buf


def _unpack(buf, shapes, lead=()):
    out, off = [], 0
    for shape in shapes:
        n = math.prod(shape)
        rows = _rows_of(n)
        piece = lax.slice_in_dim(buf, off, off + rows, axis=len(lead)).reshape(lead + (rows * LANES,))
        out.append(lax.slice_in_dim(piece, 0, n, axis=len(lead)).reshape(lead + shape))
        off += rows
    return out


def _join_blocks(g, axis):
    m = jnp.moveaxis(g, 0, axis)
    return m.reshape(m.shape[:axis] + (N_DEV * m.shape[axis + 1],) + m.shape[axis + 2:])


def _split_blocks(full, axis):
    s = full.shape
    return jnp.moveaxis(full.reshape(s[:axis] + (N_DEV, s[axis] // N_DEV) + s[axis + 1:]), axis, 0)


def _small_rows(shapes):
    total = sum(_rows_of(math.prod(s)) for s in shapes)
    return -(-total // SUBLANES) * SUBLANES


FFN_TM = 1024
FFN_TN = 512


def _ffn_forward(h, gain, w_up, cw, cb, w_down, tag, comm):
    t, d = h.shape
    tm, tn, blk, nb = min(FFN_TM, t), FFN_TN, FFN_BLK, FFN_NB
    hn = _norm_fwd(h, gain, name=f"ffn{tag}_norm")
    up = _mmx(hn, w_up, dims=NN, grid=(t // tm, 2 * nb, 1), name=f"ffn{tag}_up", tile=(tm, blk),
              a_spec=pl.BlockSpec((tm, d), lambda i, j, k: (i, 0)),
              b_spec=pl.BlockSpec((None, d, blk), lambda i, j, k: (j, 0, 0)),
              o_spec=pl.BlockSpec((None, None, tm, blk), lambda i, j, k: (j // nb, j % nb, i, 0)),
              out_shape=jax.ShapeDtypeStruct((2, nb, t, blk), F32))
    act, comm_out = _ffn_act_fwd(up, cw, cb, comm, name=f"ffn{tag}_act")
    out = _mmx(act, w_down, dims=NN, grid=(t // tm, d // tn, nb), name=f"ffn{tag}_down", tile=(tm, tn), res=h,
               a_spec=pl.BlockSpec((None, tm, blk), lambda i, j, k: (k, i, 0)),
               b_spec=pl.BlockSpec((blk, tn), lambda i, j, k: (k, j)),
               o_spec=pl.BlockSpec((tm, tn), lambda i, j, k: (i, j)),
               out_shape=jax.ShapeDtypeStruct((t, d), F32))
    return out, (hn, up, act), comm_out


def _ffn_backward(dh, h, gain, saved, w_up, cw, cb, w_down, tag):
    hn, up, act = saved
    t, d = h.shape
    tm, tn, blk, nb = min(FFN_TM, t), FFN_TN, FFN_BLK, FFN_NB
    tk = min(FFN_TM, t)
    da = _mmx(dh, w_down, dims=NT, grid=(t // tm, nb, 1), name=f"ffn{tag}_d_act", tile=(tm, blk),
              a_spec=pl.BlockSpec((tm, d), lambda i, j, k: (i, 0)),
              b_spec=pl.BlockSpec((blk, d), lambda i, j, k: (j, 0)),
              o_spec=pl.BlockSpec((None, tm, blk), lambda i, j, k: (j, i, 0)),
              out_shape=jax.ShapeDtypeStruct((nb, t, blk), F32))
    dwd = _mmx(act, dh, dims=TN, grid=(nb, d // tn, t // tk), name=f"ffn{tag}_d_wdown", tile=(blk, tn), split_rows=blk // 2,
               a_spec=pl.BlockSpec((None, tk, blk), lambda i, j, k: (i, k, 0)),
               b_spec=pl.BlockSpec((tk, tn), lambda i, j, k: (k, j)),
               o_spec=pl.BlockSpec((2, None, blk // 2, tn), lambda i, j, k: (0, i, 0, j)),
               out_shape=jax.ShapeDtypeStruct((2, nb, blk // 2, d), BF16))
    dup, dcw, dcb = _ffn_act_bwd(up, da, cw, cb, name=f"ffn{tag}_act_bwd")
    half = nb // 2
    dhn = _mmx(dup, w_up, dims=NT, grid=(t // tm, d // tn, nb), name=f"ffn{tag}_d_hn", tile=(tm, tn), pairs=True,
               a_spec=pl.BlockSpec((None, 2, tm, blk), lambda i, j, k: (k // half, k % half, i, 0)),
               b_spec=pl.BlockSpec((2, tn, blk), lambda i, j, k: (k, j, 0)),
               o_spec=pl.BlockSpec((tm, tn), lambda i, j, k: (i, j)),
               out_shape=jax.ShapeDtypeStruct((t, d), F32))
    dwu = _mmx(hn, dup, dims=TN, grid=(1, 2 * nb, t // tk), name=f"ffn{tag}_d_wup", tile=(d, blk),
               a_spec=pl.BlockSpec((tk, d), lambda i, j, k: (k, 0)),
               b_spec=pl.BlockSpec((None, None, tk, blk), lambda i, j, k: (j // nb, j % nb, k, 0)),
               o_spec=pl.BlockSpec((None, None, d, blk), lambda i, j, k: (j % 2, j // 2, 0, 0)),
               out_shape=jax.ShapeDtypeStruct((2, N_DEV // 2, d, blk), BF16))
    dh_in, dgain = _norm_bwd(h, gain, dhn, dh, name=f"ffn{tag}_norm_bwd")
    conv_w = dcw.transpose(2, 0, 1, 3).reshape(3, 2 * nb * blk)
    return dh_in, dict(w_up=dwu, w_down=dwd, conv_w=conv_w, conv_b=dcb.reshape(1, 2 * nb * blk), norm=dgain)


def kernel(x, norm_mix, norm_ffn, ret_gdn_w_in, gdn_conv_w, gdn_a_log, gdn_dt_bias, gdn_out_gain, ret_gdn_w_out, lru_w_in, lru_conv_w, lru_conv_b, lru_w_a, lru_b_a, lru_w_x, lru_b_x, lru_lambda, lru_w_out, ffn_w_up, ffn_conv_w, ffn_conv_b, ffn_w_down, norm_final, loss_target, m_norm_mix, m_norm_ffn, m_ret_gdn_w_in, m_gdn_conv_w, m_gdn_a_log, m_gdn_dt_bias, m_gdn_out_gain, m_ret_gdn_w_out, m_lru_w_in, m_lru_conv_w, m_lru_conv_b, m_lru_w_a, m_lru_b_a, m_lru_w_x, m_lru_b_x, m_lru_lambda, m_lru_w_out, m_ffn_w_up, m_ffn_conv_w, m_ffn_conv_b, m_ffn_w_down, m_norm_final, v_norm_mix, v_norm_ffn, v_ret_gdn_w_in, v_gdn_conv_w, v_gdn_a_log, v_gdn_dt_bias, v_gdn_out_gain, v_ret_gdn_w_out, v_lru_w_in, v_lru_conv_w, v_lru_conv_b, v_lru_w_a, v_lru_b_a, v_lru_w_x, v_lru_b_x, v_lru_lambda, v_lru_w_out, v_ffn_w_up, v_ffn_conv_w, v_ffn_conv_b, v_ffn_w_down, v_norm_final):
    given = dict(locals())
    w = {n: given[n] for n in WEIGHTS}
    me = 4 * lax.axis_index("x") + 2 * lax.axis_index("y") + lax.axis_index("c")
    t = x.shape[1]

    first_shards = {'w_in0': ret_gdn_w_in[0]}
    rest_shards = {'w_out0': ret_gdn_w_out[0], 'lru_in': lru_w_in[0], 'lru_out': lru_w_out[0], 'up0': ffn_w_up[0], 'up1': ffn_w_up[1],
                   'down0': ffn_w_down[0], 'down1': ffn_w_down[1]}
    small_shapes = [s for s, _ in SMALL.values()]
    small_buf = _pack([_to_rows(w[n]) for n in SMALL], _small_rows(small_shapes))
    *g_first, g_small = _Comm([_AllGather(a.astype(BF16)) for a in first_shards.values()] + [_AllGather(small_buf)]).run("gather_first")
    hosted = {'retention_fwd': ['w_out0', 'lru_out'], 'deltanet_fwd': ['lru_in', 'up0', 'down0', 'up1'], 'ffn0_act': ['down1'], 'rglru_fwd': []}
    gather_in = {host: _Comm([_AllGather(rest_shards[n].astype(BF16)) for n in names]) for host, names in hosted.items()}
    got = dict(zip(first_shards, g_first))
    small_blocks = dict(zip(SMALL, _unpack(g_small, small_shapes, lead=(N_DEV,))))
    full = {n: _join_blocks(small_blocks[n], SMALL[n][1]) for n in SMALL if n != 'ffn_conv_w'}

    w_main, w_narrow = _join_w_in(got['w_in0'], name="join_w_in")
    fcw = [small_blocks['ffn_conv_w'][:, l].reshape(2, FFN_NB, 3, FFN_BLK) for l in range(2)]
    fcb = [ffn_conv_b[l].reshape(2, FFN_NB, 1, FFN_BLK) for l in range(2)]
    gdn_cw = full['gdn_conv_w'][0]
    al_pad = jnp.pad(gdn_a_log, ((0, 0), (HEADS, LANES - 2 * HEADS)))
    dt_pad = jnp.pad(gdn_dt_bias, ((0, 0), (HEADS, LANES - 2 * HEADS)))
    lru_cw, lru_cb = full['lru_conv_w'][0], full['lru_conv_b']
    lru_ba, lru_bx, lru_lam = full['lru_b_a'], full['lru_b_x'], full['lru_lambda']
    wa, wx = lru_w_a[0], lru_w_x[0]

    h0, target = x[0], loss_target[0]
    hn0 = _norm_fwd(h0, norm_mix[0:1], name="mix0_norm")
    proj = _mm(hn0, w_main, name="mix0_in")
    pnarrow = _mm(hn0, w_narrow, name="mix0_in_narrow")
    tables = _ret_tables(t)
    y0, ret_states, g = _ret_fwd(proj, tables, gather_in['retention_fwd'], name="retention_fwd")
    got.update(zip(hosted['retention_fwd'], g))
    y0, gdn_states, g = _gdn_fwd(proj, pnarrow, gdn_cw, al_pad, dt_pad, gdn_out_gain, y0, gather_in['deltanet_fwd'], name="deltanet_fwd")
    got.update(zip(hosted['deltanet_fwd'], g))
    lru_in = _join_blocks(got['lru_in'], 1)
    lru_in_g, lru_in_x = lru_in[:, :D_MODEL], lru_in[:, D_MODEL:]
    lru_out = got['lru_out'].reshape(D_MODEL, D_MODEL)
    w_out0 = got['w_out0'].reshape(D_MODEL, D_MODEL)
    h1 = _mm(y0, w_out0, res=h0, name="mix0_out")
    h2, ffn0_saved, g = _ffn_forward(h1, norm_ffn[0:1], got['up0'], fcw[0], fcb[0], got['down0'].reshape(D_FF, D_MODEL), 0,
                                     gather_in['ffn0_act'])
    got.update(zip(hosted['ffn0_act'], g))
    hn1 = _norm_fwd(h2, norm_mix[1:2], name="mix1_norm")
    gate = _mm(hn1, lru_in_g, name="mix1_in_gate")
    xpre = _mm(hn1, lru_in_x, name="mix1_in_x")
    y1, hs, g = _lru_fwd(gate, xpre, lru_cw, lru_cb, wa, lru_ba, wx, lru_bx, lru_lam, gather_in['rglru_fwd'], name="rglru_fwd")
    got.update(zip(hosted['rglru_fwd'], g))
    h3 = _mm(y1, lru_out, res=h2, name="mix1_out")
    w_up = [got['up0'], got['up1']]
    down = [got['down0'].reshape(D_FF, D_MODEL), got['down1'].reshape(D_FF, D_MODEL)]
    h4, ffn1_saved, _ = _ffn_forward(h3, norm_ffn[1:2], w_up[1], fcw[1], fcb[1], down[1], 1, _Comm([]))
    dh4, d_norm_final, loss_part = _final_loss(h4, norm_final[None, :], target, name="final_norm_loss")

    dh3, gf1 = _ffn_backward(dh4, h3, norm_ffn[1:2], ffn1_saved, w_up[1], fcw[1], fcb[1], down[1], 1)
    dy1 = _mm(dh3, lru_out, tb=True, name="mix1_d_y")
    d_lru_out = _mm(y1, dh3, ta=True, out_dtype=BF16, name="mix1_d_wout")
    dgate, dxpre, d_lcw, d_lcb, d_wa, d_ba, d_wx, d_bx, d_lam = _lru_bwd(
        gate, xpre, hs, dy1, lru_cw, lru_cb, wa, lru_ba, wx, lru_bx, lru_lam, name="rglru_bwd")
    dhn1 = _mm(dgate, lru_in_g, tb=True, name="mix1_d_hn_gate")
    dhn1 = _mm(dxpre, lru_in_x, tb=True, res=dhn1, name="mix1_d_hn_x")
    d_lru_in = jnp.concatenate([_mm(hn1, dgate, ta=True, out_dtype=BF16, name="mix1_d_win_gate"),
                                _mm(hn1, dxpre, ta=True, out_dtype=BF16, name="mix1_d_win_x")], axis=1)
    dh2, d_mix1 = _norm_bwd(h2, norm_mix[1:2], dhn1, dh3, name="mix1_norm_bwd")
    dh1, gf0 = _ffn_backward(dh2, h1, norm_ffn[0:1], ffn0_saved, w_up[0], fcw[0], fcb[0], down[0], 0)
    dy0 = _mm(dh1, w_out0, tb=True, name="mix0_d_y")
    d_w_out0 = _mm(y0, dh1, ta=True, out_dtype=BF16, name="mix0_d_wout")

    def by_core_chip(full_grad, axis):
        blocks = _split_blocks(full_grad, axis)
        return blocks.reshape((4, 2) + blocks.shape[1:]).transpose(1, 0, 2, 3)

    def pair_add(blocks, theirs):
        return {k: _pair_add(b, o, name=f"pair_add_{k}", tr=row_tile(b.shape[2])) for (k, b), o in zip(blocks.items(), theirs)}

    row_tile = lambda rows: rows if rows <= 512 else 256

    rest_blocks = {'w_out0': by_core_chip(d_w_out0, 0), 'lru_in': by_core_chip(d_lru_in, 1), 'lru_out': by_core_chip(d_lru_out, 0),
                   'up0': gf0['w_up'], 'up1': gf1['w_up'], 'down0': gf0['w_down'], 'down1': gf1['w_down']}
    early = {'lru_conv_w': d_lcw[None], 'lru_conv_b': d_lcb, 'lru_b_a': d_ba, 'lru_b_x': d_bx, 'lru_lambda': d_lam,
             'ffn_conv_w': jnp.stack([gf0['conv_w'], gf1['conv_w']]), 'norm_ffn': jnp.concatenate([gf0['norm'], gf1['norm']], axis=0),
             'norm_mix1': d_mix1, 'lru_w_a': d_wa[None], 'lru_w_x': d_wx[None],
             'ffn_conv_b': jnp.concatenate([gf0['conv_b'], gf1['conv_b']], axis=0), 'norm_final': d_norm_final[0]}
    early_buf = _pack([_to_rows(early[n]) for n in EARLY], _small_rows(list(EARLY.values())))
    dproj, (*theirs, got_early) = _ret_bwd(proj, tables, ret_states, dy0,
                                           _Comm([_PairSwap(b) for b in rest_blocks.values()] + [_AllGather(early_buf)]), name="retention_bwd")
    z_rest = pair_add(rest_blocks, theirs)
    (dproj, dnarrow, d_gcw, d_alog, d_dtb, d_gain), w_rest = _gdn_bwd(
        proj, pnarrow, gdn_cw, al_pad, dt_pad, gdn_out_gain, gdn_states, dy0, dproj,
        _Comm([_ChipExchange(z) for z in z_rest.values()]), name="deltanet_bwd")
    dhn0 = _mm(dproj, w_main, tb=True, name="mix0_d_hn")
    dhn0 = _mm(dnarrow, w_narrow, tb=True, res=dhn0, name="mix0_d_hn_narrow")
    d_w_main = _mm(hn0, dproj, ta=True, out_dtype=BF16, name="mix0_d_win")
    d_w_narrow = _mm(hn0, dnarrow, ta=True, out_dtype=BF16, name="mix0_d_win_narrow")
    dx, d_mix0 = _norm_bwd(h0, norm_mix[0:1], dhn0, dh1, name="mix0_norm_bwd")

    first_blocks = {'w_in0': _split_w_in(d_w_main, d_w_narrow, name="split_d_w_in")}
    z_first = pair_add(first_blocks, _Comm([_PairSwap(b) for b in first_blocks.values()]).run("pair_swap_first"))
    late = {'gdn_conv_w': d_gcw[None], 'norm_mix0': d_mix0, 'gdn_a_log': d_alog[:, HEADS:2 * HEADS],
            'gdn_dt_bias': d_dtb[:, HEADS:2 * HEADS], 'gdn_out_gain': d_gain, 'loss': loss_part}
    late_buf = _pack([_to_rows(late[n]) for n in LATE], _small_rows(list(LATE.values())))
    *w_first, got_late = _Comm([_ChipExchange(z) for z in z_first.values()] + [_AllGather(late_buf)]).run("exchange_first")

    summed = {k: _sum_slots(blocks, name=f"sum_blocks_{k}", tr=row_tile(blocks.shape[1]))
              for k, blocks in list(zip(z_rest, w_rest)) + list(zip(z_first, w_first))}
    grads = {'ret_gdn_w_in': summed['w_in0'][None], 'ret_gdn_w_out': summed['w_out0'][None], 'lru_w_in': summed['lru_in'][None],
             'lru_w_out': summed['lru_out'][None], 'ffn_w_up': jnp.stack([summed['up0'], summed['up1']]),
             'ffn_w_down': jnp.stack([summed['down0'], summed['down1']])}
    partial = dict(zip(EARLY, _unpack(_sum_slots(got_early, name="sum_partials_early"), list(EARLY.values()))))
    partial.update(zip(LATE, _unpack(_sum_slots(got_late, name="sum_partials_late"), list(LATE.values()))))
    partial['norm_mix'] = jnp.concatenate([partial.pop('norm_mix0'), partial.pop('norm_mix1')], axis=0)
    loss = partial.pop('loss')[0, 0]
    for n, g_full in partial.items():
        if n in SMALL:
            shard, axis = SMALL[n]
            g_full = lax.dynamic_slice_in_dim(g_full, me * shard[axis], shard[axis], axis=axis)
        grads[n] = g_full

    delta, new_m, new_v = {}, {}, {}
    for n in WEIGHTS:
        delta[n], new_m[n], new_v[n] = _adam(w[n], grads[n], given["m_" + n], given["v_" + n], name=f"adamw_{n}")
    return (loss, dx[None], *[grads[n] for n in WEIGHTS], *[delta[n] for n in WEIGHTS],
            *[new_m[n] for n in WEIGHTS], *[new_v[n] for n in WEIGHTS])
```

```python
import math

import numpy as np
import jax
import jax.numpy as jnp
from jax import lax
from jax.experimental import pallas as pl
from jax.experimental.pallas import tpu as pltpu

F32 = jnp.float32
BF16 = jnp.bfloat16
MESH = pl.DeviceIdType.MESH

N_DEV = 8
LANES = 128
SUBLANES = 8
EPS = 1e-6
D_MODEL = 1024
HEADS = 4
HEAD_DIM = 128
RET_CHUNK = 128
GDN_CHUNK = 64
ROPE_BASE = 10000.0
LRU_C = 8.0
D_FF = 2816
MAIN_IN = 4096
SMALL_IN = 8
QSCALE = HEAD_DIM ** -0.5

ADAM_LR, ADAM_B1, ADAM_B2, ADAM_EPS, ADAM_WD, ADAM_STEP = 0.001, 0.9, 0.999, 1e-08, 0.01, 10


def _cp(sem=None, vmem_mb=None):
    kw = {}
    if sem is not None:
        kw["dimension_semantics"] = sem
    if vmem_mb is not None:
        kw["vmem_limit_bytes"] = vmem_mb << 20
    return pltpu.CompilerParams(**kw)


def _rows(shape):
    return lax.broadcasted_iota(jnp.int32, shape, 0)


def _cols(shape):
    return lax.broadcasted_iota(jnp.int32, shape, 1)


def _shift_down(cur, prev8, s):
    if s == 0:
        return cur
    rc = pltpu.roll(cur, s, 0)
    rp = pltpu.roll(prev8, s, 0)
    top = jnp.where(_rows(prev8.shape) < s, rp, rc[:SUBLANES])
    return jnp.concatenate([top, rc[SUBLANES:]], axis=0)


def _shift_up(cur, next8, s):
    if s == 0:
        return cur
    tt = cur.shape[0]
    rc = pltpu.roll(cur, tt - s, 0)
    rn = pltpu.roll(next8, SUBLANES - s, 0)
    bot = jnp.where(_rows(next8.shape) >= SUBLANES - s, rn, rc[tt - SUBLANES:])
    return jnp.concatenate([rc[:tt - SUBLANES], bot], axis=0)


def _down_fill(x, d, fill):
    return jnp.where(_rows(x.shape) < d, fill, pltpu.roll(x, d, 0))


def _up_fill(x, d, fill):
    tt = x.shape[0]
    return jnp.where(_rows(x.shape) >= tt - d, fill, pltpu.roll(x, tt - d, 0))


def _sigmoid(x):
    return 1.0 / (1.0 + jnp.exp(-x))


def _softplus(x):
    return jnp.maximum(x, 0.0) + jnp.log(1.0 + jnp.exp(-jnp.abs(x)))


def _dot(a, b, dims=(((1,), (0,)), ((), ())), precision=None):
    return lax.dot_general(a, b, dims, preferred_element_type=F32, precision=precision)


NN = (((1,), (0,)), ((), ()))
NT = (((1,), (1,)), ((), ()))
TN = (((0,), (0,)), ((), ()))


def _bdot(a, b, dims=NN):
    return _dot(a.astype(BF16), b.astype(BF16), dims)


def _split(a):
    hi = a.astype(BF16)
    return hi, (a - hi.astype(F32)).astype(BF16)


def _dot3(a, b, dims=NN):
    ah, al = _split(a)
    bh, bl = _split(b)
    return _dot(ah, bh, dims) + (_dot(ah, bl, dims) + _dot(al, bh, dims))


def _tile(dim, target):
    if dim <= target:
        return dim
    best = None
    for c in range(LANES, target + 1, LANES):
        if dim % c == 0:
            best = c
    assert best is not None, (dim, target)
    return best


def _mm(a, b, *, name, ta=False, tb=False, out_dtype=F32, res=None, tm=2048, tn=512, tk=1024):
    m, k = (a.shape[1], a.shape[0]) if ta else a.shape
    n = b.shape[0] if tb else b.shape[1]
    tn, tk = _tile(n, tn), _tile(k, tk)
    tm = _tile(m, tm if max(tn, tk) <= 1024 else tm // 2)
    nk = k // tk
    dims = (((0 if ta else 1,), (1 if tb else 0,)), ((), ()))

    def body(*refs):
        a_ref, b_ref = refs[:2]
        r_ref = refs[2] if res is not None else None
        o_ref = refs[3] if res is not None else refs[2]
        acc = refs[-1]
        kk = pl.program_id(2)
        part = _bdot(a_ref[...], b_ref[...], dims)

        def finish(r):
            if res is not None:
                r = r + r_ref[...]
            o_ref[...] = r.astype(out_dtype)

        if nk == 1:
            finish(part)
            return

        @pl.when(kk == 0)
        def _():
            acc[...] = part

        @pl.when(jnp.logical_and(kk > 0, kk < nk - 1))
        def _():
            acc[...] += part

        @pl.when(kk == nk - 1)
        def _():
            finish(acc[...] + part)

    a_spec = pl.BlockSpec((tk, tm), lambda i, j, kk: (kk, i)) if ta else pl.BlockSpec((tm, tk), lambda i, j, kk: (i, kk))
    b_spec = pl.BlockSpec((tn, tk), lambda i, j, kk: (j, kk)) if tb else pl.BlockSpec((tk, tn), lambda i, j, kk: (kk, j))
    o_spec = pl.BlockSpec((tm, tn), lambda i, j, kk: (i, j))
    in_specs = [a_spec, b_spec] + ([o_spec] if res is not None else [])
    args = (a, b) + ((res,) if res is not None else ())
    return pl.pallas_call(
        body, grid=(m // tm, n // tn, nk), in_specs=in_specs, out_specs=o_spec,
        out_shape=jax.ShapeDtypeStruct((m, n), out_dtype),
        scratch_shapes=[pltpu.VMEM((tm, tn), F32)] if nk > 1 else [], name=name,
        compiler_params=_cp(("parallel", "parallel", "arbitrary"), 56),
    )(*args)


def _mmx(a, b, *, dims, grid, a_spec, b_spec, o_spec, out_shape, tile, name, res=None, split_rows=None, pairs=False):
    nk = grid[-1]

    def body(*refs):
        a_ref, b_ref = refs[:2]
        r_ref = refs[2] if res is not None else None
        o_ref = refs[3] if res is not None else refs[2]
        acc = refs[-1]
        if pairs:
            part = _bdot(a_ref[0], b_ref[0], dims) + _bdot(a_ref[1], b_ref[1], dims)
        else:
            part = _bdot(a_ref[...], b_ref[...], dims)

        def finish(r):
            if res is not None:
                r = r + r_ref[...]
            if split_rows is None:
                o_ref[...] = r.astype(o_ref.dtype)
            else:
                o_ref[0] = r[:split_rows].astype(o_ref.dtype)
                o_ref[1] = r[split_rows:].astype(o_ref.dtype)

        if nk == 1:
            finish(part)
            return
        kk = pl.program_id(len(grid) - 1)

        @pl.when(kk == 0)
        def _():
            acc[...] = part

        @pl.when(jnp.logical_and(kk > 0, kk < nk - 1))
        def _():
            acc[...] += part

        @pl.when(kk == nk - 1)
        def _():
            finish(acc[...] + part)

    args = (a, b) + ((res,) if res is not None else ())
    return pl.pallas_call(
        body, grid=grid, in_specs=[a_spec, b_spec] + ([o_spec] if res is not None else []), out_specs=o_spec,
        out_shape=out_shape, scratch_shapes=[pltpu.VMEM(tile, F32)] if nk > 1 else [], name=name,
        compiler_params=_cp(("parallel",) * (len(grid) - 1) + ("arbitrary",), 56),
    )(*args)


W_IN_BLK = 513
W_IN_TR = 256


def _join_w_in(blocks, *, name):
    _, d, _ = blocks.shape
    tr = W_IN_TR

    def body(x_ref, main_ref, narrow_ref):
        for m in range(MAIN_IN // LANES):
            lo = LANES * m
            dev, off = divmod(lo, W_IN_BLK)
            if off + LANES <= W_IN_BLK:
                main_ref[:, lo:lo + LANES] = x_ref[dev, :, off:off + LANES]
            else:
                main_ref[:, lo:lo + LANES] = jnp.concatenate(
                    [x_ref[dev, :, off:W_IN_BLK], x_ref[dev + 1, :, 0:LANES - (W_IN_BLK - off)]], axis=1)
        tail = x_ref[N_DEV - 1, :, W_IN_BLK - SMALL_IN:W_IN_BLK]
        narrow_ref[...] = jnp.concatenate([tail, jnp.zeros((tr, LANES - SMALL_IN), tail.dtype)], axis=1)

    return pl.pallas_call(
        body, grid=(d // tr,), in_specs=[pl.BlockSpec((N_DEV, tr, W_IN_BLK), lambda i: (0, i, 0))],
        out_specs=[pl.BlockSpec((tr, MAIN_IN), lambda i: (i, 0)), pl.BlockSpec((tr, LANES), lambda i: (i, 0))],
        out_shape=[jax.ShapeDtypeStruct((d, MAIN_IN), blocks.dtype), jax.ShapeDtypeStruct((d, LANES), blocks.dtype)],
        name=name, compiler_params=_cp(("parallel",), 48),
    )(blocks)


def _split_w_in(main, narrow, *, name):
    d = main.shape[0]
    tr = W_IN_TR

    def body(m_ref, n_ref, o_ref):
        for dev in range(N_DEV):
            lo = W_IN_BLK * dev
            if dev < N_DEV - 1:
                piece = m_ref[:, lo:lo + W_IN_BLK]
            else:
                piece = jnp.concatenate([m_ref[:, lo:MAIN_IN], n_ref[:, 0:SMALL_IN]], axis=1)
            o_ref[dev % 2, dev // 2] = piece

    return pl.pallas_call(
        body, grid=(d // tr,),
        in_specs=[pl.BlockSpec((tr, MAIN_IN), lambda i: (i, 0)), pl.BlockSpec((tr, LANES), lambda i: (i, 0))],
        out_specs=pl.BlockSpec((2, N_DEV // 2, tr, W_IN_BLK), lambda i: (0, 0, i, 0)),
        out_shape=jax.ShapeDtypeStruct((2, N_DEV // 2, d, W_IN_BLK), main.dtype),
        name=name, compiler_params=_cp(("parallel",), 48),
    )(main, narrow)


def _norm_fwd(h, gain, *, name, tt=1024):
    t, d = h.shape
    tt = min(tt, t)

    def body(h_ref, g_ref, o_ref):
        x = h_ref[...]
        r = lax.rsqrt(jnp.mean(x * x, axis=-1, keepdims=True) + EPS)
        o_ref[...] = (x * r * g_ref[...]).astype(BF16)

    row = pl.BlockSpec((tt, d), lambda i: (i, 0))
    return pl.pallas_call(
        body, grid=(t // tt,), in_specs=[row, pl.BlockSpec((1, d), lambda i: (0, 0))], out_specs=row,
        out_shape=jax.ShapeDtypeStruct((t, d), BF16), name=name, compiler_params=_cp(("parallel",), 48),
    )(h, gain)


def _norm_bwd(h, gain, dhn, dres, *, name, tt=512):
    t, d = h.shape
    tt = min(tt, t)

    def body(h_ref, g_ref, dy_ref, dr_ref, dx_ref, dg_ref):
        x, dy = h_ref[...], dy_ref[...]
        r = lax.rsqrt(jnp.mean(x * x, axis=-1, keepdims=True) + EPS)
        xh = x * r

        @pl.when(pl.program_id(0) == 0)
        def _():
            dg_ref[...] = jnp.zeros_like(dg_ref)

        dg_ref[...] += jnp.sum(dy * xh, axis=0, keepdims=True)
        dxh = dy * g_ref[...]
        dx_ref[...] = dr_ref[...] + r * (dxh - xh * jnp.mean(dxh * xh, axis=-1, keepdims=True))

    row = pl.BlockSpec((tt, d), lambda i: (i, 0))
    vec = pl.BlockSpec((1, d), lambda i: (0, 0))
    return pl.pallas_call(
        body, grid=(t // tt,), in_specs=[row, vec, row, row], out_specs=[row, vec],
        out_shape=[jax.ShapeDtypeStruct((t, d), F32), jax.ShapeDtypeStruct((1, d), F32)],
        name=name, compiler_params=_cp(("arbitrary",), 48),
    )(h, gain, dhn, dres)


def _final_loss(h, gain, target, *, name, tt=512):
    t, d = h.shape
    tt = min(tt, t)

    def body(h_ref, g_ref, tg_ref, dx_ref, dg_ref, loss_ref):
        x = h_ref[...]
        r = lax.rsqrt(jnp.mean(x * x, axis=-1, keepdims=True) + EPS)
        xh = x * r
        err = xh * g_ref[...] - tg_ref[...]

        @pl.when(pl.program_id(0) == 0)
        def _():
            dg_ref[...] = jnp.zeros_like(dg_ref)
            loss_ref[...] = jnp.zeros_like(loss_ref)

        loss_ref[...] += 0.5 * jnp.sum(jnp.mean(err * err, axis=-1, keepdims=True), axis=0, keepdims=True)
        dy = err * (1.0 / d)
        dg_ref[...] += jnp.sum(dy * xh, axis=0, keepdims=True)
        dxh = dy * g_ref[...]
        dx_ref[...] = r * (dxh - xh * jnp.mean(dxh * xh, axis=-1, keepdims=True))

    row = pl.BlockSpec((tt, d), lambda i: (i, 0))
    vec = pl.BlockSpec((1, d), lambda i: (0, 0))
    return pl.pallas_call(
        body, grid=(t // tt,), in_specs=[row, vec, row],
        out_specs=[row, vec, pl.BlockSpec((1, 1), lambda i: (0, 0))],
        out_shape=[jax.ShapeDtypeStruct((t, d), F32), jax.ShapeDtypeStruct((1, d), F32), jax.ShapeDtypeStruct((1, 1), F32)],
        name=name, compiler_params=_cp(("arbitrary",), 48),
    )(h, gain, target)


FFN_BLK = 704
FFN_NB = 4
FFN_TT = 512


def _prev8(n, tt):
    return jnp.maximum(n * (tt // SUBLANES) - 1, 0)


def _ffn_conv(cur, prev8, w, b):
    s1 = _shift_down(cur, prev8, 1)
    s2 = _shift_down(cur, prev8, 2)
    return w[0:1] * s2 + w[1:2] * s1 + w[2:3] * cur + b, s1, s2


def _ffn_specs(t, tt, order):
    pair = lambda rows, row_index: pl.BlockSpec((2, None, rows, FFN_BLK), lambda j, n: (0, j, row_index(n), 0))
    return dict(cur=pair(tt, order), prev=pair(SUBLANES, lambda n: _prev8(order(n), tt)), w=pair(3, lambda n: 0), b=pair(1, lambda n: 0),
                one=pl.BlockSpec((None, tt, FFN_BLK), lambda j, n: (j, order(n), 0)))


def _ffn_act_fwd(up, cw, cb, comm, *, name):
    t = up.shape[2]
    tt = min(FFN_TT, t)
    nt = t // tt
    sp = _ffn_specs(t, tt, lambda n: n)
    hbm = pl.BlockSpec(memory_space=pl.ANY)

    def body(u_ref, p_ref, w_ref, b_ref, *rest):
        comm_in, (o_ref,), comm_out, _, comm_sems = comm.split(rest, n_out=1, n_scratch=0)
        j, n = pl.program_id(0), pl.program_id(1)

        @pl.when(jnp.logical_and(j == 0, n == 0))
        def _():
            comm.start(comm_in, comm_out, comm_sems)

        first = n == 0
        gate, _, _ = _ffn_conv(u_ref[0], jnp.where(first, 0.0, p_ref[0]), w_ref[0], b_ref[0])
        val, _, _ = _ffn_conv(u_ref[1], jnp.where(first, 0.0, p_ref[1]), w_ref[1], b_ref[1])
        o_ref[...] = (gate * _sigmoid(gate) * val).astype(BF16)

        @pl.when(jnp.logical_and(j == FFN_NB - 1, n == nt - 1))
        def _():
            comm.finish(comm_in, comm_out, comm_sems)

    outs = pl.pallas_call(
        body, grid=(FFN_NB, nt), in_specs=[sp["cur"], sp["prev"], sp["w"], sp["b"]] + [hbm] * len(comm.arrays),
        out_specs=[sp["one"]] + [hbm] * len(comm.out_shapes),
        out_shape=[jax.ShapeDtypeStruct((FFN_NB, t, FFN_BLK), BF16)] + comm.out_shapes, scratch_shapes=comm.scratch, name=name,
        compiler_params=_cp(("arbitrary", "arbitrary"), 48),
    )(up, up, cw, cb, *comm.arrays)
    return outs[0], outs[1:]


def _ffn_act_bwd(up, da, cw, cb, *, name):
    t = up.shape[2]
    tt = min(FFN_TT, t)
    nt = t // tt
    sp = _ffn_specs(t, tt, lambda n: nt - 1 - n)

    def body(u_ref, p_ref, da_ref, w_ref, b_ref, du_ref, dw_ref, db_ref, head):
        n = pl.program_id(1)
        tile0 = n == nt - 1

        @pl.when(n == 0)
        def _():
            for r in (head, dw_ref, db_ref):
                r[...] = jnp.zeros_like(r)

        convs = [_ffn_conv(u_ref[s], jnp.where(tile0, 0.0, p_ref[s]), w_ref[s], b_ref[s]) for s in range(2)]
        gate, val = convs[0][0], convs[1][0]
        d = da_ref[...]
        sg = _sigmoid(gate)
        dcs = (d * val * sg * (1.0 + gate * (1.0 - sg)), d * gate * sg)
        for s in range(2):
            dc, w, hd = dcs[s], w_ref[s], head[s]
            _, x1, x2 = convs[s]
            du_ref[s] = (w[2:3] * dc + w[1:2] * _shift_up(dc, hd, 1) + w[0:1] * _shift_up(dc, hd, 2)).astype(BF16)
            dw_ref[s, 0:1, :] += jnp.sum(dc * x2, axis=0, keepdims=True)
            dw_ref[s, 1:2, :] += jnp.sum(dc * x1, axis=0, keepdims=True)
            dw_ref[s, 2:3, :] += jnp.sum(dc * u_ref[s], axis=0, keepdims=True)
            db_ref[s] += jnp.sum(dc, axis=0, keepdims=True)
            head[s] = dc[:SUBLANES]

    return pl.pallas_call(
        body, grid=(FFN_NB, nt), in_specs=[sp["cur"], sp["prev"], sp["one"], sp["w"], sp["b"]],
        out_specs=[sp["cur"], sp["w"], sp["b"]],
        out_shape=[jax.ShapeDtypeStruct(up.shape, BF16), jax.ShapeDtypeStruct(cw.shape, F32), jax.ShapeDtypeStruct(cb.shape, F32)],
        scratch_shapes=[pltpu.VMEM((2, SUBLANES, FFN_BLK), F32)], name=name,
        compiler_params=_cp(("parallel", "arbitrary"), 48),
    )(up, up, da, cw, cb)


LRU_TT = 256
LRU_CT = 512
GELU_C = math.sqrt(2.0 / math.pi)
GELU_A = 0.044715


def _gelu(x):
    return 0.5 * x * (1.0 + jnp.tanh(GELU_C * (x + GELU_A * x * x * x)))


def _gelu_grad(x):
    th = jnp.tanh(GELU_C * (x + GELU_A * x * x * x))
    return 0.5 * (1.0 + th) + 0.5 * x * (1.0 - th * th) * GELU_C * (1.0 + 3.0 * GELU_A * x * x)


def _neg_expm1(x):
    poly = -x * (1.0 + x * (0.5 + x * (1.0 / 6 + x * (1.0 / 24 + x * (1.0 / 120)))))
    return jnp.where(x > -0.1, poly, 1.0 - jnp.exp(x))


def _conv4(x, p8, w, b=None):
    s1, s2, s3 = _shift_down(x, p8, 1), _shift_down(x, p8, 2), _shift_down(x, p8, 3)
    y = w[0:1] * s3 + w[1:2] * s2 + w[2:3] * s1 + w[3:4] * x
    return (y if b is None else y + b), (s1, s2, s3)


def _conv4_bwd(dy, head, x, shifts, w):
    s1, s2, s3 = shifts
    dx = w[3:4] * dy + w[2:3] * _shift_up(dy, head, 1) + w[1:2] * _shift_up(dy, head, 2) + w[0:1] * _shift_up(dy, head, 3)
    dws = [jnp.sum(dy * s, axis=0, keepdims=True) for s in (s3, s2, s1, x)]
    return dx, dws


def _blockdiag(x, w_ref, dims=NN):
    nb = x.shape[1] // LANES
    return jnp.concatenate([_bdot(x[:, LANES * i:LANES * (i + 1)], w_ref[i], dims) for i in range(nb)], axis=1)


def _lru_gates(xr, wa_ref, wx_ref, ba, bx, lam):
    r = _sigmoid(_blockdiag(xr, wa_ref) + ba)
    i = _sigmoid(_blockdiag(xr, wx_ref) + bx)
    sp = _softplus(-lam)
    la = -LRU_C * r * sp
    a = jnp.exp(la)
    mult = jnp.sqrt(_neg_expm1(2.0 * la))
    return r, i, sp, a, mult


def _lru_specs(t, tt, ct, order):
    nb = ct // LANES
    cur = pl.BlockSpec((tt, ct), lambda j, n: (order(n), j))
    prev = pl.BlockSpec((SUBLANES, ct), lambda j, n: (_prev8(order(n), tt), j))
    vec = lambda rows: pl.BlockSpec((rows, ct), lambda j, n: (0, j))
    blk = pl.BlockSpec((nb, LANES, LANES), lambda j, n: (j, 0, 0))
    return cur, prev, vec, blk


def _lru_fwd(gate, xpre, cw, cb, wa, ba, wx, bx, lam, comm, *, name):
    t, c = gate.shape
    tt, ct = min(LRU_TT, t), LRU_CT
    nj, nt = c // ct, t // tt
    cur, prev, vec, blk = _lru_specs(t, tt, ct, lambda n: n)
    hbm = pl.BlockSpec(memory_space=pl.ANY)

    def body(gate_ref, x_ref, p_ref, cw_ref, cb_ref, wa_ref, ba_ref, wx_ref, bx_ref, lam_ref, *rest):
        comm_in, (y_ref, hs_ref), comm_out, (carry,), comm_sems = comm.split(rest, n_out=2, n_scratch=1)
        j, n = pl.program_id(0), pl.program_id(1)

        @pl.when(jnp.logical_and(j == 0, n == 0))
        def _():
            comm.start(comm_in, comm_out, comm_sems)

        @pl.when(n == 0)
        def _():
            carry[...] = jnp.zeros_like(carry)

        p8 = jnp.where(n == 0, 0.0, p_ref[...])
        xr, _ = _conv4(x_ref[...], p8, cw_ref[...], cb_ref[...])
        r, i, sp, a, mult = _lru_gates(xr, wa_ref, wx_ref, ba_ref[...], bx_ref[...], lam_ref[...])
        acc_a, acc_b = a, mult * (i * xr)
        d = 1
        while d < tt:
            acc_b = acc_a * _down_fill(acc_b, d, 0.0) + acc_b
            acc_a = acc_a * _down_fill(acc_a, d, 1.0)
            d *= 2
        hs = acc_b + acc_a * carry[0:1]
        carry[...] = jnp.broadcast_to(hs[tt - 1:tt], carry.shape)
        hs_ref[...] = hs
        y_ref[...] = (_gelu(gate_ref[...]) * hs).astype(BF16)

        @pl.when(jnp.logical_and(j == nj - 1, n == nt - 1))
        def _():
            comm.finish(comm_in, comm_out, comm_sems)

    outs = pl.pallas_call(
        body, grid=(nj, nt),
        in_specs=[cur, cur, prev, vec(4), vec(1), blk, vec(1), blk, vec(1), vec(1)] + [hbm] * len(comm.arrays),
        out_specs=[cur, cur] + [hbm] * len(comm.out_shapes),
        out_shape=[jax.ShapeDtypeStruct((t, c), BF16), jax.ShapeDtypeStruct((t, c), F32)] + comm.out_shapes,
        scratch_shapes=[pltpu.VMEM((SUBLANES, ct), F32)] + comm.scratch, name=name,
        compiler_params=_cp(("arbitrary", "arbitrary"), 48),
    )(gate, xpre, xpre, cw, cb, wa, ba, wx, bx, lam, *comm.arrays)
    return outs[0], outs[1], outs[2:]


def _lru_bwd(gate, xpre, hs, dy, cw, cb, wa, ba, wx, bx, lam, *, name):
    t, c = gate.shape
    tt, ct = min(LRU_TT, t), LRU_CT
    nt = t // tt
    cur, prev, vec, blk = _lru_specs(t, tt, ct, lambda n: nt - 1 - n)

    def body(gate_ref, x_ref, p_ref, hs_ref, phs_ref, dy_ref, cw_ref, cb_ref, wa_ref, ba_ref, wx_ref, bx_ref, lam_ref,
             dgate_ref, dx_ref, dcw_ref, dcb_ref, dwa_ref, dba_ref, dwx_ref, dbx_ref, dlam_ref, carry, head):
        n = pl.program_id(1)
        tile0 = n == nt - 1

        @pl.when(n == 0)
        def _():
            for ref in (carry, head, dcw_ref, dcb_ref, dwa_ref, dba_ref, dwx_ref, dbx_ref, dlam_ref):
                ref[...] = jnp.zeros_like(ref)

        xp, cwv, lam = x_ref[...], cw_ref[...], lam_ref[...]
        p8 = jnp.where(tile0, 0.0, p_ref[...])
        xr, shifts = _conv4(xp, p8, cwv, cb_ref[...])
        r, i, sp, a, mult = _lru_gates(xr, wa_ref, wx_ref, ba_ref[...], bx_ref[...], lam)
        gate, hsv, dyv = gate_ref[...], hs_ref[...], dy_ref[...]
        dgate_ref[...] = (dyv * hsv * _gelu_grad(gate)).astype(BF16)
        acc_b = dyv * _gelu(gate) + jnp.where(_rows(a.shape) == tt - 1, carry[0:1], 0.0)
        acc_a = _up_fill(a, 1, 0.0)
        d = 1
        while d < tt:
            acc_b = acc_b + acc_a * _up_fill(acc_b, d, 0.0)
            acc_a = acc_a * _up_fill(acc_a, d, 0.0)
            d *= 2
        gsum = acc_b
        carry[...] = jnp.broadcast_to(a[0:1] * gsum[0:1], carry.shape)
        hprev = _shift_down(hsv, jnp.where(tile0, 0.0, phs_ref[...]), 1)
        da = gsum * hprev
        dmult = gsum * i * xr
        di = gsum * mult * xr
        dxr = gsum * mult * i
        dla = da * a - dmult * (a * a) / mult
        dr = dla * (-LRU_C * sp)
        dlam_ref[...] += jnp.sum(dla * (-LRU_C * r), axis=0, keepdims=True) * (-_sigmoid(-lam))
        dpa = dr * r * (1.0 - r)
        dpx = di * i * (1.0 - i)
        dba_ref[...] += jnp.sum(dpa, axis=0, keepdims=True)
        dbx_ref[...] += jnp.sum(dpx, axis=0, keepdims=True)
        dxr = dxr + _blockdiag(dpa, wa_ref, NT) + _blockdiag(dpx, wx_ref, NT)
        for b in range(ct // LANES):
            sl = slice(LANES * b, LANES * (b + 1))
            dwa_ref[b] += _bdot(xr[:, sl], dpa[:, sl], TN)
            dwx_ref[b] += _bdot(xr[:, sl], dpx[:, sl], TN)
        dx, dws = _conv4_bwd(dxr, head[...], xp, shifts, cwv)
        dx_ref[...] = dx.astype(BF16)
        for k in range(4):
            dcw_ref[k:k + 1, :] += dws[k]
        dcb_ref[...] += jnp.sum(dxr, axis=0, keepdims=True)
        head[...] = dxr[:SUBLANES]

    return pl.pallas_call(
        body, grid=(c // ct, nt),
        in_specs=[cur, cur, prev, cur, prev, cur, vec(4), vec(1), blk, vec(1), blk, vec(1), vec(1)],
        out_specs=[cur, cur, vec(4), vec(1), blk, vec(1), blk, vec(1), vec(1)],
        out_shape=[jax.ShapeDtypeStruct((t, c), BF16)] * 2 + [jax.ShapeDtypeStruct((4, c), F32), jax.ShapeDtypeStruct((1, c), F32),
                   jax.ShapeDtypeStruct(wa.shape, F32), jax.ShapeDtypeStruct((1, c), F32),
                   jax.ShapeDtypeStruct(wx.shape, F32), jax.ShapeDtypeStruct((1, c), F32), jax.ShapeDtypeStruct((1, c), F32)],
        scratch_shapes=[pltpu.VMEM((SUBLANES, ct), F32)] * 2, name=name,
        compiler_params=_cp(("parallel", "arbitrary"), 48),
    )(gate, xpre, xpre, hs, hs, dy, cw, cb, wa, ba, wx, bx, lam)


RET_W = HEADS * HEAD_DIM
HALF = HEAD_DIM // 2


def _ret_tables(t):
    c = RET_CHUNK
    inv_freq = ROPE_BASE ** (-jnp.arange(HALF, dtype=F32) / HALF)
    ang = jnp.arange(t, dtype=jnp.int32).astype(F32)[:, None] * inv_freq[None, :]
    cos, sin = jnp.cos(ang), jnp.sin(ang)
    cosf = jnp.concatenate([cos, cos], axis=1)
    sinf = jnp.concatenate([-sin, sin], axis=1)
    log_gamma = jnp.log1p(-jnp.exp2(-5.0 - jnp.arange(HEADS, dtype=F32)))
    idx = jnp.arange(c, dtype=F32)
    rel = idx[:, None] - idx[None, :]
    causal = rel >= 0
    dmask = jnp.where(causal, jnp.exp(log_gamma[:, None, None] * jnp.where(causal, rel, 0.0)), 0.0)
    ktail = jnp.exp(log_gamma[:, None] * (c - 1 - idx))
    qdec = jnp.exp(log_gamma[:, None] * (idx + 1.0))
    rowtab = jnp.broadcast_to(jnp.stack([ktail, qdec], axis=1)[..., None], (HEADS, 2, c, HEAD_DIM))
    cdec = jnp.broadcast_to(jnp.exp(log_gamma * c)[:, None, None], (HEADS, SUBLANES, HEAD_DIM))
    return cosf, sinf, dmask, rowtab, cdec


def _rotary(x, cosf, sinf):
    return x * cosf + pltpu.roll(x, HALF, 1) * sinf


def _rotary_t(dx, cosf, sinf):
    return dx * cosf + pltpu.roll(dx * sinf, HALF, 1)


def _ret_specs(c, order):
    full = lambda shape: pl.BlockSpec(shape, lambda n: (0,) * len(shape))
    return dict(
        proj=pl.BlockSpec((c, 4 * RET_W), lambda n: (order(n), 0)),
        rot=pl.BlockSpec((c, HEAD_DIM), lambda n: (order(n), 0)),
        dmask=full((HEADS, c, c)), rowtab=full((HEADS, 2, c, HEAD_DIM)), cdec=full((HEADS, SUBLANES, HEAD_DIM)),
        state=pl.BlockSpec((1, HEADS, HEAD_DIM, HEAD_DIM), lambda n: (order(n), 0, 0, 0)),
        half=pl.BlockSpec((c, RET_W), lambda n: (order(n), 0)),
    )


def _ret_head(p_ref, h, cosf, sinf):
    sl = lambda j: slice(j * RET_W + h * HEAD_DIM, j * RET_W + (h + 1) * HEAD_DIM)
    q, k, v, g = p_ref[:, sl(0)], p_ref[:, sl(1)], p_ref[:, sl(2)], p_ref[:, sl(3)]
    return _rotary(q, cosf, sinf), _rotary(k, cosf, sinf) * QSCALE, v, g


def _ret_fwd(proj, tables, comm, *, name):
    t = proj.shape[0]
    c = RET_CHUNK
    nc = t // c
    sp = _ret_specs(c, lambda n: n)
    hbm = pl.BlockSpec(memory_space=pl.ANY)

    def body(p_ref, cos_ref, sin_ref, dm_ref, rt_ref, cd_ref, *rest):
        comm_in, (y_ref, s_ref), comm_out, (state,), comm_sems = comm.split(rest, n_out=2, n_scratch=1)

        @pl.when(pl.program_id(0) == 0)
        def _():
            state[...] = jnp.zeros_like(state)
            comm.start(comm_in, comm_out, comm_sems)

        cosf, sinf = cos_ref[...], sin_ref[...]
        for h in range(HEADS):
            qr, kr, v, g = _ret_head(p_ref, h, cosf, sinf)
            s0 = state[h]
            s_ref[0, h] = s0
            scores = _bdot(qr, kr, NT) * dm_ref[h]
            o = _bdot(scores, v) + _bdot(qr * rt_ref[h, 1], s0)
            state[h] = s0 * cd_ref[h][0:1] + _bdot(kr * rt_ref[h, 0], v, TN)
            rinv = lax.rsqrt(jnp.mean(o * o, axis=-1, keepdims=True) + EPS)
            y_ref[:, h * HEAD_DIM:(h + 1) * HEAD_DIM] = (o * rinv * (g * _sigmoid(g))).astype(BF16)

        @pl.when(pl.program_id(0) == nc - 1)
        def _():
            comm.finish(comm_in, comm_out, comm_sems)

    outs = pl.pallas_call(
        body, grid=(nc,),
        in_specs=[sp["proj"], sp["rot"], sp["rot"], sp["dmask"], sp["rowtab"], sp["cdec"]] + [hbm] * len(comm.arrays),
        out_specs=[sp["half"], sp["state"]] + [hbm] * len(comm.out_shapes),
        out_shape=[jax.ShapeDtypeStruct((t, 2 * RET_W), BF16), jax.ShapeDtypeStruct((nc, HEADS, HEAD_DIM, HEAD_DIM), F32)]
        + comm.out_shapes,
        scratch_shapes=[pltpu.VMEM((HEADS, HEAD_DIM, HEAD_DIM), F32)] + comm.scratch, name=name,
        compiler_params=_cp(("arbitrary",), 48),
    )(proj, *tables, *comm.arrays)
    return outs[0], outs[1], outs[2:]


def _ret_bwd(proj, tables, states, dy, comm, *, name):
    t = proj.shape[0]
    c = RET_CHUNK
    nc = t // c
    sp = _ret_specs(c, lambda n: nc - 1 - n)
    hbm = pl.BlockSpec(memory_space=pl.ANY)

    def body(p_ref, cos_ref, sin_ref, dm_ref, rt_ref, cd_ref, s_ref, dy_ref, *rest):
        comm_in, (dp_ref,), comm_out, (dstate,), comm_sems = comm.split(rest, n_out=1, n_scratch=1)

        @pl.when(pl.program_id(0) == 0)
        def _():
            dstate[...] = jnp.zeros_like(dstate)
            comm.start(comm_in, comm_out, comm_sems)

        cosf, sinf = cos_ref[...], sin_ref[...]
        for h in range(HEADS):
            qr, kr, v, g = _ret_head(p_ref, h, cosf, sinf)
            s0, dm, ktl, qdc = s_ref[0, h], dm_ref[h], rt_ref[h, 0], rt_ref[h, 1]
            scores = _bdot(qr, kr, NT) * dm
            qd, kt = qr * qdc, kr * ktl
            o = _bdot(scores, v) + _bdot(qd, s0)
            rinv = lax.rsqrt(jnp.mean(o * o, axis=-1, keepdims=True) + EPS)
            oh = o * rinv
            sg = _sigmoid(g)
            dyh = dy_ref[:, h * HEAD_DIM:(h + 1) * HEAD_DIM]
            dg = dyh * oh * sg * (1.0 + g * (1.0 - sg))
            dyo = dyh * (g * sg)
            do = rinv * (dyo - oh * jnp.mean(dyo * oh, axis=-1, keepdims=True))
            ds1 = dstate[h]
            dsc = _bdot(do, v, NT) * dm
            dv = _bdot(scores, do, TN) + _bdot(kt, ds1)
            dqr = _bdot(dsc, kr) + _bdot(do, s0, NT) * qdc
            dkr = (_bdot(dsc, qr, TN) + _bdot(v, ds1, NT) * ktl) * QSCALE
            dstate[h] = ds1 * cd_ref[h][0:1] + _bdot(qd, do, TN)
            pieces = (_rotary_t(dqr, cosf, sinf), _rotary_t(dkr, cosf, sinf), dv, dg)
            for j, piece in enumerate(pieces):
                dp_ref[:, j * RET_W + h * HEAD_DIM:j * RET_W + (h + 1) * HEAD_DIM] = piece.astype(BF16)

        @pl.when(pl.program_id(0) == nc - 1)
        def _():
            comm.finish(comm_in, comm_out, comm_sems)

    outs = pl.pallas_call(
        body, grid=(nc,),
        in_specs=[sp["proj"], sp["rot"], sp["rot"], sp["dmask"], sp["rowtab"], sp["cdec"], sp["state"], sp["half"]]
        + [hbm] * len(comm.arrays),
        out_specs=[sp["proj"]] + [hbm] * len(comm.out_shapes),
        out_shape=[jax.ShapeDtypeStruct((t, 8 * RET_W), BF16)] + comm.out_shapes,
        scratch_shapes=[pltpu.VMEM((HEADS, HEAD_DIM, HEAD_DIM), F32)] + comm.scratch, name=name,
        compiler_params=_cp(("arbitrary",), 48),
    )(proj, *tables, states, dy, *comm.arrays)
    return outs[0], outs[1:]


GDN_W = HEADS * HEAD_DIM
GDN_CONV = 3 * GDN_W
NEUMANN_STEPS = 5


def _gdn_gates(ps, al, dt):
    return _sigmoid(ps), -jnp.exp(al) * _softplus(ps + dt)


def _cumsum_rows(x):
    d = 1
    while d < x.shape[0]:
        x = x + _down_fill(x, d, 0.0)
        d *= 2
    return x


def _rev_cumsum_rows(x):
    d = 1
    while d < x.shape[0]:
        x = x + _up_fill(x, d, 0.0)
        d *= 2
    return x


class _Chunk:
    pass


def _gdn_chunk(qc, kc, v, beta, g, s0, inv=None):
    c = GDN_CHUNK
    z = _Chunk()
    z.rq = lax.rsqrt(jnp.sum(qc * qc, axis=-1, keepdims=True) + EPS)
    z.rk = lax.rsqrt(jnp.sum(kc * kc, axis=-1, keepdims=True) + EPS)
    z.qn, z.k = qc * z.rq, kc * z.rk
    z.q = z.qn * QSCALE
    z.v, z.beta = v, beta
    gc = _cumsum_rows(jnp.broadcast_to(g, (c, LANES)))
    ri, ci = _rows((c, c)), _cols((c, c))
    z.tril, z.strict = ri >= ci, ri > ci
    diff = gc[:, :c] - gc.T[:c, :]
    z.decay = jnp.where(z.tril, jnp.exp(jnp.where(z.tril, diff, 0.0)), 0.0)
    z.eg = jnp.exp(gc)
    glast = gc[c - 1:c, :]
    z.egl = jnp.exp(glast - gc)
    z.cd = jnp.exp(glast)
    z.kb = z.k * beta
    both = _bdot(jnp.concatenate([z.kb, z.q], axis=0), z.k, NT)
    z.m, z.qk = both[:c], both[c:]
    if inv is None:
        neg = -jnp.where(z.strict, z.m * z.decay, 0.0)
        inv = (ri == ci).astype(F32) + neg
        pw = neg
        for _ in range(NEUMANN_STEPS):
            pw = _dot3(pw, pw)
            inv = inv + _dot3(inv, pw)
    z.inv = inv
    z.vb, z.kbg = v * beta, z.kb * z.eg
    solved = _dot3(inv, jnp.concatenate([z.vb, z.kbg], axis=1))
    z.u, z.w = solved[:, :HEAD_DIM], solved[:, HEAD_DIM:]
    z.attn = jnp.where(z.tril, z.qk * z.decay, 0.0)
    z.qd, z.kt = z.q * z.eg, z.k * z.egl
    through = _bdot(jnp.concatenate([z.w, z.qd], axis=0), s0)
    z.vnew = z.u - through[:c]
    z.o = through[c:] + _bdot(z.attn, z.vnew)
    z.s1 = s0 * z.cd + _bdot(z.kt, z.vnew, TN)
    return z


def _gdn_chunk_bwd(z, s0, do, ds1):
    c = GDN_CHUNK
    dvnew = _bdot(z.attn, do, TN) + _bdot(z.kt, ds1)
    against = _bdot(do, jnp.concatenate([s0, z.vnew], axis=0), NT)
    dqd = against[:, :HEAD_DIM]
    dattn = jnp.where(z.tril, against[:, HEAD_DIM:], 0.0)
    ds0 = ds1 * z.cd + _bdot(jnp.concatenate([z.qd, -z.w], axis=0), jnp.concatenate([do, dvnew], axis=0), TN)
    dcd = jnp.sum(jnp.sum(s0 * ds1, axis=1, keepdims=True), axis=0, keepdims=True)
    dkt = _bdot(z.vnew, ds1, NT)
    dw = -_bdot(dvnew, s0, NT)
    dsolved = _dot3(z.inv, jnp.concatenate([dvnew, dw], axis=1), TN)
    dvb, dkbg = dsolved[:, :HEAD_DIM], dsolved[:, HEAD_DIM:]
    dl = jnp.where(z.strict, -_bdot(dsolved, jnp.concatenate([z.u, z.w], axis=1), NT), 0.0)
    dml = dl * z.decay
    dqk = dattn * z.decay
    ddecay = (dl * z.m + dattn * z.qk) * z.decay
    stacked = jnp.concatenate([dqk, dml], axis=0)
    onto_k = _bdot(stacked, z.k)
    dq = onto_k[:c] + dqd * z.eg
    dkb = onto_k[c:] + dkbg * z.eg
    dk = _bdot(stacked, jnp.concatenate([z.q, z.kb], axis=0), TN) + dkt * z.egl + dkb * z.beta
    dbeta = jnp.sum(dkb * z.k, axis=-1, keepdims=True) + jnp.sum(dvb * z.v, axis=-1, keepdims=True)
    dv = dvb * z.beta
    colsum = _dot3(ddecay, jnp.ones((c, LANES), F32), TN)
    e = jnp.sum(dkt * z.kt, axis=-1, keepdims=True)
    dgc = (jnp.sum(ddecay, axis=-1, keepdims=True) - colsum
           + jnp.sum(dkbg * z.kbg, axis=-1, keepdims=True) + jnp.sum(dqd * z.qd, axis=-1, keepdims=True) - e)
    dglast = jnp.sum(e, axis=0, keepdims=True) + dcd * z.cd
    dgc = dgc + jnp.where(_rows((c, LANES)) == c - 1, dglast, 0.0)
    dg = _rev_cumsum_rows(dgc)[:, 0:1]
    dqn = dq * QSCALE
    dqc = z.rq * (dqn - z.qn * jnp.sum(dqn * z.qn, axis=-1, keepdims=True))
    dkc = z.rk * (dk - z.k * jnp.sum(dk * z.k, axis=-1, keepdims=True))
    return dqc, dkc, dv, dbeta, dg, ds0


GDN_SUB = 1


def _gdn_specs(c, order):
    full = lambda shape: pl.BlockSpec(shape, lambda n: (0,) * len(shape))
    return dict(
        proj=pl.BlockSpec((c, 4 * GDN_W), lambda n: (order(n), 1)),
        prev=pl.BlockSpec((SUBLANES, 4 * GDN_W), lambda n: (_prev8(order(n), c), 1)),
        small=pl.BlockSpec((c, LANES), lambda n: (order(n), 0)),
        convw=full((4, GDN_CONV)), vec=full((1, LANES)),
        state=pl.BlockSpec((GDN_SUB, HEADS, HEAD_DIM, HEAD_DIM), lambda n: (order(n), 0, 0, 0)),
        inv=pl.BlockSpec((GDN_SUB, HEADS, GDN_CHUNK, GDN_CHUNK), lambda n: (order(n), 0, 0, 0)),
        half=pl.BlockSpec((c, GDN_W), lambda n: (order(n), 1)),
        any=pl.BlockSpec(memory_space=pl.ANY),
    )


def _gdn_fwd(proj, psmall, conv_w, al, dt, gain, y_in, comm, *, name):
    t = proj.shape[0]
    c = GDN_CHUNK
    nc, ns = t // c, t // (c * GDN_SUB)
    sp = _gdn_specs(c * GDN_SUB, lambda n: n)

    def body(p_ref, prev_ref, ps_ref, cw_ref, al_ref, dt_ref, gain_ref, yin_ref, *rest):
        comm_in, (y_ref, s_ref, inv_ref), comm_out, (state,), comm_sems = comm.split(rest, n_out=3, n_scratch=1)
        n = pl.program_id(0)

        @pl.when(n == 0)
        def _():
            state[...] = jnp.zeros_like(state)
            comm.start(comm_in, comm_out, comm_sems)

        p8 = jnp.where(n == 0, 0.0, prev_ref[:, :GDN_CONV])
        pre, _ = _conv4(p_ref[:, :GDN_CONV], p8, cw_ref[...])
        act = pre * _sigmoid(pre)
        beta_all, g_all = _gdn_gates(ps_ref[...], al_ref[...], dt_ref[...])
        gd_all, gain = p_ref[:, GDN_CONV:], gain_ref[...]
        swish = gd_all * _sigmoid(gd_all)
        cur = [state[h] for h in range(HEADS)]
        starts, ys = [], []
        for sub in range(GDN_SUB):
            rows = slice(sub * c, (sub + 1) * c)
            starts.append(list(cur))
            pieces = []
            for h in range(HEADS):
                sl = lambda j: slice(j * GDN_W + h * HEAD_DIM, j * GDN_W + (h + 1) * HEAD_DIM)
                z = _gdn_chunk(act[rows, sl(0)], act[rows, sl(1)], act[rows, sl(2)], beta_all[rows, h:h + 1],
                               g_all[rows, HEADS + h:HEADS + h + 1], cur[h])
                cur[h] = z.s1
                inv_ref[sub, h] = z.inv
                rinv = lax.rsqrt(jnp.mean(z.o * z.o, axis=-1, keepdims=True) + EPS)
                pieces.append(z.o * rinv * gain * swish[rows, sl(0)])
            ys.append(jnp.concatenate(pieces, axis=1))
        y_ref[...] = jnp.concatenate(ys, axis=0).astype(BF16)
        for sub in range(GDN_SUB):
            for h in range(HEADS):
                s_ref[sub, h] = starts[sub][h]
        for h in range(HEADS):
            state[h] = cur[h]

        @pl.when(n == ns - 1)
        def _():
            comm.finish(comm_in, comm_out, comm_sems)

    outs = pl.pallas_call(
        body, grid=(ns,),
        in_specs=[sp["proj"], sp["prev"], sp["small"], sp["convw"], sp["vec"], sp["vec"], sp["vec"], sp["any"]]
        + [sp["any"]] * len(comm.arrays),
        out_specs=[sp["half"], sp["state"], sp["inv"]] + [sp["any"]] * len(comm.out_shapes),
        out_shape=[jax.ShapeDtypeStruct((t, 2 * GDN_W), BF16), jax.ShapeDtypeStruct((nc, HEADS, HEAD_DIM, HEAD_DIM), F32),
                   jax.ShapeDtypeStruct((nc, HEADS, c, c), F32)] + comm.out_shapes,
        scratch_shapes=[pltpu.VMEM((HEADS, HEAD_DIM, HEAD_DIM), F32)] + comm.scratch, name=name,
        input_output_aliases={7: 0}, compiler_params=_cp(("arbitrary",), 48),
    )(proj, proj, psmall, conv_w, al, dt, gain, y_in, *comm.arrays)
    return outs[0], (outs[1], outs[2]), outs[3:]


def _gdn_bwd(proj, psmall, conv_w, al, dt, gain, states, dy, dproj_in, comm, *, name):
    t = proj.shape[0]
    c = GDN_CHUNK
    nc, ns = t // c, t // (c * GDN_SUB)
    sp = _gdn_specs(c * GDN_SUB, lambda n: ns - 1 - n)

    def body(p_ref, prev_ref, ps_ref, cw_ref, al_ref, dt_ref, gain_ref, s_ref, inv_ref, dy_ref, dpin_ref, *rest):
        comm_in, outs, comm_out, (dstate, head), comm_sems = comm.split(rest, n_out=6, n_scratch=2)
        dp_ref, dps_ref, dcw_ref, dal_ref, ddt_ref, dgain_ref = outs
        n = pl.program_id(0)
        first_rows = n == ns - 1

        @pl.when(n == 0)
        def _():
            for ref in (dstate, head, dcw_ref, dal_ref, ddt_ref, dgain_ref):
                ref[...] = jnp.zeros_like(ref)
            comm.start(comm_in, comm_out, comm_sems)

        x, cwv = p_ref[:, :GDN_CONV], cw_ref[...]
        p8 = jnp.where(first_rows, 0.0, prev_ref[:, :GDN_CONV])
        pre, shifts = _conv4(x, p8, cwv)
        sg_pre = _sigmoid(pre)
        act = pre * sg_pre
        ps, alv, dtv, gain = ps_ref[...], al_ref[...], dt_ref[...], gain_ref[...]
        beta_all, g_all = _gdn_gates(ps, alv, dtv)
        lane = _cols((c, LANES))
        dgain = jnp.zeros((1, LANES), F32)
        dcur = [dstate[h] for h in range(HEADS)]
        dact_rows, dbeta_rows, dg_rows = [None] * GDN_SUB, [None] * GDN_SUB, [None] * GDN_SUB
        for sub in reversed(range(GDN_SUB)):
            rows = slice(sub * c, (sub + 1) * c)
            dbeta_all = jnp.zeros((c, LANES), F32)
            dg_all = jnp.zeros((c, LANES), F32)
            dact = [None] * (3 * HEADS)
            for h in range(HEADS):
                sl = lambda j: slice(j * GDN_W + h * HEAD_DIM, j * GDN_W + (h + 1) * HEAD_DIM)
                s0 = s_ref[sub, h]
                z = _gdn_chunk(act[rows, sl(0)], act[rows, sl(1)], act[rows, sl(2)], beta_all[rows, h:h + 1],
                               g_all[rows, HEADS + h:HEADS + h + 1], s0, inv=inv_ref[sub, h])
                rinv = lax.rsqrt(jnp.mean(z.o * z.o, axis=-1, keepdims=True) + EPS)
                oh = z.o * rinv
                gd = p_ref[rows, sl(3)]
                sgd = _sigmoid(gd)
                dyh = dy_ref[rows, sl(0)]
                dgain = dgain + jnp.sum(dyh * oh * (gd * sgd), axis=0, keepdims=True)
                dp_ref[rows, sl(3)] = (dyh * oh * gain * sgd * (1.0 + gd * (1.0 - sgd))).astype(BF16)
                dyo = dyh * gain * (gd * sgd)
                do = rinv * (dyo - oh * jnp.mean(dyo * oh, axis=-1, keepdims=True))
                dqc, dkc, dv, dbeta, dg, dcur[h] = _gdn_chunk_bwd(z, s0, do, dcur[h])
                dact[h], dact[HEADS + h], dact[2 * HEADS + h] = dqc, dkc, dv
                dbeta_all = dbeta_all + jnp.where(lane == h, dbeta, 0.0)
                dg_all = dg_all + jnp.where(lane == HEADS + h, dg, 0.0)
            dact_rows[sub], dbeta_rows[sub], dg_rows[sub] = jnp.concatenate(dact, axis=1), dbeta_all, dg_all
        for h in range(HEADS):
            dstate[h] = dcur[h]
        dbeta_all, dg_all = jnp.concatenate(dbeta_rows, axis=0), jnp.concatenate(dg_rows, axis=0)
        dpre = jnp.concatenate(dact_rows, axis=0) * sg_pre * (1.0 + pre * (1.0 - sg_pre))
        dx, dws = _conv4_bwd(dpre, head[...], x, shifts, cwv)
        dp_ref[:, :GDN_CONV] = dx.astype(BF16)
        for k in range(4):
            dcw_ref[k:k + 1, :] += dws[k]
        head[...] = dpre[:SUBLANES]
        dsp = dg_all * (-jnp.exp(alv)) * _sigmoid(ps + dtv)
        dps_ref[...] = (dbeta_all * beta_all * (1.0 - beta_all) + dsp).astype(BF16)
        ddt_ref[...] += jnp.sum(dsp, axis=0, keepdims=True)
        dal_ref[...] += jnp.sum(dg_all * g_all, axis=0, keepdims=True)
        dgain_ref[...] += dgain

        @pl.when(n == ns - 1)
        def _():
            comm.finish(comm_in, comm_out, comm_sems)

    vec_f32 = jax.ShapeDtypeStruct((1, LANES), F32)
    outs = pl.pallas_call(
        body, grid=(ns,),
        in_specs=[sp["proj"], sp["prev"], sp["small"], sp["convw"], sp["vec"], sp["vec"], sp["vec"], sp["state"], sp["inv"],
                  sp["half"], sp["any"]]
        + [sp["any"]] * len(comm.arrays),
        out_specs=[sp["proj"], sp["small"], sp["convw"], sp["vec"], sp["vec"], sp["vec"]] + [sp["any"]] * len(comm.out_shapes),
        out_shape=[jax.ShapeDtypeStruct((t, 8 * GDN_W), BF16), jax.ShapeDtypeStruct((t, LANES), BF16),
                   jax.ShapeDtypeStruct((4, GDN_CONV), F32), vec_f32, vec_f32, vec_f32] + comm.out_shapes,
        scratch_shapes=[pltpu.VMEM((HEADS, HEAD_DIM, HEAD_DIM), F32), pltpu.VMEM((SUBLANES, GDN_CONV), F32)] + comm.scratch,
        name=name, input_output_aliases={10: 0}, compiler_params=_cp(("arbitrary",), 48),
    )(proj, proj, psmall, conv_w, al, dt, gain, *states, dy, dproj_in, *comm.arrays)
    return outs[:6], outs[6:]


def _here():
    x, y, c = lax.axis_index("x"), lax.axis_index("y"), lax.axis_index("c")
    return x, y, c, [(1 - x, y), (x, 1 - y), (1 - x, 1 - y)]


def _rdma(src, dst, send, recv, k, dev):
    return pltpu.make_async_remote_copy(src_ref=src, dst_ref=dst, send_sem=send.at[k], recv_sem=recv.at[k],
                                        device_id=dev, device_id_type=MESH)


def _dma_sems(n):
    return [pltpu.SemaphoreType.DMA((n,)), pltpu.SemaphoreType.DMA((n,)), pltpu.SemaphoreType.DMA((1,))]


COPY_PIECES = 4
COPY_PIECE_ALIGN = 16


def _row_parts(rows):
    n = COPY_PIECES if rows % (COPY_PIECES * COPY_PIECE_ALIGN) == 0 and rows >= 1024 else 1
    return [pl.ds(q * (rows // n), rows // n) for q in range(n)]


class _AllGather:
    def __init__(self, array):
        self.arrays = [array]
        self.out_shapes = [jax.ShapeDtypeStruct((N_DEV,) + array.shape, array.dtype)]
        self.parts = _row_parts(array.shape[0])
        self.scratch = _dma_sems(7 * len(self.parts))

    def start(self, ins, outs, sems):
        (src,), (out,), (send, recv, loc) = ins, outs, sems
        x, y, c, chips = _here()
        me, n = 4 * x + 2 * y + c, len(self.parts)
        pltpu.make_async_copy(src, out.at[me], loc.at[0]).start()
        for q, part in enumerate(self.parts):
            _rdma(src.at[part], out.at[me, part], send, recv, q, (x, y, 1 - c)).start()
            for j, (cx, cy) in enumerate(chips):
                _rdma(src.at[part], out.at[me, part], send, recv, (1 + j) * n + q, (cx, cy, c)).start()

    def finish(self, ins, outs, sems):
        (src,), (out,), (send, recv, loc) = ins, outs, sems
        x, y, c, chips = _here()
        sibling, me, n = (x, y, 1 - c), 4 * x + 2 * y + c, len(self.parts)
        piece = lambda k, q: _rdma(src.at[self.parts[q]], out.at[me, self.parts[q]], send, recv, k * n + q, sibling)
        for j, (cx, cy) in enumerate(chips):
            for q, part in enumerate(self.parts):
                got = out.at[4 * cx + 2 * cy + c, part]
                piece(1 + j, q).wait_recv()
                _rdma(got, got, send, recv, (4 + j) * n + q, sibling).start()
        for k in (0, 4, 5, 6):
            for q in range(n):
                piece(k, q).wait_recv()
        for k in range(7):
            for q in range(n):
                piece(k, q).wait_send()
        pltpu.make_async_copy(src, out.at[me], loc.at[0]).wait()


class _ChipExchange:
    def __init__(self, array):
        self.arrays = [array]
        self.out_shapes = [jax.ShapeDtypeStruct(array.shape, array.dtype)]
        self.parts = _row_parts(array.shape[1])
        self.scratch = _dma_sems(3 * len(self.parts))

    def _copies(self, ins, outs, sems):
        (src,), (out,), (send, recv, loc) = ins, outs, sems
        x, y, c, chips = _here()
        here, n = 2 * x + y, len(self.parts)
        local = pltpu.make_async_copy(src.at[here], out.at[here], loc.at[0])
        return local, [_rdma(src.at[2 * cx + cy, part], out.at[here, part], send, recv, j * n + q, (cx, cy, c))
                       for j, (cx, cy) in enumerate(chips) for q, part in enumerate(self.parts)]

    def start(self, ins, outs, sems):
        local, remote = self._copies(ins, outs, sems)
        local.start()
        for cp in remote:
            cp.start()

    def finish(self, ins, outs, sems):
        local, remote = self._copies(ins, outs, sems)
        for cp in remote:
            cp.wait()
        local.wait()


class _PairSwap:
    def __init__(self, array):
        self.arrays = [array]
        self.out_shapes = [jax.ShapeDtypeStruct(array.shape[1:], array.dtype)]
        self.parts = _row_parts(array.shape[2])
        self.scratch = _dma_sems(4 * len(self.parts))

    def _copies(self, ins, outs, sems):
        (src,), (theirs,), (send, recv, _) = ins, outs, sems
        x, y, c, _ = _here()
        return [_rdma(src.at[1 - c, p, part], theirs.at[p, part], send, recv, p * len(self.parts) + q, (x, y, 1 - c))
                for p in range(4) for q, part in enumerate(self.parts)]

    def start(self, ins, outs, sems):
        for cp in self._copies(ins, outs, sems):
            cp.start()

    def finish(self, ins, outs, sems):
        for cp in self._copies(ins, outs, sems):
            cp.wait()


class _Comm:
    def __init__(self, ops):
        self.ops = ops
        self.arrays = [a for op in ops for a in op.arrays]
        self.out_shapes = [s for op in ops for s in op.out_shapes]
        self.scratch = [s for op in ops for s in op.scratch]

    def split(self, rest, n_out, n_scratch):
        cuts = np.cumsum([0, len(self.arrays), n_out, len(self.out_shapes), n_scratch, len(self.scratch)])
        assert cuts[-1] == len(rest)
        return tuple(rest[a:b] for a, b in zip(cuts[:-1], cuts[1:]))

    def _each(self, method, ins, outs, sems):
        i = o = s = 0
        for op in self.ops:
            ni, no, ns = len(op.arrays), len(op.out_shapes), len(op.scratch)
            getattr(op, method)(ins[i:i + ni], outs[o:o + no], sems[s:s + ns])
            i, o, s = i + ni, o + no, s + ns

    def start(self, ins, outs, sems):
        self._each("start", ins, outs, sems)

    def finish(self, ins, outs, sems):
        self._each("finish", ins, outs, sems)

    def run(self, name):
        def body(*refs):
            ins, _, outs, _, sems = self.split(refs, 0, 0)
            self.start(ins, outs, sems)
            self.finish(ins, outs, sems)

        hbm = pl.BlockSpec(memory_space=pl.ANY)
        return pl.pallas_call(body, in_specs=[hbm] * len(self.arrays), out_specs=[hbm] * len(self.out_shapes),
                              out_shape=self.out_shapes, scratch_shapes=self.scratch, name=name)(*self.arrays)


def _sum_slots(x, *, name, tr=None):
    n, r, l = x.shape
    tr = r if tr is None else tr

    def body(x_ref, o_ref):
        acc = x_ref[0].astype(F32)
        for s in range(1, n):
            acc = acc + x_ref[s].astype(F32)
        o_ref[...] = acc

    return pl.pallas_call(
        body, grid=(r // tr,), in_specs=[pl.BlockSpec((n, tr, l), lambda i: (0, i, 0))],
        out_specs=pl.BlockSpec((tr, l), lambda i: (i, 0)), out_shape=jax.ShapeDtypeStruct((r, l), F32),
        name=name, compiler_params=_cp(("parallel",), 48),
    )(x)


def _pair_add(both, theirs, *, name, tr):
    _, n, r, l = both.shape

    def body(a_ref, b_ref, o_ref):
        mine = jnp.where(lax.axis_index("c") == 0, a_ref[0], a_ref[1])
        o_ref[...] = (mine.astype(F32) + b_ref[...].astype(F32)).astype(BF16)

    spec = pl.BlockSpec((n, tr, l), lambda i: (0, i, 0))
    return pl.pallas_call(body, grid=(r // tr,), in_specs=[pl.BlockSpec((2, n, tr, l), lambda i: (0, 0, i, 0)), spec], out_specs=spec,
                          out_shape=jax.ShapeDtypeStruct(theirs.shape, BF16), name=name,
                          compiler_params=_cp(("parallel",), 48))(both, theirs)


ADAM_TILE_ELEMS = 512 * 1024


def _adam(w, g, m, v, *, name):
    shape = w.shape
    cols = shape[-1]
    rows = math.prod(shape[:-1]) if len(shape) > 1 else 1
    tr = rows
    if rows * cols > ADAM_TILE_ELEMS:
        tr = max(d for d in range(SUBLANES, ADAM_TILE_ELEMS // cols + 1, SUBLANES) if rows % d == 0)
    c1, c2 = 1.0 - ADAM_B1 ** ADAM_STEP, 1.0 - ADAM_B2 ** ADAM_STEP

    def body(w_ref, g_ref, m_ref, v_ref, d_ref, m2_ref, v2_ref):
        gv = g_ref[...]
        m2 = ADAM_B1 * m_ref[...] + (1.0 - ADAM_B1) * gv
        v2 = ADAM_B2 * v_ref[...] + (1.0 - ADAM_B2) * (gv * gv)
        d_ref[...] = -ADAM_LR * ((m2 / c1) / (jnp.sqrt(v2 / c2) + ADAM_EPS) + ADAM_WD * w_ref[...])
        m2_ref[...] = m2
        v2_ref[...] = v2

    spec = pl.BlockSpec((tr, cols), lambda i: (i, 0))
    outs = pl.pallas_call(
        body, grid=(rows // tr,), in_specs=[spec] * 4, out_specs=[spec] * 3,
        out_shape=[jax.ShapeDtypeStruct((rows, cols), F32)] * 3, name=name, compiler_params=_cp(("parallel",), 48),
    )(*(a.reshape(rows, cols) for a in (w, g, m, v)))
    return tuple(o.reshape(shape) for o in outs)


WEIGHTS = ['norm_mix', 'norm_ffn', 'ret_gdn_w_in', 'gdn_conv_w', 'gdn_a_log', 'gdn_dt_bias', 'gdn_out_gain', 'ret_gdn_w_out',
           'lru_w_in', 'lru_conv_w', 'lru_conv_b', 'lru_w_a', 'lru_b_a', 'lru_w_x', 'lru_b_x', 'lru_lambda', 'lru_w_out',
           'ffn_w_up', 'ffn_conv_w', 'ffn_conv_b', 'ffn_w_down', 'norm_final']
SMALL = {'gdn_conv_w': ((1, 4, 192), 2), 'lru_conv_w': ((1, 4, 128), 2), 'lru_conv_b': ((1, 128), 1), 'lru_b_a': ((1, 128), 1),
         'lru_b_x': ((1, 128), 1), 'lru_lambda': ((1, 128), 1), 'ffn_conv_w': ((2, 3, 704), 2)}
EARLY = {'lru_conv_w': (1, 4, 1024), 'lru_conv_b': (1, 1024), 'lru_b_a': (1, 1024), 'lru_b_x': (1, 1024), 'lru_lambda': (1, 1024),
         'ffn_conv_w': (2, 3, 5632), 'norm_ffn': (2, 1024), 'norm_mix1': (1, 1024), 'lru_w_a': (1, 8, 128, 128),
         'lru_w_x': (1, 8, 128, 128), 'ffn_conv_b': (2, 5632), 'norm_final': (1024,)}
LATE = {'gdn_conv_w': (1, 4, 1536), 'norm_mix0': (1, 1024), 'gdn_a_log': (1, 4), 'gdn_dt_bias': (1, 4), 'gdn_out_gain': (1, 128),
        'loss': (1, 1)}


def _rows_of(n_elems):
    return -(-n_elems // LANES)


def _to_rows(a, lead=()):
    flat = a.reshape(lead + (-1,))
    pad = _rows_of(flat.shape[-1]) * LANES - flat.shape[-1]
    if pad:
        flat = jnp.pad(flat, [(0, 0)] * len(lead) + [(0, pad)])
    return flat.reshape(lead + (-1, LANES))


def _pack(pieces, total_rows, lead=()):
    buf = jnp.concatenate(pieces, axis=len(lead))
    pad = total_rows - buf.shape[len(lead)]
    return jnp.pad(buf, [(0, 0)] * len(lead) + [(0, pad), (0, 0)]) if pad else buf


def _unpack(buf, shapes, lead=()):
    out, off = [], 0
    for shape in shapes:
        n = math.prod(shape)
        rows = _rows_of(n)
        piece = lax.slice_in_dim(buf, off, off + rows, axis=len(lead)).reshape(lead + (rows * LANES,))
        out.append(lax.slice_in_dim(piece, 0, n, axis=len(lead)).reshape(lead + shape))
        off += rows
    return out


def _join_blocks(g, axis):
    m = jnp.moveaxis(g, 0, axis)
    return m.reshape(m.shape[:axis] + (N_DEV * m.shape[axis + 1],) + m.shape[axis + 2:])


def _split_blocks(full, axis):
    s = full.shape
    return jnp.moveaxis(full.reshape(s[:axis] + (N_DEV, s[axis] // N_DEV) + s[axis + 1:]), axis, 0)


def _small_rows(shapes):
    total = sum(_rows_of(math.prod(s)) for s in shapes)
    return -(-total // SUBLANES) * SUBLANES


FFN_TM = 1024
FFN_TN = 512


def _ffn_forward(h, gain, w_up, cw, cb, w_down, tag, comm):
    t, d = h.shape
    tm, tn, blk, nb = min(FFN_TM, t), FFN_TN, FFN_BLK, FFN_NB
    hn = _norm_fwd(h, gain, name=f"ffn{tag}_norm")
    up = _mmx(hn, w_up, dims=NN, grid=(t // tm, 2 * nb, 1), name=f"ffn{tag}_up", tile=(tm, blk),
              a_spec=pl.BlockSpec((tm, d), lambda i, j, k: (i, 0)),
              b_spec=pl.BlockSpec((None, d, blk), lambda i, j, k: (j, 0, 0)),
              o_spec=pl.BlockSpec((None, None, tm, blk), lambda i, j, k: (j // nb, j % nb, i, 0)),
              out_shape=jax.ShapeDtypeStruct((2, nb, t, blk), F32))
    act, comm_out = _ffn_act_fwd(up, cw, cb, comm, name=f"ffn{tag}_act")
    out = _mmx(act, w_down.reshape(nb, blk, d), dims=NN, grid=(t // tm, d // tn, nb // 2), name=f"ffn{tag}_down", tile=(tm, tn),
               res=h, pairs=True,
               a_spec=pl.BlockSpec((2, tm, blk), lambda i, j, k: (k, i, 0)),
               b_spec=pl.BlockSpec((2, blk, tn), lambda i, j, k: (k, 0, j)),
               o_spec=pl.BlockSpec((tm, tn), lambda i, j, k: (i, j)),
               out_shape=jax.ShapeDtypeStruct((t, d), F32))
    return out, (hn, up, act), comm_out


def _ffn_backward(dh, h, gain, saved, w_up, cw, cb, w_down, tag):
    hn, up, act = saved
    t, d = h.shape
    tm, tn, blk, nb = min(FFN_TM, t), FFN_TN, FFN_BLK, FFN_NB
    tk = min(FFN_TM, t)
    da = _mmx(dh, w_down, dims=NT, grid=(t // tm, nb, 1), name=f"ffn{tag}_d_act", tile=(tm, blk),
              a_spec=pl.BlockSpec((tm, d), lambda i, j, k: (i, 0)),
              b_spec=pl.BlockSpec((blk, d), lambda i, j, k: (j, 0)),
              o_spec=pl.BlockSpec((None, tm, blk), lambda i, j, k: (j, i, 0)),
              out_shape=jax.ShapeDtypeStruct((nb, t, blk), F32))
    dwd = _mmx(act, dh, dims=TN, grid=(nb, d // tn, t // tk), name=f"ffn{tag}_d_wdown", tile=(blk, tn), split_rows=blk // 2,
               a_spec=pl.BlockSpec((None, tk, blk), lambda i, j, k: (i, k, 0)),
               b_spec=pl.BlockSpec((tk, tn), lambda i, j, k: (k, j)),
               o_spec=pl.BlockSpec((2, None, blk // 2, tn), lambda i, j, k: (0, i, 0, j)),
               out_shape=jax.ShapeDtypeStruct((2, nb, blk // 2, d), BF16))
    dup, dcw, dcb = _ffn_act_bwd(up, da, cw, cb, name=f"ffn{tag}_act_bwd")
    half = nb // 2
    dhn = _mmx(dup, w_up, dims=NT, grid=(t // tm, d // tn, nb), name=f"ffn{tag}_d_hn", tile=(tm, tn), pairs=True,
               a_spec=pl.BlockSpec((None, 2, tm, blk), lambda i, j, k: (k // half, k % half, i, 0)),
               b_spec=pl.BlockSpec((2, tn, blk), lambda i, j, k: (k, j, 0)),
               o_spec=pl.BlockSpec((tm, tn), lambda i, j, k: (i, j)),
               out_shape=jax.ShapeDtypeStruct((t, d), F32))
    dwu = _mmx(hn, dup, dims=TN, grid=(1, 2 * nb, t // tk), name=f"ffn{tag}_d_wup", tile=(d, blk),
               a_spec=pl.BlockSpec((tk, d), lambda i, j, k: (k, 0)),
               b_spec=pl.BlockSpec((None, None, tk, blk), lambda i, j, k: (j // nb, j % nb, k, 0)),
               o_spec=pl.BlockSpec((None, None, d, blk), lambda i, j, k: (j % 2, j // 2, 0, 0)),
               out_shape=jax.ShapeDtypeStruct((2, N_DEV // 2, d, blk), BF16))
    dh_in, dgain = _norm_bwd(h, gain, dhn, dh, name=f"ffn{tag}_norm_bwd")
    conv_w = dcw.transpose(2, 0, 1, 3).reshape(3, 2 * nb * blk)
    return dh_in, dict(w_up=dwu, w_down=dwd, conv_w=conv_w, conv_b=dcb.reshape(1, 2 * nb * blk), norm=dgain)


def kernel(x, norm_mix, norm_ffn, ret_gdn_w_in, gdn_conv_w, gdn_a_log, gdn_dt_bias, gdn_out_gain, ret_gdn_w_out, lru_w_in, lru_conv_w, lru_conv_b, lru_w_a, lru_b_a, lru_w_x, lru_b_x, lru_lambda, lru_w_out, ffn_w_up, ffn_conv_w, ffn_conv_b, ffn_w_down, norm_final, loss_target, m_norm_mix, m_norm_ffn, m_ret_gdn_w_in, m_gdn_conv_w, m_gdn_a_log, m_gdn_dt_bias, m_gdn_out_gain, m_ret_gdn_w_out, m_lru_w_in, m_lru_conv_w, m_lru_conv_b, m_lru_w_a, m_lru_b_a, m_lru_w_x, m_lru_b_x, m_lru_lambda, m_lru_w_out, m_ffn_w_up, m_ffn_conv_w, m_ffn_conv_b, m_ffn_w_down, m_norm_final, v_norm_mix, v_norm_ffn, v_ret_gdn_w_in, v_gdn_conv_w, v_gdn_a_log, v_gdn_dt_bias, v_gdn_out_gain, v_ret_gdn_w_out, v_lru_w_in, v_lru_conv_w, v_lru_conv_b, v_lru_w_a, v_lru_b_a, v_lru_w_x, v_lru_b_x, v_lru_lambda, v_lru_w_out, v_ffn_w_up, v_ffn_conv_w, v_ffn_conv_b, v_ffn_w_down, v_norm_final):
    given = dict(locals())
    w = {n: given[n] for n in WEIGHTS}
    me = 4 * lax.axis_index("x") + 2 * lax.axis_index("y") + lax.axis_index("c")
    t = x.shape[1]

    first_shards = {'w_in0': ret_gdn_w_in[0]}
    rest_shards = {'w_out0': ret_gdn_w_out[0], 'lru_in': lru_w_in[0], 'lru_out': lru_w_out[0], 'up0': ffn_w_up[0], 'up1': ffn_w_up[1],
                   'down0': ffn_w_down[0], 'down1': ffn_w_down[1]}
    small_shapes = [s for s, _ in SMALL.values()]
    small_buf = _pack([_to_rows(w[n]) for n in SMALL], _small_rows(small_shapes))
    *g_first, g_small = _Comm([_AllGather(a.astype(BF16)) for a in first_shards.values()] + [_AllGather(small_buf)]).run("gather_first")
    hosted = {'retention_fwd': ['w_out0', 'lru_out'], 'deltanet_fwd': ['lru_in', 'up0', 'down0', 'up1'], 'ffn0_act': ['down1'], 'rglru_fwd': []}
    gather_in = {host: _Comm([_AllGather(rest_shards[n].astype(BF16)) for n in names]) for host, names in hosted.items()}
    got = dict(zip(first_shards, g_first))
    small_blocks = dict(zip(SMALL, _unpack(g_small, small_shapes, lead=(N_DEV,))))
    full = {n: _join_blocks(small_blocks[n], SMALL[n][1]) for n in SMALL if n != 'ffn_conv_w'}

    w_main, w_narrow = _join_w_in(got['w_in0'], name="join_w_in")
    fcw = [small_blocks['ffn_conv_w'][:, l].reshape(2, FFN_NB, 3, FFN_BLK) for l in range(2)]
    fcb = [ffn_conv_b[l].reshape(2, FFN_NB, 1, FFN_BLK) for l in range(2)]
    gdn_cw = full['gdn_conv_w'][0]
    al_pad = jnp.pad(gdn_a_log, ((0, 0), (HEADS, LANES - 2 * HEADS)))
    dt_pad = jnp.pad(gdn_dt_bias, ((0, 0), (HEADS, LANES - 2 * HEADS)))
    lru_cw, lru_cb = full['lru_conv_w'][0], full['lru_conv_b']
    lru_ba, lru_bx, lru_lam = full['lru_b_a'], full['lru_b_x'], full['lru_lambda']
    wa, wx = lru_w_a[0], lru_w_x[0]

    h0, target = x[0], loss_target[0]
    hn0 = _norm_fwd(h0, norm_mix[0:1], name="mix0_norm")
    proj = _mm(hn0, w_main, name="mix0_in")
    pnarrow = _mm(hn0, w_narrow, name="mix0_in_narrow")
    tables = _ret_tables(t)
    y0, ret_states, g = _ret_fwd(proj, tables, gather_in['retention_fwd'], name="retention_fwd")
    got.update(zip(hosted['retention_fwd'], g))
    y0, gdn_states, g = _gdn_fwd(proj, pnarrow, gdn_cw, al_pad, dt_pad, gdn_out_gain, y0, gather_in['deltanet_fwd'], name="deltanet_fwd")
    got.update(zip(hosted['deltanet_fwd'], g))
    lru_in = _join_blocks(got['lru_in'], 1)
    lru_in_g, lru_in_x = lru_in[:, :D_MODEL], lru_in[:, D_MODEL:]
    lru_out = got['lru_out'].reshape(D_MODEL, D_MODEL)
    w_out0 = got['w_out0'].reshape(D_MODEL, D_MODEL)
    h1 = _mm(y0, w_out0, res=h0, name="mix0_out")
    h2, ffn0_saved, g = _ffn_forward(h1, norm_ffn[0:1], got['up0'], fcw[0], fcb[0], got['down0'].reshape(D_FF, D_MODEL), 0,
                                     gather_in['ffn0_act'])
    got.update(zip(hosted['ffn0_act'], g))
    hn1 = _norm_fwd(h2, norm_mix[1:2], name="mix1_norm")
    gate = _mm(hn1, lru_in_g, name="mix1_in_gate")
    xpre = _mm(hn1, lru_in_x, name="mix1_in_x")
    y1, hs, g = _lru_fwd(gate, xpre, lru_cw, lru_cb, wa, lru_ba, wx, lru_bx, lru_lam, gather_in['rglru_fwd'], name="rglru_fwd")
    got.update(zip(hosted['rglru_fwd'], g))
    h3 = _mm(y1, lru_out, res=h2, name="mix1_out")
    w_up = [got['up0'], got['up1']]
    down = [got['down0'].reshape(D_FF, D_MODEL), got['down1'].reshape(D_FF, D_MODEL)]
    h4, ffn1_saved, _ = _ffn_forward(h3, norm_ffn[1:2], w_up[1], fcw[1], fcb[1], down[1], 1, _Comm([]))
    dh4, d_norm_final, loss_part = _final_loss(h4, norm_final[None, :], target, name="final_norm_loss")

    dh3, gf1 = _ffn_backward(dh4, h3, norm_ffn[1:2], ffn1_saved, w_up[1], fcw[1], fcb[1], down[1], 1)
    dy1 = _mm(dh3, lru_out, tb=True, name="mix1_d_y")
    d_lru_out = _mm(y1, dh3, ta=True, out_dtype=BF16, name="mix1_d_wout")
    dgate, dxpre, d_lcw, d_lcb, d_wa, d_ba, d_wx, d_bx, d_lam = _lru_bwd(
        gate, xpre, hs, dy1, lru_cw, lru_cb, wa, lru_ba, wx, lru_bx, lru_lam, name="rglru_bwd")
    dhn1 = _mm(dgate, lru_in_g, tb=True, name="mix1_d_hn_gate")
    dhn1 = _mm(dxpre, lru_in_x, tb=True, res=dhn1, name="mix1_d_hn_x")
    d_lru_in = jnp.concatenate([_mm(hn1, dgate, ta=True, out_dtype=BF16, name="mix1_d_win_gate"),
                                _mm(hn1, dxpre, ta=True, out_dtype=BF16, name="mix1_d_win_x")], axis=1)
    dh2, d_mix1 = _norm_bwd(h2, norm_mix[1:2], dhn1, dh3, name="mix1_norm_bwd")
    dh1, gf0 = _ffn_backward(dh2, h1, norm_ffn[0:1], ffn0_saved, w_up[0], fcw[0], fcb[0], down[0], 0)
    dy0 = _mm(dh1, w_out0, tb=True, name="mix0_d_y")
    d_w_out0 = _mm(y0, dh1, ta=True, out_dtype=BF16, name="mix0_d_wout")

    def by_core_chip(full_grad, axis):
        blocks = _split_blocks(full_grad, axis)
        return blocks.reshape((4, 2) + blocks.shape[1:]).transpose(1, 0, 2, 3)

    def pair_add(blocks, theirs):
        return {k: _pair_add(b, o, name=f"pair_add_{k}", tr=row_tile(b.shape[2])) for (k, b), o in zip(blocks.items(), theirs)}

    row_tile = lambda rows: rows if rows <= 512 else 256

    rest_blocks = {'w_out0': by_core_chip(d_w_out0, 0), 'lru_in': by_core_chip(d_lru_in, 1), 'lru_out': by_core_chip(d_lru_out, 0),
                   'up0': gf0['w_up'], 'up1': gf1['w_up'], 'down0': gf0['w_down'], 'down1': gf1['w_down']}
    early = {'lru_conv_w': d_lcw[None], 'lru_conv_b': d_lcb, 'lru_b_a': d_ba, 'lru_b_x': d_bx, 'lru_lambda': d_lam,
             'ffn_conv_w': jnp.stack([gf0['conv_w'], gf1['conv_w']]), 'norm_ffn': jnp.concatenate([gf0['norm'], gf1['norm']], axis=0),
             'norm_mix1': d_mix1, 'lru_w_a': d_wa[None], 'lru_w_x': d_wx[None],
             'ffn_conv_b': jnp.concatenate([gf0['conv_b'], gf1['conv_b']], axis=0), 'norm_final': d_norm_final[0]}
    early_buf = _pack([_to_rows(early[n]) for n in EARLY], _small_rows(list(EARLY.values())))
    dproj, (*theirs, got_early) = _ret_bwd(proj, tables, ret_states, dy0,
                                           _Comm([_PairSwap(b) for b in rest_blocks.values()] + [_AllGather(early_buf)]), name="retention_bwd")
    z_rest = pair_add(rest_blocks, theirs)
    (dproj, dnarrow, d_gcw, d_alog, d_dtb, d_gain), w_rest = _gdn_bwd(
        proj, pnarrow, gdn_cw, al_pad, dt_pad, gdn_out_gain, gdn_states, dy0, dproj,
        _Comm([_ChipExchange(z) for z in z_rest.values()]), name="deltanet_bwd")
    dhn0 = _mm(dproj, w_main, tb=True, name="mix0_d_hn")
    dhn0 = _mm(dnarrow, w_narrow, tb=True, res=dhn0, name="mix0_d_hn_narrow")
    d_w_main = _mm(hn0, dproj, ta=True, out_dtype=BF16, name="mix0_d_win")
    d_w_narrow = _mm(hn0, dnarrow, ta=True, out_dtype=BF16, name="mix0_d_win_narrow")
    dx, d_mix0 = _norm_bwd(h0, norm_mix[0:1], dhn0, dh1, name="mix0_norm_bwd")

    first_blocks = {'w_in0': _split_w_in(d_w_main, d_w_narrow, name="split_d_w_in")}
    z_first = pair_add(first_blocks, _Comm([_PairSwap(b) for b in first_blocks.values()]).run("pair_swap_first"))
    late = {'gdn_conv_w': d_gcw[None], 'norm_mix0': d_mix0, 'gdn_a_log': d_alog[:, HEADS:2 * HEADS],
            'gdn_dt_bias': d_dtb[:, HEADS:2 * HEADS], 'gdn_out_gain': d_gain, 'loss': loss_part}
    late_buf = _pack([_to_rows(late[n]) for n in LATE], _small_rows(list(LATE.values())))
    *w_first, got_late = _Comm([_ChipExchange(z) for z in z_first.values()] + [_AllGather(late_buf)]).run("exchange_first")

    summed = {k: _sum_slots(blocks, name=f"sum_blocks_{k}", tr=row_tile(blocks.shape[1]))
              for k, blocks in list(zip(z_rest, w_rest)) + list(zip(z_first, w_first))}
    grads = {'ret_gdn_w_in': summed['w_in0'][None], 'ret_gdn_w_out': summed['w_out0'][None], 'lru_w_in': summed['lru_in'][None],
             'lru_w_out': summed['lru_out'][None], 'ffn_w_up': jnp.stack([summed['up0'], summed['up1']]),
             'ffn_w_down': jnp.stack([summed['down0'], summed['down1']])}
    partial = dict(zip(EARLY, _unpack(_sum_slots(got_early, name="sum_partials_early"), list(EARLY.values()))))
    partial.update(zip(LATE, _unpack(_sum_slots(got_late, name="sum_partials_late"), list(LATE.values()))))
    partial['norm_mix'] = jnp.concatenate([partial.pop('norm_mix0'), partial.pop('norm_mix1')], axis=0)
    loss = partial.pop('loss')[0, 0]
    for n, g_full in partial.items():
        if n in SMALL:
            shard, axis = SMALL[n]
            g_full = lax.dynamic_slice_in_dim(g_full, me * shard[axis], shard[axis], axis=axis)
        grads[n] = g_full

    delta, new_m, new_v = {}, {}, {}
    for n in WEIGHTS:
        delta[n], new_m[n], new_v[n] = _adam(w[n], grads[n], given["m_" + n], given["v_" + n], name=f"adamw_{n}")
    return (loss, dx[None], *[grads[n] for n in WEIGHTS], *[delta[n] for n in WEIGHTS],
            *[new_m[n] for n in WEIGHTS], *[new_v[n] for n in WEIGHTS])
```

```python
import math

import numpy as np
import jax
import jax.numpy as jnp
from jax import lax
from jax.experimental import pallas as pl
from jax.experimental.pallas import tpu as pltpu

F32 = jnp.float32
BF16 = jnp.bfloat16
MESH = pl.DeviceIdType.MESH

N_DEV = 8
LANES = 128
SUBLANES = 8
EPS = 1e-6
D_MODEL = 1024
HEADS = 4
HEAD_DIM = 128
RET_CHUNK = 128
GDN_CHUNK = 64
ROPE_BASE = 10000.0
LRU_C = 8.0
D_FF = 2816
MAIN_IN = 4096
SMALL_IN = 8
QSCALE = HEAD_DIM ** -0.5

ADAM_LR, ADAM_B1, ADAM_B2, ADAM_EPS, ADAM_WD, ADAM_STEP = 0.001, 0.9, 0.999, 1e-08, 0.01, 10


def _cp(sem=None, vmem_mb=None):
    kw = {}
    if sem is not None:
        kw["dimension_semantics"] = sem
    if vmem_mb is not None:
        kw["vmem_limit_bytes"] = vmem_mb << 20
    return pltpu.CompilerParams(**kw)


def _rows(shape):
    return lax.broadcasted_iota(jnp.int32, shape, 0)


def _cols(shape):
    return lax.broadcasted_iota(jnp.int32, shape, 1)


def _shift_down(cur, prev8, s):
    if s == 0:
        return cur
    rc = pltpu.roll(cur, s, 0)
    rp = pltpu.roll(prev8, s, 0)
    top = jnp.where(_rows(prev8.shape) < s, rp, rc[:SUBLANES])
    return jnp.concatenate([top, rc[SUBLANES:]], axis=0)


def _shift_up(cur, next8, s):
    if s == 0:
        return cur
    tt = cur.shape[0]
    rc = pltpu.roll(cur, tt - s, 0)
    rn = pltpu.roll(next8, SUBLANES - s, 0)
    bot = jnp.where(_rows(next8.shape) >= SUBLANES - s, rn, rc[tt - SUBLANES:])
    return jnp.concatenate([rc[:tt - SUBLANES], bot], axis=0)


def _down_fill(x, d, fill):
    return jnp.where(_rows(x.shape) < d, fill, pltpu.roll(x, d, 0))


def _up_fill(x, d, fill):
    tt = x.shape[0]
    return jnp.where(_rows(x.shape) >= tt - d, fill, pltpu.roll(x, tt - d, 0))


def _sigmoid(x):
    return 1.0 / (1.0 + jnp.exp(-x))


def _softplus(x):
    return jnp.maximum(x, 0.0) + jnp.log(1.0 + jnp.exp(-jnp.abs(x)))


def _dot(a, b, dims=(((1,), (0,)), ((), ())), precision=None):
    return lax.dot_general(a, b, dims, preferred_element_type=F32, precision=precision)


NN = (((1,), (0,)), ((), ()))
NT = (((1,), (1,)), ((), ()))
TN = (((0,), (0,)), ((), ()))


def _bdot(a, b, dims=NN):
    return _dot(a.astype(BF16), b.astype(BF16), dims)


def _split(a):
    hi = a.astype(BF16)
    return hi, (a - hi.astype(F32)).astype(BF16)


def _dot3(a, b, dims=NN):
    ah, al = _split(a)
    bh, bl = _split(b)
    return _dot(ah, bh, dims) + (_dot(ah, bl, dims) + _dot(al, bh, dims))


def _tile(dim, target):
    if dim <= target:
        return dim
    best = None
    for c in range(LANES, target + 1, LANES):
        if dim % c == 0:
            best = c
    assert best is not None, (dim, target)
    return best


def _mm(a, b, *, name, ta=False, tb=False, out_dtype=F32, res=None, tm=2048, tn=512, tk=1024):
    m, k = (a.shape[1], a.shape[0]) if ta else a.shape
    n = b.shape[0] if tb else b.shape[1]
    tn, tk = _tile(n, tn), _tile(k, tk)
    tm = _tile(m, tm if max(tn, tk) <= 1024 else tm // 2)
    nk = k // tk
    dims = (((0 if ta else 1,), (1 if tb else 0,)), ((), ()))

    def body(*refs):
        a_ref, b_ref = refs[:2]
        r_ref = refs[2] if res is not None else None
        o_ref = refs[3] if res is not None else refs[2]
        acc = refs[-1]
        kk = pl.program_id(2)
        part = _bdot(a_ref[...], b_ref[...], dims)

        def finish(r):
            if res is not None:
                r = r + r_ref[...]
            o_ref[...] = r.astype(out_dtype)

        if nk == 1:
            finish(part)
            return

        @pl.when(kk == 0)
        def _():
            acc[...] = part

        @pl.when(jnp.logical_and(kk > 0, kk < nk - 1))
        def _():
            acc[...] += part

        @pl.when(kk == nk - 1)
        def _():
            finish(acc[...] + part)

    a_spec = pl.BlockSpec((tk, tm), lambda i, j, kk: (kk, i)) if ta else pl.BlockSpec((tm, tk), lambda i, j, kk: (i, kk))
    b_spec = pl.BlockSpec((tn, tk), lambda i, j, kk: (j, kk)) if tb else pl.BlockSpec((tk, tn), lambda i, j, kk: (kk, j))
    o_spec = pl.BlockSpec((tm, tn), lambda i, j, kk: (i, j))
    in_specs = [a_spec, b_spec] + ([o_spec] if res is not None else [])
    args = (a, b) + ((res,) if res is not None else ())
    return pl.pallas_call(
        body, grid=(m // tm, n // tn, nk), in_specs=in_specs, out_specs=o_spec,
        out_shape=jax.ShapeDtypeStruct((m, n), out_dtype),
        scratch_shapes=[pltpu.VMEM((tm, tn), F32)] if nk > 1 else [], name=name,
        compiler_params=_cp(("parallel", "parallel", "arbitrary"), 56),
    )(*args)


def _mmx(a, b, *, dims, grid, a_spec, b_spec, o_spec, out_shape, tile, name, res=None, split_rows=None, pairs=False):
    nk = grid[-1]

    def body(*refs):
        a_ref, b_ref = refs[:2]
        r_ref = refs[2] if res is not None else None
        o_ref = refs[3] if res is not None else refs[2]
        acc = refs[-1]
        if pairs:
            part = _bdot(a_ref[0], b_ref[0], dims) + _bdot(a_ref[1], b_ref[1], dims)
        else:
            part = _bdot(a_ref[...], b_ref[...], dims)

        def finish(r):
            if res is not None:
                r = r + r_ref[...]
            if split_rows is None:
                o_ref[...] = r.astype(o_ref.dtype)
            else:
                o_ref[0] = r[:split_rows].astype(o_ref.dtype)
                o_ref[1] = r[split_rows:].astype(o_ref.dtype)

        if nk == 1:
            finish(part)
            return
        kk = pl.program_id(len(grid) - 1)

        @pl.when(kk == 0)
        def _():
            acc[...] = part

        @pl.when(jnp.logical_and(kk > 0, kk < nk - 1))
        def _():
            acc[...] += part

        @pl.when(kk == nk - 1)
        def _():
            finish(acc[...] + part)

    args = (a, b) + ((res,) if res is not None else ())
    return pl.pallas_call(
        body, grid=grid, in_specs=[a_spec, b_spec] + ([o_spec] if res is not None else []), out_specs=o_spec,
        out_shape=out_shape, scratch_shapes=[pltpu.VMEM(tile, F32)] if nk > 1 else [], name=name,
        compiler_params=_cp(("parallel",) * (len(grid) - 1) + ("arbitrary",), 56),
    )(*args)


W_IN_BLK = 513
W_IN_TR = 256


def _join_w_in(blocks, *, name):
    _, d, _ = blocks.shape
    tr = W_IN_TR

    def body(x_ref, main_ref, narrow_ref):
        for m in range(MAIN_IN // LANES):
            lo = LANES * m
            dev, off = divmod(lo, W_IN_BLK)
            if off + LANES <= W_IN_BLK:
                main_ref[:, lo:lo + LANES] = x_ref[dev, :, off:off + LANES]
            else:
                main_ref[:, lo:lo + LANES] = jnp.concatenate(
                    [x_ref[dev, :, off:W_IN_BLK], x_ref[dev + 1, :, 0:LANES - (W_IN_BLK - off)]], axis=1)
        tail = x_ref[N_DEV - 1, :, W_IN_BLK - SMALL_IN:W_IN_BLK]
        narrow_ref[...] = jnp.concatenate([tail, jnp.zeros((tr, LANES - SMALL_IN), tail.dtype)], axis=1)

    return pl.pallas_call(
        body, grid=(d // tr,), in_specs=[pl.BlockSpec((N_DEV, tr, W_IN_BLK), lambda i: (0, i, 0))],
        out_specs=[pl.BlockSpec((tr, MAIN_IN), lambda i: (i, 0)), pl.BlockSpec((tr, LANES), lambda i: (i, 0))],
        out_shape=[jax.ShapeDtypeStruct((d, MAIN_IN), blocks.dtype), jax.ShapeDtypeStruct((d, LANES), blocks.dtype)],
        name=name, compiler_params=_cp(("parallel",), 48),
    )(blocks)


def _split_w_in(main, narrow, *, name):
    d = main.shape[0]
    tr = W_IN_TR

    def body(m_ref, n_ref, o_ref):
        for dev in range(N_DEV):
            lo = W_IN_BLK * dev
            if dev < N_DEV - 1:
                piece = m_ref[:, lo:lo + W_IN_BLK]
            else:
                piece = jnp.concatenate([m_ref[:, lo:MAIN_IN], n_ref[:, 0:SMALL_IN]], axis=1)
            o_ref[dev % 2, dev // 2] = piece

    return pl.pallas_call(
        body, grid=(d // tr,),
        in_specs=[pl.BlockSpec((tr, MAIN_IN), lambda i: (i, 0)), pl.BlockSpec((tr, LANES), lambda i: (i, 0))],
        out_specs=pl.BlockSpec((2, N_DEV // 2, tr, W_IN_BLK), lambda i: (0, 0, i, 0)),
        out_shape=jax.ShapeDtypeStruct((2, N_DEV // 2, d, W_IN_BLK), main.dtype),
        name=name, compiler_params=_cp(("parallel",), 48),
    )(main, narrow)


def _norm_fwd(h, gain, *, name, tt=1024):
    t, d = h.shape
    tt = min(tt, t)

    def body(h_ref, g_ref, o_ref):
        x = h_ref[...]
        r = lax.rsqrt(jnp.mean(x * x, axis=-1, keepdims=True) + EPS)
        o_ref[...] = (x * r * g_ref[...]).astype(BF16)

    row = pl.BlockSpec((tt, d), lambda i: (i, 0))
    return pl.pallas_call(
        body, grid=(t // tt,), in_specs=[row, pl.BlockSpec((1, d), lambda i: (0, 0))], out_specs=row,
        out_shape=jax.ShapeDtypeStruct((t, d), BF16), name=name, compiler_params=_cp(("parallel",), 48),
    )(h, gain)


def _norm_bwd(h, gain, dhn, dres, *, name, tt=512):
    t, d = h.shape
    tt = min(tt, t)

    def body(h_ref, g_ref, dy_ref, dr_ref, dx_ref, dg_ref):
        x, dy = h_ref[...], dy_ref[...]
        r = lax.rsqrt(jnp.mean(x * x, axis=-1, keepdims=True) + EPS)
        xh = x * r

        @pl.when(pl.program_id(0) == 0)
        def _():
            dg_ref[...] = jnp.zeros_like(dg_ref)

        dg_ref[...] += jnp.sum(dy * xh, axis=0, keepdims=True)
        dxh = dy * g_ref[...]
        dx_ref[...] = dr_ref[...] + r * (dxh - xh * jnp.mean(dxh * xh, axis=-1, keepdims=True))

    row = pl.BlockSpec((tt, d), lambda i: (i, 0))
    vec = pl.BlockSpec((1, d), lambda i: (0, 0))
    return pl.pallas_call(
        body, grid=(t // tt,), in_specs=[row, vec, row, row], out_specs=[row, vec],
        out_shape=[jax.ShapeDtypeStruct((t, d), F32), jax.ShapeDtypeStruct((1, d), F32)],
        name=name, compiler_params=_cp(("arbitrary",), 48),
    )(h, gain, dhn, dres)


def _final_loss(h, gain, target, *, name, tt=512):
    t, d = h.shape
    tt = min(tt, t)

    def body(h_ref, g_ref, tg_ref, dx_ref, dg_ref, loss_ref):
        x = h_ref[...]
        r = lax.rsqrt(jnp.mean(x * x, axis=-1, keepdims=True) + EPS)
        xh = x * r
        err = xh * g_ref[...] - tg_ref[...]

        @pl.when(pl.program_id(0) == 0)
        def _():
            dg_ref[...] = jnp.zeros_like(dg_ref)
            loss_ref[...] = jnp.zeros_like(loss_ref)

        loss_ref[...] += 0.5 * jnp.sum(jnp.mean(err * err, axis=-1, keepdims=True), axis=0, keepdims=True)
        dy = err * (1.0 / d)
        dg_ref[...] += jnp.sum(dy * xh, axis=0, keepdims=True)
        dxh = dy * g_ref[...]
        dx_ref[...] = r * (dxh - xh * jnp.mean(dxh * xh, axis=-1, keepdims=True))

    row = pl.BlockSpec((tt, d), lambda i: (i, 0))
    vec = pl.BlockSpec((1, d), lambda i: (0, 0))
    return pl.pallas_call(
        body, grid=(t // tt,), in_specs=[row, vec, row],
        out_specs=[row, vec, pl.BlockSpec((1, 1), lambda i: (0, 0))],
        out_shape=[jax.ShapeDtypeStruct((t, d), F32), jax.ShapeDtypeStruct((1, d), F32), jax.ShapeDtypeStruct((1, 1), F32)],
        name=name, compiler_params=_cp(("arbitrary",), 48),
    )(h, gain, target)


FFN_BLK = 704
FFN_NB = 4
FFN_TT = 512


def _prev8(n, tt):
    return jnp.maximum(n * (tt // SUBLANES) - 1, 0)


def _ffn_conv(cur, prev8, w, b):
    s1 = _shift_down(cur, prev8, 1)
    s2 = _shift_down(cur, prev8, 2)
    return w[0:1] * s2 + w[1:2] * s1 + w[2:3] * cur + b, s1, s2


def _ffn_specs(t, tt, order):
    pair = lambda rows, row_index: pl.BlockSpec((2, None, rows, FFN_BLK), lambda j, n: (0, j, row_index(n), 0))
    return dict(cur=pair(tt, order), prev=pair(SUBLANES, lambda n: _prev8(order(n), tt)), w=pair(3, lambda n: 0), b=pair(1, lambda n: 0),
                one=pl.BlockSpec((None, tt, FFN_BLK), lambda j, n: (j, order(n), 0)))


def _ffn_act_fwd(up, cw, cb, comm, *, name):
    t = up.shape[2]
    tt = min(FFN_TT, t)
    nt = t // tt
    sp = _ffn_specs(t, tt, lambda n: n)
    hbm = pl.BlockSpec(memory_space=pl.ANY)

    def body(u_ref, p_ref, w_ref, b_ref, *rest):
        comm_in, (o_ref,), comm_out, _, comm_sems = comm.split(rest, n_out=1, n_scratch=0)
        j, n = pl.program_id(0), pl.program_id(1)

        @pl.when(jnp.logical_and(j == 0, n == 0))
        def _():
            comm.start(comm_in, comm_out, comm_sems)

        first = n == 0
        gate, _, _ = _ffn_conv(u_ref[0], jnp.where(first, 0.0, p_ref[0]), w_ref[0], b_ref[0])
        val, _, _ = _ffn_conv(u_ref[1], jnp.where(first, 0.0, p_ref[1]), w_ref[1], b_ref[1])
        o_ref[...] = (gate * _sigmoid(gate) * val).astype(BF16)

        @pl.when(jnp.logical_and(j == FFN_NB - 1, n == nt - 1))
        def _():
            comm.finish(comm_in, comm_out, comm_sems)

    outs = pl.pallas_call(
        body, grid=(FFN_NB, nt), in_specs=[sp["cur"], sp["prev"], sp["w"], sp["b"]] + [hbm] * len(comm.arrays),
        out_specs=[sp["one"]] + [hbm] * len(comm.out_shapes),
        out_shape=[jax.ShapeDtypeStruct((FFN_NB, t, FFN_BLK), BF16)] + comm.out_shapes, scratch_shapes=comm.scratch, name=name,
        compiler_params=_cp(("arbitrary", "arbitrary"), 48),
    )(up, up, cw, cb, *comm.arrays)
    return outs[0], outs[1:]


def _ffn_act_bwd(up, da, cw, cb, *, name):
    t = up.shape[2]
    tt = min(FFN_TT, t)
    nt = t // tt
    sp = _ffn_specs(t, tt, lambda n: nt - 1 - n)

    def body(u_ref, p_ref, da_ref, w_ref, b_ref, du_ref, dw_ref, db_ref, head):
        n = pl.program_id(1)
        tile0 = n == nt - 1

        @pl.when(n == 0)
        def _():
            for r in (head, dw_ref, db_ref):
                r[...] = jnp.zeros_like(r)

        convs = [_ffn_conv(u_ref[s], jnp.where(tile0, 0.0, p_ref[s]), w_ref[s], b_ref[s]) for s in range(2)]
        gate, val = convs[0][0], convs[1][0]
        d = da_ref[...]
        sg = _sigmoid(gate)
        dcs = (d * val * sg * (1.0 + gate * (1.0 - sg)), d * gate * sg)
        for s in range(2):
            dc, w, hd = dcs[s], w_ref[s], head[s]
            _, x1, x2 = convs[s]
            du_ref[s] = (w[2:3] * dc + w[1:2] * _shift_up(dc, hd, 1) + w[0:1] * _shift_up(dc, hd, 2)).astype(BF16)
            dw_ref[s, 0:1, :] += jnp.sum(dc * x2, axis=0, keepdims=True)
            dw_ref[s, 1:2, :] += jnp.sum(dc * x1, axis=0, keepdims=True)
            dw_ref[s, 2:3, :] += jnp.sum(dc * u_ref[s], axis=0, keepdims=True)
            db_ref[s] += jnp.sum(dc, axis=0, keepdims=True)
            head[s] = dc[:SUBLANES]

    return pl.pallas_call(
        body, grid=(FFN_NB, nt), in_specs=[sp["cur"], sp["prev"], sp["one"], sp["w"], sp["b"]],
        out_specs=[sp["cur"], sp["w"], sp["b"]],
        out_shape=[jax.ShapeDtypeStruct(up.shape, BF16), jax.ShapeDtypeStruct(cw.shape, F32), jax.ShapeDtypeStruct(cb.shape, F32)],
        scratch_shapes=[pltpu.VMEM((2, SUBLANES, FFN_BLK), F32)], name=name,
        compiler_params=_cp(("parallel", "arbitrary"), 48),
    )(up, up, da, cw, cb)


LRU_TT = 256
LRU_CT = 512
GELU_C = math.sqrt(2.0 / math.pi)
GELU_A = 0.044715


def _gelu(x):
    return 0.5 * x * (1.0 + jnp.tanh(GELU_C * (x + GELU_A * x * x * x)))


def _gelu_grad(x):
    th = jnp.tanh(GELU_C * (x + GELU_A * x * x * x))
    return 0.5 * (1.0 + th) + 0.5 * x * (1.0 - th * th) * GELU_C * (1.0 + 3.0 * GELU_A * x * x)


def _neg_expm1(x):
    poly = -x * (1.0 + x * (0.5 + x * (1.0 / 6 + x * (1.0 / 24 + x * (1.0 / 120)))))
    return jnp.where(x > -0.1, poly, 1.0 - jnp.exp(x))


def _conv4(x, p8, w, b=None):
    s1, s2, s3 = _shift_down(x, p8, 1), _shift_down(x, p8, 2), _shift_down(x, p8, 3)
    y = w[0:1] * s3 + w[1:2] * s2 + w[2:3] * s1 + w[3:4] * x
    return (y if b is None else y + b), (s1, s2, s3)


def _conv4_bwd(dy, head, x, shifts, w):
    s1, s2, s3 = shifts
    dx = w[3:4] * dy + w[2:3] * _shift_up(dy, head, 1) + w[1:2] * _shift_up(dy, head, 2) + w[0:1] * _shift_up(dy, head, 3)
    dws = [jnp.sum(dy * s, axis=0, keepdims=True) for s in (s3, s2, s1, x)]
    return dx, dws


def _blockdiag(x, w_ref, dims=NN):
    nb = x.shape[1] // LANES
    return jnp.concatenate([_bdot(x[:, LANES * i:LANES * (i + 1)], w_ref[i], dims) for i in range(nb)], axis=1)


def _lru_gates(xr, wa_ref, wx_ref, ba, bx, lam):
    r = _sigmoid(_blockdiag(xr, wa_ref) + ba)
    i = _sigmoid(_blockdiag(xr, wx_ref) + bx)
    sp = _softplus(-lam)
    la = -LRU_C * r * sp
    a = jnp.exp(la)
    mult = jnp.sqrt(_neg_expm1(2.0 * la))
    return r, i, sp, a, mult


def _lru_specs(t, tt, ct, order):
    nb = ct // LANES
    cur = pl.BlockSpec((tt, ct), lambda j, n: (order(n), j))
    prev = pl.BlockSpec((SUBLANES, ct), lambda j, n: (_prev8(order(n), tt), j))
    vec = lambda rows: pl.BlockSpec((rows, ct), lambda j, n: (0, j))
    blk = pl.BlockSpec((nb, LANES, LANES), lambda j, n: (j, 0, 0))
    return cur, prev, vec, blk


def _lru_fwd(gate, xpre, cw, cb, wa, ba, wx, bx, lam, comm, *, name):
    t, c = gate.shape
    tt, ct = min(LRU_TT, t), LRU_CT
    nj, nt = c // ct, t // tt
    cur, prev, vec, blk = _lru_specs(t, tt, ct, lambda n: n)
    hbm = pl.BlockSpec(memory_space=pl.ANY)

    def body(gate_ref, x_ref, p_ref, cw_ref, cb_ref, wa_ref, ba_ref, wx_ref, bx_ref, lam_ref, *rest):
        comm_in, (y_ref, hs_ref), comm_out, (carry,), comm_sems = comm.split(rest, n_out=2, n_scratch=1)
        j, n = pl.program_id(0), pl.program_id(1)

        @pl.when(jnp.logical_and(j == 0, n == 0))
        def _():
            comm.start(comm_in, comm_out, comm_sems)

        @pl.when(n == 0)
        def _():
            carry[...] = jnp.zeros_like(carry)

        p8 = jnp.where(n == 0, 0.0, p_ref[...])
        xr, _ = _conv4(x_ref[...], p8, cw_ref[...], cb_ref[...])
        r, i, sp, a, mult = _lru_gates(xr, wa_ref, wx_ref, ba_ref[...], bx_ref[...], lam_ref[...])
        acc_a, acc_b = a, mult * (i * xr)
        d = 1
        while d < tt:
            acc_b = acc_a * _down_fill(acc_b, d, 0.0) + acc_b
            acc_a = acc_a * _down_fill(acc_a, d, 1.0)
            d *= 2
        hs = acc_b + acc_a * carry[0:1]
        carry[...] = jnp.broadcast_to(hs[tt - 1:tt], carry.shape)
        hs_ref[...] = hs
        y_ref[...] = (_gelu(gate_ref[...]) * hs).astype(BF16)

        @pl.when(jnp.logical_and(j == nj - 1, n == nt - 1))
        def _():
            comm.finish(comm_in, comm_out, comm_sems)

    outs = pl.pallas_call(
        body, grid=(nj, nt),
        in_specs=[cur, cur, prev, vec(4), vec(1), blk, vec(1), blk, vec(1), vec(1)] + [hbm] * len(comm.arrays),
        out_specs=[cur, cur] + [hbm] * len(comm.out_shapes),
        out_shape=[jax.ShapeDtypeStruct((t, c), BF16), jax.ShapeDtypeStruct((t, c), F32)] + comm.out_shapes,
        scratch_shapes=[pltpu.VMEM((SUBLANES, ct), F32)] + comm.scratch, name=name,
        compiler_params=_cp(("arbitrary", "arbitrary"), 48),
    )(gate, xpre, xpre, cw, cb, wa, ba, wx, bx, lam, *comm.arrays)
    return outs[0], outs[1], outs[2:]


def _lru_bwd(gate, xpre, hs, dy, cw, cb, wa, ba, wx, bx, lam, *, name):
    t, c = gate.shape
    tt, ct = min(LRU_TT, t), LRU_CT
    nt = t // tt
    cur, prev, vec, blk = _lru_specs(t, tt, ct, lambda n: nt - 1 - n)

    def body(gate_ref, x_ref, p_ref, hs_ref, phs_ref, dy_ref, cw_ref, cb_ref, wa_ref, ba_ref, wx_ref, bx_ref, lam_ref,
             dgate_ref, dx_ref, dcw_ref, dcb_ref, dwa_ref, dba_ref, dwx_ref, dbx_ref, dlam_ref, carry, head):
        n = pl.program_id(1)
        tile0 = n == nt - 1

        @pl.when(n == 0)
        def _():
            for ref in (carry, head, dcw_ref, dcb_ref, dwa_ref, dba_ref, dwx_ref, dbx_ref, dlam_ref):
                ref[...] = jnp.zeros_like(ref)

        xp, cwv, lam = x_ref[...], cw_ref[...], lam_ref[...]
        p8 = jnp.where(tile0, 0.0, p_ref[...])
        xr, shifts = _conv4(xp, p8, cwv, cb_ref[...])
        r, i, sp, a, mult = _lru_gates(xr, wa_ref, wx_ref, ba_ref[...], bx_ref[...], lam)
        gate, hsv, dyv = gate_ref[...], hs_ref[...], dy_ref[...]
        dgate_ref[...] = (dyv * hsv * _gelu_grad(gate)).astype(BF16)
        acc_b = dyv * _gelu(gate) + jnp.where(_rows(a.shape) == tt - 1, carry[0:1], 0.0)
        acc_a = _up_fill(a, 1, 0.0)
        d = 1
        while d < tt:
            acc_b = acc_b + acc_a * _up_fill(acc_b, d, 0.0)
            acc_a = acc_a * _up_fill(acc_a, d, 0.0)
            d *= 2
        gsum = acc_b
        carry[...] = jnp.broadcast_to(a[0:1] * gsum[0:1], carry.shape)
        hprev = _shift_down(hsv, jnp.where(tile0, 0.0, phs_ref[...]), 1)
        da = gsum * hprev
        dmult = gsum * i * xr
        di = gsum * mult * xr
        dxr = gsum * mult * i
        dla = da * a - dmult * (a * a) / mult
        dr = dla * (-LRU_C * sp)
        dlam_ref[...] += jnp.sum(dla * (-LRU_C * r), axis=0, keepdims=True) * (-_sigmoid(-lam))
        dpa = dr * r * (1.0 - r)
        dpx = di * i * (1.0 - i)
        dba_ref[...] += jnp.sum(dpa, axis=0, keepdims=True)
        dbx_ref[...] += jnp.sum(dpx, axis=0, keepdims=True)
        dxr = dxr + _blockdiag(dpa, wa_ref, NT) + _blockdiag(dpx, wx_ref, NT)
        for b in range(ct // LANES):
            sl = slice(LANES * b, LANES * (b + 1))
            dwa_ref[b] += _bdot(xr[:, sl], dpa[:, sl], TN)
            dwx_ref[b] += _bdot(xr[:, sl], dpx[:, sl], TN)
        dx, dws = _conv4_bwd(dxr, head[...], xp, shifts, cwv)
        dx_ref[...] = dx.astype(BF16)
        for k in range(4):
            dcw_ref[k:k + 1, :] += dws[k]
        dcb_ref[...] += jnp.sum(dxr, axis=0, keepdims=True)
        head[...] = dxr[:SUBLANES]

    return pl.pallas_call(
        body, grid=(c // ct, nt),
        in_specs=[cur, cur, prev, cur, prev, cur, vec(4), vec(1), blk, vec(1), blk, vec(1), vec(1)],
        out_specs=[cur, cur, vec(4), vec(1), blk, vec(1), blk, vec(1), vec(1)],
        out_shape=[jax.ShapeDtypeStruct((t, c), BF16)] * 2 + [jax.ShapeDtypeStruct((4, c), F32), jax.ShapeDtypeStruct((1, c), F32),
                   jax.ShapeDtypeStruct(wa.shape, F32), jax.ShapeDtypeStruct((1, c), F32),
                   jax.ShapeDtypeStruct(wx.shape, F32), jax.ShapeDtypeStruct((1, c), F32), jax.ShapeDtypeStruct((1, c), F32)],
        scratch_shapes=[pltpu.VMEM((SUBLANES, ct), F32)] * 2, name=name,
        compiler_params=_cp(("parallel", "arbitrary"), 48),
    )(gate, xpre, xpre, hs, hs, dy, cw, cb, wa, ba, wx, bx, lam)


RET_W = HEADS * HEAD_DIM
HALF = HEAD_DIM // 2


def _ret_tables(t):
    c = RET_CHUNK
    inv_freq = ROPE_BASE ** (-jnp.arange(HALF, dtype=F32) / HALF)
    ang = jnp.arange(t, dtype=jnp.int32).astype(F32)[:, None] * inv_freq[None, :]
    cos, sin = jnp.cos(ang), jnp.sin(ang)
    cosf = jnp.concatenate([cos, cos], axis=1)
    sinf = jnp.concatenate([-sin, sin], axis=1)
    log_gamma = jnp.log1p(-jnp.exp2(-5.0 - jnp.arange(HEADS, dtype=F32)))
    idx = jnp.arange(c, dtype=F32)
    rel = idx[:, None] - idx[None, :]
    causal = rel >= 0
    dmask = jnp.where(causal, jnp.exp(log_gamma[:, None, None] * jnp.where(causal, rel, 0.0)), 0.0)
    ktail = jnp.exp(log_gamma[:, None] * (c - 1 - idx))
    qdec = jnp.exp(log_gamma[:, None] * (idx + 1.0))
    rowtab = jnp.broadcast_to(jnp.stack([ktail, qdec], axis=1)[..., None], (HEADS, 2, c, HEAD_DIM))
    cdec = jnp.broadcast_to(jnp.exp(log_gamma * c)[:, None, None], (HEADS, SUBLANES, HEAD_DIM))
    return cosf, sinf, dmask, rowtab, cdec


def _rotary(x, cosf, sinf):
    return x * cosf + pltpu.roll(x, HALF, 1) * sinf


def _rotary_t(dx, cosf, sinf):
    return dx * cosf + pltpu.roll(dx * sinf, HALF, 1)


def _ret_specs(c, order):
    full = lambda shape: pl.BlockSpec(shape, lambda n: (0,) * len(shape))
    return dict(
        proj=pl.BlockSpec((c, 4 * RET_W), lambda n: (order(n), 0)),
        rot=pl.BlockSpec((c, HEAD_DIM), lambda n: (order(n), 0)),
        dmask=full((HEADS, c, c)), rowtab=full((HEADS, 2, c, HEAD_DIM)), cdec=full((HEADS, SUBLANES, HEAD_DIM)),
        state=pl.BlockSpec((1, HEADS, HEAD_DIM, HEAD_DIM), lambda n: (order(n), 0, 0, 0)),
        half=pl.BlockSpec((c, RET_W), lambda n: (order(n), 0)),
    )


def _ret_head(p_ref, h, cosf, sinf):
    sl = lambda j: slice(j * RET_W + h * HEAD_DIM, j * RET_W + (h + 1) * HEAD_DIM)
    q, k, v, g = p_ref[:, sl(0)], p_ref[:, sl(1)], p_ref[:, sl(2)], p_ref[:, sl(3)]
    return _rotary(q, cosf, sinf), _rotary(k, cosf, sinf) * QSCALE, v, g


def _ret_fwd(proj, tables, comm, *, name):
    t = proj.shape[0]
    c = RET_CHUNK
    nc = t // c
    sp = _ret_specs(c, lambda n: n)
    hbm = pl.BlockSpec(memory_space=pl.ANY)

    def body(p_ref, cos_ref, sin_ref, dm_ref, rt_ref, cd_ref, *rest):
        comm_in, (y_ref, s_ref), comm_out, (state,), comm_sems = comm.split(rest, n_out=2, n_scratch=1)

        @pl.when(pl.program_id(0) == 0)
        def _():
            state[...] = jnp.zeros_like(state)
            comm.start(comm_in, comm_out, comm_sems)

        cosf, sinf = cos_ref[...], sin_ref[...]
        for h in range(HEADS):
            qr, kr, v, g = _ret_head(p_ref, h, cosf, sinf)
            s0 = state[h]
            s_ref[0, h] = s0
            scores = _bdot(qr, kr, NT) * dm_ref[h]
            o = _bdot(scores, v) + _bdot(qr * rt_ref[h, 1], s0)
            state[h] = s0 * cd_ref[h][0:1] + _bdot(kr * rt_ref[h, 0], v, TN)
            rinv = lax.rsqrt(jnp.mean(o * o, axis=-1, keepdims=True) + EPS)
            y_ref[:, h * HEAD_DIM:(h + 1) * HEAD_DIM] = (o * rinv * (g * _sigmoid(g))).astype(BF16)

        @pl.when(pl.program_id(0) == nc - 1)
        def _():
            comm.finish(comm_in, comm_out, comm_sems)

    outs = pl.pallas_call(
        body, grid=(nc,),
        in_specs=[sp["proj"], sp["rot"], sp["rot"], sp["dmask"], sp["rowtab"], sp["cdec"]] + [hbm] * len(comm.arrays),
        out_specs=[sp["half"], sp["state"]] + [hbm] * len(comm.out_shapes),
        out_shape=[jax.ShapeDtypeStruct((t, 2 * RET_W), BF16), jax.ShapeDtypeStruct((nc, HEADS, HEAD_DIM, HEAD_DIM), F32)]
        + comm.out_shapes,
        scratch_shapes=[pltpu.VMEM((HEADS, HEAD_DIM, HEAD_DIM), F32)] + comm.scratch, name=name,
        compiler_params=_cp(("arbitrary",), 48),
    )(proj, *tables, *comm.arrays)
    return outs[0], outs[1], outs[2:]


def _ret_bwd(proj, tables, states, dy, comm, *, name):
    t = proj.shape[0]
    c = RET_CHUNK
    nc = t // c
    sp = _ret_specs(c, lambda n: nc - 1 - n)
    hbm = pl.BlockSpec(memory_space=pl.ANY)

    def body(p_ref, cos_ref, sin_ref, dm_ref, rt_ref, cd_ref, s_ref, dy_ref, *rest):
        comm_in, (dp_ref,), comm_out, (dstate,), comm_sems = comm.split(rest, n_out=1, n_scratch=1)

        @pl.when(pl.program_id(0) == 0)
        def _():
            dstate[...] = jnp.zeros_like(dstate)
            comm.start(comm_in, comm_out, comm_sems)

        cosf, sinf = cos_ref[...], sin_ref[...]
        for h in range(HEADS):
            qr, kr, v, g = _ret_head(p_ref, h, cosf, sinf)
            s0, dm, ktl, qdc = s_ref[0, h], dm_ref[h], rt_ref[h, 0], rt_ref[h, 1]
            scores = _bdot(qr, kr, NT) * dm
            qd, kt = qr * qdc, kr * ktl
            o = _bdot(scores, v) + _bdot(qd, s0)
            rinv = lax.rsqrt(jnp.mean(o * o, axis=-1, keepdims=True) + EPS)
            oh = o * rinv
            sg = _sigmoid(g)
            dyh = dy_ref[:, h * HEAD_DIM:(h + 1) * HEAD_DIM]
            dg = dyh * oh * sg * (1.0 + g * (1.0 - sg))
            dyo = dyh * (g * sg)
            do = rinv * (dyo - oh * jnp.mean(dyo * oh, axis=-1, keepdims=True))
            ds1 = dstate[h]
            dsc = _bdot(do, v, NT) * dm
            dv = _bdot(scores, do, TN) + _bdot(kt, ds1)
            dqr = _bdot(dsc, kr) + _bdot(do, s0, NT) * qdc
            dkr = (_bdot(dsc, qr, TN) + _bdot(v, ds1, NT) * ktl) * QSCALE
            dstate[h] = ds1 * cd_ref[h][0:1] + _bdot(qd, do, TN)
            pieces = (_rotary_t(dqr, cosf, sinf), _rotary_t(dkr, cosf, sinf), dv, dg)
            for j, piece in enumerate(pieces):
                dp_ref[:, j * RET_W + h * HEAD_DIM:j * RET_W + (h + 1) * HEAD_DIM] = piece.astype(BF16)

        @pl.when(pl.program_id(0) == nc - 1)
        def _():
            comm.finish(comm_in, comm_out, comm_sems)

    outs = pl.pallas_call(
        body, grid=(nc,),
        in_specs=[sp["proj"], sp["rot"], sp["rot"], sp["dmask"], sp["rowtab"], sp["cdec"], sp["state"], sp["half"]]
        + [hbm] * len(comm.arrays),
        out_specs=[sp["proj"]] + [hbm] * len(comm.out_shapes),
        out_shape=[jax.ShapeDtypeStruct((t, 8 * RET_W), BF16)] + comm.out_shapes,
        scratch_shapes=[pltpu.VMEM((HEADS, HEAD_DIM, HEAD_DIM), F32)] + comm.scratch, name=name,
        compiler_params=_cp(("arbitrary",), 48),
    )(proj, *tables, states, dy, *comm.arrays)
    return outs[0], outs[1:]


GDN_W = HEADS * HEAD_DIM
GDN_CONV = 3 * GDN_W
NEUMANN_STEPS = 5


def _gdn_gates(ps, al, dt):
    return _sigmoid(ps), -jnp.exp(al) * _softplus(ps + dt)


def _cumsum_rows(x):
    d = 1
    while d < x.shape[0]:
        x = x + _down_fill(x, d, 0.0)
        d *= 2
    return x


def _rev_cumsum_rows(x):
    d = 1
    while d < x.shape[0]:
        x = x + _up_fill(x, d, 0.0)
        d *= 2
    return x


class _Chunk:
    pass


def _gdn_chunk(qc, kc, v, beta, g, s0, inv=None):
    c = GDN_CHUNK
    z = _Chunk()
    z.rq = lax.rsqrt(jnp.sum(qc * qc, axis=-1, keepdims=True) + EPS)
    z.rk = lax.rsqrt(jnp.sum(kc * kc, axis=-1, keepdims=True) + EPS)
    z.qn, z.k = qc * z.rq, kc * z.rk
    z.q = z.qn * QSCALE
    z.v, z.beta = v, beta
    gc = _cumsum_rows(jnp.broadcast_to(g, (c, LANES)))
    ri, ci = _rows((c, c)), _cols((c, c))
    z.tril, z.strict = ri >= ci, ri > ci
    diff = gc[:, :c] - gc.T[:c, :]
    z.decay = jnp.where(z.tril, jnp.exp(jnp.where(z.tril, diff, 0.0)), 0.0)
    z.eg = jnp.exp(gc)
    glast = gc[c - 1:c, :]
    z.egl = jnp.exp(glast - gc)
    z.cd = jnp.exp(glast)
    z.kb = z.k * beta
    both = _bdot(jnp.concatenate([z.kb, z.q], axis=0), z.k, NT)
    z.m, z.qk = both[:c], both[c:]
    if inv is None:
        neg = -jnp.where(z.strict, z.m * z.decay, 0.0)
        inv = (ri == ci).astype(F32) + neg
        pw = neg
        for _ in range(NEUMANN_STEPS):
            pw = _dot3(pw, pw)
            inv = inv + _dot3(inv, pw)
    z.inv = inv
    z.vb, z.kbg = v * beta, z.kb * z.eg
    solved = _dot3(inv, jnp.concatenate([z.vb, z.kbg], axis=1))
    z.u, z.w = solved[:, :HEAD_DIM], solved[:, HEAD_DIM:]
    z.attn = jnp.where(z.tril, z.qk * z.decay, 0.0)
    z.qd, z.kt = z.q * z.eg, z.k * z.egl
    through = _bdot(jnp.concatenate([z.w, z.qd], axis=0), s0)
    z.vnew = z.u - through[:c]
    z.o = through[c:] + _bdot(z.attn, z.vnew)
    z.s1 = s0 * z.cd + _bdot(z.kt, z.vnew, TN)
    return z


def _gdn_chunk_bwd(z, s0, do, ds1):
    c = GDN_CHUNK
    dvnew = _bdot(z.attn, do, TN) + _bdot(z.kt, ds1)
    against = _bdot(do, jnp.concatenate([s0, z.vnew], axis=0), NT)
    dqd = against[:, :HEAD_DIM]
    dattn = jnp.where(z.tril, against[:, HEAD_DIM:], 0.0)
    ds0 = ds1 * z.cd + _bdot(jnp.concatenate([z.qd, -z.w], axis=0), jnp.concatenate([do, dvnew], axis=0), TN)
    dcd = jnp.sum(jnp.sum(s0 * ds1, axis=1, keepdims=True), axis=0, keepdims=True)
    dkt = _bdot(z.vnew, ds1, NT)
    dw = -_bdot(dvnew, s0, NT)
    dsolved = _dot3(z.inv, jnp.concatenate([dvnew, dw], axis=1), TN)
    dvb, dkbg = dsolved[:, :HEAD_DIM], dsolved[:, HEAD_DIM:]
    dl = jnp.where(z.strict, -_bdot(dsolved, jnp.concatenate([z.u, z.w], axis=1), NT), 0.0)
    dml = dl * z.decay
    dqk = dattn * z.decay
    ddecay = (dl * z.m + dattn * z.qk) * z.decay
    stacked = jnp.concatenate([dqk, dml], axis=0)
    onto_k = _bdot(stacked, z.k)
    dq = onto_k[:c] + dqd * z.eg
    dkb = onto_k[c:] + dkbg * z.eg
    dk = _bdot(stacked, jnp.concatenate([z.q, z.kb], axis=0), TN) + dkt * z.egl + dkb * z.beta
    dbeta = jnp.sum(dkb * z.k, axis=-1, keepdims=True) + jnp.sum(dvb * z.v, axis=-1, keepdims=True)
    dv = dvb * z.beta
    colsum = _dot3(ddecay, jnp.ones((c, LANES), F32), TN)
    e = jnp.sum(dkt * z.kt, axis=-1, keepdims=True)
    dgc = (jnp.sum(ddecay, axis=-1, keepdims=True) - colsum
           + jnp.sum(dkbg * z.kbg, axis=-1, keepdims=True) + jnp.sum(dqd * z.qd, axis=-1, keepdims=True) - e)
    dglast = jnp.sum(e, axis=0, keepdims=True) + dcd * z.cd
    dgc = dgc + jnp.where(_rows((c, LANES)) == c - 1, dglast, 0.0)
    dg = _rev_cumsum_rows(dgc)[:, 0:1]
    dqn = dq * QSCALE
    dqc = z.rq * (dqn - z.qn * jnp.sum(dqn * z.qn, axis=-1, keepdims=True))
    dkc = z.rk * (dk - z.k * jnp.sum(dk * z.k, axis=-1, keepdims=True))
    return dqc, dkc, dv, dbeta, dg, ds0


GDN_SUB = 1


def _gdn_specs(c, order):
    full = lambda shape: pl.BlockSpec(shape, lambda n: (0,) * len(shape))
    return dict(
        proj=pl.BlockSpec((c, 4 * GDN_W), lambda n: (order(n), 1)),
        prev=pl.BlockSpec((SUBLANES, 4 * GDN_W), lambda n: (_prev8(order(n), c), 1)),
        small=pl.BlockSpec((c, LANES), lambda n: (order(n), 0)),
        convw=full((4, GDN_CONV)), vec=full((1, LANES)),
        state=pl.BlockSpec((GDN_SUB, HEADS, HEAD_DIM, HEAD_DIM), lambda n: (order(n), 0, 0, 0)),
        inv=pl.BlockSpec((GDN_SUB, HEADS, GDN_CHUNK, GDN_CHUNK), lambda n: (order(n), 0, 0, 0)),
        half=pl.BlockSpec((c, GDN_W), lambda n: (order(n), 1)),
        any=pl.BlockSpec(memory_space=pl.ANY),
    )


def _gdn_fwd(proj, psmall, conv_w, al, dt, gain, y_in, comm, *, name):
    t = proj.shape[0]
    c = GDN_CHUNK
    nc, ns = t // c, t // (c * GDN_SUB)
    sp = _gdn_specs(c * GDN_SUB, lambda n: n)

    def body(p_ref, prev_ref, ps_ref, cw_ref, al_ref, dt_ref, gain_ref, yin_ref, *rest):
        comm_in, (y_ref, s_ref, inv_ref), comm_out, (state,), comm_sems = comm.split(rest, n_out=3, n_scratch=1)
        n = pl.program_id(0)

        @pl.when(n == 0)
        def _():
            state[...] = jnp.zeros_like(state)
            comm.start(comm_in, comm_out, comm_sems)

        p8 = jnp.where(n == 0, 0.0, prev_ref[:, :GDN_CONV])
        pre, _ = _conv4(p_ref[:, :GDN_CONV], p8, cw_ref[...])
        act = pre * _sigmoid(pre)
        beta_all, g_all = _gdn_gates(ps_ref[...], al_ref[...], dt_ref[...])
        gd_all, gain = p_ref[:, GDN_CONV:], gain_ref[...]
        swish = gd_all * _sigmoid(gd_all)
        cur = [state[h] for h in range(HEADS)]
        starts, ys = [], []
        for sub in range(GDN_SUB):
            rows = slice(sub * c, (sub + 1) * c)
            starts.append(list(cur))
            pieces = []
            for h in range(HEADS):
                sl = lambda j: slice(j * GDN_W + h * HEAD_DIM, j * GDN_W + (h + 1) * HEAD_DIM)
                z = _gdn_chunk(act[rows, sl(0)], act[rows, sl(1)], act[rows, sl(2)], beta_all[rows, h:h + 1],
                               g_all[rows, HEADS + h:HEADS + h + 1], cur[h])
                cur[h] = z.s1
                inv_ref[sub, h] = z.inv
                rinv = lax.rsqrt(jnp.mean(z.o * z.o, axis=-1, keepdims=True) + EPS)
                pieces.append(z.o * rinv * gain * swish[rows, sl(0)])
            ys.append(jnp.concatenate(pieces, axis=1))
        y_ref[...] = jnp.concatenate(ys, axis=0).astype(BF16)
        for sub in range(GDN_SUB):
            for h in range(HEADS):
                s_ref[sub, h] = starts[sub][h]
        for h in range(HEADS):
            state[h] = cur[h]

        @pl.when(n == ns - 1)
        def _():
            comm.finish(comm_in, comm_out, comm_sems)

    outs = pl.pallas_call(
        body, grid=(ns,),
        in_specs=[sp["proj"], sp["prev"], sp["small"], sp["convw"], sp["vec"], sp["vec"], sp["vec"], sp["any"]]
        + [sp["any"]] * len(comm.arrays),
        out_specs=[sp["half"], sp["state"], sp["inv"]] + [sp["any"]] * len(comm.out_shapes),
        out_shape=[jax.ShapeDtypeStruct((t, 2 * GDN_W), BF16), jax.ShapeDtypeStruct((nc, HEADS, HEAD_DIM, HEAD_DIM), F32),
                   jax.ShapeDtypeStruct((nc, HEADS, c, c), F32)] + comm.out_shapes,
        scratch_shapes=[pltpu.VMEM((HEADS, HEAD_DIM, HEAD_DIM), F32)] + comm.scratch, name=name,
        input_output_aliases={7: 0}, compiler_params=_cp(("arbitrary",), 48),
    )(proj, proj, psmall, conv_w, al, dt, gain, y_in, *comm.arrays)
    return outs[0], (outs[1], outs[2]), outs[3:]


def _gdn_bwd(proj, psmall, conv_w, al, dt, gain, states, dy, dproj_in, comm, *, name):
    t = proj.shape[0]
    c = GDN_CHUNK
    nc, ns = t // c, t // (c * GDN_SUB)
    sp = _gdn_specs(c * GDN_SUB, lambda n: ns - 1 - n)

    def body(p_ref, prev_ref, ps_ref, cw_ref, al_ref, dt_ref, gain_ref, s_ref, inv_ref, dy_ref, dpin_ref, *rest):
        comm_in, outs, comm_out, (dstate, head), comm_sems = comm.split(rest, n_out=6, n_scratch=2)
        dp_ref, dps_ref, dcw_ref, dal_ref, ddt_ref, dgain_ref = outs
        n = pl.program_id(0)
        first_rows = n == ns - 1

        @pl.when(n == 0)
        def _():
            for ref in (dstate, head, dcw_ref, dal_ref, ddt_ref, dgain_ref):
                ref[...] = jnp.zeros_like(ref)
            comm.start(comm_in, comm_out, comm_sems)

        x, cwv = p_ref[:, :GDN_CONV], cw_ref[...]
        p8 = jnp.where(first_rows, 0.0, prev_ref[:, :GDN_CONV])
        pre, shifts = _conv4(x, p8, cwv)
        sg_pre = _sigmoid(pre)
        act = pre * sg_pre
        ps, alv, dtv, gain = ps_ref[...], al_ref[...], dt_ref[...], gain_ref[...]
        beta_all, g_all = _gdn_gates(ps, alv, dtv)
        lane = _cols((c, LANES))
        dgain = jnp.zeros((1, LANES), F32)
        dcur = [dstate[h] for h in range(HEADS)]
        dact_rows, dbeta_rows, dg_rows = [None] * GDN_SUB, [None] * GDN_SUB, [None] * GDN_SUB
        for sub in reversed(range(GDN_SUB)):
            rows = slice(sub * c, (sub + 1) * c)
            dbeta_all = jnp.zeros((c, LANES), F32)
            dg_all = jnp.zeros((c, LANES), F32)
            dact = [None] * (3 * HEADS)
            for h in range(HEADS):
                sl = lambda j: slice(j * GDN_W + h * HEAD_DIM, j * GDN_W + (h + 1) * HEAD_DIM)
                s0 = s_ref[sub, h]
                z = _gdn_chunk(act[rows, sl(0)], act[rows, sl(1)], act[rows, sl(2)], beta_all[rows, h:h + 1],
                               g_all[rows, HEADS + h:HEADS + h + 1], s0, inv=inv_ref[sub, h])
                rinv = lax.rsqrt(jnp.mean(z.o * z.o, axis=-1, keepdims=True) + EPS)
                oh = z.o * rinv
                gd = p_ref[rows, sl(3)]
                sgd = _sigmoid(gd)
                dyh = dy_ref[rows, sl(0)]
                dgain = dgain + jnp.sum(dyh * oh * (gd * sgd), axis=0, keepdims=True)
                dp_ref[rows, sl(3)] = (dyh * oh * gain * sgd * (1.0 + gd * (1.0 - sgd))).astype(BF16)
                dyo = dyh * gain * (gd * sgd)
                do = rinv * (dyo - oh * jnp.mean(dyo * oh, axis=-1, keepdims=True))
                dqc, dkc, dv, dbeta, dg, dcur[h] = _gdn_chunk_bwd(z, s0, do, dcur[h])
                dact[h], dact[HEADS + h], dact[2 * HEADS + h] = dqc, dkc, dv
                dbeta_all = dbeta_all + jnp.where(lane == h, dbeta, 0.0)
                dg_all = dg_all + jnp.where(lane == HEADS + h, dg, 0.0)
            dact_rows[sub], dbeta_rows[sub], dg_rows[sub] = jnp.concatenate(dact, axis=1), dbeta_all, dg_all
        for h in range(HEADS):
            dstate[h] = dcur[h]
        dbeta_all, dg_all = jnp.concatenate(dbeta_rows, axis=0), jnp.concatenate(dg_rows, axis=0)
        dpre = jnp.concatenate(dact_rows, axis=0) * sg_pre * (1.0 + pre * (1.0 - sg_pre))
        dx, dws = _conv4_bwd(dpre, head[...], x, shifts, cwv)
        dp_ref[:, :GDN_CONV] = dx.astype(BF16)
        for k in range(4):
            dcw_ref[k:k + 1, :] += dws[k]
        head[...] = dpre[:SUBLANES]
        dsp = dg_all * (-jnp.exp(alv)) * _sigmoid(ps + dtv)
        dps_ref[...] = (dbeta_all * beta_all * (1.0 - beta_all) + dsp).astype(BF16)
        ddt_ref[...] += jnp.sum(dsp, axis=0, keepdims=True)
        dal_ref[...] += jnp.sum(dg_all * g_all, axis=0, keepdims=True)
        dgain_ref[...] += dgain

        @pl.when(n == ns - 1)
        def _():
            comm.finish(comm_in, comm_out, comm_sems)

    vec_f32 = jax.ShapeDtypeStruct((1, LANES), F32)
    outs = pl.pallas_call(
        body, grid=(ns,),
        in_specs=[sp["proj"], sp["prev"], sp["small"], sp["convw"], sp["vec"], sp["vec"], sp["vec"], sp["state"], sp["inv"],
                  sp["half"], sp["any"]]
        + [sp["any"]] * len(comm.arrays),
        out_specs=[sp["proj"], sp["small"], sp["convw"], sp["vec"], sp["vec"], sp["vec"]] + [sp["any"]] * len(comm.out_shapes),
        out_shape=[jax.ShapeDtypeStruct((t, 8 * GDN_W), BF16), jax.ShapeDtypeStruct((t, LANES), BF16),
                   jax.ShapeDtypeStruct((4, GDN_CONV), F32), vec_f32, vec_f32, vec_f32] + comm.out_shapes,
        scratch_shapes=[pltpu.VMEM((HEADS, HEAD_DIM, HEAD_DIM), F32), pltpu.VMEM((SUBLANES, GDN_CONV), F32)] + comm.scratch,
        name=name, input_output_aliases={10: 0}, compiler_params=_cp(("arbitrary",), 48),
    )(proj, proj, psmall, conv_w, al, dt, gain, *states, dy, dproj_in, *comm.arrays)
    return outs[:6], outs[6:]


def _here():
    x, y, c = lax.axis_index("x"), lax.axis_index("y"), lax.axis_index("c")
    return x, y, c, [(1 - x, y), (x, 1 - y), (1 - x, 1 - y)]


def _rdma(src, dst, send, recv, k, dev):
    return pltpu.make_async_remote_copy(src_ref=src, dst_ref=dst, send_sem=send.at[k], recv_sem=recv.at[k],
                                        device_id=dev, device_id_type=MESH)


def _dma_sems(n):
    return [pltpu.SemaphoreType.DMA((n,)), pltpu.SemaphoreType.DMA((n,)), pltpu.SemaphoreType.DMA((1,))]


COPY_PIECES = 4
COPY_PIECE_ALIGN = 16


def _row_parts(rows):
    n = COPY_PIECES if rows % (COPY_PIECES * COPY_PIECE_ALIGN) == 0 and rows >= 1024 else 1
    return [pl.ds(q * (rows // n), rows // n) for q in range(n)]


class _AllGather:
    def __init__(self, array):
        self.arrays = [array]
        self.out_shapes = [jax.ShapeDtypeStruct((N_DEV,) + array.shape, array.dtype)]
        self.parts = _row_parts(array.shape[0])
        self.scratch = _dma_sems(7 * len(self.parts))

    def start(self, ins, outs, sems):
        (src,), (out,), (send, recv, loc) = ins, outs, sems
        x, y, c, chips = _here()
        me, n = 4 * x + 2 * y + c, len(self.parts)
        pltpu.make_async_copy(src, out.at[me], loc.at[0]).start()
        for q, part in enumerate(self.parts):
            _rdma(src.at[part], out.at[me, part], send, recv, q, (x, y, 1 - c)).start()
            for j, (cx, cy) in enumerate(chips):
                _rdma(src.at[part], out.at[me, part], send, recv, (1 + j) * n + q, (cx, cy, c)).start()

    def finish(self, ins, outs, sems):
        (src,), (out,), (send, recv, loc) = ins, outs, sems
        x, y, c, chips = _here()
        sibling, me, n = (x, y, 1 - c), 4 * x + 2 * y + c, len(self.parts)
        piece = lambda k, q: _rdma(src.at[self.parts[q]], out.at[me, self.parts[q]], send, recv, k * n + q, sibling)
        for j, (cx, cy) in enumerate(chips):
            for q, part in enumerate(self.parts):
                got = out.at[4 * cx + 2 * cy + c, part]
                piece(1 + j, q).wait_recv()
                _rdma(got, got, send, recv, (4 + j) * n + q, sibling).start()
        for k in (0, 4, 5, 6):
            for q in range(n):
                piece(k, q).wait_recv()
        for k in range(7):
            for q in range(n):
                piece(k, q).wait_send()
        pltpu.make_async_copy(src, out.at[me], loc.at[0]).wait()


class _ChipExchange:
    def __init__(self, array):
        self.arrays = [array]
        self.out_shapes = [jax.ShapeDtypeStruct(array.shape, array.dtype)]
        self.parts = _row_parts(array.shape[1])
        self.scratch = _dma_sems(3 * len(self.parts))

    def _copies(self, ins, outs, sems):
        (src,), (out,), (send, recv, loc) = ins, outs, sems
        x, y, c, chips = _here()
        here, n = 2 * x + y, len(self.parts)
        local = pltpu.make_async_copy(src.at[here], out.at[here], loc.at[0])
        return local, [_rdma(src.at[2 * cx + cy, part], out.at[here, part], send, recv, j * n + q, (cx, cy, c))
                       for j, (cx, cy) in enumerate(chips) for q, part in enumerate(self.parts)]

    def start(self, ins, outs, sems):
        local, remote = self._copies(ins, outs, sems)
        local.start()
        for cp in remote:
            cp.start()

    def finish(self, ins, outs, sems):
        local, remote = self._copies(ins, outs, sems)
        for cp in remote:
            cp.wait()
        local.wait()


class _PairSwap:
    def __init__(self, array):
        self.arrays = [array]
        self.out_shapes = [jax.ShapeDtypeStruct(array.shape[1:], array.dtype)]
        self.parts = _row_parts(array.shape[2])
        self.scratch = _dma_sems(4 * len(self.parts))

    def _copies(self, ins, outs, sems):
        (src,), (theirs,), (send, recv, _) = ins, outs, sems
        x, y, c, _ = _here()
        return [_rdma(src.at[1 - c, p, part], theirs.at[p, part], send, recv, p * len(self.parts) + q, (x, y, 1 - c))
                for p in range(4) for q, part in enumerate(self.parts)]

    def start(self, ins, outs, sems):
        for cp in self._copies(ins, outs, sems):
            cp.start()

    def finish(self, ins, outs, sems):
        for cp in self._copies(ins, outs, sems):
            cp.wait()


class _Comm:
    def __init__(self, ops):
        self.ops = ops
        self.arrays = [a for op in ops for a in op.arrays]
        self.out_shapes = [s for op in ops for s in op.out_shapes]
        self.scratch = [s for op in ops for s in op.scratch]

    def split(self, rest, n_out, n_scratch):
        cuts = np.cumsum([0, len(self.arrays), n_out, len(self.out_shapes), n_scratch, len(self.scratch)])
        assert cuts[-1] == len(rest)
        return tuple(rest[a:b] for a, b in zip(cuts[:-1], cuts[1:]))

    def _each(self, method, ins, outs, sems):
        i = o = s = 0
        for op in self.ops:
            ni, no, ns = len(op.arrays), len(op.out_shapes), len(op.scratch)
            getattr(op, method)(ins[i:i + ni], outs[o:o + no], sems[s:s + ns])
            i, o, s = i + ni, o + no, s + ns

    def start(self, ins, outs, sems):
        self._each("start", ins, outs, sems)

    def finish(self, ins, outs, sems):
        self._each("finish", ins, outs, sems)

    def run(self, name):
        def body(*refs):
            ins, _, outs, _, sems = self.split(refs, 0, 0)
            self.start(ins, outs, sems)
            self.finish(ins, outs, sems)

        hbm = pl.BlockSpec(memory_space=pl.ANY)
        return pl.pallas_call(body, in_specs=[hbm] * len(self.arrays), out_specs=[hbm] * len(self.out_shapes),
                              out_shape=self.out_shapes, scratch_shapes=self.scratch, name=name)(*self.arrays)


def _sum_slots(x, *, name, tr=None):
    n, r, l = x.shape
    tr = r if tr is None else tr

    def body(x_ref, o_ref):
        acc = x_ref[0].astype(F32)
        for s in range(1, n):
            acc = acc + x_ref[s].astype(F32)
        o_ref[...] = acc

    return pl.pallas_call(
        body, grid=(r // tr,), in_specs=[pl.BlockSpec((n, tr, l), lambda i: (0, i, 0))],
        out_specs=pl.BlockSpec((tr, l), lambda i: (i, 0)), out_shape=jax.ShapeDtypeStruct((r, l), F32),
        name=name, compiler_params=_cp(("parallel",), 48),
    )(x)


def _pair_add(both, theirs, *, name, tr):
    _, n, r, l = both.shape

    def body(a_ref, b_ref, o_ref):
        mine = jnp.where(lax.axis_index("c") == 0, a_ref[0], a_ref[1])
        o_ref[...] = (mine.astype(F32) + b_ref[...].astype(F32)).astype(BF16)

    spec = pl.BlockSpec((n, tr, l), lambda i: (0, i, 0))
    return pl.pallas_call(body, grid=(r // tr,), in_specs=[pl.BlockSpec((2, n, tr, l), lambda i: (0, 0, i, 0)), spec], out_specs=spec,
                          out_shape=jax.ShapeDtypeStruct(theirs.shape, BF16), name=name,
                          compiler_params=_cp(("parallel",), 48))(both, theirs)


ADAM_TILE_ELEMS = 512 * 1024


def _adam(w, g, m, v, *, name):
    shape = w.shape
    cols = shape[-1]
    rows = math.prod(shape[:-1]) if len(shape) > 1 else 1
    tr = rows
    if rows * cols > ADAM_TILE_ELEMS:
        tr = max(d for d in range(SUBLANES, ADAM_TILE_ELEMS // cols + 1, SUBLANES) if rows % d == 0)
    c1, c2 = 1.0 - ADAM_B1 ** ADAM_STEP, 1.0 - ADAM_B2 ** ADAM_STEP

    def body(w_ref, g_ref, m_ref, v_ref, d_ref, m2_ref, v2_ref):
        gv = g_ref[...]
        m2 = ADAM_B1 * m_ref[...] + (1.0 - ADAM_B1) * gv
        v2 = ADAM_B2 * v_ref[...] + (1.0 - ADAM_B2) * (gv * gv)
        d_ref[...] = -ADAM_LR * ((m2 / c1) / (jnp.sqrt(v2 / c2) + ADAM_EPS) + ADAM_WD * w_ref[...])
        m2_ref[...] = m2
        v2_ref[...] = v2

    spec = pl.BlockSpec((tr, cols), lambda i: (i, 0))
    outs = pl.pallas_call(
        body, grid=(rows // tr,), in_specs=[spec] * 4, out_specs=[spec] * 3,
        out_shape=[jax.ShapeDtypeStruct((rows, cols), F32)] * 3, name=name, compiler_params=_cp(("parallel",), 48),
    )(*(a.reshape(rows, cols) for a in (w, g, m, v)))
    return tuple(o.reshape(shape) for o in outs)


WEIGHTS = ['norm_mix', 'norm_ffn', 'ret_gdn_w_in', 'gdn_conv_w', 'gdn_a_log', 'gdn_dt_bias', 'gdn_out_gain', 'ret_gdn_w_out',
           'lru_w_in', 'lru_conv_w', 'lru_conv_b', 'lru_w_a', 'lru_b_a', 'lru_w_x', 'lru_b_x', 'lru_lambda', 'lru_w_out',
           'ffn_w_up', 'ffn_conv_w', 'ffn_conv_b', 'ffn_w_down', 'norm_final']
SMALL = {'gdn_conv_w': ((1, 4, 192), 2), 'lru_conv_w': ((1, 4, 128), 2), 'lru_conv_b': ((1, 128), 1), 'lru_b_a': ((1, 128), 1),
         'lru_b_x': ((1, 128), 1), 'lru_lambda': ((1, 128), 1), 'ffn_conv_w': ((2, 3, 704), 2)}
EARLY = {'lru_conv_w': (1, 4, 1024), 'lru_conv_b': (1, 1024), 'lru_b_a': (1, 1024), 'lru_b_x': (1, 1024), 'lru_lambda': (1, 1024),
         'ffn_conv_w': (2, 3, 5632), 'norm_ffn': (2, 1024), 'norm_mix1': (1, 1024), 'lru_w_a': (1, 8, 128, 128),
         'lru_w_x': (1, 8, 128, 128), 'ffn_conv_b': (2, 5632), 'norm_final': (1024,)}
LATE = {'gdn_conv_w': (1, 4, 1536), 'norm_mix0': (1, 1024), 'gdn_a_log': (1, 4), 'gdn_dt_bias': (1, 4), 'gdn_out_gain': (1, 128),
        'loss': (1, 1)}


def _rows_of(n_elems):
    return -(-n_elems // LANES)


def _to_rows(a, lead=()):
    flat = a.reshape(lead + (-1,))
    pad = _rows_of(flat.shape[-1]) * LANES - flat.shape[-1]
    if pad:
        flat = jnp.pad(flat, [(0, 0)] * len(lead) + [(0, pad)])
    return flat.reshape(lead + (-1, LANES))


def _pack(pieces, total_rows, lead=()):
    buf = jnp.concatenate(pieces, axis=len(lead))
    pad = total_rows - buf.shape[len(lead)]
    return jnp.pad(buf, [(0, 0)] * len(lead) + [(0, pad), (0, 0)]) if pad else buf


def _unpack(buf, shapes, lead=()):
    out, off = [], 0
    for shape in shapes:
        n = math.prod(shape)
        rows = _rows_of(n)
        piece = lax.slice_in_dim(buf, off, off + rows, axis=len(lead)).reshape(lead + (rows * LANES,))
        out.append(lax.slice_in_dim(piece, 0, n, axis=len(lead)).reshape(lead + shape))
        off += rows
    return out


def _join_blocks(g, axis):
    m = jnp.moveaxis(g, 0, axis)
    return m.reshape(m.shape[:axis] + (N_DEV * m.shape[axis + 1],) + m.shape[axis + 2:])


def _split_blocks(full, axis):
    s = full.shape
    return jnp.moveaxis(full.reshape(s[:axis] + (N_DEV, s[axis] // N_DEV) + s[axis + 1:]), axis, 0)


def _small_rows(shapes):
    total = sum(_rows_of(math.prod(s)) for s in shapes)
    return -(-total // SUBLANES) * SUBLANES


FFN_TM = 1024
FFN_TN = 512


def _ffn_forward(h, gain, w_up, cw, cb, w_down, tag, comm):
    t, d = h.shape
    tm, tn, blk, nb = min(FFN_TM, t), FFN_TN, FFN_BLK, FFN_NB
    hn = _norm_fwd(h, gain, name=f"ffn{tag}_norm")
    up = _mmx(hn, w_up, dims=NN, grid=(t // tm, 2 * nb, 1), name=f"ffn{tag}_up", tile=(tm, blk),
              a_spec=pl.BlockSpec((tm, d), lambda i, j, k: (i, 0)),
              b_spec=pl.BlockSpec((None, d, blk), lambda i, j, k: (j, 0, 0)),
              o_spec=pl.BlockSpec((None, None, tm, blk), lambda i, j, k: (j // nb, j % nb, i, 0)),
              out_shape=jax.ShapeDtypeStruct((2, nb, t, blk), F32))
    act, comm_out = _ffn_act_fwd(up, cw, cb, comm, name=f"ffn{tag}_act")
    out = _mmx(act, w_down.reshape(nb, blk, d), dims=NN, grid=(t // tm, d // tn, nb // 2), name=f"ffn{tag}_down", tile=(tm, tn),
               res=h, pairs=True,
               a_spec=pl.BlockSpec((2, tm, blk), lambda i, j, k: (k, i, 0)),
               b_spec=pl.BlockSpec((2, blk, tn), lambda i, j, k: (k, 0, j)),
               o_spec=pl.BlockSpec((tm, tn), lambda i, j, k: (i, j)),
               out_shape=jax.ShapeDtypeStruct((t, d), F32))
    return out, (hn, up, act), comm_out


def _ffn_backward(dh, h, gain, saved, w_up, cw, cb, w_down, tag):
    hn, up, act = saved
    t, d = h.shape
    tm, tn, blk, nb = min(FFN_TM, t), FFN_TN, FFN_BLK, FFN_NB
    tk = t
    da = _mmx(dh, w_down, dims=NT, grid=(t // tm, nb, 1), name=f"ffn{tag}_d_act", tile=(tm, blk),
              a_spec=pl.BlockSpec((tm, d), lambda i, j, k: (i, 0)),
              b_spec=pl.BlockSpec((blk, d), lambda i, j, k: (j, 0)),
              o_spec=pl.BlockSpec((None, tm, blk), lambda i, j, k: (j, i, 0)),
              out_shape=jax.ShapeDtypeStruct((nb, t, blk), F32))
    dwd = _mmx(act, dh, dims=TN, grid=(nb, d // tn, t // tk), name=f"ffn{tag}_d_wdown", tile=(blk, tn), split_rows=blk // 2,
               a_spec=pl.BlockSpec((None, tk, blk), lambda i, j, k: (i, k, 0)),
               b_spec=pl.BlockSpec((tk, tn), lambda i, j, k: (k, j)),
               o_spec=pl.BlockSpec((2, None, blk // 2, tn), lambda i, j, k: (0, i, 0, j)),
               out_shape=jax.ShapeDtypeStruct((2, nb, blk // 2, d), BF16))
    dup, dcw, dcb = _ffn_act_bwd(up, da, cw, cb, name=f"ffn{tag}_act_bwd")
    half = nb // 2
    dhn = _mmx(dup, w_up, dims=NT, grid=(t // tm, d // tn, nb), name=f"ffn{tag}_d_hn", tile=(tm, tn), pairs=True,
               a_spec=pl.BlockSpec((None, 2, tm, blk), lambda i, j, k: (k // half, k % half, i, 0)),
               b_spec=pl.BlockSpec((2, tn, blk), lambda i, j, k: (k, j, 0)),
               o_spec=pl.BlockSpec((tm, tn), lambda i, j, k: (i, j)),
               out_shape=jax.ShapeDtypeStruct((t, d), F32))
    dwu = _mmx(hn, dup, dims=TN, grid=(1, 2 * nb, t // tk), name=f"ffn{tag}_d_wup", tile=(d, blk),
               a_spec=pl.BlockSpec((tk, d), lambda i, j, k: (k, 0)),
               b_spec=pl.BlockSpec((None, None, tk, blk), lambda i, j, k: (j // nb, j % nb, k, 0)),
               o_spec=pl.BlockSpec((None, None, d, blk), lambda i, j, k: (j % 2, j // 2, 0, 0)),
               out_shape=jax.ShapeDtypeStruct((2, N_DEV // 2, d, blk), BF16))
    dh_in, dgain = _norm_bwd(h, gain, dhn, dh, name=f"ffn{tag}_norm_bwd")
    conv_w = dcw.transpose(2, 0, 1, 3).reshape(3, 2 * nb * blk)
    return dh_in, dict(w_up=dwu, w_down=dwd, conv_w=conv_w, conv_b=dcb.reshape(1, 2 * nb * blk), norm=dgain)


def kernel(x, norm_mix, norm_ffn, ret_gdn_w_in, gdn_conv_w, gdn_a_log, gdn_dt_bias, gdn_out_gain, ret_gdn_w_out, lru_w_in, lru_conv_w, lru_conv_b, lru_w_a, lru_b_a, lru_w_x, lru_b_x, lru_lambda, lru_w_out, ffn_w_up, ffn_conv_w, ffn_conv_b, ffn_w_down, norm_final, loss_target, m_norm_mix, m_norm_ffn, m_ret_gdn_w_in, m_gdn_conv_w, m_gdn_a_log, m_gdn_dt_bias, m_gdn_out_gain, m_ret_gdn_w_out, m_lru_w_in, m_lru_conv_w, m_lru_conv_b, m_lru_w_a, m_lru_b_a, m_lru_w_x, m_lru_b_x, m_lru_lambda, m_lru_w_out, m_ffn_w_up, m_ffn_conv_w, m_ffn_conv_b, m_ffn_w_down, m_norm_final, v_norm_mix, v_norm_ffn, v_ret_gdn_w_in, v_gdn_conv_w, v_gdn_a_log, v_gdn_dt_bias, v_gdn_out_gain, v_ret_gdn_w_out, v_lru_w_in, v_lru_conv_w, v_lru_conv_b, v_lru_w_a, v_lru_b_a, v_lru_w_x, v_lru_b_x, v_lru_lambda, v_lru_w_out, v_ffn_w_up, v_ffn_conv_w, v_ffn_conv_b, v_ffn_w_down, v_norm_final):
    given = dict(locals())
    w = {n: given[n] for n in WEIGHTS}
    me = 4 * lax.axis_index("x") + 2 * lax.axis_index("y") + lax.axis_index("c")
    t = x.shape[1]

    first_shards = {'w_in0': ret_gdn_w_in[0]}
    rest_shards = {'w_out0': ret_gdn_w_out[0], 'lru_in': lru_w_in[0], 'lru_out': lru_w_out[0], 'up0': ffn_w_up[0], 'up1': ffn_w_up[1],
                   'down0': ffn_w_down[0], 'down1': ffn_w_down[1]}
    small_shapes = [s for s, _ in SMALL.values()]
    small_buf = _pack([_to_rows(w[n]) for n in SMALL], _small_rows(small_shapes))
    *g_first, g_small = _Comm([_AllGather(a.astype(BF16)) for a in first_shards.values()] + [_AllGather(small_buf)]).run("gather_first")
    hosted = {'retention_fwd': ['w_out0', 'lru_out'], 'deltanet_fwd': ['lru_in', 'up0', 'down0', 'up1'], 'ffn0_act': ['down1'], 'rglru_fwd': []}
    gather_in = {host: _Comm([_AllGather(rest_shards[n].astype(BF16)) for n in names]) for host, names in hosted.items()}
    got = dict(zip(first_shards, g_first))
    small_blocks = dict(zip(SMALL, _unpack(g_small, small_shapes, lead=(N_DEV,))))
    full = {n: _join_blocks(small_blocks[n], SMALL[n][1]) for n in SMALL if n != 'ffn_conv_w'}

    w_main, w_narrow = _join_w_in(got['w_in0'], name="join_w_in")
    fcw = [small_blocks['ffn_conv_w'][:, l].reshape(2, FFN_NB, 3, FFN_BLK) for l in range(2)]
    fcb = [ffn_conv_b[l].reshape(2, FFN_NB, 1, FFN_BLK) for l in range(2)]
    gdn_cw = full['gdn_conv_w'][0]
    al_pad = jnp.pad(gdn_a_log, ((0, 0), (HEADS, LANES - 2 * HEADS)))
    dt_pad = jnp.pad(gdn_dt_bias, ((0, 0), (HEADS, LANES - 2 * HEADS)))
    lru_cw, lru_cb = full['lru_conv_w'][0], full['lru_conv_b']
    lru_ba, lru_bx, lru_lam = full['lru_b_a'], full['lru_b_x'], full['lru_lambda']
    wa, wx = lru_w_a[0], lru_w_x[0]

    h0, target = x[0], loss_target[0]
    hn0 = _norm_fwd(h0, norm_mix[0:1], name="mix0_norm")
    proj = _mm(hn0, w_main, name="mix0_in")
    pnarrow = _mm(hn0, w_narrow, name="mix0_in_narrow")
    tables = _ret_tables(t)
    y0, ret_states, g = _ret_fwd(proj, tables, gather_in['retention_fwd'], name="retention_fwd")
    got.update(zip(hosted['retention_fwd'], g))
    y0, gdn_states, g = _gdn_fwd(proj, pnarrow, gdn_cw, al_pad, dt_pad, gdn_out_gain, y0, gather_in['deltanet_fwd'], name="deltanet_fwd")
    got.update(zip(hosted['deltanet_fwd'], g))
    lru_in = _join_blocks(got['lru_in'], 1)
    lru_in_g, lru_in_x = lru_in[:, :D_MODEL], lru_in[:, D_MODEL:]
    lru_out = got['lru_out'].reshape(D_MODEL, D_MODEL)
    w_out0 = got['w_out0'].reshape(D_MODEL, D_MODEL)
    h1 = _mm(y0, w_out0, res=h0, name="mix0_out")
    h2, ffn0_saved, g = _ffn_forward(h1, norm_ffn[0:1], got['up0'], fcw[0], fcb[0], got['down0'].reshape(D_FF, D_MODEL), 0,
                                     gather_in['ffn0_act'])
    got.update(zip(hosted['ffn0_act'], g))
    hn1 = _norm_fwd(h2, norm_mix[1:2], name="mix1_norm")
    gate = _mm(hn1, lru_in_g, name="mix1_in_gate")
    xpre = _mm(hn1, lru_in_x, name="mix1_in_x")
    y1, hs, g = _lru_fwd(gate, xpre, lru_cw, lru_cb, wa, lru_ba, wx, lru_bx, lru_lam, gather_in['rglru_fwd'], name="rglru_fwd")
    got.update(zip(hosted['rglru_fwd'], g))
    h3 = _mm(y1, lru_out, res=h2, name="mix1_out")
    w_up = [got['up0'], got['up1']]
    down = [got['down0'].reshape(D_FF, D_MODEL), got['down1'].reshape(D_FF, D_MODEL)]
    h4, ffn1_saved, _ = _ffn_forward(h3, norm_ffn[1:2], w_up[1], fcw[1], fcb[1], down[1], 1, _Comm([]))
    dh4, d_norm_final, loss_part = _final_loss(h4, norm_final[None, :], target, name="final_norm_loss")

    dh3, gf1 = _ffn_backward(dh4, h3, norm_ffn[1:2], ffn1_saved, w_up[1], fcw[1], fcb[1], down[1], 1)
    dy1 = _mm(dh3, lru_out, tb=True, name="mix1_d_y")
    d_lru_out = _mm(y1, dh3, ta=True, out_dtype=BF16, name="mix1_d_wout")
    dgate, dxpre, d_lcw, d_lcb, d_wa, d_ba, d_wx, d_bx, d_lam = _lru_bwd(
        gate, xpre, hs, dy1, lru_cw, lru_cb, wa, lru_ba, wx, lru_bx, lru_lam, name="rglru_bwd")
    dhn1 = _mm(dgate, lru_in_g, tb=True, name="mix1_d_hn_gate")
    dhn1 = _mm(dxpre, lru_in_x, tb=True, res=dhn1, name="mix1_d_hn_x")
    d_lru_in = jnp.concatenate([_mm(hn1, dgate, ta=True, out_dtype=BF16, name="mix1_d_win_gate"),
                                _mm(hn1, dxpre, ta=True, out_dtype=BF16, name="mix1_d_win_x")], axis=1)
    dh2, d_mix1 = _norm_bwd(h2, norm_mix[1:2], dhn1, dh3, name="mix1_norm_bwd")
    dh1, gf0 = _ffn_backward(dh2, h1, norm_ffn[0:1], ffn0_saved, w_up[0], fcw[0], fcb[0], down[0], 0)
    dy0 = _mm(dh1, w_out0, tb=True, name="mix0_d_y")
    d_w_out0 = _mm(y0, dh1, ta=True, out_dtype=BF16, name="mix0_d_wout")

    def by_core_chip(full_grad, axis):
        blocks = _split_blocks(full_grad, axis)
        return blocks.reshape((4, 2) + blocks.shape[1:]).transpose(1, 0, 2, 3)

    def pair_add(blocks, theirs):
        return {k: _pair_add(b, o, name=f"pair_add_{k}", tr=row_tile(b.shape[2])) for (k, b), o in zip(blocks.items(), theirs)}

    row_tile = lambda rows: rows if rows <= 512 else 256

    rest_blocks = {'w_out0': by_core_chip(d_w_out0, 0), 'lru_in': by_core_chip(d_lru_in, 1), 'lru_out': by_core_chip(d_lru_out, 0),
                   'up0': gf0['w_up'], 'up1': gf1['w_up'], 'down0': gf0['w_down'], 'down1': gf1['w_down']}
    early = {'lru_conv_w': d_lcw[None], 'lru_conv_b': d_lcb, 'lru_b_a': d_ba, 'lru_b_x': d_bx, 'lru_lambda': d_lam,
             'ffn_conv_w': jnp.stack([gf0['conv_w'], gf1['conv_w']]), 'norm_ffn': jnp.concatenate([gf0['norm'], gf1['norm']], axis=0),
             'norm_mix1': d_mix1, 'lru_w_a': d_wa[None], 'lru_w_x': d_wx[None],
             'ffn_conv_b': jnp.concatenate([gf0['conv_b'], gf1['conv_b']], axis=0), 'norm_final': d_norm_final[0]}
    early_buf = _pack([_to_rows(early[n]) for n in EARLY], _small_rows(list(EARLY.values())))
    dproj, (*theirs, got_early) = _ret_bwd(proj, tables, ret_states, dy0,
                                           _Comm([_PairSwap(b) for b in rest_blocks.values()] + [_AllGather(early_buf)]), name="retention_bwd")
    z_rest = pair_add(rest_blocks, theirs)
    (dproj, dnarrow, d_gcw, d_alog, d_dtb, d_gain), w_rest = _gdn_bwd(
        proj, pnarrow, gdn_cw, al_pad, dt_pad, gdn_out_gain, gdn_states, dy0, dproj,
        _Comm([_ChipExchange(z) for z in z_rest.values()]), name="deltanet_bwd")
    dhn0 = _mm(dproj, w_main, tb=True, name="mix0_d_hn")
    dhn0 = _mm(dnarrow, w_narrow, tb=True, res=dhn0, name="mix0_d_hn_narrow")
    d_w_main = _mm(hn0, dproj, ta=True, out_dtype=BF16, name="mix0_d_win")
    d_w_narrow = _mm(hn0, dnarrow, ta=True, out_dtype=BF16, name="mix0_d_win_narrow")
    dx, d_mix0 = _norm_bwd(h0, norm_mix[0:1], dhn0, dh1, name="mix0_norm_bwd")

    first_blocks = {'w_in0': _split_w_in(d_w_main, d_w_narrow, name="split_d_w_in")}
    z_first = pair_add(first_blocks, _Comm([_PairSwap(b) for b in first_blocks.values()]).run("pair_swap_first"))
    late = {'gdn_conv_w': d_gcw[None], 'norm_mix0': d_mix0, 'gdn_a_log': d_alog[:, HEADS:2 * HEADS],
            'gdn_dt_bias': d_dtb[:, HEADS:2 * HEADS], 'gdn_out_gain': d_gain, 'loss': loss_part}
    late_buf = _pack([_to_rows(late[n]) for n in LATE], _small_rows(list(LATE.values())))
    *w_first, got_late = _Comm([_ChipExchange(z) for z in z_first.values()] + [_AllGather(late_buf)]).run("exchange_first")

    summed = {k: _sum_slots(blocks, name=f"sum_blocks_{k}", tr=row_tile(blocks.shape[1]))
              for k, blocks in list(zip(z_rest, w_rest)) + list(zip(z_first, w_first))}
    grads = {'ret_gdn_w_in': summed['w_in0'][None], 'ret_gdn_w_out': summed['w_out0'][None], 'lru_w_in': summed['lru_in'][None],
             'lru_w_out': summed['lru_out'][None], 'ffn_w_up': jnp.stack([summed['up0'], summed['up1']]),
             'ffn_w_down': jnp.stack([summed['down0'], summed['down1']])}
    partial = dict(zip(EARLY, _unpack(_sum_slots(got_early, name="sum_partials_early"), list(EARLY.values()))))
    partial.update(zip(LATE, _unpack(_sum_slots(got_late, name="sum_partials_late"), list(LATE.values()))))
    partial['norm_mix'] = jnp.concatenate([partial.pop('norm_mix0'), partial.pop('norm_mix1')], axis=0)
    loss = partial.pop('loss')[0, 0]
    for n, g_full in partial.items():
        if n in SMALL:
            shard, axis = SMALL[n]
            g_full = lax.dynamic_slice_in_dim(g_full, me * shard[axis], shard[axis], axis=axis)
        grads[n] = g_full

    delta, new_m, new_v = {}, {}, {}
    for n in WEIGHTS:
        delta[n], new_m[n], new_v[n] = _adam(w[n], grads[n], given["m_" + n], given["v_" + n], name=f"adamw_{n}")
    return (loss, dx[None], *[grads[n] for n in WEIGHTS], *[delta[n] for n in WEIGHTS],
            *[new_m[n] for n in WEIGHTS], *[new_v[n] for n in WEIGHTS])
```

```python
import math

import numpy as np
import jax
import jax.numpy as jnp
from jax import lax
from jax.experimental import pallas as pl
from jax.experimental.pallas import tpu as pltpu

F32 = jnp.float32
BF16 = jnp.bfloat16
MESH = pl.DeviceIdType.MESH

N_DEV = 8
LANES = 128
SUBLANES = 8
EPS = 1e-6
D_MODEL = 1024
HEADS = 4
HEAD_DIM = 128
RET_CHUNK = 128
GDN_CHUNK = 64
ROPE_BASE = 10000.0
LRU_C = 8.0
D_FF = 2816
MAIN_IN = 4096
SMALL_IN = 8
QSCALE = HEAD_DIM ** -0.5

ADAM_LR, ADAM_B1, ADAM_B2, ADAM_EPS, ADAM_WD, ADAM_STEP = 0.001, 0.9, 0.999, 1e-08, 0.01, 10


def _cp(sem=None, vmem_mb=None):
    kw = {}
    if sem is not None:
        kw["dimension_semantics"] = sem
    if vmem_mb is not None:
        kw["vmem_limit_bytes"] = vmem_mb << 20
    return pltpu.CompilerParams(**kw)


def _rows(shape):
    return lax.broadcasted_iota(jnp.int32, shape, 0)


def _cols(shape):
    return lax.broadcasted_iota(jnp.int32, shape, 1)


def _shift_down(cur, prev8, s):
    if s == 0:
        return cur
    rc = pltpu.roll(cur, s, 0)
    rp = pltpu.roll(prev8, s, 0)
    top = jnp.where(_rows(prev8.shape) < s, rp, rc[:SUBLANES])
    return jnp.concatenate([top, rc[SUBLANES:]], axis=0)


def _shift_up(cur, next8, s):
    if s == 0:
        return cur
    tt = cur.shape[0]
    rc = pltpu.roll(cur, tt - s, 0)
    rn = pltpu.roll(next8, SUBLANES - s, 0)
    bot = jnp.where(_rows(next8.shape) >= SUBLANES - s, rn, rc[tt - SUBLANES:])
    return jnp.concatenate([rc[:tt - SUBLANES], bot], axis=0)


def _down_fill(x, d, fill):
    return jnp.where(_rows(x.shape) < d, fill, pltpu.roll(x, d, 0))


def _up_fill(x, d, fill):
    tt = x.shape[0]
    return jnp.where(_rows(x.shape) >= tt - d, fill, pltpu.roll(x, tt - d, 0))


def _sigmoid(x):
    return 1.0 / (1.0 + jnp.exp(-x))


def _softplus(x):
    return jnp.maximum(x, 0.0) + jnp.log(1.0 + jnp.exp(-jnp.abs(x)))


def _dot(a, b, dims=(((1,), (0,)), ((), ())), precision=None):
    return lax.dot_general(a, b, dims, preferred_element_type=F32, precision=precision)


NN = (((1,), (0,)), ((), ()))
NT = (((1,), (1,)), ((), ()))
TN = (((0,), (0,)), ((), ()))


def _bdot(a, b, dims=NN):
    return _dot(a.astype(BF16), b.astype(BF16), dims)


def _split(a):
    hi = a.astype(BF16)
    return hi, (a - hi.astype(F32)).astype(BF16)


def _dot3(a, b, dims=NN):
    ah, al = _split(a)
    bh, bl = _split(b)
    return _dot(ah, bh, dims) + (_dot(ah, bl, dims) + _dot(al, bh, dims))


def _tile(dim, target):
    if dim <= target:
        return dim
    best = None
    for c in range(LANES, target + 1, LANES):
        if dim % c == 0:
            best = c
    assert best is not None, (dim, target)
    return best


def _mm(a, b, *, name, ta=False, tb=False, out_dtype=F32, res=None, tm=2048, tn=512, tk=1024):
    m, k = (a.shape[1], a.shape[0]) if ta else a.shape
    n = b.shape[0] if tb else b.shape[1]
    tn, tk = _tile(n, tn), (k if ta else _tile(k, tk))
    tm = _tile(m, tm if max(tn, tk) <= 1024 else tm // 2)
    nk = k // tk
    dims = (((0 if ta else 1,), (1 if tb else 0,)), ((), ()))

    def body(*refs):
        a_ref, b_ref = refs[:2]
        r_ref = refs[2] if res is not None else None
        o_ref = refs[3] if res is not None else refs[2]
        acc = refs[-1]
        kk = pl.program_id(2)
        part = _bdot(a_ref[...], b_ref[...], dims)

        def finish(r):
            if res is not None:
                r = r + r_ref[...]
            o_ref[...] = r.astype(out_dtype)

        if nk == 1:
            finish(part)
            return

        @pl.when(kk == 0)
        def _():
            acc[...] = part

        @pl.when(jnp.logical_and(kk > 0, kk < nk - 1))
        def _():
            acc[...] += part

        @pl.when(kk == nk - 1)
        def _():
            finish(acc[...] + part)

    a_spec = pl.BlockSpec((tk, tm), lambda i, j, kk: (kk, i)) if ta else pl.BlockSpec((tm, tk), lambda i, j, kk: (i, kk))
    b_spec = pl.BlockSpec((tn, tk), lambda i, j, kk: (j, kk)) if tb else pl.BlockSpec((tk, tn), lambda i, j, kk: (kk, j))
    o_spec = pl.BlockSpec((tm, tn), lambda i, j, kk: (i, j))
    in_specs = [a_spec, b_spec] + ([o_spec] if res is not None else [])
    args = (a, b) + ((res,) if res is not None else ())
    return pl.pallas_call(
        body, grid=(m // tm, n // tn, nk), in_specs=in_specs, out_specs=o_spec,
        out_shape=jax.ShapeDtypeStruct((m, n), out_dtype),
        scratch_shapes=[pltpu.VMEM((tm, tn), F32)] if nk > 1 else [], name=name,
        compiler_params=_cp(("parallel", "parallel", "arbitrary"), 56),
    )(*args)


def _mmx(a, b, *, dims, grid, a_spec, b_spec, o_spec, out_shape, tile, name, res=None, split_rows=None, pairs=False):
    nk = grid[-1]

    def body(*refs):
        a_ref, b_ref = refs[:2]
        r_ref = refs[2] if res is not None else None
        o_ref = refs[3] if res is not None else refs[2]
        acc = refs[-1]
        if pairs:
            part = _bdot(a_ref[0], b_ref[0], dims) + _bdot(a_ref[1], b_ref[1], dims)
        else:
            part = _bdot(a_ref[...], b_ref[...], dims)

        def finish(r):
            if res is not None:
                r = r + r_ref[...]
            if split_rows is None:
                o_ref[...] = r.astype(o_ref.dtype)
            else:
                o_ref[0] = r[:split_rows].astype(o_ref.dtype)
                o_ref[1] = r[split_rows:].astype(o_ref.dtype)

        if nk == 1:
            finish(part)
            return
        kk = pl.program_id(len(grid) - 1)

        @pl.when(kk == 0)
        def _():
            acc[...] = part

        @pl.when(jnp.logical_and(kk > 0, kk < nk - 1))
        def _():
            acc[...] += part

        @pl.when(kk == nk - 1)
        def _():
            finish(acc[...] + part)

    args = (a, b) + ((res,) if res is not None else ())
    return pl.pallas_call(
        body, grid=grid, in_specs=[a_spec, b_spec] + ([o_spec] if res is not None else []), out_specs=o_spec,
        out_shape=out_shape, scratch_shapes=[pltpu.VMEM(tile, F32)] if nk > 1 else [], name=name,
        compiler_params=_cp(("parallel",) * (len(grid) - 1) + ("arbitrary",), 56),
    )(*args)


W_IN_BLK = 513
W_IN_TR = 256


def _join_w_in(blocks, *, name):
    _, d, _ = blocks.shape
    tr = W_IN_TR

    def body(x_ref, main_ref, narrow_ref):
        for m in range(MAIN_IN // LANES):
            lo = LANES * m
            dev, off = divmod(lo, W_IN_BLK)
            if off + LANES <= W_IN_BLK:
                main_ref[:, lo:lo + LANES] = x_ref[dev, :, off:off + LANES]
            else:
                main_ref[:, lo:lo + LANES] = jnp.concatenate(
                    [x_ref[dev, :, off:W_IN_BLK], x_ref[dev + 1, :, 0:LANES - (W_IN_BLK - off)]], axis=1)
        tail = x_ref[N_DEV - 1, :, W_IN_BLK - SMALL_IN:W_IN_BLK]
        narrow_ref[...] = jnp.concatenate([tail, jnp.zeros((tr, LANES - SMALL_IN), tail.dtype)], axis=1)

    return pl.pallas_call(
        body, grid=(d // tr,), in_specs=[pl.BlockSpec((N_DEV, tr, W_IN_BLK), lambda i: (0, i, 0))],
        out_specs=[pl.BlockSpec((tr, MAIN_IN), lambda i: (i, 0)), pl.BlockSpec((tr, LANES), lambda i: (i, 0))],
        out_shape=[jax.ShapeDtypeStruct((d, MAIN_IN), blocks.dtype), jax.ShapeDtypeStruct((d, LANES), blocks.dtype)],
        name=name, compiler_params=_cp(("parallel",), 48),
    )(blocks)


def _split_w_in(main, narrow, *, name):
    d = main.shape[0]
    tr = W_IN_TR

    def body(m_ref, n_ref, o_ref):
        for dev in range(N_DEV):
            lo = W_IN_BLK * dev
            if dev < N_DEV - 1:
                piece = m_ref[:, lo:lo + W_IN_BLK]
            else:
                piece = jnp.concatenate([m_ref[:, lo:MAIN_IN], n_ref[:, 0:SMALL_IN]], axis=1)
            o_ref[dev % 2, dev // 2] = piece

    return pl.pallas_call(
        body, grid=(d // tr,),
        in_specs=[pl.BlockSpec((tr, MAIN_IN), lambda i: (i, 0)), pl.BlockSpec((tr, LANES), lambda i: (i, 0))],
        out_specs=pl.BlockSpec((2, N_DEV // 2, tr, W_IN_BLK), lambda i: (0, 0, i, 0)),
        out_shape=jax.ShapeDtypeStruct((2, N_DEV // 2, d, W_IN_BLK), main.dtype),
        name=name, compiler_params=_cp(("parallel",), 48),
    )(main, narrow)


def _norm_fwd(h, gain, *, name, tt=1024):
    t, d = h.shape
    tt = min(tt, t)

    def body(h_ref, g_ref, o_ref):
        x = h_ref[...]
        r = lax.rsqrt(jnp.mean(x * x, axis=-1, keepdims=True) + EPS)
        o_ref[...] = (x * r * g_ref[...]).astype(BF16)

    row = pl.BlockSpec((tt, d), lambda i: (i, 0))
    return pl.pallas_call(
        body, grid=(t // tt,), in_specs=[row, pl.BlockSpec((1, d), lambda i: (0, 0))], out_specs=row,
        out_shape=jax.ShapeDtypeStruct((t, d), BF16), name=name, compiler_params=_cp(("parallel",), 48),
    )(h, gain)


def _norm_bwd(h, gain, dhn, dres, *, name, tt=512):
    t, d = h.shape
    tt = min(tt, t)

    def body(h_ref, g_ref, dy_ref, dr_ref, dx_ref, dg_ref):
        x, dy = h_ref[...], dy_ref[...]
        r = lax.rsqrt(jnp.mean(x * x, axis=-1, keepdims=True) + EPS)
        xh = x * r

        @pl.when(pl.program_id(0) == 0)
        def _():
            dg_ref[...] = jnp.zeros_like(dg_ref)

        dg_ref[...] += jnp.sum(dy * xh, axis=0, keepdims=True)
        dxh = dy * g_ref[...]
        dx_ref[...] = dr_ref[...] + r * (dxh - xh * jnp.mean(dxh * xh, axis=-1, keepdims=True))

    row = pl.BlockSpec((tt, d), lambda i: (i, 0))
    vec = pl.BlockSpec((1, d), lambda i: (0, 0))
    return pl.pallas_call(
        body, grid=(t // tt,), in_specs=[row, vec, row, row], out_specs=[row, vec],
        out_shape=[jax.ShapeDtypeStruct((t, d), F32), jax.ShapeDtypeStruct((1, d), F32)],
        name=name, compiler_params=_cp(("arbitrary",), 48),
    )(h, gain, dhn, dres)


def _final_loss(h, gain, target, *, name, tt=512):
    t, d = h.shape
    tt = min(tt, t)

    def body(h_ref, g_ref, tg_ref, dx_ref, dg_ref, loss_ref):
        x = h_ref[...]
        r = lax.rsqrt(jnp.mean(x * x, axis=-1, keepdims=True) + EPS)
        xh = x * r
        err = xh * g_ref[...] - tg_ref[...]

        @pl.when(pl.program_id(0) == 0)
        def _():
            dg_ref[...] = jnp.zeros_like(dg_ref)
            loss_ref[...] = jnp.zeros_like(loss_ref)

        loss_ref[...] += 0.5 * jnp.sum(jnp.mean(err * err, axis=-1, keepdims=True), axis=0, keepdims=True)
        dy = err * (1.0 / d)
        dg_ref[...] += jnp.sum(dy * xh, axis=0, keepdims=True)
        dxh = dy * g_ref[...]
        dx_ref[...] = r * (dxh - xh * jnp.mean(dxh * xh, axis=-1, keepdims=True))

    row = pl.BlockSpec((tt, d), lambda i: (i, 0))
    vec = pl.BlockSpec((1, d), lambda i: (0, 0))
    return pl.pallas_call(
        body, grid=(t // tt,), in_specs=[row, vec, row],
        out_specs=[row, vec, pl.BlockSpec((1, 1), lambda i: (0, 0))],
        out_shape=[jax.ShapeDtypeStruct((t, d), F32), jax.ShapeDtypeStruct((1, d), F32), jax.ShapeDtypeStruct((1, 1), F32)],
        name=name, compiler_params=_cp(("arbitrary",), 48),
    )(h, gain, target)


FFN_BLK = 704
FFN_NB = 4
FFN_TT = 512


def _prev8(n, tt):
    return jnp.maximum(n * (tt // SUBLANES) - 1, 0)


def _ffn_conv(cur, prev8, w, b):
    s1 = _shift_down(cur, prev8, 1)
    s2 = _shift_down(cur, prev8, 2)
    return w[0:1] * s2 + w[1:2] * s1 + w[2:3] * cur + b, s1, s2


def _ffn_specs(t, tt, order):
    pair = lambda rows, row_index: pl.BlockSpec((2, None, rows, FFN_BLK), lambda j, n: (0, j, row_index(n), 0))
    return dict(cur=pair(tt, order), prev=pair(SUBLANES, lambda n: _prev8(order(n), tt)), w=pair(3, lambda n: 0), b=pair(1, lambda n: 0),
                one=pl.BlockSpec((None, tt, FFN_BLK), lambda j, n: (j, order(n), 0)))


def _ffn_act_fwd(up, cw, cb, comm, *, name):
    t = up.shape[2]
    tt = min(FFN_TT, t)
    nt = t // tt
    sp = _ffn_specs(t, tt, lambda n: n)
    hbm = pl.BlockSpec(memory_space=pl.ANY)

    def body(u_ref, p_ref, w_ref, b_ref, *rest):
        comm_in, (o_ref,), comm_out, _, comm_sems = comm.split(rest, n_out=1, n_scratch=0)
        j, n = pl.program_id(0), pl.program_id(1)

        @pl.when(jnp.logical_and(j == 0, n == 0))
        def _():
            comm.start(comm_in, comm_out, comm_sems)

        first = n == 0
        gate, _, _ = _ffn_conv(u_ref[0], jnp.where(first, 0.0, p_ref[0]), w_ref[0], b_ref[0])
        val, _, _ = _ffn_conv(u_ref[1], jnp.where(first, 0.0, p_ref[1]), w_ref[1], b_ref[1])
        o_ref[...] = (gate * _sigmoid(gate) * val).astype(BF16)

        @pl.when(jnp.logical_and(j == FFN_NB - 1, n == nt - 1))
        def _():
            comm.finish(comm_in, comm_out, comm_sems)

    outs = pl.pallas_call(
        body, grid=(FFN_NB, nt), in_specs=[sp["cur"], sp["prev"], sp["w"], sp["b"]] + [hbm] * len(comm.arrays),
        out_specs=[sp["one"]] + [hbm] * len(comm.out_shapes),
        out_shape=[jax.ShapeDtypeStruct((FFN_NB, t, FFN_BLK), BF16)] + comm.out_shapes, scratch_shapes=comm.scratch, name=name,
        compiler_params=_cp(("arbitrary", "arbitrary"), 48),
    )(up, up, cw, cb, *comm.arrays)
    return outs[0], outs[1:]


def _ffn_act_bwd(up, da, cw, cb, *, name):
    t = up.shape[2]
    tt = min(FFN_TT, t)
    nt = t // tt
    sp = _ffn_specs(t, tt, lambda n: nt - 1 - n)

    def body(u_ref, p_ref, da_ref, w_ref, b_ref, du_ref, dw_ref, db_ref, head):
        n = pl.program_id(1)
        tile0 = n == nt - 1

        @pl.when(n == 0)
        def _():
            for r in (head, dw_ref, db_ref):
                r[...] = jnp.zeros_like(r)

        convs = [_ffn_conv(u_ref[s], jnp.where(tile0, 0.0, p_ref[s]), w_ref[s], b_ref[s]) for s in range(2)]
        gate, val = convs[0][0], convs[1][0]
        d = da_ref[...]
        sg = _sigmoid(gate)
        dcs = (d * val * sg * (1.0 + gate * (1.0 - sg)), d * gate * sg)
        for s in range(2):
            dc, w, hd = dcs[s], w_ref[s], head[s]
            _, x1, x2 = convs[s]
            du_ref[s] = (w[2:3] * dc + w[1:2] * _shift_up(dc, hd, 1) + w[0:1] * _shift_up(dc, hd, 2)).astype(BF16)
            dw_ref[s, 0:1, :] += jnp.sum(dc * x2, axis=0, keepdims=True)
            dw_ref[s, 1:2, :] += jnp.sum(dc * x1, axis=0, keepdims=True)
            dw_ref[s, 2:3, :] += jnp.sum(dc * u_ref[s], axis=0, keepdims=True)
            db_ref[s] += jnp.sum(dc, axis=0, keepdims=True)
            head[s] = dc[:SUBLANES]

    return pl.pallas_call(
        body, grid=(FFN_NB, nt), in_specs=[sp["cur"], sp["prev"], sp["one"], sp["w"], sp["b"]],
        out_specs=[sp["cur"], sp["w"], sp["b"]],
        out_shape=[jax.ShapeDtypeStruct(up.shape, BF16), jax.ShapeDtypeStruct(cw.shape, F32), jax.ShapeDtypeStruct(cb.shape, F32)],
        scratch_shapes=[pltpu.VMEM((2, SUBLANES, FFN_BLK), F32)], name=name,
        compiler_params=_cp(("parallel", "arbitrary"), 48),
    )(up, up, da, cw, cb)


LRU_TT = 256
LRU_CT = 512
GELU_C = math.sqrt(2.0 / math.pi)
GELU_A = 0.044715


def _gelu(x):
    return 0.5 * x * (1.0 + jnp.tanh(GELU_C * (x + GELU_A * x * x * x)))


def _gelu_grad(x):
    th = jnp.tanh(GELU_C * (x + GELU_A * x * x * x))
    return 0.5 * (1.0 + th) + 0.5 * x * (1.0 - th * th) * GELU_C * (1.0 + 3.0 * GELU_A * x * x)


def _neg_expm1(x):
    poly = -x * (1.0 + x * (0.5 + x * (1.0 / 6 + x * (1.0 / 24 + x * (1.0 / 120)))))
    return jnp.where(x > -0.1, poly, 1.0 - jnp.exp(x))


def _conv4(x, p8, w, b=None):
    s1, s2, s3 = _shift_down(x, p8, 1), _shift_down(x, p8, 2), _shift_down(x, p8, 3)
    y = w[0:1] * s3 + w[1:2] * s2 + w[2:3] * s1 + w[3:4] * x
    return (y if b is None else y + b), (s1, s2, s3)


def _conv4_bwd(dy, head, x, shifts, w):
    s1, s2, s3 = shifts
    dx = w[3:4] * dy + w[2:3] * _shift_up(dy, head, 1) + w[1:2] * _shift_up(dy, head, 2) + w[0:1] * _shift_up(dy, head, 3)
    dws = [jnp.sum(dy * s, axis=0, keepdims=True) for s in (s3, s2, s1, x)]
    return dx, dws


def _blockdiag(x, w_ref, dims=NN):
    nb = x.shape[1] // LANES
    return jnp.concatenate([_bdot(x[:, LANES * i:LANES * (i + 1)], w_ref[i], dims) for i in range(nb)], axis=1)


def _lru_gates(xr, wa_ref, wx_ref, ba, bx, lam):
    r = _sigmoid(_blockdiag(xr, wa_ref) + ba)
    i = _sigmoid(_blockdiag(xr, wx_ref) + bx)
    sp = _softplus(-lam)
    la = -LRU_C * r * sp
    a = jnp.exp(la)
    mult = jnp.sqrt(_neg_expm1(2.0 * la))
    return r, i, sp, a, mult


def _lru_specs(t, tt, ct, order):
    nb = ct // LANES
    cur = pl.BlockSpec((tt, ct), lambda j, n: (order(n), j))
    prev = pl.BlockSpec((SUBLANES, ct), lambda j, n: (_prev8(order(n), tt), j))
    vec = lambda rows: pl.BlockSpec((rows, ct), lambda j, n: (0, j))
    blk = pl.BlockSpec((nb, LANES, LANES), lambda j, n: (j, 0, 0))
    return cur, prev, vec, blk


def _lru_fwd(gate, xpre, cw, cb, wa, ba, wx, bx, lam, comm, *, name):
    t, c = gate.shape
    tt, ct = min(LRU_TT, t), LRU_CT
    nj, nt = c // ct, t // tt
    cur, prev, vec, blk = _lru_specs(t, tt, ct, lambda n: n)
    hbm = pl.BlockSpec(memory_space=pl.ANY)

    def body(gate_ref, x_ref, p_ref, cw_ref, cb_ref, wa_ref, ba_ref, wx_ref, bx_ref, lam_ref, *rest):
        comm_in, (y_ref, hs_ref), comm_out, (carry,), comm_sems = comm.split(rest, n_out=2, n_scratch=1)
        j, n = pl.program_id(0), pl.program_id(1)

        @pl.when(jnp.logical_and(j == 0, n == 0))
        def _():
            comm.start(comm_in, comm_out, comm_sems)

        @pl.when(n == 0)
        def _():
            carry[...] = jnp.zeros_like(carry)

        p8 = jnp.where(n == 0, 0.0, p_ref[...])
        xr, _ = _conv4(x_ref[...], p8, cw_ref[...], cb_ref[...])
        r, i, sp, a, mult = _lru_gates(xr, wa_ref, wx_ref, ba_ref[...], bx_ref[...], lam_ref[...])
        acc_a, acc_b = a, mult * (i * xr)
        d = 1
        while d < tt:
            acc_b = acc_a * _down_fill(acc_b, d, 0.0) + acc_b
            acc_a = acc_a * _down_fill(acc_a, d, 1.0)
            d *= 2
        hs = acc_b + acc_a * carry[0:1]
        carry[...] = jnp.broadcast_to(hs[tt - 1:tt], carry.shape)
        hs_ref[...] = hs
        y_ref[...] = (_gelu(gate_ref[...]) * hs).astype(BF16)

        @pl.when(jnp.logical_and(j == nj - 1, n == nt - 1))
        def _():
            comm.finish(comm_in, comm_out, comm_sems)

    outs = pl.pallas_call(
        body, grid=(nj, nt),
        in_specs=[cur, cur, prev, vec(4), vec(1), blk, vec(1), blk, vec(1), vec(1)] + [hbm] * len(comm.arrays),
        out_specs=[cur, cur] + [hbm] * len(comm.out_shapes),
        out_shape=[jax.ShapeDtypeStruct((t, c), BF16), jax.ShapeDtypeStruct((t, c), F32)] + comm.out_shapes,
        scratch_shapes=[pltpu.VMEM((SUBLANES, ct), F32)] + comm.scratch, name=name,
        compiler_params=_cp(("arbitrary", "arbitrary"), 48),
    )(gate, xpre, xpre, cw, cb, wa, ba, wx, bx, lam, *comm.arrays)
    return outs[0], outs[1], outs[2:]


def _lru_bwd(gate, xpre, hs, dy, cw, cb, wa, ba, wx, bx, lam, *, name):
    t, c = gate.shape
    tt, ct = min(LRU_TT, t), LRU_CT
    nt = t // tt
    cur, prev, vec, blk = _lru_specs(t, tt, ct, lambda n: nt - 1 - n)

    def body(gate_ref, x_ref, p_ref, hs_ref, phs_ref, dy_ref, cw_ref, cb_ref, wa_ref, ba_ref, wx_ref, bx_ref, lam_ref,
             dgate_ref, dx_ref, dcw_ref, dcb_ref, dwa_ref, dba_ref, dwx_ref, dbx_ref, dlam_ref, carry, head):
        n = pl.program_id(1)
        tile0 = n == nt - 1

        @pl.when(n == 0)
        def _():
            for ref in (carry, head, dcw_ref, dcb_ref, dwa_ref, dba_ref, dwx_ref, dbx_ref, dlam_ref):
                ref[...] = jnp.zeros_like(ref)

        xp, cwv, lam = x_ref[...], cw_ref[...], lam_ref[...]
        p8 = jnp.where(tile0, 0.0, p_ref[...])
        xr, shifts = _conv4(xp, p8, cwv, cb_ref[...])
        r, i, sp, a, mult = _lru_gates(xr, wa_ref, wx_ref, ba_ref[...], bx_ref[...], lam)
        gate, hsv, dyv = gate_ref[...], hs_ref[...], dy_ref[...]
        dgate_ref[...] = (dyv * hsv * _gelu_grad(gate)).astype(BF16)
        acc_b = dyv * _gelu(gate) + jnp.where(_rows(a.shape) == tt - 1, carry[0:1], 0.0)
        acc_a = _up_fill(a, 1, 0.0)
        d = 1
        while d < tt:
            acc_b = acc_b + acc_a * _up_fill(acc_b, d, 0.0)
            acc_a = acc_a * _up_fill(acc_a, d, 0.0)
            d *= 2
        gsum = acc_b
        carry[...] = jnp.broadcast_to(a[0:1] * gsum[0:1], carry.shape)
        hprev = _shift_down(hsv, jnp.where(tile0, 0.0, phs_ref[...]), 1)
        da = gsum * hprev
        dmult = gsum * i * xr
        di = gsum * mult * xr
        dxr = gsum * mult * i
        dla = da * a - dmult * (a * a) / mult
        dr = dla * (-LRU_C * sp)
        dlam_ref[...] += jnp.sum(dla * (-LRU_C * r), axis=0, keepdims=True) * (-_sigmoid(-lam))
        dpa = dr * r * (1.0 - r)
        dpx = di * i * (1.0 - i)
        dba_ref[...] += jnp.sum(dpa, axis=0, keepdims=True)
        dbx_ref[...] += jnp.sum(dpx, axis=0, keepdims=True)
        dxr = dxr + _blockdiag(dpa, wa_ref, NT) + _blockdiag(dpx, wx_ref, NT)
        for b in range(ct // LANES):
            sl = slice(LANES * b, LANES * (b + 1))
            dwa_ref[b] += _bdot(xr[:, sl], dpa[:, sl], TN)
            dwx_ref[b] += _bdot(xr[:, sl], dpx[:, sl], TN)
        dx, dws = _conv4_bwd(dxr, head[...], xp, shifts, cwv)
        dx_ref[...] = dx.astype(BF16)
        for k in range(4):
            dcw_ref[k:k + 1, :] += dws[k]
        dcb_ref[...] += jnp.sum(dxr, axis=0, keepdims=True)
        head[...] = dxr[:SUBLANES]

    return pl.pallas_call(
        body, grid=(c // ct, nt),
        in_specs=[cur, cur, prev, cur, prev, cur, vec(4), vec(1), blk, vec(1), blk, vec(1), vec(1)],
        out_specs=[cur, cur, vec(4), vec(1), blk, vec(1), blk, vec(1), vec(1)],
        out_shape=[jax.ShapeDtypeStruct((t, c), BF16)] * 2 + [jax.ShapeDtypeStruct((4, c), F32), jax.ShapeDtypeStruct((1, c), F32),
                   jax.ShapeDtypeStruct(wa.shape, F32), jax.ShapeDtypeStruct((1, c), F32),
                   jax.ShapeDtypeStruct(wx.shape, F32), jax.ShapeDtypeStruct((1, c), F32), jax.ShapeDtypeStruct((1, c), F32)],
        scratch_shapes=[pltpu.VMEM((SUBLANES, ct), F32)] * 2, name=name,
        compiler_params=_cp(("parallel", "arbitrary"), 48),
    )(gate, xpre, xpre, hs, hs, dy, cw, cb, wa, ba, wx, bx, lam)


RET_W = HEADS * HEAD_DIM
HALF = HEAD_DIM // 2


def _ret_tables(t):
    c = RET_CHUNK
    inv_freq = ROPE_BASE ** (-jnp.arange(HALF, dtype=F32) / HALF)
    ang = jnp.arange(t, dtype=jnp.int32).astype(F32)[:, None] * inv_freq[None, :]
    cos, sin = jnp.cos(ang), jnp.sin(ang)
    cosf = jnp.concatenate([cos, cos], axis=1)
    sinf = jnp.concatenate([-sin, sin], axis=1)
    log_gamma = jnp.log1p(-jnp.exp2(-5.0 - jnp.arange(HEADS, dtype=F32)))
    idx = jnp.arange(c, dtype=F32)
    rel = idx[:, None] - idx[None, :]
    causal = rel >= 0
    dmask = jnp.where(causal, jnp.exp(log_gamma[:, None, None] * jnp.where(causal, rel, 0.0)), 0.0)
    ktail = jnp.exp(log_gamma[:, None] * (c - 1 - idx))
    qdec = jnp.exp(log_gamma[:, None] * (idx + 1.0))
    rowtab = jnp.broadcast_to(jnp.stack([ktail, qdec], axis=1)[..., None], (HEADS, 2, c, HEAD_DIM))
    cdec = jnp.broadcast_to(jnp.exp(log_gamma * c)[:, None, None], (HEADS, SUBLANES, HEAD_DIM))
    return cosf, sinf, dmask, rowtab, cdec


def _rotary(x, cosf, sinf):
    return x * cosf + pltpu.roll(x, HALF, 1) * sinf


def _rotary_t(dx, cosf, sinf):
    return dx * cosf + pltpu.roll(dx * sinf, HALF, 1)


def _ret_specs(c, order):
    full = lambda shape: pl.BlockSpec(shape, lambda n: (0,) * len(shape))
    return dict(
        proj=pl.BlockSpec((c, 4 * RET_W), lambda n: (order(n), 0)),
        rot=pl.BlockSpec((c, HEAD_DIM), lambda n: (order(n), 0)),
        dmask=full((HEADS, c, c)), rowtab=full((HEADS, 2, c, HEAD_DIM)), cdec=full((HEADS, SUBLANES, HEAD_DIM)),
        state=pl.BlockSpec((1, HEADS, HEAD_DIM, HEAD_DIM), lambda n: (order(n), 0, 0, 0)),
        half=pl.BlockSpec((c, RET_W), lambda n: (order(n), 0)),
    )


def _ret_head(p_ref, h, cosf, sinf):
    sl = lambda j: slice(j * RET_W + h * HEAD_DIM, j * RET_W + (h + 1) * HEAD_DIM)
    q, k, v, g = p_ref[:, sl(0)], p_ref[:, sl(1)], p_ref[:, sl(2)], p_ref[:, sl(3)]
    return _rotary(q, cosf, sinf), _rotary(k, cosf, sinf) * QSCALE, v, g


def _ret_fwd(proj, tables, comm, *, name):
    t = proj.shape[0]
    c = RET_CHUNK
    nc = t // c
    sp = _ret_specs(c, lambda n: n)
    hbm = pl.BlockSpec(memory_space=pl.ANY)

    def body(p_ref, cos_ref, sin_ref, dm_ref, rt_ref, cd_ref, *rest):
        comm_in, (y_ref, s_ref), comm_out, (state,), comm_sems = comm.split(rest, n_out=2, n_scratch=1)

        @pl.when(pl.program_id(0) == 0)
        def _():
            state[...] = jnp.zeros_like(state)
            comm.start(comm_in, comm_out, comm_sems)

        cosf, sinf = cos_ref[...], sin_ref[...]
        for h in range(HEADS):
            qr, kr, v, g = _ret_head(p_ref, h, cosf, sinf)
            s0 = state[h]
            s_ref[0, h] = s0
            scores = _bdot(qr, kr, NT) * dm_ref[h]
            o = _bdot(scores, v) + _bdot(qr * rt_ref[h, 1], s0)
            state[h] = s0 * cd_ref[h][0:1] + _bdot(kr * rt_ref[h, 0], v, TN)
            rinv = lax.rsqrt(jnp.mean(o * o, axis=-1, keepdims=True) + EPS)
            y_ref[:, h * HEAD_DIM:(h + 1) * HEAD_DIM] = (o * rinv * (g * _sigmoid(g))).astype(BF16)

        @pl.when(pl.program_id(0) == nc - 1)
        def _():
            comm.finish(comm_in, comm_out, comm_sems)

    outs = pl.pallas_call(
        body, grid=(nc,),
        in_specs=[sp["proj"], sp["rot"], sp["rot"], sp["dmask"], sp["rowtab"], sp["cdec"]] + [hbm] * len(comm.arrays),
        out_specs=[sp["half"], sp["state"]] + [hbm] * len(comm.out_shapes),
        out_shape=[jax.ShapeDtypeStruct((t, 2 * RET_W), BF16), jax.ShapeDtypeStruct((nc, HEADS, HEAD_DIM, HEAD_DIM), F32)]
        + comm.out_shapes,
        scratch_shapes=[pltpu.VMEM((HEADS, HEAD_DIM, HEAD_DIM), F32)] + comm.scratch, name=name,
        compiler_params=_cp(("arbitrary",), 48),
    )(proj, *tables, *comm.arrays)
    return outs[0], outs[1], outs[2:]


def _ret_bwd(proj, tables, states, dy, comm, *, name):
    t = proj.shape[0]
    c = RET_CHUNK
    nc = t // c
    sp = _ret_specs(c, lambda n: nc - 1 - n)
    hbm = pl.BlockSpec(memory_space=pl.ANY)

    def body(p_ref, cos_ref, sin_ref, dm_ref, rt_ref, cd_ref, s_ref, dy_ref, *rest):
        comm_in, (dp_ref,), comm_out, (dstate,), comm_sems = comm.split(rest, n_out=1, n_scratch=1)

        @pl.when(pl.program_id(0) == 0)
        def _():
            dstate[...] = jnp.zeros_like(dstate)
            comm.start(comm_in, comm_out, comm_sems)

        cosf, sinf = cos_ref[...], sin_ref[...]
        for h in range(HEADS):
            qr, kr, v, g = _ret_head(p_ref, h, cosf, sinf)
            s0, dm, ktl, qdc = s_ref[0, h], dm_ref[h], rt_ref[h, 0], rt_ref[h, 1]
            scores = _bdot(qr, kr, NT) * dm
            qd, kt = qr * qdc, kr * ktl
            o = _bdot(scores, v) + _bdot(qd, s0)
            rinv = lax.rsqrt(jnp.mean(o * o, axis=-1, keepdims=True) + EPS)
            oh = o * rinv
            sg = _sigmoid(g)
            dyh = dy_ref[:, h * HEAD_DIM:(h + 1) * HEAD_DIM]
            dg = dyh * oh * sg * (1.0 + g * (1.0 - sg))
            dyo = dyh * (g * sg)
            do = rinv * (dyo - oh * jnp.mean(dyo * oh, axis=-1, keepdims=True))
            ds1 = dstate[h]
            dsc = _bdot(do, v, NT) * dm
            dv = _bdot(scores, do, TN) + _bdot(kt, ds1)
            dqr = _bdot(dsc, kr) + _bdot(do, s0, NT) * qdc
            dkr = (_bdot(dsc, qr, TN) + _bdot(v, ds1, NT) * ktl) * QSCALE
            dstate[h] = ds1 * cd_ref[h][0:1] + _bdot(qd, do, TN)
            pieces = (_rotary_t(dqr, cosf, sinf), _rotary_t(dkr, cosf, sinf), dv, dg)
            for j, piece in enumerate(pieces):
                dp_ref[:, j * RET_W + h * HEAD_DIM:j * RET_W + (h + 1) * HEAD_DIM] = piece.astype(BF16)

        @pl.when(pl.program_id(0) == nc - 1)
        def _():
            comm.finish(comm_in, comm_out, comm_sems)

    outs = pl.pallas_call(
        body, grid=(nc,),
        in_specs=[sp["proj"], sp["rot"], sp["rot"], sp["dmask"], sp["rowtab"], sp["cdec"], sp["state"], sp["half"]]
        + [hbm] * len(comm.arrays),
        out_specs=[sp["proj"]] + [hbm] * len(comm.out_shapes),
        out_shape=[jax.ShapeDtypeStruct((t, 8 * RET_W), BF16)] + comm.out_shapes,
        scratch_shapes=[pltpu.VMEM((HEADS, HEAD_DIM, HEAD_DIM), F32)] + comm.scratch, name=name,
        compiler_params=_cp(("arbitrary",), 48),
    )(proj, *tables, states, dy, *comm.arrays)
    return outs[0], outs[1:]


GDN_W = HEADS * HEAD_DIM
GDN_CONV = 3 * GDN_W
NEUMANN_STEPS = 5


def _gdn_gates(ps, al, dt):
    return _sigmoid(ps), -jnp.exp(al) * _softplus(ps + dt)


def _cumsum_rows(x):
    d = 1
    while d < x.shape[0]:
        x = x + _down_fill(x, d, 0.0)
        d *= 2
    return x


def _rev_cumsum_rows(x):
    d = 1
    while d < x.shape[0]:
        x = x + _up_fill(x, d, 0.0)
        d *= 2
    return x


class _Chunk:
    pass


def _gdn_chunk(qc, kc, v, beta, g, s0, inv=None):
    c = GDN_CHUNK
    z = _Chunk()
    z.rq = lax.rsqrt(jnp.sum(qc * qc, axis=-1, keepdims=True) + EPS)
    z.rk = lax.rsqrt(jnp.sum(kc * kc, axis=-1, keepdims=True) + EPS)
    z.qn, z.k = qc * z.rq, kc * z.rk
    z.q = z.qn * QSCALE
    z.v, z.beta = v, beta
    gc = _cumsum_rows(jnp.broadcast_to(g, (c, LANES)))
    ri, ci = _rows((c, c)), _cols((c, c))
    z.tril, z.strict = ri >= ci, ri > ci
    diff = gc[:, :c] - gc.T[:c, :]
    z.decay = jnp.where(z.tril, jnp.exp(jnp.where(z.tril, diff, 0.0)), 0.0)
    z.eg = jnp.exp(gc)
    glast = gc[c - 1:c, :]
    z.egl = jnp.exp(glast - gc)
    z.cd = jnp.exp(glast)
    z.kb = z.k * beta
    both = _bdot(jnp.concatenate([z.kb, z.q], axis=0), z.k, NT)
    z.m, z.qk = both[:c], both[c:]
    if inv is None:
        neg = -jnp.where(z.strict, z.m * z.decay, 0.0)
        inv = (ri == ci).astype(F32) + neg
        pw = neg
        for _ in range(NEUMANN_STEPS):
            pw = _dot3(pw, pw)
            inv = inv + _dot3(inv, pw)
    z.inv = inv
    z.vb, z.kbg = v * beta, z.kb * z.eg
    solved = _dot3(inv, jnp.concatenate([z.vb, z.kbg], axis=1))
    z.u, z.w = solved[:, :HEAD_DIM], solved[:, HEAD_DIM:]
    z.attn = jnp.where(z.tril, z.qk * z.decay, 0.0)
    z.qd, z.kt = z.q * z.eg, z.k * z.egl
    through = _bdot(jnp.concatenate([z.w, z.qd], axis=0), s0)
    z.vnew = z.u - through[:c]
    z.o = through[c:] + _bdot(z.attn, z.vnew)
    z.s1 = s0 * z.cd + _bdot(z.kt, z.vnew, TN)
    return z


def _gdn_chunk_bwd(z, s0, do, ds1):
    c = GDN_CHUNK
    dvnew = _bdot(z.attn, do, TN) + _bdot(z.kt, ds1)
    against = _bdot(do, jnp.concatenate([s0, z.vnew], axis=0), NT)
    dqd = against[:, :HEAD_DIM]
    dattn = jnp.where(z.tril, against[:, HEAD_DIM:], 0.0)
    ds0 = ds1 * z.cd + _bdot(jnp.concatenate([z.qd, -z.w], axis=0), jnp.concatenate([do, dvnew], axis=0), TN)
    dcd = jnp.sum(jnp.sum(s0 * ds1, axis=1, keepdims=True), axis=0, keepdims=True)
    dkt = _bdot(z.vnew, ds1, NT)
    dw = -_bdot(dvnew, s0, NT)
    dsolved = _dot3(z.inv, jnp.concatenate([dvnew, dw], axis=1), TN)
    dvb, dkbg = dsolved[:, :HEAD_DIM], dsolved[:, HEAD_DIM:]
    dl = jnp.where(z.strict, -_bdot(dsolved, jnp.concatenate([z.u, z.w], axis=1), NT), 0.0)
    dml = dl * z.decay
    dqk = dattn * z.decay
    ddecay = (dl * z.m + dattn * z.qk) * z.decay
    stacked = jnp.concatenate([dqk, dml], axis=0)
    onto_k = _bdot(stacked, z.k)
    dq = onto_k[:c] + dqd * z.eg
    dkb = onto_k[c:] + dkbg * z.eg
    dk = _bdot(stacked, jnp.concatenate([z.q, z.kb], axis=0), TN) + dkt * z.egl + dkb * z.beta
    dbeta = jnp.sum(dkb * z.k, axis=-1, keepdims=True) + jnp.sum(dvb * z.v, axis=-1, keepdims=True)
    dv = dvb * z.beta
    colsum = _dot3(ddecay, jnp.ones((c, LANES), F32), TN)
    e = jnp.sum(dkt * z.kt, axis=-1, keepdims=True)
    dgc = (jnp.sum(ddecay, axis=-1, keepdims=True) - colsum
           + jnp.sum(dkbg * z.kbg, axis=-1, keepdims=True) + jnp.sum(dqd * z.qd, axis=-1, keepdims=True) - e)
    dglast = jnp.sum(e, axis=0, keepdims=True) + dcd * z.cd
    dgc = dgc + jnp.where(_rows((c, LANES)) == c - 1, dglast, 0.0)
    dg = _rev_cumsum_rows(dgc)[:, 0:1]
    dqn = dq * QSCALE
    dqc = z.rq * (dqn - z.qn * jnp.sum(dqn * z.qn, axis=-1, keepdims=True))
    dkc = z.rk * (dk - z.k * jnp.sum(dk * z.k, axis=-1, keepdims=True))
    return dqc, dkc, dv, dbeta, dg, ds0


GDN_SUB = 1


def _gdn_specs(c, order):
    full = lambda shape: pl.BlockSpec(shape, lambda n: (0,) * len(shape))
    return dict(
        proj=pl.BlockSpec((c, 4 * GDN_W), lambda n: (order(n), 1)),
        prev=pl.BlockSpec((SUBLANES, 4 * GDN_W), lambda n: (_prev8(order(n), c), 1)),
        small=pl.BlockSpec((c, LANES), lambda n: (order(n), 0)),
        convw=full((4, GDN_CONV)), vec=full((1, LANES)),
        state=pl.BlockSpec((GDN_SUB, HEADS, HEAD_DIM, HEAD_DIM), lambda n: (order(n), 0, 0, 0)),
        inv=pl.BlockSpec((GDN_SUB, HEADS, GDN_CHUNK, GDN_CHUNK), lambda n: (order(n), 0, 0, 0)),
        half=pl.BlockSpec((c, GDN_W), lambda n: (order(n), 1)),
        any=pl.BlockSpec(memory_space=pl.ANY),
    )


def _gdn_fwd(proj, psmall, conv_w, al, dt, gain, y_in, comm, *, name):
    t = proj.shape[0]
    c = GDN_CHUNK
    nc, ns = t // c, t // (c * GDN_SUB)
    sp = _gdn_specs(c * GDN_SUB, lambda n: n)

    def body(p_ref, prev_ref, ps_ref, cw_ref, al_ref, dt_ref, gain_ref, yin_ref, *rest):
        comm_in, (y_ref, s_ref, inv_ref), comm_out, (state,), comm_sems = comm.split(rest, n_out=3, n_scratch=1)
        n = pl.program_id(0)

        @pl.when(n == 0)
        def _():
            state[...] = jnp.zeros_like(state)
            comm.start(comm_in, comm_out, comm_sems)

        p8 = jnp.where(n == 0, 0.0, prev_ref[:, :GDN_CONV])
        pre, _ = _conv4(p_ref[:, :GDN_CONV], p8, cw_ref[...])
        act = pre * _sigmoid(pre)
        beta_all, g_all = _gdn_gates(ps_ref[...], al_ref[...], dt_ref[...])
        gd_all, gain = p_ref[:, GDN_CONV:], gain_ref[...]
        swish = gd_all * _sigmoid(gd_all)
        cur = [state[h] for h in range(HEADS)]
        starts, ys = [], []
        for sub in range(GDN_SUB):
            rows = slice(sub * c, (sub + 1) * c)
            starts.append(list(cur))
            pieces = []
            for h in range(HEADS):
                sl = lambda j: slice(j * GDN_W + h * HEAD_DIM, j * GDN_W + (h + 1) * HEAD_DIM)
                z = _gdn_chunk(act[rows, sl(0)], act[rows, sl(1)], act[rows, sl(2)], beta_all[rows, h:h + 1],
                               g_all[rows, HEADS + h:HEADS + h + 1], cur[h])
                cur[h] = z.s1
                inv_ref[sub, h] = z.inv
                rinv = lax.rsqrt(jnp.mean(z.o * z.o, axis=-1, keepdims=True) + EPS)
                pieces.append(z.o * rinv * gain * swish[rows, sl(0)])
            ys.append(jnp.concatenate(pieces, axis=1))
        y_ref[...] = jnp.concatenate(ys, axis=0).astype(BF16)
        for sub in range(GDN_SUB):
            for h in range(HEADS):
                s_ref[sub, h] = starts[sub][h]
        for h in range(HEADS):
            state[h] = cur[h]

        @pl.when(n == ns - 1)
        def _():
            comm.finish(comm_in, comm_out, comm_sems)

    outs = pl.pallas_call(
        body, grid=(ns,),
        in_specs=[sp["proj"], sp["prev"], sp["small"], sp["convw"], sp["vec"], sp["vec"], sp["vec"], sp["any"]]
        + [sp["any"]] * len(comm.arrays),
        out_specs=[sp["half"], sp["state"], sp["inv"]] + [sp["any"]] * len(comm.out_shapes),
        out_shape=[jax.ShapeDtypeStruct((t, 2 * GDN_W), BF16), jax.ShapeDtypeStruct((nc, HEADS, HEAD_DIM, HEAD_DIM), F32),
                   jax.ShapeDtypeStruct((nc, HEADS, c, c), F32)] + comm.out_shapes,
        scratch_shapes=[pltpu.VMEM((HEADS, HEAD_DIM, HEAD_DIM), F32)] + comm.scratch, name=name,
        input_output_aliases={7: 0}, compiler_params=_cp(("arbitrary",), 48),
    )(proj, proj, psmall, conv_w, al, dt, gain, y_in, *comm.arrays)
    return outs[0], (outs[1], outs[2]), outs[3:]


def _gdn_bwd(proj, psmall, conv_w, al, dt, gain, states, dy, dproj_in, comm, *, name):
    t = proj.shape[0]
    c = GDN_CHUNK
    nc, ns = t // c, t // (c * GDN_SUB)
    sp = _gdn_specs(c * GDN_SUB, lambda n: ns - 1 - n)

    def body(p_ref, prev_ref, ps_ref, cw_ref, al_ref, dt_ref, gain_ref, s_ref, inv_ref, dy_ref, dpin_ref, *rest):
        comm_in, outs, comm_out, (dstate, head), comm_sems = comm.split(rest, n_out=6, n_scratch=2)
        dp_ref, dps_ref, dcw_ref, dal_ref, ddt_ref, dgain_ref = outs
        n = pl.program_id(0)
        first_rows = n == ns - 1

        @pl.when(n == 0)
        def _():
            for ref in (dstate, head, dcw_ref, dal_ref, ddt_ref, dgain_ref):
                ref[...] = jnp.zeros_like(ref)
            comm.start(comm_in, comm_out, comm_sems)

        x, cwv = p_ref[:, :GDN_CONV], cw_ref[...]
        p8 = jnp.where(first_rows, 0.0, prev_ref[:, :GDN_CONV])
        pre, shifts = _conv4(x, p8, cwv)
        sg_pre = _sigmoid(pre)
        act = pre * sg_pre
        ps, alv, dtv, gain = ps_ref[...], al_ref[...], dt_ref[...], gain_ref[...]
        beta_all, g_all = _gdn_gates(ps, alv, dtv)
        lane = _cols((c, LANES))
        dgain = jnp.zeros((1, LANES), F32)
        dcur = [dstate[h] for h in range(HEADS)]
        dact_rows, dbeta_rows, dg_rows = [None] * GDN_SUB, [None] * GDN_SUB, [None] * GDN_SUB
        for sub in reversed(range(GDN_SUB)):
            rows = slice(sub * c, (sub + 1) * c)
            dbeta_all = jnp.zeros((c, LANES), F32)
            dg_all = jnp.zeros((c, LANES), F32)
            dact = [None] * (3 * HEADS)
            for h in range(HEADS):
                sl = lambda j: slice(j * GDN_W + h * HEAD_DIM, j * GDN_W + (h + 1) * HEAD_DIM)
                s0 = s_ref[sub, h]
                z = _gdn_chunk(act[rows, sl(0)], act[rows, sl(1)], act[rows, sl(2)], beta_all[rows, h:h + 1],
                               g_all[rows, HEADS + h:HEADS + h + 1], s0, inv=inv_ref[sub, h])
                rinv = lax.rsqrt(jnp.mean(z.o * z.o, axis=-1, keepdims=True) + EPS)
                oh = z.o * rinv
                gd = p_ref[rows, sl(3)]
                sgd = _sigmoid(gd)
                dyh = dy_ref[rows, sl(0)]
                dgain = dgain + jnp.sum(dyh * oh * (gd * sgd), axis=0, keepdims=True)
                dp_ref[rows, sl(3)] = (dyh * oh * gain * sgd * (1.0 + gd * (1.0 - sgd))).astype(BF16)
                dyo = dyh * gain * (gd * sgd)
                do = rinv * (dyo - oh * jnp.mean(dyo * oh, axis=-1, keepdims=True))
                dqc, dkc, dv, dbeta, dg, dcur[h] = _gdn_chunk_bwd(z, s0, do, dcur[h])
                dact[h], dact[HEADS + h], dact[2 * HEADS + h] = dqc, dkc, dv
                dbeta_all = dbeta_all + jnp.where(lane == h, dbeta, 0.0)
                dg_all = dg_all + jnp.where(lane == HEADS + h, dg, 0.0)
            dact_rows[sub], dbeta_rows[sub], dg_rows[sub] = jnp.concatenate(dact, axis=1), dbeta_all, dg_all
        for h in range(HEADS):
            dstate[h] = dcur[h]
        dbeta_all, dg_all = jnp.concatenate(dbeta_rows, axis=0), jnp.concatenate(dg_rows, axis=0)
        dpre = jnp.concatenate(dact_rows, axis=0) * sg_pre * (1.0 + pre * (1.0 - sg_pre))
        dx, dws = _conv4_bwd(dpre, head[...], x, shifts, cwv)
        dp_ref[:, :GDN_CONV] = dx.astype(BF16)
        for k in range(4):
            dcw_ref[k:k + 1, :] += dws[k]
        head[...] = dpre[:SUBLANES]
        dsp = dg_all * (-jnp.exp(alv)) * _sigmoid(ps + dtv)
        dps_ref[...] = (dbeta_all * beta_all * (1.0 - beta_all) + dsp).astype(BF16)
        ddt_ref[...] += jnp.sum(dsp, axis=0, keepdims=True)
        dal_ref[...] += jnp.sum(dg_all * g_all, axis=0, keepdims=True)
        dgain_ref[...] += dgain

        @pl.when(n == ns - 1)
        def _():
            comm.finish(comm_in, comm_out, comm_sems)

    vec_f32 = jax.ShapeDtypeStruct((1, LANES), F32)
    outs = pl.pallas_call(
        body, grid=(ns,),
        in_specs=[sp["proj"], sp["prev"], sp["small"], sp["convw"], sp["vec"], sp["vec"], sp["vec"], sp["state"], sp["inv"],
                  sp["half"], sp["any"]]
        + [sp["any"]] * len(comm.arrays),
        out_specs=[sp["proj"], sp["small"], sp["convw"], sp["vec"], sp["vec"], sp["vec"]] + [sp["any"]] * len(comm.out_shapes),
        out_shape=[jax.ShapeDtypeStruct((t, 8 * GDN_W), BF16), jax.ShapeDtypeStruct((t, LANES), BF16),
                   jax.ShapeDtypeStruct((4, GDN_CONV), F32), vec_f32, vec_f32, vec_f32] + comm.out_shapes,
        scratch_shapes=[pltpu.VMEM((HEADS, HEAD_DIM, HEAD_DIM), F32), pltpu.VMEM((SUBLANES, GDN_CONV), F32)] + comm.scratch,
        name=name, input_output_aliases={10: 0}, compiler_params=_cp(("arbitrary",), 48),
    )(proj, proj, psmall, conv_w, al, dt, gain, *states, dy, dproj_in, *comm.arrays)
    return outs[:6], outs[6:]


def _here():
    x, y, c = lax.axis_index("x"), lax.axis_index("y"), lax.axis_index("c")
    return x, y, c, [(1 - x, y), (x, 1 - y), (1 - x, 1 - y)]


def _rdma(src, dst, send, recv, k, dev):
    return pltpu.make_async_remote_copy(src_ref=src, dst_ref=dst, send_sem=send.at[k], recv_sem=recv.at[k],
                                        device_id=dev, device_id_type=MESH)


def _dma_sems(n):
    return [pltpu.SemaphoreType.DMA((n,)), pltpu.SemaphoreType.DMA((n,)), pltpu.SemaphoreType.DMA((1,))]


COPY_PIECES = 4
COPY_PIECE_ALIGN = 16


def _row_parts(rows):
    n = COPY_PIECES if rows % (COPY_PIECES * COPY_PIECE_ALIGN) == 0 and rows >= 1024 else 1
    return [pl.ds(q * (rows // n), rows // n) for q in range(n)]


class _AllGather:
    def __init__(self, array):
        self.arrays = [array]
        self.out_shapes = [jax.ShapeDtypeStruct((N_DEV,) + array.shape, array.dtype)]
        self.parts = _row_parts(array.shape[0])
        self.scratch = _dma_sems(7 * len(self.parts))

    def start(self, ins, outs, sems):
        (src,), (out,), (send, recv, loc) = ins, outs, sems
        x, y, c, chips = _here()
        me, n = 4 * x + 2 * y + c, len(self.parts)
        pltpu.make_async_copy(src, out.at[me], loc.at[0]).start()
        for q, part in enumerate(self.parts):
            _rdma(src.at[part], out.at[me, part], send, recv, q, (x, y, 1 - c)).start()
            for j, (cx, cy) in enumerate(chips):
                _rdma(src.at[part], out.at[me, part], send, recv, (1 + j) * n + q, (cx, cy, c)).start()

    def finish(self, ins, outs, sems):
        (src,), (out,), (send, recv, loc) = ins, outs, sems
        x, y, c, chips = _here()
        sibling, me, n = (x, y, 1 - c), 4 * x + 2 * y + c, len(self.parts)
        piece = lambda k, q: _rdma(src.at[self.parts[q]], out.at[me, self.parts[q]], send, recv, k * n + q, sibling)
        for j, (cx, cy) in enumerate(chips):
            for q, part in enumerate(self.parts):
                got = out.at[4 * cx + 2 * cy + c, part]
                piece(1 + j, q).wait_recv()
                _rdma(got, got, send, recv, (4 + j) * n + q, sibling).start()
        for k in (0, 4, 5, 6):
            for q in range(n):
                piece(k, q).wait_recv()
        for k in range(7):
            for q in range(n):
                piece(k, q).wait_send()
        pltpu.make_async_copy(src, out.at[me], loc.at[0]).wait()


class _ChipExchange:
    def __init__(self, array):
        self.arrays = [array]
        self.out_shapes = [jax.ShapeDtypeStruct(array.shape, array.dtype)]
        self.parts = _row_parts(array.shape[1])
        self.scratch = _dma_sems(3 * len(self.parts))

    def _copies(self, ins, outs, sems):
        (src,), (out,), (send, recv, loc) = ins, outs, sems
        x, y, c, chips = _here()
        here, n = 2 * x + y, len(self.parts)
        local = pltpu.make_async_copy(src.at[here], out.at[here], loc.at[0])
        return local, [_rdma(src.at[2 * cx + cy, part], out.at[here, part], send, recv, j * n + q, (cx, cy, c))
                       for j, (cx, cy) in enumerate(chips) for q, part in enumerate(self.parts)]

    def start(self, ins, outs, sems):
        local, remote = self._copies(ins, outs, sems)
        local.start()
        for cp in remote:
            cp.start()

    def finish(self, ins, outs, sems):
        local, remote = self._copies(ins, outs, sems)
        for cp in remote:
            cp.wait()
        local.wait()


class _PairSwap:
    def __init__(self, array):
        self.arrays = [array]
        self.out_shapes = [jax.ShapeDtypeStruct(array.shape[1:], array.dtype)]
        self.parts = _row_parts(array.shape[2])
        self.scratch = _dma_sems(4 * len(self.parts))

    def _copies(self, ins, outs, sems):
        (src,), (theirs,), (send, recv, _) = ins, outs, sems
        x, y, c, _ = _here()
        return [_rdma(src.at[1 - c, p, part], theirs.at[p, part], send, recv, p * len(self.parts) + q, (x, y, 1 - c))
                for p in range(4) for q, part in enumerate(self.parts)]

    def start(self, ins, outs, sems):
        for cp in self._copies(ins, outs, sems):
            cp.start()

    def finish(self, ins, outs, sems):
        for cp in self._copies(ins, outs, sems):
            cp.wait()


class _Comm:
    def __init__(self, ops):
        self.ops = ops
        self.arrays = [a for op in ops for a in op.arrays]
        self.out_shapes = [s for op in ops for s in op.out_shapes]
        self.scratch = [s for op in ops for s in op.scratch]

    def split(self, rest, n_out, n_scratch):
        cuts = np.cumsum([0, len(self.arrays), n_out, len(self.out_shapes), n_scratch, len(self.scratch)])
        assert cuts[-1] == len(rest)
        return tuple(rest[a:b] for a, b in zip(cuts[:-1], cuts[1:]))

    def _each(self, method, ins, outs, sems):
        i = o = s = 0
        for op in self.ops:
            ni, no, ns = len(op.arrays), len(op.out_shapes), len(op.scratch)
            getattr(op, method)(ins[i:i + ni], outs[o:o + no], sems[s:s + ns])
            i, o, s = i + ni, o + no, s + ns

    def start(self, ins, outs, sems):
        self._each("start", ins, outs, sems)

    def finish(self, ins, outs, sems):
        self._each("finish", ins, outs, sems)

    def run(self, name):
        def body(*refs):
            ins, _, outs, _, sems = self.split(refs, 0, 0)
            self.start(ins, outs, sems)
            self.finish(ins, outs, sems)

        hbm = pl.BlockSpec(memory_space=pl.ANY)
        return pl.pallas_call(body, in_specs=[hbm] * len(self.arrays), out_specs=[hbm] * len(self.out_shapes),
                              out_shape=self.out_shapes, scratch_shapes=self.scratch, name=name)(*self.arrays)


def _sum_slots(x, *, name, tr=None):
    n, r, l = x.shape
    tr = r if tr is None else tr

    def body(x_ref, o_ref):
        acc = x_ref[0].astype(F32)
        for s in range(1, n):
            acc = acc + x_ref[s].astype(F32)
        o_ref[...] = acc

    return pl.pallas_call(
        body, grid=(r // tr,), in_specs=[pl.BlockSpec((n, tr, l), lambda i: (0, i, 0))],
        out_specs=pl.BlockSpec((tr, l), lambda i: (i, 0)), out_shape=jax.ShapeDtypeStruct((r, l), F32),
        name=name, compiler_params=_cp(("parallel",), 48),
    )(x)


def _pair_add(both, theirs, *, name, tr):
    _, n, r, l = both.shape

    def body(a_ref, b_ref, o_ref):
        mine = jnp.where(lax.axis_index("c") == 0, a_ref[0], a_ref[1])
        o_ref[...] = (mine.astype(F32) + b_ref[...].astype(F32)).astype(BF16)

    spec = pl.BlockSpec((n, tr, l), lambda i: (0, i, 0))
    return pl.pallas_call(body, grid=(r // tr,), in_specs=[pl.BlockSpec((2, n, tr, l), lambda i: (0, 0, i, 0)), spec], out_specs=spec,
                          out_shape=jax.ShapeDtypeStruct(theirs.shape, BF16), name=name,
                          compiler_params=_cp(("parallel",), 48))(both, theirs)


ADAM_TILE_ELEMS = 512 * 1024


def _adam(w, g, m, v, *, name):
    shape = w.shape
    cols = shape[-1]
    rows = math.prod(shape[:-1]) if len(shape) > 1 else 1
    tr = rows
    if rows * cols > ADAM_TILE_ELEMS:
        tr = max(d for d in range(SUBLANES, ADAM_TILE_ELEMS // cols + 1, SUBLANES) if rows % d == 0)
    c1, c2 = 1.0 - ADAM_B1 ** ADAM_STEP, 1.0 - ADAM_B2 ** ADAM_STEP

    def body(w_ref, g_ref, m_ref, v_ref, d_ref, m2_ref, v2_ref):
        gv = g_ref[...]
        m2 = ADAM_B1 * m_ref[...] + (1.0 - ADAM_B1) * gv
        v2 = ADAM_B2 * v_ref[...] + (1.0 - ADAM_B2) * (gv * gv)
        d_ref[...] = -ADAM_LR * ((m2 / c1) / (jnp.sqrt(v2 / c2) + ADAM_EPS) + ADAM_WD * w_ref[...])
        m2_ref[...] = m2
        v2_ref[...] = v2

    spec = pl.BlockSpec((tr, cols), lambda i: (i, 0))
    outs = pl.pallas_call(
        body, grid=(rows // tr,), in_specs=[spec] * 4, out_specs=[spec] * 3,
        out_shape=[jax.ShapeDtypeStruct((rows, cols), F32)] * 3, name=name, compiler_params=_cp(("parallel",), 48),
    )(*(a.reshape(rows, cols) for a in (w, g, m, v)))
    return tuple(o.reshape(shape) for o in outs)


WEIGHTS = ['norm_mix', 'norm_ffn', 'ret_gdn_w_in', 'gdn_conv_w', 'gdn_a_log', 'gdn_dt_bias', 'gdn_out_gain', 'ret_gdn_w_out',
           'lru_w_in', 'lru_conv_w', 'lru_conv_b', 'lru_w_a', 'lru_b_a', 'lru_w_x', 'lru_b_x', 'lru_lambda', 'lru_w_out',
           'ffn_w_up', 'ffn_conv_w', 'ffn_conv_b', 'ffn_w_down', 'norm_final']
SMALL = {'gdn_conv_w': ((1, 4, 192), 2), 'lru_conv_w': ((1, 4, 128), 2), 'lru_conv_b': ((1, 128), 1), 'lru_b_a': ((1, 128), 1),
         'lru_b_x': ((1, 128), 1), 'lru_lambda': ((1, 128), 1), 'ffn_conv_w': ((2, 3, 704), 2)}
EARLY = {'lru_conv_w': (1, 4, 1024), 'lru_conv_b': (1, 1024), 'lru_b_a': (1, 1024), 'lru_b_x': (1, 1024), 'lru_lambda': (1, 1024),
         'ffn_conv_w': (2, 3, 5632), 'norm_ffn': (2, 1024), 'norm_mix1': (1, 1024), 'lru_w_a': (1, 8, 128, 128),
         'lru_w_x': (1, 8, 128, 128), 'ffn_conv_b': (2, 5632), 'norm_final': (1024,)}
LATE = {'gdn_conv_w': (1, 4, 1536), 'norm_mix0': (1, 1024), 'gdn_a_log': (1, 4), 'gdn_dt_bias': (1, 4), 'gdn_out_gain': (1, 128),
        'loss': (1, 1)}


def _rows_of(n_elems):
    return -(-n_elems // LANES)


def _to_rows(a, lead=()):
    flat = a.reshape(lead + (-1,))
    pad = _rows_of(flat.shape[-1]) * LANES - flat.shape[-1]
    if pad:
        flat = jnp.pad(flat, [(0, 0)] * len(lead) + [(0, pad)])
    return flat.reshape(lead + (-1, LANES))


def _pack(pieces, total_rows, lead=()):
    buf = jnp.concatenate(pieces, axis=len(lead))
    pad = total_rows - buf.shape[len(lead)]
    return jnp.pad(buf, [(0, 0)] * len(lead) + [(0, pad), (0, 0)]) if pad else buf


def _unpack(buf, shapes, lead=()):
    out, off = [], 0
    for shape in shapes:
        n = math.prod(shape)
        rows = _rows_of(n)
        piece = lax.slice_in_dim(buf, off, off + rows, axis=len(lead)).reshape(lead + (rows * LANES,))
        out.append(lax.slice_in_dim(piece, 0, n, axis=len(lead)).reshape(lead + shape))
        off += rows
    return out


def _join_blocks(g, axis):
    m = jnp.moveaxis(g, 0, axis)
    return m.reshape(m.shape[:axis] + (N_DEV * m.shape[axis + 1],) + m.shape[axis + 2:])


def _split_blocks(full, axis):
    s = full.shape
    return jnp.moveaxis(full.reshape(s[:axis] + (N_DEV, s[axis] // N_DEV) + s[axis + 1:]), axis, 0)


def _small_rows(shapes):
    total = sum(_rows_of(math.prod(s)) for s in shapes)
    return -(-total // SUBLANES) * SUBLANES


FFN_TM = 1024
FFN_TN = 512


def _ffn_forward(h, gain, w_up, cw, cb, w_down, tag, comm):
    t, d = h.shape
    tm, tn, blk, nb = min(FFN_TM, t), FFN_TN, FFN_BLK, FFN_NB
    hn = _norm_fwd(h, gain, name=f"ffn{tag}_norm")
    up = _mmx(hn, w_up, dims=NN, grid=(t // tm, 2 * nb, 1), name=f"ffn{tag}_up", tile=(tm, blk),
              a_spec=pl.BlockSpec((tm, d), lambda i, j, k: (i, 0)),
              b_spec=pl.BlockSpec((None, d, blk), lambda i, j, k: (j, 0, 0)),
              o_spec=pl.BlockSpec((None, None, tm, blk), lambda i, j, k: (j // nb, j % nb, i, 0)),
              out_shape=jax.ShapeDtypeStruct((2, nb, t, blk), F32))
    act, comm_out = _ffn_act_fwd(up, cw, cb, comm, name=f"ffn{tag}_act")
    out = _mmx(act, w_down.reshape(nb, blk, d), dims=NN, grid=(t // tm, d // tn, nb // 2), name=f"ffn{tag}_down", tile=(tm, tn),
               res=h, pairs=True,
               a_spec=pl.BlockSpec((2, tm, blk), lambda i, j, k: (k, i, 0)),
               b_spec=pl.BlockSpec((2, blk, tn), lambda i, j, k: (k, 0, j)),
               o_spec=pl.BlockSpec((tm, tn), lambda i, j, k: (i, j)),
               out_shape=jax.ShapeDtypeStruct((t, d), F32))
    return out, (hn, up, act), comm_out


def _ffn_backward(dh, h, gain, saved, w_up, cw, cb, w_down, tag):
    hn, up, act = saved
    t, d = h.shape
    tm, tn, blk, nb = min(FFN_TM, t), FFN_TN, FFN_BLK, FFN_NB
    tk = t
    da = _mmx(dh, w_down, dims=NT, grid=(t // tm, nb, 1), name=f"ffn{tag}_d_act", tile=(tm, blk),
              a_spec=pl.BlockSpec((tm, d), lambda i, j, k: (i, 0)),
              b_spec=pl.BlockSpec((blk, d), lambda i, j, k: (j, 0)),
              o_spec=pl.BlockSpec((None, tm, blk), lambda i, j, k: (j, i, 0)),
              out_shape=jax.ShapeDtypeStruct((nb, t, blk), F32))
    dwd = _mmx(act, dh, dims=TN, grid=(nb, d // tn, t // tk), name=f"ffn{tag}_d_wdown", tile=(blk, tn), split_rows=blk // 2,
               a_spec=pl.BlockSpec((None, tk, blk), lambda i, j, k: (i, k, 0)),
               b_spec=pl.BlockSpec((tk, tn), lambda i, j, k: (k, j)),
               o_spec=pl.BlockSpec((2, None, blk // 2, tn), lambda i, j, k: (0, i, 0, j)),
               out_shape=jax.ShapeDtypeStruct((2, nb, blk // 2, d), BF16))
    dup, dcw, dcb = _ffn_act_bwd(up, da, cw, cb, name=f"ffn{tag}_act_bwd")
    half = nb // 2
    dhn = _mmx(dup, w_up, dims=NT, grid=(t // tm, d // tn, nb), name=f"ffn{tag}_d_hn", tile=(tm, tn), pairs=True,
               a_spec=pl.BlockSpec((None, 2, tm, blk), lambda i, j, k: (k // half, k % half, i, 0)),
               b_spec=pl.BlockSpec((2, tn, blk), lambda i, j, k: (k, j, 0)),
               o_spec=pl.BlockSpec((tm, tn), lambda i, j, k: (i, j)),
               out_shape=jax.ShapeDtypeStruct((t, d), F32))
    dwu = _mmx(hn, dup, dims=TN, grid=(1, 2 * nb, t // tk), name=f"ffn{tag}_d_wup", tile=(d, blk),
               a_spec=pl.BlockSpec((tk, d), lambda i, j, k: (k, 0)),
               b_spec=pl.BlockSpec((None, None, tk, blk), lambda i, j, k: (j // nb, j % nb, k, 0)),
               o_spec=pl.BlockSpec((None, None, d, blk), lambda i, j, k: (j % 2, j // 2, 0, 0)),
               out_shape=jax.ShapeDtypeStruct((2, N_DEV // 2, d, blk), BF16))
    dh_in, dgain = _norm_bwd(h, gain, dhn, dh, name=f"ffn{tag}_norm_bwd")
    conv_w = dcw.transpose(2, 0, 1, 3).reshape(3, 2 * nb * blk)
    return dh_in, dict(w_up=dwu, w_down=dwd, conv_w=conv_w, conv_b=dcb.reshape(1, 2 * nb * blk), norm=dgain)


def kernel(x, norm_mix, norm_ffn, ret_gdn_w_in, gdn_conv_w, gdn_a_log, gdn_dt_bias, gdn_out_gain, ret_gdn_w_out, lru_w_in, lru_conv_w, lru_conv_b, lru_w_a, lru_b_a, lru_w_x, lru_b_x, lru_lambda, lru_w_out, ffn_w_up, ffn_conv_w, ffn_conv_b, ffn_w_down, norm_final, loss_target, m_norm_mix, m_norm_ffn, m_ret_gdn_w_in, m_gdn_conv_w, m_gdn_a_log, m_gdn_dt_bias, m_gdn_out_gain, m_ret_gdn_w_out, m_lru_w_in, m_lru_conv_w, m_lru_conv_b, m_lru_w_a, m_lru_b_a, m_lru_w_x, m_lru_b_x, m_lru_lambda, m_lru_w_out, m_ffn_w_up, m_ffn_conv_w, m_ffn_conv_b, m_ffn_w_down, m_norm_final, v_norm_mix, v_norm_ffn, v_ret_gdn_w_in, v_gdn_conv_w, v_gdn_a_log, v_gdn_dt_bias, v_gdn_out_gain, v_ret_gdn_w_out, v_lru_w_in, v_lru_conv_w, v_lru_conv_b, v_lru_w_a, v_lru_b_a, v_lru_w_x, v_lru_b_x, v_lru_lambda, v_lru_w_out, v_ffn_w_up, v_ffn_conv_w, v_ffn_conv_b, v_ffn_w_down, v_norm_final):
    given = dict(locals())
    w = {n: given[n] for n in WEIGHTS}
    me = 4 * lax.axis_index("x") + 2 * lax.axis_index("y") + lax.axis_index("c")
    t = x.shape[1]

    first_shards = {'w_in0': ret_gdn_w_in[0]}
    rest_shards = {'w_out0': ret_gdn_w_out[0], 'lru_in': lru_w_in[0], 'lru_out': lru_w_out[0], 'up0': ffn_w_up[0], 'up1': ffn_w_up[1],
                   'down0': ffn_w_down[0], 'down1': ffn_w_down[1]}
    small_shapes = [s for s, _ in SMALL.values()]
    small_buf = _pack([_to_rows(w[n]) for n in SMALL], _small_rows(small_shapes))
    *g_first, g_small = _Comm([_AllGather(a.astype(BF16)) for a in first_shards.values()] + [_AllGather(small_buf)]).run("gather_first")
    hosted = {'retention_fwd': ['w_out0', 'lru_out'], 'deltanet_fwd': ['lru_in', 'up0', 'down0', 'up1'], 'ffn0_act': ['down1'], 'rglru_fwd': []}
    gather_in = {host: _Comm([_AllGather(rest_shards[n].astype(BF16)) for n in names]) for host, names in hosted.items()}
    got = dict(zip(first_shards, g_first))
    small_blocks = dict(zip(SMALL, _unpack(g_small, small_shapes, lead=(N_DEV,))))
    full = {n: _join_blocks(small_blocks[n], SMALL[n][1]) for n in SMALL if n != 'ffn_conv_w'}

    w_main, w_narrow = _join_w_in(got['w_in0'], name="join_w_in")
    fcw = [small_blocks['ffn_conv_w'][:, l].reshape(2, FFN_NB, 3, FFN_BLK) for l in range(2)]
    fcb = [ffn_conv_b[l].reshape(2, FFN_NB, 1, FFN_BLK) for l in range(2)]
    gdn_cw = full['gdn_conv_w'][0]
    al_pad = jnp.pad(gdn_a_log, ((0, 0), (HEADS, LANES - 2 * HEADS)))
    dt_pad = jnp.pad(gdn_dt_bias, ((0, 0), (HEADS, LANES - 2 * HEADS)))
    lru_cw, lru_cb = full['lru_conv_w'][0], full['lru_conv_b']
    lru_ba, lru_bx, lru_lam = full['lru_b_a'], full['lru_b_x'], full['lru_lambda']
    wa, wx = lru_w_a[0], lru_w_x[0]

    h0, target = x[0], loss_target[0]
    hn0 = _norm_fwd(h0, norm_mix[0:1], name="mix0_norm")
    proj = _mm(hn0, w_main, name="mix0_in")
    pnarrow = _mm(hn0, w_narrow, name="mix0_in_narrow")
    tables = _ret_tables(t)
    y0, ret_states, g = _ret_fwd(proj, tables, gather_in['retention_fwd'], name="retention_fwd")
    got.update(zip(hosted['retention_fwd'], g))
    y0, gdn_states, g = _gdn_fwd(proj, pnarrow, gdn_cw, al_pad, dt_pad, gdn_out_gain, y0, gather_in['deltanet_fwd'], name="deltanet_fwd")
    got.update(zip(hosted['deltanet_fwd'], g))
    lru_in = _join_blocks(got['lru_in'], 1)
    lru_in_g, lru_in_x = lru_in[:, :D_MODEL], lru_in[:, D_MODEL:]
    lru_out = got['lru_out'].reshape(D_MODEL, D_MODEL)
    w_out0 = got['w_out0'].reshape(D_MODEL, D_MODEL)
    h1 = _mm(y0, w_out0, res=h0, name="mix0_out")
    h2, ffn0_saved, g = _ffn_forward(h1, norm_ffn[0:1], got['up0'], fcw[0], fcb[0], got['down0'].reshape(D_FF, D_MODEL), 0,
                                     gather_in['ffn0_act'])
    got.update(zip(hosted['ffn0_act'], g))
    hn1 = _norm_fwd(h2, norm_mix[1:2], name="mix1_norm")
    gate = _mm(hn1, lru_in_g, name="mix1_in_gate")
    xpre = _mm(hn1, lru_in_x, name="mix1_in_x")
    y1, hs, g = _lru_fwd(gate, xpre, lru_cw, lru_cb, wa, lru_ba, wx, lru_bx, lru_lam, gather_in['rglru_fwd'], name="rglru_fwd")
    got.update(zip(hosted['rglru_fwd'], g))
    h3 = _mm(y1, lru_out, res=h2, name="mix1_out")
    w_up = [got['up0'], got['up1']]
    down = [got['down0'].reshape(D_FF, D_MODEL), got['down1'].reshape(D_FF, D_MODEL)]
    h4, ffn1_saved, _ = _ffn_forward(h3, norm_ffn[1:2], w_up[1], fcw[1], fcb[1], down[1], 1, _Comm([]))
    dh4, d_norm_final, loss_part = _final_loss(h4, norm_final[None, :], target, name="final_norm_loss")

    dh3, gf1 = _ffn_backward(dh4, h3, norm_ffn[1:2], ffn1_saved, w_up[1], fcw[1], fcb[1], down[1], 1)
    dy1 = _mm(dh3, lru_out, tb=True, name="mix1_d_y")
    d_lru_out = _mm(y1, dh3, ta=True, out_dtype=BF16, name="mix1_d_wout")
    dgate, dxpre, d_lcw, d_lcb, d_wa, d_ba, d_wx, d_bx, d_lam = _lru_bwd(
        gate, xpre, hs, dy1, lru_cw, lru_cb, wa, lru_ba, wx, lru_bx, lru_lam, name="rglru_bwd")
    dhn1 = _mm(dgate, lru_in_g, tb=True, name="mix1_d_hn_gate")
    dhn1 = _mm(dxpre, lru_in_x, tb=True, res=dhn1, name="mix1_d_hn_x")
    d_lru_in = jnp.concatenate([_mm(hn1, dgate, ta=True, out_dtype=BF16, name="mix1_d_win_gate"),
                                _mm(hn1, dxpre, ta=True, out_dtype=BF16, name="mix1_d_win_x")], axis=1)
    dh2, d_mix1 = _norm_bwd(h2, norm_mix[1:2], dhn1, dh3, name="mix1_norm_bwd")
    dh1, gf0 = _ffn_backward(dh2, h1, norm_ffn[0:1], ffn0_saved, w_up[0], fcw[0], fcb[0], down[0], 0)
    dy0 = _mm(dh1, w_out0, tb=True, name="mix0_d_y")
    d_w_out0 = _mm(y0, dh1, ta=True, out_dtype=BF16, name="mix0_d_wout")

    def by_core_chip(full_grad, axis):
        blocks = _split_blocks(full_grad, axis)
        return blocks.reshape((4, 2) + blocks.shape[1:]).transpose(1, 0, 2, 3)

    def pair_add(blocks, theirs):
        return {k: _pair_add(b, o, name=f"pair_add_{k}", tr=row_tile(b.shape[2])) for (k, b), o in zip(blocks.items(), theirs)}

    row_tile = lambda rows: rows if rows <= 512 else 256

    rest_blocks = {'w_out0': by_core_chip(d_w_out0, 0), 'lru_in': by_core_chip(d_lru_in, 1), 'lru_out': by_core_chip(d_lru_out, 0),
                   'up0': gf0['w_up'], 'up1': gf1['w_up'], 'down0': gf0['w_down'], 'down1': gf1['w_down']}
    early = {'lru_conv_w': d_lcw[None], 'lru_conv_b': d_lcb, 'lru_b_a': d_ba, 'lru_b_x': d_bx, 'lru_lambda': d_lam,
             'ffn_conv_w': jnp.stack([gf0['conv_w'], gf1['conv_w']]), 'norm_ffn': jnp.concatenate([gf0['norm'], gf1['norm']], axis=0),
             'norm_mix1': d_mix1, 'lru_w_a': d_wa[None], 'lru_w_x': d_wx[None],
             'ffn_conv_b': jnp.concatenate([gf0['conv_b'], gf1['conv_b']], axis=0), 'norm_final': d_norm_final[0]}
    early_buf = _pack([_to_rows(early[n]) for n in EARLY], _small_rows(list(EARLY.values())))
    dproj, (*theirs, got_early) = _ret_bwd(proj, tables, ret_states, dy0,
                                           _Comm([_PairSwap(b) for b in rest_blocks.values()] + [_AllGather(early_buf)]), name="retention_bwd")
    z_rest = pair_add(rest_blocks, theirs)
    (dproj, dnarrow, d_gcw, d_alog, d_dtb, d_gain), w_rest = _gdn_bwd(
        proj, pnarrow, gdn_cw, al_pad, dt_pad, gdn_out_gain, gdn_states, dy0, dproj,
        _Comm([_ChipExchange(z) for z in z_rest.values()]), name="deltanet_bwd")
    dhn0 = _mm(dproj, w_main, tb=True, name="mix0_d_hn")
    dhn0 = _mm(dnarrow, w_narrow, tb=True, res=dhn0, name="mix0_d_hn_narrow")
    d_w_main = _mm(hn0, dproj, ta=True, out_dtype=BF16, name="mix0_d_win")
    d_w_narrow = _mm(hn0, dnarrow, ta=True, out_dtype=BF16, name="mix0_d_win_narrow")
    dx, d_mix0 = _norm_bwd(h0, norm_mix[0:1], dhn0, dh1, name="mix0_norm_bwd")

    first_blocks = {'w_in0': _split_w_in(d_w_main, d_w_narrow, name="split_d_w_in")}
    z_first = pair_add(first_blocks, _Comm([_PairSwap(b) for b in first_blocks.values()]).run("pair_swap_first"))
    late = {'gdn_conv_w': d_gcw[None], 'norm_mix0': d_mix0, 'gdn_a_log': d_alog[:, HEADS:2 * HEADS],
            'gdn_dt_bias': d_dtb[:, HEADS:2 * HEADS], 'gdn_out_gain': d_gain, 'loss': loss_part}
    late_buf = _pack([_to_rows(late[n]) for n in LATE], _small_rows(list(LATE.values())))
    *w_first, got_late = _Comm([_ChipExchange(z) for z in z_first.values()] + [_AllGather(late_buf)]).run("exchange_first")

    summed = {k: _sum_slots(blocks, name=f"sum_blocks_{k}", tr=row_tile(blocks.shape[1]))
              for k, blocks in list(zip(z_rest, w_rest)) + list(zip(z_first, w_first))}
    grads = {'ret_gdn_w_in': summed['w_in0'][None], 'ret_gdn_w_out': summed['w_out0'][None], 'lru_w_in': summed['lru_in'][None],
             'lru_w_out': summed['lru_out'][None], 'ffn_w_up': jnp.stack([summed['up0'], summed['up1']]),
             'ffn_w_down': jnp.stack([summed['down0'], summed['down1']])}
    partial = dict(zip(EARLY, _unpack(_sum_slots(got_early, name="sum_partials_early"), list(EARLY.values()))))
    partial.update(zip(LATE, _unpack(_sum_slots(got_late, name="sum_partials_late"), list(LATE.values()))))
    partial['norm_mix'] = jnp.concatenate([partial.pop('norm_mix0'), partial.pop('norm_mix1')], axis=0)
    loss = partial.pop('loss')[0, 0]
    for n, g_full in partial.items():
        if n in SMALL:
            shard, axis = SMALL[n]
            g_full = lax.dynamic_slice_in_dim(g_full, me * shard[axis], shard[axis], axis=axis)
        grads[n] = g_full

    delta, new_m, new_v = {}, {}, {}
    for n in WEIGHTS:
        delta[n], new_m[n], new_v[n] = _adam(w[n], grads[n], given["m_" + n], given["v_" + n], name=f"adamw_{n}")
    return (loss, dx[None], *[grads[n] for n in WEIGHTS], *[delta[n] for n in WEIGHTS],
            *[new_m[n] for n in WEIGHTS], *[new_v[n] for n in WEIGHTS])
```
